```python
import jax, jax.numpy as jnp
from jax import lax
import numpy as np

D_MODEL = 2048
BATCH = 8
SEQ = 8192
DEPTH = 1

D_MIX = D_MODEL
D_GMLP = D_MIX // 2
D_LRU = D_MIX - D_GMLP
CHUNK = 128
GMLP_HEAD_DIM = 128
N_GMLP_HEADS = D_GMLP // GMLP_HEAD_DIM
LRU_BLOCK = 128
N_LRU_BLOCKS = D_LRU // LRU_BLOCK
CONV_WIDTH = 4
LRU_C = 8.0
D_PLE = 256
EPS = 1e-6
D_IN_PROJ = 3 * D_GMLP + 2 * D_LRU

kernel_name = "hymba_gmlp_rglru_sandwich_ple"


def rmsnorm(x, g):
    xf = x.astype(jnp.float32)
    y = xf * lax.rsqrt(jnp.mean(xf * xf, axis=-1, keepdims=True) + EPS)
    return (y * g.astype(jnp.float32)).astype(x.dtype)


def layernorm(x, g, b):
    xf = x.astype(jnp.float32)
    mu = jnp.mean(xf, axis=-1, keepdims=True)
    xc = xf - mu
    y = xc * lax.rsqrt(jnp.mean(xc * xc, axis=-1, keepdims=True) + EPS)
    return (y * g.astype(jnp.float32) + b.astype(jnp.float32)).astype(x.dtype)


def gmlp_branch(u, v, ln_g, ln_b, w_s, b_s):
    bsz, s, _ = v.shape
    u = jax.nn.gelu(u)
    v = layernorm(jax.nn.gelu(v), ln_g, ln_b)
    vc = v.reshape(bsz, s // CHUNK, CHUNK, N_GMLP_HEADS, GMLP_HEAD_DIM)
    causal = jnp.tril(jnp.ones((CHUNK, CHUNK), dtype=bool))
    w = jnp.where(causal[None], w_s, jnp.zeros_like(w_s))
    mixed = jnp.einsum('hts,bcshd->bcthd', w, vc) + jnp.transpose(b_s)[None, None, :, :, None]
    return u * mixed.reshape(bsz, s, D_GMLP)


def _lin_rec_combine(left, right):
    a_l, b_l = left
    a_r, b_r = right
    return a_l * a_r, a_r * b_l + b_r


def rglru_branch(xb, conv_w, conv_b, w_a, b_a, w_x, b_x, lam):
    bsz, s, c = xb.shape
    xc = lax.conv_general_dilated(
        xb, conv_w, window_strides=(1,), padding=[(CONV_WIDTH - 1, 0)],
        dimension_numbers=('NWC', 'WIO', 'NWC'), feature_group_count=c) + conv_b
    xh = xc.reshape(bsz, s, N_LRU_BLOCKS, LRU_BLOCK)
    r = jax.nn.sigmoid(jnp.einsum('bshi,hij->bshj', xh, w_a) + b_a).reshape(bsz, s, c)
    i = jax.nn.sigmoid(jnp.einsum('bshi,hij->bshj', xh, w_x) + b_x).reshape(bsz, s, c)
    log_a = -LRU_C * r.astype(jnp.float32) * jax.nn.softplus(-lam.astype(jnp.float32))
    a = jnp.exp(log_a)
    mult = jnp.sqrt(-jnp.expm1(2.0 * log_a))
    is_first = (jnp.arange(s) == 0)[None, :, None]
    mult = jnp.where(is_first, jnp.ones_like(mult), mult)
    bt = mult * (i * xc).astype(jnp.float32)
    _, h = lax.associative_scan(_lin_rec_combine, (a, bt), axis=1)
    return h.astype(xb.dtype)


def _fwd_setup_inputs(seed: int = 0) -> dict:
    key = jax.random.key(seed)
    ks = jax.random.split(key, 24)
    f32 = jnp.float32
    n = lambda k, shape, scale: jax.random.normal(k, shape, f32) * scale
    gain = lambda k, shape: 1.0 + 0.01 * jax.random.normal(k, shape, f32)
    x = jax.random.normal(ks[0], (BATCH, SEQ, D_MODEL), f32)
    p = jax.random.normal(ks[1], (DEPTH, BATCH, SEQ, D_PLE), f32)
    pre_g = gain(ks[2], (DEPTH, D_MODEL))
    w_in = n(ks[3], (DEPTH, D_MODEL, D_IN_PROJ), D_MODEL ** -0.5)
    gmlp_ln_g = gain(ks[4], (DEPTH, D_GMLP))
    gmlp_ln_b = n(ks[5], (DEPTH, D_GMLP), 0.01)
    gmlp_ws = n(ks[6], (DEPTH, N_GMLP_HEADS, CHUNK, CHUNK), CHUNK ** -0.5)
    gmlp_bs = gain(ks[7], (DEPTH, N_GMLP_HEADS, CHUNK))
    conv_w = n(ks[8], (DEPTH, CONV_WIDTH, 1, D_LRU), CONV_WIDTH ** -0.5)
    conv_b = n(ks[9], (DEPTH, D_LRU), 0.01)
    w_a = n(ks[10], (DEPTH, N_LRU_BLOCKS, LRU_BLOCK, LRU_BLOCK), LRU_BLOCK ** -0.5)
    b_a = n(ks[11], (DEPTH, N_LRU_BLOCKS, LRU_BLOCK), 0.01)
    w_x = n(ks[12], (DEPTH, N_LRU_BLOCKS, LRU_BLOCK, LRU_BLOCK), LRU_BLOCK ** -0.5)
    b_x = n(ks[13], (DEPTH, N_LRU_BLOCKS, LRU_BLOCK), 0.01)
    a0 = jax.random.uniform(ks[14], (DEPTH, D_LRU), f32, 0.9, 0.999)
    s0 = a0 ** (1.0 / LRU_C)
    lam = jnp.log(s0) - jnp.log1p(-s0)
    gmlp_out_g = gain(ks[15], (DEPTH, D_GMLP))
    lru_out_g = gain(ks[16], (DEPTH, D_LRU))
    w_out = n(ks[17], (DEPTH, D_MIX, D_MODEL), D_MIX ** -0.5)
    post_g = gain(ks[18], (DEPTH, D_MODEL))
    w_pe = n(ks[19], (DEPTH, D_PLE, D_MODEL), D_PLE ** -0.5)
    w_pg = n(ks[20], (DEPTH, D_MODEL, D_MODEL), D_MODEL ** -0.5)
    return {"x": x, "p": p, "pre_g": pre_g, "w_in": w_in, "gmlp_ln_g": gmlp_ln_g,
            "gmlp_ln_b": gmlp_ln_b, "gmlp_ws": gmlp_ws, "gmlp_bs": gmlp_bs,
            "conv_w": conv_w, "conv_b": conv_b, "w_a": w_a, "b_a": b_a, "w_x": w_x,
            "b_x": b_x, "lam": lam, "gmlp_out_g": gmlp_out_g, "lru_out_g": lru_out_g,
            "w_out": w_out, "post_g": post_g, "w_pe": w_pe, "w_pg": w_pg}


def _fwd_reference(x, p, pre_g, w_in, gmlp_ln_g, gmlp_ln_b, gmlp_ws, gmlp_bs, conv_w, conv_b,
              w_a, b_a, w_x, b_x, lam, gmlp_out_g, lru_out_g, w_out, post_g, w_pe, w_pg):
    h = x
    splits = [D_GMLP, 2 * D_GMLP, 3 * D_GMLP, 3 * D_GMLP + D_LRU]
    for l in range(DEPTH):
        hn = rmsnorm(h, pre_g[l])
        z = hn @ w_in[l]
        u, v, gate_a, xb, gate_b = jnp.split(z, splits, axis=-1)
        ya = gmlp_branch(u, v, gmlp_ln_g[l], gmlp_ln_b[l], gmlp_ws[l], gmlp_bs[l]) * jax.nn.silu(gate_a)
        yb = rglru_branch(xb, conv_w[l], conv_b[l], w_a[l], b_a[l], w_x[l], b_x[l], lam[l]) * jax.nn.silu(gate_b)
        y = jnp.concatenate([rmsnorm(ya, gmlp_out_g[l]), rmsnorm(yb, lru_out_g[l])], axis=-1)
        h = h + rmsnorm(y @ w_out[l], post_g[l])
        h = h + (p[l] @ w_pe[l]) * jax.nn.sigmoid(h @ w_pg[l])
    return h


import jax as _jax
import jax.numpy as _jnp

TWIN_FORMAT = 'train_step'
FWD_PARAMS = ['x', 'p', 'pre_g', 'w_in', 'gmlp_ln_g', 'gmlp_ln_b', 'gmlp_ws', 'gmlp_bs', 'conv_w', 'conv_b', 'w_a', 'b_a', 'w_x', 'b_x', 'lam', 'gmlp_out_g', 'lru_out_g', 'w_out', 'post_g', 'w_pe', 'w_pg']
TWIN_WEIGHTS = ['pre_g', 'w_in', 'gmlp_ln_g', 'gmlp_ln_b', 'gmlp_ws', 'gmlp_bs', 'conv_w', 'conv_b', 'w_a', 'b_a', 'w_x', 'b_x', 'lam', 'gmlp_out_g', 'lru_out_g', 'w_out', 'post_g', 'w_pe', 'w_pg']
TWIN_DIFF_INPUT = 'x'
TWIN_INPUTS = ['x', 'p', 'pre_g', 'w_in', 'gmlp_ln_g', 'gmlp_ln_b', 'gmlp_ws', 'gmlp_bs', 'conv_w', 'conv_b', 'w_a', 'b_a', 'w_x', 'b_x', 'lam', 'gmlp_out_g', 'lru_out_g', 'w_out', 'post_g', 'w_pe', 'w_pg', 'loss_target', 'm_pre_g', 'm_w_in', 'm_gmlp_ln_g', 'm_gmlp_ln_b', 'm_gmlp_ws', 'm_gmlp_bs', 'm_conv_w', 'm_conv_b', 'm_w_a', 'm_b_a', 'm_w_x', 'm_b_x', 'm_lam', 'm_gmlp_out_g', 'm_lru_out_g', 'm_w_out', 'm_post_g', 'm_w_pe', 'm_w_pg', 'v_pre_g', 'v_w_in', 'v_gmlp_ln_g', 'v_gmlp_ln_b', 'v_gmlp_ws', 'v_gmlp_bs', 'v_conv_w', 'v_conv_b', 'v_w_a', 'v_b_a', 'v_w_x', 'v_b_x', 'v_lam', 'v_gmlp_out_g', 'v_lru_out_g', 'v_w_out', 'v_post_g', 'v_w_pe', 'v_w_pg']
TWIN_OUTPUTS = ['loss', 'grad_x', 'grad_pre_g', 'grad_w_in', 'grad_gmlp_ln_g', 'grad_gmlp_ln_b', 'grad_gmlp_ws', 'grad_gmlp_bs', 'grad_conv_w', 'grad_conv_b', 'grad_w_a', 'grad_b_a', 'grad_w_x', 'grad_b_x', 'grad_lam', 'grad_gmlp_out_g', 'grad_lru_out_g', 'grad_w_out', 'grad_post_g', 'grad_w_pe', 'grad_w_pg', 'delta_pre_g', 'delta_w_in', 'delta_gmlp_ln_g', 'delta_gmlp_ln_b', 'delta_gmlp_ws', 'delta_gmlp_bs', 'delta_conv_w', 'delta_conv_b', 'delta_w_a', 'delta_b_a', 'delta_w_x', 'delta_b_x', 'delta_lam', 'delta_gmlp_out_g', 'delta_lru_out_g', 'delta_w_out', 'delta_post_g', 'delta_w_pe', 'delta_w_pg', 'new_m_pre_g', 'new_m_w_in', 'new_m_gmlp_ln_g', 'new_m_gmlp_ln_b', 'new_m_gmlp_ws', 'new_m_gmlp_bs', 'new_m_conv_w', 'new_m_conv_b', 'new_m_w_a', 'new_m_b_a', 'new_m_w_x', 'new_m_b_x', 'new_m_lam', 'new_m_gmlp_out_g', 'new_m_lru_out_g', 'new_m_w_out', 'new_m_post_g', 'new_m_w_pe', 'new_m_w_pg', 'new_v_pre_g', 'new_v_w_in', 'new_v_gmlp_ln_g', 'new_v_gmlp_ln_b', 'new_v_gmlp_ws', 'new_v_gmlp_bs', 'new_v_conv_w', 'new_v_conv_b', 'new_v_w_a', 'new_v_b_a', 'new_v_w_x', 'new_v_b_x', 'new_v_lam', 'new_v_gmlp_out_g', 'new_v_lru_out_g', 'new_v_w_out', 'new_v_post_g', 'new_v_w_pe', 'new_v_w_pg']
TWIN_LEAF_KINDS = {'loss': 'loss', 'grad_x': 'grad_x', 'grad_pre_g': 'grad_w', 'grad_w_in': 'grad_w', 'grad_gmlp_ln_g': 'grad_w', 'grad_gmlp_ln_b': 'grad_w', 'grad_gmlp_ws': 'grad_w', 'grad_gmlp_bs': 'grad_w', 'grad_conv_w': 'grad_w', 'grad_conv_b': 'grad_w', 'grad_w_a': 'grad_w', 'grad_b_a': 'grad_w', 'grad_w_x': 'grad_w', 'grad_b_x': 'grad_w', 'grad_lam': 'grad_w', 'grad_gmlp_out_g': 'grad_w', 'grad_lru_out_g': 'grad_w', 'grad_w_out': 'grad_w', 'grad_post_g': 'grad_w', 'grad_w_pe': 'grad_w', 'grad_w_pg': 'grad_w', 'delta_pre_g': 'delta_w', 'delta_w_in': 'delta_w', 'delta_gmlp_ln_g': 'delta_w', 'delta_gmlp_ln_b': 'delta_w', 'delta_gmlp_ws': 'delta_w', 'delta_gmlp_bs': 'delta_w', 'delta_conv_w': 'delta_w', 'delta_conv_b': 'delta_w', 'delta_w_a': 'delta_w', 'delta_b_a': 'delta_w', 'delta_w_x': 'delta_w', 'delta_b_x': 'delta_w', 'delta_lam': 'delta_w', 'delta_gmlp_out_g': 'delta_w', 'delta_lru_out_g': 'delta_w', 'delta_w_out': 'delta_w', 'delta_post_g': 'delta_w', 'delta_w_pe': 'delta_w', 'delta_w_pg': 'delta_w', 'new_m_pre_g': 'new_m', 'new_m_w_in': 'new_m', 'new_m_gmlp_ln_g': 'new_m', 'new_m_gmlp_ln_b': 'new_m', 'new_m_gmlp_ws': 'new_m', 'new_m_gmlp_bs': 'new_m', 'new_m_conv_w': 'new_m', 'new_m_conv_b': 'new_m', 'new_m_w_a': 'new_m', 'new_m_b_a': 'new_m', 'new_m_w_x': 'new_m', 'new_m_b_x': 'new_m', 'new_m_lam': 'new_m', 'new_m_gmlp_out_g': 'new_m', 'new_m_lru_out_g': 'new_m', 'new_m_w_out': 'new_m', 'new_m_post_g': 'new_m', 'new_m_w_pe': 'new_m', 'new_m_w_pg': 'new_m', 'new_v_pre_g': 'new_v', 'new_v_w_in': 'new_v', 'new_v_gmlp_ln_g': 'new_v', 'new_v_gmlp_ln_b': 'new_v', 'new_v_gmlp_ws': 'new_v', 'new_v_gmlp_bs': 'new_v', 'new_v_conv_w': 'new_v', 'new_v_conv_b': 'new_v', 'new_v_w_a': 'new_v', 'new_v_b_a': 'new_v', 'new_v_w_x': 'new_v', 'new_v_b_x': 'new_v', 'new_v_lam': 'new_v', 'new_v_gmlp_out_g': 'new_v', 'new_v_lru_out_g': 'new_v', 'new_v_w_out': 'new_v', 'new_v_post_g': 'new_v', 'new_v_w_pe': 'new_v', 'new_v_w_pg': 'new_v'}


def _forward(args):
    return _fwd_reference(*[args[k] for k in FWD_PARAMS])


def _output_shape():
    def fwd():
        inp = _fwd_setup_inputs(0)
        return _fwd_reference(*[inp[k] for k in FWD_PARAMS])
    out = _jax.eval_shape(fwd)
    return out.shape, out.dtype

N_MICROBATCH = 1
ADAM_LR = 0.001
ADAM_B1 = 0.9
ADAM_B2 = 0.999
ADAM_EPS = 1e-08
ADAM_WD = 0.01
ADAM_STEP = 10
PER_EXAMPLE_BATCH_AXIS = {'x': 0, 'p': 1, 'loss_target': 0}
SHARED_INPUTS = []
_WEIGHT_DTYPES = {'pre_g': _jnp.float32, 'w_in': _jnp.float32, 'gmlp_ln_g': _jnp.float32, 'gmlp_ln_b': _jnp.float32, 'gmlp_ws': _jnp.float32, 'gmlp_bs': _jnp.float32, 'conv_w': _jnp.float32, 'conv_b': _jnp.float32, 'w_a': _jnp.float32, 'b_a': _jnp.float32, 'w_x': _jnp.float32, 'b_x': _jnp.float32, 'lam': _jnp.float32, 'gmlp_out_g': _jnp.float32, 'lru_out_g': _jnp.float32, 'w_out': _jnp.float32, 'post_g': _jnp.float32, 'w_pe': _jnp.float32, 'w_pg': _jnp.float32}
MOMENT_SCALE = {'pre_g': 3.322207e-01, 'w_in': 2.122017e-01, 'gmlp_ln_g': 1.135307e-01, 'gmlp_ln_b': 1.166643e-01, 'gmlp_ws': 1.108349e-01, 'gmlp_bs': 1.571889e-01, 'conv_w': 3.262525e-01, 'conv_b': 1.018181e+01, 'w_a': 2.003448e-01, 'b_a': 1.304140e-01, 'w_x': 3.751105e-01, 'b_x': 1.010507e-01, 'lam': 1.836721e-01, 'gmlp_out_g': 3.326269e-01, 'lru_out_g': 3.977058e-01, 'w_out': 3.584677e-01, 'post_g': 3.266014e+01, 'w_pe': 4.844279e-01, 'w_pg': 3.532164e-01}


def _to_microbatches(a, axis):
    t = _jnp.moveaxis(a, axis, 0)
    t = t.reshape((N_MICROBATCH, t.shape[0] // N_MICROBATCH) + t.shape[1:])
    return _jnp.moveaxis(t, 1, axis + 1)


def setup_inputs(seed: int = 0) -> dict:
    inp = _fwd_setup_inputs(seed)
    key = _jax.random.fold_in(_jax.random.key(seed), 7919)
    shape, _ = _output_shape()
    out = dict(inp)
    out["loss_target"] = _jax.random.normal(_jax.random.fold_in(key, 0), shape, _jnp.float32)
    for i, name in enumerate(TWIN_WEIGHTS):
        w = inp[name].astype(_jnp.float32)
        if MOMENT_SCALE is None:
            s = _jnp.sqrt(_jnp.mean(_jnp.square(w)) + 1e-30)
        else:
            s = MOMENT_SCALE[name]
        km, kv = _jax.random.split(_jax.random.fold_in(key, i + 1))
        out[name] = w
        out["m_" + name] = s * _jax.random.normal(km, w.shape, _jnp.float32)
        out["v_" + name] = (s * s) * _jax.random.uniform(kv, w.shape, _jnp.float32, 0.5, 1.5)
    if N_MICROBATCH > 1:
        for name, axis in PER_EXAMPLE_BATCH_AXIS.items():
            out[name] = _to_microbatches(out[name], axis)
    return {'x': out['x'], 'p': out['p'], 'pre_g': out['pre_g'], 'w_in': out['w_in'], 'gmlp_ln_g': out['gmlp_ln_g'], 'gmlp_ln_b': out['gmlp_ln_b'], 'gmlp_ws': out['gmlp_ws'], 'gmlp_bs': out['gmlp_bs'], 'conv_w': out['conv_w'], 'conv_b': out['conv_b'], 'w_a': out['w_a'], 'b_a': out['b_a'], 'w_x': out['w_x'], 'b_x': out['b_x'], 'lam': out['lam'], 'gmlp_out_g': out['gmlp_out_g'], 'lru_out_g': out['lru_out_g'], 'w_out': out['w_out'], 'post_g': out['post_g'], 'w_pe': out['w_pe'], 'w_pg': out['w_pg'], 'loss_target': out['loss_target'], 'm_pre_g': out['m_pre_g'], 'm_w_in': out['m_w_in'], 'm_gmlp_ln_g': out['m_gmlp_ln_g'], 'm_gmlp_ln_b': out['m_gmlp_ln_b'], 'm_gmlp_ws': out['m_gmlp_ws'], 'm_gmlp_bs': out['m_gmlp_bs'], 'm_conv_w': out['m_conv_w'], 'm_conv_b': out['m_conv_b'], 'm_w_a': out['m_w_a'], 'm_b_a': out['m_b_a'], 'm_w_x': out['m_w_x'], 'm_b_x': out['m_b_x'], 'm_lam': out['m_lam'], 'm_gmlp_out_g': out['m_gmlp_out_g'], 'm_lru_out_g': out['m_lru_out_g'], 'm_w_out': out['m_w_out'], 'm_post_g': out['m_post_g'], 'm_w_pe': out['m_w_pe'], 'm_w_pg': out['m_w_pg'], 'v_pre_g': out['v_pre_g'], 'v_w_in': out['v_w_in'], 'v_gmlp_ln_g': out['v_gmlp_ln_g'], 'v_gmlp_ln_b': out['v_gmlp_ln_b'], 'v_gmlp_ws': out['v_gmlp_ws'], 'v_gmlp_bs': out['v_gmlp_bs'], 'v_conv_w': out['v_conv_w'], 'v_conv_b': out['v_conv_b'], 'v_w_a': out['v_w_a'], 'v_b_a': out['v_b_a'], 'v_w_x': out['v_w_x'], 'v_b_x': out['v_b_x'], 'v_lam': out['v_lam'], 'v_gmlp_out_g': out['v_gmlp_out_g'], 'v_lru_out_g': out['v_lru_out_g'], 'v_w_out': out['v_w_out'], 'v_post_g': out['v_post_g'], 'v_w_pe': out['v_w_pe'], 'v_w_pg': out['v_w_pg']}


def _loss(weights, diff, rest, loss_target):
    with _jax.named_scope("forward"):
        args = {**rest, TWIN_DIFF_INPUT: diff, **{k: w.astype(_WEIGHT_DTYPES[k]) for k, w in weights.items()}}
        y = _forward(args)
    with _jax.named_scope("loss_head"):
        err = _jnp.square(y.astype(_jnp.float32) - loss_target)
        return 0.5 * _jnp.sum(_jnp.mean(err, axis=-1)) if err.ndim else 0.5 * err


def _adamw(w, g, m, v):
    m = ADAM_B1 * m + (1.0 - ADAM_B1) * g
    v = ADAM_B2 * v + (1.0 - ADAM_B2) * _jnp.square(g)
    m_hat = m / (1.0 - ADAM_B1 ** ADAM_STEP)
    v_hat = v / (1.0 - ADAM_B2 ** ADAM_STEP)
    delta = -ADAM_LR * (m_hat / (_jnp.sqrt(v_hat) + ADAM_EPS) + ADAM_WD * w)
    return delta, m, v


def reference(x, p, pre_g, w_in, gmlp_ln_g, gmlp_ln_b, gmlp_ws, gmlp_bs, conv_w, conv_b, w_a, b_a, w_x, b_x, lam, gmlp_out_g, lru_out_g, w_out, post_g, w_pe, w_pg, loss_target, m_pre_g, m_w_in, m_gmlp_ln_g, m_gmlp_ln_b, m_gmlp_ws, m_gmlp_bs, m_conv_w, m_conv_b, m_w_a, m_b_a, m_w_x, m_b_x, m_lam, m_gmlp_out_g, m_lru_out_g, m_w_out, m_post_g, m_w_pe, m_w_pg, v_pre_g, v_w_in, v_gmlp_ln_g, v_gmlp_ln_b, v_gmlp_ws, v_gmlp_bs, v_conv_w, v_conv_b, v_w_a, v_b_a, v_w_x, v_b_x, v_lam, v_gmlp_out_g, v_lru_out_g, v_w_out, v_post_g, v_w_pe, v_w_pg):
    given = dict(x=x, p=p, pre_g=pre_g, w_in=w_in, gmlp_ln_g=gmlp_ln_g, gmlp_ln_b=gmlp_ln_b, gmlp_ws=gmlp_ws, gmlp_bs=gmlp_bs, conv_w=conv_w, conv_b=conv_b, w_a=w_a, b_a=b_a, w_x=w_x, b_x=b_x, lam=lam, gmlp_out_g=gmlp_out_g, lru_out_g=lru_out_g, w_out=w_out, post_g=post_g, w_pe=w_pe, w_pg=w_pg, loss_target=loss_target, m_pre_g=m_pre_g, m_w_in=m_w_in, m_gmlp_ln_g=m_gmlp_ln_g, m_gmlp_ln_b=m_gmlp_ln_b, m_gmlp_ws=m_gmlp_ws, m_gmlp_bs=m_gmlp_bs, m_conv_w=m_conv_w, m_conv_b=m_conv_b, m_w_a=m_w_a, m_b_a=m_b_a, m_w_x=m_w_x, m_b_x=m_b_x, m_lam=m_lam, m_gmlp_out_g=m_gmlp_out_g, m_lru_out_g=m_lru_out_g, m_w_out=m_w_out, m_post_g=m_post_g, m_w_pe=m_w_pe, m_w_pg=m_w_pg, v_pre_g=v_pre_g, v_w_in=v_w_in, v_gmlp_ln_g=v_gmlp_ln_g, v_gmlp_ln_b=v_gmlp_ln_b, v_gmlp_ws=v_gmlp_ws, v_gmlp_bs=v_gmlp_bs, v_conv_w=v_conv_w, v_conv_b=v_conv_b, v_w_a=v_w_a, v_b_a=v_b_a, v_w_x=v_w_x, v_b_x=v_b_x, v_lam=v_lam, v_gmlp_out_g=v_gmlp_out_g, v_lru_out_g=v_lru_out_g, v_w_out=v_w_out, v_post_g=v_post_g, v_w_pe=v_w_pe, v_w_pg=v_w_pg)
    weights = {n: given[n] for n in TWIN_WEIGHTS}
    shared = {n: given[n] for n in SHARED_INPUTS}
    per_example = {n: given[n] for n in ['x', 'p']}
    grad_fn = _jax.value_and_grad(_loss, argnums=(0, 1))

    def one_microbatch(ex, loss_target):
        ex = dict(ex)
        diff = ex.pop(TWIN_DIFF_INPUT)
        return grad_fn(weights, diff, {**shared, **ex}, loss_target)

    if N_MICROBATCH == 1:
        loss, (grad_w, grad_x) = one_microbatch(per_example, given["loss_target"])
    else:
        def body(carry, xs):
            loss_sum, grad_sum = carry
            l_k, (gw_k, gx_k) = one_microbatch(xs[0], xs[1])
            with _jax.named_scope("update"):
                return (loss_sum + l_k, _jax.tree.map(_jnp.add, grad_sum, gw_k)), gx_k

        init = (_jnp.zeros((), _jnp.float32), _jax.tree.map(_jnp.zeros_like, weights))
        (loss, grad_w), grad_x = _jax.lax.scan(body, init, (per_example, given["loss_target"]))
    with _jax.named_scope("update"):
        delta_w, new_m, new_v = {}, {}, {}
        for n in TWIN_WEIGHTS:
            delta_w[n], new_m[n], new_v[n] = _adamw(weights[n], grad_w[n], given["m_" + n], given["v_" + n])
    return (loss, grad_x, *[grad_w[n] for n in TWIN_WEIGHTS], *[delta_w[n] for n in TWIN_WEIGHTS],
            *[new_m[n] for n in TWIN_WEIGHTS], *[new_v[n] for n in TWIN_WEIGHTS])
```

```python
import functools
import math

import jax
import jax.numpy as jnp
from jax import lax
from jax.experimental import pallas as pl
from jax.experimental.pallas import tpu as pltpu

F32 = jnp.float32
BF16 = jnp.bfloat16

D_MODEL = 2048
D_HALF = 1024
D_Z = 5120
D_PLE = 256
CHUNK = 128
N_HEADS = 8
N_CHIPS = 4
W_IN_COLS = D_Z // N_CHIPS
W_ROWS = D_MODEL // N_CHIPS
W_PE_COLS = D_MODEL // N_CHIPS
CONV_W = 4
CONV_COLS = D_HALF // N_CHIPS
EPS = 1e-6
LRU_C = 8.0
ADAM_LR, ADAM_B1, ADAM_B2, ADAM_EPS, ADAM_WD, ADAM_STEP = 0.001, 0.9, 0.999, 1e-08, 0.01, 10

SUBLANES = 8
LANES = 128
VMEM_LIMIT = 56 * 1024 * 1024

SMALL_ROWS = (("pre_g", 16), ("gmlp_ln_g", 8), ("gmlp_ln_b", 8), ("gmlp_ws", 1024), ("gmlp_bs", 8),
              ("conv_w", 32), ("conv_b", 8), ("w_a", 1024), ("b_a", 8), ("w_x", 1024), ("b_x", 8),
              ("lam", 8), ("gmlp_out_g", 8), ("lru_out_g", 8), ("post_g", 16))
SMALL_USED = sum(r for _, r in SMALL_ROWS)
SMALL_PIECE = 408
SMALL_TOTAL = 8 * SMALL_PIECE

MESH = pl.DeviceIdType.MESH
ANY = pl.BlockSpec(memory_space=pl.ANY)

_GELU_C0 = math.sqrt(2.0 / math.pi)
_GELU_C1 = 0.044715


def _params(*sem):
    return pltpu.CompilerParams(dimension_semantics=sem, vmem_limit_bytes=VMEM_LIMIT)


def _dot(a, b):
    return jnp.dot(a, b, preferred_element_type=F32)


def _dot_nt(a, b):
    return lax.dot_general(a, b, (((1,), (1,)), ((), ())), preferred_element_type=F32)


def _dot_tn(a, b):
    return lax.dot_general(a, b, (((0,), (0,)), ((), ())), preferred_element_type=F32)


def _gelu(x):
    t = jnp.tanh(_GELU_C0 * (x + _GELU_C1 * (x * x * x)))
    return 0.5 * x * (1.0 + t), t


def _gelu_grad(x, t):
    return 0.5 * (1.0 + t) + 0.5 * x * (1.0 - t * t) * (_GELU_C0 * (1.0 + 3.0 * _GELU_C1 * x * x))


def _rowsum8(v):
    r, n = v.shape
    return jnp.sum(v.reshape(r // SUBLANES, SUBLANES, n), axis=0)


def _lanemean(v):
    return jnp.mean(v, axis=-1, keepdims=True)


def _shift_down(v, halo8, k):
    if k == 0:
        return v
    r = pltpu.roll(v, k, 0)
    hr = pltpu.roll(halo8, k, 0)
    row = lax.broadcasted_iota(jnp.int32, halo8.shape, 0)
    top = jnp.where(row < k, hr, r[0:SUBLANES])
    return jnp.concatenate([top, r[SUBLANES:]], axis=0)


def _shift_up(v, next8, k):
    if k == 0:
        return v
    n = v.shape[0]
    r = pltpu.roll(v, n - k, 0)
    nr = pltpu.roll(next8, SUBLANES - k, 0)
    row = lax.broadcasted_iota(jnp.int32, next8.shape, 0)
    bot = jnp.where(row >= SUBLANES - k, nr, r[n - SUBLANES:])
    return jnp.concatenate([r[:n - SUBLANES], bot], axis=0)


def _inproj_fwd(x, pre_g, wg_in, tm):
    t = x.shape[0]

    def body(x_ref, g_ref, w_ref, z_ref, hn_ref):
        @pl.when(pl.program_id(1) == 0)
        def _():
            xv = x_ref[...]
            r = lax.rsqrt(_lanemean(xv * xv) + EPS)
            hn_ref[...] = (xv * r * g_ref[...]).astype(BF16)

        z_ref[...] = _dot(hn_ref[...], w_ref[...])

    return pl.pallas_call(
        body, name="inproj_fwd", grid=(t // tm, N_CHIPS),
        in_specs=[pl.BlockSpec((tm, D_MODEL), lambda i, j: (i, 0)),
                  pl.BlockSpec((1, D_MODEL), lambda i, j: (0, 0)),
                  pl.BlockSpec((None, D_MODEL, W_IN_COLS), lambda i, j: (j, 0, 0))],
        out_specs=[pl.BlockSpec((tm, W_IN_COLS), lambda i, j: (i, j)),
                   pl.BlockSpec((tm, D_MODEL), lambda i, j: (i, 0))],
        out_shape=[jax.ShapeDtypeStruct((t, D_Z), F32), jax.ShapeDtypeStruct((t, D_MODEL), BF16)],
        compiler_params=_params("parallel", "arbitrary"),
    )(x, pre_g, wg_in)


def _layernorm_parts(vg):
    mu = _lanemean(vg)
    xc = vg - mu
    rstd = lax.rsqrt(_lanemean(xc * xc) + EPS)
    return xc * rstd, rstd


def _spatial_mix(wt_ref, vn_ref, bsx_ref, mixed_ref, tm):
    for c in range(tm // CHUNK):
        rows = slice(c * CHUNK, (c + 1) * CHUNK)
        for h in range(N_HEADS):
            cols = slice(h * CHUNK, (h + 1) * CHUNK)
            mixed_ref[rows, cols] = _dot(wt_ref[h], vn_ref[rows, cols]) + bsx_ref[:, cols]


def _conv_taps(xb, halo8):
    return [_shift_down(xb, halo8, CONV_W - 1 - k) for k in range(CONV_W)]


def _lru_gates(xc_bf_ref, wa_ref, wx_ref, ba_ref, bx_ref, r_ref, i_ref):
    for h in range(N_HEADS):
        cols = slice(h * CHUNK, (h + 1) * CHUNK)
        xh = xc_bf_ref[:, cols]
        r_ref[:, cols] = jax.nn.sigmoid(_dot(xh, wa_ref[h]) + ba_ref[:, cols])
        i_ref[:, cols] = jax.nn.sigmoid(_dot(xh, wx_ref[h]) + bx_ref[:, cols])


def _softplus_neg(lam):
    return jnp.maximum(-lam, 0.0) + jnp.log(1.0 + jnp.exp(-jnp.abs(lam)))


def _decay_parts(r, lam):
    la = (-LRU_C * _softplus_neg(lam)) * r
    a = jnp.exp(la)
    th = -jnp.tanh(la)
    mult = jnp.sqrt(2.0 * th / (1.0 + th))
    return a, mult


def _branches_fwd(z, prm, tm):
    t = z.shape[0]
    nt = t // tm
    hb = tm // SUBLANES

    def body(u_ref, v_ref, ga_ref, xb_ref, gb_ref, xbh_ref,
             lng_ref, lnb_ref, wt_ref, bsx_ref, cw_ref, cb_ref, wa_ref, wx_ref, ba_ref, bx_ref, lam_ref,
             oga_ref, ogb_ref,
             y_ref, h_ref,
             vn_s, mixed_s, xcbf_s, r_s, i_s, a_s, b_s, carry_s):
        i = pl.program_id(0)

        @pl.when(i == 0)
        def _():
            carry_s[...] = jnp.zeros_like(carry_s)

        ug, _ = _gelu(u_ref[...])
        vg, _ = _gelu(v_ref[...])
        vhat, _ = _layernorm_parts(vg)
        vn_s[...] = (vhat * lng_ref[...] + lnb_ref[...]).astype(BF16)
        _spatial_mix(wt_ref, vn_s, bsx_ref, mixed_s, tm)
        ga = ga_ref[...]
        ya = ug * mixed_s[...] * (ga * jax.nn.sigmoid(ga))
        ra = lax.rsqrt(_lanemean(ya * ya) + EPS)
        y_ref[:, 0:D_HALF] = (ya * ra * oga_ref[...]).astype(BF16)

        xb = xb_ref[...]
        halo = jnp.where(i == 0, 0.0, xbh_ref[...])
        taps = _conv_taps(xb, halo)
        xc = cb_ref[...] + taps[0] * cw_ref[0:1, :]
        for k in range(1, CONV_W):
            xc = xc + taps[k] * cw_ref[k:k + 1, :]
        xcbf_s[...] = xc.astype(BF16)
        _lru_gates(xcbf_s, wa_ref, wx_ref, ba_ref, bx_ref, r_s, i_s)
        a, mult = _decay_parts(r_s[...], lam_ref[...])
        row = lax.broadcasted_iota(jnp.int32, a.shape, 0)
        mult = jnp.where(jnp.logical_and(i == 0, row == 0), 1.0, mult)
        b = mult * (i_s[...] * xc)
        r8 = row & (SUBLANES - 1)
        for d in (1, 2, 4):
            a_sh = pltpu.roll(a, d, 0)
            b_sh = pltpu.roll(b, d, 0)
            m = r8 >= d
            b = jnp.where(m, a * b_sh + b, b)
            a = jnp.where(m, a * a_sh, a)
        a_s[...] = a
        b_s[...] = b

        def step(g, carry):
            sl = pl.ds(pl.multiple_of(g * SUBLANES, SUBLANES), SUBLANES)
            hg = a_s[sl, :] * carry + b_s[sl, :]
            h_ref[sl, :] = hg
            return jnp.broadcast_to(hg[SUBLANES - 1:SUBLANES, :], hg.shape)

        carry_s[...] = lax.fori_loop(0, hb, step, carry_s[...])
        gb = gb_ref[...]
        yb = h_ref[...] * (gb * jax.nn.sigmoid(gb))
        rb = lax.rsqrt(_lanemean(yb * yb) + EPS)
        y_ref[:, D_HALF:] = (yb * rb * ogb_ref[...]).astype(BF16)

    zspec = lambda g: pl.BlockSpec((tm, D_HALF), lambda i, g=g: (i, g))
    full = lambda a: pl.BlockSpec(a.shape, lambda i, n=a.ndim: (0,) * n)
    names = ("ln_g", "ln_b", "wt", "bsx", "conv_w", "conv_b", "w_a", "w_x", "b_a", "b_x", "lam", "oga", "ogb")
    pr = [prm[n] for n in names]
    return pl.pallas_call(
        body, name="branches_fwd", grid=(nt,),
        in_specs=[zspec(0), zspec(1), zspec(2), zspec(3), zspec(4),
                  pl.BlockSpec((SUBLANES, D_HALF), lambda i: (jnp.maximum(i * hb - 1, 0), 3))]
                 + [full(a) for a in pr],
        out_specs=[pl.BlockSpec((tm, D_MODEL), lambda i: (i, 0)), pl.BlockSpec((tm, D_HALF), lambda i: (i, 0))],
        out_shape=[jax.ShapeDtypeStruct((t, D_MODEL), BF16), jax.ShapeDtypeStruct((t, D_HALF), F32)],
        scratch_shapes=[pltpu.VMEM((tm, D_HALF), BF16), pltpu.VMEM((tm, D_HALF), F32),
                        pltpu.VMEM((tm, D_HALF), BF16), pltpu.VMEM((tm, D_HALF), F32),
                        pltpu.VMEM((tm, D_HALF), F32), pltpu.VMEM((tm, D_HALF), F32),
                        pltpu.VMEM((tm, D_HALF), F32), pltpu.VMEM((SUBLANES, D_HALF), F32)],
        compiler_params=_params("arbitrary"),
    )(z, z, z, z, z, z, *pr)


def _outproj_fwd(x, y, p, tgt, post_g, w_out, w_pg, wg_pe, tm):
    t = x.shape[0]

    def body(x_ref, y_ref, p_ref, tgt_ref, pg_ref, wo_ref, wpg_ref, wpe_ref,
             o_ref, h1_ref, gt_ref, dout_ref, loss_ref):
        @pl.when(pl.program_id(0) == 0)
        def _():
            loss_ref[...] = jnp.zeros_like(loss_ref)

        o = _dot(y_ref[...], wo_ref[...])
        o_ref[...] = o
        r3 = lax.rsqrt(_lanemean(o * o) + EPS)
        h1 = x_ref[...] + (o * r3) * pg_ref[...]
        h1b = h1.astype(BF16)
        h1_ref[...] = h1b
        gt = jax.nn.sigmoid(_dot(h1b, wpg_ref[...]))
        gt_ref[...] = gt
        pb = p_ref[...].astype(BF16)
        for k in range(N_CHIPS):
            cols = slice(k * W_PE_COLS, (k + 1) * W_PE_COLS)
            pe = _dot(pb, wpe_ref[k])
            d = h1[:, cols] + pe * gt[:, cols] - tgt_ref[:, cols]
            dout_ref[:, cols] = d * (1.0 / D_MODEL)
            loss_ref[...] += jnp.sum(d * d) * (0.5 / D_MODEL)

    row = lambda n: pl.BlockSpec((tm, n), lambda i: (i, 0))
    const = lambda shp: pl.BlockSpec(shp, lambda i, n=len(shp): (0,) * n, pipeline_mode=pl.Buffered(1))
    return pl.pallas_call(
        body, name="outproj_fwd", grid=(t // tm,),
        in_specs=[row(D_MODEL), row(D_MODEL), row(D_PLE), row(D_MODEL), const((1, D_MODEL)),
                  const((D_MODEL, D_MODEL)), const((D_MODEL, D_MODEL)), const((N_CHIPS, D_PLE, W_PE_COLS))],
        out_specs=[row(D_MODEL), row(D_MODEL), row(D_MODEL), row(D_MODEL),
                   pl.BlockSpec((SUBLANES, LANES), lambda i: (0, 0))],
        out_shape=[jax.ShapeDtypeStruct((t, D_MODEL), F32), jax.ShapeDtypeStruct((t, D_MODEL), BF16),
                   jax.ShapeDtypeStruct((t, D_MODEL), F32), jax.ShapeDtypeStruct((t, D_MODEL), F32),
                   jax.ShapeDtypeStruct((SUBLANES, LANES), F32)],
        compiler_params=_params("arbitrary"),
    )(x, y, p, tgt, post_g, w_out, w_pg, wg_pe)


def _head_bwd(dout, gt, p, o, post_g, w_out, w_pg, wg_pe, tm):
    t = dout.shape[0]

    def body(dout_ref, gt_ref, p_ref, o_ref, pg_ref, wo_ref, wpg_ref, wpe_ref,
             dpe_ref, dq_ref, dh1_ref, do_ref, dy_ref, gpost_ref):
        i = pl.program_id(0)

        @pl.when(i == 0)
        def _():
            gpost_ref[...] = jnp.zeros_like(gpost_ref)

        dout = dout_ref[...]
        gt = gt_ref[...]
        dpe_ref[...] = (dout * gt).astype(BF16)
        pb = p_ref[...].astype(BF16)
        for k in range(N_CHIPS):
            cols = slice(k * W_PE_COLS, (k + 1) * W_PE_COLS)
            pe = _dot(pb, wpe_ref[k])
            g = gt[:, cols]
            dq_ref[:, cols] = (dout[:, cols] * pe * g * (1.0 - g)).astype(BF16)
        dh1 = dout + _dot_nt(dq_ref[...], wpg_ref[...])
        dh1_ref[...] = dh1
        o = o_ref[...]
        r3 = lax.rsqrt(_lanemean(o * o) + EPS)
        on = o * r3
        gpost_ref[...] += _rowsum8(dh1 * on)
        don = dh1 * pg_ref[...]
        do = r3 * (don - on * _lanemean(don * on))
        dob = do.astype(BF16)
        do_ref[...] = dob
        dy_ref[...] = _dot_nt(dob, wo_ref[...])

        @pl.when(i == pl.num_programs(0) - 1)
        def _():
            gpost_ref[...] = jnp.broadcast_to(jnp.sum(gpost_ref[...], axis=0, keepdims=True), gpost_ref.shape)

    row = lambda n: pl.BlockSpec((tm, n), lambda i: (i, 0))
    const = lambda shp: pl.BlockSpec(shp, lambda i, n=len(shp): (0,) * n, pipeline_mode=pl.Buffered(1))
    return pl.pallas_call(
        body, name="head_bwd", grid=(t // tm,),
        in_specs=[row(D_MODEL), row(D_MODEL), row(D_PLE), row(D_MODEL), const((1, D_MODEL)),
                  const((D_MODEL, D_MODEL)), const((D_MODEL, D_MODEL)), const((N_CHIPS, D_PLE, W_PE_COLS))],
        out_specs=[row(D_MODEL), row(D_MODEL), row(D_MODEL), row(D_MODEL), row(D_MODEL),
                   pl.BlockSpec((SUBLANES, D_MODEL), lambda i: (0, 0))],
        out_shape=[jax.ShapeDtypeStruct((t, D_MODEL), BF16), jax.ShapeDtypeStruct((t, D_MODEL), BF16),
                   jax.ShapeDtypeStruct((t, D_MODEL), F32), jax.ShapeDtypeStruct((t, D_MODEL), BF16),
                   jax.ShapeDtypeStruct((t, D_MODEL), F32), jax.ShapeDtypeStruct((SUBLANES, D_MODEL), F32)],
        compiler_params=_params("arbitrary"),
    )(dout, gt, p, o, post_g, w_out, w_pg, wg_pe)


def _branches_bwd(z, h, dy, prm, tm):
    t = z.shape[0]
    nt = t // tm
    hb = tm // SUBLANES

    def body(u_ref, v_ref, ga_ref, xb_ref, gb_ref, xbh_ref, h_ref, hh_ref, dy_ref,
             lng_ref, lnb_ref, wt_ref, wtt_ref, bsx_ref, cw_ref, cb_ref, wa_ref, wx_ref, ba_ref, bx_ref, lam_ref,
             oga_ref, ogb_ref,
             dz_ref, g_oga, g_ogb, g_lng, g_lnb, g_bsx, g_ws, g_cw, g_cb, g_wa, g_ba, g_wx, g_bx, g_lam,
             vn_s, mixed_s, dm_s, dvn_s, xcbf_s, r_s, i_s, a_s, b_s, dh_s, dpr_s, dpi_s, dxc_s,
             ca_s, cd_s, cx_s):
        step_i = pl.program_id(0)
        tile = nt - 1 - step_i
        accs = (g_oga, g_ogb, g_lng, g_lnb, g_bsx, g_ws, g_cw, g_cb, g_wa, g_ba, g_wx, g_bx, g_lam)

        @pl.when(step_i == 0)
        def _():
            for r in accs + (ca_s, cd_s, cx_s):
                r[...] = jnp.zeros_like(r)

        dy_a = dy_ref[:, 0:D_HALF]
        dy_b = dy_ref[:, D_HALF:]

        u = u_ref[...]
        ug, tu = _gelu(u)
        v = v_ref[...]
        vg, tv = _gelu(v)
        vhat, rstd = _layernorm_parts(vg)
        vn_s[...] = (vhat * lng_ref[...] + lnb_ref[...]).astype(BF16)
        _spatial_mix(wt_ref, vn_s, bsx_ref, mixed_s, tm)
        mixed = mixed_s[...]
        ga = ga_ref[...]
        sga = jax.nn.sigmoid(ga)
        sa = ga * sga
        um = ug * mixed
        ya = um * sa
        ra = lax.rsqrt(_lanemean(ya * ya) + EPS)
        yahat = ya * ra
        g_oga[...] += _rowsum8(dy_a * yahat)
        dn = dy_a * oga_ref[...]
        dya = ra * (dn - yahat * _lanemean(dn * yahat))
        dz_ref[:, 2 * D_HALF:3 * D_HALF] = (dya * um * (sga * (1.0 + ga * (1.0 - sga)))).astype(BF16)
        dz_ref[:, 0:D_HALF] = (dya * mixed * sa * _gelu_grad(u, tu)).astype(BF16)
        dmixed = dya * ug * sa
        g_bsx[...] += jnp.sum(dmixed.reshape(tm // CHUNK, CHUNK, D_HALF), axis=0)
        dm_s[...] = dmixed.astype(BF16)
        for c in range(tm // CHUNK):
            rows = slice(c * CHUNK, (c + 1) * CHUNK)
            for hd in range(N_HEADS):
                cols = slice(hd * CHUNK, (hd + 1) * CHUNK)
                dmh = dm_s[rows, cols]
                dvn_s[rows, cols] = _dot(wtt_ref[hd], dmh)
                g_ws[hd] += _dot_nt(dmh, vn_s[rows, cols])
        dvn = dvn_s[...]
        g_lng[...] += _rowsum8(dvn * vhat)
        g_lnb[...] += _rowsum8(dvn)
        dvh = dvn * lng_ref[...]
        dvg = rstd * (dvh - _lanemean(dvh) - vhat * _lanemean(dvh * vhat))
        dz_ref[:, D_HALF:2 * D_HALF] = (dvg * _gelu_grad(v, tv)).astype(BF16)

        xb = xb_ref[...]
        halo = jnp.where(tile == 0, 0.0, xbh_ref[...])
        taps = _conv_taps(xb, halo)
        xc = cb_ref[...] + taps[0] * cw_ref[0:1, :]
        for k in range(1, CONV_W):
            xc = xc + taps[k] * cw_ref[k:k + 1, :]
        xcbf_s[...] = xc.astype(BF16)
        _lru_gates(xcbf_s, wa_ref, wx_ref, ba_ref, bx_ref, r_s, i_s)
        rg = r_s[...]
        ig = i_s[...]
        lam = lam_ref[...]
        a, mult_true = _decay_parts(rg, lam)
        row = lax.broadcasted_iota(jnp.int32, a.shape, 0)
        first = jnp.logical_and(tile == 0, row == 0)
        mult = jnp.where(first, 1.0, mult_true)
        hcur = h_ref[...]
        hprev = _shift_down(hcur, jnp.where(tile == 0, 0.0, hh_ref[...]), 1)
        gb = gb_ref[...]
        sgb = jax.nn.sigmoid(gb)
        sb = gb * sgb
        yb = hcur * sb
        rb = lax.rsqrt(_lanemean(yb * yb) + EPS)
        ybhat = yb * rb
        g_ogb[...] += _rowsum8(dy_b * ybhat)
        dn = dy_b * ogb_ref[...]
        dyb = rb * (dn - ybhat * _lanemean(dn * ybhat))
        dz_ref[:, 4 * D_HALF:5 * D_HALF] = (dyb * hcur * (sgb * (1.0 + gb * (1.0 - sgb)))).astype(BF16)

        an = _shift_up(a, ca_s[...], 1)
        bb = dyb * sb
        r8 = row & (SUBLANES - 1)
        for d in (1, 2, 4):
            a_sh = pltpu.roll(an, tm - d, 0)
            b_sh = pltpu.roll(bb, tm - d, 0)
            m = r8 + d < SUBLANES
            bb = jnp.where(m, an * b_sh + bb, bb)
            an = jnp.where(m, an * a_sh, an)
        a_s[...] = an
        b_s[...] = bb

        def step(g, carry):
            sl = pl.ds(pl.multiple_of((hb - 1 - g) * SUBLANES, SUBLANES), SUBLANES)
            dg = a_s[sl, :] * carry + b_s[sl, :]
            dh_s[sl, :] = dg
            return jnp.broadcast_to(dg[0:1, :], dg.shape)

        cd_s[...] = lax.fori_loop(0, hb, step, cd_s[...])
        ca_s[...] = jnp.broadcast_to(a[0:1, :], ca_s.shape)
        dh = dh_s[...]
        da = dh * hprev
        gx = ig * xc
        dla = da * a - jnp.where(first, 0.0, dh * gx * (a * a / mult_true))
        g_lam[...] += _rowsum8(dla * rg)
        dr = dla * (-LRU_C * _softplus_neg(lam))
        dpr = dr * rg * (1.0 - rg)
        dpi = (dh * mult * xc) * ig * (1.0 - ig)
        g_ba[...] += _rowsum8(dpr)
        g_bx[...] += _rowsum8(dpi)
        dpr_s[...] = dpr.astype(BF16)
        dpi_s[...] = dpi.astype(BF16)
        for hd in range(N_HEADS):
            cols = slice(hd * CHUNK, (hd + 1) * CHUNK)
            xh = xcbf_s[:, cols]
            dprh = dpr_s[:, cols]
            dpih = dpi_s[:, cols]
            g_wa[hd] += _dot_tn(xh, dprh)
            g_wx[hd] += _dot_tn(xh, dpih)
            dxc_s[:, cols] = _dot_nt(dprh, wa_ref[hd]) + _dot_nt(dpih, wx_ref[hd])
        dxc = dxc_s[...] + dh * mult * ig
        g_cb[...] += _rowsum8(dxc)
        for k in range(CONV_W):
            g_cw[k * SUBLANES:(k + 1) * SUBLANES, :] += _rowsum8(dxc * taps[k])
        nxt = cx_s[...]
        dxb = dxc * cw_ref[CONV_W - 1:CONV_W, :]
        for j in range(1, CONV_W):
            dxb = dxb + _shift_up(dxc, nxt, j) * cw_ref[CONV_W - 1 - j:CONV_W - j, :]
        dz_ref[:, 3 * D_HALF:4 * D_HALF] = dxb.astype(BF16)
        cx_s[...] = dxc[0:SUBLANES]

        @pl.when(step_i == nt - 1)
        def _():
            for r in (g_oga, g_ogb, g_lng, g_lnb, g_cb, g_ba, g_bx):
                r[...] = jnp.broadcast_to(jnp.sum(r[...], axis=0, keepdims=True), r.shape)
            lam_f = LRU_C * jax.nn.sigmoid(-lam_ref[...])
            g_lam[...] = jnp.broadcast_to(jnp.sum(g_lam[...], axis=0, keepdims=True) * lam_f, g_lam.shape)
            for k in range(CONV_W):
                blk = g_cw[k * SUBLANES:(k + 1) * SUBLANES, :]
                g_cw[k * SUBLANES:(k + 1) * SUBLANES, :] = jnp.broadcast_to(jnp.sum(blk, axis=0, keepdims=True), blk.shape)
            tri = (lax.broadcasted_iota(jnp.int32, (CHUNK, CHUNK), 0) >= lax.broadcasted_iota(jnp.int32, (CHUNK, CHUNK), 1))
            for hd in range(N_HEADS):
                cols = slice(hd * CHUNK, (hd + 1) * CHUNK)
                g_ws[hd] = jnp.where(tri, g_ws[hd], 0.0)
                blk = g_bsx[:, cols]
                g_bsx[:, cols] = jnp.broadcast_to(jnp.sum(blk, axis=1, keepdims=True), blk.shape)

    rev = lambda i: nt - 1 - i
    zspec = lambda g: pl.BlockSpec((tm, D_HALF), lambda i, g=g: (rev(i), g))
    halo = lambda col: pl.BlockSpec((SUBLANES, D_HALF), lambda i: (jnp.maximum(rev(i) * hb - 1, 0), col))
    full = lambda a: pl.BlockSpec(a.shape, lambda i, n=a.ndim: (0,) * n)
    acc = lambda shp: pl.BlockSpec(shp, lambda i, n=len(shp): (0,) * n)
    names = ("ln_g", "ln_b", "wt", "wtt", "bsx", "conv_w", "conv_b", "w_a", "w_x", "b_a", "b_x", "lam", "oga", "ogb")
    pr = [prm[n] for n in names]
    vec = (SUBLANES, D_HALF)
    mat = (N_HEADS, CHUNK, CHUNK)
    acc_shapes = [vec, vec, vec, vec, (CHUNK, D_HALF), mat, (CONV_W * SUBLANES, D_HALF), vec, mat, vec, mat, vec, vec]
    big = lambda dt: pltpu.VMEM((tm, D_HALF), dt)
    return pl.pallas_call(
        body, name="branches_bwd", grid=(nt,),
        in_specs=[zspec(0), zspec(1), zspec(2), zspec(3), zspec(4), halo(3),
                  pl.BlockSpec((tm, D_HALF), lambda i: (rev(i), 0)), halo(0),
                  pl.BlockSpec((tm, D_MODEL), lambda i: (rev(i), 0))] + [full(a) for a in pr],
        out_specs=[pl.BlockSpec((tm, D_Z), lambda i: (rev(i), 0))] + [acc(s) for s in acc_shapes],
        out_shape=[jax.ShapeDtypeStruct((t, D_Z), BF16)] + [jax.ShapeDtypeStruct(s, F32) for s in acc_shapes],
        scratch_shapes=[big(BF16), big(F32), big(BF16), big(F32), big(BF16), big(F32), big(F32), big(F32), big(F32),
                        big(F32), big(BF16), big(BF16), big(F32),
                        pltpu.VMEM(vec, F32), pltpu.VMEM(vec, F32), pltpu.VMEM(vec, F32)],
        compiler_params=_params("arbitrary"),
    )(z, z, z, z, z, z, h, h, dy, *pr)


def _inproj_bwd(dz, wg_in, x, dh1, pre_g, tm):
    t = x.shape[0]
    nt = t // tm

    def body(dz_ref, w_ref, x_ref, dh1_ref, g_ref, gx_ref, gpre_ref, acc_s):
        i = pl.program_id(0)
        k = pl.program_id(1)

        @pl.when(jnp.logical_and(i == 0, k == 0))
        def _():
            gpre_ref[...] = jnp.zeros_like(gpre_ref)

        part = _dot_nt(dz_ref[...], w_ref[...])

        @pl.when(k == 0)
        def _():
            acc_s[...] = part

        @pl.when(k > 0)
        def _():
            acc_s[...] += part

        @pl.when(k == N_CHIPS - 1)
        def _():
            for s in range(tm // CHUNK):
                rows = slice(s * CHUNK, (s + 1) * CHUNK)
                xv = x_ref[rows, :]
                r = lax.rsqrt(_lanemean(xv * xv) + EPS)
                xhat = xv * r
                dhn = acc_s[rows, :]
                gpre_ref[...] += _rowsum8(dhn * xhat)
                dxh = dhn * g_ref[...]
                gx_ref[rows, :] = dh1_ref[rows, :] + r * (dxh - xhat * _lanemean(dxh * xhat))

        @pl.when(jnp.logical_and(i == nt - 1, k == N_CHIPS - 1))
        def _():
            gpre_ref[...] = jnp.broadcast_to(jnp.sum(gpre_ref[...], axis=0, keepdims=True), gpre_ref.shape)

    return pl.pallas_call(
        body, name="inproj_bwd", grid=(nt, N_CHIPS),
        in_specs=[pl.BlockSpec((tm, W_IN_COLS), lambda i, k: (i, k)),
                  pl.BlockSpec((None, D_MODEL, W_IN_COLS), lambda i, k: (k, 0, 0)),
                  pl.BlockSpec((tm, D_MODEL), lambda i, k: (i, 0)),
                  pl.BlockSpec((tm, D_MODEL), lambda i, k: (i, 0)),
                  pl.BlockSpec((1, D_MODEL), lambda i, k: (0, 0))],
        out_specs=[pl.BlockSpec((tm, D_MODEL), lambda i, k: (i, 0)),
                   pl.BlockSpec((SUBLANES, D_MODEL), lambda i, k: (0, 0))],
        out_shape=[jax.ShapeDtypeStruct((t, D_MODEL), F32), jax.ShapeDtypeStruct((SUBLANES, D_MODEL), F32)],
        scratch_shapes=[pltpu.VMEM((tm, D_MODEL), F32)],
        compiler_params=_params("arbitrary", "arbitrary"),
    )(dz, wg_in, x, dh1, pre_g)


def _weight_grad(a, b, name, kb, nb, tk, tn, tt):
    t = a.shape[0]

    def body(a_ref, b_ref, o_ref):
        @pl.when(pl.program_id(2) == 0)
        def _():
            o_ref[...] = jnp.zeros_like(o_ref)

        o_ref[...] += _dot_tn(a_ref[...].astype(BF16), b_ref[...])

    return pl.pallas_call(
        body, name=name, grid=(nb, kb, t // tt),
        in_specs=[pl.BlockSpec((tt, tk), lambda j, i, s: (s, i)), pl.BlockSpec((tt, tn), lambda j, i, s: (s, j))],
        out_specs=pl.BlockSpec((None, None, tk, tn), lambda j, i, s: (j, i, 0, 0)),
        out_shape=jax.ShapeDtypeStruct((nb, kb, tk, tn), F32),
        compiler_params=_params("parallel", "parallel", "arbitrary"),
    )(a, b)


def _place():
    x, y, c = lax.axis_index("x"), lax.axis_index("y"), lax.axis_index("c")
    return x, y, c


def _chip_of(x, y):
    return 2 * x + y


def _gather_weights(w_in, w_out, w_pg, w_pe, conv_w):
    halves = [(D_MODEL // 2, W_IN_COLS), (W_ROWS // 2, D_MODEL), (W_ROWS // 2, D_MODEL), (D_PLE // 2, W_PE_COLS)]

    def body(win_ref, wout_ref, wpg_ref, wpe_ref, cw_ref,
             gin_ref, gout_ref, gpg_ref, gpe_ref, gcw_ref,
             s0, s1, s2, s3, b0, b1, b2, b3, lsem, send_sems, recv_sems, cw_send, cw_recv):
        x, y, c = _place()
        me = _chip_of(x, y)
        sibling = (x, y, 1 - c)
        chips = [(1 - x, y), (x, 1 - y), (1 - x, 1 - y)]
        srcs = (win_ref, wout_ref, wpg_ref, wpe_ref)
        stage = (s0, s1, s2, s3)
        bf = (b0, b1, b2, b3)
        outs = (gin_ref, gout_ref, gpg_ref, gpe_ref)
        loads = []
        for n in range(4):
            rows = halves[n][0]
            cp = pltpu.make_async_copy(srcs[n].at[pl.ds(c * rows, rows), :], stage[n], lsem.at[n])
            cp.start()
            loads.append(cp)
        own_cw = pltpu.make_async_copy(cw_ref, gcw_ref.at[me], lsem.at[4])
        own_cw.start()
        for n in range(4):
            loads[n].wait()
            bf[n][...] = stage[n][...].astype(BF16)

        def copy(n, k, chip, to, src=None):
            dst = outs[n].at[chip, c]
            return pltpu.make_async_remote_copy(
                src_ref=dst if src is None else src, dst_ref=dst,
                send_sem=send_sems.at[n, k], recv_sem=recv_sems.at[n, k], device_id=to, device_id_type=MESH)

        def recv(n, k, chip, core):
            dst = outs[n].at[chip, core]
            return pltpu.make_async_remote_copy(
                src_ref=dst, dst_ref=dst, send_sem=send_sems.at[n, k], recv_sem=recv_sems.at[n, k],
                device_id=sibling, device_id_type=MESH)

        sends = []
        locals_ = []
        for n in range(4):
            lc = pltpu.make_async_copy(bf[n], outs[n].at[me, c], lsem.at[5 + n])
            lc.start()
            locals_.append(lc)
            first = [copy(n, 0, me, sibling, src=bf[n])]
            first += [copy(n, 1 + j, me, (*chip, c), src=bf[n]) for j, chip in enumerate(chips)]
            for cp in first:
                cp.start()
            sends += first
        cws = []
        for j, chip in enumerate(chips):
            cp = pltpu.make_async_remote_copy(
                src_ref=cw_ref, dst_ref=gcw_ref.at[me], send_sem=cw_send.at[j], recv_sem=cw_recv.at[j],
                device_id=(*chip, c), device_id_type=MESH)
            cp.start()
            cws.append(cp)
        for n in range(4):
            for j, chip in enumerate(chips):
                kj = _chip_of(*chip)
                recv(n, 1 + j, kj, c).wait_recv()
                fw = copy(n, 4 + j, kj, sibling)
                fw.start()
                sends.append(fw)
        for n in range(4):
            recv(n, 0, me, 1 - c).wait_recv()
            for j, chip in enumerate(chips):
                recv(n, 4 + j, _chip_of(*chip), 1 - c).wait_recv()
        for j, chip in enumerate(chips):
            pltpu.make_async_remote_copy(
                src_ref=cw_ref, dst_ref=gcw_ref.at[_chip_of(*chip)], send_sem=cw_send.at[j], recv_sem=cw_recv.at[j],
                device_id=(*chip, c), device_id_type=MESH).wait_recv()
        for cp in sends + cws:
            cp.wait_send()
        for lc in locals_:
            lc.wait()
        own_cw.wait()

    out_shape = [jax.ShapeDtypeStruct((N_CHIPS, 2) + hs, BF16) for hs in halves]
    out_shape.append(jax.ShapeDtypeStruct((N_CHIPS, CONV_W, CONV_COLS), F32))
    scratch = [pltpu.VMEM(hs, F32) for hs in halves] + [pltpu.VMEM(hs, BF16) for hs in halves]
    scratch += [pltpu.SemaphoreType.DMA((9,)), pltpu.SemaphoreType.DMA((4, 7)), pltpu.SemaphoreType.DMA((4, 7)),
                pltpu.SemaphoreType.DMA((3,)), pltpu.SemaphoreType.DMA((3,))]
    return pl.pallas_call(
        body, name="gather_weights", in_specs=[ANY] * 5, out_specs=[ANY] * 5, out_shape=out_shape,
        scratch_shapes=scratch, compiler_params=pltpu.CompilerParams(vmem_limit_bytes=VMEM_LIMIT),
    )(w_in, w_out, w_pg, w_pe, conv_w)


def _sibling_exchange(grads):
    n = len(grads)

    def body(*refs):
        g_refs, r_refs = refs[:n], refs[n:2 * n]
        send_sems, recv_sems = refs[2 * n:]
        x, y, c = _place()
        cps = []
        for b in range(n):
            cp = pltpu.make_async_remote_copy(
                src_ref=g_refs[b].at[:, 1 - c], dst_ref=r_refs[b], send_sem=send_sems.at[b], recv_sem=recv_sems.at[b],
                device_id=(x, y, 1 - c), device_id_type=MESH)
            cp.start()
            cps.append(cp)
        for cp in cps:
            cp.wait()

    out_shape = [jax.ShapeDtypeStruct((g.shape[0],) + g.shape[2:], g.dtype) for g in grads]
    return pl.pallas_call(
        body, name="sibling_exchange", in_specs=[ANY] * n, out_specs=[ANY] * n, out_shape=out_shape,
        scratch_shapes=[pltpu.SemaphoreType.DMA((n,)), pltpu.SemaphoreType.DMA((n,))],
    )(*grads)


def _pair_sum(g, r1, kc, name, tr, send_dtype):
    nk, _, rows, cols = g.shape

    def body(kc_ref, g_ref, r_ref, p_ref, own_ref):
        s = g_ref[...] + r_ref[...]
        p_ref[...] = s.astype(send_dtype)

        @pl.when(pl.program_id(1) == kc_ref[0])
        def _():
            own_ref[...] = s

    grid_spec = pltpu.PrefetchScalarGridSpec(
        num_scalar_prefetch=1, grid=(rows // tr, nk),
        in_specs=[pl.BlockSpec((None, None, tr, cols), lambda r, k, kc: (k, kc[1], r, 0)),
                  pl.BlockSpec((None, tr, cols), lambda r, k, kc: (k, r, 0))],
        out_specs=[pl.BlockSpec((None, tr, cols), lambda r, k, kc: (k, r, 0)),
                   pl.BlockSpec((tr, cols), lambda r, k, kc: (r, 0))])
    return pl.pallas_call(
        body, name=name, grid_spec=grid_spec,
        out_shape=[jax.ShapeDtypeStruct((nk, rows, cols), send_dtype), jax.ShapeDtypeStruct((rows, cols), F32)],
        compiler_params=_params("arbitrary", "arbitrary"),
    )(kc, g, r1)


def _chip_exchange(pieces):
    n = len(pieces)

    def body(*refs):
        p_refs, r_refs = refs[:n], refs[n:2 * n]
        send_sems, recv_sems = refs[2 * n:]
        x, y, c = _place()
        chips = [(1 - x, y), (x, 1 - y), (1 - x, 1 - y)]
        cps = []
        for b in range(n):
            for j, chip in enumerate(chips):
                cp = pltpu.make_async_remote_copy(
                    src_ref=p_refs[b].at[_chip_of(*chip)], dst_ref=r_refs[b].at[j],
                    send_sem=send_sems.at[b, j], recv_sem=recv_sems.at[b, j],
                    device_id=(*chip, c), device_id_type=MESH)
                cp.start()
                cps.append(cp)
        for cp in cps:
            cp.wait()

    out_shape = [jax.ShapeDtypeStruct((3,) + p.shape[1:], p.dtype) for p in pieces]
    return pl.pallas_call(
        body, name="chip_exchange", in_specs=[ANY] * n, out_specs=[ANY] * n, out_shape=out_shape,
        scratch_shapes=[pltpu.SemaphoreType.DMA((n, 3)), pltpu.SemaphoreType.DMA((n, 3))],
    )(*pieces)


def _chip_sum(own, r2, name, tr):
    rows, cols = own.shape

    def body(o_ref, r_ref, s_ref):
        s = o_ref[...]
        for j in range(3):
            s = s + r_ref[j].astype(F32)
        s_ref[...] = s

    return pl.pallas_call(
        body, name=name, grid=(rows // tr,),
        in_specs=[pl.BlockSpec((tr, cols), lambda r: (r, 0)), pl.BlockSpec((3, tr, cols), lambda r: (0, r, 0))],
        out_specs=pl.BlockSpec((tr, cols), lambda r: (r, 0)),
        out_shape=jax.ShapeDtypeStruct((rows, cols), F32),
        compiler_params=_params("parallel"),
    )(own, r2)


def _finish_exchange(sums, small_sum):
    n = len(sums)

    def body(*refs):
        s_refs, sm_ref = refs[:n], refs[n]
        f_refs, fsm_ref = refs[n + 1:2 * n + 1], refs[2 * n + 1]
        lsem, send_sems, recv_sems, sm_send, sm_recv = refs[2 * n + 2:]
        x, y, c = _place()
        me = _chip_of(x, y)
        locs, cps = [], []
        for b in range(n):
            lc = pltpu.make_async_copy(s_refs[b], f_refs[b].at[c], lsem.at[b])
            lc.start()
            locs.append(lc)
            cp = pltpu.make_async_remote_copy(
                src_ref=s_refs[b], dst_ref=f_refs[b].at[c], send_sem=send_sems.at[b], recv_sem=recv_sems.at[b],
                device_id=(x, y, 1 - c), device_id_type=MESH)
            cp.start()
            cps.append(cp)
        lc = pltpu.make_async_copy(sm_ref, fsm_ref.at[me, c], lsem.at[n])
        lc.start()
        locs.append(lc)
        flips = [(fx, fy, fc) for fx in (0, 1) for fy in (0, 1) for fc in (0, 1)][1:]
        sm = []
        for q, (fx, fy, fc) in enumerate(flips):
            cp = pltpu.make_async_remote_copy(
                src_ref=sm_ref, dst_ref=fsm_ref.at[me, c], send_sem=sm_send.at[q], recv_sem=sm_recv.at[q],
                device_id=(x ^ fx, y ^ fy, c ^ fc), device_id_type=MESH)
            cp.start()
            sm.append(cp)
        for b in range(n):
            pltpu.make_async_remote_copy(
                src_ref=s_refs[b], dst_ref=f_refs[b].at[1 - c], send_sem=send_sems.at[b], recv_sem=recv_sems.at[b],
                device_id=(x, y, 1 - c), device_id_type=MESH).wait_recv()
        for q, (fx, fy, fc) in enumerate(flips):
            pltpu.make_async_remote_copy(
                src_ref=sm_ref, dst_ref=fsm_ref.at[_chip_of(x ^ fx, y ^ fy), c ^ fc],
                send_sem=sm_send.at[q], recv_sem=sm_recv.at[q],
                device_id=(x ^ fx, y ^ fy, c ^ fc), device_id_type=MESH).wait_recv()
        for cp in cps + sm:
            cp.wait_send()
        for lc in locs:
            lc.wait()

    out_shape = [jax.ShapeDtypeStruct((2,) + s.shape, F32) for s in sums]
    out_shape.append(jax.ShapeDtypeStruct((N_CHIPS, 2) + small_sum.shape, F32))
    return pl.pallas_call(
        body, name="finish_exchange", in_specs=[ANY] * (n + 1), out_specs=[ANY] * (n + 1), out_shape=out_shape,
        scratch_shapes=[pltpu.SemaphoreType.DMA((n + 1,)), pltpu.SemaphoreType.DMA((n,)), pltpu.SemaphoreType.DMA((n,)),
                        pltpu.SemaphoreType.DMA((7,)), pltpu.SemaphoreType.DMA((7,))],
    )(*sums, small_sum)


def _adamw(w, g, m, v, name, tr):
    rows, cols = w.shape

    def body(w_ref, g_ref, m_ref, v_ref, d_ref, nm_ref, nv_ref):
        gv = g_ref[...]
        nm = ADAM_B1 * m_ref[...] + (1.0 - ADAM_B1) * gv
        nv = ADAM_B2 * v_ref[...] + (1.0 - ADAM_B2) * (gv * gv)
        m_hat = nm / (1.0 - ADAM_B1 ** ADAM_STEP)
        v_hat = nv / (1.0 - ADAM_B2 ** ADAM_STEP)
        d_ref[...] = -ADAM_LR * (m_hat / (jnp.sqrt(v_hat) + ADAM_EPS) + ADAM_WD * w_ref[...])
        nm_ref[...] = nm
        nv_ref[...] = nv

    spec = pl.BlockSpec((tr, cols), lambda r: (r, 0))
    return pl.pallas_call(
        body, name=name, grid=(rows // tr,), in_specs=[spec] * 4, out_specs=[spec] * 3,
        out_shape=[jax.ShapeDtypeStruct((rows, cols), F32)] * 3,
        compiler_params=_params("parallel"),
    )(w, g, m, v)


def _rows128(a):
    return a.reshape(-1, LANES)


def _pack_small(parts):
    pieces = [_rows128(parts[n]) for n, _ in SMALL_ROWS]
    pieces.append(jnp.zeros((SMALL_TOTAL - SMALL_USED, LANES), F32))
    return jnp.concatenate(pieces, axis=0)


def _unpack_small(packed, shapes):
    out, at = {}, 0
    for n, r in SMALL_ROWS:
        out[n] = packed[at:at + r].reshape(shapes[n])
        at += r
    return out


def kernel(x, p, pre_g, w_in, gmlp_ln_g, gmlp_ln_b, gmlp_ws, gmlp_bs, conv_w, conv_b, w_a, b_a, w_x, b_x, lam, gmlp_out_g, lru_out_g, w_out, post_g, w_pe, w_pg, loss_target, m_pre_g, m_w_in, m_gmlp_ln_g, m_gmlp_ln_b, m_gmlp_ws, m_gmlp_bs, m_conv_w, m_conv_b, m_w_a, m_b_a, m_w_x, m_b_x, m_lam, m_gmlp_out_g, m_lru_out_g, m_w_out, m_post_g, m_w_pe, m_w_pg, v_pre_g, v_w_in, v_gmlp_ln_g, v_gmlp_ln_b, v_gmlp_ws, v_gmlp_bs, v_conv_w, v_conv_b, v_w_a, v_b_a, v_w_x, v_b_x, v_lam, v_gmlp_out_g, v_lru_out_g, v_w_out, v_post_g, v_w_pe, v_w_pg):
    weights = dict(pre_g=pre_g, w_in=w_in, gmlp_ln_g=gmlp_ln_g, gmlp_ln_b=gmlp_ln_b, gmlp_ws=gmlp_ws, gmlp_bs=gmlp_bs,
                   conv_w=conv_w, conv_b=conv_b, w_a=w_a, b_a=b_a, w_x=w_x, b_x=b_x, lam=lam, gmlp_out_g=gmlp_out_g,
                   lru_out_g=lru_out_g, w_out=w_out, post_g=post_g, w_pe=w_pe, w_pg=w_pg)
    mom_m = dict(pre_g=m_pre_g, w_in=m_w_in, gmlp_ln_g=m_gmlp_ln_g, gmlp_ln_b=m_gmlp_ln_b, gmlp_ws=m_gmlp_ws,
                 gmlp_bs=m_gmlp_bs, conv_w=m_conv_w, conv_b=m_conv_b, w_a=m_w_a, b_a=m_b_a, w_x=m_w_x, b_x=m_b_x,
                 lam=m_lam, gmlp_out_g=m_gmlp_out_g, lru_out_g=m_lru_out_g, w_out=m_w_out, post_g=m_post_g,
                 w_pe=m_w_pe, w_pg=m_w_pg)
    mom_v = dict(pre_g=v_pre_g, w_in=v_w_in, gmlp_ln_g=v_gmlp_ln_g, gmlp_ln_b=v_gmlp_ln_b, gmlp_ws=v_gmlp_ws,
                 gmlp_bs=v_gmlp_bs, conv_w=v_conv_w, conv_b=v_conv_b, w_a=v_w_a, b_a=v_b_a, w_x=v_w_x, b_x=v_b_x,
                 lam=v_lam, gmlp_out_g=v_gmlp_out_g, lru_out_g=v_lru_out_g, w_out=v_w_out, post_g=v_post_g,
                 w_pe=v_w_pe, w_pg=v_w_pg)
    order = list(weights)
    xi, yi, ci = _place()
    me = _chip_of(xi, yi)
    kc = jnp.stack([me, ci]).astype(jnp.int32)

    x2 = x[0]
    p2 = p[0, 0]
    tgt = loss_target[0]

    g_in, g_out, g_pg, g_pe, g_cw = _gather_weights(w_in[0], w_out[0], w_pg[0], w_pe[0], conv_w[0, :, 0, :])
    wg_in = g_in.reshape(N_CHIPS, D_MODEL, W_IN_COLS)
    wg_out = g_out.reshape(D_MODEL, D_MODEL)
    wg_pg = g_pg.reshape(D_MODEL, D_MODEL)
    wg_pe = g_pe.reshape(N_CHIPS, D_PLE, W_PE_COLS)
    cw_full = jnp.transpose(g_cw, (1, 0, 2)).reshape(CONV_W, D_HALF)

    causal = jnp.tril(jnp.ones((CHUNK, CHUNK), dtype=bool))
    ws_m = jnp.where(causal[None], gmlp_ws[0], 0.0)
    prm = dict(
        ln_g=gmlp_ln_g, ln_b=gmlp_ln_b, wt=ws_m.astype(BF16), wtt=jnp.transpose(ws_m, (0, 2, 1)).astype(BF16),
        bsx=jnp.repeat(jnp.transpose(gmlp_bs[0]), CHUNK, axis=1),
        conv_w=cw_full, conv_b=conv_b, w_a=w_a[0].astype(BF16), w_x=w_x[0].astype(BF16),
        b_a=b_a[0].reshape(1, D_HALF), b_x=b_x[0].reshape(1, D_HALF), lam=lam, oga=gmlp_out_g, ogb=lru_out_g)

    z, hn = _inproj_fwd(x2, pre_g, wg_in, 512)
    y, h = _branches_fwd(z, prm, 256)
    o, h1, gt, dout, loss_acc = _outproj_fwd(x2, y, p2, tgt, post_g, wg_out, wg_pg, wg_pe, 256)
    loss = lax.psum(loss_acc[0, 0], ("x", "y", "c"))

    dpe, dq, dh1, do, dy, g_post = _head_bwd(dout, gt, p2, o, post_g, wg_out, wg_pg, wg_pe, 256)
    (dz, g_oga, g_ogb, g_lng, g_lnb, g_bsx, g_ws, g_cw, g_cb, g_wa, g_ba, g_wx, g_bx, g_lam) = _branches_bwd(
        z, h, dy, prm, 256)
    grad_x, g_pre = _inproj_bwd(dz, wg_in, x2, dh1, pre_g, 512)
    gw_in = _weight_grad(hn, dz, "grad_w_in", 2, N_CHIPS, D_MODEL // 2, W_IN_COLS, 512)
    gw_out = _weight_grad(y, do, "grad_w_out", 2, 1, D_MODEL // 2, D_MODEL, 512)
    gw_pg = _weight_grad(h1, dq, "grad_w_pg", 2, 1, D_MODEL // 2, D_MODEL, 512)
    gw_pe = _weight_grad(p2, dpe, "grad_w_pe", 2, N_CHIPS, D_PLE // 2, W_PE_COLS, 512)
    gw_out = gw_out.reshape(N_CHIPS, 2, W_ROWS // 2, D_MODEL)
    gw_pg = gw_pg.reshape(N_CHIPS, 2, W_ROWS // 2, D_MODEL)

    small_g = dict(
        pre_g=g_pre[0:1], gmlp_ln_g=g_lng[0:1], gmlp_ln_b=g_lnb[0:1], gmlp_ws=g_ws,
        gmlp_bs=jnp.transpose(g_bsx[:, ::CHUNK]), conv_w=g_cw[::SUBLANES], conv_b=g_cb[0:1], w_a=g_wa, b_a=g_ba[0:1],
        w_x=g_wx, b_x=g_bx[0:1], lam=g_lam[0:1], gmlp_out_g=g_oga[0:1], lru_out_g=g_ogb[0:1], post_g=g_post[0:1])
    gsm = _pack_small(small_g).reshape(N_CHIPS, 2, SMALL_PIECE, LANES)

    bufs = [gw_in, gw_out, gw_pg, gw_pe, gsm]
    r1 = _sibling_exchange(bufs)
    tiles = [256, 128, 128, 128, SMALL_PIECE]
    names = ["w_in", "w_out", "w_pg", "w_pe", "small"]
    pairs = [_pair_sum(bufs[b], r1[b], kc, "pair_sum_" + names[b], tiles[b], BF16 if b < 4 else F32) for b in range(5)]
    r2 = _chip_exchange([pr[0] for pr in pairs])
    sums = [_chip_sum(pairs[b][1], r2[b], "chip_sum_" + names[b], tiles[b]) for b in range(5)]
    f_in, f_out, f_pg, f_pe, f_sm = _finish_exchange(sums[:4], sums[4])

    big_g = dict(w_in=f_in.reshape(D_MODEL, W_IN_COLS), w_out=f_out.reshape(W_ROWS, D_MODEL),
                 w_pg=f_pg.reshape(W_ROWS, D_MODEL), w_pe=f_pe.reshape(D_PLE, W_PE_COLS))
    grads, deltas, new_m, new_v = {}, {}, {}, {}
    for n, tr in (("w_in", 256), ("w_out", 128), ("w_pg", 128), ("w_pe", 128)):
        shp = weights[n].shape
        grads[n] = big_g[n].reshape(shp)
        d, nm, nv = _adamw(weights[n][0], big_g[n], mom_m[n][0], mom_v[n][0], "adamw_" + n, tr)
        deltas[n], new_m[n], new_v[n] = d.reshape(shp), nm.reshape(shp), nv.reshape(shp)

    packed_g = f_sm.reshape(SMALL_TOTAL, LANES)
    small_names = [n for n, _ in SMALL_ROWS]
    shapes = {n: weights[n].shape for n in small_names}
    shapes["conv_w"] = (CONV_W, D_HALF)
    zero_cw = jnp.zeros((CONV_W, D_HALF), F32)
    pack_w = lambda src: _pack_small({n: (zero_cw if n == "conv_w" else src[n]) for n in small_names})
    d_sm, m_sm, v_sm = _adamw(pack_w(weights), packed_g, pack_w(mom_m), pack_w(mom_v), "adamw_small", SMALL_PIECE)
    ug, ud, um, uv = (_unpack_small(a, shapes) for a in (packed_g, d_sm, m_sm, v_sm))
    for n in small_names:
        if n != "conv_w":
            grads[n], deltas[n], new_m[n], new_v[n] = ug[n], ud[n], um[n], uv[n]
    g_conv = lax.dynamic_slice_in_dim(ug["conv_w"], me * CONV_COLS, CONV_COLS, axis=1)
    d, nm, nv = _adamw(conv_w[0, :, 0, :], g_conv, m_conv_w[0, :, 0, :], v_conv_w[0, :, 0, :], "adamw_conv_w", CONV_W)
    cshape = conv_w.shape
    grads["conv_w"], deltas["conv_w"] = g_conv.reshape(cshape), d.reshape(cshape)
    new_m["conv_w"], new_v["conv_w"] = nm.reshape(cshape), nv.reshape(cshape)

    return (loss, grad_x.reshape(x.shape), *[grads[n] for n in order], *[deltas[n] for n in order],
            *[new_m[n] for n in order], *[new_v[n] for n in order])
```

```python
import functools
import math

import jax
import jax.numpy as jnp
from jax import lax
from jax.experimental import pallas as pl
from jax.experimental.pallas import tpu as pltpu

F32 = jnp.float32
BF16 = jnp.bfloat16

D_MODEL = 2048
D_HALF = 1024
D_Z = 5120
D_PLE = 256
CHUNK = 128
N_HEADS = 8
N_CHIPS = 4
W_IN_COLS = D_Z // N_CHIPS
W_ROWS = D_MODEL // N_CHIPS
W_PE_COLS = D_MODEL // N_CHIPS
CONV_W = 4
CONV_COLS = D_HALF // N_CHIPS
EPS = 1e-6
LRU_C = 8.0
ADAM_LR, ADAM_B1, ADAM_B2, ADAM_EPS, ADAM_WD, ADAM_STEP = 0.001, 0.9, 0.999, 1e-08, 0.01, 10

SUBLANES = 8
LANES = 128
VMEM_LIMIT = 56 * 1024 * 1024

SMALL_ROWS = (("pre_g", 16), ("gmlp_ln_g", 8), ("gmlp_ln_b", 8), ("gmlp_ws", 1024), ("gmlp_bs", 8),
              ("conv_w", 32), ("conv_b", 8), ("w_a", 1024), ("b_a", 8), ("w_x", 1024), ("b_x", 8),
              ("lam", 8), ("gmlp_out_g", 8), ("lru_out_g", 8), ("post_g", 16))
SMALL_USED = sum(r for _, r in SMALL_ROWS)
SMALL_PIECE = 408
SMALL_TOTAL = 8 * SMALL_PIECE

MESH = pl.DeviceIdType.MESH
ANY = pl.BlockSpec(memory_space=pl.ANY)

_GELU_C0 = math.sqrt(2.0 / math.pi)
_GELU_C1 = 0.044715


def _params(*sem):
    return pltpu.CompilerParams(dimension_semantics=sem, vmem_limit_bytes=VMEM_LIMIT)


def _dot(a, b):
    return jnp.dot(a, b, preferred_element_type=F32)


def _dot_nt(a, b):
    return lax.dot_general(a, b, (((1,), (1,)), ((), ())), preferred_element_type=F32)


def _dot_tn(a, b):
    return lax.dot_general(a, b, (((0,), (0,)), ((), ())), preferred_element_type=F32)


def _gelu(x):
    t = jnp.tanh(_GELU_C0 * (x + _GELU_C1 * (x * x * x)))
    return 0.5 * x * (1.0 + t), t


def _gelu_grad(x, t):
    return 0.5 * (1.0 + t) + 0.5 * x * (1.0 - t * t) * (_GELU_C0 * (1.0 + 3.0 * _GELU_C1 * x * x))


def _rowsum8(v):
    r, n = v.shape
    return jnp.sum(v.reshape(r // SUBLANES, SUBLANES, n), axis=0)


def _lanemean(v):
    return jnp.mean(v, axis=-1, keepdims=True)


def _shift_down(v, halo8, k):
    if k == 0:
        return v
    r = pltpu.roll(v, k, 0)
    hr = pltpu.roll(halo8, k, 0)
    row = lax.broadcasted_iota(jnp.int32, halo8.shape, 0)
    top = jnp.where(row < k, hr, r[0:SUBLANES])
    return jnp.concatenate([top, r[SUBLANES:]], axis=0)


def _shift_up(v, next8, k):
    if k == 0:
        return v
    n = v.shape[0]
    r = pltpu.roll(v, n - k, 0)
    nr = pltpu.roll(next8, SUBLANES - k, 0)
    row = lax.broadcasted_iota(jnp.int32, next8.shape, 0)
    bot = jnp.where(row >= SUBLANES - k, nr, r[n - SUBLANES:])
    return jnp.concatenate([r[:n - SUBLANES], bot], axis=0)


def _inproj_fwd(x, pre_g, wg_in, tm):
    t = x.shape[0]

    def body(x_ref, g_ref, w_ref, z_ref, hn_ref):
        @pl.when(pl.program_id(1) == 0)
        def _():
            xv = x_ref[...]
            r = lax.rsqrt(_lanemean(xv * xv) + EPS)
            hn_ref[...] = (xv * r * g_ref[...]).astype(BF16)

        z_ref[...] = _dot(hn_ref[...], w_ref[...])

    return pl.pallas_call(
        body, name="inproj_fwd", grid=(t // tm, N_CHIPS),
        in_specs=[pl.BlockSpec((tm, D_MODEL), lambda i, j: (i, 0)),
                  pl.BlockSpec((1, D_MODEL), lambda i, j: (0, 0)),
                  pl.BlockSpec((None, D_MODEL, W_IN_COLS), lambda i, j: (j, 0, 0))],
        out_specs=[pl.BlockSpec((tm, W_IN_COLS), lambda i, j: (i, j)),
                   pl.BlockSpec((tm, D_MODEL), lambda i, j: (i, 0))],
        out_shape=[jax.ShapeDtypeStruct((t, D_Z), F32), jax.ShapeDtypeStruct((t, D_MODEL), BF16)],
        compiler_params=_params("parallel", "arbitrary"),
    )(x, pre_g, wg_in)


def _layernorm_parts(vg):
    mu = _lanemean(vg)
    xc = vg - mu
    rstd = lax.rsqrt(_lanemean(xc * xc) + EPS)
    return xc * rstd, rstd


def _spatial_mix(wt_ref, vn_ref, bsx_ref, mixed_ref, tm):
    for c in range(tm // CHUNK):
        rows = slice(c * CHUNK, (c + 1) * CHUNK)
        for h in range(N_HEADS):
            cols = slice(h * CHUNK, (h + 1) * CHUNK)
            mixed_ref[rows, cols] = _dot(wt_ref[h], vn_ref[rows, cols]) + bsx_ref[:, cols]


def _conv_taps(xb, halo8):
    return [_shift_down(xb, halo8, CONV_W - 1 - k) for k in range(CONV_W)]


def _lru_gates(xc_bf_ref, wa_ref, wx_ref, ba_ref, bx_ref, r_ref, i_ref):
    for h in range(N_HEADS):
        cols = slice(h * CHUNK, (h + 1) * CHUNK)
        xh = xc_bf_ref[:, cols]
        r_ref[:, cols] = jax.nn.sigmoid(_dot(xh, wa_ref[h]) + ba_ref[:, cols])
        i_ref[:, cols] = jax.nn.sigmoid(_dot(xh, wx_ref[h]) + bx_ref[:, cols])


def _softplus_neg(lam):
    return jnp.maximum(-lam, 0.0) + jnp.log(1.0 + jnp.exp(-jnp.abs(lam)))


def _decay_parts(r, lam):
    la = (-LRU_C * _softplus_neg(lam)) * r
    a = jnp.exp(la)
    th = -jnp.tanh(la)
    mult = jnp.sqrt(2.0 * th / (1.0 + th))
    return a, mult


def _branches_fwd(z, prm, tm):
    t = z.shape[0]
    nt = t // tm
    hb = tm // SUBLANES

    def body(u_ref, v_ref, ga_ref, xb_ref, gb_ref, xbh_ref,
             lng_ref, lnb_ref, wt_ref, bsx_ref, cw_ref, cb_ref, wa_ref, wx_ref, ba_ref, bx_ref, lam_ref,
             oga_ref, ogb_ref,
             y_ref, h_ref,
             vn_s, mixed_s, xcbf_s, r_s, i_s, a_s, b_s, carry_s):
        i = pl.program_id(0)

        @pl.when(i == 0)
        def _():
            carry_s[...] = jnp.zeros_like(carry_s)

        ug, _ = _gelu(u_ref[...])
        vg, _ = _gelu(v_ref[...])
        vhat, _ = _layernorm_parts(vg)
        vn_s[...] = (vhat * lng_ref[...] + lnb_ref[...]).astype(BF16)
        _spatial_mix(wt_ref, vn_s, bsx_ref, mixed_s, tm)
        ga = ga_ref[...]
        ya = ug * mixed_s[...] * (ga * jax.nn.sigmoid(ga))
        ra = lax.rsqrt(_lanemean(ya * ya) + EPS)
        y_ref[:, 0:D_HALF] = (ya * ra * oga_ref[...]).astype(BF16)

        xb = xb_ref[...]
        halo = jnp.where(i == 0, 0.0, xbh_ref[...])
        taps = _conv_taps(xb, halo)
        xc = cb_ref[...] + taps[0] * cw_ref[0:1, :]
        for k in range(1, CONV_W):
            xc = xc + taps[k] * cw_ref[k:k + 1, :]
        xcbf_s[...] = xc.astype(BF16)
        _lru_gates(xcbf_s, wa_ref, wx_ref, ba_ref, bx_ref, r_s, i_s)
        a, mult = _decay_parts(r_s[...], lam_ref[...])
        row = lax.broadcasted_iota(jnp.int32, a.shape, 0)
        mult = jnp.where(jnp.logical_and(i == 0, row == 0), 1.0, mult)
        b = mult * (i_s[...] * xc)
        r8 = row & (SUBLANES - 1)
        for d in (1, 2, 4):
            a_sh = pltpu.roll(a, d, 0)
            b_sh = pltpu.roll(b, d, 0)
            m = r8 >= d
            b = jnp.where(m, a * b_sh + b, b)
            a = jnp.where(m, a * a_sh, a)
        a_s[...] = a
        b_s[...] = b

        def step(g, carry):
            sl = pl.ds(pl.multiple_of(g * SUBLANES, SUBLANES), SUBLANES)
            hg = a_s[sl, :] * carry + b_s[sl, :]
            h_ref[sl, :] = hg
            return jnp.broadcast_to(hg[SUBLANES - 1:SUBLANES, :], hg.shape)

        carry_s[...] = lax.fori_loop(0, hb, step, carry_s[...])
        gb = gb_ref[...]
        yb = h_ref[...] * (gb * jax.nn.sigmoid(gb))
        rb = lax.rsqrt(_lanemean(yb * yb) + EPS)
        y_ref[:, D_HALF:] = (yb * rb * ogb_ref[...]).astype(BF16)

    zspec = lambda g: pl.BlockSpec((tm, D_HALF), lambda i, g=g: (i, g))
    full = lambda a: pl.BlockSpec(a.shape, lambda i, n=a.ndim: (0,) * n)
    names = ("ln_g", "ln_b", "wt", "bsx", "conv_w", "conv_b", "w_a", "w_x", "b_a", "b_x", "lam", "oga", "ogb")
    pr = [prm[n] for n in names]
    return pl.pallas_call(
        body, name="branches_fwd", grid=(nt,),
        in_specs=[zspec(0), zspec(1), zspec(2), zspec(3), zspec(4),
                  pl.BlockSpec((SUBLANES, D_HALF), lambda i: (jnp.maximum(i * hb - 1, 0), 3))]
                 + [full(a) for a in pr],
        out_specs=[pl.BlockSpec((tm, D_MODEL), lambda i: (i, 0)), pl.BlockSpec((tm, D_HALF), lambda i: (i, 0))],
        out_shape=[jax.ShapeDtypeStruct((t, D_MODEL), BF16), jax.ShapeDtypeStruct((t, D_HALF), F32)],
        scratch_shapes=[pltpu.VMEM((tm, D_HALF), BF16), pltpu.VMEM((tm, D_HALF), F32),
                        pltpu.VMEM((tm, D_HALF), BF16), pltpu.VMEM((tm, D_HALF), F32),
                        pltpu.VMEM((tm, D_HALF), F32), pltpu.VMEM((tm, D_HALF), F32),
                        pltpu.VMEM((tm, D_HALF), F32), pltpu.VMEM((SUBLANES, D_HALF), F32)],
        compiler_params=_params("arbitrary"),
    )(z, z, z, z, z, z, *pr)


def _outproj_fwd(x, y, p, tgt, post_g, w_out, w_pg, wg_pe, tm):
    t = x.shape[0]

    def body(x_ref, y_ref, p_ref, tgt_ref, pg_ref, wo_ref, wpg_ref, wpe_ref,
             o_ref, h1_ref, gt_ref, dout_ref, loss_ref):
        @pl.when(pl.program_id(0) == 0)
        def _():
            loss_ref[...] = jnp.zeros_like(loss_ref)

        o = _dot(y_ref[...], wo_ref[...])
        o_ref[...] = o
        r3 = lax.rsqrt(_lanemean(o * o) + EPS)
        h1 = x_ref[...] + (o * r3) * pg_ref[...]
        h1b = h1.astype(BF16)
        h1_ref[...] = h1b
        gt = jax.nn.sigmoid(_dot(h1b, wpg_ref[...]))
        gt_ref[...] = gt
        pb = p_ref[...].astype(BF16)
        for k in range(N_CHIPS):
            cols = slice(k * W_PE_COLS, (k + 1) * W_PE_COLS)
            pe = _dot(pb, wpe_ref[k])
            d = h1[:, cols] + pe * gt[:, cols] - tgt_ref[:, cols]
            dout_ref[:, cols] = d * (1.0 / D_MODEL)
            loss_ref[...] += jnp.sum(d * d) * (0.5 / D_MODEL)

    row = lambda n: pl.BlockSpec((tm, n), lambda i: (i, 0))
    const = lambda shp: pl.BlockSpec(shp, lambda i, n=len(shp): (0,) * n, pipeline_mode=pl.Buffered(1))
    return pl.pallas_call(
        body, name="outproj_fwd", grid=(t // tm,),
        in_specs=[row(D_MODEL), row(D_MODEL), row(D_PLE), row(D_MODEL), const((1, D_MODEL)),
                  const((D_MODEL, D_MODEL)), const((D_MODEL, D_MODEL)), const((N_CHIPS, D_PLE, W_PE_COLS))],
        out_specs=[row(D_MODEL), row(D_MODEL), row(D_MODEL), row(D_MODEL),
                   pl.BlockSpec((SUBLANES, LANES), lambda i: (0, 0))],
        out_shape=[jax.ShapeDtypeStruct((t, D_MODEL), F32), jax.ShapeDtypeStruct((t, D_MODEL), BF16),
                   jax.ShapeDtypeStruct((t, D_MODEL), F32), jax.ShapeDtypeStruct((t, D_MODEL), F32),
                   jax.ShapeDtypeStruct((SUBLANES, LANES), F32)],
        compiler_params=_params("arbitrary"),
    )(x, y, p, tgt, post_g, w_out, w_pg, wg_pe)


def _head_bwd(dout, gt, p, o, post_g, w_out, w_pg, wg_pe, tm):
    t = dout.shape[0]

    def body(dout_ref, gt_ref, p_ref, o_ref, pg_ref, wo_ref, wpg_ref, wpe_ref,
             gwpe_ref, dq_ref, dh1_ref, do_ref, dy_ref, gpost_ref):
        i = pl.program_id(0)

        @pl.when(i == 0)
        def _():
            gpost_ref[...] = jnp.zeros_like(gpost_ref)
            gwpe_ref[...] = jnp.zeros_like(gwpe_ref)

        dout = dout_ref[...]
        gt = gt_ref[...]
        pb = p_ref[...].astype(BF16)
        for k in range(N_CHIPS):
            cols = slice(k * W_PE_COLS, (k + 1) * W_PE_COLS)
            pe = _dot(pb, wpe_ref[k])
            g = gt[:, cols]
            dg = dout[:, cols] * g
            gwpe_ref[k] += _dot_tn(pb, dg.astype(BF16))
            dq_ref[:, cols] = (dg * pe * (1.0 - g)).astype(BF16)
        dh1 = dout + _dot_nt(dq_ref[...], wpg_ref[...])
        dh1_ref[...] = dh1
        o = o_ref[...]
        r3 = lax.rsqrt(_lanemean(o * o) + EPS)
        on = o * r3
        gpost_ref[...] += _rowsum8(dh1 * on)
        don = dh1 * pg_ref[...]
        do = r3 * (don - on * _lanemean(don * on))
        dob = do.astype(BF16)
        do_ref[...] = dob
        dy_ref[...] = _dot_nt(dob, wo_ref[...])

        @pl.when(i == pl.num_programs(0) - 1)
        def _():
            gpost_ref[...] = jnp.broadcast_to(jnp.sum(gpost_ref[...], axis=0, keepdims=True), gpost_ref.shape)

    row = lambda n: pl.BlockSpec((tm, n), lambda i: (i, 0))
    const = lambda shp: pl.BlockSpec(shp, lambda i, n=len(shp): (0,) * n, pipeline_mode=pl.Buffered(1))
    return pl.pallas_call(
        body, name="head_bwd", grid=(t // tm,),
        in_specs=[row(D_MODEL), row(D_MODEL), row(D_PLE), row(D_MODEL), const((1, D_MODEL)),
                  const((D_MODEL, D_MODEL)), const((D_MODEL, D_MODEL)), const((N_CHIPS, D_PLE, W_PE_COLS))],
        out_specs=[pl.BlockSpec((N_CHIPS, D_PLE, W_PE_COLS), lambda i: (0, 0, 0)),
                   row(D_MODEL), row(D_MODEL), row(D_MODEL), row(D_MODEL),
                   pl.BlockSpec((SUBLANES, D_MODEL), lambda i: (0, 0))],
        out_shape=[jax.ShapeDtypeStruct((N_CHIPS, D_PLE, W_PE_COLS), F32), jax.ShapeDtypeStruct((t, D_MODEL), BF16),
                   jax.ShapeDtypeStruct((t, D_MODEL), F32), jax.ShapeDtypeStruct((t, D_MODEL), BF16),
                   jax.ShapeDtypeStruct((t, D_MODEL), F32), jax.ShapeDtypeStruct((SUBLANES, D_MODEL), F32)],
        compiler_params=_params("arbitrary"),
    )(dout, gt, p, o, post_g, w_out, w_pg, wg_pe)


def _branches_bwd(z, h, dy, prm, tm):
    t = z.shape[0]
    nt = t // tm
    hb = tm // SUBLANES

    def body(u_ref, v_ref, ga_ref, xb_ref, gb_ref, xbh_ref, h_ref, hh_ref, dy_ref,
             lng_ref, lnb_ref, wt_ref, wtt_ref, bsx_ref, cw_ref, cb_ref, wa_ref, wx_ref, ba_ref, bx_ref, lam_ref,
             oga_ref, ogb_ref,
             dz_ref, g_oga, g_ogb, g_lng, g_lnb, g_bsx, g_ws, g_cw, g_cb, g_wa, g_ba, g_wx, g_bx, g_lam,
             vn_s, mixed_s, dm_s, dvn_s, xcbf_s, r_s, i_s, a_s, b_s, dh_s, dpr_s, dpi_s, dxc_s,
             ca_s, cd_s, cx_s):
        step_i = pl.program_id(0)
        tile = nt - 1 - step_i
        accs = (g_oga, g_ogb, g_lng, g_lnb, g_bsx, g_ws, g_cw, g_cb, g_wa, g_ba, g_wx, g_bx, g_lam)

        @pl.when(step_i == 0)
        def _():
            for r in accs + (ca_s, cd_s, cx_s):
                r[...] = jnp.zeros_like(r)

        dy_a = dy_ref[:, 0:D_HALF]
        dy_b = dy_ref[:, D_HALF:]

        u = u_ref[...]
        ug, tu = _gelu(u)
        v = v_ref[...]
        vg, tv = _gelu(v)
        vhat, rstd = _layernorm_parts(vg)
        vn_s[...] = (vhat * lng_ref[...] + lnb_ref[...]).astype(BF16)
        _spatial_mix(wt_ref, vn_s, bsx_ref, mixed_s, tm)
        mixed = mixed_s[...]
        ga = ga_ref[...]
        sga = jax.nn.sigmoid(ga)
        sa = ga * sga
        um = ug * mixed
        ya = um * sa
        ra = lax.rsqrt(_lanemean(ya * ya) + EPS)
        yahat = ya * ra
        g_oga[...] += _rowsum8(dy_a * yahat)
        dn = dy_a * oga_ref[...]
        dya = ra * (dn - yahat * _lanemean(dn * yahat))
        dz_ref[:, 2 * D_HALF:3 * D_HALF] = (dya * um * (sga * (1.0 + ga * (1.0 - sga)))).astype(BF16)
        dz_ref[:, 0:D_HALF] = (dya * mixed * sa * _gelu_grad(u, tu)).astype(BF16)
        dmixed = dya * ug * sa
        g_bsx[...] += jnp.sum(dmixed.reshape(tm // CHUNK, CHUNK, D_HALF), axis=0)
        dm_s[...] = dmixed.astype(BF16)
        for c in range(tm // CHUNK):
            rows = slice(c * CHUNK, (c + 1) * CHUNK)
            for hd in range(N_HEADS):
                cols = slice(hd * CHUNK, (hd + 1) * CHUNK)
                dmh = dm_s[rows, cols]
                dvn_s[rows, cols] = _dot(wtt_ref[hd], dmh)
                g_ws[hd] += _dot_nt(dmh, vn_s[rows, cols])
        dvn = dvn_s[...]
        g_lng[...] += _rowsum8(dvn * vhat)
        g_lnb[...] += _rowsum8(dvn)
        dvh = dvn * lng_ref[...]
        dvg = rstd * (dvh - _lanemean(dvh) - vhat * _lanemean(dvh * vhat))
        dz_ref[:, D_HALF:2 * D_HALF] = (dvg * _gelu_grad(v, tv)).astype(BF16)

        xb = xb_ref[...]
        halo = jnp.where(tile == 0, 0.0, xbh_ref[...])
        taps = _conv_taps(xb, halo)
        xc = cb_ref[...] + taps[0] * cw_ref[0:1, :]
        for k in range(1, CONV_W):
            xc = xc + taps[k] * cw_ref[k:k + 1, :]
        xcbf_s[...] = xc.astype(BF16)
        _lru_gates(xcbf_s, wa_ref, wx_ref, ba_ref, bx_ref, r_s, i_s)
        rg = r_s[...]
        ig = i_s[...]
        lam = lam_ref[...]
        a, mult_true = _decay_parts(rg, lam)
        row = lax.broadcasted_iota(jnp.int32, a.shape, 0)
        first = jnp.logical_and(tile == 0, row == 0)
        mult = jnp.where(first, 1.0, mult_true)
        hcur = h_ref[...]
        hprev = _shift_down(hcur, jnp.where(tile == 0, 0.0, hh_ref[...]), 1)
        gb = gb_ref[...]
        sgb = jax.nn.sigmoid(gb)
        sb = gb * sgb
        yb = hcur * sb
        rb = lax.rsqrt(_lanemean(yb * yb) + EPS)
        ybhat = yb * rb
        g_ogb[...] += _rowsum8(dy_b * ybhat)
        dn = dy_b * ogb_ref[...]
        dyb = rb * (dn - ybhat * _lanemean(dn * ybhat))
        dz_ref[:, 4 * D_HALF:5 * D_HALF] = (dyb * hcur * (sgb * (1.0 + gb * (1.0 - sgb)))).astype(BF16)

        an = _shift_up(a, ca_s[...], 1)
        bb = dyb * sb
        r8 = row & (SUBLANES - 1)
        for d in (1, 2, 4):
            a_sh = pltpu.roll(an, tm - d, 0)
            b_sh = pltpu.roll(bb, tm - d, 0)
            m = r8 + d < SUBLANES
            bb = jnp.where(m, an * b_sh + bb, bb)
            an = jnp.where(m, an * a_sh, an)
        a_s[...] = an
        b_s[...] = bb

        def step(g, carry):
            sl = pl.ds(pl.multiple_of((hb - 1 - g) * SUBLANES, SUBLANES), SUBLANES)
            dg = a_s[sl, :] * carry + b_s[sl, :]
            dh_s[sl, :] = dg
            return jnp.broadcast_to(dg[0:1, :], dg.shape)

        cd_s[...] = lax.fori_loop(0, hb, step, cd_s[...])
        ca_s[...] = jnp.broadcast_to(a[0:1, :], ca_s.shape)
        dh = dh_s[...]
        da = dh * hprev
        gx = ig * xc
        dla = da * a - jnp.where(first, 0.0, dh * gx * (a * a / mult_true))
        g_lam[...] += _rowsum8(dla * rg)
        dr = dla * (-LRU_C * _softplus_neg(lam))
        dpr = dr * rg * (1.0 - rg)
        dpi = (dh * mult * xc) * ig * (1.0 - ig)
        g_ba[...] += _rowsum8(dpr)
        g_bx[...] += _rowsum8(dpi)
        dpr_s[...] = dpr.astype(BF16)
        dpi_s[...] = dpi.astype(BF16)
        for hd in range(N_HEADS):
            cols = slice(hd * CHUNK, (hd + 1) * CHUNK)
            xh = xcbf_s[:, cols]
            dprh = dpr_s[:, cols]
            dpih = dpi_s[:, cols]
            g_wa[hd] += _dot_tn(xh, dprh)
            g_wx[hd] += _dot_tn(xh, dpih)
            dxc_s[:, cols] = _dot_nt(dprh, wa_ref[hd]) + _dot_nt(dpih, wx_ref[hd])
        dxc = dxc_s[...] + dh * mult * ig
        g_cb[...] += _rowsum8(dxc)
        for k in range(CONV_W):
            g_cw[k * SUBLANES:(k + 1) * SUBLANES, :] += _rowsum8(dxc * taps[k])
        nxt = cx_s[...]
        dxb = dxc * cw_ref[CONV_W - 1:CONV_W, :]
        for j in range(1, CONV_W):
            dxb = dxb + _shift_up(dxc, nxt, j) * cw_ref[CONV_W - 1 - j:CONV_W - j, :]
        dz_ref[:, 3 * D_HALF:4 * D_HALF] = dxb.astype(BF16)
        cx_s[...] = dxc[0:SUBLANES]

        @pl.when(step_i == nt - 1)
        def _():
            for r in (g_oga, g_ogb, g_lng, g_lnb, g_cb, g_ba, g_bx):
                r[...] = jnp.broadcast_to(jnp.sum(r[...], axis=0, keepdims=True), r.shape)
            lam_f = LRU_C * jax.nn.sigmoid(-lam_ref[...])
            g_lam[...] = jnp.broadcast_to(jnp.sum(g_lam[...], axis=0, keepdims=True) * lam_f, g_lam.shape)
            for k in range(CONV_W):
                blk = g_cw[k * SUBLANES:(k + 1) * SUBLANES, :]
                g_cw[k * SUBLANES:(k + 1) * SUBLANES, :] = jnp.broadcast_to(jnp.sum(blk, axis=0, keepdims=True), blk.shape)
            tri = (lax.broadcasted_iota(jnp.int32, (CHUNK, CHUNK), 0) >= lax.broadcasted_iota(jnp.int32, (CHUNK, CHUNK), 1))
            for hd in range(N_HEADS):
                cols = slice(hd * CHUNK, (hd + 1) * CHUNK)
                g_ws[hd] = jnp.where(tri, g_ws[hd], 0.0)
                blk = g_bsx[:, cols]
                g_bsx[:, cols] = jnp.broadcast_to(jnp.sum(blk, axis=1, keepdims=True), blk.shape)

    rev = lambda i: nt - 1 - i
    zspec = lambda g: pl.BlockSpec((tm, D_HALF), lambda i, g=g: (rev(i), g))
    halo = lambda col: pl.BlockSpec((SUBLANES, D_HALF), lambda i: (jnp.maximum(rev(i) * hb - 1, 0), col))
    full = lambda a: pl.BlockSpec(a.shape, lambda i, n=a.ndim: (0,) * n)
    acc = lambda shp: pl.BlockSpec(shp, lambda i, n=len(shp): (0,) * n)
    names = ("ln_g", "ln_b", "wt", "wtt", "bsx", "conv_w", "conv_b", "w_a", "w_x", "b_a", "b_x", "lam", "oga", "ogb")
    pr = [prm[n] for n in names]
    vec = (SUBLANES, D_HALF)
    mat = (N_HEADS, CHUNK, CHUNK)
    acc_shapes = [vec, vec, vec, vec, (CHUNK, D_HALF), mat, (CONV_W * SUBLANES, D_HALF), vec, mat, vec, mat, vec, vec]
    big = lambda dt: pltpu.VMEM((tm, D_HALF), dt)
    return pl.pallas_call(
        body, name="branches_bwd", grid=(nt,),
        in_specs=[zspec(0), zspec(1), zspec(2), zspec(3), zspec(4), halo(3),
                  pl.BlockSpec((tm, D_HALF), lambda i: (rev(i), 0)), halo(0),
                  pl.BlockSpec((tm, D_MODEL), lambda i: (rev(i), 0))] + [full(a) for a in pr],
        out_specs=[pl.BlockSpec((tm, D_Z), lambda i: (rev(i), 0))] + [acc(s) for s in acc_shapes],
        out_shape=[jax.ShapeDtypeStruct((t, D_Z), BF16)] + [jax.ShapeDtypeStruct(s, F32) for s in acc_shapes],
        scratch_shapes=[big(BF16), big(F32), big(BF16), big(F32), big(BF16), big(F32), big(F32), big(F32), big(F32),
                        big(F32), big(BF16), big(BF16), big(F32),
                        pltpu.VMEM(vec, F32), pltpu.VMEM(vec, F32), pltpu.VMEM(vec, F32)],
        compiler_params=_params("arbitrary"),
    )(z, z, z, z, z, z, h, h, dy, *pr)


def _inproj_bwd(dz, wg_in, x, dh1, pre_g, tm):
    t = x.shape[0]
    nt = t // tm

    def body(dz_ref, w_ref, x_ref, dh1_ref, g_ref, gx_ref, gpre_ref, acc_s):
        i = pl.program_id(0)
        k = pl.program_id(1)

        @pl.when(jnp.logical_and(i == 0, k == 0))
        def _():
            gpre_ref[...] = jnp.zeros_like(gpre_ref)

        part = _dot_nt(dz_ref[...], w_ref[...])

        @pl.when(k == 0)
        def _():
            acc_s[...] = part

        @pl.when(k > 0)
        def _():
            acc_s[...] += part

        @pl.when(k == N_CHIPS - 1)
        def _():
            for s in range(tm // CHUNK):
                rows = slice(s * CHUNK, (s + 1) * CHUNK)
                xv = x_ref[rows, :]
                r = lax.rsqrt(_lanemean(xv * xv) + EPS)
                xhat = xv * r
                dhn = acc_s[rows, :]
                gpre_ref[...] += _rowsum8(dhn * xhat)
                dxh = dhn * g_ref[...]
                gx_ref[rows, :] = dh1_ref[rows, :] + r * (dxh - xhat * _lanemean(dxh * xhat))

        @pl.when(jnp.logical_and(i == nt - 1, k == N_CHIPS - 1))
        def _():
            gpre_ref[...] = jnp.broadcast_to(jnp.sum(gpre_ref[...], axis=0, keepdims=True), gpre_ref.shape)

    return pl.pallas_call(
        body, name="inproj_bwd", grid=(nt, N_CHIPS),
        in_specs=[pl.BlockSpec((tm, W_IN_COLS), lambda i, k: (i, k)),
                  pl.BlockSpec((None, D_MODEL, W_IN_COLS), lambda i, k: (k, 0, 0)),
                  pl.BlockSpec((tm, D_MODEL), lambda i, k: (i, 0)),
                  pl.BlockSpec((tm, D_MODEL), lambda i, k: (i, 0)),
                  pl.BlockSpec((1, D_MODEL), lambda i, k: (0, 0))],
        out_specs=[pl.BlockSpec((tm, D_MODEL), lambda i, k: (i, 0)),
                   pl.BlockSpec((SUBLANES, D_MODEL), lambda i, k: (0, 0))],
        out_shape=[jax.ShapeDtypeStruct((t, D_MODEL), F32), jax.ShapeDtypeStruct((SUBLANES, D_MODEL), F32)],
        scratch_shapes=[pltpu.VMEM((tm, D_MODEL), F32)],
        compiler_params=_params("arbitrary", "arbitrary"),
    )(dz, wg_in, x, dh1, pre_g)


def _weight_grad(a, b, name, kb, nb, tk, tn, tt):
    t = a.shape[0]
    tt = min(tt, t)

    def body(a_ref, b_ref, o_ref):
        @pl.when(pl.program_id(2) == 0)
        def _():
            o_ref[...] = jnp.zeros_like(o_ref)

        o_ref[...] += _dot_tn(a_ref[...], b_ref[...])

    return pl.pallas_call(
        body, name=name, grid=(nb, kb, t // tt),
        in_specs=[pl.BlockSpec((tt, tk), lambda j, i, s: (s, i)), pl.BlockSpec((tt, tn), lambda j, i, s: (s, j))],
        out_specs=pl.BlockSpec((None, None, tk, tn), lambda j, i, s: (j, i, 0, 0)),
        out_shape=jax.ShapeDtypeStruct((nb, kb, tk, tn), F32),
        compiler_params=_params("parallel", "parallel", "arbitrary"),
    )(a, b)


def _place():
    x, y, c = lax.axis_index("x"), lax.axis_index("y"), lax.axis_index("c")
    return x, y, c


def _chip_of(x, y):
    return 2 * x + y


def _gather_weights(w_in, w_out, w_pg, w_pe, conv_w):
    halves = [(D_MODEL // 2, W_IN_COLS), (W_ROWS // 2, D_MODEL), (W_ROWS // 2, D_MODEL), (D_PLE // 2, W_PE_COLS)]

    def body(win_ref, wout_ref, wpg_ref, wpe_ref, cw_ref,
             gin_ref, gout_ref, gpg_ref, gpe_ref, gcw_ref,
             s0, s1, s2, s3, b0, b1, b2, b3, lsem, send_sems, recv_sems, cw_send, cw_recv):
        x, y, c = _place()
        me = _chip_of(x, y)
        sibling = (x, y, 1 - c)
        chips = [(1 - x, y), (x, 1 - y), (1 - x, 1 - y)]
        srcs = (win_ref, wout_ref, wpg_ref, wpe_ref)
        stage = (s0, s1, s2, s3)
        bf = (b0, b1, b2, b3)
        outs = (gin_ref, gout_ref, gpg_ref, gpe_ref)
        loads = []
        for n in range(4):
            rows = halves[n][0]
            cp = pltpu.make_async_copy(srcs[n].at[pl.ds(c * rows, rows), :], stage[n], lsem.at[n])
            cp.start()
            loads.append(cp)
        own_cw = pltpu.make_async_copy(cw_ref, gcw_ref.at[me], lsem.at[4])
        own_cw.start()
        for n in range(4):
            loads[n].wait()
            bf[n][...] = stage[n][...].astype(BF16)

        def copy(n, k, chip, to, src=None):
            dst = outs[n].at[chip, c]
            return pltpu.make_async_remote_copy(
                src_ref=dst if src is None else src, dst_ref=dst,
                send_sem=send_sems.at[n, k], recv_sem=recv_sems.at[n, k], device_id=to, device_id_type=MESH)

        def recv(n, k, chip, core):
            dst = outs[n].at[chip, core]
            return pltpu.make_async_remote_copy(
                src_ref=dst, dst_ref=dst, send_sem=send_sems.at[n, k], recv_sem=recv_sems.at[n, k],
                device_id=sibling, device_id_type=MESH)

        sends = []
        locals_ = []
        for n in range(4):
            lc = pltpu.make_async_copy(bf[n], outs[n].at[me, c], lsem.at[5 + n])
            lc.start()
            locals_.append(lc)
            first = [copy(n, 0, me, sibling, src=bf[n])]
            first += [copy(n, 1 + j, me, (*chip, c), src=bf[n]) for j, chip in enumerate(chips)]
            for cp in first:
                cp.start()
            sends += first
        cws = []
        for j, chip in enumerate(chips):
            cp = pltpu.make_async_remote_copy(
                src_ref=cw_ref, dst_ref=gcw_ref.at[me], send_sem=cw_send.at[j], recv_sem=cw_recv.at[j],
                device_id=(*chip, c), device_id_type=MESH)
            cp.start()
            cws.append(cp)
        for n in range(4):
            for j, chip in enumerate(chips):
                kj = _chip_of(*chip)
                recv(n, 1 + j, kj, c).wait_recv()
                fw = copy(n, 4 + j, kj, sibling)
                fw.start()
                sends.append(fw)
        for n in range(4):
            recv(n, 0, me, 1 - c).wait_recv()
            for j, chip in enumerate(chips):
                recv(n, 4 + j, _chip_of(*chip), 1 - c).wait_recv()
        for j, chip in enumerate(chips):
            pltpu.make_async_remote_copy(
                src_ref=cw_ref, dst_ref=gcw_ref.at[_chip_of(*chip)], send_sem=cw_send.at[j], recv_sem=cw_recv.at[j],
                device_id=(*chip, c), device_id_type=MESH).wait_recv()
        for cp in sends + cws:
            cp.wait_send()
        for lc in locals_:
            lc.wait()
        own_cw.wait()

    out_shape = [jax.ShapeDtypeStruct((N_CHIPS, 2) + hs, BF16) for hs in halves]
    out_shape.append(jax.ShapeDtypeStruct((N_CHIPS, CONV_W, CONV_COLS), F32))
    scratch = [pltpu.VMEM(hs, F32) for hs in halves] + [pltpu.VMEM(hs, BF16) for hs in halves]
    scratch += [pltpu.SemaphoreType.DMA((9,)), pltpu.SemaphoreType.DMA((4, 7)), pltpu.SemaphoreType.DMA((4, 7)),
                pltpu.SemaphoreType.DMA((3,)), pltpu.SemaphoreType.DMA((3,))]
    return pl.pallas_call(
        body, name="gather_weights", in_specs=[ANY] * 5, out_specs=[ANY] * 5, out_shape=out_shape,
        scratch_shapes=scratch, compiler_params=pltpu.CompilerParams(vmem_limit_bytes=VMEM_LIMIT),
    )(w_in, w_out, w_pg, w_pe, conv_w)


def _sibling_exchange(grads):
    n = len(grads)

    def body(*refs):
        g_refs, r_refs = refs[:n], refs[n:2 * n]
        send_sems, recv_sems = refs[2 * n:]
        x, y, c = _place()
        cps = []
        for b in range(n):
            cp = pltpu.make_async_remote_copy(
                src_ref=g_refs[b].at[:, 1 - c], dst_ref=r_refs[b], send_sem=send_sems.at[b], recv_sem=recv_sems.at[b],
                device_id=(x, y, 1 - c), device_id_type=MESH)
            cp.start()
            cps.append(cp)
        for cp in cps:
            cp.wait()

    out_shape = [jax.ShapeDtypeStruct((g.shape[0],) + g.shape[2:], g.dtype) for g in grads]
    return pl.pallas_call(
        body, name="sibling_exchange", in_specs=[ANY] * n, out_specs=[ANY] * n, out_shape=out_shape,
        scratch_shapes=[pltpu.SemaphoreType.DMA((n,)), pltpu.SemaphoreType.DMA((n,))],
    )(*grads)


def _pair_sum(g, r1, kc, name, tr, send_dtype):
    nk, _, rows, cols = g.shape

    def body(kc_ref, g_ref, r_ref, p_ref, own_ref):
        s = g_ref[...] + r_ref[...]
        p_ref[...] = s.astype(send_dtype)

        @pl.when(pl.program_id(1) == kc_ref[0])
        def _():
            own_ref[...] = s

    grid_spec = pltpu.PrefetchScalarGridSpec(
        num_scalar_prefetch=1, grid=(rows // tr, nk),
        in_specs=[pl.BlockSpec((None, None, tr, cols), lambda r, k, kc: (k, kc[1], r, 0)),
                  pl.BlockSpec((None, tr, cols), lambda r, k, kc: (k, r, 0))],
        out_specs=[pl.BlockSpec((None, tr, cols), lambda r, k, kc: (k, r, 0)),
                   pl.BlockSpec((tr, cols), lambda r, k, kc: (r, 0))])
    return pl.pallas_call(
        body, name=name, grid_spec=grid_spec,
        out_shape=[jax.ShapeDtypeStruct((nk, rows, cols), send_dtype), jax.ShapeDtypeStruct((rows, cols), F32)],
        compiler_params=_params("arbitrary", "arbitrary"),
    )(kc, g, r1)


def _chip_exchange(pieces):
    n = len(pieces)

    def body(*refs):
        p_refs, r_refs = refs[:n], refs[n:2 * n]
        send_sems, recv_sems = refs[2 * n:]
        x, y, c = _place()
        chips = [(1 - x, y), (x, 1 - y), (1 - x, 1 - y)]
        cps = []
        for b in range(n):
            for j, chip in enumerate(chips):
                cp = pltpu.make_async_remote_copy(
                    src_ref=p_refs[b].at[_chip_of(*chip)], dst_ref=r_refs[b].at[j],
                    send_sem=send_sems.at[b, j], recv_sem=recv_sems.at[b, j],
                    device_id=(*chip, c), device_id_type=MESH)
                cp.start()
                cps.append(cp)
        for cp in cps:
            cp.wait()

    out_shape = [jax.ShapeDtypeStruct((3,) + p.shape[1:], p.dtype) for p in pieces]
    return pl.pallas_call(
        body, name="chip_exchange", in_specs=[ANY] * n, out_specs=[ANY] * n, out_shape=out_shape,
        scratch_shapes=[pltpu.SemaphoreType.DMA((n, 3)), pltpu.SemaphoreType.DMA((n, 3))],
    )(*pieces)


def _chip_sum(own, r2, slot, lead, name, tr):
    rows, cols = own.shape
    nl = len(lead)

    def body(slot_ref, o_ref, r_ref, s_ref):
        s = o_ref[...]
        for j in range(3):
            s = s + r_ref[j].astype(F32)
        s_ref[...] = s

    grid_spec = pltpu.PrefetchScalarGridSpec(
        num_scalar_prefetch=1, grid=(rows // tr,),
        in_specs=[pl.BlockSpec((tr, cols), lambda r, sl: (r, 0)), pl.BlockSpec((3, tr, cols), lambda r, sl: (0, r, 0))],
        out_specs=pl.BlockSpec((None,) * nl + (tr, cols), lambda r, sl: tuple(sl[q] for q in range(nl)) + (r, 0)))
    return pl.pallas_call(
        body, name=name, grid_spec=grid_spec, out_shape=jax.ShapeDtypeStruct(tuple(lead) + (rows, cols), F32),
        compiler_params=_params("arbitrary"),
    )(slot, own, r2)


def _finish_exchange(fulls, full_small):
    n = len(fulls)

    def body(*refs):
        f_refs, fsm_ref = refs[n + 1:2 * n + 1], refs[2 * n + 1]
        send_sems, recv_sems, sm_send, sm_recv = refs[2 * n + 2:]
        x, y, c = _place()
        me = _chip_of(x, y)
        cps = []
        for b in range(n):
            cp = pltpu.make_async_remote_copy(
                src_ref=f_refs[b].at[c], dst_ref=f_refs[b].at[c], send_sem=send_sems.at[b], recv_sem=recv_sems.at[b],
                device_id=(x, y, 1 - c), device_id_type=MESH)
            cp.start()
            cps.append(cp)
        flips = [(fx, fy, fc) for fx in (0, 1) for fy in (0, 1) for fc in (0, 1)][1:]
        for q, (fx, fy, fc) in enumerate(flips):
            cp = pltpu.make_async_remote_copy(
                src_ref=fsm_ref.at[me, c], dst_ref=fsm_ref.at[me, c], send_sem=sm_send.at[q], recv_sem=sm_recv.at[q],
                device_id=(x ^ fx, y ^ fy, c ^ fc), device_id_type=MESH)
            cp.start()
            cps.append(cp)
        for b in range(n):
            theirs = f_refs[b].at[1 - c]
            pltpu.make_async_remote_copy(
                src_ref=theirs, dst_ref=theirs, send_sem=send_sems.at[b], recv_sem=recv_sems.at[b],
                device_id=(x, y, 1 - c), device_id_type=MESH).wait_recv()
        for q, (fx, fy, fc) in enumerate(flips):
            theirs = fsm_ref.at[_chip_of(x ^ fx, y ^ fy), c ^ fc]
            pltpu.make_async_remote_copy(
                src_ref=theirs, dst_ref=theirs, send_sem=sm_send.at[q], recv_sem=sm_recv.at[q],
                device_id=(x ^ fx, y ^ fy, c ^ fc), device_id_type=MESH).wait_recv()
        for cp in cps:
            cp.wait_send()

    bufs = list(fulls) + [full_small]
    return pl.pallas_call(
        body, name="finish_exchange", in_specs=[ANY] * (n + 1), out_specs=[ANY] * (n + 1),
        out_shape=[jax.ShapeDtypeStruct(b.shape, b.dtype) for b in bufs],
        input_output_aliases={q: q for q in range(n + 1)},
        scratch_shapes=[pltpu.SemaphoreType.DMA((n,)), pltpu.SemaphoreType.DMA((n,)),
                        pltpu.SemaphoreType.DMA((7,)), pltpu.SemaphoreType.DMA((7,))],
    )(*bufs)


def _adamw(w, g, m, v, name, tr):
    rows, cols = w.shape

    def body(w_ref, g_ref, m_ref, v_ref, d_ref, nm_ref, nv_ref):
        gv = g_ref[...]
        nm = ADAM_B1 * m_ref[...] + (1.0 - ADAM_B1) * gv
        nv = ADAM_B2 * v_ref[...] + (1.0 - ADAM_B2) * (gv * gv)
        m_hat = nm / (1.0 - ADAM_B1 ** ADAM_STEP)
        v_hat = nv / (1.0 - ADAM_B2 ** ADAM_STEP)
        d_ref[...] = -ADAM_LR * (m_hat / (jnp.sqrt(v_hat) + ADAM_EPS) + ADAM_WD * w_ref[...])
        nm_ref[...] = nm
        nv_ref[...] = nv

    spec = pl.BlockSpec((tr, cols), lambda r: (r, 0))
    return pl.pallas_call(
        body, name=name, grid=(rows // tr,), in_specs=[spec] * 4, out_specs=[spec] * 3,
        out_shape=[jax.ShapeDtypeStruct((rows, cols), F32)] * 3,
        compiler_params=_params("parallel"),
    )(w, g, m, v)


def _rows128(a):
    return a.reshape(-1, LANES)


def _pack_small(parts):
    pieces = [_rows128(parts[n]) for n, _ in SMALL_ROWS]
    pieces.append(jnp.zeros((SMALL_TOTAL - SMALL_USED, LANES), F32))
    return jnp.concatenate(pieces, axis=0)


def _unpack_small(packed, shapes):
    out, at = {}, 0
    for n, r in SMALL_ROWS:
        out[n] = packed[at:at + r].reshape(shapes[n])
        at += r
    return out


def kernel(x, p, pre_g, w_in, gmlp_ln_g, gmlp_ln_b, gmlp_ws, gmlp_bs, conv_w, conv_b, w_a, b_a, w_x, b_x, lam, gmlp_out_g, lru_out_g, w_out, post_g, w_pe, w_pg, loss_target, m_pre_g, m_w_in, m_gmlp_ln_g, m_gmlp_ln_b, m_gmlp_ws, m_gmlp_bs, m_conv_w, m_conv_b, m_w_a, m_b_a, m_w_x, m_b_x, m_lam, m_gmlp_out_g, m_lru_out_g, m_w_out, m_post_g, m_w_pe, m_w_pg, v_pre_g, v_w_in, v_gmlp_ln_g, v_gmlp_ln_b, v_gmlp_ws, v_gmlp_bs, v_conv_w, v_conv_b, v_w_a, v_b_a, v_w_x, v_b_x, v_lam, v_gmlp_out_g, v_lru_out_g, v_w_out, v_post_g, v_w_pe, v_w_pg):
    weights = dict(pre_g=pre_g, w_in=w_in, gmlp_ln_g=gmlp_ln_g, gmlp_ln_b=gmlp_ln_b, gmlp_ws=gmlp_ws, gmlp_bs=gmlp_bs,
                   conv_w=conv_w, conv_b=conv_b, w_a=w_a, b_a=b_a, w_x=w_x, b_x=b_x, lam=lam, gmlp_out_g=gmlp_out_g,
                   lru_out_g=lru_out_g, w_out=w_out, post_g=post_g, w_pe=w_pe, w_pg=w_pg)
    mom_m = dict(pre_g=m_pre_g, w_in=m_w_in, gmlp_ln_g=m_gmlp_ln_g, gmlp_ln_b=m_gmlp_ln_b, gmlp_ws=m_gmlp_ws,
                 gmlp_bs=m_gmlp_bs, conv_w=m_conv_w, conv_b=m_conv_b, w_a=m_w_a, b_a=m_b_a, w_x=m_w_x, b_x=m_b_x,
                 lam=m_lam, gmlp_out_g=m_gmlp_out_g, lru_out_g=m_lru_out_g, w_out=m_w_out, post_g=m_post_g,
                 w_pe=m_w_pe, w_pg=m_w_pg)
    mom_v = dict(pre_g=v_pre_g, w_in=v_w_in, gmlp_ln_g=v_gmlp_ln_g, gmlp_ln_b=v_gmlp_ln_b, gmlp_ws=v_gmlp_ws,
                 gmlp_bs=v_gmlp_bs, conv_w=v_conv_w, conv_b=v_conv_b, w_a=v_w_a, b_a=v_b_a, w_x=v_w_x, b_x=v_b_x,
                 lam=v_lam, gmlp_out_g=v_gmlp_out_g, lru_out_g=v_lru_out_g, w_out=v_w_out, post_g=v_post_g,
                 w_pe=v_w_pe, w_pg=v_w_pg)
    order = list(weights)
    xi, yi, ci = _place()
    me = _chip_of(xi, yi)
    kc = jnp.stack([me, ci]).astype(jnp.int32)

    x2 = x[0]
    p2 = p[0, 0]
    tgt = loss_target[0]

    g_in, g_out, g_pg, g_pe, g_cw = _gather_weights(w_in[0], w_out[0], w_pg[0], w_pe[0], conv_w[0, :, 0, :])
    wg_in = g_in.reshape(N_CHIPS, D_MODEL, W_IN_COLS)
    wg_out = g_out.reshape(D_MODEL, D_MODEL)
    wg_pg = g_pg.reshape(D_MODEL, D_MODEL)
    wg_pe = g_pe.reshape(N_CHIPS, D_PLE, W_PE_COLS)
    cw_full = jnp.transpose(g_cw, (1, 0, 2)).reshape(CONV_W, D_HALF)

    causal = jnp.tril(jnp.ones((CHUNK, CHUNK), dtype=bool))
    ws_m = jnp.where(causal[None], gmlp_ws[0], 0.0)
    prm = dict(
        ln_g=gmlp_ln_g, ln_b=gmlp_ln_b, wt=ws_m.astype(BF16), wtt=jnp.transpose(ws_m, (0, 2, 1)).astype(BF16),
        bsx=jnp.repeat(jnp.transpose(gmlp_bs[0]), CHUNK, axis=1),
        conv_w=cw_full, conv_b=conv_b, w_a=w_a[0].astype(BF16), w_x=w_x[0].astype(BF16),
        b_a=b_a[0].reshape(1, D_HALF), b_x=b_x[0].reshape(1, D_HALF), lam=lam, oga=gmlp_out_g, ogb=lru_out_g)

    z, hn = _inproj_fwd(x2, pre_g, wg_in, 512)
    y, h = _branches_fwd(z, prm, 256)
    o, h1, gt, dout, loss_acc = _outproj_fwd(x2, y, p2, tgt, post_g, wg_out, wg_pg, wg_pe, 256)
    loss = lax.psum(loss_acc[0, 0], ("x", "y", "c"))

    gw_pe, dq, dh1, do, dy, g_post = _head_bwd(dout, gt, p2, o, post_g, wg_out, wg_pg, wg_pe, 256)
    gw_pe = gw_pe.reshape(N_CHIPS, 2, D_PLE // 2, W_PE_COLS)
    (dz, g_oga, g_ogb, g_lng, g_lnb, g_bsx, g_ws, g_cw, g_cb, g_wa, g_ba, g_wx, g_bx, g_lam) = _branches_bwd(
        z, h, dy, prm, 256)
    grad_x, g_pre = _inproj_bwd(dz, wg_in, x2, dh1, pre_g, 512)
    gw_in = _weight_grad(hn, dz, "grad_w_in", 2, N_CHIPS, D_MODEL // 2, W_IN_COLS, 1024)
    gw_out = _weight_grad(y, do, "grad_w_out", 2, 1, D_MODEL // 2, D_MODEL, 1024)
    gw_pg = _weight_grad(h1, dq, "grad_w_pg", 2, 1, D_MODEL // 2, D_MODEL, 1024)
    gw_out = gw_out.reshape(N_CHIPS, 2, W_ROWS // 2, D_MODEL)
    gw_pg = gw_pg.reshape(N_CHIPS, 2, W_ROWS // 2, D_MODEL)

    small_g = dict(
        pre_g=g_pre[0:1], gmlp_ln_g=g_lng[0:1], gmlp_ln_b=g_lnb[0:1], gmlp_ws=g_ws,
        gmlp_bs=jnp.transpose(g_bsx[:, ::CHUNK]), conv_w=g_cw[::SUBLANES], conv_b=g_cb[0:1], w_a=g_wa, b_a=g_ba[0:1],
        w_x=g_wx, b_x=g_bx[0:1], lam=g_lam[0:1], gmlp_out_g=g_oga[0:1], lru_out_g=g_ogb[0:1], post_g=g_post[0:1])
    gsm = _pack_small(small_g).reshape(N_CHIPS, 2, SMALL_PIECE, LANES)

    bufs = [gw_in, gw_out, gw_pg, gw_pe, gsm]
    r1 = _sibling_exchange(bufs)
    tiles = [256, 128, 128, 128, SMALL_PIECE]
    names = ["w_in", "w_out", "w_pg", "w_pe", "small"]
    pairs = [_pair_sum(bufs[b], r1[b], kc, "pair_sum_" + names[b], tiles[b], BF16 if b < 4 else F32) for b in range(5)]
    r2 = _chip_exchange([pr[0] for pr in pairs])
    sums = [_chip_sum(pairs[b][1], r2[b], kc[1:] if b < 4 else kc, (2,) if b < 4 else (N_CHIPS, 2),
                      "chip_sum_" + names[b], tiles[b]) for b in range(5)]
    f_in, f_out, f_pg, f_pe, f_sm = _finish_exchange(sums[:4], sums[4])

    big_g = dict(w_in=f_in.reshape(D_MODEL, W_IN_COLS), w_out=f_out.reshape(W_ROWS, D_MODEL),
                 w_pg=f_pg.reshape(W_ROWS, D_MODEL), w_pe=f_pe.reshape(D_PLE, W_PE_COLS))
    grads, deltas, new_m, new_v = {}, {}, {}, {}
    for n, tr in (("w_in", 256), ("w_out", 128), ("w_pg", 128), ("w_pe", 128)):
        shp = weights[n].shape
        grads[n] = big_g[n].reshape(shp)
        d, nm, nv = _adamw(weights[n][0], big_g[n], mom_m[n][0], mom_v[n][0], "adamw_" + n, tr)
        deltas[n], new_m[n], new_v[n] = d.reshape(shp), nm.reshape(shp), nv.reshape(shp)

    packed_g = f_sm.reshape(SMALL_TOTAL, LANES)
    small_names = [n for n, _ in SMALL_ROWS]
    shapes = {n: weights[n].shape for n in small_names}
    shapes["conv_w"] = (CONV_W, D_HALF)
    zero_cw = jnp.zeros((CONV_W, D_HALF), F32)
    pack_w = lambda src: _pack_small({n: (zero_cw if n == "conv_w" else src[n]) for n in small_names})
    d_sm, m_sm, v_sm = _adamw(pack_w(weights), packed_g, pack_w(mom_m), pack_w(mom_v), "adamw_small", SMALL_PIECE)
    ug, ud, um, uv = (_unpack_small(a, shapes) for a in (packed_g, d_sm, m_sm, v_sm))
    for n in small_names:
        if n != "conv_w":
            grads[n], deltas[n], new_m[n], new_v[n] = ug[n], ud[n], um[n], uv[n]
    g_conv = lax.dynamic_slice_in_dim(ug["conv_w"], me * CONV_COLS, CONV_COLS, axis=1)
    d, nm, nv = _adamw(conv_w[0, :, 0, :], g_conv, m_conv_w[0, :, 0, :], v_conv_w[0, :, 0, :], "adamw_conv_w", CONV_W)
    cshape = conv_w.shape
    grads["conv_w"], deltas["conv_w"] = g_conv.reshape(cshape), d.reshape(cshape)
    new_m["conv_w"], new_v["conv_w"] = nm.reshape(cshape), nv.reshape(cshape)

    return (loss, grad_x.reshape(x.shape), *[grads[n] for n in order], *[deltas[n] for n in order],
            *[new_m[n] for n in order], *[new_v[n] for n in order])
```

```python
import functools
import math

import jax
import jax.numpy as jnp
from jax import lax
from jax.experimental import pallas as pl
from jax.experimental.pallas import tpu as pltpu

F32 = jnp.float32
BF16 = jnp.bfloat16

D_MODEL = 2048
D_HALF = 1024
D_Z = 5120
D_PLE = 256
CHUNK = 128
N_HEADS = 8
N_CHIPS = 4
W_IN_COLS = D_Z // N_CHIPS
W_ROWS = D_MODEL // N_CHIPS
W_PE_COLS = D_MODEL // N_CHIPS
CONV_W = 4
CONV_COLS = D_HALF // N_CHIPS
EPS = 1e-6
LRU_C = 8.0
ADAM_LR, ADAM_B1, ADAM_B2, ADAM_EPS, ADAM_WD, ADAM_STEP = 0.001, 0.9, 0.999, 1e-08, 0.01, 10

SUBLANES = 8
LANES = 128
VMEM_LIMIT = 56 * 1024 * 1024

SMALL_ROWS = (("gmlp_ln_g", 8), ("gmlp_ln_b", 8), ("gmlp_ws", 1024), ("gmlp_bs", 8),
              ("conv_w", 32), ("conv_b", 8), ("w_a", 1024), ("b_a", 8), ("w_x", 1024), ("b_x", 8),
              ("lam", 8), ("gmlp_out_g", 8), ("lru_out_g", 8), ("post_g", 16))
SMALL_USED = sum(r for _, r in SMALL_ROWS)
SMALL_PIECE = 400
SMALL_TOTAL = 8 * SMALL_PIECE

MESH = pl.DeviceIdType.MESH
ANY = pl.BlockSpec(memory_space=pl.ANY)

_GELU_C0 = math.sqrt(2.0 / math.pi)
_GELU_C1 = 0.044715


def _params(*sem):
    return pltpu.CompilerParams(dimension_semantics=sem, vmem_limit_bytes=VMEM_LIMIT)


def _dot(a, b):
    return jnp.dot(a, b, preferred_element_type=F32)


def _dot_nt(a, b):
    return lax.dot_general(a, b, (((1,), (1,)), ((), ())), preferred_element_type=F32)


def _dot_tn(a, b):
    return lax.dot_general(a, b, (((0,), (0,)), ((), ())), preferred_element_type=F32)


def _gelu(x):
    t = jnp.tanh(_GELU_C0 * (x + _GELU_C1 * (x * x * x)))
    return 0.5 * x * (1.0 + t), t


def _gelu_grad(x, t):
    return 0.5 * (1.0 + t) + 0.5 * x * (1.0 - t * t) * (_GELU_C0 * (1.0 + 3.0 * _GELU_C1 * x * x))


def _rowsum8(v):
    r, n = v.shape
    return jnp.sum(v.reshape(r // SUBLANES, SUBLANES, n), axis=0)


def _lanemean(v):
    return jnp.mean(v, axis=-1, keepdims=True)


def _shift_down(v, halo8, k):
    if k == 0:
        return v
    r = pltpu.roll(v, k, 0)
    hr = pltpu.roll(halo8, k, 0)
    row = lax.broadcasted_iota(jnp.int32, halo8.shape, 0)
    top = jnp.where(row < k, hr, r[0:SUBLANES])
    return jnp.concatenate([top, r[SUBLANES:]], axis=0)


def _shift_up(v, next8, k):
    if k == 0:
        return v
    n = v.shape[0]
    r = pltpu.roll(v, n - k, 0)
    nr = pltpu.roll(next8, SUBLANES - k, 0)
    row = lax.broadcasted_iota(jnp.int32, next8.shape, 0)
    bot = jnp.where(row >= SUBLANES - k, nr, r[n - SUBLANES:])
    return jnp.concatenate([r[:n - SUBLANES], bot], axis=0)


def _inproj_fwd(x, pre_g, wg_in, tm):
    t = x.shape[0]

    def body(x_ref, g_ref, w_ref, z_ref, hn_ref):
        @pl.when(pl.program_id(1) == 0)
        def _():
            xv = x_ref[...]
            r = lax.rsqrt(_lanemean(xv * xv) + EPS)
            hn_ref[...] = (xv * r * g_ref[...]).astype(BF16)

        z_ref[...] = _dot(hn_ref[...], w_ref[...])

    return pl.pallas_call(
        body, name="inproj_fwd", grid=(t // tm, N_CHIPS),
        in_specs=[pl.BlockSpec((tm, D_MODEL), lambda i, j: (i, 0)),
                  pl.BlockSpec((1, D_MODEL), lambda i, j: (0, 0)),
                  pl.BlockSpec((None, D_MODEL, W_IN_COLS), lambda i, j: (j, 0, 0))],
        out_specs=[pl.BlockSpec((tm, W_IN_COLS), lambda i, j: (i, j)),
                   pl.BlockSpec((tm, D_MODEL), lambda i, j: (i, 0))],
        out_shape=[jax.ShapeDtypeStruct((t, D_Z), F32), jax.ShapeDtypeStruct((t, D_MODEL), BF16)],
        compiler_params=_params("parallel", "arbitrary"),
    )(x, pre_g, wg_in)


def _layernorm_parts(vg):
    mu = _lanemean(vg)
    xc = vg - mu
    rstd = lax.rsqrt(_lanemean(xc * xc) + EPS)
    return xc * rstd, rstd


def _spatial_mix(wt_ref, vn_ref, bsx_ref, mixed_ref, tm):
    for c in range(tm // CHUNK):
        rows = slice(c * CHUNK, (c + 1) * CHUNK)
        for h in range(N_HEADS):
            cols = slice(h * CHUNK, (h + 1) * CHUNK)
            mixed_ref[rows, cols] = _dot(wt_ref[h], vn_ref[rows, cols]) + bsx_ref[:, cols]


def _conv_taps(xb, halo8):
    return [_shift_down(xb, halo8, CONV_W - 1 - k) for k in range(CONV_W)]


def _lru_gates(xc_bf_ref, wa_ref, wx_ref, ba_ref, bx_ref, r_ref, i_ref):
    for h in range(N_HEADS):
        cols = slice(h * CHUNK, (h + 1) * CHUNK)
        xh = xc_bf_ref[:, cols]
        r_ref[:, cols] = jax.nn.sigmoid(_dot(xh, wa_ref[h]) + ba_ref[:, cols])
        i_ref[:, cols] = jax.nn.sigmoid(_dot(xh, wx_ref[h]) + bx_ref[:, cols])


def _softplus_neg(lam):
    return jnp.maximum(-lam, 0.0) + jnp.log(1.0 + jnp.exp(-jnp.abs(lam)))


def _decay_parts(r, lam):
    la = (-LRU_C * _softplus_neg(lam)) * r
    a = jnp.exp(la)
    th = -jnp.tanh(la)
    mult = jnp.sqrt(2.0 * th / (1.0 + th))
    return a, mult


def _branches_fwd(z, prm, tm):
    t = z.shape[0]
    nt = t // tm
    hb = tm // SUBLANES

    def body(u_ref, v_ref, ga_ref, xb_ref, gb_ref, xbh_ref,
             lng_ref, lnb_ref, wt_ref, bsx_ref, cw_ref, cb_ref, wa_ref, wx_ref, ba_ref, bx_ref, lam_ref,
             oga_ref, ogb_ref,
             y_ref, h_ref,
             vn_s, mixed_s, xcbf_s, r_s, i_s, a_s, b_s, carry_s):
        i = pl.program_id(0)

        @pl.when(i == 0)
        def _():
            carry_s[...] = jnp.zeros_like(carry_s)

        ug, _ = _gelu(u_ref[...])
        vg, _ = _gelu(v_ref[...])
        vhat, _ = _layernorm_parts(vg)
        vn_s[...] = (vhat * lng_ref[...] + lnb_ref[...]).astype(BF16)
        _spatial_mix(wt_ref, vn_s, bsx_ref, mixed_s, tm)
        ga = ga_ref[...]
        ya = ug * mixed_s[...] * (ga * jax.nn.sigmoid(ga))
        ra = lax.rsqrt(_lanemean(ya * ya) + EPS)
        y_ref[:, 0:D_HALF] = (ya * ra * oga_ref[...]).astype(BF16)

        xb = xb_ref[...]
        halo = jnp.where(i == 0, 0.0, xbh_ref[...])
        taps = _conv_taps(xb, halo)
        xc = cb_ref[...] + taps[0] * cw_ref[0:1, :]
        for k in range(1, CONV_W):
            xc = xc + taps[k] * cw_ref[k:k + 1, :]
        xcbf_s[...] = xc.astype(BF16)
        _lru_gates(xcbf_s, wa_ref, wx_ref, ba_ref, bx_ref, r_s, i_s)
        a, mult = _decay_parts(r_s[...], lam_ref[...])
        row = lax.broadcasted_iota(jnp.int32, a.shape, 0)
        mult = jnp.where(jnp.logical_and(i == 0, row == 0), 1.0, mult)
        b = mult * (i_s[...] * xc)
        r8 = row & (SUBLANES - 1)
        for d in (1, 2, 4):
            a_sh = pltpu.roll(a, d, 0)
            b_sh = pltpu.roll(b, d, 0)
            m = r8 >= d
            b = jnp.where(m, a * b_sh + b, b)
            a = jnp.where(m, a * a_sh, a)
        a_s[...] = a
        b_s[...] = b

        def step(g, carry):
            sl = pl.ds(pl.multiple_of(g * SUBLANES, SUBLANES), SUBLANES)
            hg = a_s[sl, :] * carry + b_s[sl, :]
            h_ref[sl, :] = hg
            return jnp.broadcast_to(hg[SUBLANES - 1:SUBLANES, :], hg.shape)

        carry_s[...] = lax.fori_loop(0, hb, step, carry_s[...])
        gb = gb_ref[...]
        yb = h_ref[...] * (gb * jax.nn.sigmoid(gb))
        rb = lax.rsqrt(_lanemean(yb * yb) + EPS)
        y_ref[:, D_HALF:] = (yb * rb * ogb_ref[...]).astype(BF16)

    zspec = lambda g: pl.BlockSpec((tm, D_HALF), lambda i, g=g: (i, g))
    full = lambda a: pl.BlockSpec(a.shape, lambda i, n=a.ndim: (0,) * n)
    names = ("ln_g", "ln_b", "wt", "bsx", "conv_w", "conv_b", "w_a", "w_x", "b_a", "b_x", "lam", "oga", "ogb")
    pr = [prm[n] for n in names]
    return pl.pallas_call(
        body, name="branches_fwd", grid=(nt,),
        in_specs=[zspec(0), zspec(1), zspec(2), zspec(3), zspec(4),
                  pl.BlockSpec((SUBLANES, D_HALF), lambda i: (jnp.maximum(i * hb - 1, 0), 3))]
                 + [full(a) for a in pr],
        out_specs=[pl.BlockSpec((tm, D_MODEL), lambda i: (i, 0)), pl.BlockSpec((tm, D_HALF), lambda i: (i, 0))],
        out_shape=[jax.ShapeDtypeStruct((t, D_MODEL), BF16), jax.ShapeDtypeStruct((t, D_HALF), F32)],
        scratch_shapes=[pltpu.VMEM((tm, D_HALF), BF16), pltpu.VMEM((tm, D_HALF), F32),
                        pltpu.VMEM((tm, D_HALF), BF16), pltpu.VMEM((tm, D_HALF), F32),
                        pltpu.VMEM((tm, D_HALF), F32), pltpu.VMEM((tm, D_HALF), F32),
                        pltpu.VMEM((tm, D_HALF), F32), pltpu.VMEM((SUBLANES, D_HALF), F32)],
        compiler_params=_params("arbitrary"),
    )(z, z, z, z, z, z, *pr)


def _outproj_fwd(x, y, p, tgt, post_g, w_out, w_pg, wg_pe, tm):
    t = x.shape[0]

    def body(x_ref, y_ref, p_ref, tgt_ref, pg_ref, wo_ref, wpg_ref, wpe_ref,
             o_ref, h1_ref, gt_ref, dout_ref, loss_ref):
        @pl.when(pl.program_id(0) == 0)
        def _():
            loss_ref[...] = jnp.zeros_like(loss_ref)

        o = _dot(y_ref[...], wo_ref[...])
        o_ref[...] = o
        r3 = lax.rsqrt(_lanemean(o * o) + EPS)
        h1 = x_ref[...] + (o * r3) * pg_ref[...]
        h1b = h1.astype(BF16)
        h1_ref[...] = h1b
        gt = jax.nn.sigmoid(_dot(h1b, wpg_ref[...]))
        gt_ref[...] = gt
        pb = p_ref[...].astype(BF16)
        for k in range(N_CHIPS):
            cols = slice(k * W_PE_COLS, (k + 1) * W_PE_COLS)
            pe = _dot(pb, wpe_ref[k])
            d = h1[:, cols] + pe * gt[:, cols] - tgt_ref[:, cols]
            dout_ref[:, cols] = d * (1.0 / D_MODEL)
            loss_ref[...] += jnp.sum(d * d) * (0.5 / D_MODEL)

    row = lambda n: pl.BlockSpec((tm, n), lambda i: (i, 0))
    const = lambda shp: pl.BlockSpec(shp, lambda i, n=len(shp): (0,) * n, pipeline_mode=pl.Buffered(1))
    return pl.pallas_call(
        body, name="outproj_fwd", grid=(t // tm,),
        in_specs=[row(D_MODEL), row(D_MODEL), row(D_PLE), row(D_MODEL), const((1, D_MODEL)),
                  const((D_MODEL, D_MODEL)), const((D_MODEL, D_MODEL)), const((N_CHIPS, D_PLE, W_PE_COLS))],
        out_specs=[row(D_MODEL), row(D_MODEL), row(D_MODEL), row(D_MODEL),
                   pl.BlockSpec((SUBLANES, LANES), lambda i: (0, 0))],
        out_shape=[jax.ShapeDtypeStruct((t, D_MODEL), F32), jax.ShapeDtypeStruct((t, D_MODEL), BF16),
                   jax.ShapeDtypeStruct((t, D_MODEL), F32), jax.ShapeDtypeStruct((t, D_MODEL), F32),
                   jax.ShapeDtypeStruct((SUBLANES, LANES), F32)],
        compiler_params=_params("arbitrary"),
    )(x, y, p, tgt, post_g, w_out, w_pg, wg_pe)


def _head_bwd(dout, gt, p, o, post_g, w_out, w_pg, wg_pe, tm):
    t = dout.shape[0]

    def body(dout_ref, gt_ref, p_ref, o_ref, pg_ref, wo_ref, wpg_ref, wpe_ref,
             gwpe_ref, dq_ref, dh1_ref, do_ref, dy_ref, gpost_ref):
        i = pl.program_id(0)

        @pl.when(i == 0)
        def _():
            gpost_ref[...] = jnp.zeros_like(gpost_ref)
            gwpe_ref[...] = jnp.zeros_like(gwpe_ref)

        dout = dout_ref[...]
        gt = gt_ref[...]
        pb = p_ref[...].astype(BF16)
        for k in range(N_CHIPS):
            cols = slice(k * W_PE_COLS, (k + 1) * W_PE_COLS)
            pe = _dot(pb, wpe_ref[k])
            g = gt[:, cols]
            dg = dout[:, cols] * g
            gwpe_ref[k] += _dot_tn(pb, dg.astype(BF16))
            dq_ref[:, cols] = (dg * pe * (1.0 - g)).astype(BF16)
        dh1 = dout + _dot_nt(dq_ref[...], wpg_ref[...])
        dh1_ref[...] = dh1
        o = o_ref[...]
        r3 = lax.rsqrt(_lanemean(o * o) + EPS)
        on = o * r3
        gpost_ref[...] += _rowsum8(dh1 * on)
        don = dh1 * pg_ref[...]
        do = r3 * (don - on * _lanemean(don * on))
        dob = do.astype(BF16)
        do_ref[...] = dob
        dy_ref[...] = _dot_nt(dob, wo_ref[...])

        @pl.when(i == pl.num_programs(0) - 1)
        def _():
            gpost_ref[...] = jnp.broadcast_to(jnp.sum(gpost_ref[...], axis=0, keepdims=True), gpost_ref.shape)

    row = lambda n: pl.BlockSpec((tm, n), lambda i: (i, 0))
    const = lambda shp: pl.BlockSpec(shp, lambda i, n=len(shp): (0,) * n, pipeline_mode=pl.Buffered(1))
    return pl.pallas_call(
        body, name="head_bwd", grid=(t // tm,),
        in_specs=[row(D_MODEL), row(D_MODEL), row(D_PLE), row(D_MODEL), const((1, D_MODEL)),
                  const((D_MODEL, D_MODEL)), const((D_MODEL, D_MODEL)), const((N_CHIPS, D_PLE, W_PE_COLS))],
        out_specs=[pl.BlockSpec((N_CHIPS, D_PLE, W_PE_COLS), lambda i: (0, 0, 0)),
                   row(D_MODEL), row(D_MODEL), row(D_MODEL), row(D_MODEL),
                   pl.BlockSpec((SUBLANES, D_MODEL), lambda i: (0, 0))],
        out_shape=[jax.ShapeDtypeStruct((N_CHIPS, D_PLE, W_PE_COLS), F32), jax.ShapeDtypeStruct((t, D_MODEL), BF16),
                   jax.ShapeDtypeStruct((t, D_MODEL), F32), jax.ShapeDtypeStruct((t, D_MODEL), BF16),
                   jax.ShapeDtypeStruct((t, D_MODEL), F32), jax.ShapeDtypeStruct((SUBLANES, D_MODEL), F32)],
        compiler_params=_params("arbitrary"),
    )(dout, gt, p, o, post_g, w_out, w_pg, wg_pe)


def _branches_bwd(z, h, dy, prm, tm, token):
    t = z.shape[0]
    nt = t // tm
    hb = tm // SUBLANES

    def body(u_ref, v_ref, ga_ref, xb_ref, gb_ref, xbh_ref, h_ref, hh_ref, dy_ref,
             lng_ref, lnb_ref, wt_ref, wtt_ref, bsx_ref, cw_ref, cb_ref, wa_ref, wx_ref, ba_ref, bx_ref, lam_ref,
             oga_ref, ogb_ref, token_ref,
             dz_ref, g_oga, g_ogb, g_lng, g_lnb, g_bsx, g_ws, g_cw, g_cb, g_wa, g_ba, g_wx, g_bx, g_lam,
             vn_s, mixed_s, dm_s, dvn_s, xcbf_s, r_s, i_s, a_s, b_s, dh_s, dpr_s, dpi_s, dxc_s,
             ca_s, cd_s, cx_s):
        step_i = pl.program_id(0)
        tile = nt - 1 - step_i
        accs = (g_oga, g_ogb, g_lng, g_lnb, g_bsx, g_ws, g_cw, g_cb, g_wa, g_ba, g_wx, g_bx, g_lam)

        @pl.when(step_i == 0)
        def _():
            for r in accs + (ca_s, cd_s, cx_s):
                r[...] = jnp.zeros_like(r)

        dy_a = dy_ref[:, 0:D_HALF]
        dy_b = dy_ref[:, D_HALF:]

        u = u_ref[...]
        ug, tu = _gelu(u)
        v = v_ref[...]
        vg, tv = _gelu(v)
        vhat, rstd = _layernorm_parts(vg)
        vn_s[...] = (vhat * lng_ref[...] + lnb_ref[...]).astype(BF16)
        _spatial_mix(wt_ref, vn_s, bsx_ref, mixed_s, tm)
        mixed = mixed_s[...]
        ga = ga_ref[...]
        sga = jax.nn.sigmoid(ga)
        sa = ga * sga
        um = ug * mixed
        ya = um * sa
        ra = lax.rsqrt(_lanemean(ya * ya) + EPS)
        yahat = ya * ra
        g_oga[...] += _rowsum8(dy_a * yahat)
        dn = dy_a * oga_ref[...]
        dya = ra * (dn - yahat * _lanemean(dn * yahat))
        dz_ref[:, 2 * D_HALF:3 * D_HALF] = (dya * um * (sga * (1.0 + ga * (1.0 - sga)))).astype(BF16)
        dz_ref[:, 0:D_HALF] = (dya * mixed * sa * _gelu_grad(u, tu)).astype(BF16)
        dmixed = dya * ug * sa
        g_bsx[...] += jnp.sum(dmixed.reshape(tm // CHUNK, CHUNK, D_HALF), axis=0)
        dm_s[...] = dmixed.astype(BF16)
        for c in range(tm // CHUNK):
            rows = slice(c * CHUNK, (c + 1) * CHUNK)
            for hd in range(N_HEADS):
                cols = slice(hd * CHUNK, (hd + 1) * CHUNK)
                dmh = dm_s[rows, cols]
                dvn_s[rows, cols] = _dot(wtt_ref[hd], dmh)
                g_ws[hd] += _dot_nt(dmh, vn_s[rows, cols])
        dvn = dvn_s[...]
        g_lng[...] += _rowsum8(dvn * vhat)
        g_lnb[...] += _rowsum8(dvn)
        dvh = dvn * lng_ref[...]
        dvg = rstd * (dvh - _lanemean(dvh) - vhat * _lanemean(dvh * vhat))
        dz_ref[:, D_HALF:2 * D_HALF] = (dvg * _gelu_grad(v, tv)).astype(BF16)

        xb = xb_ref[...]
        halo = jnp.where(tile == 0, 0.0, xbh_ref[...])
        taps = _conv_taps(xb, halo)
        xc = cb_ref[...] + taps[0] * cw_ref[0:1, :]
        for k in range(1, CONV_W):
            xc = xc + taps[k] * cw_ref[k:k + 1, :]
        xcbf_s[...] = xc.astype(BF16)
        _lru_gates(xcbf_s, wa_ref, wx_ref, ba_ref, bx_ref, r_s, i_s)
        rg = r_s[...]
        ig = i_s[...]
        lam = lam_ref[...]
        a, mult_true = _decay_parts(rg, lam)
        row = lax.broadcasted_iota(jnp.int32, a.shape, 0)
        first = jnp.logical_and(tile == 0, row == 0)
        mult = jnp.where(first, 1.0, mult_true)
        hcur = h_ref[...]
        hprev = _shift_down(hcur, jnp.where(tile == 0, 0.0, hh_ref[...]), 1)
        gb = gb_ref[...]
        sgb = jax.nn.sigmoid(gb)
        sb = gb * sgb
        yb = hcur * sb
        rb = lax.rsqrt(_lanemean(yb * yb) + EPS)
        ybhat = yb * rb
        g_ogb[...] += _rowsum8(dy_b * ybhat)
        dn = dy_b * ogb_ref[...]
        dyb = rb * (dn - ybhat * _lanemean(dn * ybhat))
        dz_ref[:, 4 * D_HALF:5 * D_HALF] = (dyb * hcur * (sgb * (1.0 + gb * (1.0 - sgb)))).astype(BF16)

        an = _shift_up(a, ca_s[...], 1)
        bb = dyb * sb
        r8 = row & (SUBLANES - 1)
        for d in (1, 2, 4):
            a_sh = pltpu.roll(an, tm - d, 0)
            b_sh = pltpu.roll(bb, tm - d, 0)
            m = r8 + d < SUBLANES
            bb = jnp.where(m, an * b_sh + bb, bb)
            an = jnp.where(m, an * a_sh, an)
        a_s[...] = an
        b_s[...] = bb

        def step(g, carry):
            sl = pl.ds(pl.multiple_of((hb - 1 - g) * SUBLANES, SUBLANES), SUBLANES)
            dg = a_s[sl, :] * carry + b_s[sl, :]
            dh_s[sl, :] = dg
            return jnp.broadcast_to(dg[0:1, :], dg.shape)

        cd_s[...] = lax.fori_loop(0, hb, step, cd_s[...])
        ca_s[...] = jnp.broadcast_to(a[0:1, :], ca_s.shape)
        dh = dh_s[...]
        da = dh * hprev
        gx = ig * xc
        dla = da * a - jnp.where(first, 0.0, dh * gx * (a * a / mult_true))
        g_lam[...] += _rowsum8(dla * rg)
        dr = dla * (-LRU_C * _softplus_neg(lam))
        dpr = dr * rg * (1.0 - rg)
        dpi = (dh * mult * xc) * ig * (1.0 - ig)
        g_ba[...] += _rowsum8(dpr)
        g_bx[...] += _rowsum8(dpi)
        dpr_s[...] = dpr.astype(BF16)
        dpi_s[...] = dpi.astype(BF16)
        for hd in range(N_HEADS):
            cols = slice(hd * CHUNK, (hd + 1) * CHUNK)
            xh = xcbf_s[:, cols]
            dprh = dpr_s[:, cols]
            dpih = dpi_s[:, cols]
            g_wa[hd] += _dot_tn(xh, dprh)
            g_wx[hd] += _dot_tn(xh, dpih)
            dxc_s[:, cols] = _dot_nt(dprh, wa_ref[hd]) + _dot_nt(dpih, wx_ref[hd])
        dxc = dxc_s[...] + dh * mult * ig
        g_cb[...] += _rowsum8(dxc)
        for k in range(CONV_W):
            g_cw[k * SUBLANES:(k + 1) * SUBLANES, :] += _rowsum8(dxc * taps[k])
        nxt = cx_s[...]
        dxb = dxc * cw_ref[CONV_W - 1:CONV_W, :]
        for j in range(1, CONV_W):
            dxb = dxb + _shift_up(dxc, nxt, j) * cw_ref[CONV_W - 1 - j:CONV_W - j, :]
        dz_ref[:, 3 * D_HALF:4 * D_HALF] = dxb.astype(BF16)
        cx_s[...] = dxc[0:SUBLANES]

        @pl.when(step_i == nt - 1)
        def _():
            for r in (g_oga, g_ogb, g_lng, g_lnb, g_cb, g_ba, g_bx):
                r[...] = jnp.broadcast_to(jnp.sum(r[...], axis=0, keepdims=True), r.shape)
            lam_f = LRU_C * jax.nn.sigmoid(-lam_ref[...])
            g_lam[...] = jnp.broadcast_to(jnp.sum(g_lam[...], axis=0, keepdims=True) * lam_f, g_lam.shape)
            for k in range(CONV_W):
                blk = g_cw[k * SUBLANES:(k + 1) * SUBLANES, :]
                g_cw[k * SUBLANES:(k + 1) * SUBLANES, :] = jnp.broadcast_to(jnp.sum(blk, axis=0, keepdims=True), blk.shape)
            tri = (lax.broadcasted_iota(jnp.int32, (CHUNK, CHUNK), 0) >= lax.broadcasted_iota(jnp.int32, (CHUNK, CHUNK), 1))
            for hd in range(N_HEADS):
                cols = slice(hd * CHUNK, (hd + 1) * CHUNK)
                g_ws[hd] = jnp.where(tri, g_ws[hd], 0.0)
                blk = g_bsx[:, cols]
                g_bsx[:, cols] = jnp.broadcast_to(jnp.sum(blk, axis=1, keepdims=True), blk.shape)

    rev = lambda i: nt - 1 - i
    zspec = lambda g: pl.BlockSpec((tm, D_HALF), lambda i, g=g: (rev(i), g))
    halo = lambda col: pl.BlockSpec((SUBLANES, D_HALF), lambda i: (jnp.maximum(rev(i) * hb - 1, 0), col))
    full = lambda a: pl.BlockSpec(a.shape, lambda i, n=a.ndim: (0,) * n)
    acc = lambda shp: pl.BlockSpec(shp, lambda i, n=len(shp): (0,) * n)
    names = ("ln_g", "ln_b", "wt", "wtt", "bsx", "conv_w", "conv_b", "w_a", "w_x", "b_a", "b_x", "lam", "oga", "ogb")
    pr = [prm[n] for n in names] + [token]
    vec = (SUBLANES, D_HALF)
    mat = (N_HEADS, CHUNK, CHUNK)
    acc_shapes = [vec, vec, vec, vec, (CHUNK, D_HALF), mat, (CONV_W * SUBLANES, D_HALF), vec, mat, vec, mat, vec, vec]
    big = lambda dt: pltpu.VMEM((tm, D_HALF), dt)
    return pl.pallas_call(
        body, name="branches_bwd", grid=(nt,),
        in_specs=[zspec(0), zspec(1), zspec(2), zspec(3), zspec(4), halo(3),
                  pl.BlockSpec((tm, D_HALF), lambda i: (rev(i), 0)), halo(0),
                  pl.BlockSpec((tm, D_MODEL), lambda i: (rev(i), 0))] + [full(a) for a in pr],
        out_specs=[pl.BlockSpec((tm, D_Z), lambda i: (rev(i), 0))] + [acc(s) for s in acc_shapes],
        out_shape=[jax.ShapeDtypeStruct((t, D_Z), BF16)] + [jax.ShapeDtypeStruct(s, F32) for s in acc_shapes],
        scratch_shapes=[big(BF16), big(F32), big(BF16), big(F32), big(BF16), big(F32), big(F32), big(F32), big(F32),
                        big(F32), big(BF16), big(BF16), big(F32),
                        pltpu.VMEM(vec, F32), pltpu.VMEM(vec, F32), pltpu.VMEM(vec, F32)],
        compiler_params=_params("arbitrary"),
    )(z, z, z, z, z, z, h, h, dy, *pr)


def _inproj_bwd(dz, wg_in, x, dh1, pre_g, tm, tile0, nt, prev, last, token, name):
    t = x.shape[0]

    def body(*refs):
        dz_ref, w_ref, x_ref, dh1_ref, g_ref = refs[:5]
        gx_ref, gpre_ref, acc_s = refs[-3:]
        i = pl.program_id(0)
        k = pl.program_id(1)

        @pl.when(jnp.logical_and(i == 0, k == 0))
        def _():
            gpre_ref[...] = jnp.zeros_like(gpre_ref) if prev is None else refs[7][...]

        part = _dot_nt(dz_ref[...], w_ref[...])

        @pl.when(k == 0)
        def _():
            acc_s[...] = part

        @pl.when(k > 0)
        def _():
            acc_s[...] += part

        @pl.when(k == N_CHIPS - 1)
        def _():
            for s in range(tm // CHUNK):
                rows = slice(s * CHUNK, (s + 1) * CHUNK)
                xv = x_ref[rows, :]
                r = lax.rsqrt(_lanemean(xv * xv) + EPS)
                xhat = xv * r
                dhn = acc_s[rows, :]
                gpre_ref[...] += _rowsum8(dhn * xhat)
                dxh = dhn * g_ref[...]
                gx_ref[rows, :] = dh1_ref[rows, :] + r * (dxh - xhat * _lanemean(dxh * xhat))

        if last:
            @pl.when(jnp.logical_and(i == nt - 1, k == N_CHIPS - 1))
            def _():
                gpre_ref[...] = jnp.broadcast_to(jnp.sum(gpre_ref[...], axis=0, keepdims=True), gpre_ref.shape)

    row = pl.BlockSpec((tm, D_MODEL), lambda i, k: (tile0 + i, 0))
    small = lambda r: pl.BlockSpec((r, D_MODEL), lambda i, k: (0, 0))
    tok = pl.BlockSpec((SUBLANES, LANES), lambda i, k: (0, 0))
    in_specs = [pl.BlockSpec((tm, W_IN_COLS), lambda i, k: (tile0 + i, k)),
                pl.BlockSpec((None, D_MODEL, W_IN_COLS), lambda i, k: (k, 0, 0)), row, row, small(1), tok]
    args = [dz, wg_in, x, dh1, pre_g, token]
    aliases = {}
    if prev is not None:
        in_specs += [ANY, small(SUBLANES)]
        args += list(prev)
        aliases = {6: 0}
    return pl.pallas_call(
        body, name=name, grid=(nt, N_CHIPS), in_specs=in_specs, out_specs=[row, small(SUBLANES)],
        out_shape=[jax.ShapeDtypeStruct((t, D_MODEL), F32), jax.ShapeDtypeStruct((SUBLANES, D_MODEL), F32)],
        input_output_aliases=aliases,
        scratch_shapes=[pltpu.VMEM((tm, D_MODEL), F32)],
        compiler_params=_params("arbitrary", "arbitrary"),
    )(*args)


def _weight_grad(a, b, name, kb, nb, tk, tn, tt, token):
    t = a.shape[0]
    tt = min(tt, t)

    def body(a_ref, b_ref, token_ref, o_ref):
        @pl.when(pl.program_id(2) == 0)
        def _():
            o_ref[...] = jnp.zeros_like(o_ref)

        o_ref[...] += _dot_tn(a_ref[...], b_ref[...])

    return pl.pallas_call(
        body, name=name, grid=(nb, kb, t // tt),
        in_specs=[pl.BlockSpec((tt, tk), lambda j, i, s: (s, i)), pl.BlockSpec((tt, tn), lambda j, i, s: (s, j)),
                  pl.BlockSpec((SUBLANES, LANES), lambda j, i, s: (0, 0))],
        out_specs=pl.BlockSpec((None, None, tk, tn), lambda j, i, s: (j, i, 0, 0)),
        out_shape=jax.ShapeDtypeStruct((nb, kb, tk, tn), F32),
        compiler_params=_params("parallel", "parallel", "arbitrary"),
    )(a, b, token)


def _place():
    x, y, c = lax.axis_index("x"), lax.axis_index("y"), lax.axis_index("c")
    return x, y, c


def _chip_of(x, y):
    return 2 * x + y


def _gather_weights(w_in, w_out, w_pg, w_pe, conv_w):
    halves = [(D_MODEL // 2, W_IN_COLS), (W_ROWS // 2, D_MODEL), (W_ROWS // 2, D_MODEL), (D_PLE // 2, W_PE_COLS)]

    def body(win_ref, wout_ref, wpg_ref, wpe_ref, cw_ref,
             gin_ref, gout_ref, gpg_ref, gpe_ref, gcw_ref,
             s0, s1, s2, s3, b0, b1, b2, b3, lsem, send_sems, recv_sems, cw_send, cw_recv):
        x, y, c = _place()
        me = _chip_of(x, y)
        sibling = (x, y, 1 - c)
        chips = [(1 - x, y), (x, 1 - y), (1 - x, 1 - y)]
        srcs = (win_ref, wout_ref, wpg_ref, wpe_ref)
        stage = (s0, s1, s2, s3)
        bf = (b0, b1, b2, b3)
        outs = (gin_ref, gout_ref, gpg_ref, gpe_ref)
        loads = []
        for n in range(4):
            rows = halves[n][0]
            cp = pltpu.make_async_copy(srcs[n].at[pl.ds(c * rows, rows), :], stage[n], lsem.at[n])
            cp.start()
            loads.append(cp)
        own_cw = pltpu.make_async_copy(cw_ref, gcw_ref.at[me], lsem.at[4])
        own_cw.start()
        for n in range(4):
            loads[n].wait()
            bf[n][...] = stage[n][...].astype(BF16)

        def copy(n, k, chip, to, src=None):
            dst = outs[n].at[chip, c]
            return pltpu.make_async_remote_copy(
                src_ref=dst if src is None else src, dst_ref=dst,
                send_sem=send_sems.at[n, k], recv_sem=recv_sems.at[n, k], device_id=to, device_id_type=MESH)

        def recv(n, k, chip, core):
            dst = outs[n].at[chip, core]
            return pltpu.make_async_remote_copy(
                src_ref=dst, dst_ref=dst, send_sem=send_sems.at[n, k], recv_sem=recv_sems.at[n, k],
                device_id=sibling, device_id_type=MESH)

        sends = []
        locals_ = []
        for n in range(4):
            lc = pltpu.make_async_copy(bf[n], outs[n].at[me, c], lsem.at[5 + n])
            lc.start()
            locals_.append(lc)
            first = [copy(n, 0, me, sibling, src=bf[n])]
            first += [copy(n, 1 + j, me, (*chip, c), src=bf[n]) for j, chip in enumerate(chips)]
            for cp in first:
                cp.start()
            sends += first
        cws = []
        for j, chip in enumerate(chips):
            cp = pltpu.make_async_remote_copy(
                src_ref=cw_ref, dst_ref=gcw_ref.at[me], send_sem=cw_send.at[j], recv_sem=cw_recv.at[j],
                device_id=(*chip, c), device_id_type=MESH)
            cp.start()
            cws.append(cp)
        for n in range(4):
            for j, chip in enumerate(chips):
                kj = _chip_of(*chip)
                recv(n, 1 + j, kj, c).wait_recv()
                fw = copy(n, 4 + j, kj, sibling)
                fw.start()
                sends.append(fw)
        for n in range(4):
            recv(n, 0, me, 1 - c).wait_recv()
            for j, chip in enumerate(chips):
                recv(n, 4 + j, _chip_of(*chip), 1 - c).wait_recv()
        for j, chip in enumerate(chips):
            pltpu.make_async_remote_copy(
                src_ref=cw_ref, dst_ref=gcw_ref.at[_chip_of(*chip)], send_sem=cw_send.at[j], recv_sem=cw_recv.at[j],
                device_id=(*chip, c), device_id_type=MESH).wait_recv()
        for cp in sends + cws:
            cp.wait_send()
        for lc in locals_:
            lc.wait()
        own_cw.wait()

    out_shape = [jax.ShapeDtypeStruct((N_CHIPS, 2) + hs, BF16) for hs in halves]
    out_shape.append(jax.ShapeDtypeStruct((N_CHIPS, CONV_W, CONV_COLS), F32))
    scratch = [pltpu.VMEM(hs, F32) for hs in halves] + [pltpu.VMEM(hs, BF16) for hs in halves]
    scratch += [pltpu.SemaphoreType.DMA((9,)), pltpu.SemaphoreType.DMA((4, 7)), pltpu.SemaphoreType.DMA((4, 7)),
                pltpu.SemaphoreType.DMA((3,)), pltpu.SemaphoreType.DMA((3,))]
    return pl.pallas_call(
        body, name="gather_weights", in_specs=[ANY] * 5, out_specs=[ANY] * 5, out_shape=out_shape,
        scratch_shapes=scratch, compiler_params=pltpu.CompilerParams(vmem_limit_bytes=VMEM_LIMIT),
    )(w_in, w_out, w_pg, w_pe, conv_w)


HBM = pl.BlockSpec(memory_space=pltpu.HBM)
SEM = pl.BlockSpec(memory_space=pltpu.SEMAPHORE)
EFFECT = pltpu.SideEffectType.DATAFLOW_SIDE_EFFECTING


def _hbm(a):
    return pltpu.with_memory_space_constraint(a, pltpu.HBM)


def _landing(shape, dtype):
    return _hbm(lax.empty(shape, dtype))


def _exchange_start(name, arrays, ncopies, build):
    n = len(arrays)

    def body(*refs):
        ins, send_sems, recv_sems, token = refs[:n], refs[n], refs[n + 1], refs[-1]
        for cp in build(ins, send_sems, recv_sems):
            cp.start()
        token[...] = jnp.zeros_like(token)

    outs = pl.pallas_call(
        body, name=name,
        out_shape=(pltpu.SemaphoreType.DMA((ncopies,)), pltpu.SemaphoreType.DMA((ncopies,)),
                   *[pltpu.HBM(a.shape, a.dtype) for a in arrays], jax.ShapeDtypeStruct((SUBLANES, LANES), F32)),
        in_specs=[HBM] * n, out_specs=(SEM, SEM, *[HBM] * n, pl.BlockSpec(memory_space=pltpu.VMEM)),
        input_output_aliases={q: q + 2 for q in range(n)},
        compiler_params=pltpu.CompilerParams(has_side_effects=EFFECT),
    )(*[_hbm(a) for a in arrays])
    return (outs[0], outs[1], list(outs[2:2 + n])), outs[-1]


def _exchange_wait(name, started, after, build):
    send, recv, arrays = started
    n = len(arrays)

    def body(*refs):
        ins, send_sems, recv_sems = refs[:n], refs[n], refs[n + 1]
        for cp in build(ins, send_sems, recv_sems):
            cp.wait_send()
            cp.wait_recv()

    return pl.pallas_call(
        body, name=name, out_shape=tuple(pltpu.HBM(a.shape, a.dtype) for a in arrays),
        in_specs=[HBM] * n + [SEM, SEM, ANY], out_specs=tuple([HBM] * n),
        input_output_aliases={q: q for q in range(n)},
        compiler_params=pltpu.CompilerParams(has_side_effects=EFFECT),
    )(*arrays, send, recv, after)


def _sibling_copies(n):
    def build(refs, send_sems, recv_sems):
        x, y, c = _place()
        return [pltpu.make_async_remote_copy(
            src_ref=refs[b].at[:, 1 - c], dst_ref=refs[n + b], send_sem=send_sems.at[b], recv_sem=recv_sems.at[b],
            device_id=(x, y, 1 - c), device_id_type=MESH) for b in range(n)]
    return build


def _chip_copies(n):
    def build(refs, send_sems, recv_sems):
        x, y, c = _place()
        chips = [(1 - x, y), (x, 1 - y), (1 - x, 1 - y)]
        return [pltpu.make_async_remote_copy(
            src_ref=refs[b].at[_chip_of(*chip)], dst_ref=refs[n + b].at[j],
            send_sem=send_sems.at[3 * b + j], recv_sem=recv_sems.at[3 * b + j],
            device_id=(*chip, c), device_id_type=MESH) for b in range(n) for j, chip in enumerate(chips)]
    return build


def _finish_copies(n, with_small):
    def build(refs, send_sems, recv_sems):
        x, y, c = _place()
        cps = [pltpu.make_async_remote_copy(
            src_ref=refs[b].at[c], dst_ref=refs[b].at[c], send_sem=send_sems.at[b], recv_sem=recv_sems.at[b],
            device_id=(x, y, 1 - c), device_id_type=MESH) for b in range(n)]
        if with_small:
            mine = refs[n].at[_chip_of(x, y), c]
            flips = [(fx, fy, fc) for fx in (0, 1) for fy in (0, 1) for fc in (0, 1)][1:]
            cps += [pltpu.make_async_remote_copy(
                src_ref=mine, dst_ref=mine, send_sem=send_sems.at[n + q], recv_sem=recv_sems.at[n + q],
                device_id=(x ^ fx, y ^ fy, c ^ fc), device_id_type=MESH) for q, (fx, fy, fc) in enumerate(flips)]
        return cps
    return build


def _pair_sum(g, r1, kc, name, tr, send_dtype):
    nk, _, rows, cols = g.shape

    def body(kc_ref, g_ref, r_ref, p_ref, own_ref):
        s = g_ref[...] + r_ref[...]
        p_ref[...] = s.astype(send_dtype)

        @pl.when(pl.program_id(1) == kc_ref[0])
        def _():
            own_ref[...] = s

    grid_spec = pltpu.PrefetchScalarGridSpec(
        num_scalar_prefetch=1, grid=(rows // tr, nk),
        in_specs=[pl.BlockSpec((None, None, tr, cols), lambda r, k, kc: (k, kc[1], r, 0)),
                  pl.BlockSpec((None, tr, cols), lambda r, k, kc: (k, r, 0))],
        out_specs=[pl.BlockSpec((None, tr, cols), lambda r, k, kc: (k, r, 0)),
                   pl.BlockSpec((tr, cols), lambda r, k, kc: (r, 0))])
    return pl.pallas_call(
        body, name=name, grid_spec=grid_spec,
        out_shape=[jax.ShapeDtypeStruct((nk, rows, cols), send_dtype), jax.ShapeDtypeStruct((rows, cols), F32)],
        compiler_params=_params("arbitrary", "arbitrary"),
    )(kc, g, r1)


def _allreduce_vector(v, token):
    rows = v.shape[0]

    def body(v_ref, token_ref, o_ref, all_s, send_sems, recv_sems):
        x, y, c = _place()
        me = 2 * _chip_of(x, y) + c
        all_s[me] = v_ref[...]
        flips = [(fx, fy, fc) for fx in (0, 1) for fy in (0, 1) for fc in (0, 1)][1:]
        cps = []
        for q, (fx, fy, fc) in enumerate(flips):
            cp = pltpu.make_async_remote_copy(
                src_ref=v_ref, dst_ref=all_s.at[me], send_sem=send_sems.at[q], recv_sem=recv_sems.at[q],
                device_id=(x ^ fx, y ^ fy, c ^ fc), device_id_type=MESH)
            cp.start()
            cps.append(cp)
        for cp in cps:
            cp.wait()
        s = all_s[0]
        for d in range(1, 8):
            s = s + all_s[d]
        o_ref[...] = s

    vm = pl.BlockSpec(memory_space=pltpu.VMEM)
    return pl.pallas_call(
        body, name="allreduce_vector", in_specs=[vm, vm], out_specs=vm, out_shape=jax.ShapeDtypeStruct(v.shape, F32),
        scratch_shapes=[pltpu.VMEM((8, rows, LANES), F32), pltpu.SemaphoreType.DMA((7,)), pltpu.SemaphoreType.DMA((7,))],
    )(v, token)


def _chip_sum(own, r2, slot, lead, name, tr):
    rows, cols = own.shape
    nl = len(lead)

    def body(slot_ref, o_ref, r_ref, s_ref):
        s = o_ref[...]
        for j in range(3):
            s = s + r_ref[j].astype(F32)
        s_ref[...] = s

    grid_spec = pltpu.PrefetchScalarGridSpec(
        num_scalar_prefetch=1, grid=(rows // tr,),
        in_specs=[pl.BlockSpec((tr, cols), lambda r, sl: (r, 0)), pl.BlockSpec((3, tr, cols), lambda r, sl: (0, r, 0))],
        out_specs=pl.BlockSpec((None,) * nl + (tr, cols), lambda r, sl: tuple(sl[q] for q in range(nl)) + (r, 0)))
    return pl.pallas_call(
        body, name=name, grid_spec=grid_spec, out_shape=jax.ShapeDtypeStruct(tuple(lead) + (rows, cols), F32),
        compiler_params=_params("arbitrary"),
    )(slot, own, r2)


def _adamw(w, g, m, v, name, tr):
    rows, cols = w.shape

    def body(w_ref, g_ref, m_ref, v_ref, d_ref, nm_ref, nv_ref):
        gv = g_ref[...]
        nm = ADAM_B1 * m_ref[...] + (1.0 - ADAM_B1) * gv
        nv = ADAM_B2 * v_ref[...] + (1.0 - ADAM_B2) * (gv * gv)
        m_hat = nm / (1.0 - ADAM_B1 ** ADAM_STEP)
        v_hat = nv / (1.0 - ADAM_B2 ** ADAM_STEP)
        d_ref[...] = -ADAM_LR * (m_hat / (jnp.sqrt(v_hat) + ADAM_EPS) + ADAM_WD * w_ref[...])
        nm_ref[...] = nm
        nv_ref[...] = nv

    spec = pl.BlockSpec((tr, cols), lambda r: (r, 0))
    return pl.pallas_call(
        body, name=name, grid=(rows // tr,), in_specs=[spec] * 4, out_specs=[spec] * 3,
        out_shape=[jax.ShapeDtypeStruct((rows, cols), F32)] * 3,
        compiler_params=_params("parallel"),
    )(w, g, m, v)


def _rows128(a):
    return a.reshape(-1, LANES)


def _pack_small(parts):
    pieces = [_rows128(parts[n]) for n, _ in SMALL_ROWS]
    pieces.append(jnp.zeros((SMALL_TOTAL - SMALL_USED, LANES), F32))
    return jnp.concatenate(pieces, axis=0)


def _unpack_small(packed, shapes):
    out, at = {}, 0
    for n, r in SMALL_ROWS:
        out[n] = packed[at:at + r].reshape(shapes[n])
        at += r
    return out


def kernel(x, p, pre_g, w_in, gmlp_ln_g, gmlp_ln_b, gmlp_ws, gmlp_bs, conv_w, conv_b, w_a, b_a, w_x, b_x, lam, gmlp_out_g, lru_out_g, w_out, post_g, w_pe, w_pg, loss_target, m_pre_g, m_w_in, m_gmlp_ln_g, m_gmlp_ln_b, m_gmlp_ws, m_gmlp_bs, m_conv_w, m_conv_b, m_w_a, m_b_a, m_w_x, m_b_x, m_lam, m_gmlp_out_g, m_lru_out_g, m_w_out, m_post_g, m_w_pe, m_w_pg, v_pre_g, v_w_in, v_gmlp_ln_g, v_gmlp_ln_b, v_gmlp_ws, v_gmlp_bs, v_conv_w, v_conv_b, v_w_a, v_b_a, v_w_x, v_b_x, v_lam, v_gmlp_out_g, v_lru_out_g, v_w_out, v_post_g, v_w_pe, v_w_pg):
    weights = dict(pre_g=pre_g, w_in=w_in, gmlp_ln_g=gmlp_ln_g, gmlp_ln_b=gmlp_ln_b, gmlp_ws=gmlp_ws, gmlp_bs=gmlp_bs,
                   conv_w=conv_w, conv_b=conv_b, w_a=w_a, b_a=b_a, w_x=w_x, b_x=b_x, lam=lam, gmlp_out_g=gmlp_out_g,
                   lru_out_g=lru_out_g, w_out=w_out, post_g=post_g, w_pe=w_pe, w_pg=w_pg)
    mom_m = dict(pre_g=m_pre_g, w_in=m_w_in, gmlp_ln_g=m_gmlp_ln_g, gmlp_ln_b=m_gmlp_ln_b, gmlp_ws=m_gmlp_ws,
                 gmlp_bs=m_gmlp_bs, conv_w=m_conv_w, conv_b=m_conv_b, w_a=m_w_a, b_a=m_b_a, w_x=m_w_x, b_x=m_b_x,
                 lam=m_lam, gmlp_out_g=m_gmlp_out_g, lru_out_g=m_lru_out_g, w_out=m_w_out, post_g=m_post_g,
                 w_pe=m_w_pe, w_pg=m_w_pg)
    mom_v = dict(pre_g=v_pre_g, w_in=v_w_in, gmlp_ln_g=v_gmlp_ln_g, gmlp_ln_b=v_gmlp_ln_b, gmlp_ws=v_gmlp_ws,
                 gmlp_bs=v_gmlp_bs, conv_w=v_conv_w, conv_b=v_conv_b, w_a=v_w_a, b_a=v_b_a, w_x=v_w_x, b_x=v_b_x,
                 lam=v_lam, gmlp_out_g=v_gmlp_out_g, lru_out_g=v_lru_out_g, w_out=v_w_out, post_g=v_post_g,
                 w_pe=v_w_pe, w_pg=v_w_pg)
    order = list(weights)
    xi, yi, ci = _place()
    me = _chip_of(xi, yi)
    kc = jnp.stack([me, ci]).astype(jnp.int32)

    x2 = x[0]
    p2 = p[0, 0]
    tgt = loss_target[0]

    g_in, g_out, g_pg, g_pe, g_cw = _gather_weights(w_in[0], w_out[0], w_pg[0], w_pe[0], conv_w[0, :, 0, :])
    wg_in = g_in.reshape(N_CHIPS, D_MODEL, W_IN_COLS)
    wg_out = g_out.reshape(D_MODEL, D_MODEL)
    wg_pg = g_pg.reshape(D_MODEL, D_MODEL)
    wg_pe = g_pe.reshape(N_CHIPS, D_PLE, W_PE_COLS)
    cw_full = jnp.transpose(g_cw, (1, 0, 2)).reshape(CONV_W, D_HALF)

    causal = jnp.tril(jnp.ones((CHUNK, CHUNK), dtype=bool))
    ws_m = jnp.where(causal[None], gmlp_ws[0], 0.0)
    prm = dict(
        ln_g=gmlp_ln_g, ln_b=gmlp_ln_b, wt=ws_m.astype(BF16), wtt=jnp.transpose(ws_m, (0, 2, 1)).astype(BF16),
        bsx=jnp.repeat(jnp.transpose(gmlp_bs[0]), CHUNK, axis=1),
        conv_w=cw_full, conv_b=conv_b, w_a=w_a[0].astype(BF16), w_x=w_x[0].astype(BF16),
        b_a=b_a[0].reshape(1, D_HALF), b_x=b_x[0].reshape(1, D_HALF), lam=lam, oga=gmlp_out_g, ogb=lru_out_g)

    z, hn = _inproj_fwd(x2, pre_g, wg_in, 512)
    y, h = _branches_fwd(z, prm, 256)
    o, h1, gt, dout, loss_acc = _outproj_fwd(x2, y, p2, tgt, post_g, wg_out, wg_pg, wg_pe, 256)
    loss = lax.psum(loss_acc[0, 0], ("x", "y", "c"))

    def sibling_start(tag, bufs):
        lands = [_landing((b.shape[0],) + b.shape[2:], b.dtype) for b in bufs]
        return _exchange_start("sibling_start_" + tag, bufs + lands, len(bufs), _sibling_copies(len(bufs)))

    def pair_then_chip_start(tag, started, after, names, tiles, dtypes):
        n = len(names)
        got = _exchange_wait("sibling_wait_" + tag, started, after, _sibling_copies(n))
        pairs = [_pair_sum(got[b], got[n + b], kc, "pair_sum_" + names[b], tiles[b], dtypes[b]) for b in range(n)]
        lands = [_landing((3,) + pr[0].shape[1:], pr[0].dtype) for pr in pairs]
        return _exchange_start("chip_start_" + tag, [pr[0] for pr in pairs] + lands, 3 * n, _chip_copies(n)), pairs

    def sum_then_finish_start(tag, started, pairs, after, names, tiles, small):
        n = len(names)
        got = _exchange_wait("chip_wait_" + tag, started, after, _chip_copies(n))
        sums = [_chip_sum(pairs[b][1], got[n + b], kc if small and b == n - 1 else kc[1:],
                          (N_CHIPS, 2) if small and b == n - 1 else (2,), "chip_sum_" + names[b], tiles[b])
                for b in range(n)]
        nbig = n - 1 if small else n
        return _exchange_start("finish_start_" + tag, sums, nbig + (7 if small else 0), _finish_copies(nbig, small))

    gw_pe, dq, dh1, do, dy, g_post = _head_bwd(dout, gt, p2, o, post_g, wg_out, wg_pg, wg_pe, 256)
    gw_pe = gw_pe.reshape(N_CHIPS, 2, D_PLE // 2, W_PE_COLS)
    token0 = jnp.zeros((SUBLANES, LANES), F32)
    gw_out = _weight_grad(y, do, "grad_w_out", 2, 1, D_MODEL // 2, D_MODEL, 1024, token0)
    gw_pg = _weight_grad(h1, dq, "grad_w_pg", 2, 1, D_MODEL // 2, D_MODEL, 1024, token0)
    gw_out = gw_out.reshape(N_CHIPS, 2, W_ROWS // 2, D_MODEL)
    gw_pg = gw_pg.reshape(N_CHIPS, 2, W_ROWS // 2, D_MODEL)

    names_a, tiles_a = ["w_out", "w_pg", "w_pe"], [128, 128, 128]
    st, tok = sibling_start("a", [gw_out, gw_pg, gw_pe])
    (dz, g_oga, g_ogb, g_lng, g_lnb, g_bsx, g_ws, g_cw, g_cb, g_wa, g_ba, g_wx, g_bx, g_lam) = _branches_bwd(
        z, h, dy, prm, 256, tok)
    (st, tok), pairs_a = pair_then_chip_start("a", st, dz, names_a, tiles_a, [BF16] * 3)
    gw_in = _weight_grad(hn, dz, "grad_w_in", 2, N_CHIPS, D_MODEL // 2, W_IN_COLS, 1024, tok)
    fin_a, tok = sum_then_finish_start("a", st, pairs_a, gw_in, names_a, tiles_a, False)

    small_g = dict(
        gmlp_ln_g=g_lng[0:1], gmlp_ln_b=g_lnb[0:1], gmlp_ws=g_ws,
        gmlp_bs=jnp.transpose(g_bsx[:, ::CHUNK]), conv_w=g_cw[::SUBLANES], conv_b=g_cb[0:1], w_a=g_wa, b_a=g_ba[0:1],
        w_x=g_wx, b_x=g_bx[0:1], lam=g_lam[0:1], gmlp_out_g=g_oga[0:1], lru_out_g=g_ogb[0:1], post_g=g_post[0:1])
    gsm = _pack_small(small_g).reshape(N_CHIPS, 2, SMALL_PIECE, LANES)

    names_b, tiles_b = ["w_in", "small"], [256, SMALL_PIECE]
    half = x2.shape[0] // 512 // 2
    st, tok_b = sibling_start("b", [gw_in, gsm])
    part = _inproj_bwd(dz, wg_in, x2, dh1, pre_g, 512, 0, half, None, False, tok_b, "inproj_bwd_lo")
    f_out, f_pg, f_pe = _exchange_wait("finish_wait_a", fin_a, part[1], _finish_copies(3, False))
    (st, tok_b), pairs_b = pair_then_chip_start("b", st, part[1], names_b, tiles_b, [BF16, F32])
    grad_x, g_pre = _inproj_bwd(dz, wg_in, x2, dh1, pre_g, 512, half, half, part, True, tok_b, "inproj_bwd_hi")
    fin_b, tok_b = sum_then_finish_start("b", st, pairs_b, g_pre, names_b, tiles_b, True)
    g_pre_sum = _allreduce_vector(_rows128(g_pre[0:1]), tok_b)
    f_in, f_sm = _exchange_wait("finish_wait_b", fin_b, g_pre_sum, _finish_copies(1, True))

    big_g = dict(w_in=f_in.reshape(D_MODEL, W_IN_COLS), w_out=f_out.reshape(W_ROWS, D_MODEL),
                 w_pg=f_pg.reshape(W_ROWS, D_MODEL), w_pe=f_pe.reshape(D_PLE, W_PE_COLS))
    grads, deltas, new_m, new_v = {}, {}, {}, {}
    for n, tr in (("w_in", 256), ("w_out", 128), ("w_pg", 128), ("w_pe", 128)):
        shp = weights[n].shape
        grads[n] = big_g[n].reshape(shp)
        d, nm, nv = _adamw(weights[n][0], big_g[n], mom_m[n][0], mom_v[n][0], "adamw_" + n, tr)
        deltas[n], new_m[n], new_v[n] = d.reshape(shp), nm.reshape(shp), nv.reshape(shp)

    packed_g = f_sm.reshape(SMALL_TOTAL, LANES)
    small_names = [n for n, _ in SMALL_ROWS]
    shapes = {n: weights[n].shape for n in small_names}
    shapes["conv_w"] = (CONV_W, D_HALF)
    zero_cw = jnp.zeros((CONV_W, D_HALF), F32)
    pack_w = lambda src: _pack_small({n: (zero_cw if n == "conv_w" else src[n]) for n in small_names})
    d_sm, m_sm, v_sm = _adamw(pack_w(weights), packed_g, pack_w(mom_m), pack_w(mom_v), "adamw_small", SMALL_PIECE)
    ug, ud, um, uv = (_unpack_small(a, shapes) for a in (packed_g, d_sm, m_sm, v_sm))
    for n in small_names:
        if n != "conv_w":
            grads[n], deltas[n], new_m[n], new_v[n] = ug[n], ud[n], um[n], uv[n]
    g_conv = lax.dynamic_slice_in_dim(ug["conv_w"], me * CONV_COLS, CONV_COLS, axis=1)
    d, nm, nv = _adamw(conv_w[0, :, 0, :], g_conv, m_conv_w[0, :, 0, :], v_conv_w[0, :, 0, :], "adamw_conv_w", CONV_W)
    cshape = conv_w.shape
    grads["conv_w"], deltas["conv_w"] = g_conv.reshape(cshape), d.reshape(cshape)
    new_m["conv_w"], new_v["conv_w"] = nm.reshape(cshape), nv.reshape(cshape)
    d, nm, nv = _adamw(_rows128(pre_g), g_pre_sum, _rows128(m_pre_g), _rows128(v_pre_g), "adamw_pre_g", 16)
    pshape = pre_g.shape
    grads["pre_g"], deltas["pre_g"] = g_pre_sum.reshape(pshape), d.reshape(pshape)
    new_m["pre_g"], new_v["pre_g"] = nm.reshape(pshape), nv.reshape(pshape)

    return (loss, grad_x.reshape(x.shape), *[grads[n] for n in order], *[deltas[n] for n in order],
            *[new_m[n] for n in order], *[new_v[n] for n in order])
```

```python
import functools
import math

import jax
import jax.numpy as jnp
from jax import lax
from jax.experimental import pallas as pl
from jax.experimental.pallas import tpu as pltpu

F32 = jnp.float32
BF16 = jnp.bfloat16

D_MODEL = 2048
D_HALF = 1024
D_Z = 5120
D_PLE = 256
CHUNK = 128
N_HEADS = 8
N_CHIPS = 4
W_IN_COLS = D_Z // N_CHIPS
W_ROWS = D_MODEL // N_CHIPS
W_PE_COLS = D_MODEL // N_CHIPS
CONV_W = 4
CONV_COLS = D_HALF // N_CHIPS
EPS = 1e-6
LRU_C = 8.0
ADAM_LR, ADAM_B1, ADAM_B2, ADAM_EPS, ADAM_WD, ADAM_STEP = 0.001, 0.9, 0.999, 1e-08, 0.01, 10

SUBLANES = 8
LANES = 128
VMEM_LIMIT = 56 * 1024 * 1024

SMALL_ROWS = (("gmlp_ln_g", 8), ("gmlp_ln_b", 8), ("gmlp_ws", 1024), ("gmlp_bs", 8),
              ("conv_w", 32), ("conv_b", 8), ("w_a", 1024), ("b_a", 8), ("w_x", 1024), ("b_x", 8),
              ("lam", 8), ("gmlp_out_g", 8), ("lru_out_g", 8), ("post_g", 16))
SMALL_USED = sum(r for _, r in SMALL_ROWS)
SMALL_PIECE = 400
SMALL_TOTAL = 8 * SMALL_PIECE

MESH = pl.DeviceIdType.MESH
ANY = pl.BlockSpec(memory_space=pl.ANY)

_GELU_C0 = math.sqrt(2.0 / math.pi)
_GELU_C1 = 0.044715


def _params(*sem):
    return pltpu.CompilerParams(dimension_semantics=sem, vmem_limit_bytes=VMEM_LIMIT)


def _dot(a, b):
    return jnp.dot(a, b, preferred_element_type=F32)


def _dot_nt(a, b):
    return lax.dot_general(a, b, (((1,), (1,)), ((), ())), preferred_element_type=F32)


def _dot_tn(a, b):
    return lax.dot_general(a, b, (((0,), (0,)), ((), ())), preferred_element_type=F32)


def _gelu(x):
    t = jnp.tanh(_GELU_C0 * (x + _GELU_C1 * (x * x * x)))
    return 0.5 * x * (1.0 + t), t


def _gelu_grad(x, t):
    return 0.5 * (1.0 + t) + 0.5 * x * (1.0 - t * t) * (_GELU_C0 * (1.0 + 3.0 * _GELU_C1 * x * x))


def _rowsum8(v):
    r, n = v.shape
    return jnp.sum(v.reshape(r // SUBLANES, SUBLANES, n), axis=0)


def _lanemean(v):
    return jnp.mean(v, axis=-1, keepdims=True)


def _shift_down(v, halo8, k):
    if k == 0:
        return v
    r = pltpu.roll(v, k, 0)
    hr = pltpu.roll(halo8, k, 0)
    row = lax.broadcasted_iota(jnp.int32, halo8.shape, 0)
    top = jnp.where(row < k, hr, r[0:SUBLANES])
    return jnp.concatenate([top, r[SUBLANES:]], axis=0)


def _shift_up(v, next8, k):
    if k == 0:
        return v
    n = v.shape[0]
    r = pltpu.roll(v, n - k, 0)
    nr = pltpu.roll(next8, SUBLANES - k, 0)
    row = lax.broadcasted_iota(jnp.int32, next8.shape, 0)
    bot = jnp.where(row >= SUBLANES - k, nr, r[n - SUBLANES:])
    return jnp.concatenate([r[:n - SUBLANES], bot], axis=0)


def _layernorm_parts(vg):
    mu = _lanemean(vg)
    xc = vg - mu
    rstd = lax.rsqrt(_lanemean(xc * xc) + EPS)
    return xc * rstd, rstd


def _spatial_mix(wt_ref, vn_ref, bsx_ref, mixed_ref, tm):
    for c in range(tm // CHUNK):
        rows = slice(c * CHUNK, (c + 1) * CHUNK)
        for h in range(N_HEADS):
            cols = slice(h * CHUNK, (h + 1) * CHUNK)
            mixed_ref[rows, cols] = _dot(wt_ref[h], vn_ref[rows, cols]) + bsx_ref[:, cols]


def _conv_taps(xb, halo8):
    return [_shift_down(xb, halo8, CONV_W - 1 - k) for k in range(CONV_W)]


def _lru_gates(xc_bf_ref, wa_ref, wx_ref, ba_ref, bx_ref, r_ref, i_ref):
    for h in range(N_HEADS):
        cols = slice(h * CHUNK, (h + 1) * CHUNK)
        xh = xc_bf_ref[:, cols]
        r_ref[:, cols] = jax.nn.sigmoid(_dot(xh, wa_ref[h]) + ba_ref[:, cols])
        i_ref[:, cols] = jax.nn.sigmoid(_dot(xh, wx_ref[h]) + bx_ref[:, cols])


def _softplus_neg(lam):
    return jnp.maximum(-lam, 0.0) + jnp.log(1.0 + jnp.exp(-jnp.abs(lam)))


def _decay_parts(r, lam):
    la = (-LRU_C * _softplus_neg(lam)) * r
    a = jnp.exp(la)
    th = -jnp.tanh(la)
    mult = jnp.sqrt(2.0 * th / (1.0 + th))
    return a, mult


def _inproj_branches_fwd(x, pre_g, wg_in, prm, tm):
    t = x.shape[0]
    nt = t // tm
    hb = tm // SUBLANES

    def body(x_ref, g_ref, w_ref,
             lng_ref, lnb_ref, wt_ref, bsx_ref, cw_ref, cb_ref, wa_ref, wx_ref, ba_ref, bx_ref, lam_ref,
             oga_ref, ogb_ref,
             z_ref, hn_ref, y_ref, h_ref,
             zbuf0, zbuf1, vn_s, mixed_s, xcbf_s, r_s, i_s, halo_s, carry_s):
        s = pl.program_id(0)

        @pl.when(s == 0)
        def _():
            zbuf1[...] = jnp.zeros_like(zbuf1)

        @pl.when(s <= 1)
        def _():
            carry_s[...] = jnp.zeros_like(carry_s)
            halo_s[...] = jnp.zeros_like(halo_s)

        @pl.when(s % 2 == 0)
        def _():
            step(s, zbuf0, zbuf1, x_ref, g_ref, w_ref, lng_ref, lnb_ref, wt_ref, bsx_ref, cw_ref, cb_ref, wa_ref,
                 wx_ref, ba_ref, bx_ref, lam_ref, oga_ref, ogb_ref, z_ref, hn_ref, y_ref, h_ref, vn_s, mixed_s,
                 xcbf_s, r_s, i_s, halo_s, carry_s)

        @pl.when(s % 2 == 1)
        def _():
            step(s, zbuf1, zbuf0, x_ref, g_ref, w_ref, lng_ref, lnb_ref, wt_ref, bsx_ref, cw_ref, cb_ref, wa_ref,
                 wx_ref, ba_ref, bx_ref, lam_ref, oga_ref, ogb_ref, z_ref, hn_ref, y_ref, h_ref, vn_s, mixed_s,
                 xcbf_s, r_s, i_s, halo_s, carry_s)

    def step(s, zw, zr, x_ref, g_ref, w_ref, lng_ref, lnb_ref, wt_ref, bsx_ref, cw_ref, cb_ref, wa_ref, wx_ref,
             ba_ref, bx_ref, lam_ref, oga_ref, ogb_ref, z_ref, hn_ref, y_ref, h_ref, vn_s, mixed_s, xcbf_s, r_s,
             i_s, halo_s, carry_s):
        xv = x_ref[...]
        hn = (xv * lax.rsqrt(_lanemean(xv * xv) + EPS) * g_ref[...]).astype(BF16)
        hn_ref[...] = hn

        def project(j):
            cols = slice(j * W_IN_COLS, (j + 1) * W_IN_COLS)
            zb = _dot(hn_ref[...], w_ref[j]).astype(BF16)
            z_ref[:, cols] = zb
            zw[:, cols] = zb

        project(0)
        zin = lambda g: zr[:, g * D_HALF:(g + 1) * D_HALF].astype(F32)
        ug, _ = _gelu(zin(0))
        vg, _ = _gelu(zin(1))
        vhat, _ = _layernorm_parts(vg)
        vn_s[...] = (vhat * lng_ref[...] + lnb_ref[...]).astype(BF16)
        _spatial_mix(wt_ref, vn_s, bsx_ref, mixed_s, tm)
        ga = zin(2)
        ya = ug * mixed_s[...] * (ga * jax.nn.sigmoid(ga))
        ra = lax.rsqrt(_lanemean(ya * ya) + EPS)
        y_ref[:, 0:D_HALF] = (ya * ra * oga_ref[...]).astype(BF16)

        project(1)
        xb = zin(3)
        taps = _conv_taps(xb, halo_s[...])
        halo_s[...] = xb[tm - SUBLANES:]
        xc = cb_ref[...] + taps[0] * cw_ref[0:1, :]
        for k in range(1, CONV_W):
            xc = xc + taps[k] * cw_ref[k:k + 1, :]
        xcbf_s[...] = xc.astype(BF16)
        _lru_gates(xcbf_s, wa_ref, wx_ref, ba_ref, bx_ref, r_s, i_s)
        a, mult = _decay_parts(r_s[...], lam_ref[...])
        row = lax.broadcasted_iota(jnp.int32, a.shape, 0)
        mult = jnp.where(jnp.logical_and(s == 1, row == 0), 1.0, mult)
        b = mult * (i_s[...] * xc)
        project(2)
        r8 = row & (SUBLANES - 1)
        for d in (1, 2, 4):
            a_sh = pltpu.roll(a, d, 0)
            b_sh = pltpu.roll(b, d, 0)
            m = r8 >= d
            b = jnp.where(m, a * b_sh + b, b)
            a = jnp.where(m, a * a_sh, a)
        project(3)
        carry = carry_s[...]
        for g in range(hb):
            rows = slice(g * SUBLANES, (g + 1) * SUBLANES)
            hg = a[rows] * carry + b[rows]
            h_ref[rows, :] = hg
            carry = jnp.broadcast_to(hg[SUBLANES - 1:SUBLANES, :], hg.shape)
        carry_s[...] = carry
        gb = zin(4)
        yb = h_ref[...] * (gb * jax.nn.sigmoid(gb))
        rb = lax.rsqrt(_lanemean(yb * yb) + EPS)
        y_ref[:, D_HALF:] = (yb * rb * ogb_ref[...]).astype(BF16)

    const = lambda a: pl.BlockSpec(a.shape, lambda s, n=a.ndim: (0,) * n, pipeline_mode=pl.Buffered(1))
    proj = lambda n: pl.BlockSpec((tm, n), lambda s: (jnp.minimum(s, nt - 1), 0))
    head = lambda n: pl.BlockSpec((tm, n), lambda s: (jnp.maximum(s - 1, 0), 0))
    names = ("ln_g", "ln_b", "wt", "bsx", "conv_w", "conv_b", "w_a", "w_x", "b_a", "b_x", "lam", "oga", "ogb")
    pr = [prm[n] for n in names]
    big = lambda dt: pltpu.VMEM((tm, D_HALF), dt)
    return pl.pallas_call(
        body, name="inproj_branches_fwd", grid=(nt + 1,),
        in_specs=[proj(D_MODEL), const(pre_g), const(wg_in)] + [const(a) for a in pr],
        out_specs=[proj(D_Z), proj(D_MODEL), head(D_MODEL), head(D_HALF)],
        out_shape=[jax.ShapeDtypeStruct((t, D_Z), BF16), jax.ShapeDtypeStruct((t, D_MODEL), BF16),
                   jax.ShapeDtypeStruct((t, D_MODEL), BF16), jax.ShapeDtypeStruct((t, D_HALF), F32)],
        scratch_shapes=[pltpu.VMEM((tm, D_Z), BF16), pltpu.VMEM((tm, D_Z), BF16),
                        big(BF16), big(F32), big(BF16), big(F32), big(F32),
                        pltpu.VMEM((SUBLANES, D_HALF), F32), pltpu.VMEM((SUBLANES, D_HALF), F32)],
        compiler_params=_params("arbitrary"),
    )(x, pre_g, wg_in, *pr)


def _outproj_fwd(x, y, p, tgt, post_g, w_out, w_pg, wg_pe, tm):
    t = x.shape[0]

    def body(x_ref, y_ref, p_ref, tgt_ref, pg_ref, wo_ref, wpg_ref, wpe_ref,
             o_ref, h1_ref, gt_ref, dout_ref, loss_ref):
        @pl.when(pl.program_id(0) == 0)
        def _():
            loss_ref[...] = jnp.zeros_like(loss_ref)

        o = _dot(y_ref[...], wo_ref[...])
        o_ref[...] = o
        r3 = lax.rsqrt(_lanemean(o * o) + EPS)
        h1 = x_ref[...] + (o * r3) * pg_ref[...]
        h1b = h1.astype(BF16)
        h1_ref[...] = h1b
        gt = jax.nn.sigmoid(_dot(h1b, wpg_ref[...]))
        gt_ref[...] = gt
        pb = p_ref[...].astype(BF16)
        for k in range(N_CHIPS):
            cols = slice(k * W_PE_COLS, (k + 1) * W_PE_COLS)
            pe = _dot(pb, wpe_ref[k])
            d = h1[:, cols] + pe * gt[:, cols] - tgt_ref[:, cols]
            dout_ref[:, cols] = d * (1.0 / D_MODEL)
            loss_ref[...] += jnp.sum(d * d) * (0.5 / D_MODEL)

    row = lambda n: pl.BlockSpec((tm, n), lambda i: (i, 0))
    const = lambda shp: pl.BlockSpec(shp, lambda i, n=len(shp): (0,) * n, pipeline_mode=pl.Buffered(1))
    return pl.pallas_call(
        body, name="outproj_fwd", grid=(t // tm,),
        in_specs=[row(D_MODEL), row(D_MODEL), row(D_PLE), row(D_MODEL), const((1, D_MODEL)),
                  const((D_MODEL, D_MODEL)), const((D_MODEL, D_MODEL)), const((N_CHIPS, D_PLE, W_PE_COLS))],
        out_specs=[row(D_MODEL), row(D_MODEL), row(D_MODEL), row(D_MODEL),
                   pl.BlockSpec((SUBLANES, LANES), lambda i: (0, 0))],
        out_shape=[jax.ShapeDtypeStruct((t, D_MODEL), F32), jax.ShapeDtypeStruct((t, D_MODEL), BF16),
                   jax.ShapeDtypeStruct((t, D_MODEL), F32), jax.ShapeDtypeStruct((t, D_MODEL), F32),
                   jax.ShapeDtypeStruct((SUBLANES, LANES), F32)],
        compiler_params=_params("arbitrary"),
    )(x, y, p, tgt, post_g, w_out, w_pg, wg_pe)


def _head_bwd(dout, gt, p, o, post_g, w_out, w_pg, wg_pe, tm):
    t = dout.shape[0]

    def body(dout_ref, gt_ref, p_ref, o_ref, pg_ref, wo_ref, wpg_ref, wpe_ref,
             gwpe_ref, dq_ref, dh1_ref, do_ref, dy_ref, gpost_ref):
        i = pl.program_id(0)

        @pl.when(i == 0)
        def _():
            gpost_ref[...] = jnp.zeros_like(gpost_ref)
            gwpe_ref[...] = jnp.zeros_like(gwpe_ref)

        dout = dout_ref[...]
        gt = gt_ref[...]
        pb = p_ref[...].astype(BF16)
        for k in range(N_CHIPS):
            cols = slice(k * W_PE_COLS, (k + 1) * W_PE_COLS)
            pe = _dot(pb, wpe_ref[k])
            g = gt[:, cols]
            dg = dout[:, cols] * g
            gwpe_ref[k] += _dot_tn(pb, dg.astype(BF16))
            dq_ref[:, cols] = (dg * pe * (1.0 - g)).astype(BF16)
        dh1 = dout + _dot_nt(dq_ref[...], wpg_ref[...])
        dh1_ref[...] = dh1
        o = o_ref[...]
        r3 = lax.rsqrt(_lanemean(o * o) + EPS)
        on = o * r3
        gpost_ref[...] += _rowsum8(dh1 * on)
        don = dh1 * pg_ref[...]
        do = r3 * (don - on * _lanemean(don * on))
        dob = do.astype(BF16)
        do_ref[...] = dob
        dy_ref[...] = _dot_nt(dob, wo_ref[...])

        @pl.when(i == pl.num_programs(0) - 1)
        def _():
            gpost_ref[...] = jnp.broadcast_to(jnp.sum(gpost_ref[...], axis=0, keepdims=True), gpost_ref.shape)

    row = lambda n: pl.BlockSpec((tm, n), lambda i: (i, 0))
    const = lambda shp: pl.BlockSpec(shp, lambda i, n=len(shp): (0,) * n, pipeline_mode=pl.Buffered(1))
    return pl.pallas_call(
        body, name="head_bwd", grid=(t // tm,),
        in_specs=[row(D_MODEL), row(D_MODEL), row(D_PLE), row(D_MODEL), const((1, D_MODEL)),
                  const((D_MODEL, D_MODEL)), const((D_MODEL, D_MODEL)), const((N_CHIPS, D_PLE, W_PE_COLS))],
        out_specs=[pl.BlockSpec((N_CHIPS, D_PLE, W_PE_COLS), lambda i: (0, 0, 0)),
                   row(D_MODEL), row(D_MODEL), row(D_MODEL), row(D_MODEL),
                   pl.BlockSpec((SUBLANES, D_MODEL), lambda i: (0, 0))],
        out_shape=[jax.ShapeDtypeStruct((N_CHIPS, D_PLE, W_PE_COLS), F32), jax.ShapeDtypeStruct((t, D_MODEL), BF16),
                   jax.ShapeDtypeStruct((t, D_MODEL), F32), jax.ShapeDtypeStruct((t, D_MODEL), BF16),
                   jax.ShapeDtypeStruct((t, D_MODEL), F32), jax.ShapeDtypeStruct((SUBLANES, D_MODEL), F32)],
        compiler_params=_params("arbitrary"),
    )(dout, gt, p, o, post_g, w_out, w_pg, wg_pe)


def _branches_bwd(z, h, dy, prm, tm, token):
    t = z.shape[0]
    nt = t // tm
    hb = tm // SUBLANES

    def body(u_ref, v_ref, ga_ref, xb_ref, gb_ref, xbh_ref, h_ref, hh_ref, dy_ref,
             lng_ref, lnb_ref, wt_ref, wtt_ref, bsx_ref, cw_ref, cb_ref, wa_ref, wx_ref, ba_ref, bx_ref, lam_ref,
             oga_ref, ogb_ref, token_ref,
             dz_ref, g_oga, g_ogb, g_lng, g_lnb, g_bsx, g_ws, g_cw, g_cb, g_wa, g_ba, g_wx, g_bx, g_lam,
             vn_s, mixed_s, dm_s, dvn_s, xcbf_s, r_s, i_s, a_s, b_s, dh_s, dpr_s, dpi_s, dxc_s,
             ca_s, cd_s, cx_s):
        step_i = pl.program_id(0)
        tile = nt - 1 - step_i
        accs = (g_oga, g_ogb, g_lng, g_lnb, g_bsx, g_ws, g_cw, g_cb, g_wa, g_ba, g_wx, g_bx, g_lam)

        @pl.when(step_i == 0)
        def _():
            for r in accs + (ca_s, cd_s, cx_s):
                r[...] = jnp.zeros_like(r)

        dy_a = dy_ref[:, 0:D_HALF]
        dy_b = dy_ref[:, D_HALF:]

        u = u_ref[...].astype(F32)
        ug, tu = _gelu(u)
        v = v_ref[...].astype(F32)
        vg, tv = _gelu(v)
        vhat, rstd = _layernorm_parts(vg)
        vn_s[...] = (vhat * lng_ref[...] + lnb_ref[...]).astype(BF16)
        _spatial_mix(wt_ref, vn_s, bsx_ref, mixed_s, tm)
        mixed = mixed_s[...]
        ga = ga_ref[...].astype(F32)
        sga = jax.nn.sigmoid(ga)
        sa = ga * sga
        um = ug * mixed
        ya = um * sa
        ra = lax.rsqrt(_lanemean(ya * ya) + EPS)
        yahat = ya * ra
        g_oga[...] += _rowsum8(dy_a * yahat)
        dn = dy_a * oga_ref[...]
        dya = ra * (dn - yahat * _lanemean(dn * yahat))
        dz_ref[:, 2 * D_HALF:3 * D_HALF] = (dya * um * (sga * (1.0 + ga * (1.0 - sga)))).astype(BF16)
        dz_ref[:, 0:D_HALF] = (dya * mixed * sa * _gelu_grad(u, tu)).astype(BF16)
        dmixed = dya * ug * sa
        g_bsx[...] += jnp.sum(dmixed.reshape(tm // CHUNK, CHUNK, D_HALF), axis=0)
        dm_s[...] = dmixed.astype(BF16)
        for c in range(tm // CHUNK):
            rows = slice(c * CHUNK, (c + 1) * CHUNK)
            for hd in range(N_HEADS):
                cols = slice(hd * CHUNK, (hd + 1) * CHUNK)
                dmh = dm_s[rows, cols]
                dvn_s[rows, cols] = _dot(wtt_ref[hd], dmh)
                g_ws[hd] += _dot_nt(dmh, vn_s[rows, cols])
        dvn = dvn_s[...]
        g_lng[...] += _rowsum8(dvn * vhat)
        g_lnb[...] += _rowsum8(dvn)
        dvh = dvn * lng_ref[...]
        dvg = rstd * (dvh - _lanemean(dvh) - vhat * _lanemean(dvh * vhat))
        dz_ref[:, D_HALF:2 * D_HALF] = (dvg * _gelu_grad(v, tv)).astype(BF16)

        xb = xb_ref[...].astype(F32)
        halo = jnp.where(tile == 0, 0.0, xbh_ref[...].astype(F32)[SUBLANES:])
        taps = _conv_taps(xb, halo)
        xc = cb_ref[...] + taps[0] * cw_ref[0:1, :]
        for k in range(1, CONV_W):
            xc = xc + taps[k] * cw_ref[k:k + 1, :]
        xcbf_s[...] = xc.astype(BF16)
        _lru_gates(xcbf_s, wa_ref, wx_ref, ba_ref, bx_ref, r_s, i_s)
        rg = r_s[...]
        ig = i_s[...]
        lam = lam_ref[...]
        a, mult_true = _decay_parts(rg, lam)
        row = lax.broadcasted_iota(jnp.int32, a.shape, 0)
        first = jnp.logical_and(tile == 0, row == 0)
        mult = jnp.where(first, 1.0, mult_true)
        hcur = h_ref[...]
        hprev = _shift_down(hcur, jnp.where(tile == 0, 0.0, hh_ref[...]), 1)
        gb = gb_ref[...].astype(F32)
        sgb = jax.nn.sigmoid(gb)
        sb = gb * sgb
        yb = hcur * sb
        rb = lax.rsqrt(_lanemean(yb * yb) + EPS)
        ybhat = yb * rb
        g_ogb[...] += _rowsum8(dy_b * ybhat)
        dn = dy_b * ogb_ref[...]
        dyb = rb * (dn - ybhat * _lanemean(dn * ybhat))
        dz_ref[:, 4 * D_HALF:5 * D_HALF] = (dyb * hcur * (sgb * (1.0 + gb * (1.0 - sgb)))).astype(BF16)

        an = _shift_up(a, ca_s[...], 1)
        bb = dyb * sb
        r8 = row & (SUBLANES - 1)
        for d in (1, 2, 4):
            a_sh = pltpu.roll(an, tm - d, 0)
            b_sh = pltpu.roll(bb, tm - d, 0)
            m = r8 + d < SUBLANES
            bb = jnp.where(m, an * b_sh + bb, bb)
            an = jnp.where(m, an * a_sh, an)
        a_s[...] = an
        b_s[...] = bb

        def step(g, carry):
            sl = pl.ds(pl.multiple_of((hb - 1 - g) * SUBLANES, SUBLANES), SUBLANES)
            dg = a_s[sl, :] * carry + b_s[sl, :]
            dh_s[sl, :] = dg
            return jnp.broadcast_to(dg[0:1, :], dg.shape)

        cd_s[...] = lax.fori_loop(0, hb, step, cd_s[...])
        ca_s[...] = jnp.broadcast_to(a[0:1, :], ca_s.shape)
        dh = dh_s[...]
        da = dh * hprev
        gx = ig * xc
        dla = da * a - jnp.where(first, 0.0, dh * gx * (a * a / mult_true))
        g_lam[...] += _rowsum8(dla * rg)
        dr = dla * (-LRU_C * _softplus_neg(lam))
        dpr = dr * rg * (1.0 - rg)
        dpi = (dh * mult * xc) * ig * (1.0 - ig)
        g_ba[...] += _rowsum8(dpr)
        g_bx[...] += _rowsum8(dpi)
        dpr_s[...] = dpr.astype(BF16)
        dpi_s[...] = dpi.astype(BF16)
        for hd in range(N_HEADS):
            cols = slice(hd * CHUNK, (hd + 1) * CHUNK)
            xh = xcbf_s[:, cols]
            dprh = dpr_s[:, cols]
            dpih = dpi_s[:, cols]
            g_wa[hd] += _dot_tn(xh, dprh)
            g_wx[hd] += _dot_tn(xh, dpih)
            dxc_s[:, cols] = _dot_nt(dprh, wa_ref[hd]) + _dot_nt(dpih, wx_ref[hd])
        dxc = dxc_s[...] + dh * mult * ig
        g_cb[...] += _rowsum8(dxc)
        for k in range(CONV_W):
            g_cw[k * SUBLANES:(k + 1) * SUBLANES, :] += _rowsum8(dxc * taps[k])
        nxt = cx_s[...]
        dxb = dxc * cw_ref[CONV_W - 1:CONV_W, :]
        for j in range(1, CONV_W):
            dxb = dxb + _shift_up(dxc, nxt, j) * cw_ref[CONV_W - 1 - j:CONV_W - j, :]
        dz_ref[:, 3 * D_HALF:4 * D_HALF] = dxb.astype(BF16)
        cx_s[...] = dxc[0:SUBLANES]

        @pl.when(step_i == nt - 1)
        def _():
            for r in (g_oga, g_ogb, g_lng, g_lnb, g_cb, g_ba, g_bx):
                r[...] = jnp.broadcast_to(jnp.sum(r[...], axis=0, keepdims=True), r.shape)
            lam_f = LRU_C * jax.nn.sigmoid(-lam_ref[...])
            g_lam[...] = jnp.broadcast_to(jnp.sum(g_lam[...], axis=0, keepdims=True) * lam_f, g_lam.shape)
            for k in range(CONV_W):
                blk = g_cw[k * SUBLANES:(k + 1) * SUBLANES, :]
                g_cw[k * SUBLANES:(k + 1) * SUBLANES, :] = jnp.broadcast_to(jnp.sum(blk, axis=0, keepdims=True), blk.shape)
            tri = (lax.broadcasted_iota(jnp.int32, (CHUNK, CHUNK), 0) >= lax.broadcasted_iota(jnp.int32, (CHUNK, CHUNK), 1))
            for hd in range(N_HEADS):
                cols = slice(hd * CHUNK, (hd + 1) * CHUNK)
                g_ws[hd] = jnp.where(tri, g_ws[hd], 0.0)
                blk = g_bsx[:, cols]
                g_bsx[:, cols] = jnp.broadcast_to(jnp.sum(blk, axis=1, keepdims=True), blk.shape)

    rev = lambda i: nt - 1 - i
    zspec = lambda g: pl.BlockSpec((tm, D_HALF), lambda i, g=g: (rev(i), g))
    halo = lambda col: pl.BlockSpec((SUBLANES, D_HALF), lambda i: (jnp.maximum(rev(i) * hb - 1, 0), col))
    zhalo = pl.BlockSpec((2 * SUBLANES, D_HALF), lambda i: (jnp.maximum(rev(i) * (hb // 2) - 1, 0), 3))
    full = lambda a: pl.BlockSpec(a.shape, lambda i, n=a.ndim: (0,) * n)
    acc = lambda shp: pl.BlockSpec(shp, lambda i, n=len(shp): (0,) * n)
    names = ("ln_g", "ln_b", "wt", "wtt", "bsx", "conv_w", "conv_b", "w_a", "w_x", "b_a", "b_x", "lam", "oga", "ogb")
    pr = [prm[n] for n in names] + [token]
    vec = (SUBLANES, D_HALF)
    mat = (N_HEADS, CHUNK, CHUNK)
    acc_shapes = [vec, vec, vec, vec, (CHUNK, D_HALF), mat, (CONV_W * SUBLANES, D_HALF), vec, mat, vec, mat, vec, vec]
    big = lambda dt: pltpu.VMEM((tm, D_HALF), dt)
    return pl.pallas_call(
        body, name="branches_bwd", grid=(nt,),
        in_specs=[zspec(0), zspec(1), zspec(2), zspec(3), zspec(4), zhalo,
                  pl.BlockSpec((tm, D_HALF), lambda i: (rev(i), 0)), halo(0),
                  pl.BlockSpec((tm, D_MODEL), lambda i: (rev(i), 0))] + [full(a) for a in pr],
        out_specs=[pl.BlockSpec((tm, D_Z), lambda i: (rev(i), 0))] + [acc(s) for s in acc_shapes],
        out_shape=[jax.ShapeDtypeStruct((t, D_Z), BF16)] + [jax.ShapeDtypeStruct(s, F32) for s in acc_shapes],
        scratch_shapes=[big(BF16), big(F32), big(BF16), big(F32), big(BF16), big(F32), big(F32), big(F32), big(F32),
                        big(F32), big(BF16), big(BF16), big(F32),
                        pltpu.VMEM(vec, F32), pltpu.VMEM(vec, F32), pltpu.VMEM(vec, F32)],
        compiler_params=_params("arbitrary"),
    )(z, z, z, z, z, z, h, h, dy, *pr)


def _inproj_bwd(dz, wg_in, x, dh1, pre_g, tm, tile0, nt, prev, last, token, name):
    t = x.shape[0]

    def body(*refs):
        dz_ref, w_ref, x_ref, dh1_ref, g_ref = refs[:5]
        gx_ref, gpre_ref, acc_s = refs[-3:]
        i = pl.program_id(0)
        k = pl.program_id(1)

        @pl.when(jnp.logical_and(i == 0, k == 0))
        def _():
            gpre_ref[...] = jnp.zeros_like(gpre_ref) if prev is None else refs[7][...]

        part = _dot_nt(dz_ref[...], w_ref[...])

        @pl.when(k == 0)
        def _():
            acc_s[...] = part

        @pl.when(k > 0)
        def _():
            acc_s[...] += part

        @pl.when(k == N_CHIPS - 1)
        def _():
            for s in range(tm // CHUNK):
                rows = slice(s * CHUNK, (s + 1) * CHUNK)
                xv = x_ref[rows, :]
                r = lax.rsqrt(_lanemean(xv * xv) + EPS)
                xhat = xv * r
                dhn = acc_s[rows, :]
                gpre_ref[...] += _rowsum8(dhn * xhat)
                dxh = dhn * g_ref[...]
                gx_ref[rows, :] = dh1_ref[rows, :] + r * (dxh - xhat * _lanemean(dxh * xhat))

        if last:
            @pl.when(jnp.logical_and(i == nt - 1, k == N_CHIPS - 1))
            def _():
                gpre_ref[...] = jnp.broadcast_to(jnp.sum(gpre_ref[...], axis=0, keepdims=True), gpre_ref.shape)

    row = pl.BlockSpec((tm, D_MODEL), lambda i, k: (tile0 + i, 0))
    small = lambda r: pl.BlockSpec((r, D_MODEL), lambda i, k: (0, 0))
    tok = pl.BlockSpec((SUBLANES, LANES), lambda i, k: (0, 0))
    in_specs = [pl.BlockSpec((tm, W_IN_COLS), lambda i, k: (tile0 + i, k)),
                pl.BlockSpec((None, D_MODEL, W_IN_COLS), lambda i, k: (k, 0, 0)), row, row, small(1), tok]
    args = [dz, wg_in, x, dh1, pre_g, token]
    aliases = {}
    if prev is not None:
        in_specs += [ANY, small(SUBLANES)]
        args += list(prev)
        aliases = {6: 0}
    return pl.pallas_call(
        body, name=name, grid=(nt, N_CHIPS), in_specs=in_specs, out_specs=[row, small(SUBLANES)],
        out_shape=[jax.ShapeDtypeStruct((t, D_MODEL), F32), jax.ShapeDtypeStruct((SUBLANES, D_MODEL), F32)],
        input_output_aliases=aliases,
        scratch_shapes=[pltpu.VMEM((tm, D_MODEL), F32)],
        compiler_params=_params("arbitrary", "arbitrary"),
    )(*args)


def _weight_grad(a, b, name, kb, nb, tk, tn, tt, token):
    t = a.shape[0]
    tt = min(tt, t)

    def body(a_ref, b_ref, token_ref, o_ref):
        @pl.when(pl.program_id(2) == 0)
        def _():
            o_ref[...] = jnp.zeros_like(o_ref)

        o_ref[...] += _dot_tn(a_ref[...], b_ref[...])

    return pl.pallas_call(
        body, name=name, grid=(nb, kb, t // tt),
        in_specs=[pl.BlockSpec((tt, tk), lambda j, i, s: (s, i)), pl.BlockSpec((tt, tn), lambda j, i, s: (s, j)),
                  pl.BlockSpec((SUBLANES, LANES), lambda j, i, s: (0, 0))],
        out_specs=pl.BlockSpec((None, None, tk, tn), lambda j, i, s: (j, i, 0, 0)),
        out_shape=jax.ShapeDtypeStruct((nb, kb, tk, tn), F32),
        compiler_params=_params("parallel", "parallel", "arbitrary"),
    )(a, b, token)


def _place():
    x, y, c = lax.axis_index("x"), lax.axis_index("y"), lax.axis_index("c")
    return x, y, c


def _chip_of(x, y):
    return 2 * x + y


def _gather_weights(w_in, w_out, w_pg, w_pe, conv_w):
    halves = [(D_MODEL // 2, W_IN_COLS), (W_ROWS // 2, D_MODEL), (W_ROWS // 2, D_MODEL), (D_PLE // 2, W_PE_COLS)]

    def body(win_ref, wout_ref, wpg_ref, wpe_ref, cw_ref,
             gin_ref, gout_ref, gpg_ref, gpe_ref, gcw_ref,
             s0, s1, s2, s3, b0, b1, b2, b3, lsem, send_sems, recv_sems, cw_send, cw_recv):
        x, y, c = _place()
        me = _chip_of(x, y)
        sibling = (x, y, 1 - c)
        chips = [(1 - x, y), (x, 1 - y), (1 - x, 1 - y)]
        srcs = (win_ref, wout_ref, wpg_ref, wpe_ref)
        stage = (s0, s1, s2, s3)
        bf = (b0, b1, b2, b3)
        outs = (gin_ref, gout_ref, gpg_ref, gpe_ref)
        loads = []
        for n in range(4):
            rows = halves[n][0]
            cp = pltpu.make_async_copy(srcs[n].at[pl.ds(c * rows, rows), :], stage[n], lsem.at[n])
            cp.start()
            loads.append(cp)
        own_cw = pltpu.make_async_copy(cw_ref, gcw_ref.at[me], lsem.at[4])
        own_cw.start()
        for n in range(4):
            loads[n].wait()
            bf[n][...] = stage[n][...].astype(BF16)

        def copy(n, k, chip, to, src=None):
            dst = outs[n].at[chip, c]
            return pltpu.make_async_remote_copy(
                src_ref=dst if src is None else src, dst_ref=dst,
                send_sem=send_sems.at[n, k], recv_sem=recv_sems.at[n, k], device_id=to, device_id_type=MESH)

        def recv(n, k, chip, core):
            dst = outs[n].at[chip, core]
            return pltpu.make_async_remote_copy(
                src_ref=dst, dst_ref=dst, send_sem=send_sems.at[n, k], recv_sem=recv_sems.at[n, k],
                device_id=sibling, device_id_type=MESH)

        sends = []
        locals_ = []
        for n in range(4):
            lc = pltpu.make_async_copy(bf[n], outs[n].at[me, c], lsem.at[5 + n])
            lc.start()
            locals_.append(lc)
            first = [copy(n, 0, me, sibling, src=bf[n])]
            first += [copy(n, 1 + j, me, (*chip, c), src=bf[n]) for j, chip in enumerate(chips)]
            for cp in first:
                cp.start()
            sends += first
        cws = []
        for j, chip in enumerate(chips):
            cp = pltpu.make_async_remote_copy(
                src_ref=cw_ref, dst_ref=gcw_ref.at[me], send_sem=cw_send.at[j], recv_sem=cw_recv.at[j],
                device_id=(*chip, c), device_id_type=MESH)
            cp.start()
            cws.append(cp)
        for n in range(4):
            for j, chip in enumerate(chips):
                kj = _chip_of(*chip)
                recv(n, 1 + j, kj, c).wait_recv()
                fw = copy(n, 4 + j, kj, sibling)
                fw.start()
                sends.append(fw)
        for n in range(4):
            recv(n, 0, me, 1 - c).wait_recv()
            for j, chip in enumerate(chips):
                recv(n, 4 + j, _chip_of(*chip), 1 - c).wait_recv()
        for j, chip in enumerate(chips):
            pltpu.make_async_remote_copy(
                src_ref=cw_ref, dst_ref=gcw_ref.at[_chip_of(*chip)], send_sem=cw_send.at[j], recv_sem=cw_recv.at[j],
                device_id=(*chip, c), device_id_type=MESH).wait_recv()
        for cp in sends + cws:
            cp.wait_send()
        for lc in locals_:
            lc.wait()
        own_cw.wait()

    out_shape = [jax.ShapeDtypeStruct((N_CHIPS, 2) + hs, BF16) for hs in halves]
    out_shape.append(jax.ShapeDtypeStruct((N_CHIPS, CONV_W, CONV_COLS), F32))
    scratch = [pltpu.VMEM(hs, F32) for hs in halves] + [pltpu.VMEM(hs, BF16) for hs in halves]
    scratch += [pltpu.SemaphoreType.DMA((9,)), pltpu.SemaphoreType.DMA((4, 7)), pltpu.SemaphoreType.DMA((4, 7)),
                pltpu.SemaphoreType.DMA((3,)), pltpu.SemaphoreType.DMA((3,))]
    return pl.pallas_call(
        body, name="gather_weights", in_specs=[ANY] * 5, out_specs=[ANY] * 5, out_shape=out_shape,
        scratch_shapes=scratch, compiler_params=pltpu.CompilerParams(vmem_limit_bytes=VMEM_LIMIT),
    )(w_in, w_out, w_pg, w_pe, conv_w)


HBM = pl.BlockSpec(memory_space=pltpu.HBM)
SEM = pl.BlockSpec(memory_space=pltpu.SEMAPHORE)
EFFECT = pltpu.SideEffectType.DATAFLOW_SIDE_EFFECTING


def _hbm(a):
    return pltpu.with_memory_space_constraint(a, pltpu.HBM)


def _landing(shape, dtype):
    return _hbm(lax.empty(shape, dtype))


def _exchange_start(name, arrays, ncopies, build):
    n = len(arrays)

    def body(*refs):
        ins, send_sems, recv_sems, token = refs[:n], refs[n], refs[n + 1], refs[-1]
        for cp in build(ins, send_sems, recv_sems):
            cp.start()
        token[...] = jnp.zeros_like(token)

    outs = pl.pallas_call(
        body, name=name,
        out_shape=(pltpu.SemaphoreType.DMA((ncopies,)), pltpu.SemaphoreType.DMA((ncopies,)),
                   *[pltpu.HBM(a.shape, a.dtype) for a in arrays], jax.ShapeDtypeStruct((SUBLANES, LANES), F32)),
        in_specs=[HBM] * n, out_specs=(SEM, SEM, *[HBM] * n, pl.BlockSpec(memory_space=pltpu.VMEM)),
        input_output_aliases={q: q + 2 for q in range(n)},
        compiler_params=pltpu.CompilerParams(has_side_effects=EFFECT),
    )(*[_hbm(a) for a in arrays])
    return (outs[0], outs[1], list(outs[2:2 + n])), outs[-1]


def _exchange_wait(name, started, after, build):
    send, recv, arrays = started
    n = len(arrays)

    def body(*refs):
        ins, send_sems, recv_sems = refs[:n], refs[n], refs[n + 1]
        for cp in build(ins, send_sems, recv_sems):
            cp.wait_send()
            cp.wait_recv()

    return pl.pallas_call(
        body, name=name, out_shape=tuple(pltpu.HBM(a.shape, a.dtype) for a in arrays),
        in_specs=[HBM] * n + [SEM, SEM, ANY], out_specs=tuple([HBM] * n),
        input_output_aliases={q: q for q in range(n)},
        compiler_params=pltpu.CompilerParams(has_side_effects=EFFECT),
    )(*arrays, send, recv, after)


def _sibling_copies(n):
    def build(refs, send_sems, recv_sems):
        x, y, c = _place()
        return [pltpu.make_async_remote_copy(
            src_ref=refs[b].at[:, 1 - c], dst_ref=refs[n + b], send_sem=send_sems.at[b], recv_sem=recv_sems.at[b],
            device_id=(x, y, 1 - c), device_id_type=MESH) for b in range(n)]
    return build


def _chip_copies(n):
    def build(refs, send_sems, recv_sems):
        x, y, c = _place()
        chips = [(1 - x, y), (x, 1 - y), (1 - x, 1 - y)]
        return [pltpu.make_async_remote_copy(
            src_ref=refs[b].at[_chip_of(*chip)], dst_ref=refs[n + b].at[j],
            send_sem=send_sems.at[3 * b + j], recv_sem=recv_sems.at[3 * b + j],
            device_id=(*chip, c), device_id_type=MESH) for b in range(n) for j, chip in enumerate(chips)]
    return build


def _finish_copies(n, with_small):
    def build(refs, send_sems, recv_sems):
        x, y, c = _place()
        cps = [pltpu.make_async_remote_copy(
            src_ref=refs[b].at[c], dst_ref=refs[b].at[c], send_sem=send_sems.at[b], recv_sem=recv_sems.at[b],
            device_id=(x, y, 1 - c), device_id_type=MESH) for b in range(n)]
        if with_small:
            mine = refs[n].at[_chip_of(x, y), c]
            flips = [(fx, fy, fc) for fx in (0, 1) for fy in (0, 1) for fc in (0, 1)][1:]
            cps += [pltpu.make_async_remote_copy(
                src_ref=mine, dst_ref=mine, send_sem=send_sems.at[n + q], recv_sem=recv_sems.at[n + q],
                device_id=(x ^ fx, y ^ fy, c ^ fc), device_id_type=MESH) for q, (fx, fy, fc) in enumerate(flips)]
        return cps
    return build


def _pair_sum(g, r1, kc, name, tr, send_dtype):
    nk, _, rows, cols = g.shape

    def body(kc_ref, g_ref, r_ref, p_ref, own_ref):
        s = g_ref[...] + r_ref[...]
        p_ref[...] = s.astype(send_dtype)

        @pl.when(pl.program_id(1) == kc_ref[0])
        def _():
            own_ref[...] = s

    grid_spec = pltpu.PrefetchScalarGridSpec(
        num_scalar_prefetch=1, grid=(rows // tr, nk),
        in_specs=[pl.BlockSpec((None, None, tr, cols), lambda r, k, kc: (k, kc[1], r, 0)),
                  pl.BlockSpec((None, tr, cols), lambda r, k, kc: (k, r, 0))],
        out_specs=[pl.BlockSpec((None, tr, cols), lambda r, k, kc: (k, r, 0)),
                   pl.BlockSpec((tr, cols), lambda r, k, kc: (r, 0))])
    return pl.pallas_call(
        body, name=name, grid_spec=grid_spec,
        out_shape=[jax.ShapeDtypeStruct((nk, rows, cols), send_dtype), jax.ShapeDtypeStruct((rows, cols), F32)],
        compiler_params=_params("arbitrary", "arbitrary"),
    )(kc, g, r1)


def _allreduce_vector(v, token):
    rows = v.shape[0]

    def body(v_ref, token_ref, o_ref, all_s, send_sems, recv_sems):
        x, y, c = _place()
        me = 2 * _chip_of(x, y) + c
        all_s[me] = v_ref[...]
        flips = [(fx, fy, fc) for fx in (0, 1) for fy in (0, 1) for fc in (0, 1)][1:]
        cps = []
        for q, (fx, fy, fc) in enumerate(flips):
            cp = pltpu.make_async_remote_copy(
                src_ref=v_ref, dst_ref=all_s.at[me], send_sem=send_sems.at[q], recv_sem=recv_sems.at[q],
                device_id=(x ^ fx, y ^ fy, c ^ fc), device_id_type=MESH)
            cp.start()
            cps.append(cp)
        for cp in cps:
            cp.wait()
        s = all_s[0]
        for d in range(1, 8):
            s = s + all_s[d]
        o_ref[...] = s

    vm = pl.BlockSpec(memory_space=pltpu.VMEM)
    return pl.pallas_call(
        body, name="allreduce_vector", in_specs=[vm, vm], out_specs=vm, out_shape=jax.ShapeDtypeStruct(v.shape, F32),
        scratch_shapes=[pltpu.VMEM((8, rows, LANES), F32), pltpu.SemaphoreType.DMA((7,)), pltpu.SemaphoreType.DMA((7,))],
    )(v, token)


def _chip_sum(own, r2, slot, lead, name, tr):
    rows, cols = own.shape
    nl = len(lead)

    def body(slot_ref, o_ref, r_ref, s_ref):
        s = o_ref[...]
        for j in range(3):
            s = s + r_ref[j].astype(F32)
        s_ref[...] = s

    grid_spec = pltpu.PrefetchScalarGridSpec(
        num_scalar_prefetch=1, grid=(rows // tr,),
        in_specs=[pl.BlockSpec((tr, cols), lambda r, sl: (r, 0)), pl.BlockSpec((3, tr, cols), lambda r, sl: (0, r, 0))],
        out_specs=pl.BlockSpec((None,) * nl + (tr, cols), lambda r, sl: tuple(sl[q] for q in range(nl)) + (r, 0)))
    return pl.pallas_call(
        body, name=name, grid_spec=grid_spec, out_shape=jax.ShapeDtypeStruct(tuple(lead) + (rows, cols), F32),
        compiler_params=_params("arbitrary"),
    )(slot, own, r2)


def _adamw(w, g, m, v, name, tr):
    rows, cols = w.shape

    def body(w_ref, g_ref, m_ref, v_ref, d_ref, nm_ref, nv_ref):
        gv = g_ref[...]
        nm = ADAM_B1 * m_ref[...] + (1.0 - ADAM_B1) * gv
        nv = ADAM_B2 * v_ref[...] + (1.0 - ADAM_B2) * (gv * gv)
        m_hat = nm / (1.0 - ADAM_B1 ** ADAM_STEP)
        v_hat = nv / (1.0 - ADAM_B2 ** ADAM_STEP)
        d_ref[...] = -ADAM_LR * (m_hat / (jnp.sqrt(v_hat) + ADAM_EPS) + ADAM_WD * w_ref[...])
        nm_ref[...] = nm
        nv_ref[...] = nv

    spec = pl.BlockSpec((tr, cols), lambda r: (r, 0))
    return pl.pallas_call(
        body, name=name, grid=(rows // tr,), in_specs=[spec] * 4, out_specs=[spec] * 3,
        out_shape=[jax.ShapeDtypeStruct((rows, cols), F32)] * 3,
        compiler_params=_params("parallel"),
    )(w, g, m, v)


def _rows128(a):
    return a.reshape(-1, LANES)


def _pack_small(parts):
    pieces = [_rows128(parts[n]) for n, _ in SMALL_ROWS]
    pieces.append(jnp.zeros((SMALL_TOTAL - SMALL_USED, LANES), F32))
    return jnp.concatenate(pieces, axis=0)


def _unpack_small(packed, shapes):
    out, at = {}, 0
    for n, r in SMALL_ROWS:
        out[n] = packed[at:at + r].reshape(shapes[n])
        at += r
    return out


def kernel(x, p, pre_g, w_in, gmlp_ln_g, gmlp_ln_b, gmlp_ws, gmlp_bs, conv_w, conv_b, w_a, b_a, w_x, b_x, lam, gmlp_out_g, lru_out_g, w_out, post_g, w_pe, w_pg, loss_target, m_pre_g, m_w_in, m_gmlp_ln_g, m_gmlp_ln_b, m_gmlp_ws, m_gmlp_bs, m_conv_w, m_conv_b, m_w_a, m_b_a, m_w_x, m_b_x, m_lam, m_gmlp_out_g, m_lru_out_g, m_w_out, m_post_g, m_w_pe, m_w_pg, v_pre_g, v_w_in, v_gmlp_ln_g, v_gmlp_ln_b, v_gmlp_ws, v_gmlp_bs, v_conv_w, v_conv_b, v_w_a, v_b_a, v_w_x, v_b_x, v_lam, v_gmlp_out_g, v_lru_out_g, v_w_out, v_post_g, v_w_pe, v_w_pg):
    weights = dict(pre_g=pre_g, w_in=w_in, gmlp_ln_g=gmlp_ln_g, gmlp_ln_b=gmlp_ln_b, gmlp_ws=gmlp_ws, gmlp_bs=gmlp_bs,
                   conv_w=conv_w, conv_b=conv_b, w_a=w_a, b_a=b_a, w_x=w_x, b_x=b_x, lam=lam, gmlp_out_g=gmlp_out_g,
                   lru_out_g=lru_out_g, w_out=w_out, post_g=post_g, w_pe=w_pe, w_pg=w_pg)
    mom_m = dict(pre_g=m_pre_g, w_in=m_w_in, gmlp_ln_g=m_gmlp_ln_g, gmlp_ln_b=m_gmlp_ln_b, gmlp_ws=m_gmlp_ws,
                 gmlp_bs=m_gmlp_bs, conv_w=m_conv_w, conv_b=m_conv_b, w_a=m_w_a, b_a=m_b_a, w_x=m_w_x, b_x=m_b_x,
                 lam=m_lam, gmlp_out_g=m_gmlp_out_g, lru_out_g=m_lru_out_g, w_out=m_w_out, post_g=m_post_g,
                 w_pe=m_w_pe, w_pg=m_w_pg)
    mom_v = dict(pre_g=v_pre_g, w_in=v_w_in, gmlp_ln_g=v_gmlp_ln_g, gmlp_ln_b=v_gmlp_ln_b, gmlp_ws=v_gmlp_ws,
                 gmlp_bs=v_gmlp_bs, conv_w=v_conv_w, conv_b=v_conv_b, w_a=v_w_a, b_a=v_b_a, w_x=v_w_x, b_x=v_b_x,
                 lam=v_lam, gmlp_out_g=v_gmlp_out_g, lru_out_g=v_lru_out_g, w_out=v_w_out, post_g=v_post_g,
                 w_pe=v_w_pe, w_pg=v_w_pg)
    order = list(weights)
    xi, yi, ci = _place()
    me = _chip_of(xi, yi)
    kc = jnp.stack([me, ci]).astype(jnp.int32)

    x2 = x[0]
    p2 = p[0, 0]
    tgt = loss_target[0]

    g_in, g_out, g_pg, g_pe, g_cw = _gather_weights(w_in[0], w_out[0], w_pg[0], w_pe[0], conv_w[0, :, 0, :])
    wg_in = g_in.reshape(N_CHIPS, D_MODEL, W_IN_COLS)
    wg_out = g_out.reshape(D_MODEL, D_MODEL)
    wg_pg = g_pg.reshape(D_MODEL, D_MODEL)
    wg_pe = g_pe.reshape(N_CHIPS, D_PLE, W_PE_COLS)
    cw_full = jnp.transpose(g_cw, (1, 0, 2)).reshape(CONV_W, D_HALF)

    causal = jnp.tril(jnp.ones((CHUNK, CHUNK), dtype=bool))
    ws_m = jnp.where(causal[None], gmlp_ws[0], 0.0)
    prm = dict(
        ln_g=gmlp_ln_g, ln_b=gmlp_ln_b, wt=ws_m.astype(BF16), wtt=jnp.transpose(ws_m, (0, 2, 1)).astype(BF16),
        bsx=jnp.repeat(jnp.transpose(gmlp_bs[0]), CHUNK, axis=1),
        conv_w=cw_full, conv_b=conv_b, w_a=w_a[0].astype(BF16), w_x=w_x[0].astype(BF16),
        b_a=b_a[0].reshape(1, D_HALF), b_x=b_x[0].reshape(1, D_HALF), lam=lam, oga=gmlp_out_g, ogb=lru_out_g)

    z, hn, y, h = _inproj_branches_fwd(x2, pre_g, wg_in, prm, 256)
    o, h1, gt, dout, loss_acc = _outproj_fwd(x2, y, p2, tgt, post_g, wg_out, wg_pg, wg_pe, 256)
    loss = lax.psum(loss_acc[0, 0], ("x", "y", "c"))

    def sibling_start(tag, bufs):
        lands = [_landing((b.shape[0],) + b.shape[2:], b.dtype) for b in bufs]
        return _exchange_start("sibling_start_" + tag, bufs + lands, len(bufs), _sibling_copies(len(bufs)))

    def pair_then_chip_start(tag, started, after, names, tiles, dtypes):
        n = len(names)
        got = _exchange_wait("sibling_wait_" + tag, started, after, _sibling_copies(n))
        pairs = [_pair_sum(got[b], got[n + b], kc, "pair_sum_" + names[b], tiles[b], dtypes[b]) for b in range(n)]
        lands = [_landing((3,) + pr[0].shape[1:], pr[0].dtype) for pr in pairs]
        return _exchange_start("chip_start_" + tag, [pr[0] for pr in pairs] + lands, 3 * n, _chip_copies(n)), pairs

    def sum_then_finish_start(tag, started, pairs, after, names, tiles, small):
        n = len(names)
        got = _exchange_wait("chip_wait_" + tag, started, after, _chip_copies(n))
        sums = [_chip_sum(pairs[b][1], got[n + b], kc if small and b == n - 1 else kc[1:],
                          (N_CHIPS, 2) if small and b == n - 1 else (2,), "chip_sum_" + names[b], tiles[b])
                for b in range(n)]
        nbig = n - 1 if small else n
        return _exchange_start("finish_start_" + tag, sums, nbig + (7 if small else 0), _finish_copies(nbig, small))

    gw_pe, dq, dh1, do, dy, g_post = _head_bwd(dout, gt, p2, o, post_g, wg_out, wg_pg, wg_pe, 256)
    gw_pe = gw_pe.reshape(N_CHIPS, 2, D_PLE // 2, W_PE_COLS)
    token0 = jnp.zeros((SUBLANES, LANES), F32)
    gw_out = _weight_grad(y, do, "grad_w_out", 2, 1, D_MODEL // 2, D_MODEL, 1024, token0)
    gw_pg = _weight_grad(h1, dq, "grad_w_pg", 2, 1, D_MODEL // 2, D_MODEL, 1024, token0)
    gw_out = gw_out.reshape(N_CHIPS, 2, W_ROWS // 2, D_MODEL)
    gw_pg = gw_pg.reshape(N_CHIPS, 2, W_ROWS // 2, D_MODEL)

    names_a, tiles_a = ["w_out", "w_pg", "w_pe"], [128, 128, 128]
    st, tok = sibling_start("a", [gw_out, gw_pg, gw_pe])
    (dz, g_oga, g_ogb, g_lng, g_lnb, g_bsx, g_ws, g_cw, g_cb, g_wa, g_ba, g_wx, g_bx, g_lam) = _branches_bwd(
        z, h, dy, prm, 256, tok)
    (st, tok), pairs_a = pair_then_chip_start("a", st, dz, names_a, tiles_a, [BF16] * 3)
    gw_in = _weight_grad(hn, dz, "grad_w_in", 2, N_CHIPS, D_MODEL // 2, W_IN_COLS, 1024, tok)
    fin_a, tok = sum_then_finish_start("a", st, pairs_a, gw_in, names_a, tiles_a, False)

    small_g = dict(
        gmlp_ln_g=g_lng[0:1], gmlp_ln_b=g_lnb[0:1], gmlp_ws=g_ws,
        gmlp_bs=jnp.transpose(g_bsx[:, ::CHUNK]), conv_w=g_cw[::SUBLANES], conv_b=g_cb[0:1], w_a=g_wa, b_a=g_ba[0:1],
        w_x=g_wx, b_x=g_bx[0:1], lam=g_lam[0:1], gmlp_out_g=g_oga[0:1], lru_out_g=g_ogb[0:1], post_g=g_post[0:1])
    gsm = _pack_small(small_g).reshape(N_CHIPS, 2, SMALL_PIECE, LANES)

    names_b, tiles_b = ["w_in", "small"], [256, SMALL_PIECE]
    half = x2.shape[0] // 512 // 2
    st, tok_b = sibling_start("b", [gw_in, gsm])
    part = _inproj_bwd(dz, wg_in, x2, dh1, pre_g, 512, 0, half, None, False, tok_b, "inproj_bwd_lo")
    f_out, f_pg, f_pe = _exchange_wait("finish_wait_a", fin_a, part[1], _finish_copies(3, False))
    (st, tok_b), pairs_b = pair_then_chip_start("b", st, part[1], names_b, tiles_b, [BF16, F32])
    grad_x, g_pre = _inproj_bwd(dz, wg_in, x2, dh1, pre_g, 512, half, half, part, True, tok_b, "inproj_bwd_hi")
    fin_b, tok_b = sum_then_finish_start("b", st, pairs_b, g_pre, names_b, tiles_b, True)
    g_pre_sum = _allreduce_vector(_rows128(g_pre[0:1]), tok_b)
    f_in, f_sm = _exchange_wait("finish_wait_b", fin_b, g_pre_sum, _finish_copies(1, True))

    big_g = dict(w_in=f_in.reshape(D_MODEL, W_IN_COLS), w_out=f_out.reshape(W_ROWS, D_MODEL),
                 w_pg=f_pg.reshape(W_ROWS, D_MODEL), w_pe=f_pe.reshape(D_PLE, W_PE_COLS))
    grads, deltas, new_m, new_v = {}, {}, {}, {}
    for n, tr in (("w_in", 256), ("w_out", 128), ("w_pg", 128), ("w_pe", 128)):
        shp = weights[n].shape
        grads[n] = big_g[n].reshape(shp)
        d, nm, nv = _adamw(weights[n][0], big_g[n], mom_m[n][0], mom_v[n][0], "adamw_" + n, tr)
        deltas[n], new_m[n], new_v[n] = d.reshape(shp), nm.reshape(shp), nv.reshape(shp)

    packed_g = f_sm.reshape(SMALL_TOTAL, LANES)
    small_names = [n for n, _ in SMALL_ROWS]
    shapes = {n: weights[n].shape for n in small_names}
    shapes["conv_w"] = (CONV_W, D_HALF)
    zero_cw = jnp.zeros((CONV_W, D_HALF), F32)
    pack_w = lambda src: _pack_small({n: (zero_cw if n == "conv_w" else src[n]) for n in small_names})
    d_sm, m_sm, v_sm = _adamw(pack_w(weights), packed_g, pack_w(mom_m), pack_w(mom_v), "adamw_small", SMALL_PIECE)
    ug, ud, um, uv = (_unpack_small(a, shapes) for a in (packed_g, d_sm, m_sm, v_sm))
    for n in small_names:
        if n != "conv_w":
            grads[n], deltas[n], new_m[n], new_v[n] = ug[n], ud[n], um[n], uv[n]
    g_conv = lax.dynamic_slice_in_dim(ug["conv_w"], me * CONV_COLS, CONV_COLS, axis=1)
    d, nm, nv = _adamw(conv_w[0, :, 0, :], g_conv, m_conv_w[0, :, 0, :], v_conv_w[0, :, 0, :], "adamw_conv_w", CONV_W)
    cshape = conv_w.shape
    grads["conv_w"], deltas["conv_w"] = g_conv.reshape(cshape), d.reshape(cshape)
    new_m["conv_w"], new_v["conv_w"] = nm.reshape(cshape), nv.reshape(cshape)
    d, nm, nv = _adamw(_rows128(pre_g), g_pre_sum, _rows128(m_pre_g), _rows128(v_pre_g), "adamw_pre_g", 16)
    pshape = pre_g.shape
    grads["pre_g"], deltas["pre_g"] = g_pre_sum.reshape(pshape), d.reshape(pshape)
    new_m["pre_g"], new_v["pre_g"] = nm.reshape(pshape), nv.reshape(pshape)

    return (loss, grad_x.reshape(x.shape), *[grads[n] for n in order], *[deltas[n] for n in order],
            *[new_m[n] for n in order], *[new_v[n] for n in order])
```

```python
import functools
import math

import jax
import jax.numpy as jnp
from jax import lax
from jax.experimental import pallas as pl
from jax.experimental.pallas import tpu as pltpu

F32 = jnp.float32
BF16 = jnp.bfloat16

D_MODEL = 2048
D_HALF = 1024
D_Z = 5120
D_PLE = 256
CHUNK = 128
N_HEADS = 8
N_CHIPS = 4
W_IN_COLS = D_Z // N_CHIPS
W_ROWS = D_MODEL // N_CHIPS
W_PE_COLS = D_MODEL // N_CHIPS
CONV_W = 4
CONV_COLS = D_HALF // N_CHIPS
EPS = 1e-6
LRU_C = 8.0
ADAM_LR, ADAM_B1, ADAM_B2, ADAM_EPS, ADAM_WD, ADAM_STEP = 0.001, 0.9, 0.999, 1e-08, 0.01, 10

SUBLANES = 8
LANES = 128
VMEM_LIMIT = 56 * 1024 * 1024

SMALL_ROWS = (("gmlp_ln_g", 8), ("gmlp_ln_b", 8), ("gmlp_ws", 1024), ("gmlp_bs", 8),
              ("conv_w", 32), ("conv_b", 8), ("w_a", 1024), ("b_a", 8), ("w_x", 1024), ("b_x", 8),
              ("lam", 8), ("gmlp_out_g", 8), ("lru_out_g", 8), ("post_g", 16))
SMALL_USED = sum(r for _, r in SMALL_ROWS)
SMALL_PIECE = 400
SMALL_TOTAL = 8 * SMALL_PIECE

MESH = pl.DeviceIdType.MESH
ANY = pl.BlockSpec(memory_space=pl.ANY)

_GELU_C0 = math.sqrt(2.0 / math.pi)
_GELU_C1 = 0.044715


def _params(*sem):
    return pltpu.CompilerParams(dimension_semantics=sem, vmem_limit_bytes=VMEM_LIMIT)


def _dot(a, b):
    return jnp.dot(a, b, preferred_element_type=F32)


def _dot_nt(a, b):
    return lax.dot_general(a, b, (((1,), (1,)), ((), ())), preferred_element_type=F32)


def _dot_tn(a, b):
    return lax.dot_general(a, b, (((0,), (0,)), ((), ())), preferred_element_type=F32)


def _gelu(x):
    t = jnp.tanh(_GELU_C0 * (x + _GELU_C1 * (x * x * x)))
    return 0.5 * x * (1.0 + t), t


def _gelu_grad(x, t):
    return 0.5 * (1.0 + t) + 0.5 * x * (1.0 - t * t) * (_GELU_C0 * (1.0 + 3.0 * _GELU_C1 * x * x))


def _rowsum8(v):
    r, n = v.shape
    return jnp.sum(v.reshape(r // SUBLANES, SUBLANES, n), axis=0)


def _lanemean(v):
    return jnp.mean(v, axis=-1, keepdims=True)


def _shift_down(v, halo8, k):
    if k == 0:
        return v
    r = pltpu.roll(v, k, 0)
    hr = pltpu.roll(halo8, k, 0)
    row = lax.broadcasted_iota(jnp.int32, halo8.shape, 0)
    top = jnp.where(row < k, hr, r[0:SUBLANES])
    return jnp.concatenate([top, r[SUBLANES:]], axis=0)


def _shift_up(v, next8, k):
    if k == 0:
        return v
    n = v.shape[0]
    r = pltpu.roll(v, n - k, 0)
    nr = pltpu.roll(next8, SUBLANES - k, 0)
    row = lax.broadcasted_iota(jnp.int32, next8.shape, 0)
    bot = jnp.where(row >= SUBLANES - k, nr, r[n - SUBLANES:])
    return jnp.concatenate([r[:n - SUBLANES], bot], axis=0)


def _layernorm_parts(vg):
    mu = _lanemean(vg)
    xc = vg - mu
    rstd = lax.rsqrt(_lanemean(xc * xc) + EPS)
    return xc * rstd, rstd


def _spatial_mix(wt_ref, vn_ref, bsx_ref, mixed_ref, tm):
    for c in range(tm // CHUNK):
        rows = slice(c * CHUNK, (c + 1) * CHUNK)
        for h in range(N_HEADS):
            cols = slice(h * CHUNK, (h + 1) * CHUNK)
            mixed_ref[rows, cols] = _dot(wt_ref[h], vn_ref[rows, cols]) + bsx_ref[:, cols]


def _conv_taps(xb, halo8):
    return [_shift_down(xb, halo8, CONV_W - 1 - k) for k in range(CONV_W)]


def _lru_gates(xc_bf_ref, wa_ref, wx_ref, ba_ref, bx_ref, r_ref, i_ref):
    for h in range(N_HEADS):
        cols = slice(h * CHUNK, (h + 1) * CHUNK)
        xh = xc_bf_ref[:, cols]
        r_ref[:, cols] = jax.nn.sigmoid(_dot(xh, wa_ref[h]) + ba_ref[:, cols])
        i_ref[:, cols] = jax.nn.sigmoid(_dot(xh, wx_ref[h]) + bx_ref[:, cols])


def _softplus_neg(lam):
    return jnp.maximum(-lam, 0.0) + jnp.log(1.0 + jnp.exp(-jnp.abs(lam)))


def _decay_parts(r, lam):
    la = (-LRU_C * _softplus_neg(lam)) * r
    a = jnp.exp(la)
    th = -jnp.tanh(la)
    mult = jnp.sqrt(2.0 * th / (1.0 + th))
    return a, mult


def _inproj_branches_fwd(x, pre_g, wg_in, prm, tm):
    t = x.shape[0]
    nt = t // tm
    hb = tm // SUBLANES

    def body(x_ref, g_ref, w_ref,
             lng_ref, lnb_ref, wt_ref, bsx_ref, cw_ref, cb_ref, wa_ref, wx_ref, ba_ref, bx_ref, lam_ref,
             oga_ref, ogb_ref,
             z_ref, hn_ref, y_ref, h_ref,
             zbuf0, zbuf1, vn_s, mixed_s, xcbf_s, r_s, i_s, halo_s, carry_s):
        s = pl.program_id(0)

        @pl.when(s == 0)
        def _():
            zbuf1[...] = jnp.zeros_like(zbuf1)

        @pl.when(s <= 1)
        def _():
            carry_s[...] = jnp.zeros_like(carry_s)
            halo_s[...] = jnp.zeros_like(halo_s)

        @pl.when(s % 2 == 0)
        def _():
            step(s, zbuf0, zbuf1, x_ref, g_ref, w_ref, lng_ref, lnb_ref, wt_ref, bsx_ref, cw_ref, cb_ref, wa_ref,
                 wx_ref, ba_ref, bx_ref, lam_ref, oga_ref, ogb_ref, z_ref, hn_ref, y_ref, h_ref, vn_s, mixed_s,
                 xcbf_s, r_s, i_s, halo_s, carry_s)

        @pl.when(s % 2 == 1)
        def _():
            step(s, zbuf1, zbuf0, x_ref, g_ref, w_ref, lng_ref, lnb_ref, wt_ref, bsx_ref, cw_ref, cb_ref, wa_ref,
                 wx_ref, ba_ref, bx_ref, lam_ref, oga_ref, ogb_ref, z_ref, hn_ref, y_ref, h_ref, vn_s, mixed_s,
                 xcbf_s, r_s, i_s, halo_s, carry_s)

    def step(s, zw, zr, x_ref, g_ref, w_ref, lng_ref, lnb_ref, wt_ref, bsx_ref, cw_ref, cb_ref, wa_ref, wx_ref,
             ba_ref, bx_ref, lam_ref, oga_ref, ogb_ref, z_ref, hn_ref, y_ref, h_ref, vn_s, mixed_s, xcbf_s, r_s,
             i_s, halo_s, carry_s):
        xv = x_ref[...]
        hn = (xv * lax.rsqrt(_lanemean(xv * xv) + EPS) * g_ref[...]).astype(BF16)
        hn_ref[...] = hn

        def project(j):
            cols = slice(j * W_IN_COLS, (j + 1) * W_IN_COLS)
            zb = _dot(hn_ref[...], w_ref[j]).astype(BF16)
            z_ref[:, cols] = zb
            zw[:, cols] = zb

        project(0)
        zin = lambda g: zr[:, g * D_HALF:(g + 1) * D_HALF].astype(F32)
        ug, _ = _gelu(zin(0))
        vg, _ = _gelu(zin(1))
        vhat, _ = _layernorm_parts(vg)
        vn_s[...] = (vhat * lng_ref[...] + lnb_ref[...]).astype(BF16)
        _spatial_mix(wt_ref, vn_s, bsx_ref, mixed_s, tm)
        ga = zin(2)
        ya = ug * mixed_s[...] * (ga * jax.nn.sigmoid(ga))
        ra = lax.rsqrt(_lanemean(ya * ya) + EPS)
        y_ref[:, 0:D_HALF] = (ya * ra * oga_ref[...]).astype(BF16)

        project(1)
        xb = zin(3)
        taps = _conv_taps(xb, halo_s[...])
        halo_s[...] = xb[tm - SUBLANES:]
        xc = cb_ref[...] + taps[0] * cw_ref[0:1, :]
        for k in range(1, CONV_W):
            xc = xc + taps[k] * cw_ref[k:k + 1, :]
        xcbf_s[...] = xc.astype(BF16)
        _lru_gates(xcbf_s, wa_ref, wx_ref, ba_ref, bx_ref, r_s, i_s)
        a, mult = _decay_parts(r_s[...], lam_ref[...])
        row = lax.broadcasted_iota(jnp.int32, a.shape, 0)
        mult = jnp.where(jnp.logical_and(s == 1, row == 0), 1.0, mult)
        b = mult * (i_s[...] * xc)
        project(2)
        r8 = row & (SUBLANES - 1)
        for d in (1, 2, 4):
            a_sh = pltpu.roll(a, d, 0)
            b_sh = pltpu.roll(b, d, 0)
            m = r8 >= d
            b = jnp.where(m, a * b_sh + b, b)
            a = jnp.where(m, a * a_sh, a)
        project(3)
        carry = carry_s[...]
        for g in range(hb):
            rows = slice(g * SUBLANES, (g + 1) * SUBLANES)
            hg = a[rows] * carry + b[rows]
            h_ref[rows, :] = hg
            carry = jnp.broadcast_to(hg[SUBLANES - 1:SUBLANES, :], hg.shape)
        carry_s[...] = carry
        gb = zin(4)
        yb = h_ref[...] * (gb * jax.nn.sigmoid(gb))
        rb = lax.rsqrt(_lanemean(yb * yb) + EPS)
        y_ref[:, D_HALF:] = (yb * rb * ogb_ref[...]).astype(BF16)

    const = lambda a: pl.BlockSpec(a.shape, lambda s, n=a.ndim: (0,) * n, pipeline_mode=pl.Buffered(1))
    proj = lambda n: pl.BlockSpec((tm, n), lambda s: (jnp.minimum(s, nt - 1), 0))
    head = lambda n: pl.BlockSpec((tm, n), lambda s: (jnp.maximum(s - 1, 0), 0))
    names = ("ln_g", "ln_b", "wt", "bsx", "conv_w", "conv_b", "w_a", "w_x", "b_a", "b_x", "lam", "oga", "ogb")
    pr = [prm[n] for n in names]
    big = lambda dt: pltpu.VMEM((tm, D_HALF), dt)
    return pl.pallas_call(
        body, name="inproj_branches_fwd", grid=(nt + 1,),
        in_specs=[proj(D_MODEL), const(pre_g), const(wg_in)] + [const(a) for a in pr],
        out_specs=[proj(D_Z), proj(D_MODEL), head(D_MODEL), head(D_HALF)],
        out_shape=[jax.ShapeDtypeStruct((t, D_Z), BF16), jax.ShapeDtypeStruct((t, D_MODEL), BF16),
                   jax.ShapeDtypeStruct((t, D_MODEL), BF16), jax.ShapeDtypeStruct((t, D_HALF), F32)],
        scratch_shapes=[pltpu.VMEM((tm, D_Z), BF16), pltpu.VMEM((tm, D_Z), BF16),
                        big(BF16), big(F32), big(BF16), big(F32), big(F32),
                        pltpu.VMEM((SUBLANES, D_HALF), F32), pltpu.VMEM((SUBLANES, D_HALF), F32)],
        compiler_params=_params("arbitrary"),
    )(x, pre_g, wg_in, *pr)


def _outproj_fwd(x, y, p, tgt, post_g, w_out, w_pg, wg_pe, tm):
    t = x.shape[0]

    def body(x_ref, y_ref, p_ref, tgt_ref, pg_ref, wo_ref, wpg_ref, wpe_ref,
             o_ref, h1_ref, gt_ref, dout_ref, loss_ref):
        @pl.when(pl.program_id(0) == 0)
        def _():
            loss_ref[...] = jnp.zeros_like(loss_ref)

        o = _dot(y_ref[...], wo_ref[...])
        o_ref[...] = o
        r3 = lax.rsqrt(_lanemean(o * o) + EPS)
        h1 = x_ref[...] + (o * r3) * pg_ref[...]
        h1b = h1.astype(BF16)
        h1_ref[...] = h1b
        gt = jax.nn.sigmoid(_dot(h1b, wpg_ref[...]))
        gt_ref[...] = gt
        pb = p_ref[...].astype(BF16)
        for k in range(N_CHIPS):
            cols = slice(k * W_PE_COLS, (k + 1) * W_PE_COLS)
            pe = _dot(pb, wpe_ref[k])
            d = h1[:, cols] + pe * gt[:, cols] - tgt_ref[:, cols]
            dout_ref[:, cols] = d * (1.0 / D_MODEL)
            loss_ref[...] += jnp.sum(d * d) * (0.5 / D_MODEL)

    row = lambda n: pl.BlockSpec((tm, n), lambda i: (i, 0))
    const = lambda shp: pl.BlockSpec(shp, lambda i, n=len(shp): (0,) * n, pipeline_mode=pl.Buffered(1))
    return pl.pallas_call(
        body, name="outproj_fwd", grid=(t // tm,),
        in_specs=[row(D_MODEL), row(D_MODEL), row(D_PLE), row(D_MODEL), const((1, D_MODEL)),
                  const((D_MODEL, D_MODEL)), const((D_MODEL, D_MODEL)), const((N_CHIPS, D_PLE, W_PE_COLS))],
        out_specs=[row(D_MODEL), row(D_MODEL), row(D_MODEL), row(D_MODEL),
                   pl.BlockSpec((SUBLANES, LANES), lambda i: (0, 0))],
        out_shape=[jax.ShapeDtypeStruct((t, D_MODEL), F32), jax.ShapeDtypeStruct((t, D_MODEL), BF16),
                   jax.ShapeDtypeStruct((t, D_MODEL), F32), jax.ShapeDtypeStruct((t, D_MODEL), F32),
                   jax.ShapeDtypeStruct((SUBLANES, LANES), F32)],
        compiler_params=_params("arbitrary"),
    )(x, y, p, tgt, post_g, w_out, w_pg, wg_pe)


def _head_bwd(dout, gt, p, o, post_g, w_out, w_pg, wg_pe, tm):
    t = dout.shape[0]

    def body(dout_ref, gt_ref, p_ref, o_ref, pg_ref, wo_ref, wpg_ref, wpe_ref,
             gwpe_ref, dq_ref, dh1_ref, do_ref, dy_ref, gpost_ref):
        i = pl.program_id(0)

        @pl.when(i == 0)
        def _():
            gpost_ref[...] = jnp.zeros_like(gpost_ref)
            gwpe_ref[...] = jnp.zeros_like(gwpe_ref)

        dout = dout_ref[...]
        gt = gt_ref[...]
        pb = p_ref[...].astype(BF16)
        for k in range(N_CHIPS):
            cols = slice(k * W_PE_COLS, (k + 1) * W_PE_COLS)
            pe = _dot(pb, wpe_ref[k])
            g = gt[:, cols]
            dg = dout[:, cols] * g
            gwpe_ref[k] += _dot_tn(pb, dg.astype(BF16))
            dq_ref[:, cols] = (dg * pe * (1.0 - g)).astype(BF16)
        dh1 = dout + _dot_nt(dq_ref[...], wpg_ref[...])
        dh1_ref[...] = dh1
        o = o_ref[...]
        r3 = lax.rsqrt(_lanemean(o * o) + EPS)
        on = o * r3
        gpost_ref[...] += _rowsum8(dh1 * on)
        don = dh1 * pg_ref[...]
        do = r3 * (don - on * _lanemean(don * on))
        dob = do.astype(BF16)
        do_ref[...] = dob
        dy_ref[...] = _dot_nt(dob, wo_ref[...])

        @pl.when(i == pl.num_programs(0) - 1)
        def _():
            gpost_ref[...] = jnp.broadcast_to(jnp.sum(gpost_ref[...], axis=0, keepdims=True), gpost_ref.shape)

    row = lambda n: pl.BlockSpec((tm, n), lambda i: (i, 0))
    const = lambda shp: pl.BlockSpec(shp, lambda i, n=len(shp): (0,) * n, pipeline_mode=pl.Buffered(1))
    return pl.pallas_call(
        body, name="head_bwd", grid=(t // tm,),
        in_specs=[row(D_MODEL), row(D_MODEL), row(D_PLE), row(D_MODEL), const((1, D_MODEL)),
                  const((D_MODEL, D_MODEL)), const((D_MODEL, D_MODEL)), const((N_CHIPS, D_PLE, W_PE_COLS))],
        out_specs=[pl.BlockSpec((N_CHIPS, D_PLE, W_PE_COLS), lambda i: (0, 0, 0)),
                   row(D_MODEL), row(D_MODEL), row(D_MODEL), row(D_MODEL),
                   pl.BlockSpec((SUBLANES, D_MODEL), lambda i: (0, 0))],
        out_shape=[jax.ShapeDtypeStruct((N_CHIPS, D_PLE, W_PE_COLS), F32), jax.ShapeDtypeStruct((t, D_MODEL), BF16),
                   jax.ShapeDtypeStruct((t, D_MODEL), F32), jax.ShapeDtypeStruct((t, D_MODEL), BF16),
                   jax.ShapeDtypeStruct((t, D_MODEL), F32), jax.ShapeDtypeStruct((SUBLANES, D_MODEL), F32)],
        compiler_params=_params("arbitrary"),
    )(dout, gt, p, o, post_g, w_out, w_pg, wg_pe)


def _branches_bwd(z, h, dy, prm, tm, token):
    t = z.shape[0]
    nt = t // tm
    hb = tm // SUBLANES

    def body(u_ref, v_ref, ga_ref, xb_ref, gb_ref, xbh_ref, h_ref, hh_ref, dy_ref,
             lng_ref, lnb_ref, wt_ref, wtt_ref, bsx_ref, cw_ref, cb_ref, wa_ref, wx_ref, ba_ref, bx_ref, lam_ref,
             oga_ref, ogb_ref, token_ref,
             dz_ref, g_oga, g_ogb, g_lng, g_lnb, g_bsx, g_ws, g_cw, g_cb, g_wa, g_ba, g_wx, g_bx, g_lam,
             vn_s, mixed_s, dm_s, dvn_s, xcbf_s, r_s, i_s, a_s, b_s, dh_s, dpr_s, dpi_s, dxc_s,
             ca_s, cd_s, cx_s):
        step_i = pl.program_id(0)
        tile = nt - 1 - step_i
        accs = (g_oga, g_ogb, g_lng, g_lnb, g_bsx, g_ws, g_cw, g_cb, g_wa, g_ba, g_wx, g_bx, g_lam)

        @pl.when(step_i == 0)
        def _():
            for r in accs + (ca_s, cd_s, cx_s):
                r[...] = jnp.zeros_like(r)

        dy_a = dy_ref[:, 0:D_HALF]
        dy_b = dy_ref[:, D_HALF:]

        u = u_ref[...].astype(F32)
        ug, tu = _gelu(u)
        v = v_ref[...].astype(F32)
        vg, tv = _gelu(v)
        vhat, rstd = _layernorm_parts(vg)
        vn_s[...] = (vhat * lng_ref[...] + lnb_ref[...]).astype(BF16)
        _spatial_mix(wt_ref, vn_s, bsx_ref, mixed_s, tm)
        mixed = mixed_s[...]
        ga = ga_ref[...].astype(F32)
        sga = jax.nn.sigmoid(ga)
        sa = ga * sga
        um = ug * mixed
        ya = um * sa
        ra = lax.rsqrt(_lanemean(ya * ya) + EPS)
        yahat = ya * ra
        g_oga[...] += _rowsum8(dy_a * yahat)
        dn = dy_a * oga_ref[...]
        dya = ra * (dn - yahat * _lanemean(dn * yahat))
        dz_ref[:, 2 * D_HALF:3 * D_HALF] = (dya * um * (sga * (1.0 + ga * (1.0 - sga)))).astype(BF16)
        dz_ref[:, 0:D_HALF] = (dya * mixed * sa * _gelu_grad(u, tu)).astype(BF16)
        dmixed = dya * ug * sa
        g_bsx[...] += jnp.sum(dmixed.reshape(tm // CHUNK, CHUNK, D_HALF), axis=0)
        dm_s[...] = dmixed.astype(BF16)
        for c in range(tm // CHUNK):
            rows = slice(c * CHUNK, (c + 1) * CHUNK)
            for hd in range(N_HEADS):
                cols = slice(hd * CHUNK, (hd + 1) * CHUNK)
                dmh = dm_s[rows, cols]
                dvn_s[rows, cols] = _dot(wtt_ref[hd], dmh)
                g_ws[hd] += _dot_nt(dmh, vn_s[rows, cols])
        dvn = dvn_s[...]
        g_lng[...] += _rowsum8(dvn * vhat)
        g_lnb[...] += _rowsum8(dvn)
        dvh = dvn * lng_ref[...]
        dvg = rstd * (dvh - _lanemean(dvh) - vhat * _lanemean(dvh * vhat))
        dz_ref[:, D_HALF:2 * D_HALF] = (dvg * _gelu_grad(v, tv)).astype(BF16)

        xb = xb_ref[...].astype(F32)
        halo = jnp.where(tile == 0, 0.0, xbh_ref[...].astype(F32)[SUBLANES:])
        taps = _conv_taps(xb, halo)
        xc = cb_ref[...] + taps[0] * cw_ref[0:1, :]
        for k in range(1, CONV_W):
            xc = xc + taps[k] * cw_ref[k:k + 1, :]
        xcbf_s[...] = xc.astype(BF16)
        _lru_gates(xcbf_s, wa_ref, wx_ref, ba_ref, bx_ref, r_s, i_s)
        rg = r_s[...]
        ig = i_s[...]
        lam = lam_ref[...]
        a, mult_true = _decay_parts(rg, lam)
        row = lax.broadcasted_iota(jnp.int32, a.shape, 0)
        first = jnp.logical_and(tile == 0, row == 0)
        mult = jnp.where(first, 1.0, mult_true)
        hcur = h_ref[...]
        hprev = _shift_down(hcur, jnp.where(tile == 0, 0.0, hh_ref[...]), 1)
        gb = gb_ref[...].astype(F32)
        sgb = jax.nn.sigmoid(gb)
        sb = gb * sgb
        yb = hcur * sb
        rb = lax.rsqrt(_lanemean(yb * yb) + EPS)
        ybhat = yb * rb
        g_ogb[...] += _rowsum8(dy_b * ybhat)
        dn = dy_b * ogb_ref[...]
        dyb = rb * (dn - ybhat * _lanemean(dn * ybhat))
        dz_ref[:, 4 * D_HALF:5 * D_HALF] = (dyb * hcur * (sgb * (1.0 + gb * (1.0 - sgb)))).astype(BF16)

        an = _shift_up(a, ca_s[...], 1)
        bb = dyb * sb
        r8 = row & (SUBLANES - 1)
        for d in (1, 2, 4):
            a_sh = pltpu.roll(an, tm - d, 0)
            b_sh = pltpu.roll(bb, tm - d, 0)
            m = r8 + d < SUBLANES
            bb = jnp.where(m, an * b_sh + bb, bb)
            an = jnp.where(m, an * a_sh, an)
        a_s[...] = an
        b_s[...] = bb

        def step(g, carry):
            sl = pl.ds(pl.multiple_of((hb - 1 - g) * SUBLANES, SUBLANES), SUBLANES)
            dg = a_s[sl, :] * carry + b_s[sl, :]
            dh_s[sl, :] = dg
            return jnp.broadcast_to(dg[0:1, :], dg.shape)

        cd_s[...] = lax.fori_loop(0, hb, step, cd_s[...])
        ca_s[...] = jnp.broadcast_to(a[0:1, :], ca_s.shape)
        dh = dh_s[...]
        da = dh * hprev
        gx = ig * xc
        dla = da * a - jnp.where(first, 0.0, dh * gx * (a * a / mult_true))
        g_lam[...] += _rowsum8(dla * rg)
        dr = dla * (-LRU_C * _softplus_neg(lam))
        dpr = dr * rg * (1.0 - rg)
        dpi = (dh * mult * xc) * ig * (1.0 - ig)
        g_ba[...] += _rowsum8(dpr)
        g_bx[...] += _rowsum8(dpi)
        dpr_s[...] = dpr.astype(BF16)
        dpi_s[...] = dpi.astype(BF16)
        for hd in range(N_HEADS):
            cols = slice(hd * CHUNK, (hd + 1) * CHUNK)
            xh = xcbf_s[:, cols]
            dprh = dpr_s[:, cols]
            dpih = dpi_s[:, cols]
            g_wa[hd] += _dot_tn(xh, dprh)
            g_wx[hd] += _dot_tn(xh, dpih)
            dxc_s[:, cols] = _dot_nt(dprh, wa_ref[hd]) + _dot_nt(dpih, wx_ref[hd])
        dxc = dxc_s[...] + dh * mult * ig
        g_cb[...] += _rowsum8(dxc)
        for k in range(CONV_W):
            g_cw[k * SUBLANES:(k + 1) * SUBLANES, :] += _rowsum8(dxc * taps[k])
        nxt = cx_s[...]
        dxb = dxc * cw_ref[CONV_W - 1:CONV_W, :]
        for j in range(1, CONV_W):
            dxb = dxb + _shift_up(dxc, nxt, j) * cw_ref[CONV_W - 1 - j:CONV_W - j, :]
        dz_ref[:, 3 * D_HALF:4 * D_HALF] = dxb.astype(BF16)
        cx_s[...] = dxc[0:SUBLANES]

        @pl.when(step_i == nt - 1)
        def _():
            for r in (g_oga, g_ogb, g_lng, g_lnb, g_cb, g_ba, g_bx):
                r[...] = jnp.broadcast_to(jnp.sum(r[...], axis=0, keepdims=True), r.shape)
            lam_f = LRU_C * jax.nn.sigmoid(-lam_ref[...])
            g_lam[...] = jnp.broadcast_to(jnp.sum(g_lam[...], axis=0, keepdims=True) * lam_f, g_lam.shape)
            for k in range(CONV_W):
                blk = g_cw[k * SUBLANES:(k + 1) * SUBLANES, :]
                g_cw[k * SUBLANES:(k + 1) * SUBLANES, :] = jnp.broadcast_to(jnp.sum(blk, axis=0, keepdims=True), blk.shape)
            tri = (lax.broadcasted_iota(jnp.int32, (CHUNK, CHUNK), 0) >= lax.broadcasted_iota(jnp.int32, (CHUNK, CHUNK), 1))
            for hd in range(N_HEADS):
                cols = slice(hd * CHUNK, (hd + 1) * CHUNK)
                g_ws[hd] = jnp.where(tri, g_ws[hd], 0.0)
                blk = g_bsx[:, cols]
                g_bsx[:, cols] = jnp.broadcast_to(jnp.sum(blk, axis=1, keepdims=True), blk.shape)

    rev = lambda i: nt - 1 - i
    zspec = lambda g: pl.BlockSpec((tm, D_HALF), lambda i, g=g: (rev(i), g))
    halo = lambda col: pl.BlockSpec((SUBLANES, D_HALF), lambda i: (jnp.maximum(rev(i) * hb - 1, 0), col))
    zhalo = pl.BlockSpec((2 * SUBLANES, D_HALF), lambda i: (jnp.maximum(rev(i) * (hb // 2) - 1, 0), 3))
    full = lambda a: pl.BlockSpec(a.shape, lambda i, n=a.ndim: (0,) * n)
    acc = lambda shp: pl.BlockSpec(shp, lambda i, n=len(shp): (0,) * n)
    names = ("ln_g", "ln_b", "wt", "wtt", "bsx", "conv_w", "conv_b", "w_a", "w_x", "b_a", "b_x", "lam", "oga", "ogb")
    pr = [prm[n] for n in names] + [token]
    vec = (SUBLANES, D_HALF)
    mat = (N_HEADS, CHUNK, CHUNK)
    acc_shapes = [vec, vec, vec, vec, (CHUNK, D_HALF), mat, (CONV_W * SUBLANES, D_HALF), vec, mat, vec, mat, vec, vec]
    big = lambda dt: pltpu.VMEM((tm, D_HALF), dt)
    return pl.pallas_call(
        body, name="branches_bwd", grid=(nt,),
        in_specs=[zspec(0), zspec(1), zspec(2), zspec(3), zspec(4), zhalo,
                  pl.BlockSpec((tm, D_HALF), lambda i: (rev(i), 0)), halo(0),
                  pl.BlockSpec((tm, D_MODEL), lambda i: (rev(i), 0))] + [full(a) for a in pr],
        out_specs=[pl.BlockSpec((tm, D_Z), lambda i: (rev(i), 0))] + [acc(s) for s in acc_shapes],
        out_shape=[jax.ShapeDtypeStruct((t, D_Z), BF16)] + [jax.ShapeDtypeStruct(s, F32) for s in acc_shapes],
        scratch_shapes=[big(BF16), big(F32), big(BF16), big(F32), big(BF16), big(F32), big(F32), big(F32), big(F32),
                        big(F32), big(BF16), big(BF16), big(F32),
                        pltpu.VMEM(vec, F32), pltpu.VMEM(vec, F32), pltpu.VMEM(vec, F32)],
        compiler_params=_params("arbitrary"),
    )(z, z, z, z, z, z, h, h, dy, *pr)


def _inproj_bwd(dz, wg_in, x, dh1, pre_g, tm, tile0, nt, prev, last, token, name):
    t = x.shape[0]

    def body(*refs):
        dz_ref, w_ref, x_ref, dh1_ref, g_ref = refs[:5]
        gx_ref, gpre_ref, acc_s = refs[-3:]
        i = pl.program_id(0)

        @pl.when(i == 0)
        def _():
            gpre_ref[...] = jnp.zeros_like(gpre_ref) if prev is None else refs[7][...]

        acc = _dot_nt(dz_ref[:, 0:W_IN_COLS], w_ref[0])
        for k in range(1, N_CHIPS):
            acc = acc + _dot_nt(dz_ref[:, k * W_IN_COLS:(k + 1) * W_IN_COLS], w_ref[k])
        acc_s[...] = acc
        for s in range(tm // CHUNK):
            rows = slice(s * CHUNK, (s + 1) * CHUNK)
            xv = x_ref[rows, :]
            r = lax.rsqrt(_lanemean(xv * xv) + EPS)
            xhat = xv * r
            dhn = acc_s[rows, :]
            gpre_ref[...] += _rowsum8(dhn * xhat)
            dxh = dhn * g_ref[...]
            gx_ref[rows, :] = dh1_ref[rows, :] + r * (dxh - xhat * _lanemean(dxh * xhat))

        if last:
            @pl.when(i == nt - 1)
            def _():
                gpre_ref[...] = jnp.broadcast_to(jnp.sum(gpre_ref[...], axis=0, keepdims=True), gpre_ref.shape)

    row = lambda n: pl.BlockSpec((tm, n), lambda i: (tile0 + i, 0))
    small = lambda r: pl.BlockSpec((r, D_MODEL), lambda i: (0, 0))
    tok = pl.BlockSpec((SUBLANES, LANES), lambda i: (0, 0))
    in_specs = [row(D_Z), pl.BlockSpec(wg_in.shape, lambda i: (0, 0, 0), pipeline_mode=pl.Buffered(1)),
                row(D_MODEL), row(D_MODEL), small(1), tok]
    args = [dz, wg_in, x, dh1, pre_g, token]
    aliases = {}
    if prev is not None:
        in_specs += [ANY, small(SUBLANES)]
        args += list(prev)
        aliases = {6: 0}
    return pl.pallas_call(
        body, name=name, grid=(nt,), in_specs=in_specs, out_specs=[row(D_MODEL), small(SUBLANES)],
        out_shape=[jax.ShapeDtypeStruct((t, D_MODEL), F32), jax.ShapeDtypeStruct((SUBLANES, D_MODEL), F32)],
        input_output_aliases=aliases,
        scratch_shapes=[pltpu.VMEM((tm, D_MODEL), F32)],
        compiler_params=_params("arbitrary"),
    )(*args)


def _weight_grad(a, b, name, kb, nb, tk, tn, tt, token):
    t = a.shape[0]
    tt = min(tt, t)

    def body(a_ref, b_ref, token_ref, o_ref):
        @pl.when(pl.program_id(2) == 0)
        def _():
            o_ref[...] = jnp.zeros_like(o_ref)

        o_ref[...] += _dot_tn(a_ref[...], b_ref[...])

    return pl.pallas_call(
        body, name=name, grid=(nb, kb, t // tt),
        in_specs=[pl.BlockSpec((tt, tk), lambda j, i, s: (s, i)), pl.BlockSpec((tt, tn), lambda j, i, s: (s, j)),
                  pl.BlockSpec((SUBLANES, LANES), lambda j, i, s: (0, 0))],
        out_specs=pl.BlockSpec((None, None, tk, tn), lambda j, i, s: (j, i, 0, 0)),
        out_shape=jax.ShapeDtypeStruct((nb, kb, tk, tn), F32),
        compiler_params=_params("parallel", "parallel", "arbitrary"),
    )(a, b, token)


def _place():
    x, y, c = lax.axis_index("x"), lax.axis_index("y"), lax.axis_index("c")
    return x, y, c


def _chip_of(x, y):
    return 2 * x + y


def _gather_weights(w_in, w_out, w_pg, w_pe, conv_w):
    halves = [(D_MODEL // 2, W_IN_COLS), (W_ROWS // 2, D_MODEL), (W_ROWS // 2, D_MODEL), (D_PLE // 2, W_PE_COLS)]

    def body(win_ref, wout_ref, wpg_ref, wpe_ref, cw_ref,
             gin_ref, gout_ref, gpg_ref, gpe_ref, gcw_ref,
             s0, s1, s2, s3, b0, b1, b2, b3, lsem, send_sems, recv_sems, cw_send, cw_recv):
        x, y, c = _place()
        me = _chip_of(x, y)
        sibling = (x, y, 1 - c)
        chips = [(1 - x, y), (x, 1 - y), (1 - x, 1 - y)]
        srcs = (win_ref, wout_ref, wpg_ref, wpe_ref)
        stage = (s0, s1, s2, s3)
        bf = (b0, b1, b2, b3)
        outs = (gin_ref, gout_ref, gpg_ref, gpe_ref)
        loads = []
        for n in range(4):
            rows = halves[n][0]
            cp = pltpu.make_async_copy(srcs[n].at[pl.ds(c * rows, rows), :], stage[n], lsem.at[n])
            cp.start()
            loads.append(cp)
        own_cw = pltpu.make_async_copy(cw_ref, gcw_ref.at[me], lsem.at[4])
        own_cw.start()
        for n in range(4):
            loads[n].wait()
            bf[n][...] = stage[n][...].astype(BF16)

        def copy(n, k, chip, to, src=None):
            dst = outs[n].at[chip, c]
            return pltpu.make_async_remote_copy(
                src_ref=dst if src is None else src, dst_ref=dst,
                send_sem=send_sems.at[n, k], recv_sem=recv_sems.at[n, k], device_id=to, device_id_type=MESH)

        def recv(n, k, chip, core):
            dst = outs[n].at[chip, core]
            return pltpu.make_async_remote_copy(
                src_ref=dst, dst_ref=dst, send_sem=send_sems.at[n, k], recv_sem=recv_sems.at[n, k],
                device_id=sibling, device_id_type=MESH)

        sends = []
        locals_ = []
        for n in range(4):
            lc = pltpu.make_async_copy(bf[n], outs[n].at[me, c], lsem.at[5 + n])
            lc.start()
            locals_.append(lc)
            first = [copy(n, 0, me, sibling, src=bf[n])]
            first += [copy(n, 1 + j, me, (*chip, c), src=bf[n]) for j, chip in enumerate(chips)]
            for cp in first:
                cp.start()
            sends += first
        cws = []
        for j, chip in enumerate(chips):
            cp = pltpu.make_async_remote_copy(
                src_ref=cw_ref, dst_ref=gcw_ref.at[me], send_sem=cw_send.at[j], recv_sem=cw_recv.at[j],
                device_id=(*chip, c), device_id_type=MESH)
            cp.start()
            cws.append(cp)
        for n in range(4):
            for j, chip in enumerate(chips):
                kj = _chip_of(*chip)
                recv(n, 1 + j, kj, c).wait_recv()
                fw = copy(n, 4 + j, kj, sibling)
                fw.start()
                sends.append(fw)
        for n in range(4):
            recv(n, 0, me, 1 - c).wait_recv()
            for j, chip in enumerate(chips):
                recv(n, 4 + j, _chip_of(*chip), 1 - c).wait_recv()
        for j, chip in enumerate(chips):
            pltpu.make_async_remote_copy(
                src_ref=cw_ref, dst_ref=gcw_ref.at[_chip_of(*chip)], send_sem=cw_send.at[j], recv_sem=cw_recv.at[j],
                device_id=(*chip, c), device_id_type=MESH).wait_recv()
        for cp in sends + cws:
            cp.wait_send()
        for lc in locals_:
            lc.wait()
        own_cw.wait()

    out_shape = [jax.ShapeDtypeStruct((N_CHIPS, 2) + hs, BF16) for hs in halves]
    out_shape.append(jax.ShapeDtypeStruct((N_CHIPS, CONV_W, CONV_COLS), F32))
    scratch = [pltpu.VMEM(hs, F32) for hs in halves] + [pltpu.VMEM(hs, BF16) for hs in halves]
    scratch += [pltpu.SemaphoreType.DMA((9,)), pltpu.SemaphoreType.DMA((4, 7)), pltpu.SemaphoreType.DMA((4, 7)),
                pltpu.SemaphoreType.DMA((3,)), pltpu.SemaphoreType.DMA((3,))]
    return pl.pallas_call(
        body, name="gather_weights", in_specs=[ANY] * 5, out_specs=[ANY] * 5, out_shape=out_shape,
        scratch_shapes=scratch, compiler_params=pltpu.CompilerParams(vmem_limit_bytes=VMEM_LIMIT),
    )(w_in, w_out, w_pg, w_pe, conv_w)


HBM = pl.BlockSpec(memory_space=pltpu.HBM)
SEM = pl.BlockSpec(memory_space=pltpu.SEMAPHORE)
EFFECT = pltpu.SideEffectType.DATAFLOW_SIDE_EFFECTING


def _hbm(a):
    return pltpu.with_memory_space_constraint(a, pltpu.HBM)


def _landing(shape, dtype):
    return _hbm(lax.empty(shape, dtype))


def _exchange_start(name, arrays, ncopies, build):
    n = len(arrays)

    def body(*refs):
        ins, send_sems, recv_sems, token = refs[:n], refs[n], refs[n + 1], refs[-1]
        for cp in build(ins, send_sems, recv_sems):
            cp.start()
        token[...] = jnp.zeros_like(token)

    outs = pl.pallas_call(
        body, name=name,
        out_shape=(pltpu.SemaphoreType.DMA((ncopies,)), pltpu.SemaphoreType.DMA((ncopies,)),
                   *[pltpu.HBM(a.shape, a.dtype) for a in arrays], jax.ShapeDtypeStruct((SUBLANES, LANES), F32)),
        in_specs=[HBM] * n, out_specs=(SEM, SEM, *[HBM] * n, pl.BlockSpec(memory_space=pltpu.VMEM)),
        input_output_aliases={q: q + 2 for q in range(n)},
        compiler_params=pltpu.CompilerParams(has_side_effects=EFFECT),
    )(*[_hbm(a) for a in arrays])
    return (outs[0], outs[1], list(outs[2:2 + n])), outs[-1]


def _exchange_wait(name, started, after, build):
    send, recv, arrays = started
    n = len(arrays)

    def body(*refs):
        ins, send_sems, recv_sems = refs[:n], refs[n], refs[n + 1]
        for cp in build(ins, send_sems, recv_sems):
            cp.wait_send()
            cp.wait_recv()

    return pl.pallas_call(
        body, name=name, out_shape=tuple(pltpu.HBM(a.shape, a.dtype) for a in arrays),
        in_specs=[HBM] * n + [SEM, SEM, ANY], out_specs=tuple([HBM] * n),
        input_output_aliases={q: q for q in range(n)},
        compiler_params=pltpu.CompilerParams(has_side_effects=EFFECT),
    )(*arrays, send, recv, after)


def _sibling_copies(n):
    def build(refs, send_sems, recv_sems):
        x, y, c = _place()
        return [pltpu.make_async_remote_copy(
            src_ref=refs[b].at[:, 1 - c], dst_ref=refs[n + b], send_sem=send_sems.at[b], recv_sem=recv_sems.at[b],
            device_id=(x, y, 1 - c), device_id_type=MESH) for b in range(n)]
    return build


def _chip_copies(n):
    def build(refs, send_sems, recv_sems):
        x, y, c = _place()
        chips = [(1 - x, y), (x, 1 - y), (1 - x, 1 - y)]
        return [pltpu.make_async_remote_copy(
            src_ref=refs[b].at[_chip_of(*chip)], dst_ref=refs[n + b].at[j],
            send_sem=send_sems.at[3 * b + j], recv_sem=recv_sems.at[3 * b + j],
            device_id=(*chip, c), device_id_type=MESH) for b in range(n) for j, chip in enumerate(chips)]
    return build


def _finish_copies(n, with_small):
    def build(refs, send_sems, recv_sems):
        x, y, c = _place()
        cps = [pltpu.make_async_remote_copy(
            src_ref=refs[b].at[c], dst_ref=refs[b].at[c], send_sem=send_sems.at[b], recv_sem=recv_sems.at[b],
            device_id=(x, y, 1 - c), device_id_type=MESH) for b in range(n)]
        if with_small:
            mine = refs[n].at[_chip_of(x, y), c]
            flips = [(fx, fy, fc) for fx in (0, 1) for fy in (0, 1) for fc in (0, 1)][1:]
            cps += [pltpu.make_async_remote_copy(
                src_ref=mine, dst_ref=mine, send_sem=send_sems.at[n + q], recv_sem=recv_sems.at[n + q],
                device_id=(x ^ fx, y ^ fy, c ^ fc), device_id_type=MESH) for q, (fx, fy, fc) in enumerate(flips)]
        return cps
    return build


def _pair_sum(g, r1, kc, name, tr, send_dtype):
    nk, _, rows, cols = g.shape

    def body(kc_ref, g_ref, r_ref, p_ref, own_ref):
        s = g_ref[...] + r_ref[...]
        p_ref[...] = s.astype(send_dtype)

        @pl.when(pl.program_id(1) == kc_ref[0])
        def _():
            own_ref[...] = s

    grid_spec = pltpu.PrefetchScalarGridSpec(
        num_scalar_prefetch=1, grid=(rows // tr, nk),
        in_specs=[pl.BlockSpec((None, None, tr, cols), lambda r, k, kc: (k, kc[1], r, 0)),
                  pl.BlockSpec((None, tr, cols), lambda r, k, kc: (k, r, 0))],
        out_specs=[pl.BlockSpec((None, tr, cols), lambda r, k, kc: (k, r, 0)),
                   pl.BlockSpec((tr, cols), lambda r, k, kc: (r, 0))])
    return pl.pallas_call(
        body, name=name, grid_spec=grid_spec,
        out_shape=[jax.ShapeDtypeStruct((nk, rows, cols), send_dtype), jax.ShapeDtypeStruct((rows, cols), F32)],
        compiler_params=_params("arbitrary", "arbitrary"),
    )(kc, g, r1)


def _allreduce_vector(v, token):
    rows = v.shape[0]

    def body(v_ref, token_ref, o_ref, all_s, send_sems, recv_sems):
        x, y, c = _place()
        me = 2 * _chip_of(x, y) + c
        all_s[me] = v_ref[...]
        flips = [(fx, fy, fc) for fx in (0, 1) for fy in (0, 1) for fc in (0, 1)][1:]
        cps = []
        for q, (fx, fy, fc) in enumerate(flips):
            cp = pltpu.make_async_remote_copy(
                src_ref=v_ref, dst_ref=all_s.at[me], send_sem=send_sems.at[q], recv_sem=recv_sems.at[q],
                device_id=(x ^ fx, y ^ fy, c ^ fc), device_id_type=MESH)
            cp.start()
            cps.append(cp)
        for cp in cps:
            cp.wait()
        s = all_s[0]
        for d in range(1, 8):
            s = s + all_s[d]
        o_ref[...] = s

    vm = pl.BlockSpec(memory_space=pltpu.VMEM)
    return pl.pallas_call(
        body, name="allreduce_vector", in_specs=[vm, vm], out_specs=vm, out_shape=jax.ShapeDtypeStruct(v.shape, F32),
        scratch_shapes=[pltpu.VMEM((8, rows, LANES), F32), pltpu.SemaphoreType.DMA((7,)), pltpu.SemaphoreType.DMA((7,))],
    )(v, token)


def _chip_sum(own, r2, slot, lead, name, tr):
    rows, cols = own.shape
    nl = len(lead)

    def body(slot_ref, o_ref, r_ref, s_ref):
        s = o_ref[...]
        for j in range(3):
            s = s + r_ref[j].astype(F32)
        s_ref[...] = s

    grid_spec = pltpu.PrefetchScalarGridSpec(
        num_scalar_prefetch=1, grid=(rows // tr,),
        in_specs=[pl.BlockSpec((tr, cols), lambda r, sl: (r, 0)), pl.BlockSpec((3, tr, cols), lambda r, sl: (0, r, 0))],
        out_specs=pl.BlockSpec((None,) * nl + (tr, cols), lambda r, sl: tuple(sl[q] for q in range(nl)) + (r, 0)))
    return pl.pallas_call(
        body, name=name, grid_spec=grid_spec, out_shape=jax.ShapeDtypeStruct(tuple(lead) + (rows, cols), F32),
        compiler_params=_params("arbitrary"),
    )(slot, own, r2)


def _adamw(w, g, m, v, name, tr):
    rows, cols = w.shape

    def body(w_ref, g_ref, m_ref, v_ref, d_ref, nm_ref, nv_ref):
        gv = g_ref[...]
        nm = ADAM_B1 * m_ref[...] + (1.0 - ADAM_B1) * gv
        nv = ADAM_B2 * v_ref[...] + (1.0 - ADAM_B2) * (gv * gv)
        m_hat = nm / (1.0 - ADAM_B1 ** ADAM_STEP)
        v_hat = nv / (1.0 - ADAM_B2 ** ADAM_STEP)
        d_ref[...] = -ADAM_LR * (m_hat / (jnp.sqrt(v_hat) + ADAM_EPS) + ADAM_WD * w_ref[...])
        nm_ref[...] = nm
        nv_ref[...] = nv

    spec = pl.BlockSpec((tr, cols), lambda r: (r, 0))
    return pl.pallas_call(
        body, name=name, grid=(rows // tr,), in_specs=[spec] * 4, out_specs=[spec] * 3,
        out_shape=[jax.ShapeDtypeStruct((rows, cols), F32)] * 3,
        compiler_params=_params("parallel"),
    )(w, g, m, v)


def _rows128(a):
    return a.reshape(-1, LANES)


def _pack_small(parts):
    pieces = [_rows128(parts[n]) for n, _ in SMALL_ROWS]
    pieces.append(jnp.zeros((SMALL_TOTAL - SMALL_USED, LANES), F32))
    return jnp.concatenate(pieces, axis=0)


def _unpack_small(packed, shapes):
    out, at = {}, 0
    for n, r in SMALL_ROWS:
        out[n] = packed[at:at + r].reshape(shapes[n])
        at += r
    return out


def kernel(x, p, pre_g, w_in, gmlp_ln_g, gmlp_ln_b, gmlp_ws, gmlp_bs, conv_w, conv_b, w_a, b_a, w_x, b_x, lam, gmlp_out_g, lru_out_g, w_out, post_g, w_pe, w_pg, loss_target, m_pre_g, m_w_in, m_gmlp_ln_g, m_gmlp_ln_b, m_gmlp_ws, m_gmlp_bs, m_conv_w, m_conv_b, m_w_a, m_b_a, m_w_x, m_b_x, m_lam, m_gmlp_out_g, m_lru_out_g, m_w_out, m_post_g, m_w_pe, m_w_pg, v_pre_g, v_w_in, v_gmlp_ln_g, v_gmlp_ln_b, v_gmlp_ws, v_gmlp_bs, v_conv_w, v_conv_b, v_w_a, v_b_a, v_w_x, v_b_x, v_lam, v_gmlp_out_g, v_lru_out_g, v_w_out, v_post_g, v_w_pe, v_w_pg):
    weights = dict(pre_g=pre_g, w_in=w_in, gmlp_ln_g=gmlp_ln_g, gmlp_ln_b=gmlp_ln_b, gmlp_ws=gmlp_ws, gmlp_bs=gmlp_bs,
                   conv_w=conv_w, conv_b=conv_b, w_a=w_a, b_a=b_a, w_x=w_x, b_x=b_x, lam=lam, gmlp_out_g=gmlp_out_g,
                   lru_out_g=lru_out_g, w_out=w_out, post_g=post_g, w_pe=w_pe, w_pg=w_pg)
    mom_m = dict(pre_g=m_pre_g, w_in=m_w_in, gmlp_ln_g=m_gmlp_ln_g, gmlp_ln_b=m_gmlp_ln_b, gmlp_ws=m_gmlp_ws,
                 gmlp_bs=m_gmlp_bs, conv_w=m_conv_w, conv_b=m_conv_b, w_a=m_w_a, b_a=m_b_a, w_x=m_w_x, b_x=m_b_x,
                 lam=m_lam, gmlp_out_g=m_gmlp_out_g, lru_out_g=m_lru_out_g, w_out=m_w_out, post_g=m_post_g,
                 w_pe=m_w_pe, w_pg=m_w_pg)
    mom_v = dict(pre_g=v_pre_g, w_in=v_w_in, gmlp_ln_g=v_gmlp_ln_g, gmlp_ln_b=v_gmlp_ln_b, gmlp_ws=v_gmlp_ws,
                 gmlp_bs=v_gmlp_bs, conv_w=v_conv_w, conv_b=v_conv_b, w_a=v_w_a, b_a=v_b_a, w_x=v_w_x, b_x=v_b_x,
                 lam=v_lam, gmlp_out_g=v_gmlp_out_g, lru_out_g=v_lru_out_g, w_out=v_w_out, post_g=v_post_g,
                 w_pe=v_w_pe, w_pg=v_w_pg)
    order = list(weights)
    xi, yi, ci = _place()
    me = _chip_of(xi, yi)
    kc = jnp.stack([me, ci]).astype(jnp.int32)

    x2 = x[0]
    p2 = p[0, 0]
    tgt = loss_target[0]

    g_in, g_out, g_pg, g_pe, g_cw = _gather_weights(w_in[0], w_out[0], w_pg[0], w_pe[0], conv_w[0, :, 0, :])
    wg_in = g_in.reshape(N_CHIPS, D_MODEL, W_IN_COLS)
    wg_out = g_out.reshape(D_MODEL, D_MODEL)
    wg_pg = g_pg.reshape(D_MODEL, D_MODEL)
    wg_pe = g_pe.reshape(N_CHIPS, D_PLE, W_PE_COLS)
    cw_full = jnp.transpose(g_cw, (1, 0, 2)).reshape(CONV_W, D_HALF)

    causal = jnp.tril(jnp.ones((CHUNK, CHUNK), dtype=bool))
    ws_m = jnp.where(causal[None], gmlp_ws[0], 0.0)
    prm = dict(
        ln_g=gmlp_ln_g, ln_b=gmlp_ln_b, wt=ws_m.astype(BF16), wtt=jnp.transpose(ws_m, (0, 2, 1)).astype(BF16),
        bsx=jnp.repeat(jnp.transpose(gmlp_bs[0]), CHUNK, axis=1),
        conv_w=cw_full, conv_b=conv_b, w_a=w_a[0].astype(BF16), w_x=w_x[0].astype(BF16),
        b_a=b_a[0].reshape(1, D_HALF), b_x=b_x[0].reshape(1, D_HALF), lam=lam, oga=gmlp_out_g, ogb=lru_out_g)

    z, hn, y, h = _inproj_branches_fwd(x2, pre_g, wg_in, prm, 256)
    o, h1, gt, dout, loss_acc = _outproj_fwd(x2, y, p2, tgt, post_g, wg_out, wg_pg, wg_pe, 256)
    loss = lax.psum(loss_acc[0, 0], ("x", "y", "c"))

    def sibling_start(tag, bufs):
        lands = [_landing((b.shape[0],) + b.shape[2:], b.dtype) for b in bufs]
        return _exchange_start("sibling_start_" + tag, bufs + lands, len(bufs), _sibling_copies(len(bufs)))

    def pair_then_chip_start(tag, started, after, names, tiles, dtypes):
        n = len(names)
        got = _exchange_wait("sibling_wait_" + tag, started, after, _sibling_copies(n))
        pairs = [_pair_sum(got[b], got[n + b], kc, "pair_sum_" + names[b], tiles[b], dtypes[b]) for b in range(n)]
        lands = [_landing((3,) + pr[0].shape[1:], pr[0].dtype) for pr in pairs]
        return _exchange_start("chip_start_" + tag, [pr[0] for pr in pairs] + lands, 3 * n, _chip_copies(n)), pairs

    def sum_then_finish_start(tag, started, pairs, after, names, tiles, small):
        n = len(names)
        got = _exchange_wait("chip_wait_" + tag, started, after, _chip_copies(n))
        sums = [_chip_sum(pairs[b][1], got[n + b], kc if small and b == n - 1 else kc[1:],
                          (N_CHIPS, 2) if small and b == n - 1 else (2,), "chip_sum_" + names[b], tiles[b])
                for b in range(n)]
        nbig = n - 1 if small else n
        return _exchange_start("finish_start_" + tag, sums, nbig + (7 if small else 0), _finish_copies(nbig, small))

    gw_pe, dq, dh1, do, dy, g_post = _head_bwd(dout, gt, p2, o, post_g, wg_out, wg_pg, wg_pe, 256)
    gw_pe = gw_pe.reshape(N_CHIPS, 2, D_PLE // 2, W_PE_COLS)
    token0 = jnp.zeros((SUBLANES, LANES), F32)
    gw_out = _weight_grad(y, do, "grad_w_out", 2, 1, D_MODEL // 2, D_MODEL, 1024, token0)
    gw_pg = _weight_grad(h1, dq, "grad_w_pg", 2, 1, D_MODEL // 2, D_MODEL, 1024, token0)
    gw_out = gw_out.reshape(N_CHIPS, 2, W_ROWS // 2, D_MODEL)
    gw_pg = gw_pg.reshape(N_CHIPS, 2, W_ROWS // 2, D_MODEL)

    names_a, tiles_a = ["w_out", "w_pg", "w_pe"], [128, 128, 128]
    st, tok = sibling_start("a", [gw_out, gw_pg, gw_pe])
    (dz, g_oga, g_ogb, g_lng, g_lnb, g_bsx, g_ws, g_cw, g_cb, g_wa, g_ba, g_wx, g_bx, g_lam) = _branches_bwd(
        z, h, dy, prm, 256, tok)
    (st, tok), pairs_a = pair_then_chip_start("a", st, dz, names_a, tiles_a, [BF16] * 3)
    gw_in = _weight_grad(hn, dz, "grad_w_in", 2, N_CHIPS, D_MODEL // 2, W_IN_COLS, 1024, tok)
    fin_a, tok = sum_then_finish_start("a", st, pairs_a, gw_in, names_a, tiles_a, False)

    small_g = dict(
        gmlp_ln_g=g_lng[0:1], gmlp_ln_b=g_lnb[0:1], gmlp_ws=g_ws,
        gmlp_bs=jnp.transpose(g_bsx[:, ::CHUNK]), conv_w=g_cw[::SUBLANES], conv_b=g_cb[0:1], w_a=g_wa, b_a=g_ba[0:1],
        w_x=g_wx, b_x=g_bx[0:1], lam=g_lam[0:1], gmlp_out_g=g_oga[0:1], lru_out_g=g_ogb[0:1], post_g=g_post[0:1])
    gsm = _pack_small(small_g).reshape(N_CHIPS, 2, SMALL_PIECE, LANES)

    names_b, tiles_b = ["w_in", "small"], [256, SMALL_PIECE]
    half = x2.shape[0] // 256 // 2
    st, tok_b = sibling_start("b", [gw_in, gsm])
    part = _inproj_bwd(dz, wg_in, x2, dh1, pre_g, 256, 0, half, None, False, tok_b, "inproj_bwd_lo")
    f_out, f_pg, f_pe = _exchange_wait("finish_wait_a", fin_a, part[1], _finish_copies(3, False))
    (st, tok_b), pairs_b = pair_then_chip_start("b", st, part[1], names_b, tiles_b, [BF16, F32])
    grad_x, g_pre = _inproj_bwd(dz, wg_in, x2, dh1, pre_g, 256, half, half, part, True, tok_b, "inproj_bwd_hi")
    fin_b, tok_b = sum_then_finish_start("b", st, pairs_b, g_pre, names_b, tiles_b, True)
    g_pre_sum = _allreduce_vector(_rows128(g_pre[0:1]), tok_b)
    f_in, f_sm = _exchange_wait("finish_wait_b", fin_b, g_pre_sum, _finish_copies(1, True))

    big_g = dict(w_in=f_in.reshape(D_MODEL, W_IN_COLS), w_out=f_out.reshape(W_ROWS, D_MODEL),
                 w_pg=f_pg.reshape(W_ROWS, D_MODEL), w_pe=f_pe.reshape(D_PLE, W_PE_COLS))
    grads, deltas, new_m, new_v = {}, {}, {}, {}
    for n, tr in (("w_in", 256), ("w_out", 128), ("w_pg", 128), ("w_pe", 128)):
        shp = weights[n].shape
        grads[n] = big_g[n].reshape(shp)
        d, nm, nv = _adamw(weights[n][0], big_g[n], mom_m[n][0], mom_v[n][0], "adamw_" + n, tr)
        deltas[n], new_m[n], new_v[n] = d.reshape(shp), nm.reshape(shp), nv.reshape(shp)

    packed_g = f_sm.reshape(SMALL_TOTAL, LANES)
    small_names = [n for n, _ in SMALL_ROWS]
    shapes = {n: weights[n].shape for n in small_names}
    shapes["conv_w"] = (CONV_W, D_HALF)
    zero_cw = jnp.zeros((CONV_W, D_HALF), F32)
    pack_w = lambda src: _pack_small({n: (zero_cw if n == "conv_w" else src[n]) for n in small_names})
    d_sm, m_sm, v_sm = _adamw(pack_w(weights), packed_g, pack_w(mom_m), pack_w(mom_v), "adamw_small", SMALL_PIECE)
    ug, ud, um, uv = (_unpack_small(a, shapes) for a in (packed_g, d_sm, m_sm, v_sm))
    for n in small_names:
        if n != "conv_w":
            grads[n], deltas[n], new_m[n], new_v[n] = ug[n], ud[n], um[n], uv[n]
    g_conv = lax.dynamic_slice_in_dim(ug["conv_w"], me * CONV_COLS, CONV_COLS, axis=1)
    d, nm, nv = _adamw(conv_w[0, :, 0, :], g_conv, m_conv_w[0, :, 0, :], v_conv_w[0, :, 0, :], "adamw_conv_w", CONV_W)
    cshape = conv_w.shape
    grads["conv_w"], deltas["conv_w"] = g_conv.reshape(cshape), d.reshape(cshape)
    new_m["conv_w"], new_v["conv_w"] = nm.reshape(cshape), nv.reshape(cshape)
    d, nm, nv = _adamw(_rows128(pre_g), g_pre_sum, _rows128(m_pre_g), _rows128(v_pre_g), "adamw_pre_g", 16)
    pshape = pre_g.shape
    grads["pre_g"], deltas["pre_g"] = g_pre_sum.reshape(pshape), d.reshape(pshape)
    new_m["pre_g"], new_v["pre_g"] = nm.reshape(pshape), nv.reshape(pshape)

    return (loss, grad_x.reshape(x.shape), *[grads[n] for n in order], *[deltas[n] for n in order],
            *[new_m[n] for n in order], *[new_v[n] for n in order])
```

```python
import functools
import math

import jax
import jax.numpy as jnp
from jax import lax
from jax.experimental import pallas as pl
from jax.experimental.pallas import tpu as pltpu

F32 = jnp.float32
BF16 = jnp.bfloat16

D_MODEL = 2048
D_HALF = 1024
D_Z = 5120
D_PLE = 256
CHUNK = 128
N_HEADS = 8
N_CHIPS = 4
W_IN_COLS = D_Z // N_CHIPS
W_ROWS = D_MODEL // N_CHIPS
W_PE_COLS = D_MODEL // N_CHIPS
CONV_W = 4
CONV_COLS = D_HALF // N_CHIPS
EPS = 1e-6
LRU_C = 8.0
ADAM_LR, ADAM_B1, ADAM_B2, ADAM_EPS, ADAM_WD, ADAM_STEP = 0.001, 0.9, 0.999, 1e-08, 0.01, 10

SUBLANES = 8
LANES = 128
VMEM_LIMIT = 56 * 1024 * 1024

SMALL_ROWS = (("gmlp_ln_g", 8), ("gmlp_ln_b", 8), ("gmlp_ws", 1024), ("gmlp_bs", 8),
              ("conv_w", 32), ("conv_b", 8), ("w_a", 1024), ("b_a", 8), ("w_x", 1024), ("b_x", 8),
              ("lam", 8), ("gmlp_out_g", 8), ("lru_out_g", 8), ("post_g", 16))
SMALL_USED = sum(r for _, r in SMALL_ROWS)
SMALL_PIECE = 400
SMALL_TOTAL = 8 * SMALL_PIECE

MESH = pl.DeviceIdType.MESH
ANY = pl.BlockSpec(memory_space=pl.ANY)

_GELU_C0 = math.sqrt(2.0 / math.pi)
_GELU_C1 = 0.044715


def _params(*sem):
    return pltpu.CompilerParams(dimension_semantics=sem, vmem_limit_bytes=VMEM_LIMIT)


def _dot(a, b):
    return jnp.dot(a, b, preferred_element_type=F32)


def _dot_nt(a, b):
    return lax.dot_general(a, b, (((1,), (1,)), ((), ())), preferred_element_type=F32)


def _dot_tn(a, b):
    return lax.dot_general(a, b, (((0,), (0,)), ((), ())), preferred_element_type=F32)


def _gelu(x):
    t = jnp.tanh(_GELU_C0 * (x + _GELU_C1 * (x * x * x)))
    return 0.5 * x * (1.0 + t), t


def _gelu_grad(x, t):
    return 0.5 * (1.0 + t) + 0.5 * x * (1.0 - t * t) * (_GELU_C0 * (1.0 + 3.0 * _GELU_C1 * x * x))


def _rowsum8(v):
    r, n = v.shape
    return jnp.sum(v.reshape(r // SUBLANES, SUBLANES, n), axis=0)


def _lanemean(v):
    return jnp.mean(v, axis=-1, keepdims=True)


def _shift_down(v, halo8, k):
    if k == 0:
        return v
    r = pltpu.roll(v, k, 0)
    hr = pltpu.roll(halo8, k, 0)
    row = lax.broadcasted_iota(jnp.int32, halo8.shape, 0)
    top = jnp.where(row < k, hr, r[0:SUBLANES])
    return jnp.concatenate([top, r[SUBLANES:]], axis=0)


def _shift_up(v, next8, k):
    if k == 0:
        return v
    n = v.shape[0]
    r = pltpu.roll(v, n - k, 0)
    nr = pltpu.roll(next8, SUBLANES - k, 0)
    row = lax.broadcasted_iota(jnp.int32, next8.shape, 0)
    bot = jnp.where(row >= SUBLANES - k, nr, r[n - SUBLANES:])
    return jnp.concatenate([r[:n - SUBLANES], bot], axis=0)


def _layernorm_parts(vg):
    mu = _lanemean(vg)
    xc = vg - mu
    rstd = lax.rsqrt(_lanemean(xc * xc) + EPS)
    return xc * rstd, rstd


def _spatial_mix(wt_ref, vn_ref, bsx_ref, mixed_ref, tm):
    for c in range(tm // CHUNK):
        rows = slice(c * CHUNK, (c + 1) * CHUNK)
        for h in range(N_HEADS):
            cols = slice(h * CHUNK, (h + 1) * CHUNK)
            mixed_ref[rows, cols] = _dot(wt_ref[h], vn_ref[rows, cols]) + bsx_ref[:, cols]


def _conv_taps(xb, halo8):
    return [_shift_down(xb, halo8, CONV_W - 1 - k) for k in range(CONV_W)]


def _lru_gates(xc_bf_ref, wa_ref, wx_ref, ba_ref, bx_ref, r_ref, i_ref):
    for h in range(N_HEADS):
        cols = slice(h * CHUNK, (h + 1) * CHUNK)
        xh = xc_bf_ref[:, cols]
        r_ref[:, cols] = jax.nn.sigmoid(_dot(xh, wa_ref[h]) + ba_ref[:, cols])
        i_ref[:, cols] = jax.nn.sigmoid(_dot(xh, wx_ref[h]) + bx_ref[:, cols])


def _softplus_neg(lam):
    return jnp.maximum(-lam, 0.0) + jnp.log(1.0 + jnp.exp(-jnp.abs(lam)))


def _decay_parts(r, lam):
    la = (-LRU_C * _softplus_neg(lam)) * r
    a = jnp.exp(la)
    th = -jnp.tanh(la)
    mult = jnp.sqrt(2.0 * th / (1.0 + th))
    return a, mult


def _inproj_branches_fwd(x, pre_g, wg_in, prm, tm, token):
    t = x.shape[0]
    nt = t // tm
    hb = tm // SUBLANES

    def body(x_ref, g_ref, w_ref,
             lng_ref, lnb_ref, wt_ref, bsx_ref, cw_ref, cb_ref, wa_ref, wx_ref, ba_ref, bx_ref, lam_ref,
             oga_ref, ogb_ref, token_ref,
             z_ref, hn_ref, y_ref, h_ref,
             zbuf0, zbuf1, vn_s, mixed_s, xcbf_s, r_s, i_s, ug_s, halo_s, carry_s):
        s = pl.program_id(0)

        @pl.when(s == 0)
        def _():
            zbuf1[...] = jnp.zeros_like(zbuf1)

        @pl.when(s <= 1)
        def _():
            carry_s[...] = jnp.zeros_like(carry_s)
            halo_s[...] = jnp.zeros_like(halo_s)

        xv = x_ref[...]
        hn_ref[...] = (xv * lax.rsqrt(_lanemean(xv * xv) + EPS) * g_ref[...]).astype(BF16)

        def step(zw, zr):
            def project(j):
                cols = slice(j * W_IN_COLS, (j + 1) * W_IN_COLS)
                zb = _dot(hn_ref[...], w_ref[j]).astype(BF16)
                z_ref[:, cols] = zb
                zw[:, cols] = zb

            zin = lambda g: zr[:, g * D_HALF:(g + 1) * D_HALF].astype(F32)
            always = [s >= 0] * 4

            @pl.when(always[0])
            def _():
                project(0)
                ug, _ = _gelu(zin(0))
                ug_s[...] = ug
                vg, _ = _gelu(zin(1))
                vhat, _ = _layernorm_parts(vg)
                vn_s[...] = (vhat * lng_ref[...] + lnb_ref[...]).astype(BF16)

            @pl.when(always[1])
            def _():
                project(1)
                _spatial_mix(wt_ref, vn_s, bsx_ref, mixed_s, tm)
                ga = zin(2)
                ya = ug_s[...] * mixed_s[...] * (ga * jax.nn.sigmoid(ga))
                ra = lax.rsqrt(_lanemean(ya * ya) + EPS)
                y_ref[:, 0:D_HALF] = (ya * ra * oga_ref[...]).astype(BF16)

            @pl.when(always[2])
            def _():
                project(2)
                xb = zin(3)
                taps = _conv_taps(xb, halo_s[...])
                halo_s[...] = xb[tm - SUBLANES:]
                xc = cb_ref[...] + taps[0] * cw_ref[0:1, :]
                for k in range(1, CONV_W):
                    xc = xc + taps[k] * cw_ref[k:k + 1, :]
                xcbf_s[...] = xc.astype(BF16)
                _lru_gates(xcbf_s, wa_ref, wx_ref, ba_ref, bx_ref, r_s, i_s)
                a, mult = _decay_parts(r_s[...], lam_ref[...])
                row = lax.broadcasted_iota(jnp.int32, a.shape, 0)
                mult = jnp.where(jnp.logical_and(s == 1, row == 0), 1.0, mult)
                r_s[...] = a
                i_s[...] = mult * (i_s[...] * xc)

            @pl.when(always[3])
            def _():
                project(3)
                a = r_s[...]
                b = i_s[...]
                r8 = lax.broadcasted_iota(jnp.int32, a.shape, 0) & (SUBLANES - 1)
                for d in (1, 2, 4):
                    a_sh = pltpu.roll(a, d, 0)
                    b_sh = pltpu.roll(b, d, 0)
                    m = r8 >= d
                    b = jnp.where(m, a * b_sh + b, b)
                    a = jnp.where(m, a * a_sh, a)
                carry = carry_s[...]
                for g in range(hb):
                    rows = slice(g * SUBLANES, (g + 1) * SUBLANES)
                    hg = a[rows] * carry + b[rows]
                    h_ref[rows, :] = hg
                    carry = jnp.broadcast_to(hg[SUBLANES - 1:SUBLANES, :], hg.shape)
                carry_s[...] = carry
                gb = zin(4)
                yb = h_ref[...] * (gb * jax.nn.sigmoid(gb))
                rb = lax.rsqrt(_lanemean(yb * yb) + EPS)
                y_ref[:, D_HALF:] = (yb * rb * ogb_ref[...]).astype(BF16)

        @pl.when(s % 2 == 0)
        def _():
            step(zbuf0, zbuf1)

        @pl.when(s % 2 == 1)
        def _():
            step(zbuf1, zbuf0)

    const = lambda a: pl.BlockSpec(a.shape, lambda s, n=a.ndim: (0,) * n, pipeline_mode=pl.Buffered(1))
    proj = lambda n: pl.BlockSpec((tm, n), lambda s: (jnp.minimum(s, nt - 1), 0))
    head = lambda n: pl.BlockSpec((tm, n), lambda s: (jnp.maximum(s - 1, 0), 0))
    names = ("ln_g", "ln_b", "wt", "bsx", "conv_w", "conv_b", "w_a", "w_x", "b_a", "b_x", "lam", "oga", "ogb")
    pr = [prm[n] for n in names] + [token]
    big = lambda dt: pltpu.VMEM((tm, D_HALF), dt)
    return pl.pallas_call(
        body, name="inproj_branches_fwd", grid=(nt + 1,),
        in_specs=[proj(D_MODEL), const(pre_g), const(wg_in)] + [const(a) for a in pr],
        out_specs=[proj(D_Z), proj(D_MODEL), head(D_MODEL), head(D_HALF)],
        out_shape=[jax.ShapeDtypeStruct((t, D_Z), BF16), jax.ShapeDtypeStruct((t, D_MODEL), BF16),
                   jax.ShapeDtypeStruct((t, D_MODEL), BF16), jax.ShapeDtypeStruct((t, D_HALF), F32)],
        scratch_shapes=[pltpu.VMEM((tm, D_Z), BF16), pltpu.VMEM((tm, D_Z), BF16),
                        big(BF16), big(F32), big(BF16), big(F32), big(F32), big(F32),
                        pltpu.VMEM((SUBLANES, D_HALF), F32), pltpu.VMEM((SUBLANES, D_HALF), F32)],
        compiler_params=_params("arbitrary"),
    )(x, pre_g, wg_in, *pr)


def _outproj_fwd(x, y, p, tgt, post_g, w_out, w_pg, wg_pe, tm):
    t = x.shape[0]

    def body(x_ref, y_ref, p_ref, tgt_ref, pg_ref, wo_ref, wpg_ref, wpe_ref,
             o_ref, h1_ref, gt_ref, dout_ref, loss_ref):
        @pl.when(pl.program_id(0) == 0)
        def _():
            loss_ref[...] = jnp.zeros_like(loss_ref)

        o = _dot(y_ref[...], wo_ref[...])
        o_ref[...] = o
        r3 = lax.rsqrt(_lanemean(o * o) + EPS)
        h1 = x_ref[...] + (o * r3) * pg_ref[...]
        h1b = h1.astype(BF16)
        h1_ref[...] = h1b
        gt = jax.nn.sigmoid(_dot(h1b, wpg_ref[...]))
        gt_ref[...] = gt
        pb = p_ref[...].astype(BF16)
        for k in range(N_CHIPS):
            cols = slice(k * W_PE_COLS, (k + 1) * W_PE_COLS)
            pe = _dot(pb, wpe_ref[k])
            d = h1[:, cols] + pe * gt[:, cols] - tgt_ref[:, cols]
            dout_ref[:, cols] = d * (1.0 / D_MODEL)
            loss_ref[...] += jnp.sum(d * d) * (0.5 / D_MODEL)

    row = lambda n: pl.BlockSpec((tm, n), lambda i: (i, 0))
    const = lambda shp: pl.BlockSpec(shp, lambda i, n=len(shp): (0,) * n, pipeline_mode=pl.Buffered(1))
    return pl.pallas_call(
        body, name="outproj_fwd", grid=(t // tm,),
        in_specs=[row(D_MODEL), row(D_MODEL), row(D_PLE), row(D_MODEL), const((1, D_MODEL)),
                  const((D_MODEL, D_MODEL)), const((D_MODEL, D_MODEL)), const((N_CHIPS, D_PLE, W_PE_COLS))],
        out_specs=[row(D_MODEL), row(D_MODEL), row(D_MODEL), row(D_MODEL),
                   pl.BlockSpec((SUBLANES, LANES), lambda i: (0, 0))],
        out_shape=[jax.ShapeDtypeStruct((t, D_MODEL), F32), jax.ShapeDtypeStruct((t, D_MODEL), BF16),
                   jax.ShapeDtypeStruct((t, D_MODEL), F32), jax.ShapeDtypeStruct((t, D_MODEL), F32),
                   jax.ShapeDtypeStruct((SUBLANES, LANES), F32)],
        compiler_params=_params("arbitrary"),
    )(x, y, p, tgt, post_g, w_out, w_pg, wg_pe)


def _head_bwd(dout, gt, p, o, post_g, w_out, w_pg, wg_pe, tm):
    t = dout.shape[0]

    def body(dout_ref, gt_ref, p_ref, o_ref, pg_ref, wo_ref, wpg_ref, wpe_ref,
             gwpe_ref, dq_ref, dh1_ref, do_ref, dy_ref, gpost_ref):
        i = pl.program_id(0)

        @pl.when(i == 0)
        def _():
            gpost_ref[...] = jnp.zeros_like(gpost_ref)
            gwpe_ref[...] = jnp.zeros_like(gwpe_ref)

        dout = dout_ref[...]
        gt = gt_ref[...]
        pb = p_ref[...].astype(BF16)
        for k in range(N_CHIPS):
            cols = slice(k * W_PE_COLS, (k + 1) * W_PE_COLS)
            pe = _dot(pb, wpe_ref[k])
            g = gt[:, cols]
            dg = dout[:, cols] * g
            gwpe_ref[k] += _dot_tn(pb, dg.astype(BF16))
            dq_ref[:, cols] = (dg * pe * (1.0 - g)).astype(BF16)
        dh1 = dout + _dot_nt(dq_ref[...], wpg_ref[...])
        dh1_ref[...] = dh1
        o = o_ref[...]
        r3 = lax.rsqrt(_lanemean(o * o) + EPS)
        on = o * r3
        gpost_ref[...] += _rowsum8(dh1 * on)
        don = dh1 * pg_ref[...]
        do = r3 * (don - on * _lanemean(don * on))
        dob = do.astype(BF16)
        do_ref[...] = dob
        dy_ref[...] = _dot_nt(dob, wo_ref[...])

        @pl.when(i == pl.num_programs(0) - 1)
        def _():
            gpost_ref[...] = jnp.broadcast_to(jnp.sum(gpost_ref[...], axis=0, keepdims=True), gpost_ref.shape)

    row = lambda n: pl.BlockSpec((tm, n), lambda i: (i, 0))
    const = lambda shp: pl.BlockSpec(shp, lambda i, n=len(shp): (0,) * n, pipeline_mode=pl.Buffered(1))
    return pl.pallas_call(
        body, name="head_bwd", grid=(t // tm,),
        in_specs=[row(D_MODEL), row(D_MODEL), row(D_PLE), row(D_MODEL), const((1, D_MODEL)),
                  const((D_MODEL, D_MODEL)), const((D_MODEL, D_MODEL)), const((N_CHIPS, D_PLE, W_PE_COLS))],
        out_specs=[pl.BlockSpec((N_CHIPS, D_PLE, W_PE_COLS), lambda i: (0, 0, 0)),
                   row(D_MODEL), row(D_MODEL), row(D_MODEL), row(D_MODEL),
                   pl.BlockSpec((SUBLANES, D_MODEL), lambda i: (0, 0))],
        out_shape=[jax.ShapeDtypeStruct((N_CHIPS, D_PLE, W_PE_COLS), F32), jax.ShapeDtypeStruct((t, D_MODEL), BF16),
                   jax.ShapeDtypeStruct((t, D_MODEL), F32), jax.ShapeDtypeStruct((t, D_MODEL), BF16),
                   jax.ShapeDtypeStruct((t, D_MODEL), F32), jax.ShapeDtypeStruct((SUBLANES, D_MODEL), F32)],
        compiler_params=_params("arbitrary"),
    )(dout, gt, p, o, post_g, w_out, w_pg, wg_pe)


def _branches_bwd(z, h, dy, prm, tm, token):
    t = z.shape[0]
    nt = t // tm
    hb = tm // SUBLANES

    def body(u_ref, v_ref, ga_ref, xb_ref, gb_ref, xbh_ref, h_ref, hh_ref, dy_ref,
             lng_ref, lnb_ref, wt_ref, wtt_ref, bsx_ref, cw_ref, cb_ref, wa_ref, wx_ref, ba_ref, bx_ref, lam_ref,
             oga_ref, ogb_ref, token_ref,
             dz_ref, g_oga, g_ogb, g_lng, g_lnb, g_bsx, g_ws, g_cw, g_cb, g_wa, g_ba, g_wx, g_bx, g_lam,
             vn_s, mixed_s, dm_s, dvn_s, xcbf_s, r_s, i_s, a_s, b_s, dh_s, dpr_s, dpi_s, dxc_s,
             ca_s, cd_s, cx_s):
        step_i = pl.program_id(0)
        tile = nt - 1 - step_i
        accs = (g_oga, g_ogb, g_lng, g_lnb, g_bsx, g_ws, g_cw, g_cb, g_wa, g_ba, g_wx, g_bx, g_lam)

        @pl.when(step_i == 0)
        def _():
            for r in accs + (ca_s, cd_s, cx_s):
                r[...] = jnp.zeros_like(r)

        dy_a = dy_ref[:, 0:D_HALF]
        dy_b = dy_ref[:, D_HALF:]

        u = u_ref[...].astype(F32)
        ug, tu = _gelu(u)
        v = v_ref[...].astype(F32)
        vg, tv = _gelu(v)
        vhat, rstd = _layernorm_parts(vg)
        vn_s[...] = (vhat * lng_ref[...] + lnb_ref[...]).astype(BF16)
        _spatial_mix(wt_ref, vn_s, bsx_ref, mixed_s, tm)
        mixed = mixed_s[...]
        ga = ga_ref[...].astype(F32)
        sga = jax.nn.sigmoid(ga)
        sa = ga * sga
        um = ug * mixed
        ya = um * sa
        ra = lax.rsqrt(_lanemean(ya * ya) + EPS)
        yahat = ya * ra
        g_oga[...] += _rowsum8(dy_a * yahat)
        dn = dy_a * oga_ref[...]
        dya = ra * (dn - yahat * _lanemean(dn * yahat))
        dz_ref[:, 2 * D_HALF:3 * D_HALF] = (dya * um * (sga * (1.0 + ga * (1.0 - sga)))).astype(BF16)
        dz_ref[:, 0:D_HALF] = (dya * mixed * sa * _gelu_grad(u, tu)).astype(BF16)
        dmixed = dya * ug * sa
        g_bsx[...] += jnp.sum(dmixed.reshape(tm // CHUNK, CHUNK, D_HALF), axis=0)
        dm_s[...] = dmixed.astype(BF16)
        for c in range(tm // CHUNK):
            rows = slice(c * CHUNK, (c + 1) * CHUNK)
            for hd in range(N_HEADS):
                cols = slice(hd * CHUNK, (hd + 1) * CHUNK)
                dmh = dm_s[rows, cols]
                dvn_s[rows, cols] = _dot(wtt_ref[hd], dmh)
                g_ws[hd] += _dot_nt(dmh, vn_s[rows, cols])
        dvn = dvn_s[...]
        g_lng[...] += _rowsum8(dvn * vhat)
        g_lnb[...] += _rowsum8(dvn)
        dvh = dvn * lng_ref[...]
        dvg = rstd * (dvh - _lanemean(dvh) - vhat * _lanemean(dvh * vhat))
        dz_ref[:, D_HALF:2 * D_HALF] = (dvg * _gelu_grad(v, tv)).astype(BF16)

        xb = xb_ref[...].astype(F32)
        halo = jnp.where(tile == 0, 0.0, xbh_ref[...].astype(F32)[SUBLANES:])
        taps = _conv_taps(xb, halo)
        xc = cb_ref[...] + taps[0] * cw_ref[0:1, :]
        for k in range(1, CONV_W):
            xc = xc + taps[k] * cw_ref[k:k + 1, :]
        xcbf_s[...] = xc.astype(BF16)
        _lru_gates(xcbf_s, wa_ref, wx_ref, ba_ref, bx_ref, r_s, i_s)
        rg = r_s[...]
        ig = i_s[...]
        lam = lam_ref[...]
        a, mult_true = _decay_parts(rg, lam)
        row = lax.broadcasted_iota(jnp.int32, a.shape, 0)
        first = jnp.logical_and(tile == 0, row == 0)
        mult = jnp.where(first, 1.0, mult_true)
        hcur = h_ref[...]
        hprev = _shift_down(hcur, jnp.where(tile == 0, 0.0, hh_ref[...]), 1)
        gb = gb_ref[...].astype(F32)
        sgb = jax.nn.sigmoid(gb)
        sb = gb * sgb
        yb = hcur * sb
        rb = lax.rsqrt(_lanemean(yb * yb) + EPS)
        ybhat = yb * rb
        g_ogb[...] += _rowsum8(dy_b * ybhat)
        dn = dy_b * ogb_ref[...]
        dyb = rb * (dn - ybhat * _lanemean(dn * ybhat))
        dz_ref[:, 4 * D_HALF:5 * D_HALF] = (dyb * hcur * (sgb * (1.0 + gb * (1.0 - sgb)))).astype(BF16)

        an = _shift_up(a, ca_s[...], 1)
        bb = dyb * sb
        r8 = row & (SUBLANES - 1)
        for d in (1, 2, 4):
            a_sh = pltpu.roll(an, tm - d, 0)
            b_sh = pltpu.roll(bb, tm - d, 0)
            m = r8 + d < SUBLANES
            bb = jnp.where(m, an * b_sh + bb, bb)
            an = jnp.where(m, an * a_sh, an)
        a_s[...] = an
        b_s[...] = bb

        def step(g, carry):
            sl = pl.ds(pl.multiple_of((hb - 1 - g) * SUBLANES, SUBLANES), SUBLANES)
            dg = a_s[sl, :] * carry + b_s[sl, :]
            dh_s[sl, :] = dg
            return jnp.broadcast_to(dg[0:1, :], dg.shape)

        cd_s[...] = lax.fori_loop(0, hb, step, cd_s[...])
        ca_s[...] = jnp.broadcast_to(a[0:1, :], ca_s.shape)
        dh = dh_s[...]
        da = dh * hprev
        gx = ig * xc
        dla = da * a - jnp.where(first, 0.0, dh * gx * (a * a / mult_true))
        g_lam[...] += _rowsum8(dla * rg)
        dr = dla * (-LRU_C * _softplus_neg(lam))
        dpr = dr * rg * (1.0 - rg)
        dpi = (dh * mult * xc) * ig * (1.0 - ig)
        g_ba[...] += _rowsum8(dpr)
        g_bx[...] += _rowsum8(dpi)
        dpr_s[...] = dpr.astype(BF16)
        dpi_s[...] = dpi.astype(BF16)
        for hd in range(N_HEADS):
            cols = slice(hd * CHUNK, (hd + 1) * CHUNK)
            xh = xcbf_s[:, cols]
            dprh = dpr_s[:, cols]
            dpih = dpi_s[:, cols]
            g_wa[hd] += _dot_tn(xh, dprh)
            g_wx[hd] += _dot_tn(xh, dpih)
            dxc_s[:, cols] = _dot_nt(dprh, wa_ref[hd]) + _dot_nt(dpih, wx_ref[hd])
        dxc = dxc_s[...] + dh * mult * ig
        g_cb[...] += _rowsum8(dxc)
        for k in range(CONV_W):
            g_cw[k * SUBLANES:(k + 1) * SUBLANES, :] += _rowsum8(dxc * taps[k])
        nxt = cx_s[...]
        dxb = dxc * cw_ref[CONV_W - 1:CONV_W, :]
        for j in range(1, CONV_W):
            dxb = dxb + _shift_up(dxc, nxt, j) * cw_ref[CONV_W - 1 - j:CONV_W - j, :]
        dz_ref[:, 3 * D_HALF:4 * D_HALF] = dxb.astype(BF16)
        cx_s[...] = dxc[0:SUBLANES]

        @pl.when(step_i == nt - 1)
        def _():
            for r in (g_oga, g_ogb, g_lng, g_lnb, g_cb, g_ba, g_bx):
                r[...] = jnp.broadcast_to(jnp.sum(r[...], axis=0, keepdims=True), r.shape)
            lam_f = LRU_C * jax.nn.sigmoid(-lam_ref[...])
            g_lam[...] = jnp.broadcast_to(jnp.sum(g_lam[...], axis=0, keepdims=True) * lam_f, g_lam.shape)
            for k in range(CONV_W):
                blk = g_cw[k * SUBLANES:(k + 1) * SUBLANES, :]
                g_cw[k * SUBLANES:(k + 1) * SUBLANES, :] = jnp.broadcast_to(jnp.sum(blk, axis=0, keepdims=True), blk.shape)
            tri = (lax.broadcasted_iota(jnp.int32, (CHUNK, CHUNK), 0) >= lax.broadcasted_iota(jnp.int32, (CHUNK, CHUNK), 1))
            for hd in range(N_HEADS):
                cols = slice(hd * CHUNK, (hd + 1) * CHUNK)
                g_ws[hd] = jnp.where(tri, g_ws[hd], 0.0)
                blk = g_bsx[:, cols]
                g_bsx[:, cols] = jnp.broadcast_to(jnp.sum(blk, axis=1, keepdims=True), blk.shape)

    rev = lambda i: nt - 1 - i
    zspec = lambda g: pl.BlockSpec((tm, D_HALF), lambda i, g=g: (rev(i), g))
    halo = lambda col: pl.BlockSpec((SUBLANES, D_HALF), lambda i: (jnp.maximum(rev(i) * hb - 1, 0), col))
    zhalo = pl.BlockSpec((2 * SUBLANES, D_HALF), lambda i: (jnp.maximum(rev(i) * (hb // 2) - 1, 0), 3))
    full = lambda a: pl.BlockSpec(a.shape, lambda i, n=a.ndim: (0,) * n)
    acc = lambda shp: pl.BlockSpec(shp, lambda i, n=len(shp): (0,) * n)
    names = ("ln_g", "ln_b", "wt", "wtt", "bsx", "conv_w", "conv_b", "w_a", "w_x", "b_a", "b_x", "lam", "oga", "ogb")
    pr = [prm[n] for n in names] + [token]
    vec = (SUBLANES, D_HALF)
    mat = (N_HEADS, CHUNK, CHUNK)
    acc_shapes = [vec, vec, vec, vec, (CHUNK, D_HALF), mat, (CONV_W * SUBLANES, D_HALF), vec, mat, vec, mat, vec, vec]
    big = lambda dt: pltpu.VMEM((tm, D_HALF), dt)
    return pl.pallas_call(
        body, name="branches_bwd", grid=(nt,),
        in_specs=[zspec(0), zspec(1), zspec(2), zspec(3), zspec(4), zhalo,
                  pl.BlockSpec((tm, D_HALF), lambda i: (rev(i), 0)), halo(0),
                  pl.BlockSpec((tm, D_MODEL), lambda i: (rev(i), 0))] + [full(a) for a in pr],
        out_specs=[pl.BlockSpec((tm, D_Z), lambda i: (rev(i), 0))] + [acc(s) for s in acc_shapes],
        out_shape=[jax.ShapeDtypeStruct((t, D_Z), BF16)] + [jax.ShapeDtypeStruct(s, F32) for s in acc_shapes],
        scratch_shapes=[big(BF16), big(F32), big(BF16), big(F32), big(BF16), big(F32), big(F32), big(F32), big(F32),
                        big(F32), big(BF16), big(BF16), big(F32),
                        pltpu.VMEM(vec, F32), pltpu.VMEM(vec, F32), pltpu.VMEM(vec, F32)],
        compiler_params=_params("arbitrary"),
    )(z, z, z, z, z, z, h, h, dy, *pr)


def _inproj_bwd(dz, wg_in, x, dh1, pre_g, tm, tile0, nt, prev, last, token, name):
    t = x.shape[0]

    def body(*refs):
        dz_ref, w_ref, x_ref, dh1_ref, g_ref = refs[:5]
        gx_ref, gpre_ref, acc_s = refs[-3:]
        i = pl.program_id(0)

        @pl.when(i == 0)
        def _():
            gpre_ref[...] = jnp.zeros_like(gpre_ref) if prev is None else refs[7][...]

        acc = _dot_nt(dz_ref[:, 0:W_IN_COLS], w_ref[0])
        for k in range(1, N_CHIPS):
            acc = acc + _dot_nt(dz_ref[:, k * W_IN_COLS:(k + 1) * W_IN_COLS], w_ref[k])
        acc_s[...] = acc
        for s in range(tm // CHUNK):
            rows = slice(s * CHUNK, (s + 1) * CHUNK)
            xv = x_ref[rows, :]
            r = lax.rsqrt(_lanemean(xv * xv) + EPS)
            xhat = xv * r
            dhn = acc_s[rows, :]
            gpre_ref[...] += _rowsum8(dhn * xhat)
            dxh = dhn * g_ref[...]
            gx_ref[rows, :] = dh1_ref[rows, :] + r * (dxh - xhat * _lanemean(dxh * xhat))

        if last:
            @pl.when(i == nt - 1)
            def _():
                gpre_ref[...] = jnp.broadcast_to(jnp.sum(gpre_ref[...], axis=0, keepdims=True), gpre_ref.shape)

    row = lambda n: pl.BlockSpec((tm, n), lambda i: (tile0 + i, 0))
    small = lambda r: pl.BlockSpec((r, D_MODEL), lambda i: (0, 0))
    tok = pl.BlockSpec((SUBLANES, LANES), lambda i: (0, 0))
    in_specs = [row(D_Z), pl.BlockSpec(wg_in.shape, lambda i: (0, 0, 0), pipeline_mode=pl.Buffered(1)),
                row(D_MODEL), row(D_MODEL), small(1), tok]
    args = [dz, wg_in, x, dh1, pre_g, token]
    aliases = {}
    if prev is not None:
        in_specs += [ANY, small(SUBLANES)]
        args += list(prev)
        aliases = {6: 0}
    return pl.pallas_call(
        body, name=name, grid=(nt,), in_specs=in_specs, out_specs=[row(D_MODEL), small(SUBLANES)],
        out_shape=[jax.ShapeDtypeStruct((t, D_MODEL), F32), jax.ShapeDtypeStruct((SUBLANES, D_MODEL), F32)],
        input_output_aliases=aliases,
        scratch_shapes=[pltpu.VMEM((tm, D_MODEL), F32)],
        compiler_params=_params("arbitrary"),
    )(*args)


def _weight_grad(a, b, name, kb, nb, tk, tn, tt, token):
    t = a.shape[0]
    tt = min(tt, t)

    def body(a_ref, b_ref, token_ref, o_ref):
        @pl.when(pl.program_id(2) == 0)
        def _():
            o_ref[...] = jnp.zeros_like(o_ref)

        o_ref[...] += _dot_tn(a_ref[...], b_ref[...])

    return pl.pallas_call(
        body, name=name, grid=(nb, kb, t // tt),
        in_specs=[pl.BlockSpec((tt, tk), lambda j, i, s: (s, i)), pl.BlockSpec((tt, tn), lambda j, i, s: (s, j)),
                  pl.BlockSpec((SUBLANES, LANES), lambda j, i, s: (0, 0))],
        out_specs=pl.BlockSpec((None, None, tk, tn), lambda j, i, s: (j, i, 0, 0)),
        out_shape=jax.ShapeDtypeStruct((nb, kb, tk, tn), F32),
        compiler_params=_params("parallel", "parallel", "arbitrary"),
    )(a, b, token)


def _place():
    x, y, c = lax.axis_index("x"), lax.axis_index("y"), lax.axis_index("c")
    return x, y, c


def _chip_of(x, y):
    return 2 * x + y


def _gather_weights(w_in, conv_w):
    halves = [(D_MODEL // 2, W_IN_COLS)]
    nw = len(halves)

    def body(win_ref, cw_ref, gin_ref, gcw_ref, s0, b0, lsem, send_sems, recv_sems, cw_send, cw_recv):
        x, y, c = _place()
        me = _chip_of(x, y)
        sibling = (x, y, 1 - c)
        chips = [(1 - x, y), (x, 1 - y), (1 - x, 1 - y)]
        srcs = (win_ref,)
        stage = (s0,)
        bf = (b0,)
        outs = (gin_ref,)
        loads = []
        for n in range(nw):
            rows = halves[n][0]
            cp = pltpu.make_async_copy(srcs[n].at[pl.ds(c * rows, rows), :], stage[n], lsem.at[n])
            cp.start()
            loads.append(cp)
        own_cw = pltpu.make_async_copy(cw_ref, gcw_ref.at[me], lsem.at[2 * nw])
        own_cw.start()
        for n in range(nw):
            loads[n].wait()
            bf[n][...] = stage[n][...].astype(BF16)

        def copy(n, k, chip, to, src=None):
            dst = outs[n].at[chip, c]
            return pltpu.make_async_remote_copy(
                src_ref=dst if src is None else src, dst_ref=dst,
                send_sem=send_sems.at[n, k], recv_sem=recv_sems.at[n, k], device_id=to, device_id_type=MESH)

        def recv(n, k, chip, core):
            dst = outs[n].at[chip, core]
            return pltpu.make_async_remote_copy(
                src_ref=dst, dst_ref=dst, send_sem=send_sems.at[n, k], recv_sem=recv_sems.at[n, k],
                device_id=sibling, device_id_type=MESH)

        sends = []
        locals_ = []
        for n in range(nw):
            lc = pltpu.make_async_copy(bf[n], outs[n].at[me, c], lsem.at[nw + n])
            lc.start()
            locals_.append(lc)
            first = [copy(n, 0, me, sibling, src=bf[n])]
            first += [copy(n, 1 + j, me, (*chip, c), src=bf[n]) for j, chip in enumerate(chips)]
            for cp in first:
                cp.start()
            sends += first
        cws = []
        for j, chip in enumerate(chips):
            cp = pltpu.make_async_remote_copy(
                src_ref=cw_ref, dst_ref=gcw_ref.at[me], send_sem=cw_send.at[j], recv_sem=cw_recv.at[j],
                device_id=(*chip, c), device_id_type=MESH)
            cp.start()
            cws.append(cp)
        for n in range(nw):
            for j, chip in enumerate(chips):
                kj = _chip_of(*chip)
                recv(n, 1 + j, kj, c).wait_recv()
                fw = copy(n, 4 + j, kj, sibling)
                fw.start()
                sends.append(fw)
        for n in range(nw):
            recv(n, 0, me, 1 - c).wait_recv()
            for j, chip in enumerate(chips):
                recv(n, 4 + j, _chip_of(*chip), 1 - c).wait_recv()
        for j, chip in enumerate(chips):
            pltpu.make_async_remote_copy(
                src_ref=cw_ref, dst_ref=gcw_ref.at[_chip_of(*chip)], send_sem=cw_send.at[j], recv_sem=cw_recv.at[j],
                device_id=(*chip, c), device_id_type=MESH).wait_recv()
        for cp in sends + cws:
            cp.wait_send()
        for lc in locals_:
            lc.wait()
        own_cw.wait()

    out_shape = [jax.ShapeDtypeStruct((N_CHIPS, 2) + hs, BF16) for hs in halves]
    out_shape.append(jax.ShapeDtypeStruct((N_CHIPS, CONV_W, CONV_COLS), F32))
    scratch = [pltpu.VMEM(hs, F32) for hs in halves] + [pltpu.VMEM(hs, BF16) for hs in halves]
    scratch += [pltpu.SemaphoreType.DMA((2 * nw + 1,)), pltpu.SemaphoreType.DMA((nw, 7)),
                pltpu.SemaphoreType.DMA((nw, 7)), pltpu.SemaphoreType.DMA((3,)), pltpu.SemaphoreType.DMA((3,))]
    return pl.pallas_call(
        body, name="gather_w_in", in_specs=[ANY] * 2, out_specs=[ANY] * 2, out_shape=out_shape,
        scratch_shapes=scratch, compiler_params=pltpu.CompilerParams(vmem_limit_bytes=VMEM_LIMIT),
    )(w_in, conv_w)


HBM = pl.BlockSpec(memory_space=pltpu.HBM)
SEM = pl.BlockSpec(memory_space=pltpu.SEMAPHORE)
EFFECT = pltpu.SideEffectType.DATAFLOW_SIDE_EFFECTING


def _hbm(a):
    return pltpu.with_memory_space_constraint(a, pltpu.HBM)


def _landing(shape, dtype):
    return _hbm(lax.empty(shape, dtype))


def _exchange_start(name, arrays, ncopies, build, after=None):
    n = len(arrays)
    extra = [] if after is None else [after]

    def body(*refs):
        ins, token = refs[:n], refs[-1]
        send_sems, recv_sems = refs[n + len(extra)], refs[n + len(extra) + 1]
        for cp in build(ins, send_sems, recv_sems):
            cp.start()
        token[...] = jnp.zeros_like(token)

    outs = pl.pallas_call(
        body, name=name,
        out_shape=(pltpu.SemaphoreType.DMA((ncopies,)), pltpu.SemaphoreType.DMA((ncopies,)),
                   *[pltpu.HBM(a.shape, a.dtype) for a in arrays], jax.ShapeDtypeStruct((SUBLANES, LANES), F32)),
        in_specs=[HBM] * n + [ANY] * len(extra),
        out_specs=(SEM, SEM, *[HBM] * n, pl.BlockSpec(memory_space=pltpu.VMEM)),
        input_output_aliases={q: q + 2 for q in range(n)},
        compiler_params=pltpu.CompilerParams(has_side_effects=EFFECT),
    )(*[_hbm(a) for a in arrays], *extra)
    return (outs[0], outs[1], list(outs[2:2 + n])), outs[-1]


def _exchange_wait(name, started, after, build):
    send, recv, arrays = started
    n = len(arrays)

    def body(*refs):
        ins, send_sems, recv_sems = refs[:n], refs[n], refs[n + 1]
        for cp in build(ins, send_sems, recv_sems):
            cp.wait_send()
            cp.wait_recv()

    return pl.pallas_call(
        body, name=name, out_shape=tuple(pltpu.HBM(a.shape, a.dtype) for a in arrays),
        in_specs=[HBM] * n + [SEM, SEM, ANY], out_specs=tuple([HBM] * n),
        input_output_aliases={q: q for q in range(n)},
        compiler_params=pltpu.CompilerParams(has_side_effects=EFFECT),
    )(*arrays, send, recv, after)


def _exchange_wait_start(name, started, after, build_wait, ncopies, build_start):
    send, recv, arrays = started
    n = len(arrays)

    def body(*refs):
        ins, send_sems, recv_sems = refs[:n], refs[n], refs[n + 1]
        send2, recv2, token = refs[n + 3], refs[n + 4], refs[-1]
        for cp in build_wait(ins, send_sems, recv_sems):
            cp.wait_send()
            cp.wait_recv()
        for cp in build_start(ins, send2, recv2):
            cp.start()
        token[...] = jnp.zeros_like(token)

    outs = pl.pallas_call(
        body, name=name,
        out_shape=(pltpu.SemaphoreType.DMA((ncopies,)), pltpu.SemaphoreType.DMA((ncopies,)),
                   *[pltpu.HBM(a.shape, a.dtype) for a in arrays], jax.ShapeDtypeStruct((SUBLANES, LANES), F32)),
        in_specs=[HBM] * n + [SEM, SEM, ANY], out_specs=(SEM, SEM, *[HBM] * n, pl.BlockSpec(memory_space=pltpu.VMEM)),
        input_output_aliases={q: q + 2 for q in range(n)},
        compiler_params=pltpu.CompilerParams(has_side_effects=EFFECT),
    )(*arrays, send, recv, after)
    return (outs[0], outs[1], list(outs[2:2 + n])), outs[-1]


def _cast_into_slot(w, kc, name):
    rows, cols = w.shape
    tr = min(rows, 256)

    def body(kc_ref, w_ref, o_ref):
        o_ref[...] = w_ref[...].astype(BF16)

    grid_spec = pltpu.PrefetchScalarGridSpec(
        num_scalar_prefetch=1, grid=(rows // tr,),
        in_specs=[pl.BlockSpec((tr, cols), lambda r, kc: (r, 0))],
        out_specs=pl.BlockSpec((None, tr, cols), lambda r, kc: (kc[0], r, 0)))
    return pl.pallas_call(
        body, name=name, grid_spec=grid_spec, out_shape=jax.ShapeDtypeStruct((N_CHIPS, rows, cols), BF16),
        compiler_params=_params("arbitrary"),
    )(kc, w)


def _gather_ici_copies(n):
    def build(refs, send_sems, recv_sems):
        x, y, c = _place()
        mine = lambda b: refs[b].at[_chip_of(x, y), c]
        chips = [(1 - x, y), (x, 1 - y), (1 - x, 1 - y)]
        return [pltpu.make_async_remote_copy(
            src_ref=mine(b), dst_ref=mine(b), send_sem=send_sems.at[3 * b + j], recv_sem=recv_sems.at[3 * b + j],
            device_id=(*chip, c), device_id_type=MESH) for b in range(n) for j, chip in enumerate(chips)]
    return build


def _gather_relay_copies(n):
    def build(refs, send_sems, recv_sems):
        x, y, c = _place()
        chips = [(1 - x, y), (x, 1 - y), (1 - x, 1 - y)]
        cps = []
        for b in range(n):
            for j, chip in enumerate(chips):
                got = refs[b].at[_chip_of(*chip), c]
                cps.append(pltpu.make_async_remote_copy(
                    src_ref=got, dst_ref=got, send_sem=send_sems.at[3 * b + j], recv_sem=recv_sems.at[3 * b + j],
                    device_id=(x, y, 1 - c), device_id_type=MESH))
        return cps
    return build


def _sibling_copies(n):
    def build(refs, send_sems, recv_sems):
        x, y, c = _place()
        return [pltpu.make_async_remote_copy(
            src_ref=refs[b].at[:, 1 - c], dst_ref=refs[n + b], send_sem=send_sems.at[b], recv_sem=recv_sems.at[b],
            device_id=(x, y, 1 - c), device_id_type=MESH) for b in range(n)]
    return build


def _chip_copies(n):
    def build(refs, send_sems, recv_sems):
        x, y, c = _place()
        chips = [(1 - x, y), (x, 1 - y), (1 - x, 1 - y)]
        return [pltpu.make_async_remote_copy(
            src_ref=refs[b].at[_chip_of(*chip)], dst_ref=refs[n + b].at[j],
            send_sem=send_sems.at[3 * b + j], recv_sem=recv_sems.at[3 * b + j],
            device_id=(*chip, c), device_id_type=MESH) for b in range(n) for j, chip in enumerate(chips)]
    return build


def _finish_copies(n, with_small):
    def build(refs, send_sems, recv_sems):
        x, y, c = _place()
        cps = [pltpu.make_async_remote_copy(
            src_ref=refs[b].at[c], dst_ref=refs[b].at[c], send_sem=send_sems.at[b], recv_sem=recv_sems.at[b],
            device_id=(x, y, 1 - c), device_id_type=MESH) for b in range(n)]
        if with_small:
            mine = refs[n].at[_chip_of(x, y), c]
            flips = [(fx, fy, fc) for fx in (0, 1) for fy in (0, 1) for fc in (0, 1)][1:]
            cps += [pltpu.make_async_remote_copy(
                src_ref=mine, dst_ref=mine, send_sem=send_sems.at[n + q], recv_sem=recv_sems.at[n + q],
                device_id=(x ^ fx, y ^ fy, c ^ fc), device_id_type=MESH) for q, (fx, fy, fc) in enumerate(flips)]
        return cps
    return build


def _pair_sum(g, r1, kc, name, tr, send_dtype):
    nk, _, rows, cols = g.shape

    def body(kc_ref, g_ref, r_ref, p_ref, own_ref):
        s = g_ref[...] + r_ref[...]
        p_ref[...] = s.astype(send_dtype)

        @pl.when(pl.program_id(1) == kc_ref[0])
        def _():
            own_ref[...] = s

    grid_spec = pltpu.PrefetchScalarGridSpec(
        num_scalar_prefetch=1, grid=(rows // tr, nk),
        in_specs=[pl.BlockSpec((None, None, tr, cols), lambda r, k, kc: (k, kc[1], r, 0)),
                  pl.BlockSpec((None, tr, cols), lambda r, k, kc: (k, r, 0))],
        out_specs=[pl.BlockSpec((None, tr, cols), lambda r, k, kc: (k, r, 0)),
                   pl.BlockSpec((tr, cols), lambda r, k, kc: (r, 0))])
    return pl.pallas_call(
        body, name=name, grid_spec=grid_spec,
        out_shape=[jax.ShapeDtypeStruct((nk, rows, cols), send_dtype), jax.ShapeDtypeStruct((rows, cols), F32)],
        compiler_params=_params("arbitrary", "arbitrary"),
    )(kc, g, r1)


def _allreduce_vector(v, token):
    rows = v.shape[0]

    def body(v_ref, token_ref, o_ref, all_s, send_sems, recv_sems):
        x, y, c = _place()
        me = 2 * _chip_of(x, y) + c
        all_s[me] = v_ref[...]
        flips = [(fx, fy, fc) for fx in (0, 1) for fy in (0, 1) for fc in (0, 1)][1:]
        cps = []
        for q, (fx, fy, fc) in enumerate(flips):
            cp = pltpu.make_async_remote_copy(
                src_ref=v_ref, dst_ref=all_s.at[me], send_sem=send_sems.at[q], recv_sem=recv_sems.at[q],
                device_id=(x ^ fx, y ^ fy, c ^ fc), device_id_type=MESH)
            cp.start()
            cps.append(cp)
        for cp in cps:
            cp.wait()
        s = all_s[0]
        for d in range(1, 8):
            s = s + all_s[d]
        o_ref[...] = s

    vm = pl.BlockSpec(memory_space=pltpu.VMEM)
    return pl.pallas_call(
        body, name="allreduce_vector", in_specs=[vm, vm], out_specs=vm, out_shape=jax.ShapeDtypeStruct(v.shape, F32),
        scratch_shapes=[pltpu.VMEM((8, rows, LANES), F32), pltpu.SemaphoreType.DMA((7,)), pltpu.SemaphoreType.DMA((7,))],
    )(v, token)


def _chip_sum(own, r2, slot, lead, name, tr):
    rows, cols = own.shape
    nl = len(lead)

    def body(slot_ref, o_ref, r_ref, s_ref):
        s = o_ref[...]
        for j in range(3):
            s = s + r_ref[j].astype(F32)
        s_ref[...] = s

    grid_spec = pltpu.PrefetchScalarGridSpec(
        num_scalar_prefetch=1, grid=(rows // tr,),
        in_specs=[pl.BlockSpec((tr, cols), lambda r, sl: (r, 0)), pl.BlockSpec((3, tr, cols), lambda r, sl: (0, r, 0))],
        out_specs=pl.BlockSpec((None,) * nl + (tr, cols), lambda r, sl: tuple(sl[q] for q in range(nl)) + (r, 0)))
    return pl.pallas_call(
        body, name=name, grid_spec=grid_spec, out_shape=jax.ShapeDtypeStruct(tuple(lead) + (rows, cols), F32),
        compiler_params=_params("arbitrary"),
    )(slot, own, r2)


def _adamw(w, g, m, v, name, tr):
    rows, cols = w.shape

    def body(w_ref, g_ref, m_ref, v_ref, d_ref, nm_ref, nv_ref):
        gv = g_ref[...]
        nm = ADAM_B1 * m_ref[...] + (1.0 - ADAM_B1) * gv
        nv = ADAM_B2 * v_ref[...] + (1.0 - ADAM_B2) * (gv * gv)
        m_hat = nm / (1.0 - ADAM_B1 ** ADAM_STEP)
        v_hat = nv / (1.0 - ADAM_B2 ** ADAM_STEP)
        d_ref[...] = -ADAM_LR * (m_hat / (jnp.sqrt(v_hat) + ADAM_EPS) + ADAM_WD * w_ref[...])
        nm_ref[...] = nm
        nv_ref[...] = nv

    spec = pl.BlockSpec((tr, cols), lambda r: (r, 0))
    return pl.pallas_call(
        body, name=name, grid=(rows // tr,), in_specs=[spec] * 4, out_specs=[spec] * 3,
        out_shape=[jax.ShapeDtypeStruct((rows, cols), F32)] * 3,
        compiler_params=_params("parallel"),
    )(w, g, m, v)


def _rows128(a):
    return a.reshape(-1, LANES)


def _pack_small(parts):
    pieces = [_rows128(parts[n]) for n, _ in SMALL_ROWS]
    pieces.append(jnp.zeros((SMALL_TOTAL - SMALL_USED, LANES), F32))
    return jnp.concatenate(pieces, axis=0)


def _unpack_small(packed, shapes):
    out, at = {}, 0
    for n, r in SMALL_ROWS:
        out[n] = packed[at:at + r].reshape(shapes[n])
        at += r
    return out


def kernel(x, p, pre_g, w_in, gmlp_ln_g, gmlp_ln_b, gmlp_ws, gmlp_bs, conv_w, conv_b, w_a, b_a, w_x, b_x, lam, gmlp_out_g, lru_out_g, w_out, post_g, w_pe, w_pg, loss_target, m_pre_g, m_w_in, m_gmlp_ln_g, m_gmlp_ln_b, m_gmlp_ws, m_gmlp_bs, m_conv_w, m_conv_b, m_w_a, m_b_a, m_w_x, m_b_x, m_lam, m_gmlp_out_g, m_lru_out_g, m_w_out, m_post_g, m_w_pe, m_w_pg, v_pre_g, v_w_in, v_gmlp_ln_g, v_gmlp_ln_b, v_gmlp_ws, v_gmlp_bs, v_conv_w, v_conv_b, v_w_a, v_b_a, v_w_x, v_b_x, v_lam, v_gmlp_out_g, v_lru_out_g, v_w_out, v_post_g, v_w_pe, v_w_pg):
    weights = dict(pre_g=pre_g, w_in=w_in, gmlp_ln_g=gmlp_ln_g, gmlp_ln_b=gmlp_ln_b, gmlp_ws=gmlp_ws, gmlp_bs=gmlp_bs,
                   conv_w=conv_w, conv_b=conv_b, w_a=w_a, b_a=b_a, w_x=w_x, b_x=b_x, lam=lam, gmlp_out_g=gmlp_out_g,
                   lru_out_g=lru_out_g, w_out=w_out, post_g=post_g, w_pe=w_pe, w_pg=w_pg)
    mom_m = dict(pre_g=m_pre_g, w_in=m_w_in, gmlp_ln_g=m_gmlp_ln_g, gmlp_ln_b=m_gmlp_ln_b, gmlp_ws=m_gmlp_ws,
                 gmlp_bs=m_gmlp_bs, conv_w=m_conv_w, conv_b=m_conv_b, w_a=m_w_a, b_a=m_b_a, w_x=m_w_x, b_x=m_b_x,
                 lam=m_lam, gmlp_out_g=m_gmlp_out_g, lru_out_g=m_lru_out_g, w_out=m_w_out, post_g=m_post_g,
                 w_pe=m_w_pe, w_pg=m_w_pg)
    mom_v = dict(pre_g=v_pre_g, w_in=v_w_in, gmlp_ln_g=v_gmlp_ln_g, gmlp_ln_b=v_gmlp_ln_b, gmlp_ws=v_gmlp_ws,
                 gmlp_bs=v_gmlp_bs, conv_w=v_conv_w, conv_b=v_conv_b, w_a=v_w_a, b_a=v_b_a, w_x=v_w_x, b_x=v_b_x,
                 lam=v_lam, gmlp_out_g=v_gmlp_out_g, lru_out_g=v_lru_out_g, w_out=v_w_out, post_g=v_post_g,
                 w_pe=v_w_pe, w_pg=v_w_pg)
    order = list(weights)
    xi, yi, ci = _place()
    me = _chip_of(xi, yi)
    kc = jnp.stack([me, ci]).astype(jnp.int32)

    x2 = x[0]
    p2 = p[0, 0]
    tgt = loss_target[0]

    g_in, g_cw = _gather_weights(w_in[0], conv_w[0, :, 0, :])
    wg_in = g_in.reshape(N_CHIPS, D_MODEL, W_IN_COLS)
    cw_full = jnp.transpose(g_cw, (1, 0, 2)).reshape(CONV_W, D_HALF)
    later = [_cast_into_slot(w_out[0], kc, "cast_w_out").reshape(N_CHIPS, 2, W_ROWS // 2, D_MODEL),
             _cast_into_slot(w_pg[0], kc, "cast_w_pg").reshape(N_CHIPS, 2, W_ROWS // 2, D_MODEL),
             _cast_into_slot(w_pe[0], kc, "cast_w_pe").reshape(N_CHIPS, 2, D_PLE // 2, W_PE_COLS)]
    gather_st, gather_tok = _exchange_start("gather_start", later, 9, _gather_ici_copies(3), after=g_cw)

    causal = jnp.tril(jnp.ones((CHUNK, CHUNK), dtype=bool))
    ws_m = jnp.where(causal[None], gmlp_ws[0], 0.0)
    prm = dict(
        ln_g=gmlp_ln_g, ln_b=gmlp_ln_b, wt=ws_m.astype(BF16), wtt=jnp.transpose(ws_m, (0, 2, 1)).astype(BF16),
        bsx=jnp.repeat(jnp.transpose(gmlp_bs[0]), CHUNK, axis=1),
        conv_w=cw_full, conv_b=conv_b, w_a=w_a[0].astype(BF16), w_x=w_x[0].astype(BF16),
        b_a=b_a[0].reshape(1, D_HALF), b_x=b_x[0].reshape(1, D_HALF), lam=lam, oga=gmlp_out_g, ogb=lru_out_g)

    z, hn, y, h = _inproj_branches_fwd(x2, pre_g, wg_in, prm, 256, gather_tok)
    gather_st, gather_tok = _exchange_wait_start("gather_relay", gather_st, y, _gather_ici_copies(3), 9,
                                                 _gather_relay_copies(3))
    g_out, g_pg, g_pe = _exchange_wait("gather_wait", gather_st, gather_tok, _gather_relay_copies(3))
    wg_out = g_out.reshape(D_MODEL, D_MODEL)
    wg_pg = g_pg.reshape(D_MODEL, D_MODEL)
    wg_pe = g_pe.reshape(N_CHIPS, D_PLE, W_PE_COLS)
    o, h1, gt, dout, loss_acc = _outproj_fwd(x2, y, p2, tgt, post_g, wg_out, wg_pg, wg_pe, 256)
    loss = lax.psum(loss_acc[0, 0], ("x", "y", "c"))

    def sibling_start(tag, bufs):
        lands = [_landing((b.shape[0],) + b.shape[2:], b.dtype) for b in bufs]
        return _exchange_start("sibling_start_" + tag, bufs + lands, len(bufs), _sibling_copies(len(bufs)))

    def pair_then_chip_start(tag, started, after, names, tiles, dtypes):
        n = len(names)
        got = _exchange_wait("sibling_wait_" + tag, started, after, _sibling_copies(n))
        pairs = [_pair_sum(got[b], got[n + b], kc, "pair_sum_" + names[b], tiles[b], dtypes[b]) for b in range(n)]
        lands = [_landing((3,) + pr[0].shape[1:], pr[0].dtype) for pr in pairs]
        return _exchange_start("chip_start_" + tag, [pr[0] for pr in pairs] + lands, 3 * n, _chip_copies(n)), pairs

    def sum_then_finish_start(tag, started, pairs, after, names, tiles, small):
        n = len(names)
        got = _exchange_wait("chip_wait_" + tag, started, after, _chip_copies(n))
        sums = [_chip_sum(pairs[b][1], got[n + b], kc if small and b == n - 1 else kc[1:],
                          (N_CHIPS, 2) if small and b == n - 1 else (2,), "chip_sum_" + names[b], tiles[b])
                for b in range(n)]
        nbig = n - 1 if small else n
        return _exchange_start("finish_start_" + tag, sums, nbig + (7 if small else 0), _finish_copies(nbig, small))

    gw_pe, dq, dh1, do, dy, g_post = _head_bwd(dout, gt, p2, o, post_g, wg_out, wg_pg, wg_pe, 256)
    gw_pe = gw_pe.reshape(N_CHIPS, 2, D_PLE // 2, W_PE_COLS)
    token0 = jnp.zeros((SUBLANES, LANES), F32)
    gw_out = _weight_grad(y, do, "grad_w_out", 2, 1, D_MODEL // 2, D_MODEL, 1024, token0)
    gw_pg = _weight_grad(h1, dq, "grad_w_pg", 2, 1, D_MODEL // 2, D_MODEL, 1024, token0)
    gw_out = gw_out.reshape(N_CHIPS, 2, W_ROWS // 2, D_MODEL)
    gw_pg = gw_pg.reshape(N_CHIPS, 2, W_ROWS // 2, D_MODEL)

    names_a, tiles_a = ["w_out", "w_pg", "w_pe"], [128, 128, 128]
    st, tok = sibling_start("a", [gw_out, gw_pg, gw_pe])
    (dz, g_oga, g_ogb, g_lng, g_lnb, g_bsx, g_ws, g_cw, g_cb, g_wa, g_ba, g_wx, g_bx, g_lam) = _branches_bwd(
        z, h, dy, prm, 256, tok)
    (st, tok), pairs_a = pair_then_chip_start("a", st, dz, names_a, tiles_a, [BF16] * 3)
    gw_in = _weight_grad(hn, dz, "grad_w_in", 2, N_CHIPS, D_MODEL // 2, W_IN_COLS, 1024, tok)
    fin_a, tok = sum_then_finish_start("a", st, pairs_a, gw_in, names_a, tiles_a, False)

    small_g = dict(
        gmlp_ln_g=g_lng[0:1], gmlp_ln_b=g_lnb[0:1], gmlp_ws=g_ws,
        gmlp_bs=jnp.transpose(g_bsx[:, ::CHUNK]), conv_w=g_cw[::SUBLANES], conv_b=g_cb[0:1], w_a=g_wa, b_a=g_ba[0:1],
        w_x=g_wx, b_x=g_bx[0:1], lam=g_lam[0:1], gmlp_out_g=g_oga[0:1], lru_out_g=g_ogb[0:1], post_g=g_post[0:1])
    gsm = _pack_small(small_g).reshape(N_CHIPS, 2, SMALL_PIECE, LANES)

    names_b, tiles_b = ["w_in", "small"], [256, SMALL_PIECE]
    half = x2.shape[0] // 256 // 2
    st, tok_b = sibling_start("b", [gw_in, gsm])
    part = _inproj_bwd(dz, wg_in, x2, dh1, pre_g, 256, 0, half, None, False, tok_b, "inproj_bwd_lo")
    f_out, f_pg, f_pe = _exchange_wait("finish_wait_a", fin_a, part[1], _finish_copies(3, False))
    (st, tok_b), pairs_b = pair_then_chip_start("b", st, part[1], names_b, tiles_b, [BF16, F32])
    grad_x, g_pre = _inproj_bwd(dz, wg_in, x2, dh1, pre_g, 256, half, half, part, True, tok_b, "inproj_bwd_hi")
    fin_b, tok_b = sum_then_finish_start("b", st, pairs_b, g_pre, names_b, tiles_b, True)
    g_pre_sum = _allreduce_vector(_rows128(g_pre[0:1]), tok_b)
    f_in, f_sm = _exchange_wait("finish_wait_b", fin_b, g_pre_sum, _finish_copies(1, True))

    big_g = dict(w_in=f_in.reshape(D_MODEL, W_IN_COLS), w_out=f_out.reshape(W_ROWS, D_MODEL),
                 w_pg=f_pg.reshape(W_ROWS, D_MODEL), w_pe=f_pe.reshape(D_PLE, W_PE_COLS))
    grads, deltas, new_m, new_v = {}, {}, {}, {}
    for n, tr in (("w_in", 256), ("w_out", 128), ("w_pg", 128), ("w_pe", 128)):
        shp = weights[n].shape
        grads[n] = big_g[n].reshape(shp)
        d, nm, nv = _adamw(weights[n][0], big_g[n], mom_m[n][0], mom_v[n][0], "adamw_" + n, tr)
        deltas[n], new_m[n], new_v[n] = d.reshape(shp), nm.reshape(shp), nv.reshape(shp)

    packed_g = f_sm.reshape(SMALL_TOTAL, LANES)
    small_names = [n for n, _ in SMALL_ROWS]
    shapes = {n: weights[n].shape for n in small_names}
    shapes["conv_w"] = (CONV_W, D_HALF)
    zero_cw = jnp.zeros((CONV_W, D_HALF), F32)
    pack_w = lambda src: _pack_small({n: (zero_cw if n == "conv_w" else src[n]) for n in small_names})
    d_sm, m_sm, v_sm = _adamw(pack_w(weights), packed_g, pack_w(mom_m), pack_w(mom_v), "adamw_small", SMALL_PIECE)
    ug, ud, um, uv = (_unpack_small(a, shapes) for a in (packed_g, d_sm, m_sm, v_sm))
    for n in small_names:
        if n != "conv_w":
            grads[n], deltas[n], new_m[n], new_v[n] = ug[n], ud[n], um[n], uv[n]
    g_conv = lax.dynamic_slice_in_dim(ug["conv_w"], me * CONV_COLS, CONV_COLS, axis=1)
    d, nm, nv = _adamw(conv_w[0, :, 0, :], g_conv, m_conv_w[0, :, 0, :], v_conv_w[0, :, 0, :], "adamw_conv_w", CONV_W)
    cshape = conv_w.shape
    grads["conv_w"], deltas["conv_w"] = g_conv.reshape(cshape), d.reshape(cshape)
    new_m["conv_w"], new_v["conv_w"] = nm.reshape(cshape), nv.reshape(cshape)
    d, nm, nv = _adamw(_rows128(pre_g), g_pre_sum, _rows128(m_pre_g), _rows128(v_pre_g), "adamw_pre_g", 16)
    pshape = pre_g.shape
    grads["pre_g"], deltas["pre_g"] = g_pre_sum.reshape(pshape), d.reshape(pshape)
    new_m["pre_g"], new_v["pre_g"] = nm.reshape(pshape), nv.reshape(pshape)

    return (loss, grad_x.reshape(x.shape), *[grads[n] for n in order], *[deltas[n] for n in order],
            *[new_m[n] for n in order], *[new_v[n] for n in order])
```

```python
import functools
import math

import jax
import jax.numpy as jnp
from jax import lax
from jax.experimental import pallas as pl
from jax.experimental.pallas import tpu as pltpu

F32 = jnp.float32
BF16 = jnp.bfloat16

D_MODEL = 2048
D_HALF = 1024
D_Z = 5120
D_PLE = 256
CHUNK = 128
N_HEADS = 8
N_CHIPS = 4
W_IN_COLS = D_Z // N_CHIPS
W_ROWS = D_MODEL // N_CHIPS
W_PE_COLS = D_MODEL // N_CHIPS
CONV_W = 4
CONV_COLS = D_HALF // N_CHIPS
EPS = 1e-6
LRU_C = 8.0
ADAM_LR, ADAM_B1, ADAM_B2, ADAM_EPS, ADAM_WD, ADAM_STEP = 0.001, 0.9, 0.999, 1e-08, 0.01, 10

SUBLANES = 8
LANES = 128
VMEM_LIMIT = 56 * 1024 * 1024

SMALL_ROWS = (("gmlp_ln_g", 8), ("gmlp_ln_b", 8), ("gmlp_ws", 1024), ("gmlp_bs", 8),
              ("conv_w", 32), ("conv_b", 8), ("w_a", 1024), ("b_a", 8), ("w_x", 1024), ("b_x", 8),
              ("lam", 8), ("gmlp_out_g", 8), ("lru_out_g", 8), ("post_g", 16))
SMALL_USED = sum(r for _, r in SMALL_ROWS)
SMALL_PIECE = 400
SMALL_TOTAL = 8 * SMALL_PIECE

MESH = pl.DeviceIdType.MESH
ANY = pl.BlockSpec(memory_space=pl.ANY)

_GELU_C0 = math.sqrt(2.0 / math.pi)
_GELU_C1 = 0.044715


def _params(*sem):
    return pltpu.CompilerParams(dimension_semantics=sem, vmem_limit_bytes=VMEM_LIMIT)


def _dot(a, b):
    return jnp.dot(a, b, preferred_element_type=F32)


def _dot_nt(a, b):
    return lax.dot_general(a, b, (((1,), (1,)), ((), ())), preferred_element_type=F32)


def _dot_tn(a, b):
    return lax.dot_general(a, b, (((0,), (0,)), ((), ())), preferred_element_type=F32)


def _gelu(x):
    t = jnp.tanh(_GELU_C0 * (x + _GELU_C1 * (x * x * x)))
    return 0.5 * x * (1.0 + t), t


def _gelu_grad(x, t):
    return 0.5 * (1.0 + t) + 0.5 * x * (1.0 - t * t) * (_GELU_C0 * (1.0 + 3.0 * _GELU_C1 * x * x))


def _rowsum8(v):
    r, n = v.shape
    return jnp.sum(v.reshape(r // SUBLANES, SUBLANES, n), axis=0)


def _lanemean(v):
    return jnp.mean(v, axis=-1, keepdims=True)


def _shift_down(v, halo8, k):
    if k == 0:
        return v
    r = pltpu.roll(v, k, 0)
    hr = pltpu.roll(halo8, k, 0)
    row = lax.broadcasted_iota(jnp.int32, halo8.shape, 0)
    top = jnp.where(row < k, hr, r[0:SUBLANES])
    return jnp.concatenate([top, r[SUBLANES:]], axis=0)


def _shift_up(v, next8, k):
    if k == 0:
        return v
    n = v.shape[0]
    r = pltpu.roll(v, n - k, 0)
    nr = pltpu.roll(next8, SUBLANES - k, 0)
    row = lax.broadcasted_iota(jnp.int32, next8.shape, 0)
    bot = jnp.where(row >= SUBLANES - k, nr, r[n - SUBLANES:])
    return jnp.concatenate([r[:n - SUBLANES], bot], axis=0)


def _layernorm_parts(vg):
    mu = _lanemean(vg)
    xc = vg - mu
    rstd = lax.rsqrt(_lanemean(xc * xc) + EPS)
    return xc * rstd, rstd


def _spatial_mix(wt_ref, vn_ref, bsx_ref, mixed_ref, tm):
    for c in range(tm // CHUNK):
        rows = slice(c * CHUNK, (c + 1) * CHUNK)
        for h in range(N_HEADS):
            cols = slice(h * CHUNK, (h + 1) * CHUNK)
            mixed_ref[rows, cols] = _dot(wt_ref[h], vn_ref[rows, cols]) + bsx_ref[:, cols]


def _conv_taps(xb, halo8):
    return [_shift_down(xb, halo8, CONV_W - 1 - k) for k in range(CONV_W)]


def _lru_gates(xc_bf_ref, wa_ref, wx_ref, ba_ref, bx_ref, r_ref, i_ref):
    for h in range(N_HEADS):
        cols = slice(h * CHUNK, (h + 1) * CHUNK)
        xh = xc_bf_ref[:, cols]
        r_ref[:, cols] = jax.nn.sigmoid(_dot(xh, wa_ref[h]) + ba_ref[:, cols])
        i_ref[:, cols] = jax.nn.sigmoid(_dot(xh, wx_ref[h]) + bx_ref[:, cols])


def _softplus_neg(lam):
    return jnp.maximum(-lam, 0.0) + jnp.log(1.0 + jnp.exp(-jnp.abs(lam)))


def _decay_parts(r, lam):
    la = (-LRU_C * _softplus_neg(lam)) * r
    a = jnp.exp(la)
    th = -jnp.tanh(la)
    mult = jnp.sqrt(2.0 * th / (1.0 + th))
    return a, mult


def _inproj_branches_fwd(x, pre_g, wg_in, prm, tm, token):
    t = x.shape[0]
    nt = t // tm
    hb = tm // SUBLANES

    def body(x_ref, g_ref, w_ref,
             lng_ref, lnb_ref, wt_ref, bsx_ref, cw_ref, cb_ref, wa_ref, wx_ref, ba_ref, bx_ref, lam_ref,
             oga_ref, ogb_ref, token_ref,
             z_ref, hn_ref, y_ref, h_ref,
             zbuf0, zbuf1, vn_s, mixed_s, xcbf_s, r_s, i_s, ug_s, halo_s, carry_s):
        s = pl.program_id(0)

        @pl.when(s == 0)
        def _():
            zbuf1[...] = jnp.zeros_like(zbuf1)

        @pl.when(s <= 1)
        def _():
            carry_s[...] = jnp.zeros_like(carry_s)
            halo_s[...] = jnp.zeros_like(halo_s)

        xv = x_ref[...]
        hn_ref[...] = (xv * lax.rsqrt(_lanemean(xv * xv) + EPS) * g_ref[...]).astype(BF16)

        def step(zw, zr):
            def project(j):
                cols = slice(j * W_IN_COLS, (j + 1) * W_IN_COLS)
                zb = _dot(hn_ref[...], w_ref[j]).astype(BF16)
                z_ref[:, cols] = zb
                zw[:, cols] = zb

            zin = lambda g: zr[:, g * D_HALF:(g + 1) * D_HALF].astype(F32)
            always = [s >= 0] * 4

            @pl.when(always[0])
            def _():
                project(0)
                ug, _ = _gelu(zin(0))
                ug_s[...] = ug
                vg, _ = _gelu(zin(1))
                vhat, _ = _layernorm_parts(vg)
                vn_s[...] = (vhat * lng_ref[...] + lnb_ref[...]).astype(BF16)

            @pl.when(always[1])
            def _():
                project(1)
                _spatial_mix(wt_ref, vn_s, bsx_ref, mixed_s, tm)
                ga = zin(2)
                ya = ug_s[...] * mixed_s[...] * (ga * jax.nn.sigmoid(ga))
                ra = lax.rsqrt(_lanemean(ya * ya) + EPS)
                y_ref[:, 0:D_HALF] = (ya * ra * oga_ref[...]).astype(BF16)

            @pl.when(always[2])
            def _():
                project(2)
                xb = zin(3)
                taps = _conv_taps(xb, halo_s[...])
                halo_s[...] = xb[tm - SUBLANES:]
                xc = cb_ref[...] + taps[0] * cw_ref[0:1, :]
                for k in range(1, CONV_W):
                    xc = xc + taps[k] * cw_ref[k:k + 1, :]
                xcbf_s[...] = xc.astype(BF16)
                _lru_gates(xcbf_s, wa_ref, wx_ref, ba_ref, bx_ref, r_s, i_s)
                a, mult = _decay_parts(r_s[...], lam_ref[...])
                row = lax.broadcasted_iota(jnp.int32, a.shape, 0)
                mult = jnp.where(jnp.logical_and(s == 1, row == 0), 1.0, mult)
                r_s[...] = a
                i_s[...] = mult * (i_s[...] * xc)

            @pl.when(always[3])
            def _():
                project(3)
                a = r_s[...]
                b = i_s[...]
                r8 = lax.broadcasted_iota(jnp.int32, a.shape, 0) & (SUBLANES - 1)
                for d in (1, 2, 4):
                    a_sh = pltpu.roll(a, d, 0)
                    b_sh = pltpu.roll(b, d, 0)
                    m = r8 >= d
                    b = jnp.where(m, a * b_sh + b, b)
                    a = jnp.where(m, a * a_sh, a)
                carry = carry_s[...]
                for g in range(hb):
                    rows = slice(g * SUBLANES, (g + 1) * SUBLANES)
                    hg = a[rows] * carry + b[rows]
                    h_ref[rows, :] = hg
                    carry = jnp.broadcast_to(hg[SUBLANES - 1:SUBLANES, :], hg.shape)
                carry_s[...] = carry
                gb = zin(4)
                yb = h_ref[...] * (gb * jax.nn.sigmoid(gb))
                rb = lax.rsqrt(_lanemean(yb * yb) + EPS)
                y_ref[:, D_HALF:] = (yb * rb * ogb_ref[...]).astype(BF16)

        @pl.when(s % 2 == 0)
        def _():
            step(zbuf0, zbuf1)

        @pl.when(s % 2 == 1)
        def _():
            step(zbuf1, zbuf0)

    const = lambda a: pl.BlockSpec(a.shape, lambda s, n=a.ndim: (0,) * n, pipeline_mode=pl.Buffered(1))
    proj = lambda n: pl.BlockSpec((tm, n), lambda s: (jnp.minimum(s, nt - 1), 0))
    head = lambda n: pl.BlockSpec((tm, n), lambda s: (jnp.maximum(s - 1, 0), 0))
    names = ("ln_g", "ln_b", "wt", "bsx", "conv_w", "conv_b", "w_a", "w_x", "b_a", "b_x", "lam", "oga", "ogb")
    pr = [prm[n] for n in names] + [token]
    big = lambda dt: pltpu.VMEM((tm, D_HALF), dt)
    return pl.pallas_call(
        body, name="inproj_branches_fwd", grid=(nt + 1,),
        in_specs=[proj(D_MODEL), const(pre_g), const(wg_in)] + [const(a) for a in pr],
        out_specs=[proj(D_Z), proj(D_MODEL), head(D_MODEL), head(D_HALF)],
        out_shape=[jax.ShapeDtypeStruct((t, D_Z), BF16), jax.ShapeDtypeStruct((t, D_MODEL), BF16),
                   jax.ShapeDtypeStruct((t, D_MODEL), BF16), jax.ShapeDtypeStruct((t, D_HALF), F32)],
        scratch_shapes=[pltpu.VMEM((tm, D_Z), BF16), pltpu.VMEM((tm, D_Z), BF16),
                        big(BF16), big(F32), big(BF16), big(F32), big(F32), big(F32),
                        pltpu.VMEM((SUBLANES, D_HALF), F32), pltpu.VMEM((SUBLANES, D_HALF), F32)],
        compiler_params=_params("arbitrary"),
    )(x, pre_g, wg_in, *pr)


def _outproj_fwd(x, y, p, tgt, post_g, w_out, w_pg, wg_pe, tm):
    t = x.shape[0]

    def body(x_ref, y_ref, p_ref, tgt_ref, pg_ref, wo_ref, wpg_ref, wpe_ref,
             o_ref, h1_ref, gt_ref, dout_ref, loss_ref):
        @pl.when(pl.program_id(0) == 0)
        def _():
            loss_ref[...] = jnp.zeros_like(loss_ref)

        o = _dot(y_ref[...], wo_ref[...])
        o_ref[...] = o
        r3 = lax.rsqrt(_lanemean(o * o) + EPS)
        h1 = x_ref[...] + (o * r3) * pg_ref[...]
        h1b = h1.astype(BF16)
        h1_ref[...] = h1b
        gt = jax.nn.sigmoid(_dot(h1b, wpg_ref[...]))
        gt_ref[...] = gt
        pb = p_ref[...].astype(BF16)
        for k in range(N_CHIPS):
            cols = slice(k * W_PE_COLS, (k + 1) * W_PE_COLS)
            pe = _dot(pb, wpe_ref[k])
            d = h1[:, cols] + pe * gt[:, cols] - tgt_ref[:, cols]
            dout_ref[:, cols] = d * (1.0 / D_MODEL)
            loss_ref[...] += jnp.sum(d * d) * (0.5 / D_MODEL)

    row = lambda n: pl.BlockSpec((tm, n), lambda i: (i, 0))
    const = lambda shp: pl.BlockSpec(shp, lambda i, n=len(shp): (0,) * n, pipeline_mode=pl.Buffered(1))
    return pl.pallas_call(
        body, name="outproj_fwd", grid=(t // tm,),
        in_specs=[row(D_MODEL), row(D_MODEL), row(D_PLE), row(D_MODEL), const((1, D_MODEL)),
                  const((D_MODEL, D_MODEL)), const((D_MODEL, D_MODEL)), const((N_CHIPS, D_PLE, W_PE_COLS))],
        out_specs=[row(D_MODEL), row(D_MODEL), row(D_MODEL), row(D_MODEL),
                   pl.BlockSpec((SUBLANES, LANES), lambda i: (0, 0))],
        out_shape=[jax.ShapeDtypeStruct((t, D_MODEL), F32), jax.ShapeDtypeStruct((t, D_MODEL), BF16),
                   jax.ShapeDtypeStruct((t, D_MODEL), F32), jax.ShapeDtypeStruct((t, D_MODEL), F32),
                   jax.ShapeDtypeStruct((SUBLANES, LANES), F32)],
        compiler_params=_params("arbitrary"),
    )(x, y, p, tgt, post_g, w_out, w_pg, wg_pe)


def _head_bwd(dout, gt, p, o, post_g, w_out, w_pg, wg_pe, tm):
    t = dout.shape[0]

    def body(dout_ref, gt_ref, p_ref, o_ref, pg_ref, wo_ref, wpg_ref, wpe_ref,
             gwpe_ref, dq_ref, dh1_ref, do_ref, dy_ref, gpost_ref):
        i = pl.program_id(0)

        @pl.when(i == 0)
        def _():
            gpost_ref[...] = jnp.zeros_like(gpost_ref)
            gwpe_ref[...] = jnp.zeros_like(gwpe_ref)

        dout = dout_ref[...]
        gt = gt_ref[...]
        pb = p_ref[...].astype(BF16)
        for k in range(N_CHIPS):
            cols = slice(k * W_PE_COLS, (k + 1) * W_PE_COLS)
            pe = _dot(pb, wpe_ref[k])
            g = gt[:, cols]
            dg = dout[:, cols] * g
            gwpe_ref[k] += _dot_tn(pb, dg.astype(BF16))
            dq_ref[:, cols] = (dg * pe * (1.0 - g)).astype(BF16)
        dh1 = dout + _dot_nt(dq_ref[...], wpg_ref[...])
        dh1_ref[...] = dh1
        o = o_ref[...]
        r3 = lax.rsqrt(_lanemean(o * o) + EPS)
        on = o * r3
        gpost_ref[...] += _rowsum8(dh1 * on)
        don = dh1 * pg_ref[...]
        do = r3 * (don - on * _lanemean(don * on))
        dob = do.astype(BF16)
        do_ref[...] = dob
        dy_ref[...] = _dot_nt(dob, wo_ref[...])

        @pl.when(i == pl.num_programs(0) - 1)
        def _():
            gpost_ref[...] = jnp.broadcast_to(jnp.sum(gpost_ref[...], axis=0, keepdims=True), gpost_ref.shape)

    row = lambda n: pl.BlockSpec((tm, n), lambda i: (i, 0))
    const = lambda shp: pl.BlockSpec(shp, lambda i, n=len(shp): (0,) * n, pipeline_mode=pl.Buffered(1))
    return pl.pallas_call(
        body, name="head_bwd", grid=(t // tm,),
        in_specs=[row(D_MODEL), row(D_MODEL), row(D_PLE), row(D_MODEL), const((1, D_MODEL)),
                  const((D_MODEL, D_MODEL)), const((D_MODEL, D_MODEL)), const((N_CHIPS, D_PLE, W_PE_COLS))],
        out_specs=[pl.BlockSpec((N_CHIPS, D_PLE, W_PE_COLS), lambda i: (0, 0, 0)),
                   row(D_MODEL), row(D_MODEL), row(D_MODEL), row(D_MODEL),
                   pl.BlockSpec((SUBLANES, D_MODEL), lambda i: (0, 0))],
        out_shape=[jax.ShapeDtypeStruct((N_CHIPS, D_PLE, W_PE_COLS), F32), jax.ShapeDtypeStruct((t, D_MODEL), BF16),
                   jax.ShapeDtypeStruct((t, D_MODEL), F32), jax.ShapeDtypeStruct((t, D_MODEL), BF16),
                   jax.ShapeDtypeStruct((t, D_MODEL), F32), jax.ShapeDtypeStruct((SUBLANES, D_MODEL), F32)],
        compiler_params=_params("arbitrary"),
    )(dout, gt, p, o, post_g, w_out, w_pg, wg_pe)


def _branches_bwd(z, h, dy, prm, tm, token):
    t = z.shape[0]
    nt = t // tm
    hb = tm // SUBLANES

    def body(u_ref, v_ref, ga_ref, xb_ref, gb_ref, xbh_ref, h_ref, hh_ref, dy_ref,
             lng_ref, lnb_ref, wt_ref, wtt_ref, bsx_ref, cw_ref, cb_ref, wa_ref, wx_ref, ba_ref, bx_ref, lam_ref,
             oga_ref, ogb_ref, token_ref,
             dz_ref, g_oga, g_ogb, g_lng, g_lnb, g_bsx, g_ws, g_cw, g_cb, g_wa, g_ba, g_wx, g_bx, g_lam,
             vn_s, mixed_s, dm_s, dvn_s, xcbf_s, r_s, i_s, a_s, b_s, dh_s, dpr_s, dpi_s, dxc_s,
             ca_s, cd_s, cx_s):
        step_i = pl.program_id(0)
        tile = nt - 1 - step_i
        accs = (g_oga, g_ogb, g_lng, g_lnb, g_bsx, g_ws, g_cw, g_cb, g_wa, g_ba, g_wx, g_bx, g_lam)

        @pl.when(step_i == 0)
        def _():
            for r in accs + (ca_s, cd_s, cx_s):
                r[...] = jnp.zeros_like(r)

        dy_a = dy_ref[:, 0:D_HALF]
        dy_b = dy_ref[:, D_HALF:]

        u = u_ref[...].astype(F32)
        ug, tu = _gelu(u)
        v = v_ref[...].astype(F32)
        vg, tv = _gelu(v)
        vhat, rstd = _layernorm_parts(vg)
        vn_s[...] = (vhat * lng_ref[...] + lnb_ref[...]).astype(BF16)
        _spatial_mix(wt_ref, vn_s, bsx_ref, mixed_s, tm)
        mixed = mixed_s[...]
        ga = ga_ref[...].astype(F32)
        sga = jax.nn.sigmoid(ga)
        sa = ga * sga
        um = ug * mixed
        ya = um * sa
        ra = lax.rsqrt(_lanemean(ya * ya) + EPS)
        yahat = ya * ra
        g_oga[...] += _rowsum8(dy_a * yahat)
        dn = dy_a * oga_ref[...]
        dya = ra * (dn - yahat * _lanemean(dn * yahat))
        dz_ref[:, 2 * D_HALF:3 * D_HALF] = (dya * um * (sga * (1.0 + ga * (1.0 - sga)))).astype(BF16)
        dz_ref[:, 0:D_HALF] = (dya * mixed * sa * _gelu_grad(u, tu)).astype(BF16)
        dmixed = dya * ug * sa
        g_bsx[...] += jnp.sum(dmixed.reshape(tm // CHUNK, CHUNK, D_HALF), axis=0)
        dm_s[...] = dmixed.astype(BF16)
        for c in range(tm // CHUNK):
            rows = slice(c * CHUNK, (c + 1) * CHUNK)
            for hd in range(N_HEADS):
                cols = slice(hd * CHUNK, (hd + 1) * CHUNK)
                dmh = dm_s[rows, cols]
                dvn_s[rows, cols] = _dot(wtt_ref[hd], dmh)
                g_ws[hd] += _dot_nt(dmh, vn_s[rows, cols])
        dvn = dvn_s[...]
        g_lng[...] += _rowsum8(dvn * vhat)
        g_lnb[...] += _rowsum8(dvn)
        dvh = dvn * lng_ref[...]
        dvg = rstd * (dvh - _lanemean(dvh) - vhat * _lanemean(dvh * vhat))
        dz_ref[:, D_HALF:2 * D_HALF] = (dvg * _gelu_grad(v, tv)).astype(BF16)

        xb = xb_ref[...].astype(F32)
        halo = jnp.where(tile == 0, 0.0, xbh_ref[...].astype(F32)[SUBLANES:])
        taps = _conv_taps(xb, halo)
        xc = cb_ref[...] + taps[0] * cw_ref[0:1, :]
        for k in range(1, CONV_W):
            xc = xc + taps[k] * cw_ref[k:k + 1, :]
        xcbf_s[...] = xc.astype(BF16)
        _lru_gates(xcbf_s, wa_ref, wx_ref, ba_ref, bx_ref, r_s, i_s)
        rg = r_s[...]
        ig = i_s[...]
        lam = lam_ref[...]
        a, mult_true = _decay_parts(rg, lam)
        row = lax.broadcasted_iota(jnp.int32, a.shape, 0)
        first = jnp.logical_and(tile == 0, row == 0)
        mult = jnp.where(first, 1.0, mult_true)
        hcur = h_ref[...]
        hprev = _shift_down(hcur, jnp.where(tile == 0, 0.0, hh_ref[...]), 1)
        gb = gb_ref[...].astype(F32)
        sgb = jax.nn.sigmoid(gb)
        sb = gb * sgb
        yb = hcur * sb
        rb = lax.rsqrt(_lanemean(yb * yb) + EPS)
        ybhat = yb * rb
        g_ogb[...] += _rowsum8(dy_b * ybhat)
        dn = dy_b * ogb_ref[...]
        dyb = rb * (dn - ybhat * _lanemean(dn * ybhat))
        dz_ref[:, 4 * D_HALF:5 * D_HALF] = (dyb * hcur * (sgb * (1.0 + gb * (1.0 - sgb)))).astype(BF16)

        an = _shift_up(a, ca_s[...], 1)
        bb = dyb * sb
        r8 = row & (SUBLANES - 1)
        for d in (1, 2, 4):
            a_sh = pltpu.roll(an, tm - d, 0)
            b_sh = pltpu.roll(bb, tm - d, 0)
            m = r8 + d < SUBLANES
            bb = jnp.where(m, an * b_sh + bb, bb)
            an = jnp.where(m, an * a_sh, an)
        a_s[...] = an
        b_s[...] = bb

        def step(g, carry):
            sl = pl.ds(pl.multiple_of((hb - 1 - g) * SUBLANES, SUBLANES), SUBLANES)
            dg = a_s[sl, :] * carry + b_s[sl, :]
            dh_s[sl, :] = dg
            return jnp.broadcast_to(dg[0:1, :], dg.shape)

        cd_s[...] = lax.fori_loop(0, hb, step, cd_s[...])
        ca_s[...] = jnp.broadcast_to(a[0:1, :], ca_s.shape)
        dh = dh_s[...]
        da = dh * hprev
        gx = ig * xc
        dla = da * a - jnp.where(first, 0.0, dh * gx * (a * a / mult_true))
        g_lam[...] += _rowsum8(dla * rg)
        dr = dla * (-LRU_C * _softplus_neg(lam))
        dpr = dr * rg * (1.0 - rg)
        dpi = (dh * mult * xc) * ig * (1.0 - ig)
        g_ba[...] += _rowsum8(dpr)
        g_bx[...] += _rowsum8(dpi)
        dpr_s[...] = dpr.astype(BF16)
        dpi_s[...] = dpi.astype(BF16)
        for hd in range(N_HEADS):
            cols = slice(hd * CHUNK, (hd + 1) * CHUNK)
            xh = xcbf_s[:, cols]
            dprh = dpr_s[:, cols]
            dpih = dpi_s[:, cols]
            g_wa[hd] += _dot_tn(xh, dprh)
            g_wx[hd] += _dot_tn(xh, dpih)
            dxc_s[:, cols] = _dot_nt(dprh, wa_ref[hd]) + _dot_nt(dpih, wx_ref[hd])
        dxc = dxc_s[...] + dh * mult * ig
        g_cb[...] += _rowsum8(dxc)
        for k in range(CONV_W):
            g_cw[k * SUBLANES:(k + 1) * SUBLANES, :] += _rowsum8(dxc * taps[k])
        nxt = cx_s[...]
        dxb = dxc * cw_ref[CONV_W - 1:CONV_W, :]
        for j in range(1, CONV_W):
            dxb = dxb + _shift_up(dxc, nxt, j) * cw_ref[CONV_W - 1 - j:CONV_W - j, :]
        dz_ref[:, 3 * D_HALF:4 * D_HALF] = dxb.astype(BF16)
        cx_s[...] = dxc[0:SUBLANES]

        @pl.when(step_i == nt - 1)
        def _():
            for r in (g_oga, g_ogb, g_lng, g_lnb, g_cb, g_ba, g_bx):
                r[...] = jnp.broadcast_to(jnp.sum(r[...], axis=0, keepdims=True), r.shape)
            lam_f = LRU_C * jax.nn.sigmoid(-lam_ref[...])
            g_lam[...] = jnp.broadcast_to(jnp.sum(g_lam[...], axis=0, keepdims=True) * lam_f, g_lam.shape)
            for k in range(CONV_W):
                blk = g_cw[k * SUBLANES:(k + 1) * SUBLANES, :]
                g_cw[k * SUBLANES:(k + 1) * SUBLANES, :] = jnp.broadcast_to(jnp.sum(blk, axis=0, keepdims=True), blk.shape)
            tri = (lax.broadcasted_iota(jnp.int32, (CHUNK, CHUNK), 0) >= lax.broadcasted_iota(jnp.int32, (CHUNK, CHUNK), 1))
            for hd in range(N_HEADS):
                cols = slice(hd * CHUNK, (hd + 1) * CHUNK)
                g_ws[hd] = jnp.where(tri, g_ws[hd], 0.0)
                blk = g_bsx[:, cols]
                g_bsx[:, cols] = jnp.broadcast_to(jnp.sum(blk, axis=1, keepdims=True), blk.shape)

    rev = lambda i: nt - 1 - i
    zspec = lambda g: pl.BlockSpec((tm, D_HALF), lambda i, g=g: (rev(i), g))
    halo = lambda col: pl.BlockSpec((SUBLANES, D_HALF), lambda i: (jnp.maximum(rev(i) * hb - 1, 0), col))
    zhalo = pl.BlockSpec((2 * SUBLANES, D_HALF), lambda i: (jnp.maximum(rev(i) * (hb // 2) - 1, 0), 3))
    full = lambda a: pl.BlockSpec(a.shape, lambda i, n=a.ndim: (0,) * n)
    acc = lambda shp: pl.BlockSpec(shp, lambda i, n=len(shp): (0,) * n)
    names = ("ln_g", "ln_b", "wt", "wtt", "bsx", "conv_w", "conv_b", "w_a", "w_x", "b_a", "b_x", "lam", "oga", "ogb")
    pr = [prm[n] for n in names] + [token]
    vec = (SUBLANES, D_HALF)
    mat = (N_HEADS, CHUNK, CHUNK)
    acc_shapes = [vec, vec, vec, vec, (CHUNK, D_HALF), mat, (CONV_W * SUBLANES, D_HALF), vec, mat, vec, mat, vec, vec]
    big = lambda dt: pltpu.VMEM((tm, D_HALF), dt)
    return pl.pallas_call(
        body, name="branches_bwd", grid=(nt,),
        in_specs=[zspec(0), zspec(1), zspec(2), zspec(3), zspec(4), zhalo,
                  pl.BlockSpec((tm, D_HALF), lambda i: (rev(i), 0)), halo(0),
                  pl.BlockSpec((tm, D_MODEL), lambda i: (rev(i), 0))] + [full(a) for a in pr],
        out_specs=[pl.BlockSpec((tm, D_Z), lambda i: (rev(i), 0))] + [acc(s) for s in acc_shapes],
        out_shape=[jax.ShapeDtypeStruct((t, D_Z), BF16)] + [jax.ShapeDtypeStruct(s, F32) for s in acc_shapes],
        scratch_shapes=[big(BF16), big(F32), big(BF16), big(F32), big(BF16), big(F32), big(F32), big(F32), big(F32),
                        big(F32), big(BF16), big(BF16), big(F32),
                        pltpu.VMEM(vec, F32), pltpu.VMEM(vec, F32), pltpu.VMEM(vec, F32)],
        compiler_params=_params("arbitrary"),
    )(z, z, z, z, z, z, h, h, dy, *pr)


def _inproj_bwd(dz, wg_in, x, dh1, pre_g, tm, tile0, nt, prev, last, token, name):
    t = x.shape[0]

    def body(*refs):
        dz_ref, w_ref, x_ref, dh1_ref, g_ref = refs[:5]
        gx_ref, gpre_ref, acc_s = refs[-3:]
        i = pl.program_id(0)

        @pl.when(i == 0)
        def _():
            gpre_ref[...] = jnp.zeros_like(gpre_ref) if prev is None else refs[7][...]

        acc = _dot_nt(dz_ref[:, 0:W_IN_COLS], w_ref[0])
        for k in range(1, N_CHIPS):
            acc = acc + _dot_nt(dz_ref[:, k * W_IN_COLS:(k + 1) * W_IN_COLS], w_ref[k])
        acc_s[...] = acc
        for s in range(tm // CHUNK):
            rows = slice(s * CHUNK, (s + 1) * CHUNK)
            xv = x_ref[rows, :]
            r = lax.rsqrt(_lanemean(xv * xv) + EPS)
            xhat = xv * r
            dhn = acc_s[rows, :]
            gpre_ref[...] += _rowsum8(dhn * xhat)
            dxh = dhn * g_ref[...]
            gx_ref[rows, :] = dh1_ref[rows, :] + r * (dxh - xhat * _lanemean(dxh * xhat))

        if last:
            @pl.when(i == nt - 1)
            def _():
                gpre_ref[...] = jnp.broadcast_to(jnp.sum(gpre_ref[...], axis=0, keepdims=True), gpre_ref.shape)

    row = lambda n: pl.BlockSpec((tm, n), lambda i: (tile0 + i, 0))
    small = lambda r: pl.BlockSpec((r, D_MODEL), lambda i: (0, 0))
    tok = pl.BlockSpec((SUBLANES, LANES), lambda i: (0, 0))
    in_specs = [row(D_Z), pl.BlockSpec(wg_in.shape, lambda i: (0, 0, 0), pipeline_mode=pl.Buffered(1)),
                row(D_MODEL), row(D_MODEL), small(1), tok]
    args = [dz, wg_in, x, dh1, pre_g, token]
    aliases = {}
    if prev is not None:
        in_specs += [ANY, small(SUBLANES)]
        args += list(prev)
        aliases = {6: 0}
    return pl.pallas_call(
        body, name=name, grid=(nt,), in_specs=in_specs, out_specs=[row(D_MODEL), small(SUBLANES)],
        out_shape=[jax.ShapeDtypeStruct((t, D_MODEL), F32), jax.ShapeDtypeStruct((SUBLANES, D_MODEL), F32)],
        input_output_aliases=aliases,
        scratch_shapes=[pltpu.VMEM((tm, D_MODEL), F32)],
        compiler_params=_params("arbitrary"),
    )(*args)


def _weight_grad(a, b, name, kb, nb, tk, tn, tt, token):
    t = a.shape[0]
    tt = min(tt, t)

    def body(a_ref, b_ref, token_ref, o_ref):
        @pl.when(pl.program_id(2) == 0)
        def _():
            o_ref[...] = jnp.zeros_like(o_ref)

        o_ref[...] += _dot_tn(a_ref[...], b_ref[...])

    return pl.pallas_call(
        body, name=name, grid=(nb, kb, t // tt),
        in_specs=[pl.BlockSpec((tt, tk), lambda j, i, s: (s, i)), pl.BlockSpec((tt, tn), lambda j, i, s: (s, j)),
                  pl.BlockSpec((SUBLANES, LANES), lambda j, i, s: (0, 0))],
        out_specs=pl.BlockSpec((None, None, tk, tn), lambda j, i, s: (j, i, 0, 0)),
        out_shape=jax.ShapeDtypeStruct((nb, kb, tk, tn), F32),
        compiler_params=_params("parallel", "parallel", "arbitrary"),
    )(a, b, token)


def _place():
    x, y, c = lax.axis_index("x"), lax.axis_index("y"), lax.axis_index("c")
    return x, y, c


def _chip_of(x, y):
    return 2 * x + y


def _gather_weights(w_in, conv_w):
    halves = [(D_MODEL // 2, W_IN_COLS)]
    nw = len(halves)

    def body(win_ref, cw_ref, gin_ref, gcw_ref, s0, b0, lsem, send_sems, recv_sems, cw_send, cw_recv):
        x, y, c = _place()
        me = _chip_of(x, y)
        sibling = (x, y, 1 - c)
        chips = [(1 - x, y), (x, 1 - y), (1 - x, 1 - y)]
        srcs = (win_ref,)
        stage = (s0,)
        bf = (b0,)
        outs = (gin_ref,)
        loads = []
        for n in range(nw):
            rows = halves[n][0]
            cp = pltpu.make_async_copy(srcs[n].at[pl.ds(c * rows, rows), :], stage[n], lsem.at[n])
            cp.start()
            loads.append(cp)
        own_cw = pltpu.make_async_copy(cw_ref, gcw_ref.at[me], lsem.at[2 * nw])
        own_cw.start()
        for n in range(nw):
            loads[n].wait()
            bf[n][...] = stage[n][...].astype(BF16)

        def copy(n, k, chip, to, src=None):
            dst = outs[n].at[chip, c]
            return pltpu.make_async_remote_copy(
                src_ref=dst if src is None else src, dst_ref=dst,
                send_sem=send_sems.at[n, k], recv_sem=recv_sems.at[n, k], device_id=to, device_id_type=MESH)

        def recv(n, k, chip, core):
            dst = outs[n].at[chip, core]
            return pltpu.make_async_remote_copy(
                src_ref=dst, dst_ref=dst, send_sem=send_sems.at[n, k], recv_sem=recv_sems.at[n, k],
                device_id=sibling, device_id_type=MESH)

        sends = []
        locals_ = []
        for n in range(nw):
            lc = pltpu.make_async_copy(bf[n], outs[n].at[me, c], lsem.at[nw + n])
            lc.start()
            locals_.append(lc)
            first = [copy(n, 0, me, sibling, src=bf[n])]
            first += [copy(n, 1 + j, me, (*chip, c), src=bf[n]) for j, chip in enumerate(chips)]
            for cp in first:
                cp.start()
            sends += first
        cws = []
        for j, chip in enumerate(chips):
            cp = pltpu.make_async_remote_copy(
                src_ref=cw_ref, dst_ref=gcw_ref.at[me], send_sem=cw_send.at[j], recv_sem=cw_recv.at[j],
                device_id=(*chip, c), device_id_type=MESH)
            cp.start()
            cws.append(cp)
        for n in range(nw):
            for j, chip in enumerate(chips):
                kj = _chip_of(*chip)
                recv(n, 1 + j, kj, c).wait_recv()
                fw = copy(n, 4 + j, kj, sibling)
                fw.start()
                sends.append(fw)
        for n in range(nw):
            recv(n, 0, me, 1 - c).wait_recv()
            for j, chip in enumerate(chips):
                recv(n, 4 + j, _chip_of(*chip), 1 - c).wait_recv()
        for j, chip in enumerate(chips):
            pltpu.make_async_remote_copy(
                src_ref=cw_ref, dst_ref=gcw_ref.at[_chip_of(*chip)], send_sem=cw_send.at[j], recv_sem=cw_recv.at[j],
                device_id=(*chip, c), device_id_type=MESH).wait_recv()
        for cp in sends + cws:
            cp.wait_send()
        for lc in locals_:
            lc.wait()
        own_cw.wait()

    out_shape = [jax.ShapeDtypeStruct((N_CHIPS, 2) + hs, BF16) for hs in halves]
    out_shape.append(jax.ShapeDtypeStruct((N_CHIPS, CONV_W, CONV_COLS), F32))
    scratch = [pltpu.VMEM(hs, F32) for hs in halves] + [pltpu.VMEM(hs, BF16) for hs in halves]
    scratch += [pltpu.SemaphoreType.DMA((2 * nw + 1,)), pltpu.SemaphoreType.DMA((nw, 7)),
                pltpu.SemaphoreType.DMA((nw, 7)), pltpu.SemaphoreType.DMA((3,)), pltpu.SemaphoreType.DMA((3,))]
    return pl.pallas_call(
        body, name="gather_w_in", in_specs=[ANY] * 2, out_specs=[ANY] * 2, out_shape=out_shape,
        scratch_shapes=scratch, compiler_params=pltpu.CompilerParams(vmem_limit_bytes=VMEM_LIMIT),
    )(w_in, conv_w)


HBM = pl.BlockSpec(memory_space=pltpu.HBM)
SEM = pl.BlockSpec(memory_space=pltpu.SEMAPHORE)
EFFECT = pltpu.SideEffectType.DATAFLOW_SIDE_EFFECTING


def _hbm(a):
    return pltpu.with_memory_space_constraint(a, pltpu.HBM)


def _landing(shape, dtype):
    return _hbm(lax.empty(shape, dtype))


def _exchange_start(name, arrays, ncopies, build, after=None):
    n = len(arrays)
    extra = [] if after is None else [after]

    def body(*refs):
        ins, token = refs[:n], refs[-1]
        send_sems, recv_sems = refs[n + len(extra)], refs[n + len(extra) + 1]
        for cp in build(ins, send_sems, recv_sems):
            cp.start()
        token[...] = jnp.zeros_like(token)

    outs = pl.pallas_call(
        body, name=name,
        out_shape=(pltpu.SemaphoreType.DMA((ncopies,)), pltpu.SemaphoreType.DMA((ncopies,)),
                   *[pltpu.HBM(a.shape, a.dtype) for a in arrays], jax.ShapeDtypeStruct((SUBLANES, LANES), F32)),
        in_specs=[HBM] * n + [ANY] * len(extra),
        out_specs=(SEM, SEM, *[HBM] * n, pl.BlockSpec(memory_space=pltpu.VMEM)),
        input_output_aliases={q: q + 2 for q in range(n)},
        compiler_params=pltpu.CompilerParams(has_side_effects=EFFECT),
    )(*[_hbm(a) for a in arrays], *extra)
    return (outs[0], outs[1], list(outs[2:2 + n])), outs[-1]


def _exchange_wait(name, started, after, build):
    send, recv, arrays = started
    n = len(arrays)

    def body(*refs):
        ins, send_sems, recv_sems = refs[:n], refs[n], refs[n + 1]
        for cp in build(ins, send_sems, recv_sems):
            cp.wait_send()
            cp.wait_recv()

    return pl.pallas_call(
        body, name=name, out_shape=tuple(pltpu.HBM(a.shape, a.dtype) for a in arrays),
        in_specs=[HBM] * n + [SEM, SEM, ANY], out_specs=tuple([HBM] * n),
        input_output_aliases={q: q for q in range(n)},
        compiler_params=pltpu.CompilerParams(has_side_effects=EFFECT),
    )(*arrays, send, recv, after)


def _exchange_wait_start(name, started, after, build_wait, ncopies, build_start):
    send, recv, arrays = started
    n = len(arrays)

    def body(*refs):
        ins, send_sems, recv_sems = refs[:n], refs[n], refs[n + 1]
        send2, recv2, token = refs[n + 3], refs[n + 4], refs[-1]
        for cp in build_wait(ins, send_sems, recv_sems):
            cp.wait_send()
            cp.wait_recv()
        for cp in build_start(ins, send2, recv2):
            cp.start()
        token[...] = jnp.zeros_like(token)

    outs = pl.pallas_call(
        body, name=name,
        out_shape=(pltpu.SemaphoreType.DMA((ncopies,)), pltpu.SemaphoreType.DMA((ncopies,)),
                   *[pltpu.HBM(a.shape, a.dtype) for a in arrays], jax.ShapeDtypeStruct((SUBLANES, LANES), F32)),
        in_specs=[HBM] * n + [SEM, SEM, ANY], out_specs=(SEM, SEM, *[HBM] * n, pl.BlockSpec(memory_space=pltpu.VMEM)),
        input_output_aliases={q: q + 2 for q in range(n)},
        compiler_params=pltpu.CompilerParams(has_side_effects=EFFECT),
    )(*arrays, send, recv, after)
    return (outs[0], outs[1], list(outs[2:2 + n])), outs[-1]


def _cast_into_slot(w, kc, name):
    rows, cols = w.shape
    tr = min(rows, 256)

    def body(kc_ref, w_ref, o_ref):
        o_ref[...] = w_ref[...].astype(BF16)

    grid_spec = pltpu.PrefetchScalarGridSpec(
        num_scalar_prefetch=1, grid=(rows // tr,),
        in_specs=[pl.BlockSpec((tr, cols), lambda r, kc: (r, 0))],
        out_specs=pl.BlockSpec((None, tr, cols), lambda r, kc: (kc[0], r, 0)))
    return pl.pallas_call(
        body, name=name, grid_spec=grid_spec, out_shape=jax.ShapeDtypeStruct((N_CHIPS, rows, cols), BF16),
        compiler_params=_params("arbitrary"),
    )(kc, w)


def _gather_ici_copies(n):
    def build(refs, send_sems, recv_sems):
        x, y, c = _place()
        mine = lambda b: refs[b].at[_chip_of(x, y), c]
        chips = [(1 - x, y), (x, 1 - y), (1 - x, 1 - y)]
        return [pltpu.make_async_remote_copy(
            src_ref=mine(b), dst_ref=mine(b), send_sem=send_sems.at[3 * b + j], recv_sem=recv_sems.at[3 * b + j],
            device_id=(*chip, c), device_id_type=MESH) for b in range(n) for j, chip in enumerate(chips)]
    return build


def _gather_relay_copies(n):
    def build(refs, send_sems, recv_sems):
        x, y, c = _place()
        chips = [(1 - x, y), (x, 1 - y), (1 - x, 1 - y)]
        cps = []
        for b in range(n):
            for j, chip in enumerate(chips):
                got = refs[b].at[_chip_of(*chip), c]
                cps.append(pltpu.make_async_remote_copy(
                    src_ref=got, dst_ref=got, send_sem=send_sems.at[3 * b + j], recv_sem=recv_sems.at[3 * b + j],
                    device_id=(x, y, 1 - c), device_id_type=MESH))
        return cps
    return build


def _sibling_copies(n):
    def build(refs, send_sems, recv_sems):
        x, y, c = _place()
        return [pltpu.make_async_remote_copy(
            src_ref=refs[b].at[:, 1 - c], dst_ref=refs[n + b], send_sem=send_sems.at[b], recv_sem=recv_sems.at[b],
            device_id=(x, y, 1 - c), device_id_type=MESH) for b in range(n)]
    return build


def _chip_copies(n):
    def build(refs, send_sems, recv_sems):
        x, y, c = _place()
        chips = [(1 - x, y), (x, 1 - y), (1 - x, 1 - y)]
        return [pltpu.make_async_remote_copy(
            src_ref=refs[b].at[_chip_of(*chip)], dst_ref=refs[n + b].at[j],
            send_sem=send_sems.at[3 * b + j], recv_sem=recv_sems.at[3 * b + j],
            device_id=(*chip, c), device_id_type=MESH) for b in range(n) for j, chip in enumerate(chips)]
    return build


def _finish_copies(n, n_all):
    def build(refs, send_sems, recv_sems):
        x, y, c = _place()
        cps = [pltpu.make_async_remote_copy(
            src_ref=refs[b].at[c], dst_ref=refs[b].at[c], send_sem=send_sems.at[b], recv_sem=recv_sems.at[b],
            device_id=(x, y, 1 - c), device_id_type=MESH) for b in range(n)]
        flips = [(fx, fy, fc) for fx in (0, 1) for fy in (0, 1) for fc in (0, 1)][1:]
        for b in range(n_all):
            mine = refs[n + b].at[_chip_of(x, y), c]
            cps += [pltpu.make_async_remote_copy(
                src_ref=mine, dst_ref=mine, send_sem=send_sems.at[n + 7 * b + q], recv_sem=recv_sems.at[n + 7 * b + q],
                device_id=(x ^ fx, y ^ fy, c ^ fc), device_id_type=MESH) for q, (fx, fy, fc) in enumerate(flips)]
        return cps
    return build


def _pair_sum(g, r1, kc, name, tr, send_dtype):
    nk, _, rows, cols = g.shape

    def body(kc_ref, g_ref, r_ref, p_ref, own_ref):
        s = g_ref[...] + r_ref[...]
        p_ref[...] = s.astype(send_dtype)

        @pl.when(pl.program_id(1) == kc_ref[0])
        def _():
            own_ref[...] = s

    grid_spec = pltpu.PrefetchScalarGridSpec(
        num_scalar_prefetch=1, grid=(rows // tr, nk),
        in_specs=[pl.BlockSpec((None, None, tr, cols), lambda r, k, kc: (k, kc[1], r, 0)),
                  pl.BlockSpec((None, tr, cols), lambda r, k, kc: (k, r, 0))],
        out_specs=[pl.BlockSpec((None, tr, cols), lambda r, k, kc: (k, r, 0)),
                   pl.BlockSpec((tr, cols), lambda r, k, kc: (r, 0))])
    return pl.pallas_call(
        body, name=name, grid_spec=grid_spec,
        out_shape=[jax.ShapeDtypeStruct((nk, rows, cols), send_dtype), jax.ShapeDtypeStruct((rows, cols), F32)],
        compiler_params=_params("arbitrary", "arbitrary"),
    )(kc, g, r1)


def _chip_sum(own, r2, slot, lead, name, tr):
    rows, cols = own.shape
    nl = len(lead)

    def body(slot_ref, o_ref, r_ref, s_ref):
        s = o_ref[...]
        for j in range(3):
            s = s + r_ref[j].astype(F32)
        s_ref[...] = s

    grid_spec = pltpu.PrefetchScalarGridSpec(
        num_scalar_prefetch=1, grid=(rows // tr,),
        in_specs=[pl.BlockSpec((tr, cols), lambda r, sl: (r, 0)), pl.BlockSpec((3, tr, cols), lambda r, sl: (0, r, 0))],
        out_specs=pl.BlockSpec((None,) * nl + (tr, cols), lambda r, sl: tuple(sl[q] for q in range(nl)) + (r, 0)))
    return pl.pallas_call(
        body, name=name, grid_spec=grid_spec, out_shape=jax.ShapeDtypeStruct(tuple(lead) + (rows, cols), F32),
        compiler_params=_params("arbitrary"),
    )(slot, own, r2)


def _adam_update(w, g, m, v):
    nm = ADAM_B1 * m + (1.0 - ADAM_B1) * g
    nv = ADAM_B2 * v + (1.0 - ADAM_B2) * (g * g)
    m_hat = nm / (1.0 - ADAM_B1 ** ADAM_STEP)
    v_hat = nv / (1.0 - ADAM_B2 ** ADAM_STEP)
    return -ADAM_LR * (m_hat / (jnp.sqrt(v_hat) + ADAM_EPS) + ADAM_WD * w), nm, nv


def _adamw(w, g, m, v, name, tr):
    rows, cols = w.shape

    def body(w_ref, g_ref, m_ref, v_ref, go_ref, d_ref, nm_ref, nv_ref):
        gv = g_ref[...]
        go_ref[...] = gv
        d_ref[...], nm_ref[...], nv_ref[...] = _adam_update(w_ref[...], gv, m_ref[...], v_ref[...])

    spec = pl.BlockSpec((tr, cols), lambda r: (r, 0))
    return pl.pallas_call(
        body, name=name, grid=(rows // tr,), in_specs=[spec] * 4, out_specs=[spec] * 4,
        out_shape=[jax.ShapeDtypeStruct((rows, cols), F32)] * 4,
        compiler_params=_params("parallel"),
    )(w, g, m, v)


def _adamw_small(packed_g, pre_g_parts, ws, ms, vs):
    names = ["pre_g"] + [n for n, _ in SMALL_ROWS if n != "conv_w"]
    rows = dict(SMALL_ROWS)
    offset, at = {}, 0
    for n, r in SMALL_ROWS:
        offset[n] = at
        at += r
    k = len(names)

    def body(*refs):
        g_ref, pg_ref = refs[0], refs[1]
        w_refs, m_refs, v_refs = refs[2:2 + k], refs[2 + k:2 + 2 * k], refs[2 + 2 * k:2 + 3 * k]
        outs = refs[2 + 3 * k:]
        go, do, mo, vo = outs[:k], outs[k:2 * k], outs[2 * k:3 * k], outs[3 * k:]
        pre = pg_ref[0]
        for dev in range(1, 8):
            pre = pre + pg_ref[dev]
        for i, n in enumerate(names):
            shp = w_refs[i].shape
            if len(shp) == 2 and shp[0] == 1:
                for r in range(shp[1] // LANES):
                    cols = slice(r * LANES, (r + 1) * LANES)
                    g = pre[r:r + 1, :] if n == "pre_g" else g_ref[offset[n] + r:offset[n] + r + 1, :]
                    go[i][:, cols] = g
                    do[i][:, cols], mo[i][:, cols], vo[i][:, cols] = _adam_update(
                        w_refs[i][:, cols], g, m_refs[i][:, cols], v_refs[i][:, cols])
            else:
                g = g_ref[offset[n]:offset[n] + rows[n], :].reshape(shp)
                go[i][...] = g
                do[i][...], mo[i][...], vo[i][...] = _adam_update(w_refs[i][...], g, m_refs[i][...], v_refs[i][...])

    vm = pl.BlockSpec(memory_space=pltpu.VMEM)
    args = [packed_g, pre_g_parts] + [src[n] for src in (ws, ms, vs) for n in names]
    out_shape = [jax.ShapeDtypeStruct(ws[n].shape, F32) for _ in range(4) for n in names]
    outs = pl.pallas_call(
        body, name="adamw_small", in_specs=[vm] * len(args), out_specs=[vm] * (4 * k), out_shape=out_shape,
    )(*args)
    return [dict(zip(names, outs[q * k:(q + 1) * k])) for q in range(4)]


def _into_slot(v, slot, lead, name):
    n = v.shape[1]
    nl = len(lead)

    def body(slot_ref, v_ref, o_ref):
        for r in range(n // LANES):
            o_ref[r:r + 1, :] = v_ref[0:1, r * LANES:(r + 1) * LANES]

    grid_spec = pltpu.PrefetchScalarGridSpec(
        num_scalar_prefetch=1, grid=(1,),
        in_specs=[pl.BlockSpec(v.shape, lambda i, sl: (0, 0))],
        out_specs=pl.BlockSpec((None,) * nl + (n // LANES, LANES), lambda i, sl: tuple(sl[q] for q in range(nl)) + (0, 0)))
    return pl.pallas_call(
        body, name=name, grid_spec=grid_spec, out_shape=jax.ShapeDtypeStruct(tuple(lead) + (n // LANES, LANES), F32),
    )(slot, v)


def _rows128(a):
    return a.reshape(-1, LANES)


def _pack_small(parts):
    pieces = [_rows128(parts[n]) for n, _ in SMALL_ROWS]
    pieces.append(jnp.zeros((SMALL_TOTAL - SMALL_USED, LANES), F32))
    return jnp.concatenate(pieces, axis=0)


def kernel(x, p, pre_g, w_in, gmlp_ln_g, gmlp_ln_b, gmlp_ws, gmlp_bs, conv_w, conv_b, w_a, b_a, w_x, b_x, lam, gmlp_out_g, lru_out_g, w_out, post_g, w_pe, w_pg, loss_target, m_pre_g, m_w_in, m_gmlp_ln_g, m_gmlp_ln_b, m_gmlp_ws, m_gmlp_bs, m_conv_w, m_conv_b, m_w_a, m_b_a, m_w_x, m_b_x, m_lam, m_gmlp_out_g, m_lru_out_g, m_w_out, m_post_g, m_w_pe, m_w_pg, v_pre_g, v_w_in, v_gmlp_ln_g, v_gmlp_ln_b, v_gmlp_ws, v_gmlp_bs, v_conv_w, v_conv_b, v_w_a, v_b_a, v_w_x, v_b_x, v_lam, v_gmlp_out_g, v_lru_out_g, v_w_out, v_post_g, v_w_pe, v_w_pg):
    weights = dict(pre_g=pre_g, w_in=w_in, gmlp_ln_g=gmlp_ln_g, gmlp_ln_b=gmlp_ln_b, gmlp_ws=gmlp_ws, gmlp_bs=gmlp_bs,
                   conv_w=conv_w, conv_b=conv_b, w_a=w_a, b_a=b_a, w_x=w_x, b_x=b_x, lam=lam, gmlp_out_g=gmlp_out_g,
                   lru_out_g=lru_out_g, w_out=w_out, post_g=post_g, w_pe=w_pe, w_pg=w_pg)
    mom_m = dict(pre_g=m_pre_g, w_in=m_w_in, gmlp_ln_g=m_gmlp_ln_g, gmlp_ln_b=m_gmlp_ln_b, gmlp_ws=m_gmlp_ws,
                 gmlp_bs=m_gmlp_bs, conv_w=m_conv_w, conv_b=m_conv_b, w_a=m_w_a, b_a=m_b_a, w_x=m_w_x, b_x=m_b_x,
                 lam=m_lam, gmlp_out_g=m_gmlp_out_g, lru_out_g=m_lru_out_g, w_out=m_w_out, post_g=m_post_g,
                 w_pe=m_w_pe, w_pg=m_w_pg)
    mom_v = dict(pre_g=v_pre_g, w_in=v_w_in, gmlp_ln_g=v_gmlp_ln_g, gmlp_ln_b=v_gmlp_ln_b, gmlp_ws=v_gmlp_ws,
                 gmlp_bs=v_gmlp_bs, conv_w=v_conv_w, conv_b=v_conv_b, w_a=v_w_a, b_a=v_b_a, w_x=v_w_x, b_x=v_b_x,
                 lam=v_lam, gmlp_out_g=v_gmlp_out_g, lru_out_g=v_lru_out_g, w_out=v_w_out, post_g=v_post_g,
                 w_pe=v_w_pe, w_pg=v_w_pg)
    order = list(weights)
    xi, yi, ci = _place()
    me = _chip_of(xi, yi)
    kc = jnp.stack([me, ci]).astype(jnp.int32)

    x2 = x[0]
    p2 = p[0, 0]
    tgt = loss_target[0]

    g_in, g_cw = _gather_weights(w_in[0], conv_w[0, :, 0, :])
    wg_in = g_in.reshape(N_CHIPS, D_MODEL, W_IN_COLS)
    cw_full = jnp.transpose(g_cw, (1, 0, 2)).reshape(CONV_W, D_HALF)
    later = [_cast_into_slot(w_out[0], kc, "cast_w_out").reshape(N_CHIPS, 2, W_ROWS // 2, D_MODEL),
             _cast_into_slot(w_pg[0], kc, "cast_w_pg").reshape(N_CHIPS, 2, W_ROWS // 2, D_MODEL),
             _cast_into_slot(w_pe[0], kc, "cast_w_pe").reshape(N_CHIPS, 2, D_PLE // 2, W_PE_COLS)]
    gather_st, gather_tok = _exchange_start("gather_start", later, 9, _gather_ici_copies(3), after=g_cw)

    causal = jnp.tril(jnp.ones((CHUNK, CHUNK), dtype=bool))
    ws_m = jnp.where(causal[None], gmlp_ws[0], 0.0)
    prm = dict(
        ln_g=gmlp_ln_g, ln_b=gmlp_ln_b, wt=ws_m.astype(BF16), wtt=jnp.transpose(ws_m, (0, 2, 1)).astype(BF16),
        bsx=jnp.repeat(jnp.transpose(gmlp_bs[0]), CHUNK, axis=1),
        conv_w=cw_full, conv_b=conv_b, w_a=w_a[0].astype(BF16), w_x=w_x[0].astype(BF16),
        b_a=b_a[0].reshape(1, D_HALF), b_x=b_x[0].reshape(1, D_HALF), lam=lam, oga=gmlp_out_g, ogb=lru_out_g)

    z, hn, y, h = _inproj_branches_fwd(x2, pre_g, wg_in, prm, 256, gather_tok)
    gather_st, gather_tok = _exchange_wait_start("gather_relay", gather_st, y, _gather_ici_copies(3), 9,
                                                 _gather_relay_copies(3))
    g_out, g_pg, g_pe = _exchange_wait("gather_wait", gather_st, gather_tok, _gather_relay_copies(3))
    wg_out = g_out.reshape(D_MODEL, D_MODEL)
    wg_pg = g_pg.reshape(D_MODEL, D_MODEL)
    wg_pe = g_pe.reshape(N_CHIPS, D_PLE, W_PE_COLS)
    o, h1, gt, dout, loss_acc = _outproj_fwd(x2, y, p2, tgt, post_g, wg_out, wg_pg, wg_pe, 256)
    loss = lax.psum(loss_acc[0, 0], ("x", "y", "c"))

    def sibling_start(tag, bufs):
        lands = [_landing((b.shape[0],) + b.shape[2:], b.dtype) for b in bufs]
        return _exchange_start("sibling_start_" + tag, bufs + lands, len(bufs), _sibling_copies(len(bufs)))

    def pair_then_chip_start(tag, started, after, names, tiles, dtypes):
        n = len(names)
        got = _exchange_wait("sibling_wait_" + tag, started, after, _sibling_copies(n))
        pairs = [_pair_sum(got[b], got[n + b], kc, "pair_sum_" + names[b], tiles[b], dtypes[b]) for b in range(n)]
        lands = [_landing((3,) + pr[0].shape[1:], pr[0].dtype) for pr in pairs]
        return _exchange_start("chip_start_" + tag, [pr[0] for pr in pairs] + lands, 3 * n, _chip_copies(n)), pairs

    def sum_then_finish_start(tag, started, pairs, after, names, tiles, small, to_all=()):
        n = len(names)
        got = _exchange_wait("chip_wait_" + tag, started, after, _chip_copies(n))
        sums = [_chip_sum(pairs[b][1], got[n + b], kc if small and b == n - 1 else kc[1:],
                          (N_CHIPS, 2) if small and b == n - 1 else (2,), "chip_sum_" + names[b], tiles[b])
                for b in range(n)]
        nbig = n - 1 if small else n
        n_all = n - nbig + len(to_all)
        return _exchange_start("finish_start_" + tag, sums + list(to_all), nbig + 7 * n_all,
                               _finish_copies(nbig, n_all))

    gw_pe, dq, dh1, do, dy, g_post = _head_bwd(dout, gt, p2, o, post_g, wg_out, wg_pg, wg_pe, 256)
    gw_pe = gw_pe.reshape(N_CHIPS, 2, D_PLE // 2, W_PE_COLS)
    token0 = jnp.zeros((SUBLANES, LANES), F32)
    gw_out = _weight_grad(y, do, "grad_w_out", 2, 1, D_MODEL // 2, D_MODEL, 1024, token0)
    gw_pg = _weight_grad(h1, dq, "grad_w_pg", 2, 1, D_MODEL // 2, D_MODEL, 1024, token0)
    gw_out = gw_out.reshape(N_CHIPS, 2, W_ROWS // 2, D_MODEL)
    gw_pg = gw_pg.reshape(N_CHIPS, 2, W_ROWS // 2, D_MODEL)

    names_a, tiles_a = ["w_out", "w_pg", "w_pe"], [128, 128, 128]
    st, tok = sibling_start("a", [gw_out, gw_pg, gw_pe])
    (dz, g_oga, g_ogb, g_lng, g_lnb, g_bsx, g_ws, g_cw, g_cb, g_wa, g_ba, g_wx, g_bx, g_lam) = _branches_bwd(
        z, h, dy, prm, 256, tok)
    (st, tok), pairs_a = pair_then_chip_start("a", st, dz, names_a, tiles_a, [BF16] * 3)
    gw_in = _weight_grad(hn, dz, "grad_w_in", 2, N_CHIPS, D_MODEL // 2, W_IN_COLS, 1024, tok)
    fin_a, tok = sum_then_finish_start("a", st, pairs_a, gw_in, names_a, tiles_a, False)

    small_g = dict(
        gmlp_ln_g=g_lng[0:1], gmlp_ln_b=g_lnb[0:1], gmlp_ws=g_ws,
        gmlp_bs=jnp.transpose(g_bsx[:, ::CHUNK]), conv_w=g_cw[::SUBLANES], conv_b=g_cb[0:1], w_a=g_wa, b_a=g_ba[0:1],
        w_x=g_wx, b_x=g_bx[0:1], lam=g_lam[0:1], gmlp_out_g=g_oga[0:1], lru_out_g=g_ogb[0:1], post_g=g_post[0:1])
    gsm = _pack_small(small_g).reshape(N_CHIPS, 2, SMALL_PIECE, LANES)

    names_b, tiles_b = ["w_in", "small"], [256, SMALL_PIECE]
    half = x2.shape[0] // 256 // 2
    st, tok_b = _exchange_start(
        "sibling_start_b", [gw_in, gsm] + [_landing((N_CHIPS,) + b.shape[2:], F32) for b in (gw_in, gsm)], 2,
        _sibling_copies(2), after=tok)
    part = _inproj_bwd(dz, wg_in, x2, dh1, pre_g, 256, 0, half, None, False, tok_b, "inproj_bwd_lo")
    f_out, f_pg, f_pe = _exchange_wait("finish_wait_a", fin_a, part[1], _finish_copies(3, 0))
    (st, tok_b), pairs_b = pair_then_chip_start("b", st, part[1], names_b, tiles_b, [BF16, F32])
    grad_x, g_pre = _inproj_bwd(dz, wg_in, x2, dh1, pre_g, 256, half, half, part, True, tok_b, "inproj_bwd_hi")
    pre_parts = _into_slot(g_pre, kc, (N_CHIPS, 2), "pre_g_into_slot")
    fin_b, tok_b = sum_then_finish_start("b", st, pairs_b, g_pre, names_b, tiles_b, True, to_all=[pre_parts])
    f_in, f_sm, pre_parts = _exchange_wait("finish_wait_b", fin_b, tok_b, _finish_copies(1, 2))

    big_g = dict(w_in=f_in.reshape(D_MODEL, W_IN_COLS), w_out=f_out.reshape(W_ROWS, D_MODEL),
                 w_pg=f_pg.reshape(W_ROWS, D_MODEL), w_pe=f_pe.reshape(D_PLE, W_PE_COLS))
    grads, deltas, new_m, new_v = {}, {}, {}, {}
    for n, tr in (("w_in", 256), ("w_out", 128), ("w_pg", 128), ("w_pe", 128)):
        shp = weights[n].shape
        g, d, nm, nv = _adamw(weights[n][0], big_g[n], mom_m[n][0], mom_v[n][0], "adamw_" + n, tr)
        grads[n], deltas[n], new_m[n], new_v[n] = g.reshape(shp), d.reshape(shp), nm.reshape(shp), nv.reshape(shp)

    packed_g = f_sm.reshape(SMALL_TOTAL, LANES)
    small_names = ["pre_g"] + [n for n, _ in SMALL_ROWS if n != "conv_w"]
    natural = lambda src: {n: (src[n] if src[n].ndim == 2 else src[n][0]) for n in small_names}
    outs = _adamw_small(packed_g, pre_parts.reshape(8, D_MODEL // LANES, LANES), natural(weights), natural(mom_m),
                        natural(mom_v))
    for dst, got in zip((grads, deltas, new_m, new_v), outs):
        for n in small_names:
            dst[n] = got[n].reshape(weights[n].shape)
    at = sum(r for n, r in SMALL_ROWS[:[n for n, _ in SMALL_ROWS].index("conv_w")])
    g_cw_all = packed_g[at:at + CONV_W * D_HALF // LANES].reshape(CONV_W, D_HALF)
    g_conv = lax.dynamic_slice_in_dim(g_cw_all, me * CONV_COLS, CONV_COLS, axis=1)
    g, d, nm, nv = _adamw(conv_w[0, :, 0, :], g_conv, m_conv_w[0, :, 0, :], v_conv_w[0, :, 0, :], "adamw_conv_w", CONV_W)
    cshape = conv_w.shape
    grads["conv_w"], deltas["conv_w"] = g.reshape(cshape), d.reshape(cshape)
    new_m["conv_w"], new_v["conv_w"] = nm.reshape(cshape), nv.reshape(cshape)

    return (loss, grad_x.reshape(x.shape), *[grads[n] for n in order], *[deltas[n] for n in order],
            *[new_m[n] for n in order], *[new_v[n] for n in order])
```

```python
import functools
import math

import jax
import jax.numpy as jnp
from jax import lax
from jax.experimental import pallas as pl
from jax.experimental.pallas import tpu as pltpu

F32 = jnp.float32
BF16 = jnp.bfloat16

D_MODEL = 2048
D_HALF = 1024
D_Z = 5120
D_PLE = 256
CHUNK = 128
N_HEADS = 8
N_CHIPS = 4
W_IN_COLS = D_Z // N_CHIPS
W_ROWS = D_MODEL // N_CHIPS
W_PE_COLS = D_MODEL // N_CHIPS
CONV_W = 4
CONV_COLS = D_HALF // N_CHIPS
EPS = 1e-6
LRU_C = 8.0
ADAM_LR, ADAM_B1, ADAM_B2, ADAM_EPS, ADAM_WD, ADAM_STEP = 0.001, 0.9, 0.999, 1e-08, 0.01, 10

SUBLANES = 8
LANES = 128
VMEM_LIMIT = 56 * 1024 * 1024

SMALL_ROWS = (("gmlp_ln_g", 8), ("gmlp_ln_b", 8), ("gmlp_ws", 1024), ("gmlp_bs", 8),
              ("conv_w", 32), ("conv_b", 8), ("w_a", 1024), ("b_a", 8), ("w_x", 1024), ("b_x", 8),
              ("lam", 8), ("gmlp_out_g", 8), ("lru_out_g", 8), ("post_g", 16))
SMALL_USED = sum(r for _, r in SMALL_ROWS)
SMALL_PIECE = 400
SMALL_TOTAL = 8 * SMALL_PIECE

MESH = pl.DeviceIdType.MESH
ANY = pl.BlockSpec(memory_space=pl.ANY)

_GELU_C0 = math.sqrt(2.0 / math.pi)
_GELU_C1 = 0.044715


def _params(*sem):
    return pltpu.CompilerParams(dimension_semantics=sem, vmem_limit_bytes=VMEM_LIMIT)


def _dot(a, b):
    return jnp.dot(a, b, preferred_element_type=F32)


def _dot_nt(a, b):
    return lax.dot_general(a, b, (((1,), (1,)), ((), ())), preferred_element_type=F32)


def _dot_tn(a, b):
    return lax.dot_general(a, b, (((0,), (0,)), ((), ())), preferred_element_type=F32)


def _gelu(x):
    t = jnp.tanh(_GELU_C0 * (x + _GELU_C1 * (x * x * x)))
    return 0.5 * x * (1.0 + t), t


def _gelu_grad(x, t):
    return 0.5 * (1.0 + t) + 0.5 * x * (1.0 - t * t) * (_GELU_C0 * (1.0 + 3.0 * _GELU_C1 * x * x))


def _rowsum8(v):
    r, n = v.shape
    return jnp.sum(v.reshape(r // SUBLANES, SUBLANES, n), axis=0)


def _lanemean(v):
    return jnp.mean(v, axis=-1, keepdims=True)


def _shift_down(v, halo8, k):
    if k == 0:
        return v
    r = pltpu.roll(v, k, 0)
    hr = pltpu.roll(halo8, k, 0)
    row = lax.broadcasted_iota(jnp.int32, halo8.shape, 0)
    top = jnp.where(row < k, hr, r[0:SUBLANES])
    return jnp.concatenate([top, r[SUBLANES:]], axis=0)


def _shift_up(v, next8, k):
    if k == 0:
        return v
    n = v.shape[0]
    r = pltpu.roll(v, n - k, 0)
    nr = pltpu.roll(next8, SUBLANES - k, 0)
    row = lax.broadcasted_iota(jnp.int32, next8.shape, 0)
    bot = jnp.where(row >= SUBLANES - k, nr, r[n - SUBLANES:])
    return jnp.concatenate([r[:n - SUBLANES], bot], axis=0)


def _layernorm_parts(vg):
    mu = _lanemean(vg)
    xc = vg - mu
    rstd = lax.rsqrt(_lanemean(xc * xc) + EPS)
    return xc * rstd, rstd


def _spatial_mix(wt_ref, vn_ref, bsx_ref, mixed_ref, tm):
    for c in range(tm // CHUNK):
        rows = slice(c * CHUNK, (c + 1) * CHUNK)
        for h in range(N_HEADS):
            cols = slice(h * CHUNK, (h + 1) * CHUNK)
            mixed_ref[rows, cols] = _dot(wt_ref[h], vn_ref[rows, cols]) + bsx_ref[:, cols]


def _conv_taps(xb, halo8):
    return [_shift_down(xb, halo8, CONV_W - 1 - k) for k in range(CONV_W)]


def _lru_gates(xc_bf_ref, wa_ref, wx_ref, ba_ref, bx_ref, r_ref, i_ref):
    for h in range(N_HEADS):
        cols = slice(h * CHUNK, (h + 1) * CHUNK)
        xh = xc_bf_ref[:, cols]
        r_ref[:, cols] = jax.nn.sigmoid(_dot(xh, wa_ref[h]) + ba_ref[:, cols])
        i_ref[:, cols] = jax.nn.sigmoid(_dot(xh, wx_ref[h]) + bx_ref[:, cols])


def _softplus_neg(lam):
    return jnp.maximum(-lam, 0.0) + jnp.log(1.0 + jnp.exp(-jnp.abs(lam)))


def _decay_parts(r, lam):
    la = (-LRU_C * _softplus_neg(lam)) * r
    a = jnp.exp(la)
    th = -jnp.tanh(la)
    mult = jnp.sqrt(2.0 * th / (1.0 + th))
    return a, mult


def _inproj_branches_fwd(x, pre_g, wg_in, prm, tm, token):
    t = x.shape[0]
    nt = t // tm
    hb = tm // SUBLANES

    def body(x_ref, g_ref, w_ref,
             lng_ref, lnb_ref, wt_ref, bsx_ref, cw_ref, cb_ref, wa_ref, wx_ref, ba_ref, bx_ref, lam_ref,
             oga_ref, ogb_ref, token_ref,
             z_ref, hn_ref, y_ref, h_ref,
             zbuf0, zbuf1, vn_s, mixed_s, xcbf_s, r_s, i_s, ug_s, halo_s, carry_s):
        s = pl.program_id(0)

        @pl.when(s == 0)
        def _():
            zbuf1[...] = jnp.zeros_like(zbuf1)

        @pl.when(s <= 1)
        def _():
            carry_s[...] = jnp.zeros_like(carry_s)
            halo_s[...] = jnp.zeros_like(halo_s)

        xv = x_ref[...]
        hn_ref[...] = (xv * lax.rsqrt(_lanemean(xv * xv) + EPS) * g_ref[...]).astype(BF16)

        def step(zw, zr):
            def project(j):
                cols = slice(j * W_IN_COLS, (j + 1) * W_IN_COLS)
                zb = _dot(hn_ref[...], w_ref[j]).astype(BF16)
                z_ref[:, cols] = zb
                zw[:, cols] = zb

            zin = lambda g: zr[:, g * D_HALF:(g + 1) * D_HALF].astype(F32)
            always = [s >= 0] * 4

            @pl.when(always[0])
            def _():
                project(0)
                ug, _ = _gelu(zin(0))
                ug_s[...] = ug
                vg, _ = _gelu(zin(1))
                vhat, _ = _layernorm_parts(vg)
                vn_s[...] = (vhat * lng_ref[...] + lnb_ref[...]).astype(BF16)

            @pl.when(always[1])
            def _():
                project(1)
                _spatial_mix(wt_ref, vn_s, bsx_ref, mixed_s, tm)
                ga = zin(2)
                ya = ug_s[...] * mixed_s[...] * (ga * jax.nn.sigmoid(ga))
                ra = lax.rsqrt(_lanemean(ya * ya) + EPS)
                y_ref[:, 0:D_HALF] = (ya * ra * oga_ref[...]).astype(BF16)

            @pl.when(always[2])
            def _():
                project(2)
                xb = zin(3)
                taps = _conv_taps(xb, halo_s[...])
                halo_s[...] = xb[tm - SUBLANES:]
                xc = cb_ref[...] + taps[0] * cw_ref[0:1, :]
                for k in range(1, CONV_W):
                    xc = xc + taps[k] * cw_ref[k:k + 1, :]
                xcbf_s[...] = xc.astype(BF16)
                _lru_gates(xcbf_s, wa_ref, wx_ref, ba_ref, bx_ref, r_s, i_s)
                a, mult = _decay_parts(r_s[...], lam_ref[...])
                row = lax.broadcasted_iota(jnp.int32, a.shape, 0)
                mult = jnp.where(jnp.logical_and(s == 1, row == 0), 1.0, mult)
                r_s[...] = a
                i_s[...] = mult * (i_s[...] * xc)

            @pl.when(always[3])
            def _():
                project(3)
                a = r_s[...]
                b = i_s[...]
                r8 = lax.broadcasted_iota(jnp.int32, a.shape, 0) & (SUBLANES - 1)
                for d in (1, 2, 4):
                    a_sh = pltpu.roll(a, d, 0)
                    b_sh = pltpu.roll(b, d, 0)
                    m = r8 >= d
                    b = jnp.where(m, a * b_sh + b, b)
                    a = jnp.where(m, a * a_sh, a)
                carry = carry_s[...]
                for g in range(hb):
                    rows = slice(g * SUBLANES, (g + 1) * SUBLANES)
                    hg = a[rows] * carry + b[rows]
                    h_ref[rows, :] = hg
                    carry = jnp.broadcast_to(hg[SUBLANES - 1:SUBLANES, :], hg.shape)
                carry_s[...] = carry
                gb = zin(4)
                yb = h_ref[...] * (gb * jax.nn.sigmoid(gb))
                rb = lax.rsqrt(_lanemean(yb * yb) + EPS)
                y_ref[:, D_HALF:] = (yb * rb * ogb_ref[...]).astype(BF16)

        @pl.when(s % 2 == 0)
        def _():
            step(zbuf0, zbuf1)

        @pl.when(s % 2 == 1)
        def _():
            step(zbuf1, zbuf0)

    const = lambda a: pl.BlockSpec(a.shape, lambda s, n=a.ndim: (0,) * n, pipeline_mode=pl.Buffered(1))
    proj = lambda n: pl.BlockSpec((tm, n), lambda s: (jnp.minimum(s, nt - 1), 0))
    head = lambda n: pl.BlockSpec((tm, n), lambda s: (jnp.maximum(s - 1, 0), 0))
    names = ("ln_g", "ln_b", "wt", "bsx", "conv_w", "conv_b", "w_a", "w_x", "b_a", "b_x", "lam", "oga", "ogb")
    pr = [prm[n] for n in names] + [token]
    big = lambda dt: pltpu.VMEM((tm, D_HALF), dt)
    return pl.pallas_call(
        body, name="inproj_branches_fwd", grid=(nt + 1,),
        in_specs=[proj(D_MODEL), const(pre_g), const(wg_in)] + [const(a) for a in pr],
        out_specs=[proj(D_Z), proj(D_MODEL), head(D_MODEL), head(D_HALF)],
        out_shape=[jax.ShapeDtypeStruct((t, D_Z), BF16), jax.ShapeDtypeStruct((t, D_MODEL), BF16),
                   jax.ShapeDtypeStruct((t, D_MODEL), BF16), jax.ShapeDtypeStruct((t, D_HALF), F32)],
        scratch_shapes=[pltpu.VMEM((tm, D_Z), BF16), pltpu.VMEM((tm, D_Z), BF16),
                        big(BF16), big(F32), big(BF16), big(F32), big(F32), big(F32),
                        pltpu.VMEM((SUBLANES, D_HALF), F32), pltpu.VMEM((SUBLANES, D_HALF), F32)],
        compiler_params=_params("arbitrary"),
    )(x, pre_g, wg_in, *pr)


def _outproj_fwd(x, y, p, tgt, post_g, w_out, w_pg, wg_pe, tm):
    t = x.shape[0]

    def body(x_ref, y_ref, p_ref, tgt_ref, pg_ref, wo_ref, wpg_ref, wpe_ref,
             o_ref, h1_ref, gt_ref, dout_ref, loss_ref):
        @pl.when(pl.program_id(0) == 0)
        def _():
            loss_ref[...] = jnp.zeros_like(loss_ref)

        o = _dot(y_ref[...], wo_ref[...])
        o_ref[...] = o
        r3 = lax.rsqrt(_lanemean(o * o) + EPS)
        h1 = x_ref[...] + (o * r3) * pg_ref[...]
        h1b = h1.astype(BF16)
        h1_ref[...] = h1b
        gt = jax.nn.sigmoid(_dot(h1b, wpg_ref[...]))
        gt_ref[...] = gt
        pb = p_ref[...].astype(BF16)
        for k in range(N_CHIPS):
            cols = slice(k * W_PE_COLS, (k + 1) * W_PE_COLS)
            pe = _dot(pb, wpe_ref[k])
            d = h1[:, cols] + pe * gt[:, cols] - tgt_ref[:, cols]
            dout_ref[:, cols] = d * (1.0 / D_MODEL)
            loss_ref[...] += jnp.sum(d * d) * (0.5 / D_MODEL)

    row = lambda n: pl.BlockSpec((tm, n), lambda i: (i, 0))
    const = lambda shp: pl.BlockSpec(shp, lambda i, n=len(shp): (0,) * n, pipeline_mode=pl.Buffered(1))
    return pl.pallas_call(
        body, name="outproj_fwd", grid=(t // tm,),
        in_specs=[row(D_MODEL), row(D_MODEL), row(D_PLE), row(D_MODEL), const((1, D_MODEL)),
                  const((D_MODEL, D_MODEL)), const((D_MODEL, D_MODEL)), const((N_CHIPS, D_PLE, W_PE_COLS))],
        out_specs=[row(D_MODEL), row(D_MODEL), row(D_MODEL), row(D_MODEL),
                   pl.BlockSpec((SUBLANES, LANES), lambda i: (0, 0))],
        out_shape=[jax.ShapeDtypeStruct((t, D_MODEL), F32), jax.ShapeDtypeStruct((t, D_MODEL), BF16),
                   jax.ShapeDtypeStruct((t, D_MODEL), F32), jax.ShapeDtypeStruct((t, D_MODEL), F32),
                   jax.ShapeDtypeStruct((SUBLANES, LANES), F32)],
        compiler_params=_params("arbitrary"),
    )(x, y, p, tgt, post_g, w_out, w_pg, wg_pe)


def _head_bwd(dout, gt, p, o, post_g, w_out, w_pg, wg_pe, tm):
    t = dout.shape[0]

    def body(dout_ref, gt_ref, p_ref, o_ref, pg_ref, wo_ref, wpg_ref, wpe_ref,
             gwpe_ref, dq_ref, dh1_ref, do_ref, dy_ref, gpost_ref):
        i = pl.program_id(0)

        @pl.when(i == 0)
        def _():
            gpost_ref[...] = jnp.zeros_like(gpost_ref)
            gwpe_ref[...] = jnp.zeros_like(gwpe_ref)

        dout = dout_ref[...]
        gt = gt_ref[...]
        pb = p_ref[...].astype(BF16)
        for k in range(N_CHIPS):
            cols = slice(k * W_PE_COLS, (k + 1) * W_PE_COLS)
            pe = _dot(pb, wpe_ref[k])
            g = gt[:, cols]
            dg = dout[:, cols] * g
            gwpe_ref[k] += _dot_tn(pb, dg.astype(BF16))
            dq_ref[:, cols] = (dg * pe * (1.0 - g)).astype(BF16)
        dh1 = dout + _dot_nt(dq_ref[...], wpg_ref[...])
        dh1_ref[...] = dh1
        o = o_ref[...]
        r3 = lax.rsqrt(_lanemean(o * o) + EPS)
        on = o * r3
        gpost_ref[...] += _rowsum8(dh1 * on)
        don = dh1 * pg_ref[...]
        do = r3 * (don - on * _lanemean(don * on))
        dob = do.astype(BF16)
        do_ref[...] = dob
        dy_ref[...] = _dot_nt(dob, wo_ref[...])

        @pl.when(i == pl.num_programs(0) - 1)
        def _():
            gpost_ref[...] = jnp.broadcast_to(jnp.sum(gpost_ref[...], axis=0, keepdims=True), gpost_ref.shape)

    row = lambda n: pl.BlockSpec((tm, n), lambda i: (i, 0))
    const = lambda shp: pl.BlockSpec(shp, lambda i, n=len(shp): (0,) * n, pipeline_mode=pl.Buffered(1))
    return pl.pallas_call(
        body, name="head_bwd", grid=(t // tm,),
        in_specs=[row(D_MODEL), row(D_MODEL), row(D_PLE), row(D_MODEL), const((1, D_MODEL)),
                  const((D_MODEL, D_MODEL)), const((D_MODEL, D_MODEL)), const((N_CHIPS, D_PLE, W_PE_COLS))],
        out_specs=[pl.BlockSpec((N_CHIPS, D_PLE, W_PE_COLS), lambda i: (0, 0, 0)),
                   row(D_MODEL), row(D_MODEL), row(D_MODEL), row(D_MODEL),
                   pl.BlockSpec((SUBLANES, D_MODEL), lambda i: (0, 0))],
        out_shape=[jax.ShapeDtypeStruct((N_CHIPS, D_PLE, W_PE_COLS), F32), jax.ShapeDtypeStruct((t, D_MODEL), BF16),
                   jax.ShapeDtypeStruct((t, D_MODEL), F32), jax.ShapeDtypeStruct((t, D_MODEL), BF16),
                   jax.ShapeDtypeStruct((t, D_MODEL), F32), jax.ShapeDtypeStruct((SUBLANES, D_MODEL), F32)],
        compiler_params=_params("arbitrary"),
    )(dout, gt, p, o, post_g, w_out, w_pg, wg_pe)


def _branches_bwd(z, h, dy, prm, tm, token):
    t = z.shape[0]
    nt = t // tm
    hb = tm // SUBLANES

    def body(u_ref, v_ref, ga_ref, xb_ref, gb_ref, xbh_ref, h_ref, hh_ref, dy_ref,
             lng_ref, lnb_ref, wt_ref, wtt_ref, bsx_ref, cw_ref, cb_ref, wa_ref, wx_ref, ba_ref, bx_ref, lam_ref,
             oga_ref, ogb_ref, token_ref,
             dz_ref, g_oga, g_ogb, g_lng, g_lnb, g_bsx, g_ws, g_cw, g_cb, g_wa, g_ba, g_wx, g_bx, g_lam,
             vn_s, mixed_s, dm_s, dvn_s, xcbf_s, r_s, i_s, a_s, b_s, dh_s, dpr_s, dpi_s, dxc_s,
             ca_s, cd_s, cx_s):
        step_i = pl.program_id(0)
        tile = nt - 1 - step_i
        accs = (g_oga, g_ogb, g_lng, g_lnb, g_bsx, g_ws, g_cw, g_cb, g_wa, g_ba, g_wx, g_bx, g_lam)

        @pl.when(step_i == 0)
        def _():
            for r in accs + (ca_s, cd_s, cx_s):
                r[...] = jnp.zeros_like(r)

        dy_a = dy_ref[:, 0:D_HALF]
        dy_b = dy_ref[:, D_HALF:]

        u = u_ref[...].astype(F32)
        ug, tu = _gelu(u)
        v = v_ref[...].astype(F32)
        vg, tv = _gelu(v)
        vhat, rstd = _layernorm_parts(vg)
        vn_s[...] = (vhat * lng_ref[...] + lnb_ref[...]).astype(BF16)
        _spatial_mix(wt_ref, vn_s, bsx_ref, mixed_s, tm)
        mixed = mixed_s[...]
        ga = ga_ref[...].astype(F32)
        sga = jax.nn.sigmoid(ga)
        sa = ga * sga
        um = ug * mixed
        ya = um * sa
        ra = lax.rsqrt(_lanemean(ya * ya) + EPS)
        yahat = ya * ra
        g_oga[...] += _rowsum8(dy_a * yahat)
        dn = dy_a * oga_ref[...]
        dya = ra * (dn - yahat * _lanemean(dn * yahat))
        dz_ref[:, 2 * D_HALF:3 * D_HALF] = (dya * um * (sga * (1.0 + ga * (1.0 - sga)))).astype(BF16)
        dz_ref[:, 0:D_HALF] = (dya * mixed * sa * _gelu_grad(u, tu)).astype(BF16)
        dmixed = dya * ug * sa
        g_bsx[...] += jnp.sum(dmixed.reshape(tm // CHUNK, CHUNK, D_HALF), axis=0)
        dm_s[...] = dmixed.astype(BF16)
        for c in range(tm // CHUNK):
            rows = slice(c * CHUNK, (c + 1) * CHUNK)
            for hd in range(N_HEADS):
                cols = slice(hd * CHUNK, (hd + 1) * CHUNK)
                dmh = dm_s[rows, cols]
                dvn_s[rows, cols] = _dot(wtt_ref[hd], dmh)
                g_ws[hd] += _dot_nt(dmh, vn_s[rows, cols])
        dvn = dvn_s[...]
        g_lng[...] += _rowsum8(dvn * vhat)
        g_lnb[...] += _rowsum8(dvn)
        dvh = dvn * lng_ref[...]
        dvg = rstd * (dvh - _lanemean(dvh) - vhat * _lanemean(dvh * vhat))
        dz_ref[:, D_HALF:2 * D_HALF] = (dvg * _gelu_grad(v, tv)).astype(BF16)

        xb = xb_ref[...].astype(F32)
        halo = jnp.where(tile == 0, 0.0, xbh_ref[...].astype(F32)[SUBLANES:])
        taps = _conv_taps(xb, halo)
        xc = cb_ref[...] + taps[0] * cw_ref[0:1, :]
        for k in range(1, CONV_W):
            xc = xc + taps[k] * cw_ref[k:k + 1, :]
        xcbf_s[...] = xc.astype(BF16)
        _lru_gates(xcbf_s, wa_ref, wx_ref, ba_ref, bx_ref, r_s, i_s)
        rg = r_s[...]
        ig = i_s[...]
        lam = lam_ref[...]
        a, mult_true = _decay_parts(rg, lam)
        row = lax.broadcasted_iota(jnp.int32, a.shape, 0)
        first = jnp.logical_and(tile == 0, row == 0)
        mult = jnp.where(first, 1.0, mult_true)
        hcur = h_ref[...]
        hprev = _shift_down(hcur, jnp.where(tile == 0, 0.0, hh_ref[...]), 1)
        gb = gb_ref[...].astype(F32)
        sgb = jax.nn.sigmoid(gb)
        sb = gb * sgb
        yb = hcur * sb
        rb = lax.rsqrt(_lanemean(yb * yb) + EPS)
        ybhat = yb * rb
        g_ogb[...] += _rowsum8(dy_b * ybhat)
        dn = dy_b * ogb_ref[...]
        dyb = rb * (dn - ybhat * _lanemean(dn * ybhat))
        dz_ref[:, 4 * D_HALF:5 * D_HALF] = (dyb * hcur * (sgb * (1.0 + gb * (1.0 - sgb)))).astype(BF16)

        an = _shift_up(a, ca_s[...], 1)
        bb = dyb * sb
        r8 = row & (SUBLANES - 1)
        for d in (1, 2, 4):
            a_sh = pltpu.roll(an, tm - d, 0)
            b_sh = pltpu.roll(bb, tm - d, 0)
            m = r8 + d < SUBLANES
            bb = jnp.where(m, an * b_sh + bb, bb)
            an = jnp.where(m, an * a_sh, an)
        a_s[...] = an
        b_s[...] = bb

        def step(g, carry):
            sl = pl.ds(pl.multiple_of((hb - 1 - g) * SUBLANES, SUBLANES), SUBLANES)
            dg = a_s[sl, :] * carry + b_s[sl, :]
            dh_s[sl, :] = dg
            return jnp.broadcast_to(dg[0:1, :], dg.shape)

        cd_s[...] = lax.fori_loop(0, hb, step, cd_s[...])
        ca_s[...] = jnp.broadcast_to(a[0:1, :], ca_s.shape)
        dh = dh_s[...]
        da = dh * hprev
        gx = ig * xc
        dla = da * a - jnp.where(first, 0.0, dh * gx * (a * a / mult_true))
        g_lam[...] += _rowsum8(dla * rg)
        dr = dla * (-LRU_C * _softplus_neg(lam))
        dpr = dr * rg * (1.0 - rg)
        dpi = (dh * mult * xc) * ig * (1.0 - ig)
        g_ba[...] += _rowsum8(dpr)
        g_bx[...] += _rowsum8(dpi)
        dpr_s[...] = dpr.astype(BF16)
        dpi_s[...] = dpi.astype(BF16)
        for hd in range(N_HEADS):
            cols = slice(hd * CHUNK, (hd + 1) * CHUNK)
            xh = xcbf_s[:, cols]
            dprh = dpr_s[:, cols]
            dpih = dpi_s[:, cols]
            g_wa[hd] += _dot_tn(xh, dprh)
            g_wx[hd] += _dot_tn(xh, dpih)
            dxc_s[:, cols] = _dot_nt(dprh, wa_ref[hd]) + _dot_nt(dpih, wx_ref[hd])
        dxc = dxc_s[...] + dh * mult * ig
        g_cb[...] += _rowsum8(dxc)
        for k in range(CONV_W):
            g_cw[k * SUBLANES:(k + 1) * SUBLANES, :] += _rowsum8(dxc * taps[k])
        nxt = cx_s[...]
        dxb = dxc * cw_ref[CONV_W - 1:CONV_W, :]
        for j in range(1, CONV_W):
            dxb = dxb + _shift_up(dxc, nxt, j) * cw_ref[CONV_W - 1 - j:CONV_W - j, :]
        dz_ref[:, 3 * D_HALF:4 * D_HALF] = dxb.astype(BF16)
        cx_s[...] = dxc[0:SUBLANES]

        @pl.when(step_i == nt - 1)
        def _():
            for r in (g_oga, g_ogb, g_lng, g_lnb, g_cb, g_ba, g_bx):
                r[...] = jnp.broadcast_to(jnp.sum(r[...], axis=0, keepdims=True), r.shape)
            lam_f = LRU_C * jax.nn.sigmoid(-lam_ref[...])
            g_lam[...] = jnp.broadcast_to(jnp.sum(g_lam[...], axis=0, keepdims=True) * lam_f, g_lam.shape)
            for k in range(CONV_W):
                blk = g_cw[k * SUBLANES:(k + 1) * SUBLANES, :]
                g_cw[k * SUBLANES:(k + 1) * SUBLANES, :] = jnp.broadcast_to(jnp.sum(blk, axis=0, keepdims=True), blk.shape)
            tri = (lax.broadcasted_iota(jnp.int32, (CHUNK, CHUNK), 0) >= lax.broadcasted_iota(jnp.int32, (CHUNK, CHUNK), 1))
            for hd in range(N_HEADS):
                cols = slice(hd * CHUNK, (hd + 1) * CHUNK)
                g_ws[hd] = jnp.where(tri, g_ws[hd], 0.0)
                blk = g_bsx[:, cols]
                g_bsx[:, cols] = jnp.broadcast_to(jnp.sum(blk, axis=1, keepdims=True), blk.shape)

    rev = lambda i: nt - 1 - i
    zspec = lambda g: pl.BlockSpec((tm, D_HALF), lambda i, g=g: (rev(i), g))
    halo = lambda col: pl.BlockSpec((SUBLANES, D_HALF), lambda i: (jnp.maximum(rev(i) * hb - 1, 0), col))
    zhalo = pl.BlockSpec((2 * SUBLANES, D_HALF), lambda i: (jnp.maximum(rev(i) * (hb // 2) - 1, 0), 3))
    full = lambda a: pl.BlockSpec(a.shape, lambda i, n=a.ndim: (0,) * n)
    acc = lambda shp: pl.BlockSpec(shp, lambda i, n=len(shp): (0,) * n)
    names = ("ln_g", "ln_b", "wt", "wtt", "bsx", "conv_w", "conv_b", "w_a", "w_x", "b_a", "b_x", "lam", "oga", "ogb")
    pr = [prm[n] for n in names] + [token]
    vec = (SUBLANES, D_HALF)
    mat = (N_HEADS, CHUNK, CHUNK)
    acc_shapes = [vec, vec, vec, vec, (CHUNK, D_HALF), mat, (CONV_W * SUBLANES, D_HALF), vec, mat, vec, mat, vec, vec]
    big = lambda dt: pltpu.VMEM((tm, D_HALF), dt)
    return pl.pallas_call(
        body, name="branches_bwd", grid=(nt,),
        in_specs=[zspec(0), zspec(1), zspec(2), zspec(3), zspec(4), zhalo,
                  pl.BlockSpec((tm, D_HALF), lambda i: (rev(i), 0)), halo(0),
                  pl.BlockSpec((tm, D_MODEL), lambda i: (rev(i), 0))] + [full(a) for a in pr],
        out_specs=[pl.BlockSpec((tm, D_Z), lambda i: (rev(i), 0))] + [acc(s) for s in acc_shapes],
        out_shape=[jax.ShapeDtypeStruct((t, D_Z), BF16)] + [jax.ShapeDtypeStruct(s, F32) for s in acc_shapes],
        scratch_shapes=[big(BF16), big(F32), big(BF16), big(F32), big(BF16), big(F32), big(F32), big(F32), big(F32),
                        big(F32), big(BF16), big(BF16), big(F32),
                        pltpu.VMEM(vec, F32), pltpu.VMEM(vec, F32), pltpu.VMEM(vec, F32)],
        compiler_params=_params("arbitrary"),
    )(z, z, z, z, z, z, h, h, dy, *pr)


def _inproj_bwd(dz, wg_in, x, dh1, pre_g, tm, tile0, nt, prev, last, token, name):
    t = x.shape[0]

    def body(*refs):
        dz_ref, w_ref, x_ref, dh1_ref, g_ref = refs[:5]
        gx_ref, gpre_ref, acc_s = refs[-3:]
        i = pl.program_id(0)

        @pl.when(i == 0)
        def _():
            gpre_ref[...] = jnp.zeros_like(gpre_ref) if prev is None else refs[7][...]

        acc = _dot_nt(dz_ref[:, 0:W_IN_COLS], w_ref[0])
        for k in range(1, N_CHIPS):
            acc = acc + _dot_nt(dz_ref[:, k * W_IN_COLS:(k + 1) * W_IN_COLS], w_ref[k])
        acc_s[...] = acc
        for s in range(tm // CHUNK):
            rows = slice(s * CHUNK, (s + 1) * CHUNK)
            xv = x_ref[rows, :]
            r = lax.rsqrt(_lanemean(xv * xv) + EPS)
            xhat = xv * r
            dhn = acc_s[rows, :]
            gpre_ref[...] += _rowsum8(dhn * xhat)
            dxh = dhn * g_ref[...]
            gx_ref[rows, :] = dh1_ref[rows, :] + r * (dxh - xhat * _lanemean(dxh * xhat))

        if last:
            @pl.when(i == nt - 1)
            def _():
                gpre_ref[...] = jnp.broadcast_to(jnp.sum(gpre_ref[...], axis=0, keepdims=True), gpre_ref.shape)

    row = lambda n: pl.BlockSpec((tm, n), lambda i: (tile0 + i, 0))
    small = lambda r: pl.BlockSpec((r, D_MODEL), lambda i: (0, 0))
    tok = pl.BlockSpec((SUBLANES, LANES), lambda i: (0, 0))
    in_specs = [row(D_Z), pl.BlockSpec(wg_in.shape, lambda i: (0, 0, 0), pipeline_mode=pl.Buffered(1)),
                row(D_MODEL), row(D_MODEL), small(1), tok]
    args = [dz, wg_in, x, dh1, pre_g, token]
    aliases = {}
    if prev is not None:
        in_specs += [ANY, small(SUBLANES)]
        args += list(prev)
        aliases = {6: 0}
    return pl.pallas_call(
        body, name=name, grid=(nt,), in_specs=in_specs, out_specs=[row(D_MODEL), small(SUBLANES)],
        out_shape=[jax.ShapeDtypeStruct((t, D_MODEL), F32), jax.ShapeDtypeStruct((SUBLANES, D_MODEL), F32)],
        input_output_aliases=aliases,
        scratch_shapes=[pltpu.VMEM((tm, D_MODEL), F32)],
        compiler_params=_params("arbitrary"),
    )(*args)


def _weight_grad(a, b, name, kb, nb, tk, tn, tt, token):
    t = a.shape[0]
    tt = min(tt, t)

    def body(a_ref, b_ref, token_ref, o_ref):
        @pl.when(pl.program_id(2) == 0)
        def _():
            o_ref[...] = jnp.zeros_like(o_ref)

        o_ref[...] += _dot_tn(a_ref[...], b_ref[...])

    return pl.pallas_call(
        body, name=name, grid=(nb, kb, t // tt),
        in_specs=[pl.BlockSpec((tt, tk), lambda j, i, s: (s, i)), pl.BlockSpec((tt, tn), lambda j, i, s: (s, j)),
                  pl.BlockSpec((SUBLANES, LANES), lambda j, i, s: (0, 0))],
        out_specs=pl.BlockSpec((None, None, tk, tn), lambda j, i, s: (j, i, 0, 0)),
        out_shape=jax.ShapeDtypeStruct((nb, kb, tk, tn), F32),
        compiler_params=_params("parallel", "parallel", "arbitrary"),
    )(a, b, token)


def _place():
    x, y, c = lax.axis_index("x"), lax.axis_index("y"), lax.axis_index("c")
    return x, y, c


def _chip_of(x, y):
    return 2 * x + y


def _gather_weights(w_in, conv_w):
    halves = [(D_MODEL // 2, W_IN_COLS)]
    nw = len(halves)

    def body(win_ref, cw_ref, gin_ref, gcw_ref, s0, b0, lsem, send_sems, recv_sems, cw_send, cw_recv):
        x, y, c = _place()
        me = _chip_of(x, y)
        sibling = (x, y, 1 - c)
        chips = [(1 - x, y), (x, 1 - y), (1 - x, 1 - y)]
        srcs = (win_ref,)
        stage = (s0,)
        bf = (b0,)
        outs = (gin_ref,)
        loads = []
        for n in range(nw):
            rows = halves[n][0]
            cp = pltpu.make_async_copy(srcs[n].at[pl.ds(c * rows, rows), :], stage[n], lsem.at[n])
            cp.start()
            loads.append(cp)
        own_cw = pltpu.make_async_copy(cw_ref, gcw_ref.at[me], lsem.at[2 * nw])
        own_cw.start()
        for n in range(nw):
            loads[n].wait()
            bf[n][...] = stage[n][...].astype(BF16)

        def copy(n, k, chip, to, src=None):
            dst = outs[n].at[chip, c]
            return pltpu.make_async_remote_copy(
                src_ref=dst if src is None else src, dst_ref=dst,
                send_sem=send_sems.at[n, k], recv_sem=recv_sems.at[n, k], device_id=to, device_id_type=MESH)

        def recv(n, k, chip, core):
            dst = outs[n].at[chip, core]
            return pltpu.make_async_remote_copy(
                src_ref=dst, dst_ref=dst, send_sem=send_sems.at[n, k], recv_sem=recv_sems.at[n, k],
                device_id=sibling, device_id_type=MESH)

        sends = []
        locals_ = []
        for n in range(nw):
            lc = pltpu.make_async_copy(bf[n], outs[n].at[me, c], lsem.at[nw + n])
            lc.start()
            locals_.append(lc)
            first = [copy(n, 0, me, sibling, src=bf[n])]
            first += [copy(n, 1 + j, me, (*chip, c), src=bf[n]) for j, chip in enumerate(chips)]
            for cp in first:
                cp.start()
            sends += first
        cws = []
        for j, chip in enumerate(chips):
            cp = pltpu.make_async_remote_copy(
                src_ref=cw_ref, dst_ref=gcw_ref.at[me], send_sem=cw_send.at[j], recv_sem=cw_recv.at[j],
                device_id=(*chip, c), device_id_type=MESH)
            cp.start()
            cws.append(cp)
        for n in range(nw):
            for j, chip in enumerate(chips):
                kj = _chip_of(*chip)
                recv(n, 1 + j, kj, c).wait_recv()
                fw = copy(n, 4 + j, kj, sibling)
                fw.start()
                sends.append(fw)
        for n in range(nw):
            recv(n, 0, me, 1 - c).wait_recv()
            for j, chip in enumerate(chips):
                recv(n, 4 + j, _chip_of(*chip), 1 - c).wait_recv()
        for j, chip in enumerate(chips):
            pltpu.make_async_remote_copy(
                src_ref=cw_ref, dst_ref=gcw_ref.at[_chip_of(*chip)], send_sem=cw_send.at[j], recv_sem=cw_recv.at[j],
                device_id=(*chip, c), device_id_type=MESH).wait_recv()
        for cp in sends + cws:
            cp.wait_send()
        for lc in locals_:
            lc.wait()
        own_cw.wait()

    out_shape = [jax.ShapeDtypeStruct((N_CHIPS, 2) + hs, BF16) for hs in halves]
    out_shape.append(jax.ShapeDtypeStruct((N_CHIPS, CONV_W, CONV_COLS), F32))
    scratch = [pltpu.VMEM(hs, F32) for hs in halves] + [pltpu.VMEM(hs, BF16) for hs in halves]
    scratch += [pltpu.SemaphoreType.DMA((2 * nw + 1,)), pltpu.SemaphoreType.DMA((nw, 7)),
                pltpu.SemaphoreType.DMA((nw, 7)), pltpu.SemaphoreType.DMA((3,)), pltpu.SemaphoreType.DMA((3,))]
    return pl.pallas_call(
        body, name="gather_w_in", in_specs=[ANY] * 2, out_specs=[ANY] * 2, out_shape=out_shape,
        scratch_shapes=scratch, compiler_params=pltpu.CompilerParams(vmem_limit_bytes=VMEM_LIMIT),
    )(w_in, conv_w)


HBM = pl.BlockSpec(memory_space=pltpu.HBM)
SEM = pl.BlockSpec(memory_space=pltpu.SEMAPHORE)
EFFECT = pltpu.SideEffectType.DATAFLOW_SIDE_EFFECTING


def _hbm(a):
    return pltpu.with_memory_space_constraint(a, pltpu.HBM)


def _landing(shape, dtype):
    return _hbm(lax.empty(shape, dtype))


def _exchange_start(name, arrays, ncopies, build, after=None):
    n = len(arrays)
    extra = [] if after is None else [after]

    def body(*refs):
        ins, token = refs[:n], refs[-1]
        send_sems, recv_sems = refs[n + len(extra)], refs[n + len(extra) + 1]
        for cp in build(ins, send_sems, recv_sems):
            cp.start()
        token[...] = jnp.zeros_like(token)

    outs = pl.pallas_call(
        body, name=name,
        out_shape=(pltpu.SemaphoreType.DMA((ncopies,)), pltpu.SemaphoreType.DMA((ncopies,)),
                   *[pltpu.HBM(a.shape, a.dtype) for a in arrays], jax.ShapeDtypeStruct((SUBLANES, LANES), F32)),
        in_specs=[HBM] * n + [ANY] * len(extra),
        out_specs=(SEM, SEM, *[HBM] * n, pl.BlockSpec(memory_space=pltpu.VMEM)),
        input_output_aliases={q: q + 2 for q in range(n)},
        compiler_params=pltpu.CompilerParams(has_side_effects=EFFECT),
    )(*[_hbm(a) for a in arrays], *extra)
    return (outs[0], outs[1], list(outs[2:2 + n])), outs[-1]


def _exchange_wait(name, started, after, build):
    send, recv, arrays = started
    n = len(arrays)

    def body(*refs):
        ins, send_sems, recv_sems = refs[:n], refs[n], refs[n + 1]
        for cp in build(ins, send_sems, recv_sems):
            cp.wait_send()
            cp.wait_recv()

    return pl.pallas_call(
        body, name=name, out_shape=tuple(pltpu.HBM(a.shape, a.dtype) for a in arrays),
        in_specs=[HBM] * n + [SEM, SEM, ANY], out_specs=tuple([HBM] * n),
        input_output_aliases={q: q for q in range(n)},
        compiler_params=pltpu.CompilerParams(has_side_effects=EFFECT),
    )(*arrays, send, recv, after)


def _exchange_wait_start(name, started, after, build_wait, ncopies, build_start):
    send, recv, arrays = started
    n = len(arrays)

    def body(*refs):
        ins, send_sems, recv_sems = refs[:n], refs[n], refs[n + 1]
        send2, recv2, token = refs[n + 3], refs[n + 4], refs[-1]
        for cp in build_wait(ins, send_sems, recv_sems):
            cp.wait_send()
            cp.wait_recv()
        for cp in build_start(ins, send2, recv2):
            cp.start()
        token[...] = jnp.zeros_like(token)

    outs = pl.pallas_call(
        body, name=name,
        out_shape=(pltpu.SemaphoreType.DMA((ncopies,)), pltpu.SemaphoreType.DMA((ncopies,)),
                   *[pltpu.HBM(a.shape, a.dtype) for a in arrays], jax.ShapeDtypeStruct((SUBLANES, LANES), F32)),
        in_specs=[HBM] * n + [SEM, SEM, ANY], out_specs=(SEM, SEM, *[HBM] * n, pl.BlockSpec(memory_space=pltpu.VMEM)),
        input_output_aliases={q: q + 2 for q in range(n)},
        compiler_params=pltpu.CompilerParams(has_side_effects=EFFECT),
    )(*arrays, send, recv, after)
    return (outs[0], outs[1], list(outs[2:2 + n])), outs[-1]


def _cast_into_slot(w, kc, name):
    rows, cols = w.shape
    tr = min(rows, 256)

    def body(kc_ref, w_ref, o_ref):
        o_ref[...] = w_ref[...].astype(BF16)

    grid_spec = pltpu.PrefetchScalarGridSpec(
        num_scalar_prefetch=1, grid=(rows // tr,),
        in_specs=[pl.BlockSpec((tr, cols), lambda r, kc: (r, 0))],
        out_specs=pl.BlockSpec((None, tr, cols), lambda r, kc: (kc[0], r, 0)))
    return pl.pallas_call(
        body, name=name, grid_spec=grid_spec, out_shape=jax.ShapeDtypeStruct((N_CHIPS, rows, cols), BF16),
        compiler_params=_params("arbitrary"),
    )(kc, w)


def _gather_ici_copies(n):
    def build(refs, send_sems, recv_sems):
        x, y, c = _place()
        mine = lambda b: refs[b].at[_chip_of(x, y), c]
        chips = [(1 - x, y), (x, 1 - y), (1 - x, 1 - y)]
        return [pltpu.make_async_remote_copy(
            src_ref=mine(b), dst_ref=mine(b), send_sem=send_sems.at[3 * b + j], recv_sem=recv_sems.at[3 * b + j],
            device_id=(*chip, c), device_id_type=MESH) for b in range(n) for j, chip in enumerate(chips)]
    return build


def _gather_relay_copies(n):
    def build(refs, send_sems, recv_sems):
        x, y, c = _place()
        chips = [(1 - x, y), (x, 1 - y), (1 - x, 1 - y)]
        cps = []
        for b in range(n):
            for j, chip in enumerate(chips):
                got = refs[b].at[_chip_of(*chip), c]
                cps.append(pltpu.make_async_remote_copy(
                    src_ref=got, dst_ref=got, send_sem=send_sems.at[3 * b + j], recv_sem=recv_sems.at[3 * b + j],
                    device_id=(x, y, 1 - c), device_id_type=MESH))
        return cps
    return build


def _sibling_copies(n):
    def build(refs, send_sems, recv_sems):
        x, y, c = _place()
        return [pltpu.make_async_remote_copy(
            src_ref=refs[b].at[:, 1 - c], dst_ref=refs[n + b], send_sem=send_sems.at[b], recv_sem=recv_sems.at[b],
            device_id=(x, y, 1 - c), device_id_type=MESH) for b in range(n)]
    return build


def _chip_copies(n):
    def build(refs, send_sems, recv_sems):
        x, y, c = _place()
        chips = [(1 - x, y), (x, 1 - y), (1 - x, 1 - y)]
        return [pltpu.make_async_remote_copy(
            src_ref=refs[b].at[_chip_of(*chip)], dst_ref=refs[n + b].at[j],
            send_sem=send_sems.at[3 * b + j], recv_sem=recv_sems.at[3 * b + j],
            device_id=(*chip, c), device_id_type=MESH) for b in range(n) for j, chip in enumerate(chips)]
    return build


def _finish_copies(n, n_all):
    def build(refs, send_sems, recv_sems):
        x, y, c = _place()
        cps = [pltpu.make_async_remote_copy(
            src_ref=refs[b].at[c], dst_ref=refs[b].at[c], send_sem=send_sems.at[b], recv_sem=recv_sems.at[b],
            device_id=(x, y, 1 - c), device_id_type=MESH) for b in range(n)]
        flips = [(fx, fy, fc) for fx in (0, 1) for fy in (0, 1) for fc in (0, 1)][1:]
        for b in range(n_all):
            mine = refs[n + b].at[_chip_of(x, y), c]
            cps += [pltpu.make_async_remote_copy(
                src_ref=mine, dst_ref=mine, send_sem=send_sems.at[n + 7 * b + q], recv_sem=recv_sems.at[n + 7 * b + q],
                device_id=(x ^ fx, y ^ fy, c ^ fc), device_id_type=MESH) for q, (fx, fy, fc) in enumerate(flips)]
        return cps
    return build


def _pair_sum(g, r1, kc, name, tr, send_dtype):
    nk, _, rows, cols = g.shape

    def body(kc_ref, g_ref, r_ref, p_ref, own_ref):
        s = g_ref[...] + r_ref[...]
        p_ref[...] = s.astype(send_dtype)

        @pl.when(pl.program_id(1) == kc_ref[0])
        def _():
            own_ref[...] = s

    grid_spec = pltpu.PrefetchScalarGridSpec(
        num_scalar_prefetch=1, grid=(rows // tr, nk),
        in_specs=[pl.BlockSpec((None, None, tr, cols), lambda r, k, kc: (k, kc[1], r, 0)),
                  pl.BlockSpec((None, tr, cols), lambda r, k, kc: (k, r, 0))],
        out_specs=[pl.BlockSpec((None, tr, cols), lambda r, k, kc: (k, r, 0)),
                   pl.BlockSpec((tr, cols), lambda r, k, kc: (r, 0))])
    return pl.pallas_call(
        body, name=name, grid_spec=grid_spec,
        out_shape=[jax.ShapeDtypeStruct((nk, rows, cols), send_dtype), jax.ShapeDtypeStruct((rows, cols), F32)],
        compiler_params=_params("arbitrary", "arbitrary"),
    )(kc, g, r1)


def _chip_sum(own, r2, slot, lead, name, tr):
    rows, cols = own.shape
    nl = len(lead)

    def body(slot_ref, o_ref, r_ref, s_ref):
        s = o_ref[...]
        for j in range(3):
            s = s + r_ref[j].astype(F32)
        s_ref[...] = s

    grid_spec = pltpu.PrefetchScalarGridSpec(
        num_scalar_prefetch=1, grid=(rows // tr,),
        in_specs=[pl.BlockSpec((tr, cols), lambda r, sl: (r, 0)), pl.BlockSpec((3, tr, cols), lambda r, sl: (0, r, 0))],
        out_specs=pl.BlockSpec((None,) * nl + (tr, cols), lambda r, sl: tuple(sl[q] for q in range(nl)) + (r, 0)))
    return pl.pallas_call(
        body, name=name, grid_spec=grid_spec, out_shape=jax.ShapeDtypeStruct(tuple(lead) + (rows, cols), F32),
        compiler_params=_params("arbitrary"),
    )(slot, own, r2)


def _adam_update(w, g, m, v):
    nm = ADAM_B1 * m + (1.0 - ADAM_B1) * g
    nv = ADAM_B2 * v + (1.0 - ADAM_B2) * (g * g)
    m_hat = nm / (1.0 - ADAM_B1 ** ADAM_STEP)
    v_hat = nv / (1.0 - ADAM_B2 ** ADAM_STEP)
    return -ADAM_LR * (m_hat / (jnp.sqrt(v_hat) + ADAM_EPS) + ADAM_WD * w), nm, nv


def _adamw(w, g, m, v, name, tr, token):
    rows, cols = w.shape

    def body(w_ref, g_ref, m_ref, v_ref, token_ref, go_ref, d_ref, nm_ref, nv_ref):
        gv = g_ref[...]
        go_ref[...] = gv
        d_ref[...], nm_ref[...], nv_ref[...] = _adam_update(w_ref[...], gv, m_ref[...], v_ref[...])

    spec = pl.BlockSpec((tr, cols), lambda r: (r, 0))
    return pl.pallas_call(
        body, name=name, grid=(rows // tr,),
        in_specs=[spec] * 4 + [pl.BlockSpec((SUBLANES, LANES), lambda r: (0, 0))], out_specs=[spec] * 4,
        out_shape=[jax.ShapeDtypeStruct((rows, cols), F32)] * 4,
        compiler_params=_params("parallel"),
    )(w, g, m, v, token)


def _adamw_small(packed_g, pre_g_parts, ws, ms, vs):
    names = ["pre_g"] + [n for n, _ in SMALL_ROWS if n != "conv_w"]
    rows = dict(SMALL_ROWS)
    offset, at = {}, 0
    for n, r in SMALL_ROWS:
        offset[n] = at
        at += r
    k = len(names)

    def body(*refs):
        g_ref, pg_ref = refs[0], refs[1]
        w_refs, m_refs, v_refs = refs[2:2 + k], refs[2 + k:2 + 2 * k], refs[2 + 2 * k:2 + 3 * k]
        outs = refs[2 + 3 * k:]
        go, do, mo, vo = outs[:k], outs[k:2 * k], outs[2 * k:3 * k], outs[3 * k:]
        pre = pg_ref[0]
        for dev in range(1, 8):
            pre = pre + pg_ref[dev]
        for i, n in enumerate(names):
            shp = w_refs[i].shape
            if len(shp) == 2 and shp[0] == 1:
                for r in range(shp[1] // LANES):
                    cols = slice(r * LANES, (r + 1) * LANES)
                    g = pre[r:r + 1, :] if n == "pre_g" else g_ref[offset[n] + r:offset[n] + r + 1, :]
                    go[i][:, cols] = g
                    do[i][:, cols], mo[i][:, cols], vo[i][:, cols] = _adam_update(
                        w_refs[i][:, cols], g, m_refs[i][:, cols], v_refs[i][:, cols])
            else:
                g = g_ref[offset[n]:offset[n] + rows[n], :].reshape(shp)
                go[i][...] = g
                do[i][...], mo[i][...], vo[i][...] = _adam_update(w_refs[i][...], g, m_refs[i][...], v_refs[i][...])

    vm = pl.BlockSpec(memory_space=pltpu.VMEM)
    args = [packed_g, pre_g_parts] + [src[n] for src in (ws, ms, vs) for n in names]
    out_shape = [jax.ShapeDtypeStruct(ws[n].shape, F32) for _ in range(4) for n in names]
    outs = pl.pallas_call(
        body, name="adamw_small", in_specs=[vm] * len(args), out_specs=[vm] * (4 * k), out_shape=out_shape,
    )(*args)
    return [dict(zip(names, outs[q * k:(q + 1) * k])) for q in range(4)]


def _into_slot(v, slot, lead, name):
    n = v.shape[1]
    nl = len(lead)

    def body(slot_ref, v_ref, o_ref):
        for r in range(n // LANES):
            o_ref[r:r + 1, :] = v_ref[0:1, r * LANES:(r + 1) * LANES]

    grid_spec = pltpu.PrefetchScalarGridSpec(
        num_scalar_prefetch=1, grid=(1,),
        in_specs=[pl.BlockSpec(v.shape, lambda i, sl: (0, 0))],
        out_specs=pl.BlockSpec((None,) * nl + (n // LANES, LANES), lambda i, sl: tuple(sl[q] for q in range(nl)) + (0, 0)))
    return pl.pallas_call(
        body, name=name, grid_spec=grid_spec, out_shape=jax.ShapeDtypeStruct(tuple(lead) + (n // LANES, LANES), F32),
    )(slot, v)


def _rows128(a):
    return a.reshape(-1, LANES)


def _pack_small(parts):
    pieces = [_rows128(parts[n]) for n, _ in SMALL_ROWS]
    pieces.append(jnp.zeros((SMALL_TOTAL - SMALL_USED, LANES), F32))
    return jnp.concatenate(pieces, axis=0)


def kernel(x, p, pre_g, w_in, gmlp_ln_g, gmlp_ln_b, gmlp_ws, gmlp_bs, conv_w, conv_b, w_a, b_a, w_x, b_x, lam, gmlp_out_g, lru_out_g, w_out, post_g, w_pe, w_pg, loss_target, m_pre_g, m_w_in, m_gmlp_ln_g, m_gmlp_ln_b, m_gmlp_ws, m_gmlp_bs, m_conv_w, m_conv_b, m_w_a, m_b_a, m_w_x, m_b_x, m_lam, m_gmlp_out_g, m_lru_out_g, m_w_out, m_post_g, m_w_pe, m_w_pg, v_pre_g, v_w_in, v_gmlp_ln_g, v_gmlp_ln_b, v_gmlp_ws, v_gmlp_bs, v_conv_w, v_conv_b, v_w_a, v_b_a, v_w_x, v_b_x, v_lam, v_gmlp_out_g, v_lru_out_g, v_w_out, v_post_g, v_w_pe, v_w_pg):
    weights = dict(pre_g=pre_g, w_in=w_in, gmlp_ln_g=gmlp_ln_g, gmlp_ln_b=gmlp_ln_b, gmlp_ws=gmlp_ws, gmlp_bs=gmlp_bs,
                   conv_w=conv_w, conv_b=conv_b, w_a=w_a, b_a=b_a, w_x=w_x, b_x=b_x, lam=lam, gmlp_out_g=gmlp_out_g,
                   lru_out_g=lru_out_g, w_out=w_out, post_g=post_g, w_pe=w_pe, w_pg=w_pg)
    mom_m = dict(pre_g=m_pre_g, w_in=m_w_in, gmlp_ln_g=m_gmlp_ln_g, gmlp_ln_b=m_gmlp_ln_b, gmlp_ws=m_gmlp_ws,
                 gmlp_bs=m_gmlp_bs, conv_w=m_conv_w, conv_b=m_conv_b, w_a=m_w_a, b_a=m_b_a, w_x=m_w_x, b_x=m_b_x,
                 lam=m_lam, gmlp_out_g=m_gmlp_out_g, lru_out_g=m_lru_out_g, w_out=m_w_out, post_g=m_post_g,
                 w_pe=m_w_pe, w_pg=m_w_pg)
    mom_v = dict(pre_g=v_pre_g, w_in=v_w_in, gmlp_ln_g=v_gmlp_ln_g, gmlp_ln_b=v_gmlp_ln_b, gmlp_ws=v_gmlp_ws,
                 gmlp_bs=v_gmlp_bs, conv_w=v_conv_w, conv_b=v_conv_b, w_a=v_w_a, b_a=v_b_a, w_x=v_w_x, b_x=v_b_x,
                 lam=v_lam, gmlp_out_g=v_gmlp_out_g, lru_out_g=v_lru_out_g, w_out=v_w_out, post_g=v_post_g,
                 w_pe=v_w_pe, w_pg=v_w_pg)
    order = list(weights)
    xi, yi, ci = _place()
    me = _chip_of(xi, yi)
    kc = jnp.stack([me, ci]).astype(jnp.int32)

    x2 = x[0]
    p2 = p[0, 0]
    tgt = loss_target[0]

    g_in, g_cw = _gather_weights(w_in[0], conv_w[0, :, 0, :])
    wg_in = g_in.reshape(N_CHIPS, D_MODEL, W_IN_COLS)
    cw_full = jnp.transpose(g_cw, (1, 0, 2)).reshape(CONV_W, D_HALF)
    later = [_cast_into_slot(w_out[0], kc, "cast_w_out").reshape(N_CHIPS, 2, W_ROWS // 2, D_MODEL),
             _cast_into_slot(w_pg[0], kc, "cast_w_pg").reshape(N_CHIPS, 2, W_ROWS // 2, D_MODEL),
             _cast_into_slot(w_pe[0], kc, "cast_w_pe").reshape(N_CHIPS, 2, D_PLE // 2, W_PE_COLS)]
    gather_st, gather_tok = _exchange_start("gather_start", later, 9, _gather_ici_copies(3), after=g_cw)

    causal = jnp.tril(jnp.ones((CHUNK, CHUNK), dtype=bool))
    ws_m = jnp.where(causal[None], gmlp_ws[0], 0.0)
    prm = dict(
        ln_g=gmlp_ln_g, ln_b=gmlp_ln_b, wt=ws_m.astype(BF16), wtt=jnp.transpose(ws_m, (0, 2, 1)).astype(BF16),
        bsx=jnp.repeat(jnp.transpose(gmlp_bs[0]), CHUNK, axis=1),
        conv_w=cw_full, conv_b=conv_b, w_a=w_a[0].astype(BF16), w_x=w_x[0].astype(BF16),
        b_a=b_a[0].reshape(1, D_HALF), b_x=b_x[0].reshape(1, D_HALF), lam=lam, oga=gmlp_out_g, ogb=lru_out_g)

    z, hn, y, h = _inproj_branches_fwd(x2, pre_g, wg_in, prm, 256, gather_tok)
    gather_st, gather_tok = _exchange_wait_start("gather_relay", gather_st, y, _gather_ici_copies(3), 9,
                                                 _gather_relay_copies(3))
    g_out, g_pg, g_pe = _exchange_wait("gather_wait", gather_st, gather_tok, _gather_relay_copies(3))
    wg_out = g_out.reshape(D_MODEL, D_MODEL)
    wg_pg = g_pg.reshape(D_MODEL, D_MODEL)
    wg_pe = g_pe.reshape(N_CHIPS, D_PLE, W_PE_COLS)
    o, h1, gt, dout, loss_acc = _outproj_fwd(x2, y, p2, tgt, post_g, wg_out, wg_pg, wg_pe, 256)
    loss = lax.psum(loss_acc[0, 0], ("x", "y", "c"))

    def sibling_start(tag, bufs):
        lands = [_landing((b.shape[0],) + b.shape[2:], b.dtype) for b in bufs]
        return _exchange_start("sibling_start_" + tag, bufs + lands, len(bufs), _sibling_copies(len(bufs)))

    def pair_then_chip_start(tag, started, after, names, tiles, dtypes):
        n = len(names)
        got = _exchange_wait("sibling_wait_" + tag, started, after, _sibling_copies(n))
        pairs = [_pair_sum(got[b], got[n + b], kc, "pair_sum_" + names[b], tiles[b], dtypes[b]) for b in range(n)]
        lands = [_landing((3,) + pr[0].shape[1:], pr[0].dtype) for pr in pairs]
        return _exchange_start("chip_start_" + tag, [pr[0] for pr in pairs] + lands, 3 * n, _chip_copies(n)), pairs

    def sum_then_finish_start(tag, started, pairs, after, names, tiles, small, to_all=()):
        n = len(names)
        got = _exchange_wait("chip_wait_" + tag, started, after, _chip_copies(n))
        sums = [_chip_sum(pairs[b][1], got[n + b], kc if small and b == n - 1 else kc[1:],
                          (N_CHIPS, 2) if small and b == n - 1 else (2,), "chip_sum_" + names[b], tiles[b])
                for b in range(n)]
        nbig = n - 1 if small else n
        n_all = n - nbig + len(to_all)
        return _exchange_start("finish_start_" + tag, sums + list(to_all), nbig + 7 * n_all,
                               _finish_copies(nbig, n_all))

    gw_pe, dq, dh1, do, dy, g_post = _head_bwd(dout, gt, p2, o, post_g, wg_out, wg_pg, wg_pe, 256)
    gw_pe = gw_pe.reshape(N_CHIPS, 2, D_PLE // 2, W_PE_COLS)
    token0 = jnp.zeros((SUBLANES, LANES), F32)
    gw_out = _weight_grad(y, do, "grad_w_out", 2, 1, D_MODEL // 2, D_MODEL, 1024, token0)
    gw_pg = _weight_grad(h1, dq, "grad_w_pg", 2, 1, D_MODEL // 2, D_MODEL, 1024, token0)
    gw_out = gw_out.reshape(N_CHIPS, 2, W_ROWS // 2, D_MODEL)
    gw_pg = gw_pg.reshape(N_CHIPS, 2, W_ROWS // 2, D_MODEL)

    names_a, tiles_a = ["w_out", "w_pg", "w_pe"], [128, 128, 128]
    st, tok = sibling_start("a", [gw_out, gw_pg, gw_pe])
    (dz, g_oga, g_ogb, g_lng, g_lnb, g_bsx, g_ws, g_cw, g_cb, g_wa, g_ba, g_wx, g_bx, g_lam) = _branches_bwd(
        z, h, dy, prm, 256, tok)
    (st, tok), pairs_a = pair_then_chip_start("a", st, dz, names_a, tiles_a, [BF16] * 3)
    gw_in = _weight_grad(hn, dz, "grad_w_in", 2, N_CHIPS, D_MODEL // 2, W_IN_COLS, 1024, tok)
    fin_a, tok = sum_then_finish_start("a", st, pairs_a, gw_in, names_a, tiles_a, False)

    small_g = dict(
        gmlp_ln_g=g_lng[0:1], gmlp_ln_b=g_lnb[0:1], gmlp_ws=g_ws,
        gmlp_bs=jnp.transpose(g_bsx[:, ::CHUNK]), conv_w=g_cw[::SUBLANES], conv_b=g_cb[0:1], w_a=g_wa, b_a=g_ba[0:1],
        w_x=g_wx, b_x=g_bx[0:1], lam=g_lam[0:1], gmlp_out_g=g_oga[0:1], lru_out_g=g_ogb[0:1], post_g=g_post[0:1])
    gsm = _pack_small(small_g).reshape(N_CHIPS, 2, SMALL_PIECE, LANES)

    names_b, tiles_b = ["w_in", "small"], [256, SMALL_PIECE]
    n_tiles = x2.shape[0] // 256
    n_lo = max(1, (5 * n_tiles) // 16)
    st, tok_b = _exchange_start(
        "sibling_start_b", [gw_in, gsm] + [_landing((N_CHIPS,) + b.shape[2:], F32) for b in (gw_in, gsm)], 2,
        _sibling_copies(2), after=tok)
    part = _inproj_bwd(dz, wg_in, x2, dh1, pre_g, 256, 0, n_lo, None, False, tok_b, "inproj_bwd_lo")
    f_out, f_pg, f_pe = _exchange_wait("finish_wait_a", fin_a, part[1], _finish_copies(3, 0))
    (st, tok_b), pairs_b = pair_then_chip_start("b", st, part[1], names_b, tiles_b, [BF16, F32])
    grad_x, g_pre = _inproj_bwd(dz, wg_in, x2, dh1, pre_g, 256, n_lo, n_tiles - n_lo, part, True, tok_b,
                                "inproj_bwd_hi")
    pre_parts = _into_slot(g_pre, kc, (N_CHIPS, 2), "pre_g_into_slot")
    fin_b, tok_b = sum_then_finish_start("b", st, pairs_b, g_pre, names_b, tiles_b, True, to_all=[pre_parts])

    grads, deltas, new_m, new_v = {}, {}, {}, {}

    def adam_big(n, g2d, tr, token):
        shp = weights[n].shape
        g, d, nm, nv = _adamw(weights[n][0], g2d, mom_m[n][0], mom_v[n][0], "adamw_" + n, tr, token)
        grads[n], deltas[n], new_m[n], new_v[n] = g.reshape(shp), d.reshape(shp), nm.reshape(shp), nv.reshape(shp)
        return d

    adam_big("w_out", f_out.reshape(W_ROWS, D_MODEL), 128, tok_b)
    adam_big("w_pg", f_pg.reshape(W_ROWS, D_MODEL), 128, tok_b)
    last = adam_big("w_pe", f_pe.reshape(D_PLE, W_PE_COLS), 128, tok_b)
    f_in, f_sm, pre_parts = _exchange_wait("finish_wait_b", fin_b, last, _finish_copies(1, 2))
    adam_big("w_in", f_in.reshape(D_MODEL, W_IN_COLS), 256, tok_b)

    packed_g = f_sm.reshape(SMALL_TOTAL, LANES)
    small_names = ["pre_g"] + [n for n, _ in SMALL_ROWS if n != "conv_w"]
    natural = lambda src: {n: (src[n] if src[n].ndim == 2 else src[n][0]) for n in small_names}
    outs = _adamw_small(packed_g, pre_parts.reshape(8, D_MODEL // LANES, LANES), natural(weights), natural(mom_m),
                        natural(mom_v))
    for dst, got in zip((grads, deltas, new_m, new_v), outs):
        for n in small_names:
            dst[n] = got[n].reshape(weights[n].shape)
    at = sum(r for n, r in SMALL_ROWS[:[n for n, _ in SMALL_ROWS].index("conv_w")])
    g_cw_all = packed_g[at:at + CONV_W * D_HALF // LANES].reshape(CONV_W, D_HALF)
    g_conv = lax.dynamic_slice_in_dim(g_cw_all, me * CONV_COLS, CONV_COLS, axis=1)
    g, d, nm, nv = _adamw(conv_w[0, :, 0, :], g_conv, m_conv_w[0, :, 0, :], v_conv_w[0, :, 0, :], "adamw_conv_w", CONV_W,
                          tok_b)
    cshape = conv_w.shape
    grads["conv_w"], deltas["conv_w"] = g.reshape(cshape), d.reshape(cshape)
    new_m["conv_w"], new_v["conv_w"] = nm.reshape(cshape), nv.reshape(cshape)

    return (loss, grad_x.reshape(x.shape), *[grads[n] for n in order], *[deltas[n] for n in order],
            *[new_m[n] for n in order], *[new_v[n] for n in order])
```

```python
import functools
import math

import jax
import jax.numpy as jnp
from jax import lax
from jax.experimental import pallas as pl
from jax.experimental.pallas import tpu as pltpu

F32 = jnp.float32
BF16 = jnp.bfloat16

D_MODEL = 2048
D_HALF = 1024
D_Z = 5120
D_PLE = 256
CHUNK = 128
N_HEADS = 8
N_CHIPS = 4
W_IN_COLS = D_Z // N_CHIPS
W_ROWS = D_MODEL // N_CHIPS
W_PE_COLS = D_MODEL // N_CHIPS
CONV_W = 4
CONV_COLS = D_HALF // N_CHIPS
EPS = 1e-6
LRU_C = 8.0
ADAM_LR, ADAM_B1, ADAM_B2, ADAM_EPS, ADAM_WD, ADAM_STEP = 0.001, 0.9, 0.999, 1e-08, 0.01, 10

SUBLANES = 8
LANES = 128
VMEM_LIMIT = 56 * 1024 * 1024

SMALL_ROWS = (("gmlp_ln_g", 8), ("gmlp_ln_b", 8), ("gmlp_ws", 1024), ("gmlp_bs", 8),
              ("conv_w", 32), ("conv_b", 8), ("w_a", 1024), ("b_a", 8), ("w_x", 1024), ("b_x", 8),
              ("lam", 8), ("gmlp_out_g", 8), ("lru_out_g", 8), ("post_g", 16))
SMALL_USED = sum(r for _, r in SMALL_ROWS)
SMALL_PIECE = 400
SMALL_TOTAL = 8 * SMALL_PIECE

MESH = pl.DeviceIdType.MESH
ANY = pl.BlockSpec(memory_space=pl.ANY)

_GELU_C0 = math.sqrt(2.0 / math.pi)
_GELU_C1 = 0.044715


def _params(*sem):
    return pltpu.CompilerParams(dimension_semantics=sem, vmem_limit_bytes=VMEM_LIMIT)


def _dot(a, b):
    return jnp.dot(a, b, preferred_element_type=F32)


def _dot_nt(a, b):
    return lax.dot_general(a, b, (((1,), (1,)), ((), ())), preferred_element_type=F32)


def _dot_tn(a, b):
    return lax.dot_general(a, b, (((0,), (0,)), ((), ())), preferred_element_type=F32)


def _gelu(x):
    t = jnp.tanh(_GELU_C0 * (x + _GELU_C1 * (x * x * x)))
    return 0.5 * x * (1.0 + t), t


def _gelu_grad(x, t):
    return 0.5 * (1.0 + t) + 0.5 * x * (1.0 - t * t) * (_GELU_C0 * (1.0 + 3.0 * _GELU_C1 * x * x))


def _rowsum8(v):
    r, n = v.shape
    return jnp.sum(v.reshape(r // SUBLANES, SUBLANES, n), axis=0)


def _lanemean(v):
    return jnp.mean(v, axis=-1, keepdims=True)


def _shift_down(v, halo8, k):
    if k == 0:
        return v
    r = pltpu.roll(v, k, 0)
    hr = pltpu.roll(halo8, k, 0)
    row = lax.broadcasted_iota(jnp.int32, halo8.shape, 0)
    top = jnp.where(row < k, hr, r[0:SUBLANES])
    return jnp.concatenate([top, r[SUBLANES:]], axis=0)


def _shift_up(v, next8, k):
    if k == 0:
        return v
    n = v.shape[0]
    r = pltpu.roll(v, n - k, 0)
    nr = pltpu.roll(next8, SUBLANES - k, 0)
    row = lax.broadcasted_iota(jnp.int32, next8.shape, 0)
    bot = jnp.where(row >= SUBLANES - k, nr, r[n - SUBLANES:])
    return jnp.concatenate([r[:n - SUBLANES], bot], axis=0)


def _layernorm_parts(vg):
    mu = _lanemean(vg)
    xc = vg - mu
    rstd = lax.rsqrt(_lanemean(xc * xc) + EPS)
    return xc * rstd, rstd


def _spatial_mix(wt_ref, vn_ref, bsx_ref, mixed_ref, tm):
    for c in range(tm // CHUNK):
        rows = slice(c * CHUNK, (c + 1) * CHUNK)
        for h in range(N_HEADS):
            cols = slice(h * CHUNK, (h + 1) * CHUNK)
            mixed_ref[rows, cols] = _dot(wt_ref[h], vn_ref[rows, cols]) + bsx_ref[:, cols]


def _conv_taps(xb, halo8):
    return [_shift_down(xb, halo8, CONV_W - 1 - k) for k in range(CONV_W)]


def _lru_gates(xc_bf_ref, wa_ref, wx_ref, ba_ref, bx_ref, r_ref, i_ref):
    for h in range(N_HEADS):
        cols = slice(h * CHUNK, (h + 1) * CHUNK)
        xh = xc_bf_ref[:, cols]
        r_ref[:, cols] = jax.nn.sigmoid(_dot(xh, wa_ref[h]) + ba_ref[:, cols])
        i_ref[:, cols] = jax.nn.sigmoid(_dot(xh, wx_ref[h]) + bx_ref[:, cols])


def _softplus_neg(lam):
    return jnp.maximum(-lam, 0.0) + jnp.log(1.0 + jnp.exp(-jnp.abs(lam)))


def _decay_parts(r, lam):
    la = (-LRU_C * _softplus_neg(lam)) * r
    a = jnp.exp(la)
    th = -jnp.tanh(la)
    mult = jnp.sqrt(2.0 * th / (1.0 + th))
    return a, mult


def _inproj_branches_fwd(x, pre_g, wg_in, prm, tm, token):
    t = x.shape[0]
    nt = t // tm
    hb = tm // SUBLANES

    def body(x_ref, g_ref, w_ref,
             lng_ref, lnb_ref, wt_ref, bsx_ref, cw_ref, cb_ref, wa_ref, wx_ref, ba_ref, bx_ref, lam_ref,
             oga_ref, ogb_ref, token_ref,
             z_ref, hn_ref, y_ref, h_ref,
             zbuf0, zbuf1, vn_s, mixed_s, xcbf_s, r_s, i_s, ug_s, halo_s, carry_s):
        s = pl.program_id(0)

        @pl.when(s == 0)
        def _():
            zbuf1[...] = jnp.zeros_like(zbuf1)

        @pl.when(s <= 1)
        def _():
            carry_s[...] = jnp.zeros_like(carry_s)
            halo_s[...] = jnp.zeros_like(halo_s)

        xv = x_ref[...]
        hn_ref[...] = (xv * lax.rsqrt(_lanemean(xv * xv) + EPS) * g_ref[...]).astype(BF16)

        def step(zw, zr):
            def project(j):
                cols = slice(j * W_IN_COLS, (j + 1) * W_IN_COLS)
                zb = _dot(hn_ref[...], w_ref[j]).astype(BF16)
                z_ref[:, cols] = zb
                zw[:, cols] = zb

            zin = lambda g: zr[:, g * D_HALF:(g + 1) * D_HALF].astype(F32)
            always = [s >= 0] * 4

            @pl.when(always[0])
            def _():
                project(0)
                ug, _ = _gelu(zin(0))
                ug_s[...] = ug
                vg, _ = _gelu(zin(1))
                vhat, _ = _layernorm_parts(vg)
                vn_s[...] = (vhat * lng_ref[...] + lnb_ref[...]).astype(BF16)

            @pl.when(always[1])
            def _():
                project(1)
                _spatial_mix(wt_ref, vn_s, bsx_ref, mixed_s, tm)
                ga = zin(2)
                ya = ug_s[...] * mixed_s[...] * (ga * jax.nn.sigmoid(ga))
                ra = lax.rsqrt(_lanemean(ya * ya) + EPS)
                y_ref[:, 0:D_HALF] = (ya * ra * oga_ref[...]).astype(BF16)

            @pl.when(always[2])
            def _():
                project(2)
                xb = zin(3)
                taps = _conv_taps(xb, halo_s[...])
                halo_s[...] = xb[tm - SUBLANES:]
                xc = cb_ref[...] + taps[0] * cw_ref[0:1, :]
                for k in range(1, CONV_W):
                    xc = xc + taps[k] * cw_ref[k:k + 1, :]
                xcbf_s[...] = xc.astype(BF16)
                _lru_gates(xcbf_s, wa_ref, wx_ref, ba_ref, bx_ref, r_s, i_s)
                a, mult = _decay_parts(r_s[...], lam_ref[...])
                row = lax.broadcasted_iota(jnp.int32, a.shape, 0)
                mult = jnp.where(jnp.logical_and(s == 1, row == 0), 1.0, mult)
                r_s[...] = a
                i_s[...] = mult * (i_s[...] * xc)

            @pl.when(always[3])
            def _():
                project(3)
                a = r_s[...]
                b = i_s[...]
                r8 = lax.broadcasted_iota(jnp.int32, a.shape, 0) & (SUBLANES - 1)
                for d in (1, 2, 4):
                    a_sh = pltpu.roll(a, d, 0)
                    b_sh = pltpu.roll(b, d, 0)
                    m = r8 >= d
                    b = jnp.where(m, a * b_sh + b, b)
                    a = jnp.where(m, a * a_sh, a)
                carry = carry_s[...]
                for g in range(hb):
                    rows = slice(g * SUBLANES, (g + 1) * SUBLANES)
                    hg = a[rows] * carry + b[rows]
                    h_ref[rows, :] = hg
                    carry = jnp.broadcast_to(hg[SUBLANES - 1:SUBLANES, :], hg.shape)
                carry_s[...] = carry
                gb = zin(4)
                yb = h_ref[...] * (gb * jax.nn.sigmoid(gb))
                rb = lax.rsqrt(_lanemean(yb * yb) + EPS)
                y_ref[:, D_HALF:] = (yb * rb * ogb_ref[...]).astype(BF16)

        @pl.when(s % 2 == 0)
        def _():
            step(zbuf0, zbuf1)

        @pl.when(s % 2 == 1)
        def _():
            step(zbuf1, zbuf0)

    const = lambda a: pl.BlockSpec(a.shape, lambda s, n=a.ndim: (0,) * n, pipeline_mode=pl.Buffered(1))
    proj = lambda n: pl.BlockSpec((tm, n), lambda s: (jnp.minimum(s, nt - 1), 0))
    head = lambda n: pl.BlockSpec((tm, n), lambda s: (jnp.maximum(s - 1, 0), 0))
    names = ("ln_g", "ln_b", "wt", "bsx", "conv_w", "conv_b", "w_a", "w_x", "b_a", "b_x", "lam", "oga", "ogb")
    pr = [prm[n] for n in names] + [token]
    big = lambda dt: pltpu.VMEM((tm, D_HALF), dt)
    return pl.pallas_call(
        body, name="inproj_branches_fwd", grid=(nt + 1,),
        in_specs=[proj(D_MODEL), const(pre_g), const(wg_in)] + [const(a) for a in pr],
        out_specs=[proj(D_Z), proj(D_MODEL), head(D_MODEL), head(D_HALF)],
        out_shape=[jax.ShapeDtypeStruct((t, D_Z), BF16), jax.ShapeDtypeStruct((t, D_MODEL), BF16),
                   jax.ShapeDtypeStruct((t, D_MODEL), BF16), jax.ShapeDtypeStruct((t, D_HALF), F32)],
        scratch_shapes=[pltpu.VMEM((tm, D_Z), BF16), pltpu.VMEM((tm, D_Z), BF16),
                        big(BF16), big(F32), big(BF16), big(F32), big(F32), big(F32),
                        pltpu.VMEM((SUBLANES, D_HALF), F32), pltpu.VMEM((SUBLANES, D_HALF), F32)],
        compiler_params=_params("arbitrary"),
    )(x, pre_g, wg_in, *pr)


def _outproj_fwd(x, y, p, tgt, post_g, w_out, w_pg, wg_pe, tm):
    t = x.shape[0]

    def body(x_ref, y_ref, p_ref, tgt_ref, pg_ref, wo_ref, wpg_ref, wpe_ref,
             o_ref, h1_ref, gt_ref, dout_ref, loss_ref):
        @pl.when(pl.program_id(0) == 0)
        def _():
            loss_ref[...] = jnp.zeros_like(loss_ref)

        o = _dot(y_ref[...], wo_ref[...])
        o_ref[...] = o
        r3 = lax.rsqrt(_lanemean(o * o) + EPS)
        h1 = x_ref[...] + (o * r3) * pg_ref[...]
        h1b = h1.astype(BF16)
        h1_ref[...] = h1b
        gt = jax.nn.sigmoid(_dot(h1b, wpg_ref[...]))
        gt_ref[...] = gt
        pb = p_ref[...].astype(BF16)
        for k in range(N_CHIPS):
            cols = slice(k * W_PE_COLS, (k + 1) * W_PE_COLS)
            pe = _dot(pb, wpe_ref[k])
            d = h1[:, cols] + pe * gt[:, cols] - tgt_ref[:, cols]
            dout_ref[:, cols] = d * (1.0 / D_MODEL)
            loss_ref[...] += jnp.sum(d * d) * (0.5 / D_MODEL)

    row = lambda n: pl.BlockSpec((tm, n), lambda i: (i, 0))
    const = lambda shp: pl.BlockSpec(shp, lambda i, n=len(shp): (0,) * n, pipeline_mode=pl.Buffered(1))
    return pl.pallas_call(
        body, name="outproj_fwd", grid=(t // tm,),
        in_specs=[row(D_MODEL), row(D_MODEL), row(D_PLE), row(D_MODEL), const((1, D_MODEL)),
                  const((D_MODEL, D_MODEL)), const((D_MODEL, D_MODEL)), const((N_CHIPS, D_PLE, W_PE_COLS))],
        out_specs=[row(D_MODEL), row(D_MODEL), row(D_MODEL), row(D_MODEL),
                   pl.BlockSpec((SUBLANES, LANES), lambda i: (0, 0))],
        out_shape=[jax.ShapeDtypeStruct((t, D_MODEL), F32), jax.ShapeDtypeStruct((t, D_MODEL), BF16),
                   jax.ShapeDtypeStruct((t, D_MODEL), F32), jax.ShapeDtypeStruct((t, D_MODEL), F32),
                   jax.ShapeDtypeStruct((SUBLANES, LANES), F32)],
        compiler_params=_params("arbitrary"),
    )(x, y, p, tgt, post_g, w_out, w_pg, wg_pe)


def _head_bwd(dout, gt, p, o, post_g, w_out, w_pg, wg_pe, tm):
    t = dout.shape[0]

    def body(dout_ref, gt_ref, p_ref, o_ref, pg_ref, wo_ref, wpg_ref, wpe_ref,
             gwpe_ref, dq_ref, dh1_ref, do_ref, dy_ref, gpost_ref):
        i = pl.program_id(0)

        @pl.when(i == 0)
        def _():
            gpost_ref[...] = jnp.zeros_like(gpost_ref)
            gwpe_ref[...] = jnp.zeros_like(gwpe_ref)

        dout = dout_ref[...]
        gt = gt_ref[...]
        pb = p_ref[...].astype(BF16)
        for k in range(N_CHIPS):
            cols = slice(k * W_PE_COLS, (k + 1) * W_PE_COLS)
            pe = _dot(pb, wpe_ref[k])
            g = gt[:, cols]
            dg = dout[:, cols] * g
            gwpe_ref[k] += _dot_tn(pb, dg.astype(BF16))
            dq_ref[:, cols] = (dg * pe * (1.0 - g)).astype(BF16)
        dh1 = dout + _dot_nt(dq_ref[...], wpg_ref[...])
        dh1_ref[...] = dh1
        o = o_ref[...]
        r3 = lax.rsqrt(_lanemean(o * o) + EPS)
        on = o * r3
        gpost_ref[...] += _rowsum8(dh1 * on)
        don = dh1 * pg_ref[...]
        do = r3 * (don - on * _lanemean(don * on))
        dob = do.astype(BF16)
        do_ref[...] = dob
        dy_ref[...] = _dot_nt(dob, wo_ref[...])

        @pl.when(i == pl.num_programs(0) - 1)
        def _():
            gpost_ref[...] = jnp.broadcast_to(jnp.sum(gpost_ref[...], axis=0, keepdims=True), gpost_ref.shape)

    row = lambda n: pl.BlockSpec((tm, n), lambda i: (i, 0))
    const = lambda shp: pl.BlockSpec(shp, lambda i, n=len(shp): (0,) * n, pipeline_mode=pl.Buffered(1))
    return pl.pallas_call(
        body, name="head_bwd", grid=(t // tm,),
        in_specs=[row(D_MODEL), row(D_MODEL), row(D_PLE), row(D_MODEL), const((1, D_MODEL)),
                  const((D_MODEL, D_MODEL)), const((D_MODEL, D_MODEL)), const((N_CHIPS, D_PLE, W_PE_COLS))],
        out_specs=[pl.BlockSpec((N_CHIPS, D_PLE, W_PE_COLS), lambda i: (0, 0, 0)),
                   row(D_MODEL), row(D_MODEL), row(D_MODEL), row(D_MODEL),
                   pl.BlockSpec((SUBLANES, D_MODEL), lambda i: (0, 0))],
        out_shape=[jax.ShapeDtypeStruct((N_CHIPS, D_PLE, W_PE_COLS), F32), jax.ShapeDtypeStruct((t, D_MODEL), BF16),
                   jax.ShapeDtypeStruct((t, D_MODEL), F32), jax.ShapeDtypeStruct((t, D_MODEL), BF16),
                   jax.ShapeDtypeStruct((t, D_MODEL), F32), jax.ShapeDtypeStruct((SUBLANES, D_MODEL), F32)],
        compiler_params=_params("arbitrary"),
    )(dout, gt, p, o, post_g, w_out, w_pg, wg_pe)


def _branches_bwd(z, h, dy, prm, tm, token):
    t = z.shape[0]
    nt = t // tm
    hb = tm // SUBLANES

    def body(u_ref, v_ref, ga_ref, xb_ref, gb_ref, xbh_ref, h_ref, hh_ref, dy_ref,
             lng_ref, lnb_ref, wt_ref, wtt_ref, bsx_ref, cw_ref, cb_ref, wa_ref, wx_ref, ba_ref, bx_ref, lam_ref,
             oga_ref, ogb_ref, token_ref,
             dz_ref, g_oga, g_ogb, g_lng, g_lnb, g_bsx, g_ws, g_cw, g_cb, g_wa, g_ba, g_wx, g_bx, g_lam,
             vn_s, mixed_s, dm_s, dvn_s, xcbf_s, r_s, i_s, a_s, b_s, dh_s, dpr_s, dpi_s, dxc_s,
             ca_s, cd_s, cx_s):
        step_i = pl.program_id(0)
        tile = nt - 1 - step_i
        accs = (g_oga, g_ogb, g_lng, g_lnb, g_bsx, g_ws, g_cw, g_cb, g_wa, g_ba, g_wx, g_bx, g_lam)

        @pl.when(step_i == 0)
        def _():
            for r in accs + (ca_s, cd_s, cx_s):
                r[...] = jnp.zeros_like(r)

        dy_a = dy_ref[:, 0:D_HALF]
        dy_b = dy_ref[:, D_HALF:]

        u = u_ref[...].astype(F32)
        ug, tu = _gelu(u)
        v = v_ref[...].astype(F32)
        vg, tv = _gelu(v)
        vhat, rstd = _layernorm_parts(vg)
        vn_s[...] = (vhat * lng_ref[...] + lnb_ref[...]).astype(BF16)
        _spatial_mix(wt_ref, vn_s, bsx_ref, mixed_s, tm)
        mixed = mixed_s[...]
        ga = ga_ref[...].astype(F32)
        sga = jax.nn.sigmoid(ga)
        sa = ga * sga
        um = ug * mixed
        ya = um * sa
        ra = lax.rsqrt(_lanemean(ya * ya) + EPS)
        yahat = ya * ra
        g_oga[...] += _rowsum8(dy_a * yahat)
        dn = dy_a * oga_ref[...]
        dya = ra * (dn - yahat * _lanemean(dn * yahat))
        dz_ref[:, 2 * D_HALF:3 * D_HALF] = (dya * um * (sga * (1.0 + ga * (1.0 - sga)))).astype(BF16)
        dz_ref[:, 0:D_HALF] = (dya * mixed * sa * _gelu_grad(u, tu)).astype(BF16)
        dmixed = dya * ug * sa
        g_bsx[...] += jnp.sum(dmixed.reshape(tm // CHUNK, CHUNK, D_HALF), axis=0)
        dm_s[...] = dmixed.astype(BF16)
        for c in range(tm // CHUNK):
            rows = slice(c * CHUNK, (c + 1) * CHUNK)
            for hd in range(N_HEADS):
                cols = slice(hd * CHUNK, (hd + 1) * CHUNK)
                dmh = dm_s[rows, cols]
                dvn_s[rows, cols] = _dot(wtt_ref[hd], dmh)
                g_ws[hd] += _dot_nt(dmh, vn_s[rows, cols])
        dvn = dvn_s[...]
        g_lng[...] += _rowsum8(dvn * vhat)
        g_lnb[...] += _rowsum8(dvn)
        dvh = dvn * lng_ref[...]
        dvg = rstd * (dvh - _lanemean(dvh) - vhat * _lanemean(dvh * vhat))
        dz_ref[:, D_HALF:2 * D_HALF] = (dvg * _gelu_grad(v, tv)).astype(BF16)

        xb = xb_ref[...].astype(F32)
        halo = jnp.where(tile == 0, 0.0, xbh_ref[...].astype(F32)[SUBLANES:])
        taps = _conv_taps(xb, halo)
        xc = cb_ref[...] + taps[0] * cw_ref[0:1, :]
        for k in range(1, CONV_W):
            xc = xc + taps[k] * cw_ref[k:k + 1, :]
        xcbf_s[...] = xc.astype(BF16)
        _lru_gates(xcbf_s, wa_ref, wx_ref, ba_ref, bx_ref, r_s, i_s)
        rg = r_s[...]
        ig = i_s[...]
        lam = lam_ref[...]
        a, mult_true = _decay_parts(rg, lam)
        row = lax.broadcasted_iota(jnp.int32, a.shape, 0)
        first = jnp.logical_and(tile == 0, row == 0)
        mult = jnp.where(first, 1.0, mult_true)
        hcur = h_ref[...]
        hprev = _shift_down(hcur, jnp.where(tile == 0, 0.0, hh_ref[...]), 1)
        gb = gb_ref[...].astype(F32)
        sgb = jax.nn.sigmoid(gb)
        sb = gb * sgb
        yb = hcur * sb
        rb = lax.rsqrt(_lanemean(yb * yb) + EPS)
        ybhat = yb * rb
        g_ogb[...] += _rowsum8(dy_b * ybhat)
        dn = dy_b * ogb_ref[...]
        dyb = rb * (dn - ybhat * _lanemean(dn * ybhat))
        dz_ref[:, 4 * D_HALF:5 * D_HALF] = (dyb * hcur * (sgb * (1.0 + gb * (1.0 - sgb)))).astype(BF16)

        an = _shift_up(a, ca_s[...], 1)
        bb = dyb * sb
        r8 = row & (SUBLANES - 1)
        for d in (1, 2, 4):
            a_sh = pltpu.roll(an, tm - d, 0)
            b_sh = pltpu.roll(bb, tm - d, 0)
            m = r8 + d < SUBLANES
            bb = jnp.where(m, an * b_sh + bb, bb)
            an = jnp.where(m, an * a_sh, an)
        a_s[...] = an
        b_s[...] = bb

        def step(g, carry):
            sl = pl.ds(pl.multiple_of((hb - 1 - g) * SUBLANES, SUBLANES), SUBLANES)
            dg = a_s[sl, :] * carry + b_s[sl, :]
            dh_s[sl, :] = dg
            return jnp.broadcast_to(dg[0:1, :], dg.shape)

        cd_s[...] = lax.fori_loop(0, hb, step, cd_s[...])
        ca_s[...] = jnp.broadcast_to(a[0:1, :], ca_s.shape)
        dh = dh_s[...]
        da = dh * hprev
        gx = ig * xc
        dla = da * a - jnp.where(first, 0.0, dh * gx * (a * a / mult_true))
        g_lam[...] += _rowsum8(dla * rg)
        dr = dla * (-LRU_C * _softplus_neg(lam))
        dpr = dr * rg * (1.0 - rg)
        dpi = (dh * mult * xc) * ig * (1.0 - ig)
        g_ba[...] += _rowsum8(dpr)
        g_bx[...] += _rowsum8(dpi)
        dpr_s[...] = dpr.astype(BF16)
        dpi_s[...] = dpi.astype(BF16)
        for hd in range(N_HEADS):
            cols = slice(hd * CHUNK, (hd + 1) * CHUNK)
            xh = xcbf_s[:, cols]
            dprh = dpr_s[:, cols]
            dpih = dpi_s[:, cols]
            g_wa[hd] += _dot_tn(xh, dprh)
            g_wx[hd] += _dot_tn(xh, dpih)
            dxc_s[:, cols] = _dot_nt(dprh, wa_ref[hd]) + _dot_nt(dpih, wx_ref[hd])
        dxc = dxc_s[...] + dh * mult * ig
        g_cb[...] += _rowsum8(dxc)
        for k in range(CONV_W):
            g_cw[k * SUBLANES:(k + 1) * SUBLANES, :] += _rowsum8(dxc * taps[k])
        nxt = cx_s[...]
        dxb = dxc * cw_ref[CONV_W - 1:CONV_W, :]
        for j in range(1, CONV_W):
            dxb = dxb + _shift_up(dxc, nxt, j) * cw_ref[CONV_W - 1 - j:CONV_W - j, :]
        dz_ref[:, 3 * D_HALF:4 * D_HALF] = dxb.astype(BF16)
        cx_s[...] = dxc[0:SUBLANES]

        @pl.when(step_i == nt - 1)
        def _():
            for r in (g_oga, g_ogb, g_lng, g_lnb, g_cb, g_ba, g_bx):
                r[...] = jnp.broadcast_to(jnp.sum(r[...], axis=0, keepdims=True), r.shape)
            lam_f = LRU_C * jax.nn.sigmoid(-lam_ref[...])
            g_lam[...] = jnp.broadcast_to(jnp.sum(g_lam[...], axis=0, keepdims=True) * lam_f, g_lam.shape)
            for k in range(CONV_W):
                blk = g_cw[k * SUBLANES:(k + 1) * SUBLANES, :]
                g_cw[k * SUBLANES:(k + 1) * SUBLANES, :] = jnp.broadcast_to(jnp.sum(blk, axis=0, keepdims=True), blk.shape)
            tri = (lax.broadcasted_iota(jnp.int32, (CHUNK, CHUNK), 0) >= lax.broadcasted_iota(jnp.int32, (CHUNK, CHUNK), 1))
            for hd in range(N_HEADS):
                cols = slice(hd * CHUNK, (hd + 1) * CHUNK)
                g_ws[hd] = jnp.where(tri, g_ws[hd], 0.0)
                blk = g_bsx[:, cols]
                g_bsx[:, cols] = jnp.broadcast_to(jnp.sum(blk, axis=1, keepdims=True), blk.shape)

    rev = lambda i: nt - 1 - i
    zspec = lambda g: pl.BlockSpec((tm, D_HALF), lambda i, g=g: (rev(i), g))
    halo = lambda col: pl.BlockSpec((SUBLANES, D_HALF), lambda i: (jnp.maximum(rev(i) * hb - 1, 0), col))
    zhalo = pl.BlockSpec((2 * SUBLANES, D_HALF), lambda i: (jnp.maximum(rev(i) * (hb // 2) - 1, 0), 3))
    full = lambda a: pl.BlockSpec(a.shape, lambda i, n=a.ndim: (0,) * n)
    acc = lambda shp: pl.BlockSpec(shp, lambda i, n=len(shp): (0,) * n)
    names = ("ln_g", "ln_b", "wt", "wtt", "bsx", "conv_w", "conv_b", "w_a", "w_x", "b_a", "b_x", "lam", "oga", "ogb")
    pr = [prm[n] for n in names] + [token]
    vec = (SUBLANES, D_HALF)
    mat = (N_HEADS, CHUNK, CHUNK)
    acc_shapes = [vec, vec, vec, vec, (CHUNK, D_HALF), mat, (CONV_W * SUBLANES, D_HALF), vec, mat, vec, mat, vec, vec]
    big = lambda dt: pltpu.VMEM((tm, D_HALF), dt)
    return pl.pallas_call(
        body, name="branches_bwd", grid=(nt,),
        in_specs=[zspec(0), zspec(1), zspec(2), zspec(3), zspec(4), zhalo,
                  pl.BlockSpec((tm, D_HALF), lambda i: (rev(i), 0)), halo(0),
                  pl.BlockSpec((tm, D_MODEL), lambda i: (rev(i), 0))] + [full(a) for a in pr],
        out_specs=[pl.BlockSpec((tm, D_Z), lambda i: (rev(i), 0))] + [acc(s) for s in acc_shapes],
        out_shape=[jax.ShapeDtypeStruct((t, D_Z), BF16)] + [jax.ShapeDtypeStruct(s, F32) for s in acc_shapes],
        scratch_shapes=[big(BF16), big(F32), big(BF16), big(F32), big(BF16), big(F32), big(F32), big(F32), big(F32),
                        big(F32), big(BF16), big(BF16), big(F32),
                        pltpu.VMEM(vec, F32), pltpu.VMEM(vec, F32), pltpu.VMEM(vec, F32)],
        compiler_params=_params("arbitrary"),
    )(z, z, z, z, z, z, h, h, dy, *pr)


def _inproj_bwd(dz, wg_in, x, dh1, pre_g, tm, tile0, nt, prev, last, token, name):
    t = x.shape[0]

    def body(*refs):
        dz_ref, w_ref, x_ref, dh1_ref, g_ref = refs[:5]
        gx_ref, gpre_ref, acc_s = refs[-3:]
        i = pl.program_id(0)

        @pl.when(i == 0)
        def _():
            gpre_ref[...] = jnp.zeros_like(gpre_ref) if prev is None else refs[7][...]

        acc = _dot_nt(dz_ref[:, 0:W_IN_COLS], w_ref[0])
        for k in range(1, N_CHIPS):
            acc = acc + _dot_nt(dz_ref[:, k * W_IN_COLS:(k + 1) * W_IN_COLS], w_ref[k])
        acc_s[...] = acc
        for s in range(tm // CHUNK):
            rows = slice(s * CHUNK, (s + 1) * CHUNK)
            xv = x_ref[rows, :]
            r = lax.rsqrt(_lanemean(xv * xv) + EPS)
            xhat = xv * r
            dhn = acc_s[rows, :]
            gpre_ref[...] += _rowsum8(dhn * xhat)
            dxh = dhn * g_ref[...]
            gx_ref[rows, :] = dh1_ref[rows, :] + r * (dxh - xhat * _lanemean(dxh * xhat))

        if last:
            @pl.when(i == nt - 1)
            def _():
                gpre_ref[...] = jnp.broadcast_to(jnp.sum(gpre_ref[...], axis=0, keepdims=True), gpre_ref.shape)

    row = lambda n: pl.BlockSpec((tm, n), lambda i: (tile0 + i, 0))
    small = lambda r: pl.BlockSpec((r, D_MODEL), lambda i: (0, 0))
    tok = pl.BlockSpec((SUBLANES, LANES), lambda i: (0, 0))
    in_specs = [row(D_Z), pl.BlockSpec(wg_in.shape, lambda i: (0, 0, 0), pipeline_mode=pl.Buffered(1)),
                row(D_MODEL), row(D_MODEL), small(1), tok]
    args = [dz, wg_in, x, dh1, pre_g, token]
    aliases = {}
    if prev is not None:
        in_specs += [ANY, small(SUBLANES)]
        args += list(prev)
        aliases = {6: 0}
    return pl.pallas_call(
        body, name=name, grid=(nt,), in_specs=in_specs, out_specs=[row(D_MODEL), small(SUBLANES)],
        out_shape=[jax.ShapeDtypeStruct((t, D_MODEL), F32), jax.ShapeDtypeStruct((SUBLANES, D_MODEL), F32)],
        input_output_aliases=aliases,
        scratch_shapes=[pltpu.VMEM((tm, D_MODEL), F32)],
        compiler_params=_params("arbitrary"),
    )(*args)


def _weight_grad(a, b, name, kb, nb, tk, tn, tt, token):
    t = a.shape[0]
    tt = min(tt, t)

    def body(a_ref, b_ref, token_ref, o_ref):
        @pl.when(pl.program_id(2) == 0)
        def _():
            o_ref[...] = jnp.zeros_like(o_ref)

        o_ref[...] += _dot_tn(a_ref[...], b_ref[...])

    return pl.pallas_call(
        body, name=name, grid=(nb, kb, t // tt),
        in_specs=[pl.BlockSpec((tt, tk), lambda j, i, s: (s, i)), pl.BlockSpec((tt, tn), lambda j, i, s: (s, j)),
                  pl.BlockSpec((SUBLANES, LANES), lambda j, i, s: (0, 0))],
        out_specs=pl.BlockSpec((None, None, tk, tn), lambda j, i, s: (j, i, 0, 0)),
        out_shape=jax.ShapeDtypeStruct((nb, kb, tk, tn), F32),
        compiler_params=_params("parallel", "parallel", "arbitrary"),
    )(a, b, token)


def _place():
    x, y, c = lax.axis_index("x"), lax.axis_index("y"), lax.axis_index("c")
    return x, y, c


def _chip_of(x, y):
    return 2 * x + y


def _gather_weights(w_in, conv_w):
    halves = [(D_MODEL // 2, W_IN_COLS)]
    nw = len(halves)

    def body(win_ref, cw_ref, gin_ref, gcw_ref, s0, b0, lsem, send_sems, recv_sems, cw_send, cw_recv):
        x, y, c = _place()
        me = _chip_of(x, y)
        sibling = (x, y, 1 - c)
        chips = [(1 - x, y), (x, 1 - y), (1 - x, 1 - y)]
        srcs = (win_ref,)
        stage = (s0,)
        bf = (b0,)
        outs = (gin_ref,)
        loads = []
        for n in range(nw):
            rows = halves[n][0]
            cp = pltpu.make_async_copy(srcs[n].at[pl.ds(c * rows, rows), :], stage[n], lsem.at[n])
            cp.start()
            loads.append(cp)
        own_cw = pltpu.make_async_copy(cw_ref, gcw_ref.at[me], lsem.at[2 * nw])
        own_cw.start()
        for n in range(nw):
            loads[n].wait()
            bf[n][...] = stage[n][...].astype(BF16)

        def copy(n, k, chip, to, src=None):
            dst = outs[n].at[chip, c]
            return pltpu.make_async_remote_copy(
                src_ref=dst if src is None else src, dst_ref=dst,
                send_sem=send_sems.at[n, k], recv_sem=recv_sems.at[n, k], device_id=to, device_id_type=MESH)

        def recv(n, k, chip, core):
            dst = outs[n].at[chip, core]
            return pltpu.make_async_remote_copy(
                src_ref=dst, dst_ref=dst, send_sem=send_sems.at[n, k], recv_sem=recv_sems.at[n, k],
                device_id=sibling, device_id_type=MESH)

        sends = []
        locals_ = []
        for n in range(nw):
            lc = pltpu.make_async_copy(bf[n], outs[n].at[me, c], lsem.at[nw + n])
            lc.start()
            locals_.append(lc)
            first = [copy(n, 0, me, sibling, src=bf[n])]
            first += [copy(n, 1 + j, me, (*chip, c), src=bf[n]) for j, chip in enumerate(chips)]
            for cp in first:
                cp.start()
            sends += first
        cws = []
        for j, chip in enumerate(chips):
            cp = pltpu.make_async_remote_copy(
                src_ref=cw_ref, dst_ref=gcw_ref.at[me], send_sem=cw_send.at[j], recv_sem=cw_recv.at[j],
                device_id=(*chip, c), device_id_type=MESH)
            cp.start()
            cws.append(cp)
        for n in range(nw):
            for j, chip in enumerate(chips):
                kj = _chip_of(*chip)
                recv(n, 1 + j, kj, c).wait_recv()
                fw = copy(n, 4 + j, kj, sibling)
                fw.start()
                sends.append(fw)
        for n in range(nw):
            recv(n, 0, me, 1 - c).wait_recv()
            for j, chip in enumerate(chips):
                recv(n, 4 + j, _chip_of(*chip), 1 - c).wait_recv()
        for j, chip in enumerate(chips):
            pltpu.make_async_remote_copy(
                src_ref=cw_ref, dst_ref=gcw_ref.at[_chip_of(*chip)], send_sem=cw_send.at[j], recv_sem=cw_recv.at[j],
                device_id=(*chip, c), device_id_type=MESH).wait_recv()
        for cp in sends + cws:
            cp.wait_send()
        for lc in locals_:
            lc.wait()
        own_cw.wait()

    out_shape = [jax.ShapeDtypeStruct((N_CHIPS, 2) + hs, BF16) for hs in halves]
    out_shape.append(jax.ShapeDtypeStruct((N_CHIPS, CONV_W, CONV_COLS), F32))
    scratch = [pltpu.VMEM(hs, F32) for hs in halves] + [pltpu.VMEM(hs, BF16) for hs in halves]
    scratch += [pltpu.SemaphoreType.DMA((2 * nw + 1,)), pltpu.SemaphoreType.DMA((nw, 7)),
                pltpu.SemaphoreType.DMA((nw, 7)), pltpu.SemaphoreType.DMA((3,)), pltpu.SemaphoreType.DMA((3,))]
    return pl.pallas_call(
        body, name="gather_w_in", in_specs=[ANY] * 2, out_specs=[ANY] * 2, out_shape=out_shape,
        scratch_shapes=scratch, compiler_params=pltpu.CompilerParams(vmem_limit_bytes=VMEM_LIMIT),
    )(w_in, conv_w)


HBM = pl.BlockSpec(memory_space=pltpu.HBM)
SEM = pl.BlockSpec(memory_space=pltpu.SEMAPHORE)
EFFECT = pltpu.SideEffectType.DATAFLOW_SIDE_EFFECTING


def _hbm(a):
    return pltpu.with_memory_space_constraint(a, pltpu.HBM)


def _landing(shape, dtype):
    return _hbm(lax.empty(shape, dtype))


def _exchange_start(name, arrays, ncopies, build, after=None):
    n = len(arrays)
    extra = [] if after is None else [after]

    def body(*refs):
        ins, token = refs[:n], refs[-1]
        send_sems, recv_sems = refs[n + len(extra)], refs[n + len(extra) + 1]
        for cp in build(ins, send_sems, recv_sems):
            cp.start()
        token[...] = jnp.zeros_like(token)

    outs = pl.pallas_call(
        body, name=name,
        out_shape=(pltpu.SemaphoreType.DMA((ncopies,)), pltpu.SemaphoreType.DMA((ncopies,)),
                   *[pltpu.HBM(a.shape, a.dtype) for a in arrays], jax.ShapeDtypeStruct((SUBLANES, LANES), F32)),
        in_specs=[HBM] * n + [ANY] * len(extra),
        out_specs=(SEM, SEM, *[HBM] * n, pl.BlockSpec(memory_space=pltpu.VMEM)),
        input_output_aliases={q: q + 2 for q in range(n)},
        compiler_params=pltpu.CompilerParams(has_side_effects=EFFECT),
    )(*[_hbm(a) for a in arrays], *extra)
    return (outs[0], outs[1], list(outs[2:2 + n])), outs[-1]


def _exchange_wait(name, started, after, build):
    send, recv, arrays = started
    n = len(arrays)

    def body(*refs):
        ins, send_sems, recv_sems = refs[:n], refs[n], refs[n + 1]
        for cp in build(ins, send_sems, recv_sems):
            cp.wait_send()
            cp.wait_recv()

    return pl.pallas_call(
        body, name=name, out_shape=tuple(pltpu.HBM(a.shape, a.dtype) for a in arrays),
        in_specs=[HBM] * n + [SEM, SEM, ANY], out_specs=tuple([HBM] * n),
        input_output_aliases={q: q for q in range(n)},
        compiler_params=pltpu.CompilerParams(has_side_effects=EFFECT),
    )(*arrays, send, recv, after)


def _exchange_wait_start(name, started, after, build_wait, ncopies, build_start):
    send, recv, arrays = started
    n = len(arrays)

    def body(*refs):
        ins, send_sems, recv_sems = refs[:n], refs[n], refs[n + 1]
        send2, recv2, token = refs[n + 3], refs[n + 4], refs[-1]
        for cp in build_wait(ins, send_sems, recv_sems):
            cp.wait_send()
            cp.wait_recv()
        for cp in build_start(ins, send2, recv2):
            cp.start()
        token[...] = jnp.zeros_like(token)

    outs = pl.pallas_call(
        body, name=name,
        out_shape=(pltpu.SemaphoreType.DMA((ncopies,)), pltpu.SemaphoreType.DMA((ncopies,)),
                   *[pltpu.HBM(a.shape, a.dtype) for a in arrays], jax.ShapeDtypeStruct((SUBLANES, LANES), F32)),
        in_specs=[HBM] * n + [SEM, SEM, ANY], out_specs=(SEM, SEM, *[HBM] * n, pl.BlockSpec(memory_space=pltpu.VMEM)),
        input_output_aliases={q: q + 2 for q in range(n)},
        compiler_params=pltpu.CompilerParams(has_side_effects=EFFECT),
    )(*arrays, send, recv, after)
    return (outs[0], outs[1], list(outs[2:2 + n])), outs[-1]


def _cast_into_slot(w, kc, name):
    rows, cols = w.shape
    tr = min(rows, 256)

    def body(kc_ref, w_ref, o_ref):
        o_ref[...] = w_ref[...].astype(BF16)

    grid_spec = pltpu.PrefetchScalarGridSpec(
        num_scalar_prefetch=1, grid=(rows // tr,),
        in_specs=[pl.BlockSpec((tr, cols), lambda r, kc: (r, 0))],
        out_specs=pl.BlockSpec((None, tr, cols), lambda r, kc: (kc[0], r, 0)))
    return pl.pallas_call(
        body, name=name, grid_spec=grid_spec, out_shape=jax.ShapeDtypeStruct((N_CHIPS, rows, cols), BF16),
        compiler_params=_params("arbitrary"),
    )(kc, w)


def _gather_ici_copies(n):
    def build(refs, send_sems, recv_sems):
        x, y, c = _place()
        mine = lambda b: refs[b].at[_chip_of(x, y), c]
        chips = [(1 - x, y), (x, 1 - y), (1 - x, 1 - y)]
        return [pltpu.make_async_remote_copy(
            src_ref=mine(b), dst_ref=mine(b), send_sem=send_sems.at[3 * b + j], recv_sem=recv_sems.at[3 * b + j],
            device_id=(*chip, c), device_id_type=MESH) for b in range(n) for j, chip in enumerate(chips)]
    return build


def _gather_relay_copies(n):
    def build(refs, send_sems, recv_sems):
        x, y, c = _place()
        chips = [(1 - x, y), (x, 1 - y), (1 - x, 1 - y)]
        cps = []
        for b in range(n):
            for j, chip in enumerate(chips):
                got = refs[b].at[_chip_of(*chip), c]
                cps.append(pltpu.make_async_remote_copy(
                    src_ref=got, dst_ref=got, send_sem=send_sems.at[3 * b + j], recv_sem=recv_sems.at[3 * b + j],
                    device_id=(x, y, 1 - c), device_id_type=MESH))
        return cps
    return build


def _sibling_copies(n):
    def build(refs, send_sems, recv_sems):
        x, y, c = _place()
        return [pltpu.make_async_remote_copy(
            src_ref=refs[b].at[:, 1 - c], dst_ref=refs[n + b], send_sem=send_sems.at[b], recv_sem=recv_sems.at[b],
            device_id=(x, y, 1 - c), device_id_type=MESH) for b in range(n)]
    return build


def _chip_copies(n):
    def build(refs, send_sems, recv_sems):
        x, y, c = _place()
        chips = [(1 - x, y), (x, 1 - y), (1 - x, 1 - y)]
        return [pltpu.make_async_remote_copy(
            src_ref=refs[b].at[_chip_of(*chip)], dst_ref=refs[n + b].at[j],
            send_sem=send_sems.at[3 * b + j], recv_sem=recv_sems.at[3 * b + j],
            device_id=(*chip, c), device_id_type=MESH) for b in range(n) for j, chip in enumerate(chips)]
    return build


def _finish_copies(n, n_all):
    def build(refs, send_sems, recv_sems):
        x, y, c = _place()
        cps = [pltpu.make_async_remote_copy(
            src_ref=refs[b].at[c], dst_ref=refs[b].at[c], send_sem=send_sems.at[b], recv_sem=recv_sems.at[b],
            device_id=(x, y, 1 - c), device_id_type=MESH) for b in range(n)]
        flips = [(fx, fy, fc) for fx in (0, 1) for fy in (0, 1) for fc in (0, 1)][1:]
        for b in range(n_all):
            mine = refs[n + b].at[_chip_of(x, y), c]
            cps += [pltpu.make_async_remote_copy(
                src_ref=mine, dst_ref=mine, send_sem=send_sems.at[n + 7 * b + q], recv_sem=recv_sems.at[n + 7 * b + q],
                device_id=(x ^ fx, y ^ fy, c ^ fc), device_id_type=MESH) for q, (fx, fy, fc) in enumerate(flips)]
        return cps
    return build


def _pair_sum(g, r1, kc, name, tr, send_dtype):
    nk, _, rows, cols = g.shape

    def body(kc_ref, g_ref, r_ref, p_ref, own_ref):
        s = g_ref[...] + r_ref[...]
        p_ref[...] = s.astype(send_dtype)

        @pl.when(pl.program_id(1) == kc_ref[0])
        def _():
            own_ref[...] = s

    grid_spec = pltpu.PrefetchScalarGridSpec(
        num_scalar_prefetch=1, grid=(rows // tr, nk),
        in_specs=[pl.BlockSpec((None, None, tr, cols), lambda r, k, kc: (k, kc[1], r, 0)),
                  pl.BlockSpec((None, tr, cols), lambda r, k, kc: (k, r, 0))],
        out_specs=[pl.BlockSpec((None, tr, cols), lambda r, k, kc: (k, r, 0)),
                   pl.BlockSpec((tr, cols), lambda r, k, kc: (r, 0))])
    return pl.pallas_call(
        body, name=name, grid_spec=grid_spec,
        out_shape=[jax.ShapeDtypeStruct((nk, rows, cols), send_dtype), jax.ShapeDtypeStruct((rows, cols), F32)],
        compiler_params=_params("arbitrary", "arbitrary"),
    )(kc, g, r1)


def _chip_sum(own, r2, slot, lead, name, tr):
    rows, cols = own.shape
    nl = len(lead)

    def body(slot_ref, o_ref, r_ref, s_ref):
        s = o_ref[...]
        for j in range(3):
            s = s + r_ref[j].astype(F32)
        s_ref[...] = s

    grid_spec = pltpu.PrefetchScalarGridSpec(
        num_scalar_prefetch=1, grid=(rows // tr,),
        in_specs=[pl.BlockSpec((tr, cols), lambda r, sl: (r, 0)), pl.BlockSpec((3, tr, cols), lambda r, sl: (0, r, 0))],
        out_specs=pl.BlockSpec((None,) * nl + (tr, cols), lambda r, sl: tuple(sl[q] for q in range(nl)) + (r, 0)))
    return pl.pallas_call(
        body, name=name, grid_spec=grid_spec, out_shape=jax.ShapeDtypeStruct(tuple(lead) + (rows, cols), F32),
        compiler_params=_params("arbitrary"),
    )(slot, own, r2)


def _adam_update(w, g, m, v):
    nm = ADAM_B1 * m + (1.0 - ADAM_B1) * g
    nv = ADAM_B2 * v + (1.0 - ADAM_B2) * (g * g)
    m_hat = nm / (1.0 - ADAM_B1 ** ADAM_STEP)
    v_hat = nv / (1.0 - ADAM_B2 ** ADAM_STEP)
    return -ADAM_LR * (m_hat / (jnp.sqrt(v_hat) + ADAM_EPS) + ADAM_WD * w), nm, nv


def _adamw(w, g, m, v, name, tr, token):
    rows, cols = w.shape

    def body(w_ref, g_ref, m_ref, v_ref, token_ref, go_ref, d_ref, nm_ref, nv_ref):
        gv = g_ref[...]
        go_ref[...] = gv
        d_ref[...], nm_ref[...], nv_ref[...] = _adam_update(w_ref[...], gv, m_ref[...], v_ref[...])

    spec = pl.BlockSpec((tr, cols), lambda r: (r, 0))
    return pl.pallas_call(
        body, name=name, grid=(rows // tr,),
        in_specs=[spec] * 4 + [pl.BlockSpec((SUBLANES, LANES), lambda r: (0, 0))], out_specs=[spec] * 4,
        out_shape=[jax.ShapeDtypeStruct((rows, cols), F32)] * 4,
        compiler_params=_params("parallel"),
    )(w, g, m, v, token)


def _adamw_small(packed_g, pre_g_parts, ws, ms, vs):
    names = ["pre_g"] + [n for n, _ in SMALL_ROWS if n != "conv_w"]
    rows = dict(SMALL_ROWS)
    offset, at = {}, 0
    for n, r in SMALL_ROWS:
        offset[n] = at
        at += r
    k = len(names)

    def body(*refs):
        g_ref, pg_ref = refs[0], refs[1]
        w_refs, m_refs, v_refs = refs[2:2 + k], refs[2 + k:2 + 2 * k], refs[2 + 2 * k:2 + 3 * k]
        outs = refs[2 + 3 * k:]
        go, do, mo, vo = outs[:k], outs[k:2 * k], outs[2 * k:3 * k], outs[3 * k:4 * k]
        pre = pg_ref[0]
        for dev in range(1, 8):
            pre = pre + pg_ref[dev]
        outs[4 * k][...] = pre[D_MODEL // LANES:, :]
        for i, n in enumerate(names):
            shp = w_refs[i].shape
            if len(shp) == 2 and shp[0] == 1:
                for r in range(shp[1] // LANES):
                    cols = slice(r * LANES, (r + 1) * LANES)
                    g = pre[r:r + 1, :] if n == "pre_g" else g_ref[offset[n] + r:offset[n] + r + 1, :]
                    go[i][:, cols] = g
                    do[i][:, cols], mo[i][:, cols], vo[i][:, cols] = _adam_update(
                        w_refs[i][:, cols], g, m_refs[i][:, cols], v_refs[i][:, cols])
            else:
                g = g_ref[offset[n]:offset[n] + rows[n], :].reshape(shp)
                go[i][...] = g
                do[i][...], mo[i][...], vo[i][...] = _adam_update(w_refs[i][...], g, m_refs[i][...], v_refs[i][...])

    vm = pl.BlockSpec(memory_space=pltpu.VMEM)
    args = [packed_g, pre_g_parts] + [src[n] for src in (ws, ms, vs) for n in names]
    out_shape = [jax.ShapeDtypeStruct(ws[n].shape, F32) for _ in range(4) for n in names]
    out_shape.append(jax.ShapeDtypeStruct((SUBLANES, LANES), F32))
    outs = pl.pallas_call(
        body, name="adamw_small", in_specs=[vm] * len(args), out_specs=[vm] * (4 * k + 1), out_shape=out_shape,
    )(*args)
    return [dict(zip(names, outs[q * k:(q + 1) * k])) for q in range(4)], outs[4 * k]


def _into_slot(v, tail, slot, lead, name):
    n = v.shape[1]
    nl = len(lead)
    rows = n // LANES + SUBLANES

    def body(slot_ref, v_ref, t_ref, o_ref):
        for r in range(n // LANES):
            o_ref[r:r + 1, :] = v_ref[0:1, r * LANES:(r + 1) * LANES]
        o_ref[n // LANES:, :] = t_ref[...]

    grid_spec = pltpu.PrefetchScalarGridSpec(
        num_scalar_prefetch=1, grid=(1,),
        in_specs=[pl.BlockSpec(v.shape, lambda i, sl: (0, 0)), pl.BlockSpec(tail.shape, lambda i, sl: (0, 0))],
        out_specs=pl.BlockSpec((None,) * nl + (rows, LANES), lambda i, sl: tuple(sl[q] for q in range(nl)) + (0, 0)))
    return pl.pallas_call(
        body, name=name, grid_spec=grid_spec, out_shape=jax.ShapeDtypeStruct(tuple(lead) + (rows, LANES), F32),
    )(slot, v, tail)


def _rows128(a):
    return a.reshape(-1, LANES)


def _pack_small(parts):
    pieces = [_rows128(parts[n]) for n, _ in SMALL_ROWS]
    pieces.append(jnp.zeros((SMALL_TOTAL - SMALL_USED, LANES), F32))
    return jnp.concatenate(pieces, axis=0)


def kernel(x, p, pre_g, w_in, gmlp_ln_g, gmlp_ln_b, gmlp_ws, gmlp_bs, conv_w, conv_b, w_a, b_a, w_x, b_x, lam, gmlp_out_g, lru_out_g, w_out, post_g, w_pe, w_pg, loss_target, m_pre_g, m_w_in, m_gmlp_ln_g, m_gmlp_ln_b, m_gmlp_ws, m_gmlp_bs, m_conv_w, m_conv_b, m_w_a, m_b_a, m_w_x, m_b_x, m_lam, m_gmlp_out_g, m_lru_out_g, m_w_out, m_post_g, m_w_pe, m_w_pg, v_pre_g, v_w_in, v_gmlp_ln_g, v_gmlp_ln_b, v_gmlp_ws, v_gmlp_bs, v_conv_w, v_conv_b, v_w_a, v_b_a, v_w_x, v_b_x, v_lam, v_gmlp_out_g, v_lru_out_g, v_w_out, v_post_g, v_w_pe, v_w_pg):
    weights = dict(pre_g=pre_g, w_in=w_in, gmlp_ln_g=gmlp_ln_g, gmlp_ln_b=gmlp_ln_b, gmlp_ws=gmlp_ws, gmlp_bs=gmlp_bs,
                   conv_w=conv_w, conv_b=conv_b, w_a=w_a, b_a=b_a, w_x=w_x, b_x=b_x, lam=lam, gmlp_out_g=gmlp_out_g,
                   lru_out_g=lru_out_g, w_out=w_out, post_g=post_g, w_pe=w_pe, w_pg=w_pg)
    mom_m = dict(pre_g=m_pre_g, w_in=m_w_in, gmlp_ln_g=m_gmlp_ln_g, gmlp_ln_b=m_gmlp_ln_b, gmlp_ws=m_gmlp_ws,
                 gmlp_bs=m_gmlp_bs, conv_w=m_conv_w, conv_b=m_conv_b, w_a=m_w_a, b_a=m_b_a, w_x=m_w_x, b_x=m_b_x,
                 lam=m_lam, gmlp_out_g=m_gmlp_out_g, lru_out_g=m_lru_out_g, w_out=m_w_out, post_g=m_post_g,
                 w_pe=m_w_pe, w_pg=m_w_pg)
    mom_v = dict(pre_g=v_pre_g, w_in=v_w_in, gmlp_ln_g=v_gmlp_ln_g, gmlp_ln_b=v_gmlp_ln_b, gmlp_ws=v_gmlp_ws,
                 gmlp_bs=v_gmlp_bs, conv_w=v_conv_w, conv_b=v_conv_b, w_a=v_w_a, b_a=v_b_a, w_x=v_w_x, b_x=v_b_x,
                 lam=v_lam, gmlp_out_g=v_gmlp_out_g, lru_out_g=v_lru_out_g, w_out=v_w_out, post_g=v_post_g,
                 w_pe=v_w_pe, w_pg=v_w_pg)
    order = list(weights)
    xi, yi, ci = _place()
    me = _chip_of(xi, yi)
    kc = jnp.stack([me, ci]).astype(jnp.int32)

    x2 = x[0]
    p2 = p[0, 0]
    tgt = loss_target[0]

    g_in, g_cw = _gather_weights(w_in[0], conv_w[0, :, 0, :])
    wg_in = g_in.reshape(N_CHIPS, D_MODEL, W_IN_COLS)
    cw_full = jnp.transpose(g_cw, (1, 0, 2)).reshape(CONV_W, D_HALF)
    later = [_cast_into_slot(w_out[0], kc, "cast_w_out").reshape(N_CHIPS, 2, W_ROWS // 2, D_MODEL),
             _cast_into_slot(w_pg[0], kc, "cast_w_pg").reshape(N_CHIPS, 2, W_ROWS // 2, D_MODEL),
             _cast_into_slot(w_pe[0], kc, "cast_w_pe").reshape(N_CHIPS, 2, D_PLE // 2, W_PE_COLS)]
    gather_st, gather_tok = _exchange_start("gather_start", later, 9, _gather_ici_copies(3), after=g_cw)

    causal = jnp.tril(jnp.ones((CHUNK, CHUNK), dtype=bool))
    ws_m = jnp.where(causal[None], gmlp_ws[0], 0.0)
    prm = dict(
        ln_g=gmlp_ln_g, ln_b=gmlp_ln_b, wt=ws_m.astype(BF16), wtt=jnp.transpose(ws_m, (0, 2, 1)).astype(BF16),
        bsx=jnp.repeat(jnp.transpose(gmlp_bs[0]), CHUNK, axis=1),
        conv_w=cw_full, conv_b=conv_b, w_a=w_a[0].astype(BF16), w_x=w_x[0].astype(BF16),
        b_a=b_a[0].reshape(1, D_HALF), b_x=b_x[0].reshape(1, D_HALF), lam=lam, oga=gmlp_out_g, ogb=lru_out_g)

    z, hn, y, h = _inproj_branches_fwd(x2, pre_g, wg_in, prm, 256, gather_tok)
    gather_st, gather_tok = _exchange_wait_start("gather_relay", gather_st, y, _gather_ici_copies(3), 9,
                                                 _gather_relay_copies(3))
    g_out, g_pg, g_pe = _exchange_wait("gather_wait", gather_st, gather_tok, _gather_relay_copies(3))
    wg_out = g_out.reshape(D_MODEL, D_MODEL)
    wg_pg = g_pg.reshape(D_MODEL, D_MODEL)
    wg_pe = g_pe.reshape(N_CHIPS, D_PLE, W_PE_COLS)
    o, h1, gt, dout, loss_acc = _outproj_fwd(x2, y, p2, tgt, post_g, wg_out, wg_pg, wg_pe, 256)

    def sibling_start(tag, bufs):
        lands = [_landing((b.shape[0],) + b.shape[2:], b.dtype) for b in bufs]
        return _exchange_start("sibling_start_" + tag, bufs + lands, len(bufs), _sibling_copies(len(bufs)))

    def pair_then_chip_start(tag, started, after, names, tiles, dtypes):
        n = len(names)
        got = _exchange_wait("sibling_wait_" + tag, started, after, _sibling_copies(n))
        pairs = [_pair_sum(got[b], got[n + b], kc, "pair_sum_" + names[b], tiles[b], dtypes[b]) for b in range(n)]
        lands = [_landing((3,) + pr[0].shape[1:], pr[0].dtype) for pr in pairs]
        return _exchange_start("chip_start_" + tag, [pr[0] for pr in pairs] + lands, 3 * n, _chip_copies(n)), pairs

    def sum_then_finish_start(tag, started, pairs, after, names, tiles, small, to_all=()):
        n = len(names)
        got = _exchange_wait("chip_wait_" + tag, started, after, _chip_copies(n))
        sums = [_chip_sum(pairs[b][1], got[n + b], kc if small and b == n - 1 else kc[1:],
                          (N_CHIPS, 2) if small and b == n - 1 else (2,), "chip_sum_" + names[b], tiles[b])
                for b in range(n)]
        nbig = n - 1 if small else n
        n_all = n - nbig + len(to_all)
        return _exchange_start("finish_start_" + tag, sums + list(to_all), nbig + 7 * n_all,
                               _finish_copies(nbig, n_all))

    gw_pe, dq, dh1, do, dy, g_post = _head_bwd(dout, gt, p2, o, post_g, wg_out, wg_pg, wg_pe, 256)
    gw_pe = gw_pe.reshape(N_CHIPS, 2, D_PLE // 2, W_PE_COLS)
    token0 = jnp.zeros((SUBLANES, LANES), F32)
    gw_out = _weight_grad(y, do, "grad_w_out", 2, 1, D_MODEL // 2, D_MODEL, 1024, token0)
    gw_pg = _weight_grad(h1, dq, "grad_w_pg", 2, 1, D_MODEL // 2, D_MODEL, 1024, token0)
    gw_out = gw_out.reshape(N_CHIPS, 2, W_ROWS // 2, D_MODEL)
    gw_pg = gw_pg.reshape(N_CHIPS, 2, W_ROWS // 2, D_MODEL)

    names_a, tiles_a = ["w_out", "w_pg", "w_pe"], [128, 128, 128]
    st, tok = sibling_start("a", [gw_out, gw_pg, gw_pe])
    (dz, g_oga, g_ogb, g_lng, g_lnb, g_bsx, g_ws, g_cw, g_cb, g_wa, g_ba, g_wx, g_bx, g_lam) = _branches_bwd(
        z, h, dy, prm, 256, tok)
    (st, tok), pairs_a = pair_then_chip_start("a", st, dz, names_a, tiles_a, [BF16] * 3)
    gw_in = _weight_grad(hn, dz, "grad_w_in", 2, N_CHIPS, D_MODEL // 2, W_IN_COLS, 1024, tok)
    fin_a, tok = sum_then_finish_start("a", st, pairs_a, gw_in, names_a, tiles_a, False)

    small_g = dict(
        gmlp_ln_g=g_lng[0:1], gmlp_ln_b=g_lnb[0:1], gmlp_ws=g_ws,
        gmlp_bs=jnp.transpose(g_bsx[:, ::CHUNK]), conv_w=g_cw[::SUBLANES], conv_b=g_cb[0:1], w_a=g_wa, b_a=g_ba[0:1],
        w_x=g_wx, b_x=g_bx[0:1], lam=g_lam[0:1], gmlp_out_g=g_oga[0:1], lru_out_g=g_ogb[0:1], post_g=g_post[0:1])
    gsm = _pack_small(small_g).reshape(N_CHIPS, 2, SMALL_PIECE, LANES)

    names_b, tiles_b = ["w_in", "small"], [256, SMALL_PIECE]
    n_tiles = x2.shape[0] // 256
    n_lo = max(1, (5 * n_tiles) // 16)
    st, tok_b = _exchange_start(
        "sibling_start_b", [gw_in, gsm] + [_landing((N_CHIPS,) + b.shape[2:], F32) for b in (gw_in, gsm)], 2,
        _sibling_copies(2), after=tok)
    part = _inproj_bwd(dz, wg_in, x2, dh1, pre_g, 256, 0, n_lo, None, False, tok_b, "inproj_bwd_lo")
    f_out, f_pg, f_pe = _exchange_wait("finish_wait_a", fin_a, part[1], _finish_copies(3, 0))
    (st, tok_b), pairs_b = pair_then_chip_start("b", st, part[1], names_b, tiles_b, [BF16, F32])
    grad_x, g_pre = _inproj_bwd(dz, wg_in, x2, dh1, pre_g, 256, n_lo, n_tiles - n_lo, part, True, tok_b,
                                "inproj_bwd_hi")
    pre_parts = _into_slot(g_pre, loss_acc, kc, (N_CHIPS, 2), "pre_g_into_slot")
    fin_b, tok_b = sum_then_finish_start("b", st, pairs_b, g_pre, names_b, tiles_b, True, to_all=[pre_parts])

    grads, deltas, new_m, new_v = {}, {}, {}, {}

    def adam_big(n, g2d, tr, token):
        shp = weights[n].shape
        g, d, nm, nv = _adamw(weights[n][0], g2d, mom_m[n][0], mom_v[n][0], "adamw_" + n, tr, token)
        grads[n], deltas[n], new_m[n], new_v[n] = g.reshape(shp), d.reshape(shp), nm.reshape(shp), nv.reshape(shp)
        return d

    as_token = lambda d: d[:SUBLANES, :LANES]
    last = adam_big("w_out", f_out.reshape(W_ROWS, D_MODEL), 128, tok_b)
    last = adam_big("w_pg", f_pg.reshape(W_ROWS, D_MODEL), 128, as_token(last))
    last = adam_big("w_pe", f_pe.reshape(D_PLE, W_PE_COLS), 128, as_token(last))
    f_in, f_sm, pre_parts = _exchange_wait("finish_wait_b", fin_b, last, _finish_copies(1, 2))
    adam_big("w_in", f_in.reshape(D_MODEL, W_IN_COLS), 256, tok_b)

    packed_g = f_sm.reshape(SMALL_TOTAL, LANES)
    small_names = ["pre_g"] + [n for n, _ in SMALL_ROWS if n != "conv_w"]
    natural = lambda src: {n: (src[n] if src[n].ndim == 2 else src[n][0]) for n in small_names}
    outs, loss_block = _adamw_small(packed_g, pre_parts.reshape(8, D_MODEL // LANES + SUBLANES, LANES),
                                    natural(weights), natural(mom_m), natural(mom_v))
    loss = loss_block[0, 0]
    for dst, got in zip((grads, deltas, new_m, new_v), outs):
        for n in small_names:
            dst[n] = got[n].reshape(weights[n].shape)
    at = sum(r for n, r in SMALL_ROWS[:[n for n, _ in SMALL_ROWS].index("conv_w")])
    g_cw_all = packed_g[at:at + CONV_W * D_HALF // LANES].reshape(CONV_W, D_HALF)
    g_conv = lax.dynamic_slice_in_dim(g_cw_all, me * CONV_COLS, CONV_COLS, axis=1)
    g, d, nm, nv = _adamw(conv_w[0, :, 0, :], g_conv, m_conv_w[0, :, 0, :], v_conv_w[0, :, 0, :], "adamw_conv_w", CONV_W,
                          tok_b)
    cshape = conv_w.shape
    grads["conv_w"], deltas["conv_w"] = g.reshape(cshape), d.reshape(cshape)
    new_m["conv_w"], new_v["conv_w"] = nm.reshape(cshape), nv.reshape(cshape)

    return (loss, grad_x.reshape(x.shape), *[grads[n] for n in order], *[deltas[n] for n in order],
            *[new_m[n] for n in order], *[new_v[n] for n in order])
```

```python
import functools
import math

import jax
import jax.numpy as jnp
from jax import lax
from jax.experimental import pallas as pl
from jax.experimental.pallas import tpu as pltpu

F32 = jnp.float32
BF16 = jnp.bfloat16

D_MODEL = 2048
D_HALF = 1024
D_Z = 5120
D_PLE = 256
CHUNK = 128
N_HEADS = 8
N_CHIPS = 4
W_IN_COLS = D_Z // N_CHIPS
W_ROWS = D_MODEL // N_CHIPS
W_PE_COLS = D_MODEL // N_CHIPS
CONV_W = 4
CONV_COLS = D_HALF // N_CHIPS
EPS = 1e-6
LRU_C = 8.0
ADAM_LR, ADAM_B1, ADAM_B2, ADAM_EPS, ADAM_WD, ADAM_STEP = 0.001, 0.9, 0.999, 1e-08, 0.01, 10

SUBLANES = 8
LANES = 128
VMEM_LIMIT = 56 * 1024 * 1024

SMALL_ROWS = (("gmlp_ln_g", 8), ("gmlp_ln_b", 8), ("gmlp_ws", 1024), ("gmlp_bs", 8),
              ("conv_w", 32), ("conv_b", 8), ("w_a", 1024), ("b_a", 8), ("w_x", 1024), ("b_x", 8),
              ("lam", 8), ("gmlp_out_g", 8), ("lru_out_g", 8), ("post_g", 16))
SMALL_USED = sum(r for _, r in SMALL_ROWS)
SMALL_PIECE = 400
SMALL_TOTAL = 8 * SMALL_PIECE

MESH = pl.DeviceIdType.MESH
ANY = pl.BlockSpec(memory_space=pl.ANY)

_GELU_C0 = math.sqrt(2.0 / math.pi)
_GELU_C1 = 0.044715


def _params(*sem):
    return pltpu.CompilerParams(dimension_semantics=sem, vmem_limit_bytes=VMEM_LIMIT)


def _dot(a, b):
    return jnp.dot(a, b, preferred_element_type=F32)


def _dot_nt(a, b):
    return lax.dot_general(a, b, (((1,), (1,)), ((), ())), preferred_element_type=F32)


def _dot_tn(a, b):
    return lax.dot_general(a, b, (((0,), (0,)), ((), ())), preferred_element_type=F32)


def _gelu(x):
    t = jnp.tanh(_GELU_C0 * (x + _GELU_C1 * (x * x * x)))
    return 0.5 * x * (1.0 + t), t


def _gelu_grad(x, t):
    return 0.5 * (1.0 + t) + 0.5 * x * (1.0 - t * t) * (_GELU_C0 * (1.0 + 3.0 * _GELU_C1 * x * x))


def _rowsum8(v):
    r, n = v.shape
    return jnp.sum(v.reshape(r // SUBLANES, SUBLANES, n), axis=0)


def _lanemean(v):
    return jnp.mean(v, axis=-1, keepdims=True)


def _shift_down(v, halo8, k):
    if k == 0:
        return v
    r = pltpu.roll(v, k, 0)
    hr = pltpu.roll(halo8, k, 0)
    row = lax.broadcasted_iota(jnp.int32, halo8.shape, 0)
    top = jnp.where(row < k, hr, r[0:SUBLANES])
    return jnp.concatenate([top, r[SUBLANES:]], axis=0)


def _shift_up(v, next8, k):
    if k == 0:
        return v
    n = v.shape[0]
    r = pltpu.roll(v, n - k, 0)
    nr = pltpu.roll(next8, SUBLANES - k, 0)
    row = lax.broadcasted_iota(jnp.int32, next8.shape, 0)
    bot = jnp.where(row >= SUBLANES - k, nr, r[n - SUBLANES:])
    return jnp.concatenate([r[:n - SUBLANES], bot], axis=0)


def _layernorm_parts(vg):
    mu = _lanemean(vg)
    xc = vg - mu
    rstd = lax.rsqrt(_lanemean(xc * xc) + EPS)
    return xc * rstd, rstd


def _spatial_mix(wt_ref, vn_ref, bsx_ref, mixed_ref, tm):
    for c in range(tm // CHUNK):
        rows = slice(c * CHUNK, (c + 1) * CHUNK)
        for h in range(N_HEADS):
            cols = slice(h * CHUNK, (h + 1) * CHUNK)
            mixed_ref[rows, cols] = _dot(wt_ref[h], vn_ref[rows, cols]) + bsx_ref[:, cols]


def _conv_taps(xb, halo8):
    return [_shift_down(xb, halo8, CONV_W - 1 - k) for k in range(CONV_W)]


def _lru_gates(xc_bf_ref, wa_ref, wx_ref, ba_ref, bx_ref, r_ref, i_ref):
    for h in range(N_HEADS):
        cols = slice(h * CHUNK, (h + 1) * CHUNK)
        xh = xc_bf_ref[:, cols]
        r_ref[:, cols] = jax.nn.sigmoid(_dot(xh, wa_ref[h]) + ba_ref[:, cols])
        i_ref[:, cols] = jax.nn.sigmoid(_dot(xh, wx_ref[h]) + bx_ref[:, cols])


def _softplus_neg(lam):
    return jnp.maximum(-lam, 0.0) + jnp.log(1.0 + jnp.exp(-jnp.abs(lam)))


def _decay_parts(r, lam):
    la = (-LRU_C * _softplus_neg(lam)) * r
    a = jnp.exp(la)
    th = -jnp.tanh(la)
    mult = jnp.sqrt(2.0 * th / (1.0 + th))
    return a, mult


def _z_group(zref, g, rows=slice(None)):
    lo = g * D_HALF
    blk, off = lo // W_IN_COLS, lo % W_IN_COLS
    if off + D_HALF <= W_IN_COLS:
        return zref[blk, rows, off:off + D_HALF]
    return jnp.concatenate([zref[blk, rows, off:W_IN_COLS], zref[blk + 1, rows, 0:off + D_HALF - W_IN_COLS]], axis=1)


def _inproj_local(x, pre_g, w_own, tm, token):
    t = x.shape[0]

    def body(x_ref, g_ref, w_ref, token_ref, hn_ref, zl_ref, wbf_s):
        @pl.when(pl.program_id(0) == 0)
        def _():
            wbf_s[...] = w_ref[...].astype(BF16)

        xv = x_ref[...]
        hn = (xv * lax.rsqrt(_lanemean(xv * xv) + EPS) * g_ref[...]).astype(BF16)
        hn_ref[...] = hn
        zl_ref[...] = _dot(hn, wbf_s[...]).astype(BF16)

    row = lambda n: pl.BlockSpec((tm, n), lambda i: (i, 0))
    const = lambda shp: pl.BlockSpec(shp, lambda i: (0, 0), pipeline_mode=pl.Buffered(1))
    return pl.pallas_call(
        body, name="inproj_local", grid=(t // tm,),
        in_specs=[row(D_MODEL), const((1, D_MODEL)), const((D_MODEL, W_IN_COLS)), const((SUBLANES, LANES))],
        out_specs=[row(D_MODEL), row(W_IN_COLS)],
        out_shape=[jax.ShapeDtypeStruct((t, D_MODEL), BF16), jax.ShapeDtypeStruct((t, W_IN_COLS), BF16)],
        scratch_shapes=[pltpu.VMEM((D_MODEL, W_IN_COLS), BF16)],
        compiler_params=_params("arbitrary"),
    )(x, pre_g, w_own, token)


def _inproj_branches_fwd(hn, z_own, wg_in, kc, prm, tm, token):
    t = hn.shape[0]
    nt = t // tm
    hb = tm // SUBLANES

    def body(kc_ref, hn_ref, zo_ref, w1_ref, w2_ref, w3_ref,
             lng_ref, lnb_ref, wt_ref, bsx_ref, cw_ref, cb_ref, wa_ref, wx_ref, ba_ref, bx_ref, lam_ref,
             oga_ref, ogb_ref, token_ref,
             z_ref, y_ref, h_ref,
             zbuf0, zbuf1, vn_s, mixed_s, xcbf_s, r_s, i_s, ug_s, halo_s, carry_s):
        s = pl.program_id(0)
        me = kc_ref[0]
        w_refs = (None, w1_ref, w2_ref, w3_ref)

        @pl.when(s == 0)
        def _():
            zbuf1[...] = jnp.zeros_like(zbuf1)

        @pl.when(s <= 1)
        def _():
            carry_s[...] = jnp.zeros_like(carry_s)
            halo_s[...] = jnp.zeros_like(halo_s)

        def step(zw, zr):
            def project(r):
                blk = (me + r) % N_CHIPS
                zb = zo_ref[...] if r == 0 else _dot(hn_ref[...], w_refs[r][...]).astype(BF16)
                z_ref[blk] = zb
                zw[blk] = zb

            zin = lambda g: _z_group(zr, g).astype(F32)
            always = [s >= 0] * 4

            @pl.when(always[0])
            def _():
                project(0)
                ug, _ = _gelu(zin(0))
                ug_s[...] = ug
                vg, _ = _gelu(zin(1))
                vhat, _ = _layernorm_parts(vg)
                vn_s[...] = (vhat * lng_ref[...] + lnb_ref[...]).astype(BF16)

            @pl.when(always[1])
            def _():
                project(1)
                _spatial_mix(wt_ref, vn_s, bsx_ref, mixed_s, tm)
                ga = zin(2)
                ya = ug_s[...] * mixed_s[...] * (ga * jax.nn.sigmoid(ga))
                ra = lax.rsqrt(_lanemean(ya * ya) + EPS)
                y_ref[:, 0:D_HALF] = (ya * ra * oga_ref[...]).astype(BF16)

            @pl.when(always[2])
            def _():
                project(2)
                xb = zin(3)
                taps = _conv_taps(xb, halo_s[...])
                halo_s[...] = xb[tm - SUBLANES:]
                xc = cb_ref[...] + taps[0] * cw_ref[0:1, :]
                for k in range(1, CONV_W):
                    xc = xc + taps[k] * cw_ref[k:k + 1, :]
                xcbf_s[...] = xc.astype(BF16)
                _lru_gates(xcbf_s, wa_ref, wx_ref, ba_ref, bx_ref, r_s, i_s)
                a, mult = _decay_parts(r_s[...], lam_ref[...])
                row = lax.broadcasted_iota(jnp.int32, a.shape, 0)
                mult = jnp.where(jnp.logical_and(s == 1, row == 0), 1.0, mult)
                r_s[...] = a
                i_s[...] = mult * (i_s[...] * xc)

            @pl.when(always[3])
            def _():
                project(3)
                a = r_s[...]
                b = i_s[...]
                r8 = lax.broadcasted_iota(jnp.int32, a.shape, 0) & (SUBLANES - 1)
                for d in (1, 2, 4):
                    a_sh = pltpu.roll(a, d, 0)
                    b_sh = pltpu.roll(b, d, 0)
                    m = r8 >= d
                    b = jnp.where(m, a * b_sh + b, b)
                    a = jnp.where(m, a * a_sh, a)
                carry = carry_s[...]
                for g in range(hb):
                    rows = slice(g * SUBLANES, (g + 1) * SUBLANES)
                    hg = a[rows] * carry + b[rows]
                    h_ref[rows, :] = hg
                    carry = jnp.broadcast_to(hg[SUBLANES - 1:SUBLANES, :], hg.shape)
                carry_s[...] = carry
                gb = zin(4)
                yb = h_ref[...] * (gb * jax.nn.sigmoid(gb))
                rb = lax.rsqrt(_lanemean(yb * yb) + EPS)
                y_ref[:, D_HALF:] = (yb * rb * ogb_ref[...]).astype(BF16)

        @pl.when(s % 2 == 0)
        def _():
            step(zbuf0, zbuf1)

        @pl.when(s % 2 == 1)
        def _():
            step(zbuf1, zbuf0)

    const = lambda a: pl.BlockSpec(a.shape, lambda s, kc, n=a.ndim: (0,) * n, pipeline_mode=pl.Buffered(1))
    proj = lambda n: pl.BlockSpec((tm, n), lambda s, kc: (jnp.minimum(s, nt - 1), 0))
    head = lambda n: pl.BlockSpec((tm, n), lambda s, kc: (jnp.maximum(s - 1, 0), 0))
    other = lambda r: pl.BlockSpec((None, D_MODEL, W_IN_COLS), lambda s, kc, r=r: ((kc[0] + r) % N_CHIPS, 0, 0),
                                   pipeline_mode=pl.Buffered(1))
    names = ("ln_g", "ln_b", "wt", "bsx", "conv_w", "conv_b", "w_a", "w_x", "b_a", "b_x", "lam", "oga", "ogb")
    pr = [prm[n] for n in names] + [token]
    big = lambda dt: pltpu.VMEM((tm, D_HALF), dt)
    zblocks = pltpu.VMEM((N_CHIPS, tm, W_IN_COLS), BF16)
    grid_spec = pltpu.PrefetchScalarGridSpec(
        num_scalar_prefetch=1, grid=(nt + 1,),
        in_specs=[proj(D_MODEL), proj(W_IN_COLS), other(1), other(2), other(3)] + [const(a) for a in pr],
        out_specs=[pl.BlockSpec((N_CHIPS, tm, W_IN_COLS), lambda s, kc: (0, jnp.minimum(s, nt - 1), 0)),
                   head(D_MODEL), head(D_HALF)],
        scratch_shapes=[zblocks, zblocks, big(BF16), big(F32), big(BF16), big(F32), big(F32), big(F32),
                        pltpu.VMEM((SUBLANES, D_HALF), F32), pltpu.VMEM((SUBLANES, D_HALF), F32)])
    return pl.pallas_call(
        body, name="inproj_branches_fwd", grid_spec=grid_spec,
        out_shape=[jax.ShapeDtypeStruct((N_CHIPS, t, W_IN_COLS), BF16), jax.ShapeDtypeStruct((t, D_MODEL), BF16),
                   jax.ShapeDtypeStruct((t, D_HALF), F32)],
        compiler_params=_params("arbitrary"),
    )(kc, hn, z_own, wg_in, wg_in, wg_in, *pr)


def _outproj_fwd(x, y, p, tgt, post_g, w_out, w_pg, wg_pe, tm):
    t = x.shape[0]

    def body(x_ref, y_ref, p_ref, tgt_ref, pg_ref, wo_ref, wpg_ref, wpe_ref,
             o_ref, h1_ref, gt_ref, dout_ref, loss_ref):
        @pl.when(pl.program_id(0) == 0)
        def _():
            loss_ref[...] = jnp.zeros_like(loss_ref)

        o = _dot(y_ref[...], wo_ref[...])
        o_ref[...] = o
        r3 = lax.rsqrt(_lanemean(o * o) + EPS)
        h1 = x_ref[...] + (o * r3) * pg_ref[...]
        h1b = h1.astype(BF16)
        h1_ref[...] = h1b
        gt = jax.nn.sigmoid(_dot(h1b, wpg_ref[...]))
        gt_ref[...] = gt
        pb = p_ref[...].astype(BF16)
        for k in range(N_CHIPS):
            cols = slice(k * W_PE_COLS, (k + 1) * W_PE_COLS)
            pe = _dot(pb, wpe_ref[k])
            d = h1[:, cols] + pe * gt[:, cols] - tgt_ref[:, cols]
            dout_ref[:, cols] = d * (1.0 / D_MODEL)
            loss_ref[...] += jnp.sum(d * d) * (0.5 / D_MODEL)

    row = lambda n: pl.BlockSpec((tm, n), lambda i: (i, 0))
    const = lambda shp: pl.BlockSpec(shp, lambda i, n=len(shp): (0,) * n, pipeline_mode=pl.Buffered(1))
    return pl.pallas_call(
        body, name="outproj_fwd", grid=(t // tm,),
        in_specs=[row(D_MODEL), row(D_MODEL), row(D_PLE), row(D_MODEL), const((1, D_MODEL)),
                  const((D_MODEL, D_MODEL)), const((D_MODEL, D_MODEL)), const((N_CHIPS, D_PLE, W_PE_COLS))],
        out_specs=[row(D_MODEL), row(D_MODEL), row(D_MODEL), row(D_MODEL),
                   pl.BlockSpec((SUBLANES, LANES), lambda i: (0, 0))],
        out_shape=[jax.ShapeDtypeStruct((t, D_MODEL), F32), jax.ShapeDtypeStruct((t, D_MODEL), BF16),
                   jax.ShapeDtypeStruct((t, D_MODEL), F32), jax.ShapeDtypeStruct((t, D_MODEL), F32),
                   jax.ShapeDtypeStruct((SUBLANES, LANES), F32)],
        compiler_params=_params("arbitrary"),
    )(x, y, p, tgt, post_g, w_out, w_pg, wg_pe)


def _head_bwd(dout, gt, p, o, post_g, w_out, w_pg, wg_pe, tm):
    t = dout.shape[0]

    def body(dout_ref, gt_ref, p_ref, o_ref, pg_ref, wo_ref, wpg_ref, wpe_ref,
             gwpe_ref, dq_ref, dh1_ref, do_ref, dy_ref, gpost_ref):
        i = pl.program_id(0)

        @pl.when(i == 0)
        def _():
            gpost_ref[...] = jnp.zeros_like(gpost_ref)
            gwpe_ref[...] = jnp.zeros_like(gwpe_ref)

        dout = dout_ref[...]
        gt = gt_ref[...]
        pb = p_ref[...].astype(BF16)
        for k in range(N_CHIPS):
            cols = slice(k * W_PE_COLS, (k + 1) * W_PE_COLS)
            pe = _dot(pb, wpe_ref[k])
            g = gt[:, cols]
            dg = dout[:, cols] * g
            gwpe_ref[k] += _dot_tn(pb, dg.astype(BF16))
            dq_ref[:, cols] = (dg * pe * (1.0 - g)).astype(BF16)
        dh1 = dout + _dot_nt(dq_ref[...], wpg_ref[...])
        dh1_ref[...] = dh1
        o = o_ref[...]
        r3 = lax.rsqrt(_lanemean(o * o) + EPS)
        on = o * r3
        gpost_ref[...] += _rowsum8(dh1 * on)
        don = dh1 * pg_ref[...]
        do = r3 * (don - on * _lanemean(don * on))
        dob = do.astype(BF16)
        do_ref[...] = dob
        dy_ref[...] = _dot_nt(dob, wo_ref[...])

        @pl.when(i == pl.num_programs(0) - 1)
        def _():
            gpost_ref[...] = jnp.broadcast_to(jnp.sum(gpost_ref[...], axis=0, keepdims=True), gpost_ref.shape)

    row = lambda n: pl.BlockSpec((tm, n), lambda i: (i, 0))
    const = lambda shp: pl.BlockSpec(shp, lambda i, n=len(shp): (0,) * n, pipeline_mode=pl.Buffered(1))
    return pl.pallas_call(
        body, name="head_bwd", grid=(t // tm,),
        in_specs=[row(D_MODEL), row(D_MODEL), row(D_PLE), row(D_MODEL), const((1, D_MODEL)),
                  const((D_MODEL, D_MODEL)), const((D_MODEL, D_MODEL)), const((N_CHIPS, D_PLE, W_PE_COLS))],
        out_specs=[pl.BlockSpec((N_CHIPS, D_PLE, W_PE_COLS), lambda i: (0, 0, 0)),
                   row(D_MODEL), row(D_MODEL), row(D_MODEL), row(D_MODEL),
                   pl.BlockSpec((SUBLANES, D_MODEL), lambda i: (0, 0))],
        out_shape=[jax.ShapeDtypeStruct((N_CHIPS, D_PLE, W_PE_COLS), F32), jax.ShapeDtypeStruct((t, D_MODEL), BF16),
                   jax.ShapeDtypeStruct((t, D_MODEL), F32), jax.ShapeDtypeStruct((t, D_MODEL), BF16),
                   jax.ShapeDtypeStruct((t, D_MODEL), F32), jax.ShapeDtypeStruct((SUBLANES, D_MODEL), F32)],
        compiler_params=_params("arbitrary"),
    )(dout, gt, p, o, post_g, w_out, w_pg, wg_pe)


def _branches_bwd(z, h, dy, prm, tm, token):
    t = h.shape[0]
    nt = t // tm
    hb = tm // SUBLANES

    def body(z_ref, zh_ref, h_ref, hh_ref, dy_ref,
             lng_ref, lnb_ref, wt_ref, wtt_ref, bsx_ref, cw_ref, cb_ref, wa_ref, wx_ref, ba_ref, bx_ref, lam_ref,
             oga_ref, ogb_ref, token_ref,
             dz_ref, g_oga, g_ogb, g_lng, g_lnb, g_bsx, g_ws, g_cw, g_cb, g_wa, g_ba, g_wx, g_bx, g_lam,
             vn_s, mixed_s, dm_s, dvn_s, xcbf_s, r_s, i_s, a_s, b_s, dh_s, dpr_s, dpi_s, dxc_s,
             ca_s, cd_s, cx_s):
        step_i = pl.program_id(0)
        tile = nt - 1 - step_i
        accs = (g_oga, g_ogb, g_lng, g_lnb, g_bsx, g_ws, g_cw, g_cb, g_wa, g_ba, g_wx, g_bx, g_lam)

        @pl.when(step_i == 0)
        def _():
            for r in accs + (ca_s, cd_s, cx_s):
                r[...] = jnp.zeros_like(r)

        dy_a = dy_ref[:, 0:D_HALF]
        dy_b = dy_ref[:, D_HALF:]

        u = _z_group(z_ref, 0).astype(F32)
        ug, tu = _gelu(u)
        v = _z_group(z_ref, 1).astype(F32)
        vg, tv = _gelu(v)
        vhat, rstd = _layernorm_parts(vg)
        vn_s[...] = (vhat * lng_ref[...] + lnb_ref[...]).astype(BF16)
        _spatial_mix(wt_ref, vn_s, bsx_ref, mixed_s, tm)
        mixed = mixed_s[...]
        ga = _z_group(z_ref, 2).astype(F32)
        sga = jax.nn.sigmoid(ga)
        sa = ga * sga
        um = ug * mixed
        ya = um * sa
        ra = lax.rsqrt(_lanemean(ya * ya) + EPS)
        yahat = ya * ra
        g_oga[...] += _rowsum8(dy_a * yahat)
        dn = dy_a * oga_ref[...]
        dya = ra * (dn - yahat * _lanemean(dn * yahat))
        dz_ref[:, 2 * D_HALF:3 * D_HALF] = (dya * um * (sga * (1.0 + ga * (1.0 - sga)))).astype(BF16)
        dz_ref[:, 0:D_HALF] = (dya * mixed * sa * _gelu_grad(u, tu)).astype(BF16)
        dmixed = dya * ug * sa
        g_bsx[...] += jnp.sum(dmixed.reshape(tm // CHUNK, CHUNK, D_HALF), axis=0)
        dm_s[...] = dmixed.astype(BF16)
        for c in range(tm // CHUNK):
            rows = slice(c * CHUNK, (c + 1) * CHUNK)
            for hd in range(N_HEADS):
                cols = slice(hd * CHUNK, (hd + 1) * CHUNK)
                dmh = dm_s[rows, cols]
                dvn_s[rows, cols] = _dot(wtt_ref[hd], dmh)
                g_ws[hd] += _dot_nt(dmh, vn_s[rows, cols])
        dvn = dvn_s[...]
        g_lng[...] += _rowsum8(dvn * vhat)
        g_lnb[...] += _rowsum8(dvn)
        dvh = dvn * lng_ref[...]
        dvg = rstd * (dvh - _lanemean(dvh) - vhat * _lanemean(dvh * vhat))
        dz_ref[:, D_HALF:2 * D_HALF] = (dvg * _gelu_grad(v, tv)).astype(BF16)

        xb = _z_group(z_ref, 3).astype(F32)
        halo = jnp.where(tile == 0, 0.0, _z_group(zh_ref, 3).astype(F32)[SUBLANES:])
        taps = _conv_taps(xb, halo)
        xc = cb_ref[...] + taps[0] * cw_ref[0:1, :]
        for k in range(1, CONV_W):
            xc = xc + taps[k] * cw_ref[k:k + 1, :]
        xcbf_s[...] = xc.astype(BF16)
        _lru_gates(xcbf_s, wa_ref, wx_ref, ba_ref, bx_ref, r_s, i_s)
        rg = r_s[...]
        ig = i_s[...]
        lam = lam_ref[...]
        a, mult_true = _decay_parts(rg, lam)
        row = lax.broadcasted_iota(jnp.int32, a.shape, 0)
        first = jnp.logical_and(tile == 0, row == 0)
        mult = jnp.where(first, 1.0, mult_true)
        hcur = h_ref[...]
        hprev = _shift_down(hcur, jnp.where(tile == 0, 0.0, hh_ref[...]), 1)
        gb = _z_group(z_ref, 4).astype(F32)
        sgb = jax.nn.sigmoid(gb)
        sb = gb * sgb
        yb = hcur * sb
        rb = lax.rsqrt(_lanemean(yb * yb) + EPS)
        ybhat = yb * rb
        g_ogb[...] += _rowsum8(dy_b * ybhat)
        dn = dy_b * ogb_ref[...]
        dyb = rb * (dn - ybhat * _lanemean(dn * ybhat))
        dz_ref[:, 4 * D_HALF:5 * D_HALF] = (dyb * hcur * (sgb * (1.0 + gb * (1.0 - sgb)))).astype(BF16)

        an = _shift_up(a, ca_s[...], 1)
        bb = dyb * sb
        r8 = row & (SUBLANES - 1)
        for d in (1, 2, 4):
            a_sh = pltpu.roll(an, tm - d, 0)
            b_sh = pltpu.roll(bb, tm - d, 0)
            m = r8 + d < SUBLANES
            bb = jnp.where(m, an * b_sh + bb, bb)
            an = jnp.where(m, an * a_sh, an)
        a_s[...] = an
        b_s[...] = bb

        def step(g, carry):
            sl = pl.ds(pl.multiple_of((hb - 1 - g) * SUBLANES, SUBLANES), SUBLANES)
            dg = a_s[sl, :] * carry + b_s[sl, :]
            dh_s[sl, :] = dg
            return jnp.broadcast_to(dg[0:1, :], dg.shape)

        cd_s[...] = lax.fori_loop(0, hb, step, cd_s[...])
        ca_s[...] = jnp.broadcast_to(a[0:1, :], ca_s.shape)
        dh = dh_s[...]
        da = dh * hprev
        gx = ig * xc
        dla = da * a - jnp.where(first, 0.0, dh * gx * (a * a / mult_true))
        g_lam[...] += _rowsum8(dla * rg)
        dr = dla * (-LRU_C * _softplus_neg(lam))
        dpr = dr * rg * (1.0 - rg)
        dpi = (dh * mult * xc) * ig * (1.0 - ig)
        g_ba[...] += _rowsum8(dpr)
        g_bx[...] += _rowsum8(dpi)
        dpr_s[...] = dpr.astype(BF16)
        dpi_s[...] = dpi.astype(BF16)
        for hd in range(N_HEADS):
            cols = slice(hd * CHUNK, (hd + 1) * CHUNK)
            xh = xcbf_s[:, cols]
            dprh = dpr_s[:, cols]
            dpih = dpi_s[:, cols]
            g_wa[hd] += _dot_tn(xh, dprh)
            g_wx[hd] += _dot_tn(xh, dpih)
            dxc_s[:, cols] = _dot_nt(dprh, wa_ref[hd]) + _dot_nt(dpih, wx_ref[hd])
        dxc = dxc_s[...] + dh * mult * ig
        g_cb[...] += _rowsum8(dxc)
        for k in range(CONV_W):
            g_cw[k * SUBLANES:(k + 1) * SUBLANES, :] += _rowsum8(dxc * taps[k])
        nxt = cx_s[...]
        dxb = dxc * cw_ref[CONV_W - 1:CONV_W, :]
        for j in range(1, CONV_W):
            dxb = dxb + _shift_up(dxc, nxt, j) * cw_ref[CONV_W - 1 - j:CONV_W - j, :]
        dz_ref[:, 3 * D_HALF:4 * D_HALF] = dxb.astype(BF16)
        cx_s[...] = dxc[0:SUBLANES]

        @pl.when(step_i == nt - 1)
        def _():
            for r in (g_oga, g_ogb, g_lng, g_lnb, g_cb, g_ba, g_bx):
                r[...] = jnp.broadcast_to(jnp.sum(r[...], axis=0, keepdims=True), r.shape)
            lam_f = LRU_C * jax.nn.sigmoid(-lam_ref[...])
            g_lam[...] = jnp.broadcast_to(jnp.sum(g_lam[...], axis=0, keepdims=True) * lam_f, g_lam.shape)
            for k in range(CONV_W):
                blk = g_cw[k * SUBLANES:(k + 1) * SUBLANES, :]
                g_cw[k * SUBLANES:(k + 1) * SUBLANES, :] = jnp.broadcast_to(jnp.sum(blk, axis=0, keepdims=True), blk.shape)
            tri = (lax.broadcasted_iota(jnp.int32, (CHUNK, CHUNK), 0) >= lax.broadcasted_iota(jnp.int32, (CHUNK, CHUNK), 1))
            for hd in range(N_HEADS):
                cols = slice(hd * CHUNK, (hd + 1) * CHUNK)
                g_ws[hd] = jnp.where(tri, g_ws[hd], 0.0)
                blk = g_bsx[:, cols]
                g_bsx[:, cols] = jnp.broadcast_to(jnp.sum(blk, axis=1, keepdims=True), blk.shape)

    rev = lambda i: nt - 1 - i
    zspec = pl.BlockSpec((N_CHIPS, tm, W_IN_COLS), lambda i: (0, rev(i), 0))
    halo = lambda col: pl.BlockSpec((SUBLANES, D_HALF), lambda i: (jnp.maximum(rev(i) * hb - 1, 0), col))
    zhalo = pl.BlockSpec((N_CHIPS, 2 * SUBLANES, W_IN_COLS), lambda i: (0, jnp.maximum(rev(i) * (hb // 2) - 1, 0), 0))
    full = lambda a: pl.BlockSpec(a.shape, lambda i, n=a.ndim: (0,) * n)
    acc = lambda shp: pl.BlockSpec(shp, lambda i, n=len(shp): (0,) * n)
    names = ("ln_g", "ln_b", "wt", "wtt", "bsx", "conv_w", "conv_b", "w_a", "w_x", "b_a", "b_x", "lam", "oga", "ogb")
    pr = [prm[n] for n in names] + [token]
    vec = (SUBLANES, D_HALF)
    mat = (N_HEADS, CHUNK, CHUNK)
    acc_shapes = [vec, vec, vec, vec, (CHUNK, D_HALF), mat, (CONV_W * SUBLANES, D_HALF), vec, mat, vec, mat, vec, vec]
    big = lambda dt: pltpu.VMEM((tm, D_HALF), dt)
    return pl.pallas_call(
        body, name="branches_bwd", grid=(nt,),
        in_specs=[zspec, zhalo,
                  pl.BlockSpec((tm, D_HALF), lambda i: (rev(i), 0)), halo(0),
                  pl.BlockSpec((tm, D_MODEL), lambda i: (rev(i), 0))] + [full(a) for a in pr],
        out_specs=[pl.BlockSpec((tm, D_Z), lambda i: (rev(i), 0))] + [acc(s) for s in acc_shapes],
        out_shape=[jax.ShapeDtypeStruct((t, D_Z), BF16)] + [jax.ShapeDtypeStruct(s, F32) for s in acc_shapes],
        scratch_shapes=[big(BF16), big(F32), big(BF16), big(F32), big(BF16), big(F32), big(F32), big(F32), big(F32),
                        big(F32), big(BF16), big(BF16), big(F32),
                        pltpu.VMEM(vec, F32), pltpu.VMEM(vec, F32), pltpu.VMEM(vec, F32)],
        compiler_params=_params("arbitrary"),
    )(z, z, h, h, dy, *pr)


def _inproj_bwd(dz, wg_in, x, dh1, pre_g, tm, tile0, nt, prev, last, token, name):
    t = x.shape[0]

    def body(*refs):
        dz_ref, w_ref, x_ref, dh1_ref, g_ref = refs[:5]
        gx_ref, gpre_ref, acc_s = refs[-3:]
        i = pl.program_id(0)

        @pl.when(i == 0)
        def _():
            gpre_ref[...] = jnp.zeros_like(gpre_ref) if prev is None else refs[7][...]

        acc = _dot_nt(dz_ref[:, 0:W_IN_COLS], w_ref[0])
        for k in range(1, N_CHIPS):
            acc = acc + _dot_nt(dz_ref[:, k * W_IN_COLS:(k + 1) * W_IN_COLS], w_ref[k])
        acc_s[...] = acc
        for s in range(tm // CHUNK):
            rows = slice(s * CHUNK, (s + 1) * CHUNK)
            xv = x_ref[rows, :]
            r = lax.rsqrt(_lanemean(xv * xv) + EPS)
            xhat = xv * r
            dhn = acc_s[rows, :]
            gpre_ref[...] += _rowsum8(dhn * xhat)
            dxh = dhn * g_ref[...]
            gx_ref[rows, :] = dh1_ref[rows, :] + r * (dxh - xhat * _lanemean(dxh * xhat))

        if last:
            @pl.when(i == nt - 1)
            def _():
                gpre_ref[...] = jnp.broadcast_to(jnp.sum(gpre_ref[...], axis=0, keepdims=True), gpre_ref.shape)

    row = lambda n: pl.BlockSpec((tm, n), lambda i: (tile0 + i, 0))
    small = lambda r: pl.BlockSpec((r, D_MODEL), lambda i: (0, 0))
    tok = pl.BlockSpec((SUBLANES, LANES), lambda i: (0, 0))
    in_specs = [row(D_Z), pl.BlockSpec(wg_in.shape, lambda i: (0, 0, 0), pipeline_mode=pl.Buffered(1)),
                row(D_MODEL), row(D_MODEL), small(1), tok]
    args = [dz, wg_in, x, dh1, pre_g, token]
    aliases = {}
    if prev is not None:
        in_specs += [ANY, small(SUBLANES)]
        args += list(prev)
        aliases = {6: 0}
    return pl.pallas_call(
        body, name=name, grid=(nt,), in_specs=in_specs, out_specs=[row(D_MODEL), small(SUBLANES)],
        out_shape=[jax.ShapeDtypeStruct((t, D_MODEL), F32), jax.ShapeDtypeStruct((SUBLANES, D_MODEL), F32)],
        input_output_aliases=aliases,
        scratch_shapes=[pltpu.VMEM((tm, D_MODEL), F32)],
        compiler_params=_params("arbitrary"),
    )(*args)


def _weight_grad(a, b, name, kb, nb, tk, tn, tt, token):
    t = a.shape[0]
    tt = min(tt, t)

    def body(a_ref, b_ref, token_ref, o_ref):
        @pl.when(pl.program_id(2) == 0)
        def _():
            o_ref[...] = jnp.zeros_like(o_ref)

        o_ref[...] += _dot_tn(a_ref[...], b_ref[...])

    return pl.pallas_call(
        body, name=name, grid=(nb, kb, t // tt),
        in_specs=[pl.BlockSpec((tt, tk), lambda j, i, s: (s, i)), pl.BlockSpec((tt, tn), lambda j, i, s: (s, j)),
                  pl.BlockSpec((SUBLANES, LANES), lambda j, i, s: (0, 0))],
        out_specs=pl.BlockSpec((None, None, tk, tn), lambda j, i, s: (j, i, 0, 0)),
        out_shape=jax.ShapeDtypeStruct((nb, kb, tk, tn), F32),
        compiler_params=_params("parallel", "parallel", "arbitrary"),
    )(a, b, token)


def _place():
    x, y, c = lax.axis_index("x"), lax.axis_index("y"), lax.axis_index("c")
    return x, y, c


def _chip_of(x, y):
    return 2 * x + y


def _gather_weights(w_in, conv_w):
    halves = [(D_MODEL // 2, W_IN_COLS)]
    nw = len(halves)

    def body(win_ref, cw_ref, gin_ref, gcw_ref, s0, b0, lsem, send_sems, recv_sems, cw_send, cw_recv):
        x, y, c = _place()
        me = _chip_of(x, y)
        sibling = (x, y, 1 - c)
        chips = [(1 - x, y), (x, 1 - y), (1 - x, 1 - y)]
        srcs = (win_ref,)
        stage = (s0,)
        bf = (b0,)
        outs = (gin_ref,)
        loads = []
        for n in range(nw):
            rows = halves[n][0]
            cp = pltpu.make_async_copy(srcs[n].at[pl.ds(c * rows, rows), :], stage[n], lsem.at[n])
            cp.start()
            loads.append(cp)
        own_cw = pltpu.make_async_copy(cw_ref, gcw_ref.at[me], lsem.at[2 * nw])
        own_cw.start()
        for n in range(nw):
            loads[n].wait()
            bf[n][...] = stage[n][...].astype(BF16)

        def copy(n, k, chip, to, src=None):
            dst = outs[n].at[chip, c]
            return pltpu.make_async_remote_copy(
                src_ref=dst if src is None else src, dst_ref=dst,
                send_sem=send_sems.at[n, k], recv_sem=recv_sems.at[n, k], device_id=to, device_id_type=MESH)

        def recv(n, k, chip, core):
            dst = outs[n].at[chip, core]
            return pltpu.make_async_remote_copy(
                src_ref=dst, dst_ref=dst, send_sem=send_sems.at[n, k], recv_sem=recv_sems.at[n, k],
                device_id=sibling, device_id_type=MESH)

        sends = []
        locals_ = []
        for n in range(nw):
            lc = pltpu.make_async_copy(bf[n], outs[n].at[me, c], lsem.at[nw + n])
            lc.start()
            locals_.append(lc)
            first = [copy(n, 0, me, sibling, src=bf[n])]
            first += [copy(n, 1 + j, me, (*chip, c), src=bf[n]) for j, chip in enumerate(chips)]
            for cp in first:
                cp.start()
            sends += first
        cws = []
        for j, chip in enumerate(chips):
            cp = pltpu.make_async_remote_copy(
                src_ref=cw_ref, dst_ref=gcw_ref.at[me], send_sem=cw_send.at[j], recv_sem=cw_recv.at[j],
                device_id=(*chip, c), device_id_type=MESH)
            cp.start()
            cws.append(cp)
        for n in range(nw):
            for j, chip in enumerate(chips):
                kj = _chip_of(*chip)
                recv(n, 1 + j, kj, c).wait_recv()
                fw = copy(n, 4 + j, kj, sibling)
                fw.start()
                sends.append(fw)
        for n in range(nw):
            recv(n, 0, me, 1 - c).wait_recv()
            for j, chip in enumerate(chips):
                recv(n, 4 + j, _chip_of(*chip), 1 - c).wait_recv()
        for j, chip in enumerate(chips):
            pltpu.make_async_remote_copy(
                src_ref=cw_ref, dst_ref=gcw_ref.at[_chip_of(*chip)], send_sem=cw_send.at[j], recv_sem=cw_recv.at[j],
                device_id=(*chip, c), device_id_type=MESH).wait_recv()
        for cp in sends + cws:
            cp.wait_send()
        for lc in locals_:
            lc.wait()
        own_cw.wait()

    out_shape = [jax.ShapeDtypeStruct((N_CHIPS, 2) + hs, BF16) for hs in halves]
    out_shape.append(jax.ShapeDtypeStruct((N_CHIPS, CONV_W, CONV_COLS), F32))
    scratch = [pltpu.VMEM(hs, F32) for hs in halves] + [pltpu.VMEM(hs, BF16) for hs in halves]
    scratch += [pltpu.SemaphoreType.DMA((2 * nw + 1,)), pltpu.SemaphoreType.DMA((nw, 7)),
                pltpu.SemaphoreType.DMA((nw, 7)), pltpu.SemaphoreType.DMA((3,)), pltpu.SemaphoreType.DMA((3,))]
    return pl.pallas_call(
        body, name="gather_w_in", in_specs=[ANY] * 2, out_specs=[ANY] * 2, out_shape=out_shape,
        scratch_shapes=scratch, compiler_params=pltpu.CompilerParams(vmem_limit_bytes=VMEM_LIMIT),
    )(w_in, conv_w)


HBM = pl.BlockSpec(memory_space=pltpu.HBM)
SEM = pl.BlockSpec(memory_space=pltpu.SEMAPHORE)
EFFECT = pltpu.SideEffectType.DATAFLOW_SIDE_EFFECTING


def _hbm(a):
    return pltpu.with_memory_space_constraint(a, pltpu.HBM)


def _landing(shape, dtype):
    return _hbm(lax.empty(shape, dtype))


def _exchange_start(name, arrays, ncopies, build, after=None):
    n = len(arrays)
    extra = [] if after is None else [after]

    def body(*refs):
        ins, token = refs[:n], refs[-1]
        send_sems, recv_sems = refs[n + len(extra)], refs[n + len(extra) + 1]
        for cp in build(ins, send_sems, recv_sems):
            cp.start()
        token[...] = jnp.zeros_like(token)

    outs = pl.pallas_call(
        body, name=name,
        out_shape=(pltpu.SemaphoreType.DMA((ncopies,)), pltpu.SemaphoreType.DMA((ncopies,)),
                   *[pltpu.HBM(a.shape, a.dtype) for a in arrays], jax.ShapeDtypeStruct((SUBLANES, LANES), F32)),
        in_specs=[HBM] * n + [ANY] * len(extra),
        out_specs=(SEM, SEM, *[HBM] * n, pl.BlockSpec(memory_space=pltpu.VMEM)),
        input_output_aliases={q: q + 2 for q in range(n)},
        compiler_params=pltpu.CompilerParams(has_side_effects=EFFECT),
    )(*[_hbm(a) for a in arrays], *extra)
    return (outs[0], outs[1], list(outs[2:2 + n])), outs[-1]


def _exchange_wait(name, started, after, build):
    send, recv, arrays = started
    n = len(arrays)

    def body(*refs):
        ins, send_sems, recv_sems = refs[:n], refs[n], refs[n + 1]
        for cp in build(ins, send_sems, recv_sems):
            cp.wait_send()
            cp.wait_recv()

    return pl.pallas_call(
        body, name=name, out_shape=tuple(pltpu.HBM(a.shape, a.dtype) for a in arrays),
        in_specs=[HBM] * n + [SEM, SEM, ANY], out_specs=tuple([HBM] * n),
        input_output_aliases={q: q for q in range(n)},
        compiler_params=pltpu.CompilerParams(has_side_effects=EFFECT),
    )(*arrays, send, recv, after)


def _exchange_wait_start(name, started, after, build_wait, ncopies, build_start):
    send, recv, arrays = started
    n = len(arrays)

    def body(*refs):
        ins, send_sems, recv_sems = refs[:n], refs[n], refs[n + 1]
        send2, recv2, token = refs[n + 3], refs[n + 4], refs[-1]
        for cp in build_wait(ins, send_sems, recv_sems):
            cp.wait_send()
            cp.wait_recv()
        for cp in build_start(ins, send2, recv2):
            cp.start()
        token[...] = jnp.zeros_like(token)

    outs = pl.pallas_call(
        body, name=name,
        out_shape=(pltpu.SemaphoreType.DMA((ncopies,)), pltpu.SemaphoreType.DMA((ncopies,)),
                   *[pltpu.HBM(a.shape, a.dtype) for a in arrays], jax.ShapeDtypeStruct((SUBLANES, LANES), F32)),
        in_specs=[HBM] * n + [SEM, SEM, ANY], out_specs=(SEM, SEM, *[HBM] * n, pl.BlockSpec(memory_space=pltpu.VMEM)),
        input_output_aliases={q: q + 2 for q in range(n)},
        compiler_params=pltpu.CompilerParams(has_side_effects=EFFECT),
    )(*arrays, send, recv, after)
    return (outs[0], outs[1], list(outs[2:2 + n])), outs[-1]


def _cast_into_slot(w, kc, name, dtype=BF16):
    rows, cols = w.shape
    tr = min(rows, 256)

    def body(kc_ref, w_ref, o_ref):
        o_ref[...] = w_ref[...].astype(dtype)

    grid_spec = pltpu.PrefetchScalarGridSpec(
        num_scalar_prefetch=1, grid=(rows // tr,),
        in_specs=[pl.BlockSpec((tr, cols), lambda r, kc: (r, 0))],
        out_specs=pl.BlockSpec((None, tr, cols), lambda r, kc: (kc[0], r, 0)))
    return pl.pallas_call(
        body, name=name, grid_spec=grid_spec, out_shape=jax.ShapeDtypeStruct((N_CHIPS, rows, cols), dtype),
        compiler_params=_params("arbitrary"),
    )(kc, w)


def _gather_ici_copies(n):
    def build(refs, send_sems, recv_sems):
        x, y, c = _place()
        mine = lambda b: refs[b].at[_chip_of(x, y), c]
        chips = [(1 - x, y), (x, 1 - y), (1 - x, 1 - y)]
        return [pltpu.make_async_remote_copy(
            src_ref=mine(b), dst_ref=mine(b), send_sem=send_sems.at[3 * b + j], recv_sem=recv_sems.at[3 * b + j],
            device_id=(*chip, c), device_id_type=MESH) for b in range(n) for j, chip in enumerate(chips)]
    return build


def _gather_relay_copies(n):
    def build(refs, send_sems, recv_sems):
        x, y, c = _place()
        chips = [(1 - x, y), (x, 1 - y), (1 - x, 1 - y)]
        cps = []
        for b in range(n):
            for j, chip in enumerate(chips):
                got = refs[b].at[_chip_of(*chip), c]
                cps.append(pltpu.make_async_remote_copy(
                    src_ref=got, dst_ref=got, send_sem=send_sems.at[3 * b + j], recv_sem=recv_sems.at[3 * b + j],
                    device_id=(x, y, 1 - c), device_id_type=MESH))
        return cps
    return build


def _sibling_copies(n):
    def build(refs, send_sems, recv_sems):
        x, y, c = _place()
        return [pltpu.make_async_remote_copy(
            src_ref=refs[b].at[:, 1 - c], dst_ref=refs[n + b], send_sem=send_sems.at[b], recv_sem=recv_sems.at[b],
            device_id=(x, y, 1 - c), device_id_type=MESH) for b in range(n)]
    return build


def _chip_copies(n):
    def build(refs, send_sems, recv_sems):
        x, y, c = _place()
        chips = [(1 - x, y), (x, 1 - y), (1 - x, 1 - y)]
        return [pltpu.make_async_remote_copy(
            src_ref=refs[b].at[_chip_of(*chip)], dst_ref=refs[n + b].at[j],
            send_sem=send_sems.at[3 * b + j], recv_sem=recv_sems.at[3 * b + j],
            device_id=(*chip, c), device_id_type=MESH) for b in range(n) for j, chip in enumerate(chips)]
    return build


def _finish_copies(n, n_all):
    def build(refs, send_sems, recv_sems):
        x, y, c = _place()
        cps = [pltpu.make_async_remote_copy(
            src_ref=refs[b].at[c], dst_ref=refs[b].at[c], send_sem=send_sems.at[b], recv_sem=recv_sems.at[b],
            device_id=(x, y, 1 - c), device_id_type=MESH) for b in range(n)]
        flips = [(fx, fy, fc) for fx in (0, 1) for fy in (0, 1) for fc in (0, 1)][1:]
        for b in range(n_all):
            mine = refs[n + b].at[_chip_of(x, y), c]
            cps += [pltpu.make_async_remote_copy(
                src_ref=mine, dst_ref=mine, send_sem=send_sems.at[n + 7 * b + q], recv_sem=recv_sems.at[n + 7 * b + q],
                device_id=(x ^ fx, y ^ fy, c ^ fc), device_id_type=MESH) for q, (fx, fy, fc) in enumerate(flips)]
        return cps
    return build


def _pair_sum(g, r1, kc, name, tr, send_dtype):
    nk, _, rows, cols = g.shape

    def body(kc_ref, g_ref, r_ref, p_ref, own_ref):
        s = g_ref[...] + r_ref[...]
        p_ref[...] = s.astype(send_dtype)

        @pl.when(pl.program_id(1) == kc_ref[0])
        def _():
            own_ref[...] = s

    grid_spec = pltpu.PrefetchScalarGridSpec(
        num_scalar_prefetch=1, grid=(rows // tr, nk),
        in_specs=[pl.BlockSpec((None, None, tr, cols), lambda r, k, kc: (k, kc[1], r, 0)),
                  pl.BlockSpec((None, tr, cols), lambda r, k, kc: (k, r, 0))],
        out_specs=[pl.BlockSpec((None, tr, cols), lambda r, k, kc: (k, r, 0)),
                   pl.BlockSpec((tr, cols), lambda r, k, kc: (r, 0))])
    return pl.pallas_call(
        body, name=name, grid_spec=grid_spec,
        out_shape=[jax.ShapeDtypeStruct((nk, rows, cols), send_dtype), jax.ShapeDtypeStruct((rows, cols), F32)],
        compiler_params=_params("arbitrary", "arbitrary"),
    )(kc, g, r1)


def _chip_sum(own, r2, slot, lead, name, tr):
    rows, cols = own.shape
    nl = len(lead)

    def body(slot_ref, o_ref, r_ref, s_ref):
        s = o_ref[...]
        for j in range(3):
            s = s + r_ref[j].astype(F32)
        s_ref[...] = s

    grid_spec = pltpu.PrefetchScalarGridSpec(
        num_scalar_prefetch=1, grid=(rows // tr,),
        in_specs=[pl.BlockSpec((tr, cols), lambda r, sl: (r, 0)), pl.BlockSpec((3, tr, cols), lambda r, sl: (0, r, 0))],
        out_specs=pl.BlockSpec((None,) * nl + (tr, cols), lambda r, sl: tuple(sl[q] for q in range(nl)) + (r, 0)))
    return pl.pallas_call(
        body, name=name, grid_spec=grid_spec, out_shape=jax.ShapeDtypeStruct(tuple(lead) + (rows, cols), F32),
        compiler_params=_params("arbitrary"),
    )(slot, own, r2)


def _adam_update(w, g, m, v):
    nm = ADAM_B1 * m + (1.0 - ADAM_B1) * g
    nv = ADAM_B2 * v + (1.0 - ADAM_B2) * (g * g)
    m_hat = nm / (1.0 - ADAM_B1 ** ADAM_STEP)
    v_hat = nv / (1.0 - ADAM_B2 ** ADAM_STEP)
    return -ADAM_LR * (m_hat / (jnp.sqrt(v_hat) + ADAM_EPS) + ADAM_WD * w), nm, nv


def _adamw(w, g, m, v, name, tr, token):
    rows, cols = w.shape

    def body(w_ref, g_ref, m_ref, v_ref, token_ref, go_ref, d_ref, nm_ref, nv_ref):
        gv = g_ref[...]
        go_ref[...] = gv
        d_ref[...], nm_ref[...], nv_ref[...] = _adam_update(w_ref[...], gv, m_ref[...], v_ref[...])

    spec = pl.BlockSpec((tr, cols), lambda r: (r, 0))
    return pl.pallas_call(
        body, name=name, grid=(rows // tr,),
        in_specs=[spec] * 4 + [pl.BlockSpec((SUBLANES, LANES), lambda r: (0, 0))], out_specs=[spec] * 4,
        out_shape=[jax.ShapeDtypeStruct((rows, cols), F32)] * 4,
        compiler_params=_params("parallel"),
    )(w, g, m, v, token)


def _adamw_small(packed_g, pre_g_parts, ws, ms, vs):
    names = ["pre_g"] + [n for n, _ in SMALL_ROWS if n != "conv_w"]
    rows = dict(SMALL_ROWS)
    offset, at = {}, 0
    for n, r in SMALL_ROWS:
        offset[n] = at
        at += r
    k = len(names)

    def body(*refs):
        g_ref, pg_ref = refs[0], refs[1]
        w_refs, m_refs, v_refs = refs[2:2 + k], refs[2 + k:2 + 2 * k], refs[2 + 2 * k:2 + 3 * k]
        outs = refs[2 + 3 * k:]
        go, do, mo, vo = outs[:k], outs[k:2 * k], outs[2 * k:3 * k], outs[3 * k:4 * k]
        pre = pg_ref[0]
        for dev in range(1, 8):
            pre = pre + pg_ref[dev]
        outs[4 * k][...] = pre[D_MODEL // LANES:, :]
        for i, n in enumerate(names):
            shp = w_refs[i].shape
            if len(shp) == 2 and shp[0] == 1:
                for r in range(shp[1] // LANES):
                    cols = slice(r * LANES, (r + 1) * LANES)
                    g = pre[r:r + 1, :] if n == "pre_g" else g_ref[offset[n] + r:offset[n] + r + 1, :]
                    go[i][:, cols] = g
                    do[i][:, cols], mo[i][:, cols], vo[i][:, cols] = _adam_update(
                        w_refs[i][:, cols], g, m_refs[i][:, cols], v_refs[i][:, cols])
            else:
                g = g_ref[offset[n]:offset[n] + rows[n], :].reshape(shp)
                go[i][...] = g
                do[i][...], mo[i][...], vo[i][...] = _adam_update(w_refs[i][...], g, m_refs[i][...], v_refs[i][...])

    vm = pl.BlockSpec(memory_space=pltpu.VMEM)
    args = [packed_g, pre_g_parts] + [src[n] for src in (ws, ms, vs) for n in names]
    out_shape = [jax.ShapeDtypeStruct(ws[n].shape, F32) for _ in range(4) for n in names]
    out_shape.append(jax.ShapeDtypeStruct((SUBLANES, LANES), F32))
    outs = pl.pallas_call(
        body, name="adamw_small", in_specs=[vm] * len(args), out_specs=[vm] * (4 * k + 1), out_shape=out_shape,
    )(*args)
    return [dict(zip(names, outs[q * k:(q + 1) * k])) for q in range(4)], outs[4 * k]


def _into_slot(v, tail, slot, lead, name):
    n = v.shape[1]
    nl = len(lead)
    rows = n // LANES + SUBLANES

    def body(slot_ref, v_ref, t_ref, o_ref):
        for r in range(n // LANES):
            o_ref[r:r + 1, :] = v_ref[0:1, r * LANES:(r + 1) * LANES]
        o_ref[n // LANES:, :] = t_ref[...]

    grid_spec = pltpu.PrefetchScalarGridSpec(
        num_scalar_prefetch=1, grid=(1,),
        in_specs=[pl.BlockSpec(v.shape, lambda i, sl: (0, 0)), pl.BlockSpec(tail.shape, lambda i, sl: (0, 0))],
        out_specs=pl.BlockSpec((None,) * nl + (rows, LANES), lambda i, sl: tuple(sl[q] for q in range(nl)) + (0, 0)))
    return pl.pallas_call(
        body, name=name, grid_spec=grid_spec, out_shape=jax.ShapeDtypeStruct(tuple(lead) + (rows, LANES), F32),
    )(slot, v, tail)


def _rows128(a):
    return a.reshape(-1, LANES)


def _pack_small(parts):
    pieces = [_rows128(parts[n]) for n, _ in SMALL_ROWS]
    pieces.append(jnp.zeros((SMALL_TOTAL - SMALL_USED, LANES), F32))
    return jnp.concatenate(pieces, axis=0)


def kernel(x, p, pre_g, w_in, gmlp_ln_g, gmlp_ln_b, gmlp_ws, gmlp_bs, conv_w, conv_b, w_a, b_a, w_x, b_x, lam, gmlp_out_g, lru_out_g, w_out, post_g, w_pe, w_pg, loss_target, m_pre_g, m_w_in, m_gmlp_ln_g, m_gmlp_ln_b, m_gmlp_ws, m_gmlp_bs, m_conv_w, m_conv_b, m_w_a, m_b_a, m_w_x, m_b_x, m_lam, m_gmlp_out_g, m_lru_out_g, m_w_out, m_post_g, m_w_pe, m_w_pg, v_pre_g, v_w_in, v_gmlp_ln_g, v_gmlp_ln_b, v_gmlp_ws, v_gmlp_bs, v_conv_w, v_conv_b, v_w_a, v_b_a, v_w_x, v_b_x, v_lam, v_gmlp_out_g, v_lru_out_g, v_w_out, v_post_g, v_w_pe, v_w_pg):
    weights = dict(pre_g=pre_g, w_in=w_in, gmlp_ln_g=gmlp_ln_g, gmlp_ln_b=gmlp_ln_b, gmlp_ws=gmlp_ws, gmlp_bs=gmlp_bs,
                   conv_w=conv_w, conv_b=conv_b, w_a=w_a, b_a=b_a, w_x=w_x, b_x=b_x, lam=lam, gmlp_out_g=gmlp_out_g,
                   lru_out_g=lru_out_g, w_out=w_out, post_g=post_g, w_pe=w_pe, w_pg=w_pg)
    mom_m = dict(pre_g=m_pre_g, w_in=m_w_in, gmlp_ln_g=m_gmlp_ln_g, gmlp_ln_b=m_gmlp_ln_b, gmlp_ws=m_gmlp_ws,
                 gmlp_bs=m_gmlp_bs, conv_w=m_conv_w, conv_b=m_conv_b, w_a=m_w_a, b_a=m_b_a, w_x=m_w_x, b_x=m_b_x,
                 lam=m_lam, gmlp_out_g=m_gmlp_out_g, lru_out_g=m_lru_out_g, w_out=m_w_out, post_g=m_post_g,
                 w_pe=m_w_pe, w_pg=m_w_pg)
    mom_v = dict(pre_g=v_pre_g, w_in=v_w_in, gmlp_ln_g=v_gmlp_ln_g, gmlp_ln_b=v_gmlp_ln_b, gmlp_ws=v_gmlp_ws,
                 gmlp_bs=v_gmlp_bs, conv_w=v_conv_w, conv_b=v_conv_b, w_a=v_w_a, b_a=v_b_a, w_x=v_w_x, b_x=v_b_x,
                 lam=v_lam, gmlp_out_g=v_gmlp_out_g, lru_out_g=v_lru_out_g, w_out=v_w_out, post_g=v_post_g,
                 w_pe=v_w_pe, w_pg=v_w_pg)
    order = list(weights)
    xi, yi, ci = _place()
    me = _chip_of(xi, yi)
    kc = jnp.stack([me, ci]).astype(jnp.int32)

    x2 = x[0]
    p2 = p[0, 0]
    tgt = loss_target[0]

    first = [_cast_into_slot(w_in[0], kc, "cast_w_in").reshape(N_CHIPS, 2, D_MODEL // 2, W_IN_COLS),
             _cast_into_slot(conv_w[0, :, 0, :], kc, "conv_w_into_slot", F32).reshape(N_CHIPS, 2, CONV_W // 2, CONV_COLS)]
    gather_st, gather_tok = _exchange_start("gather_in_start", first, 6, _gather_ici_copies(2))
    hn, z_own = _inproj_local(x2, pre_g, w_in[0], 256, gather_tok)
    gather_st, gather_tok = _exchange_wait_start("gather_in_relay", gather_st, z_own, _gather_ici_copies(2), 6,
                                                 _gather_relay_copies(2))
    g_in, g_cw = _exchange_wait("gather_in_wait", gather_st, gather_tok, _gather_relay_copies(2))
    wg_in = g_in.reshape(N_CHIPS, D_MODEL, W_IN_COLS)
    cw_full = jnp.transpose(g_cw.reshape(N_CHIPS, CONV_W, CONV_COLS), (1, 0, 2)).reshape(CONV_W, D_HALF)
    later = [_cast_into_slot(w_out[0], kc, "cast_w_out").reshape(N_CHIPS, 2, W_ROWS // 2, D_MODEL),
             _cast_into_slot(w_pg[0], kc, "cast_w_pg").reshape(N_CHIPS, 2, W_ROWS // 2, D_MODEL),
             _cast_into_slot(w_pe[0], kc, "cast_w_pe").reshape(N_CHIPS, 2, D_PLE // 2, W_PE_COLS)]
    gather_st, gather_tok = _exchange_start("gather_start", later, 9, _gather_ici_copies(3), after=g_cw)

    causal = jnp.tril(jnp.ones((CHUNK, CHUNK), dtype=bool))
    ws_m = jnp.where(causal[None], gmlp_ws[0], 0.0)
    prm = dict(
        ln_g=gmlp_ln_g, ln_b=gmlp_ln_b, wt=ws_m.astype(BF16), wtt=jnp.transpose(ws_m, (0, 2, 1)).astype(BF16),
        bsx=jnp.repeat(jnp.transpose(gmlp_bs[0]), CHUNK, axis=1),
        conv_w=cw_full, conv_b=conv_b, w_a=w_a[0].astype(BF16), w_x=w_x[0].astype(BF16),
        b_a=b_a[0].reshape(1, D_HALF), b_x=b_x[0].reshape(1, D_HALF), lam=lam, oga=gmlp_out_g, ogb=lru_out_g)

    z, y, h = _inproj_branches_fwd(hn, z_own, wg_in, kc, prm, 256, gather_tok)
    gather_st, gather_tok = _exchange_wait_start("gather_relay", gather_st, y, _gather_ici_copies(3), 9,
                                                 _gather_relay_copies(3))
    g_out, g_pg, g_pe = _exchange_wait("gather_wait", gather_st, gather_tok, _gather_relay_copies(3))
    wg_out = g_out.reshape(D_MODEL, D_MODEL)
    wg_pg = g_pg.reshape(D_MODEL, D_MODEL)
    wg_pe = g_pe.reshape(N_CHIPS, D_PLE, W_PE_COLS)
    o, h1, gt, dout, loss_acc = _outproj_fwd(x2, y, p2, tgt, post_g, wg_out, wg_pg, wg_pe, 256)

    def sibling_start(tag, bufs):
        lands = [_landing((b.shape[0],) + b.shape[2:], b.dtype) for b in bufs]
        return _exchange_start("sibling_start_" + tag, bufs + lands, len(bufs), _sibling_copies(len(bufs)))

    def pair_then_chip_start(tag, started, after, names, tiles, dtypes):
        n = len(names)
        got = _exchange_wait("sibling_wait_" + tag, started, after, _sibling_copies(n))
        pairs = [_pair_sum(got[b], got[n + b], kc, "pair_sum_" + names[b], tiles[b], dtypes[b]) for b in range(n)]
        lands = [_landing((3,) + pr[0].shape[1:], pr[0].dtype) for pr in pairs]
        return _exchange_start("chip_start_" + tag, [pr[0] for pr in pairs] + lands, 3 * n, _chip_copies(n)), pairs

    def sum_then_finish_start(tag, started, pairs, after, names, tiles, small, to_all=()):
        n = len(names)
        got = _exchange_wait("chip_wait_" + tag, started, after, _chip_copies(n))
        sums = [_chip_sum(pairs[b][1], got[n + b], kc if small and b == n - 1 else kc[1:],
                          (N_CHIPS, 2) if small and b == n - 1 else (2,), "chip_sum_" + names[b], tiles[b])
                for b in range(n)]
        nbig = n - 1 if small else n
        n_all = n - nbig + len(to_all)
        return _exchange_start("finish_start_" + tag, sums + list(to_all), nbig + 7 * n_all,
                               _finish_copies(nbig, n_all))

    gw_pe, dq, dh1, do, dy, g_post = _head_bwd(dout, gt, p2, o, post_g, wg_out, wg_pg, wg_pe, 256)
    gw_pe = gw_pe.reshape(N_CHIPS, 2, D_PLE // 2, W_PE_COLS)
    token0 = jnp.zeros((SUBLANES, LANES), F32)
    gw_out = _weight_grad(y, do, "grad_w_out", 2, 1, D_MODEL // 2, D_MODEL, 1024, token0)
    gw_pg = _weight_grad(h1, dq, "grad_w_pg", 2, 1, D_MODEL // 2, D_MODEL, 1024, token0)
    gw_out = gw_out.reshape(N_CHIPS, 2, W_ROWS // 2, D_MODEL)
    gw_pg = gw_pg.reshape(N_CHIPS, 2, W_ROWS // 2, D_MODEL)

    names_a, tiles_a = ["w_out", "w_pg", "w_pe"], [128, 128, 128]
    st, tok = sibling_start("a", [gw_out, gw_pg, gw_pe])
    (dz, g_oga, g_ogb, g_lng, g_lnb, g_bsx, g_ws, g_cw, g_cb, g_wa, g_ba, g_wx, g_bx, g_lam) = _branches_bwd(
        z, h, dy, prm, 256, tok)
    (st, tok), pairs_a = pair_then_chip_start("a", st, dz, names_a, tiles_a, [BF16] * 3)
    gw_in = _weight_grad(hn, dz, "grad_w_in", 2, N_CHIPS, D_MODEL // 2, W_IN_COLS, 1024, tok)
    fin_a, tok = sum_then_finish_start("a", st, pairs_a, gw_in, names_a, tiles_a, False)

    small_g = dict(
        gmlp_ln_g=g_lng[0:1], gmlp_ln_b=g_lnb[0:1], gmlp_ws=g_ws,
        gmlp_bs=jnp.transpose(g_bsx[:, ::CHUNK]), conv_w=g_cw[::SUBLANES], conv_b=g_cb[0:1], w_a=g_wa, b_a=g_ba[0:1],
        w_x=g_wx, b_x=g_bx[0:1], lam=g_lam[0:1], gmlp_out_g=g_oga[0:1], lru_out_g=g_ogb[0:1], post_g=g_post[0:1])
    gsm = _pack_small(small_g).reshape(N_CHIPS, 2, SMALL_PIECE, LANES)

    names_b, tiles_b = ["w_in", "small"], [256, SMALL_PIECE]
    n_tiles = x2.shape[0] // 256
    n_lo = max(1, (5 * n_tiles) // 16)
    st, tok_b = _exchange_start(
        "sibling_start_b", [gw_in, gsm] + [_landing((N_CHIPS,) + b.shape[2:], F32) for b in (gw_in, gsm)], 2,
        _sibling_copies(2), after=tok)
    part = _inproj_bwd(dz, wg_in, x2, dh1, pre_g, 256, 0, n_lo, None, False, tok_b, "inproj_bwd_lo")
    f_out, f_pg, f_pe = _exchange_wait("finish_wait_a", fin_a, part[1], _finish_copies(3, 0))
    (st, tok_b), pairs_b = pair_then_chip_start("b", st, part[1], names_b, tiles_b, [BF16, F32])
    grad_x, g_pre = _inproj_bwd(dz, wg_in, x2, dh1, pre_g, 256, n_lo, n_tiles - n_lo, part, True, tok_b,
                                "inproj_bwd_hi")
    pre_parts = _into_slot(g_pre, loss_acc, kc, (N_CHIPS, 2), "pre_g_into_slot")
    fin_b, tok_b = sum_then_finish_start("b", st, pairs_b, g_pre, names_b, tiles_b, True, to_all=[pre_parts])

    grads, deltas, new_m, new_v = {}, {}, {}, {}

    def adam_big(n, g2d, tr, token):
        shp = weights[n].shape
        g, d, nm, nv = _adamw(weights[n][0], g2d, mom_m[n][0], mom_v[n][0], "adamw_" + n, tr, token)
        grads[n], deltas[n], new_m[n], new_v[n] = g.reshape(shp), d.reshape(shp), nm.reshape(shp), nv.reshape(shp)
        return d

    as_token = lambda d: d[:SUBLANES, :LANES]
    last = adam_big("w_out", f_out.reshape(W_ROWS, D_MODEL), 128, tok_b)
    last = adam_big("w_pg", f_pg.reshape(W_ROWS, D_MODEL), 128, as_token(last))
    last = adam_big("w_pe", f_pe.reshape(D_PLE, W_PE_COLS), 128, as_token(last))
    f_in, f_sm, pre_parts = _exchange_wait("finish_wait_b", fin_b, last, _finish_copies(1, 2))
    adam_big("w_in", f_in.reshape(D_MODEL, W_IN_COLS), 256, tok_b)

    packed_g = f_sm.reshape(SMALL_TOTAL, LANES)
    small_names = ["pre_g"] + [n for n, _ in SMALL_ROWS if n != "conv_w"]
    natural = lambda src: {n: (src[n] if src[n].ndim == 2 else src[n][0]) for n in small_names}
    outs, loss_block = _adamw_small(packed_g, pre_parts.reshape(8, D_MODEL // LANES + SUBLANES, LANES),
                                    natural(weights), natural(mom_m), natural(mom_v))
    loss = loss_block[0, 0]
    for dst, got in zip((grads, deltas, new_m, new_v), outs):
        for n in small_names:
            dst[n] = got[n].reshape(weights[n].shape)
    at = sum(r for n, r in SMALL_ROWS[:[n for n, _ in SMALL_ROWS].index("conv_w")])
    g_cw_all = packed_g[at:at + CONV_W * D_HALF // LANES].reshape(CONV_W, D_HALF)
    g_conv = lax.dynamic_slice_in_dim(g_cw_all, me * CONV_COLS, CONV_COLS, axis=1)
    g, d, nm, nv = _adamw(conv_w[0, :, 0, :], g_conv, m_conv_w[0, :, 0, :], v_conv_w[0, :, 0, :], "adamw_conv_w", CONV_W,
                          tok_b)
    cshape = conv_w.shape
    grads["conv_w"], deltas["conv_w"] = g.reshape(cshape), d.reshape(cshape)
    new_m["conv_w"], new_v["conv_w"] = nm.reshape(cshape), nv.reshape(cshape)

    return (loss, grad_x.reshape(x.shape), *[grads[n] for n in order], *[deltas[n] for n in order],
            *[new_m[n] for n in order], *[new_v[n] for n in order])
```

```python
import functools
import math

import jax
import jax.numpy as jnp
from jax import lax
from jax.experimental import pallas as pl
from jax.experimental.pallas import tpu as pltpu

F32 = jnp.float32
BF16 = jnp.bfloat16

D_MODEL = 2048
D_HALF = 1024
D_Z = 5120
D_PLE = 256
CHUNK = 128
N_HEADS = 8
N_CHIPS = 4
W_IN_COLS = D_Z // N_CHIPS
W_ROWS = D_MODEL // N_CHIPS
W_PE_COLS = D_MODEL // N_CHIPS
CONV_W = 4
CONV_COLS = D_HALF // N_CHIPS
EPS = 1e-6
LRU_C = 8.0
ADAM_LR, ADAM_B1, ADAM_B2, ADAM_EPS, ADAM_WD, ADAM_STEP = 0.001, 0.9, 0.999, 1e-08, 0.01, 10

SUBLANES = 8
LANES = 128
VMEM_LIMIT = 56 * 1024 * 1024

SMALL_ROWS = (("gmlp_ln_g", 8), ("gmlp_ln_b", 8), ("gmlp_ws", 1024), ("gmlp_bs", 8),
              ("conv_w", 32), ("conv_b", 8), ("w_a", 1024), ("b_a", 8), ("w_x", 1024), ("b_x", 8),
              ("lam", 8), ("gmlp_out_g", 8), ("lru_out_g", 8), ("post_g", 16))
SMALL_USED = sum(r for _, r in SMALL_ROWS)
SMALL_PIECE = 400
SMALL_TOTAL = 8 * SMALL_PIECE

MESH = pl.DeviceIdType.MESH
ANY = pl.BlockSpec(memory_space=pl.ANY)

_GELU_C0 = math.sqrt(2.0 / math.pi)
_GELU_C1 = 0.044715


def _params(*sem):
    return pltpu.CompilerParams(dimension_semantics=sem, vmem_limit_bytes=VMEM_LIMIT)


def _dot(a, b):
    return jnp.dot(a, b, preferred_element_type=F32)


def _dot_nt(a, b):
    return lax.dot_general(a, b, (((1,), (1,)), ((), ())), preferred_element_type=F32)


def _dot_tn(a, b):
    return lax.dot_general(a, b, (((0,), (0,)), ((), ())), preferred_element_type=F32)


def _gelu(x):
    t = jnp.tanh(_GELU_C0 * (x + _GELU_C1 * (x * x * x)))
    return 0.5 * x * (1.0 + t), t


def _gelu_grad(x, t):
    return 0.5 * (1.0 + t) + 0.5 * x * (1.0 - t * t) * (_GELU_C0 * (1.0 + 3.0 * _GELU_C1 * x * x))


def _rowsum8(v):
    r, n = v.shape
    return jnp.sum(v.reshape(r // SUBLANES, SUBLANES, n), axis=0)


def _lanemean(v):
    return jnp.mean(v, axis=-1, keepdims=True)


def _shift_down(v, halo8, k):
    if k == 0:
        return v
    r = pltpu.roll(v, k, 0)
    hr = pltpu.roll(halo8, k, 0)
    row = lax.broadcasted_iota(jnp.int32, halo8.shape, 0)
    top = jnp.where(row < k, hr, r[0:SUBLANES])
    return jnp.concatenate([top, r[SUBLANES:]], axis=0)


def _shift_up(v, next8, k):
    if k == 0:
        return v
    n = v.shape[0]
    r = pltpu.roll(v, n - k, 0)
    nr = pltpu.roll(next8, SUBLANES - k, 0)
    row = lax.broadcasted_iota(jnp.int32, next8.shape, 0)
    bot = jnp.where(row >= SUBLANES - k, nr, r[n - SUBLANES:])
    return jnp.concatenate([r[:n - SUBLANES], bot], axis=0)


def _layernorm_parts(vg):
    mu = _lanemean(vg)
    xc = vg - mu
    rstd = lax.rsqrt(_lanemean(xc * xc) + EPS)
    return xc * rstd, rstd


def _spatial_mix(wt_ref, vn_ref, bsx_ref, mixed_ref, tm):
    for c in range(tm // CHUNK):
        rows = slice(c * CHUNK, (c + 1) * CHUNK)
        for h in range(N_HEADS):
            cols = slice(h * CHUNK, (h + 1) * CHUNK)
            mixed_ref[rows, cols] = _dot(wt_ref[h], vn_ref[rows, cols]) + bsx_ref[:, cols]


def _conv_taps(xb, halo8):
    return [_shift_down(xb, halo8, CONV_W - 1 - k) for k in range(CONV_W)]


def _lru_gates(xc_bf_ref, wa_ref, wx_ref, ba_ref, bx_ref, r_ref, i_ref):
    for h in range(N_HEADS):
        cols = slice(h * CHUNK, (h + 1) * CHUNK)
        xh = xc_bf_ref[:, cols]
        r_ref[:, cols] = jax.nn.sigmoid(_dot(xh, wa_ref[h]) + ba_ref[:, cols])
        i_ref[:, cols] = jax.nn.sigmoid(_dot(xh, wx_ref[h]) + bx_ref[:, cols])


def _softplus_neg(lam):
    return jnp.maximum(-lam, 0.0) + jnp.log(1.0 + jnp.exp(-jnp.abs(lam)))


def _decay_parts(r, lam):
    la = (-LRU_C * _softplus_neg(lam)) * r
    a = jnp.exp(la)
    th = -jnp.tanh(la)
    mult = jnp.sqrt(2.0 * th / (1.0 + th))
    return a, mult


def _z_group(zref, g, rows=slice(None)):
    lo = g * D_HALF
    blk, off = lo // W_IN_COLS, lo % W_IN_COLS
    if off + D_HALF <= W_IN_COLS:
        return zref[blk, rows, off:off + D_HALF]
    return jnp.concatenate([zref[blk, rows, off:W_IN_COLS], zref[blk + 1, rows, 0:off + D_HALF - W_IN_COLS]], axis=1)


def _inproj_local(x, pre_g, w_own, tm, token):
    t = x.shape[0]

    def body(x_ref, g_ref, w_ref, token_ref, hn_ref, zl_ref, wbf_s):
        @pl.when(pl.program_id(0) == 0)
        def _():
            wbf_s[...] = w_ref[...].astype(BF16)

        xv = x_ref[...]
        hn = (xv * lax.rsqrt(_lanemean(xv * xv) + EPS) * g_ref[...]).astype(BF16)
        hn_ref[...] = hn
        zl_ref[...] = _dot(hn, wbf_s[...]).astype(BF16)

    row = lambda n: pl.BlockSpec((tm, n), lambda i: (i, 0))
    const = lambda shp: pl.BlockSpec(shp, lambda i: (0, 0), pipeline_mode=pl.Buffered(1))
    return pl.pallas_call(
        body, name="inproj_local", grid=(t // tm,),
        in_specs=[row(D_MODEL), const((1, D_MODEL)), const((D_MODEL, W_IN_COLS)), const((SUBLANES, LANES))],
        out_specs=[row(D_MODEL), row(W_IN_COLS)],
        out_shape=[jax.ShapeDtypeStruct((t, D_MODEL), BF16), jax.ShapeDtypeStruct((t, W_IN_COLS), BF16)],
        scratch_shapes=[pltpu.VMEM((D_MODEL, W_IN_COLS), BF16)],
        compiler_params=_params("arbitrary"),
    )(x, pre_g, w_own, token)


def _inproj_branches_fwd(hn, z_own, wg_in, kc, prm, tm, token):
    t = hn.shape[0]
    nt = t // tm
    hb = tm // SUBLANES

    def body(kc_ref, hn_ref, zo_ref, w1_ref, w2_ref, w3_ref,
             lng_ref, lnb_ref, wt_ref, bsx_ref, cw_ref, cb_ref, wa_ref, wx_ref, ba_ref, bx_ref, lam_ref,
             oga_ref, ogb_ref, token_ref,
             z_ref, y_ref, h_ref,
             zbuf0, zbuf1, vn_s, mixed_s, xcbf_s, r_s, i_s, ug_s, halo_s, carry_s):
        s = pl.program_id(0)
        me = kc_ref[0]
        w_refs = (None, w1_ref, w2_ref, w3_ref)

        @pl.when(s == 0)
        def _():
            zbuf1[...] = jnp.zeros_like(zbuf1)

        @pl.when(s <= 1)
        def _():
            carry_s[...] = jnp.zeros_like(carry_s)
            halo_s[...] = jnp.zeros_like(halo_s)

        def step(zw, zr):
            def project(r):
                blk = (me + r) % N_CHIPS
                zb = zo_ref[...] if r == 0 else _dot(hn_ref[...], w_refs[r][...]).astype(BF16)
                z_ref[blk] = zb
                zw[blk] = zb

            zin = lambda g: _z_group(zr, g).astype(F32)
            always = [s >= 0] * 4

            @pl.when(always[0])
            def _():
                project(0)
                ug, _ = _gelu(zin(0))
                ug_s[...] = ug
                vg, _ = _gelu(zin(1))
                vhat, _ = _layernorm_parts(vg)
                vn_s[...] = (vhat * lng_ref[...] + lnb_ref[...]).astype(BF16)

            @pl.when(always[1])
            def _():
                project(1)
                _spatial_mix(wt_ref, vn_s, bsx_ref, mixed_s, tm)
                ga = zin(2)
                ya = ug_s[...] * mixed_s[...] * (ga * jax.nn.sigmoid(ga))
                ra = lax.rsqrt(_lanemean(ya * ya) + EPS)
                y_ref[:, 0:D_HALF] = (ya * ra * oga_ref[...]).astype(BF16)

            @pl.when(always[2])
            def _():
                project(2)
                xb = zin(3)
                taps = _conv_taps(xb, halo_s[...])
                halo_s[...] = xb[tm - SUBLANES:]
                xc = cb_ref[...] + taps[0] * cw_ref[0:1, :]
                for k in range(1, CONV_W):
                    xc = xc + taps[k] * cw_ref[k:k + 1, :]
                xcbf_s[...] = xc.astype(BF16)
                _lru_gates(xcbf_s, wa_ref, wx_ref, ba_ref, bx_ref, r_s, i_s)
                a, mult = _decay_parts(r_s[...], lam_ref[...])
                row = lax.broadcasted_iota(jnp.int32, a.shape, 0)
                mult = jnp.where(jnp.logical_and(s == 1, row == 0), 1.0, mult)
                r_s[...] = a
                i_s[...] = mult * (i_s[...] * xc)

            @pl.when(always[3])
            def _():
                project(3)
                a = r_s[...]
                b = i_s[...]
                r8 = lax.broadcasted_iota(jnp.int32, a.shape, 0) & (SUBLANES - 1)
                for d in (1, 2, 4):
                    a_sh = pltpu.roll(a, d, 0)
                    b_sh = pltpu.roll(b, d, 0)
                    m = r8 >= d
                    b = jnp.where(m, a * b_sh + b, b)
                    a = jnp.where(m, a * a_sh, a)
                carry = carry_s[...]
                for g in range(hb):
                    rows = slice(g * SUBLANES, (g + 1) * SUBLANES)
                    hg = a[rows] * carry + b[rows]
                    h_ref[rows, :] = hg
                    carry = jnp.broadcast_to(hg[SUBLANES - 1:SUBLANES, :], hg.shape)
                carry_s[...] = carry
                gb = zin(4)
                yb = h_ref[...] * (gb * jax.nn.sigmoid(gb))
                rb = lax.rsqrt(_lanemean(yb * yb) + EPS)
                y_ref[:, D_HALF:] = (yb * rb * ogb_ref[...]).astype(BF16)

        @pl.when(s % 2 == 0)
        def _():
            step(zbuf0, zbuf1)

        @pl.when(s % 2 == 1)
        def _():
            step(zbuf1, zbuf0)

    const = lambda a: pl.BlockSpec(a.shape, lambda s, kc, n=a.ndim: (0,) * n, pipeline_mode=pl.Buffered(1))
    proj = lambda n: pl.BlockSpec((tm, n), lambda s, kc: (jnp.minimum(s, nt - 1), 0))
    head = lambda n: pl.BlockSpec((tm, n), lambda s, kc: (jnp.maximum(s - 1, 0), 0))
    other = lambda r: pl.BlockSpec((None, D_MODEL, W_IN_COLS), lambda s, kc, r=r: ((kc[0] + r) % N_CHIPS, 0, 0),
                                   pipeline_mode=pl.Buffered(1))
    names = ("ln_g", "ln_b", "wt", "bsx", "conv_w", "conv_b", "w_a", "w_x", "b_a", "b_x", "lam", "oga", "ogb")
    pr = [prm[n] for n in names] + [token]
    big = lambda dt: pltpu.VMEM((tm, D_HALF), dt)
    zblocks = pltpu.VMEM((N_CHIPS, tm, W_IN_COLS), BF16)
    grid_spec = pltpu.PrefetchScalarGridSpec(
        num_scalar_prefetch=1, grid=(nt + 1,),
        in_specs=[proj(D_MODEL), proj(W_IN_COLS), other(1), other(2), other(3)] + [const(a) for a in pr],
        out_specs=[pl.BlockSpec((N_CHIPS, tm, W_IN_COLS), lambda s, kc: (0, jnp.minimum(s, nt - 1), 0)),
                   head(D_MODEL), head(D_HALF)],
        scratch_shapes=[zblocks, zblocks, big(BF16), big(F32), big(BF16), big(F32), big(F32), big(F32),
                        pltpu.VMEM((SUBLANES, D_HALF), F32), pltpu.VMEM((SUBLANES, D_HALF), F32)])
    return pl.pallas_call(
        body, name="inproj_branches_fwd", grid_spec=grid_spec,
        out_shape=[jax.ShapeDtypeStruct((N_CHIPS, t, W_IN_COLS), BF16), jax.ShapeDtypeStruct((t, D_MODEL), BF16),
                   jax.ShapeDtypeStruct((t, D_HALF), F32)],
        compiler_params=_params("arbitrary"),
    )(kc, hn, z_own, wg_in, wg_in, wg_in, *pr)


def _outproj_fwd(x, y, p, tgt, post_g, w_out, w_pg, wg_pe, tm):
    t = x.shape[0]

    def body(x_ref, y_ref, p_ref, tgt_ref, pg_ref, wo_ref, wpg_ref, wpe_ref,
             o_ref, h1_ref, gt_ref, dout_ref, loss_ref):
        @pl.when(pl.program_id(0) == 0)
        def _():
            loss_ref[...] = jnp.zeros_like(loss_ref)

        o = _dot(y_ref[...], wo_ref[...])
        o_ref[...] = o
        r3 = lax.rsqrt(_lanemean(o * o) + EPS)
        h1 = x_ref[...] + (o * r3) * pg_ref[...]
        h1b = h1.astype(BF16)
        h1_ref[...] = h1b
        gt = jax.nn.sigmoid(_dot(h1b, wpg_ref[...]))
        gt_ref[...] = gt
        pb = p_ref[...].astype(BF16)
        for k in range(N_CHIPS):
            cols = slice(k * W_PE_COLS, (k + 1) * W_PE_COLS)
            pe = _dot(pb, wpe_ref[k])
            d = h1[:, cols] + pe * gt[:, cols] - tgt_ref[:, cols]
            dout_ref[:, cols] = d * (1.0 / D_MODEL)
            loss_ref[...] += jnp.sum(d * d) * (0.5 / D_MODEL)

    row = lambda n: pl.BlockSpec((tm, n), lambda i: (i, 0))
    const = lambda shp: pl.BlockSpec(shp, lambda i, n=len(shp): (0,) * n, pipeline_mode=pl.Buffered(1))
    return pl.pallas_call(
        body, name="outproj_fwd", grid=(t // tm,),
        in_specs=[row(D_MODEL), row(D_MODEL), row(D_PLE), row(D_MODEL), const((1, D_MODEL)),
                  const((D_MODEL, D_MODEL)), const((D_MODEL, D_MODEL)), const((N_CHIPS, D_PLE, W_PE_COLS))],
        out_specs=[row(D_MODEL), row(D_MODEL), row(D_MODEL), row(D_MODEL),
                   pl.BlockSpec((SUBLANES, LANES), lambda i: (0, 0))],
        out_shape=[jax.ShapeDtypeStruct((t, D_MODEL), F32), jax.ShapeDtypeStruct((t, D_MODEL), BF16),
                   jax.ShapeDtypeStruct((t, D_MODEL), F32), jax.ShapeDtypeStruct((t, D_MODEL), F32),
                   jax.ShapeDtypeStruct((SUBLANES, LANES), F32)],
        compiler_params=_params("arbitrary"),
    )(x, y, p, tgt, post_g, w_out, w_pg, wg_pe)


def _head_bwd(dout, gt, p, o, post_g, w_out, w_pg, wg_pe, tm):
    t = dout.shape[0]

    def body(dout_ref, gt_ref, p_ref, o_ref, pg_ref, wo_ref, wpg_ref, wpe_ref,
             gwpe_ref, dq_ref, dh1_ref, do_ref, dy_ref, gpost_ref):
        i = pl.program_id(0)

        @pl.when(i == 0)
        def _():
            gpost_ref[...] = jnp.zeros_like(gpost_ref)
            gwpe_ref[...] = jnp.zeros_like(gwpe_ref)

        dout = dout_ref[...]
        gt = gt_ref[...]
        pb = p_ref[...].astype(BF16)
        for k in range(N_CHIPS):
            cols = slice(k * W_PE_COLS, (k + 1) * W_PE_COLS)
            pe = _dot(pb, wpe_ref[k])
            g = gt[:, cols]
            dg = dout[:, cols] * g
            gwpe_ref[k] += _dot_tn(pb, dg.astype(BF16))
            dq_ref[:, cols] = (dg * pe * (1.0 - g)).astype(BF16)
        dh1 = dout + _dot_nt(dq_ref[...], wpg_ref[...])
        dh1_ref[...] = dh1
        o = o_ref[...]
        r3 = lax.rsqrt(_lanemean(o * o) + EPS)
        on = o * r3
        gpost_ref[...] += _rowsum8(dh1 * on)
        don = dh1 * pg_ref[...]
        do = r3 * (don - on * _lanemean(don * on))
        dob = do.astype(BF16)
        do_ref[...] = dob
        dy_ref[...] = _dot_nt(dob, wo_ref[...])

        @pl.when(i == pl.num_programs(0) - 1)
        def _():
            gpost_ref[...] = jnp.broadcast_to(jnp.sum(gpost_ref[...], axis=0, keepdims=True), gpost_ref.shape)

    row = lambda n: pl.BlockSpec((tm, n), lambda i: (i, 0))
    const = lambda shp: pl.BlockSpec(shp, lambda i, n=len(shp): (0,) * n, pipeline_mode=pl.Buffered(1))
    return pl.pallas_call(
        body, name="head_bwd", grid=(t // tm,),
        in_specs=[row(D_MODEL), row(D_MODEL), row(D_PLE), row(D_MODEL), const((1, D_MODEL)),
                  const((D_MODEL, D_MODEL)), const((D_MODEL, D_MODEL)), const((N_CHIPS, D_PLE, W_PE_COLS))],
        out_specs=[pl.BlockSpec((N_CHIPS, D_PLE, W_PE_COLS), lambda i: (0, 0, 0)),
                   row(D_MODEL), row(D_MODEL), row(D_MODEL), row(D_MODEL),
                   pl.BlockSpec((SUBLANES, D_MODEL), lambda i: (0, 0))],
        out_shape=[jax.ShapeDtypeStruct((N_CHIPS, D_PLE, W_PE_COLS), F32), jax.ShapeDtypeStruct((t, D_MODEL), BF16),
                   jax.ShapeDtypeStruct((t, D_MODEL), F32), jax.ShapeDtypeStruct((t, D_MODEL), BF16),
                   jax.ShapeDtypeStruct((t, D_MODEL), F32), jax.ShapeDtypeStruct((SUBLANES, D_MODEL), F32)],
        compiler_params=_params("arbitrary"),
    )(dout, gt, p, o, post_g, w_out, w_pg, wg_pe)


def _branches_bwd(z, h, dy, prm, tm, token):
    t = h.shape[0]
    nt = t // tm
    hb = tm // SUBLANES

    def body(z_ref, zh_ref, h_ref, hh_ref, dy_ref,
             lng_ref, lnb_ref, wt_ref, wtt_ref, bsx_ref, cw_ref, cb_ref, wa_ref, wx_ref, ba_ref, bx_ref, lam_ref,
             oga_ref, ogb_ref, token_ref,
             dz_ref, g_oga, g_ogb, g_lng, g_lnb, g_bsx, g_ws, g_cw, g_cb, g_wa, g_ba, g_wx, g_bx, g_lam,
             vn_s, mixed_s, dm_s, dvn_s, xcbf_s, r_s, i_s, a_s, b_s, dh_s, dpr_s, dpi_s, dxc_s,
             ca_s, cd_s, cx_s):
        step_i = pl.program_id(0)
        tile = nt - 1 - step_i
        accs = (g_oga, g_ogb, g_lng, g_lnb, g_bsx, g_ws, g_cw, g_cb, g_wa, g_ba, g_wx, g_bx, g_lam)

        @pl.when(step_i == 0)
        def _():
            for r in accs + (ca_s, cd_s, cx_s):
                r[...] = jnp.zeros_like(r)

        dy_a = dy_ref[:, 0:D_HALF]
        dy_b = dy_ref[:, D_HALF:]

        u = _z_group(z_ref, 0).astype(F32)
        ug, tu = _gelu(u)
        v = _z_group(z_ref, 1).astype(F32)
        vg, tv = _gelu(v)
        vhat, rstd = _layernorm_parts(vg)
        vn_s[...] = (vhat * lng_ref[...] + lnb_ref[...]).astype(BF16)
        _spatial_mix(wt_ref, vn_s, bsx_ref, mixed_s, tm)
        mixed = mixed_s[...]
        ga = _z_group(z_ref, 2).astype(F32)
        sga = jax.nn.sigmoid(ga)
        sa = ga * sga
        um = ug * mixed
        ya = um * sa
        ra = lax.rsqrt(_lanemean(ya * ya) + EPS)
        yahat = ya * ra
        g_oga[...] += _rowsum8(dy_a * yahat)
        dn = dy_a * oga_ref[...]
        dya = ra * (dn - yahat * _lanemean(dn * yahat))
        dz_ref[:, 2 * D_HALF:3 * D_HALF] = (dya * um * (sga * (1.0 + ga * (1.0 - sga)))).astype(BF16)
        dz_ref[:, 0:D_HALF] = (dya * mixed * sa * _gelu_grad(u, tu)).astype(BF16)
        dmixed = dya * ug * sa
        g_bsx[...] += jnp.sum(dmixed.reshape(tm // CHUNK, CHUNK, D_HALF), axis=0)
        dm_s[...] = dmixed.astype(BF16)
        for c in range(tm // CHUNK):
            rows = slice(c * CHUNK, (c + 1) * CHUNK)
            for hd in range(N_HEADS):
                cols = slice(hd * CHUNK, (hd + 1) * CHUNK)
                dmh = dm_s[rows, cols]
                dvn_s[rows, cols] = _dot(wtt_ref[hd], dmh)
                g_ws[hd] += _dot_nt(dmh, vn_s[rows, cols])
        dvn = dvn_s[...]
        g_lng[...] += _rowsum8(dvn * vhat)
        g_lnb[...] += _rowsum8(dvn)
        dvh = dvn * lng_ref[...]
        dvg = rstd * (dvh - _lanemean(dvh) - vhat * _lanemean(dvh * vhat))
        dz_ref[:, D_HALF:2 * D_HALF] = (dvg * _gelu_grad(v, tv)).astype(BF16)

        xb = _z_group(z_ref, 3).astype(F32)
        halo = jnp.where(tile == 0, 0.0, _z_group(zh_ref, 3).astype(F32)[SUBLANES:])
        taps = _conv_taps(xb, halo)
        xc = cb_ref[...] + taps[0] * cw_ref[0:1, :]
        for k in range(1, CONV_W):
            xc = xc + taps[k] * cw_ref[k:k + 1, :]
        xcbf_s[...] = xc.astype(BF16)
        _lru_gates(xcbf_s, wa_ref, wx_ref, ba_ref, bx_ref, r_s, i_s)
        rg = r_s[...]
        ig = i_s[...]
        lam = lam_ref[...]
        a, mult_true = _decay_parts(rg, lam)
        row = lax.broadcasted_iota(jnp.int32, a.shape, 0)
        first = jnp.logical_and(tile == 0, row == 0)
        mult = jnp.where(first, 1.0, mult_true)
        hcur = h_ref[...]
        hprev = _shift_down(hcur, jnp.where(tile == 0, 0.0, hh_ref[...]), 1)
        gb = _z_group(z_ref, 4).astype(F32)
        sgb = jax.nn.sigmoid(gb)
        sb = gb * sgb
        yb = hcur * sb
        rb = lax.rsqrt(_lanemean(yb * yb) + EPS)
        ybhat = yb * rb
        g_ogb[...] += _rowsum8(dy_b * ybhat)
        dn = dy_b * ogb_ref[...]
        dyb = rb * (dn - ybhat * _lanemean(dn * ybhat))
        dz_ref[:, 4 * D_HALF:5 * D_HALF] = (dyb * hcur * (sgb * (1.0 + gb * (1.0 - sgb)))).astype(BF16)

        an = _shift_up(a, ca_s[...], 1)
        bb = dyb * sb
        r8 = row & (SUBLANES - 1)
        for d in (1, 2, 4):
            a_sh = pltpu.roll(an, tm - d, 0)
            b_sh = pltpu.roll(bb, tm - d, 0)
            m = r8 + d < SUBLANES
            bb = jnp.where(m, an * b_sh + bb, bb)
            an = jnp.where(m, an * a_sh, an)
        a_s[...] = an
        b_s[...] = bb

        def step(g, carry):
            sl = pl.ds(pl.multiple_of((hb - 1 - g) * SUBLANES, SUBLANES), SUBLANES)
            dg = a_s[sl, :] * carry + b_s[sl, :]
            dh_s[sl, :] = dg
            return jnp.broadcast_to(dg[0:1, :], dg.shape)

        cd_s[...] = lax.fori_loop(0, hb, step, cd_s[...])
        ca_s[...] = jnp.broadcast_to(a[0:1, :], ca_s.shape)
        dh = dh_s[...]
        da = dh * hprev
        gx = ig * xc
        dla = da * a - jnp.where(first, 0.0, dh * gx * (a * a / mult_true))
        g_lam[...] += _rowsum8(dla * rg)
        dr = dla * (-LRU_C * _softplus_neg(lam))
        dpr = dr * rg * (1.0 - rg)
        dpi = (dh * mult * xc) * ig * (1.0 - ig)
        g_ba[...] += _rowsum8(dpr)
        g_bx[...] += _rowsum8(dpi)
        dpr_s[...] = dpr.astype(BF16)
        dpi_s[...] = dpi.astype(BF16)
        for hd in range(N_HEADS):
            cols = slice(hd * CHUNK, (hd + 1) * CHUNK)
            xh = xcbf_s[:, cols]
            dprh = dpr_s[:, cols]
            dpih = dpi_s[:, cols]
            g_wa[hd] += _dot_tn(xh, dprh)
            g_wx[hd] += _dot_tn(xh, dpih)
            dxc_s[:, cols] = _dot_nt(dprh, wa_ref[hd]) + _dot_nt(dpih, wx_ref[hd])
        dxc = dxc_s[...] + dh * mult * ig
        g_cb[...] += _rowsum8(dxc)
        for k in range(CONV_W):
            g_cw[k * SUBLANES:(k + 1) * SUBLANES, :] += _rowsum8(dxc * taps[k])
        nxt = cx_s[...]
        dxb = dxc * cw_ref[CONV_W - 1:CONV_W, :]
        for j in range(1, CONV_W):
            dxb = dxb + _shift_up(dxc, nxt, j) * cw_ref[CONV_W - 1 - j:CONV_W - j, :]
        dz_ref[:, 3 * D_HALF:4 * D_HALF] = dxb.astype(BF16)
        cx_s[...] = dxc[0:SUBLANES]

        @pl.when(step_i == nt - 1)
        def _():
            for r in (g_oga, g_ogb, g_lng, g_lnb, g_cb, g_ba, g_bx):
                r[...] = jnp.broadcast_to(jnp.sum(r[...], axis=0, keepdims=True), r.shape)
            lam_f = LRU_C * jax.nn.sigmoid(-lam_ref[...])
            g_lam[...] = jnp.broadcast_to(jnp.sum(g_lam[...], axis=0, keepdims=True) * lam_f, g_lam.shape)
            for k in range(CONV_W):
                blk = g_cw[k * SUBLANES:(k + 1) * SUBLANES, :]
                g_cw[k * SUBLANES:(k + 1) * SUBLANES, :] = jnp.broadcast_to(jnp.sum(blk, axis=0, keepdims=True), blk.shape)
            tri = (lax.broadcasted_iota(jnp.int32, (CHUNK, CHUNK), 0) >= lax.broadcasted_iota(jnp.int32, (CHUNK, CHUNK), 1))
            for hd in range(N_HEADS):
                cols = slice(hd * CHUNK, (hd + 1) * CHUNK)
                g_ws[hd] = jnp.where(tri, g_ws[hd], 0.0)
                blk = g_bsx[:, cols]
                g_bsx[:, cols] = jnp.broadcast_to(jnp.sum(blk, axis=1, keepdims=True), blk.shape)

    rev = lambda i: nt - 1 - i
    zspec = pl.BlockSpec((N_CHIPS, tm, W_IN_COLS), lambda i: (0, rev(i), 0))
    halo = lambda col: pl.BlockSpec((SUBLANES, D_HALF), lambda i: (jnp.maximum(rev(i) * hb - 1, 0), col))
    zhalo = pl.BlockSpec((N_CHIPS, 2 * SUBLANES, W_IN_COLS), lambda i: (0, jnp.maximum(rev(i) * (hb // 2) - 1, 0), 0))
    full = lambda a: pl.BlockSpec(a.shape, lambda i, n=a.ndim: (0,) * n)
    acc = lambda shp: pl.BlockSpec(shp, lambda i, n=len(shp): (0,) * n)
    names = ("ln_g", "ln_b", "wt", "wtt", "bsx", "conv_w", "conv_b", "w_a", "w_x", "b_a", "b_x", "lam", "oga", "ogb")
    pr = [prm[n] for n in names] + [token]
    vec = (SUBLANES, D_HALF)
    mat = (N_HEADS, CHUNK, CHUNK)
    acc_shapes = [vec, vec, vec, vec, (CHUNK, D_HALF), mat, (CONV_W * SUBLANES, D_HALF), vec, mat, vec, mat, vec, vec]
    big = lambda dt: pltpu.VMEM((tm, D_HALF), dt)
    return pl.pallas_call(
        body, name="branches_bwd", grid=(nt,),
        in_specs=[zspec, zhalo,
                  pl.BlockSpec((tm, D_HALF), lambda i: (rev(i), 0)), halo(0),
                  pl.BlockSpec((tm, D_MODEL), lambda i: (rev(i), 0))] + [full(a) for a in pr],
        out_specs=[pl.BlockSpec((tm, D_Z), lambda i: (rev(i), 0))] + [acc(s) for s in acc_shapes],
        out_shape=[jax.ShapeDtypeStruct((t, D_Z), BF16)] + [jax.ShapeDtypeStruct(s, F32) for s in acc_shapes],
        scratch_shapes=[big(BF16), big(F32), big(BF16), big(F32), big(BF16), big(F32), big(F32), big(F32), big(F32),
                        big(F32), big(BF16), big(BF16), big(F32),
                        pltpu.VMEM(vec, F32), pltpu.VMEM(vec, F32), pltpu.VMEM(vec, F32)],
        compiler_params=_params("arbitrary"),
    )(z, z, h, h, dy, *pr)


def _inproj_bwd(dz, wg_in, x, dh1, pre_g, tm, tile0, nt, prev, last, token, name):
    t = x.shape[0]

    def body(*refs):
        dz_ref, w_ref, x_ref, dh1_ref, g_ref = refs[:5]
        gx_ref, gpre_ref, acc_s = refs[-3:]
        i = pl.program_id(0)

        @pl.when(i == 0)
        def _():
            gpre_ref[...] = jnp.zeros_like(gpre_ref) if prev is None else refs[7][...]

        acc = _dot_nt(dz_ref[:, 0:W_IN_COLS], w_ref[0])
        for k in range(1, N_CHIPS):
            acc = acc + _dot_nt(dz_ref[:, k * W_IN_COLS:(k + 1) * W_IN_COLS], w_ref[k])
        acc_s[...] = acc
        for s in range(tm // CHUNK):
            rows = slice(s * CHUNK, (s + 1) * CHUNK)
            xv = x_ref[rows, :]
            r = lax.rsqrt(_lanemean(xv * xv) + EPS)
            xhat = xv * r
            dhn = acc_s[rows, :]
            gpre_ref[...] += _rowsum8(dhn * xhat)
            dxh = dhn * g_ref[...]
            gx_ref[rows, :] = dh1_ref[rows, :] + r * (dxh - xhat * _lanemean(dxh * xhat))

        if last:
            @pl.when(i == nt - 1)
            def _():
                gpre_ref[...] = jnp.broadcast_to(jnp.sum(gpre_ref[...], axis=0, keepdims=True), gpre_ref.shape)

    row = lambda n: pl.BlockSpec((tm, n), lambda i: (tile0 + i, 0))
    small = lambda r: pl.BlockSpec((r, D_MODEL), lambda i: (0, 0))
    tok = pl.BlockSpec((SUBLANES, LANES), lambda i: (0, 0))
    in_specs = [row(D_Z), pl.BlockSpec(wg_in.shape, lambda i: (0, 0, 0), pipeline_mode=pl.Buffered(1)),
                row(D_MODEL), row(D_MODEL), small(1), tok]
    args = [dz, wg_in, x, dh1, pre_g, token]
    aliases = {}
    if prev is not None:
        in_specs += [ANY, small(SUBLANES)]
        args += list(prev)
        aliases = {6: 0}
    return pl.pallas_call(
        body, name=name, grid=(nt,), in_specs=in_specs, out_specs=[row(D_MODEL), small(SUBLANES)],
        out_shape=[jax.ShapeDtypeStruct((t, D_MODEL), F32), jax.ShapeDtypeStruct((SUBLANES, D_MODEL), F32)],
        input_output_aliases=aliases,
        scratch_shapes=[pltpu.VMEM((tm, D_MODEL), F32)],
        compiler_params=_params("arbitrary"),
    )(*args)


def _weight_grad(a, b, name, kb, nb, tk, tn, tt, token):
    t = a.shape[0]
    tt = min(tt, t)

    def body(a_ref, b_ref, token_ref, o_ref):
        @pl.when(pl.program_id(2) == 0)
        def _():
            o_ref[...] = jnp.zeros_like(o_ref)

        o_ref[...] += _dot_tn(a_ref[...], b_ref[...])

    return pl.pallas_call(
        body, name=name, grid=(nb, kb, t // tt),
        in_specs=[pl.BlockSpec((tt, tk), lambda j, i, s: (s, i)), pl.BlockSpec((tt, tn), lambda j, i, s: (s, j)),
                  pl.BlockSpec((SUBLANES, LANES), lambda j, i, s: (0, 0))],
        out_specs=pl.BlockSpec((None, None, tk, tn), lambda j, i, s: (j, i, 0, 0)),
        out_shape=jax.ShapeDtypeStruct((nb, kb, tk, tn), F32),
        compiler_params=_params("parallel", "parallel", "arbitrary"),
    )(a, b, token)


def _place():
    x, y, c = lax.axis_index("x"), lax.axis_index("y"), lax.axis_index("c")
    return x, y, c


def _chip_of(x, y):
    return 2 * x + y


def _gather_weights(w_in, conv_w):
    halves = [(D_MODEL // 2, W_IN_COLS)]
    nw = len(halves)

    def body(win_ref, cw_ref, gin_ref, gcw_ref, s0, b0, lsem, send_sems, recv_sems, cw_send, cw_recv):
        x, y, c = _place()
        me = _chip_of(x, y)
        sibling = (x, y, 1 - c)
        chips = [(1 - x, y), (x, 1 - y), (1 - x, 1 - y)]
        srcs = (win_ref,)
        stage = (s0,)
        bf = (b0,)
        outs = (gin_ref,)
        loads = []
        for n in range(nw):
            rows = halves[n][0]
            cp = pltpu.make_async_copy(srcs[n].at[pl.ds(c * rows, rows), :], stage[n], lsem.at[n])
            cp.start()
            loads.append(cp)
        own_cw = pltpu.make_async_copy(cw_ref, gcw_ref.at[me], lsem.at[2 * nw])
        own_cw.start()
        for n in range(nw):
            loads[n].wait()
            bf[n][...] = stage[n][...].astype(BF16)

        def copy(n, k, chip, to, src=None):
            dst = outs[n].at[chip, c]
            return pltpu.make_async_remote_copy(
                src_ref=dst if src is None else src, dst_ref=dst,
                send_sem=send_sems.at[n, k], recv_sem=recv_sems.at[n, k], device_id=to, device_id_type=MESH)

        def recv(n, k, chip, core):
            dst = outs[n].at[chip, core]
            return pltpu.make_async_remote_copy(
                src_ref=dst, dst_ref=dst, send_sem=send_sems.at[n, k], recv_sem=recv_sems.at[n, k],
                device_id=sibling, device_id_type=MESH)

        sends = []
        locals_ = []
        for n in range(nw):
            lc = pltpu.make_async_copy(bf[n], outs[n].at[me, c], lsem.at[nw + n])
            lc.start()
            locals_.append(lc)
            first = [copy(n, 0, me, sibling, src=bf[n])]
            first += [copy(n, 1 + j, me, (*chip, c), src=bf[n]) for j, chip in enumerate(chips)]
            for cp in first:
                cp.start()
            sends += first
        cws = []
        for j, chip in enumerate(chips):
            cp = pltpu.make_async_remote_copy(
                src_ref=cw_ref, dst_ref=gcw_ref.at[me], send_sem=cw_send.at[j], recv_sem=cw_recv.at[j],
                device_id=(*chip, c), device_id_type=MESH)
            cp.start()
            cws.append(cp)
        for n in range(nw):
            for j, chip in enumerate(chips):
                kj = _chip_of(*chip)
                recv(n, 1 + j, kj, c).wait_recv()
                fw = copy(n, 4 + j, kj, sibling)
                fw.start()
                sends.append(fw)
        for n in range(nw):
            recv(n, 0, me, 1 - c).wait_recv()
            for j, chip in enumerate(chips):
                recv(n, 4 + j, _chip_of(*chip), 1 - c).wait_recv()
        for j, chip in enumerate(chips):
            pltpu.make_async_remote_copy(
                src_ref=cw_ref, dst_ref=gcw_ref.at[_chip_of(*chip)], send_sem=cw_send.at[j], recv_sem=cw_recv.at[j],
                device_id=(*chip, c), device_id_type=MESH).wait_recv()
        for cp in sends + cws:
            cp.wait_send()
        for lc in locals_:
            lc.wait()
        own_cw.wait()

    out_shape = [jax.ShapeDtypeStruct((N_CHIPS, 2) + hs, BF16) for hs in halves]
    out_shape.append(jax.ShapeDtypeStruct((N_CHIPS, CONV_W, CONV_COLS), F32))
    scratch = [pltpu.VMEM(hs, F32) for hs in halves] + [pltpu.VMEM(hs, BF16) for hs in halves]
    scratch += [pltpu.SemaphoreType.DMA((2 * nw + 1,)), pltpu.SemaphoreType.DMA((nw, 7)),
                pltpu.SemaphoreType.DMA((nw, 7)), pltpu.SemaphoreType.DMA((3,)), pltpu.SemaphoreType.DMA((3,))]
    return pl.pallas_call(
        body, name="gather_w_in", in_specs=[ANY] * 2, out_specs=[ANY] * 2, out_shape=out_shape,
        scratch_shapes=scratch, compiler_params=pltpu.CompilerParams(vmem_limit_bytes=VMEM_LIMIT),
    )(w_in, conv_w)


HBM = pl.BlockSpec(memory_space=pltpu.HBM)
SEM = pl.BlockSpec(memory_space=pltpu.SEMAPHORE)
EFFECT = pltpu.SideEffectType.DATAFLOW_SIDE_EFFECTING


def _hbm(a):
    return pltpu.with_memory_space_constraint(a, pltpu.HBM)


def _landing(shape, dtype):
    return _hbm(lax.empty(shape, dtype))


def _exchange_start(name, arrays, ncopies, build, after=None):
    n = len(arrays)
    extra = [] if after is None else [after]

    def body(*refs):
        ins, token = refs[:n], refs[-1]
        send_sems, recv_sems = refs[n + len(extra)], refs[n + len(extra) + 1]
        for cp in build(ins, send_sems, recv_sems):
            cp.start()
        token[...] = jnp.zeros_like(token)

    outs = pl.pallas_call(
        body, name=name,
        out_shape=(pltpu.SemaphoreType.DMA((ncopies,)), pltpu.SemaphoreType.DMA((ncopies,)),
                   *[pltpu.HBM(a.shape, a.dtype) for a in arrays], jax.ShapeDtypeStruct((SUBLANES, LANES), F32)),
        in_specs=[HBM] * n + [ANY] * len(extra),
        out_specs=(SEM, SEM, *[HBM] * n, pl.BlockSpec(memory_space=pltpu.VMEM)),
        input_output_aliases={q: q + 2 for q in range(n)},
        compiler_params=pltpu.CompilerParams(has_side_effects=EFFECT),
    )(*[_hbm(a) for a in arrays], *extra)
    return (outs[0], outs[1], list(outs[2:2 + n])), outs[-1]


def _exchange_wait(name, started, after, build):
    send, recv, arrays = started
    n = len(arrays)

    def body(*refs):
        ins, send_sems, recv_sems = refs[:n], refs[n], refs[n + 1]
        for cp in build(ins, send_sems, recv_sems):
            cp.wait_send()
            cp.wait_recv()

    return pl.pallas_call(
        body, name=name, out_shape=tuple(pltpu.HBM(a.shape, a.dtype) for a in arrays),
        in_specs=[HBM] * n + [SEM, SEM, ANY], out_specs=tuple([HBM] * n),
        input_output_aliases={q: q for q in range(n)},
        compiler_params=pltpu.CompilerParams(has_side_effects=EFFECT),
    )(*arrays, send, recv, after)


def _exchange_wait_start(name, started, after, build_wait, ncopies, build_start):
    send, recv, arrays = started
    n = len(arrays)

    def body(*refs):
        ins, send_sems, recv_sems = refs[:n], refs[n], refs[n + 1]
        send2, recv2, token = refs[n + 3], refs[n + 4], refs[-1]
        for cp in build_wait(ins, send_sems, recv_sems):
            cp.wait_send()
            cp.wait_recv()
        for cp in build_start(ins, send2, recv2):
            cp.start()
        token[...] = jnp.zeros_like(token)

    outs = pl.pallas_call(
        body, name=name,
        out_shape=(pltpu.SemaphoreType.DMA((ncopies,)), pltpu.SemaphoreType.DMA((ncopies,)),
                   *[pltpu.HBM(a.shape, a.dtype) for a in arrays], jax.ShapeDtypeStruct((SUBLANES, LANES), F32)),
        in_specs=[HBM] * n + [SEM, SEM, ANY], out_specs=(SEM, SEM, *[HBM] * n, pl.BlockSpec(memory_space=pltpu.VMEM)),
        input_output_aliases={q: q + 2 for q in range(n)},
        compiler_params=pltpu.CompilerParams(has_side_effects=EFFECT),
    )(*arrays, send, recv, after)
    return (outs[0], outs[1], list(outs[2:2 + n])), outs[-1]


def _cast_into_slot(w, kc, name, dtype=BF16):
    rows, cols = w.shape
    tr = min(rows, 256)

    def body(kc_ref, w_ref, o_ref):
        o_ref[...] = w_ref[...].astype(dtype)

    grid_spec = pltpu.PrefetchScalarGridSpec(
        num_scalar_prefetch=1, grid=(rows // tr,),
        in_specs=[pl.BlockSpec((tr, cols), lambda r, kc: (r, 0))],
        out_specs=pl.BlockSpec((None, tr, cols), lambda r, kc: (kc[0], r, 0)))
    return pl.pallas_call(
        body, name=name, grid_spec=grid_spec, out_shape=jax.ShapeDtypeStruct((N_CHIPS, rows, cols), dtype),
        compiler_params=_params("arbitrary"),
    )(kc, w)


def _gather_ici_copies(n):
    def build(refs, send_sems, recv_sems):
        x, y, c = _place()
        mine = lambda b: refs[b].at[_chip_of(x, y), c]
        chips = [(1 - x, y), (x, 1 - y), (1 - x, 1 - y)]
        return [pltpu.make_async_remote_copy(
            src_ref=mine(b), dst_ref=mine(b), send_sem=send_sems.at[3 * b + j], recv_sem=recv_sems.at[3 * b + j],
            device_id=(*chip, c), device_id_type=MESH) for b in range(n) for j, chip in enumerate(chips)]
    return build


def _gather_relay_copies(n):
    def build(refs, send_sems, recv_sems):
        x, y, c = _place()
        chips = [(1 - x, y), (x, 1 - y), (1 - x, 1 - y)]
        cps = []
        for b in range(n):
            for j, chip in enumerate(chips):
                got = refs[b].at[_chip_of(*chip), c]
                cps.append(pltpu.make_async_remote_copy(
                    src_ref=got, dst_ref=got, send_sem=send_sems.at[3 * b + j], recv_sem=recv_sems.at[3 * b + j],
                    device_id=(x, y, 1 - c), device_id_type=MESH))
        return cps
    return build


def _sibling_copies(n):
    def build(refs, send_sems, recv_sems):
        x, y, c = _place()
        return [pltpu.make_async_remote_copy(
            src_ref=refs[b].at[:, 1 - c], dst_ref=refs[n + b], send_sem=send_sems.at[b], recv_sem=recv_sems.at[b],
            device_id=(x, y, 1 - c), device_id_type=MESH) for b in range(n)]
    return build


def _chip_copies(n):
    def build(refs, send_sems, recv_sems):
        x, y, c = _place()
        chips = [(1 - x, y), (x, 1 - y), (1 - x, 1 - y)]
        return [pltpu.make_async_remote_copy(
            src_ref=refs[b].at[_chip_of(*chip)], dst_ref=refs[n + b].at[j],
            send_sem=send_sems.at[3 * b + j], recv_sem=recv_sems.at[3 * b + j],
            device_id=(*chip, c), device_id_type=MESH) for b in range(n) for j, chip in enumerate(chips)]
    return build


def _finish_copies(n, n_all):
    def build(refs, send_sems, recv_sems):
        x, y, c = _place()
        cps = [pltpu.make_async_remote_copy(
            src_ref=refs[b].at[c], dst_ref=refs[b].at[c], send_sem=send_sems.at[b], recv_sem=recv_sems.at[b],
            device_id=(x, y, 1 - c), device_id_type=MESH) for b in range(n)]
        flips = [(fx, fy, fc) for fx in (0, 1) for fy in (0, 1) for fc in (0, 1)][1:]
        for b in range(n_all):
            mine = refs[n + b].at[_chip_of(x, y), c]
            cps += [pltpu.make_async_remote_copy(
                src_ref=mine, dst_ref=mine, send_sem=send_sems.at[n + 7 * b + q], recv_sem=recv_sems.at[n + 7 * b + q],
                device_id=(x ^ fx, y ^ fy, c ^ fc), device_id_type=MESH) for q, (fx, fy, fc) in enumerate(flips)]
        return cps
    return build


def _pair_sum(g, r1, kc, name, tr, send_dtype):
    nk, _, rows, cols = g.shape

    def body(kc_ref, g_ref, r_ref, p_ref, own_ref):
        s = g_ref[...] + r_ref[...]
        p_ref[...] = s.astype(send_dtype)

        @pl.when(pl.program_id(1) == kc_ref[0])
        def _():
            own_ref[...] = s

    grid_spec = pltpu.PrefetchScalarGridSpec(
        num_scalar_prefetch=1, grid=(rows // tr, nk),
        in_specs=[pl.BlockSpec((None, None, tr, cols), lambda r, k, kc: (k, kc[1], r, 0)),
                  pl.BlockSpec((None, tr, cols), lambda r, k, kc: (k, r, 0))],
        out_specs=[pl.BlockSpec((None, tr, cols), lambda r, k, kc: (k, r, 0)),
                   pl.BlockSpec((tr, cols), lambda r, k, kc: (r, 0))])
    return pl.pallas_call(
        body, name=name, grid_spec=grid_spec,
        out_shape=[jax.ShapeDtypeStruct((nk, rows, cols), send_dtype), jax.ShapeDtypeStruct((rows, cols), F32)],
        compiler_params=_params("arbitrary", "arbitrary"),
    )(kc, g, r1)


def _chip_sum(own, r2, slot, lead, name, tr):
    rows, cols = own.shape
    nl = len(lead)

    def body(slot_ref, o_ref, r_ref, s_ref):
        s = o_ref[...]
        for j in range(3):
            s = s + r_ref[j].astype(F32)
        s_ref[...] = s

    grid_spec = pltpu.PrefetchScalarGridSpec(
        num_scalar_prefetch=1, grid=(rows // tr,),
        in_specs=[pl.BlockSpec((tr, cols), lambda r, sl: (r, 0)), pl.BlockSpec((3, tr, cols), lambda r, sl: (0, r, 0))],
        out_specs=pl.BlockSpec((None,) * nl + (tr, cols), lambda r, sl: tuple(sl[q] for q in range(nl)) + (r, 0)))
    return pl.pallas_call(
        body, name=name, grid_spec=grid_spec, out_shape=jax.ShapeDtypeStruct(tuple(lead) + (rows, cols), F32),
        compiler_params=_params("arbitrary"),
    )(slot, own, r2)


def _adam_update(w, g, m, v):
    nm = ADAM_B1 * m + (1.0 - ADAM_B1) * g
    nv = ADAM_B2 * v + (1.0 - ADAM_B2) * (g * g)
    m_hat = nm / (1.0 - ADAM_B1 ** ADAM_STEP)
    v_hat = nv / (1.0 - ADAM_B2 ** ADAM_STEP)
    return -ADAM_LR * (m_hat / (jnp.sqrt(v_hat) + ADAM_EPS) + ADAM_WD * w), nm, nv


def _adamw(w, g, m, v, name, tr, token):
    rows, cols = w.shape

    def body(w_ref, g_ref, m_ref, v_ref, token_ref, go_ref, d_ref, nm_ref, nv_ref):
        gv = g_ref[...]
        go_ref[...] = gv
        d_ref[...], nm_ref[...], nv_ref[...] = _adam_update(w_ref[...], gv, m_ref[...], v_ref[...])

    spec = pl.BlockSpec((tr, cols), lambda r: (r, 0))
    return pl.pallas_call(
        body, name=name, grid=(rows // tr,),
        in_specs=[spec] * 4 + [pl.BlockSpec((SUBLANES, LANES), lambda r: (0, 0))], out_specs=[spec] * 4,
        out_shape=[jax.ShapeDtypeStruct((rows, cols), F32)] * 4,
        compiler_params=_params("parallel"),
    )(w, g, m, v, token)


def _adamw_small(packed_g, pre_g_parts, ws, ms, vs):
    names = ["pre_g"] + [n for n, _ in SMALL_ROWS if n != "conv_w"]
    rows = dict(SMALL_ROWS)
    offset, at = {}, 0
    for n, r in SMALL_ROWS:
        offset[n] = at
        at += r
    k = len(names)

    def body(*refs):
        g_ref, pg_ref = refs[0], refs[1]
        w_refs, m_refs, v_refs = refs[2:2 + k], refs[2 + k:2 + 2 * k], refs[2 + 2 * k:2 + 3 * k]
        outs = refs[2 + 3 * k:]
        go, do, mo, vo = outs[:k], outs[k:2 * k], outs[2 * k:3 * k], outs[3 * k:4 * k]
        pre = pg_ref[0]
        for dev in range(1, 8):
            pre = pre + pg_ref[dev]
        outs[4 * k][...] = pre[D_MODEL // LANES:, :]
        for i, n in enumerate(names):
            shp = w_refs[i].shape
            if len(shp) == 2 and shp[0] == 1:
                for r in range(shp[1] // LANES):
                    cols = slice(r * LANES, (r + 1) * LANES)
                    g = pre[r:r + 1, :] if n == "pre_g" else g_ref[offset[n] + r:offset[n] + r + 1, :]
                    go[i][:, cols] = g
                    do[i][:, cols], mo[i][:, cols], vo[i][:, cols] = _adam_update(
                        w_refs[i][:, cols], g, m_refs[i][:, cols], v_refs[i][:, cols])
            else:
                g = g_ref[offset[n]:offset[n] + rows[n], :].reshape(shp)
                go[i][...] = g
                do[i][...], mo[i][...], vo[i][...] = _adam_update(w_refs[i][...], g, m_refs[i][...], v_refs[i][...])

    vm = pl.BlockSpec(memory_space=pltpu.VMEM)
    args = [packed_g, pre_g_parts] + [src[n] for src in (ws, ms, vs) for n in names]
    out_shape = [jax.ShapeDtypeStruct(ws[n].shape, F32) for _ in range(4) for n in names]
    out_shape.append(jax.ShapeDtypeStruct((SUBLANES, LANES), F32))
    outs = pl.pallas_call(
        body, name="adamw_small", in_specs=[vm] * len(args), out_specs=[vm] * (4 * k + 1), out_shape=out_shape,
    )(*args)
    return [dict(zip(names, outs[q * k:(q + 1) * k])) for q in range(4)], outs[4 * k]


def _into_slot(v, tail, slot, lead, name):
    n = v.shape[1]
    nl = len(lead)
    rows = n // LANES + SUBLANES

    def body(slot_ref, v_ref, t_ref, o_ref):
        for r in range(n // LANES):
            o_ref[r:r + 1, :] = v_ref[0:1, r * LANES:(r + 1) * LANES]
        o_ref[n // LANES:, :] = t_ref[...]

    grid_spec = pltpu.PrefetchScalarGridSpec(
        num_scalar_prefetch=1, grid=(1,),
        in_specs=[pl.BlockSpec(v.shape, lambda i, sl: (0, 0)), pl.BlockSpec(tail.shape, lambda i, sl: (0, 0))],
        out_specs=pl.BlockSpec((None,) * nl + (rows, LANES), lambda i, sl: tuple(sl[q] for q in range(nl)) + (0, 0)))
    return pl.pallas_call(
        body, name=name, grid_spec=grid_spec, out_shape=jax.ShapeDtypeStruct(tuple(lead) + (rows, LANES), F32),
    )(slot, v, tail)


def _rows128(a):
    return a.reshape(-1, LANES)


def _pack_small(parts):
    pieces = [_rows128(parts[n]) for n, _ in SMALL_ROWS]
    pieces.append(jnp.zeros((SMALL_TOTAL - SMALL_USED, LANES), F32))
    return jnp.concatenate(pieces, axis=0)


def kernel(x, p, pre_g, w_in, gmlp_ln_g, gmlp_ln_b, gmlp_ws, gmlp_bs, conv_w, conv_b, w_a, b_a, w_x, b_x, lam, gmlp_out_g, lru_out_g, w_out, post_g, w_pe, w_pg, loss_target, m_pre_g, m_w_in, m_gmlp_ln_g, m_gmlp_ln_b, m_gmlp_ws, m_gmlp_bs, m_conv_w, m_conv_b, m_w_a, m_b_a, m_w_x, m_b_x, m_lam, m_gmlp_out_g, m_lru_out_g, m_w_out, m_post_g, m_w_pe, m_w_pg, v_pre_g, v_w_in, v_gmlp_ln_g, v_gmlp_ln_b, v_gmlp_ws, v_gmlp_bs, v_conv_w, v_conv_b, v_w_a, v_b_a, v_w_x, v_b_x, v_lam, v_gmlp_out_g, v_lru_out_g, v_w_out, v_post_g, v_w_pe, v_w_pg):
    weights = dict(pre_g=pre_g, w_in=w_in, gmlp_ln_g=gmlp_ln_g, gmlp_ln_b=gmlp_ln_b, gmlp_ws=gmlp_ws, gmlp_bs=gmlp_bs,
                   conv_w=conv_w, conv_b=conv_b, w_a=w_a, b_a=b_a, w_x=w_x, b_x=b_x, lam=lam, gmlp_out_g=gmlp_out_g,
                   lru_out_g=lru_out_g, w_out=w_out, post_g=post_g, w_pe=w_pe, w_pg=w_pg)
    mom_m = dict(pre_g=m_pre_g, w_in=m_w_in, gmlp_ln_g=m_gmlp_ln_g, gmlp_ln_b=m_gmlp_ln_b, gmlp_ws=m_gmlp_ws,
                 gmlp_bs=m_gmlp_bs, conv_w=m_conv_w, conv_b=m_conv_b, w_a=m_w_a, b_a=m_b_a, w_x=m_w_x, b_x=m_b_x,
                 lam=m_lam, gmlp_out_g=m_gmlp_out_g, lru_out_g=m_lru_out_g, w_out=m_w_out, post_g=m_post_g,
                 w_pe=m_w_pe, w_pg=m_w_pg)
    mom_v = dict(pre_g=v_pre_g, w_in=v_w_in, gmlp_ln_g=v_gmlp_ln_g, gmlp_ln_b=v_gmlp_ln_b, gmlp_ws=v_gmlp_ws,
                 gmlp_bs=v_gmlp_bs, conv_w=v_conv_w, conv_b=v_conv_b, w_a=v_w_a, b_a=v_b_a, w_x=v_w_x, b_x=v_b_x,
                 lam=v_lam, gmlp_out_g=v_gmlp_out_g, lru_out_g=v_lru_out_g, w_out=v_w_out, post_g=v_post_g,
                 w_pe=v_w_pe, w_pg=v_w_pg)
    order = list(weights)
    xi, yi, ci = _place()
    me = _chip_of(xi, yi)
    kc = jnp.stack([me, ci]).astype(jnp.int32)

    x2 = x[0]
    p2 = p[0, 0]
    tgt = loss_target[0]

    first = [_cast_into_slot(w_in[0], kc, "cast_w_in").reshape(N_CHIPS, 2, D_MODEL // 2, W_IN_COLS),
             _cast_into_slot(conv_w[0, :, 0, :], kc, "conv_w_into_slot", F32).reshape(N_CHIPS, 2, CONV_W // 2, CONV_COLS)]
    in_st, in_tok = _exchange_start("gather_in_start", first, 6, _gather_ici_copies(2))
    later = [_cast_into_slot(w_out[0], kc, "cast_w_out").reshape(N_CHIPS, 2, W_ROWS // 2, D_MODEL),
             _cast_into_slot(w_pg[0], kc, "cast_w_pg").reshape(N_CHIPS, 2, W_ROWS // 2, D_MODEL),
             _cast_into_slot(w_pe[0], kc, "cast_w_pe").reshape(N_CHIPS, 2, D_PLE // 2, W_PE_COLS)]
    gather_st, gather_tok = _exchange_start("gather_start", later, 9, _gather_ici_copies(3), after=in_tok)
    hn, z_own = _inproj_local(x2, pre_g, w_in[0], 256, gather_tok)
    in_st, in_tok = _exchange_wait_start("gather_in_relay", in_st, z_own, _gather_ici_copies(2), 6,
                                         _gather_relay_copies(2))
    g_in, g_cw = _exchange_wait("gather_in_wait", in_st, in_tok, _gather_relay_copies(2))
    wg_in = g_in.reshape(N_CHIPS, D_MODEL, W_IN_COLS)
    cw_full = jnp.transpose(g_cw.reshape(N_CHIPS, CONV_W, CONV_COLS), (1, 0, 2)).reshape(CONV_W, D_HALF)

    causal = jnp.tril(jnp.ones((CHUNK, CHUNK), dtype=bool))
    ws_m = jnp.where(causal[None], gmlp_ws[0], 0.0)
    prm = dict(
        ln_g=gmlp_ln_g, ln_b=gmlp_ln_b, wt=ws_m.astype(BF16), wtt=jnp.transpose(ws_m, (0, 2, 1)).astype(BF16),
        bsx=jnp.repeat(jnp.transpose(gmlp_bs[0]), CHUNK, axis=1),
        conv_w=cw_full, conv_b=conv_b, w_a=w_a[0].astype(BF16), w_x=w_x[0].astype(BF16),
        b_a=b_a[0].reshape(1, D_HALF), b_x=b_x[0].reshape(1, D_HALF), lam=lam, oga=gmlp_out_g, ogb=lru_out_g)

    z, y, h = _inproj_branches_fwd(hn, z_own, wg_in, kc, prm, 256, gather_tok)
    gather_st, gather_tok = _exchange_wait_start("gather_relay", gather_st, y, _gather_ici_copies(3), 9,
                                                 _gather_relay_copies(3))
    g_out, g_pg, g_pe = _exchange_wait("gather_wait", gather_st, gather_tok, _gather_relay_copies(3))
    wg_out = g_out.reshape(D_MODEL, D_MODEL)
    wg_pg = g_pg.reshape(D_MODEL, D_MODEL)
    wg_pe = g_pe.reshape(N_CHIPS, D_PLE, W_PE_COLS)
    o, h1, gt, dout, loss_acc = _outproj_fwd(x2, y, p2, tgt, post_g, wg_out, wg_pg, wg_pe, 256)

    def sibling_start(tag, bufs):
        lands = [_landing((b.shape[0],) + b.shape[2:], b.dtype) for b in bufs]
        return _exchange_start("sibling_start_" + tag, bufs + lands, len(bufs), _sibling_copies(len(bufs)))

    def pair_then_chip_start(tag, started, after, names, tiles, dtypes):
        n = len(names)
        got = _exchange_wait("sibling_wait_" + tag, started, after, _sibling_copies(n))
        pairs = [_pair_sum(got[b], got[n + b], kc, "pair_sum_" + names[b], tiles[b], dtypes[b]) for b in range(n)]
        lands = [_landing((3,) + pr[0].shape[1:], pr[0].dtype) for pr in pairs]
        return _exchange_start("chip_start_" + tag, [pr[0] for pr in pairs] + lands, 3 * n, _chip_copies(n)), pairs

    def sum_then_finish_start(tag, started, pairs, after, names, tiles, small, to_all=()):
        n = len(names)
        got = _exchange_wait("chip_wait_" + tag, started, after, _chip_copies(n))
        sums = [_chip_sum(pairs[b][1], got[n + b], kc if small and b == n - 1 else kc[1:],
                          (N_CHIPS, 2) if small and b == n - 1 else (2,), "chip_sum_" + names[b], tiles[b])
                for b in range(n)]
        nbig = n - 1 if small else n
        n_all = n - nbig + len(to_all)
        return _exchange_start("finish_start_" + tag, sums + list(to_all), nbig + 7 * n_all,
                               _finish_copies(nbig, n_all))

    gw_pe, dq, dh1, do, dy, g_post = _head_bwd(dout, gt, p2, o, post_g, wg_out, wg_pg, wg_pe, 256)
    gw_pe = gw_pe.reshape(N_CHIPS, 2, D_PLE // 2, W_PE_COLS)
    token0 = jnp.zeros((SUBLANES, LANES), F32)
    gw_out = _weight_grad(y, do, "grad_w_out", 2, 1, D_MODEL // 2, D_MODEL, 1024, token0)
    gw_pg = _weight_grad(h1, dq, "grad_w_pg", 2, 1, D_MODEL // 2, D_MODEL, 1024, token0)
    gw_out = gw_out.reshape(N_CHIPS, 2, W_ROWS // 2, D_MODEL)
    gw_pg = gw_pg.reshape(N_CHIPS, 2, W_ROWS // 2, D_MODEL)

    names_a, tiles_a = ["w_out", "w_pg", "w_pe"], [128, 128, 128]
    st, tok = sibling_start("a", [gw_out, gw_pg, gw_pe])
    (dz, g_oga, g_ogb, g_lng, g_lnb, g_bsx, g_ws, g_cw, g_cb, g_wa, g_ba, g_wx, g_bx, g_lam) = _branches_bwd(
        z, h, dy, prm, 256, tok)
    (st, tok), pairs_a = pair_then_chip_start("a", st, dz, names_a, tiles_a, [BF16] * 3)
    gw_in = _weight_grad(hn, dz, "grad_w_in", 2, N_CHIPS, D_MODEL // 2, W_IN_COLS, 1024, tok)
    fin_a, tok = sum_then_finish_start("a", st, pairs_a, gw_in, names_a, tiles_a, False)

    small_g = dict(
        gmlp_ln_g=g_lng[0:1], gmlp_ln_b=g_lnb[0:1], gmlp_ws=g_ws,
        gmlp_bs=jnp.transpose(g_bsx[:, ::CHUNK]), conv_w=g_cw[::SUBLANES], conv_b=g_cb[0:1], w_a=g_wa, b_a=g_ba[0:1],
        w_x=g_wx, b_x=g_bx[0:1], lam=g_lam[0:1], gmlp_out_g=g_oga[0:1], lru_out_g=g_ogb[0:1], post_g=g_post[0:1])
    gsm = _pack_small(small_g).reshape(N_CHIPS, 2, SMALL_PIECE, LANES)

    names_b, tiles_b = ["w_in", "small"], [256, SMALL_PIECE]
    n_tiles = x2.shape[0] // 256
    n_lo = max(1, (5 * n_tiles) // 16)
    st, tok_b = _exchange_start(
        "sibling_start_b", [gw_in, gsm] + [_landing((N_CHIPS,) + b.shape[2:], F32) for b in (gw_in, gsm)], 2,
        _sibling_copies(2), after=tok)
    part = _inproj_bwd(dz, wg_in, x2, dh1, pre_g, 256, 0, n_lo, None, False, tok_b, "inproj_bwd_lo")
    f_out, f_pg, f_pe = _exchange_wait("finish_wait_a", fin_a, part[1], _finish_copies(3, 0))
    (st, tok_b), pairs_b = pair_then_chip_start("b", st, part[1], names_b, tiles_b, [BF16, F32])
    grad_x, g_pre = _inproj_bwd(dz, wg_in, x2, dh1, pre_g, 256, n_lo, n_tiles - n_lo, part, True, tok_b,
                                "inproj_bwd_hi")
    pre_parts = _into_slot(g_pre, loss_acc, kc, (N_CHIPS, 2), "pre_g_into_slot")
    fin_b, tok_b = sum_then_finish_start("b", st, pairs_b, g_pre, names_b, tiles_b, True, to_all=[pre_parts])

    grads, deltas, new_m, new_v = {}, {}, {}, {}

    def adam_big(n, g2d, tr, token):
        shp = weights[n].shape
        g, d, nm, nv = _adamw(weights[n][0], g2d, mom_m[n][0], mom_v[n][0], "adamw_" + n, tr, token)
        grads[n], deltas[n], new_m[n], new_v[n] = g.reshape(shp), d.reshape(shp), nm.reshape(shp), nv.reshape(shp)
        return d

    as_token = lambda d: d[:SUBLANES, :LANES]
    last = adam_big("w_out", f_out.reshape(W_ROWS, D_MODEL), 128, tok_b)
    last = adam_big("w_pg", f_pg.reshape(W_ROWS, D_MODEL), 128, as_token(last))
    last = adam_big("w_pe", f_pe.reshape(D_PLE, W_PE_COLS), 128, as_token(last))
    f_in, f_sm, pre_parts = _exchange_wait("finish_wait_b", fin_b, last, _finish_copies(1, 2))
    adam_big("w_in", f_in.reshape(D_MODEL, W_IN_COLS), 256, tok_b)

    packed_g = f_sm.reshape(SMALL_TOTAL, LANES)
    small_names = ["pre_g"] + [n for n, _ in SMALL_ROWS if n != "conv_w"]
    natural = lambda src: {n: (src[n] if src[n].ndim == 2 else src[n][0]) for n in small_names}
    outs, loss_block = _adamw_small(packed_g, pre_parts.reshape(8, D_MODEL // LANES + SUBLANES, LANES),
                                    natural(weights), natural(mom_m), natural(mom_v))
    loss = loss_block[0, 0]
    for dst, got in zip((grads, deltas, new_m, new_v), outs):
        for n in small_names:
            dst[n] = got[n].reshape(weights[n].shape)
    at = sum(r for n, r in SMALL_ROWS[:[n for n, _ in SMALL_ROWS].index("conv_w")])
    g_cw_all = packed_g[at:at + CONV_W * D_HALF // LANES].reshape(CONV_W, D_HALF)
    g_conv = lax.dynamic_slice_in_dim(g_cw_all, me * CONV_COLS, CONV_COLS, axis=1)
    g, d, nm, nv = _adamw(conv_w[0, :, 0, :], g_conv, m_conv_w[0, :, 0, :], v_conv_w[0, :, 0, :], "adamw_conv_w", CONV_W,
                          tok_b)
    cshape = conv_w.shape
    grads["conv_w"], deltas["conv_w"] = g.reshape(cshape), d.reshape(cshape)
    new_m["conv_w"], new_v["conv_w"] = nm.reshape(cshape), nv.reshape(cshape)

    return (loss, grad_x.reshape(x.shape), *[grads[n] for n in order], *[deltas[n] for n in order],
            *[new_m[n] for n in order], *[new_v[n] for n in order])
```

```python
import math

import jax
import jax.numpy as jnp
from jax import lax
from jax.experimental import pallas as pl
from jax.experimental.pallas import tpu as pltpu

F32 = jnp.float32
BF16 = jnp.bfloat16

D_MODEL = 2048
D_HALF = 1024
D_Z = 5120
D_PLE = 256
CHUNK = 128
N_HEADS = 8
N_CHIPS = 4
W_IN_COLS = D_Z // N_CHIPS
W_ROWS = D_MODEL // N_CHIPS
W_PE_COLS = D_MODEL // N_CHIPS
CONV_W = 4
CONV_COLS = D_HALF // N_CHIPS
EPS = 1e-6
LRU_C = 8.0
ADAM_LR, ADAM_B1, ADAM_B2, ADAM_EPS, ADAM_WD, ADAM_STEP = 0.001, 0.9, 0.999, 1e-08, 0.01, 10

SUBLANES = 8
LANES = 128
VMEM_LIMIT = 56 * 1024 * 1024
ROW_TILE = 256
CONTRACT_TILE = 1024
SUM_TILE = 128

SMALL_ROWS = (("gmlp_ln_g", 8), ("gmlp_ln_b", 8), ("gmlp_ws", 1024), ("gmlp_bs", 8),
              ("conv_w", 32), ("conv_b", 8), ("w_a", 1024), ("b_a", 8), ("w_x", 1024), ("b_x", 8),
              ("lam", 8), ("gmlp_out_g", 8), ("lru_out_g", 8), ("post_g", 16))
SMALL_USED = sum(r for _, r in SMALL_ROWS)
SMALL_PIECE = 400
SMALL_TOTAL = 8 * SMALL_PIECE

MESH = pl.DeviceIdType.MESH
ANY = pl.BlockSpec(memory_space=pl.ANY)

_GELU_C0 = math.sqrt(2.0 / math.pi)
_GELU_C1 = 0.044715


def _params(*sem):
    return pltpu.CompilerParams(dimension_semantics=sem, vmem_limit_bytes=VMEM_LIMIT)


def _dot(a, b):
    return jnp.dot(a, b, preferred_element_type=F32)


def _dot_nt(a, b):
    return lax.dot_general(a, b, (((1,), (1,)), ((), ())), preferred_element_type=F32)


def _dot_tn(a, b):
    return lax.dot_general(a, b, (((0,), (0,)), ((), ())), preferred_element_type=F32)


def _gelu(x):
    t = jnp.tanh(_GELU_C0 * (x + _GELU_C1 * (x * x * x)))
    return 0.5 * x * (1.0 + t), t


def _gelu_grad(x, t):
    return 0.5 * (1.0 + t) + 0.5 * x * (1.0 - t * t) * (_GELU_C0 * (1.0 + 3.0 * _GELU_C1 * x * x))


def _rowsum8(v):
    r, n = v.shape
    return jnp.sum(v.reshape(r // SUBLANES, SUBLANES, n), axis=0)


def _lanemean(v):
    return jnp.mean(v, axis=-1, keepdims=True)


def _shift_down(v, halo8, k):
    if k == 0:
        return v
    r = pltpu.roll(v, k, 0)
    hr = pltpu.roll(halo8, k, 0)
    row = lax.broadcasted_iota(jnp.int32, halo8.shape, 0)
    top = jnp.where(row < k, hr, r[0:SUBLANES])
    return jnp.concatenate([top, r[SUBLANES:]], axis=0)


def _shift_up(v, next8, k):
    if k == 0:
        return v
    n = v.shape[0]
    r = pltpu.roll(v, n - k, 0)
    nr = pltpu.roll(next8, SUBLANES - k, 0)
    row = lax.broadcasted_iota(jnp.int32, next8.shape, 0)
    bot = jnp.where(row >= SUBLANES - k, nr, r[n - SUBLANES:])
    return jnp.concatenate([r[:n - SUBLANES], bot], axis=0)


def _layernorm_parts(vg):
    mu = _lanemean(vg)
    xc = vg - mu
    rstd = lax.rsqrt(_lanemean(xc * xc) + EPS)
    return xc * rstd, rstd


def _spatial_mix(wt_ref, vn_ref, bsx_ref, mixed_ref, tm):
    for c in range(tm // CHUNK):
        rows = slice(c * CHUNK, (c + 1) * CHUNK)
        for h in range(N_HEADS):
            cols = slice(h * CHUNK, (h + 1) * CHUNK)
            mixed_ref[rows, cols] = _dot(wt_ref[h], vn_ref[rows, cols]) + bsx_ref[:, cols]


def _conv_taps(xb, halo8):
    return [_shift_down(xb, halo8, CONV_W - 1 - k) for k in range(CONV_W)]


def _lru_gates(xc_bf_ref, wa_ref, wx_ref, ba_ref, bx_ref, r_ref, i_ref):
    for h in range(N_HEADS):
        cols = slice(h * CHUNK, (h + 1) * CHUNK)
        xh = xc_bf_ref[:, cols]
        r_ref[:, cols] = jax.nn.sigmoid(_dot(xh, wa_ref[h]) + ba_ref[:, cols])
        i_ref[:, cols] = jax.nn.sigmoid(_dot(xh, wx_ref[h]) + bx_ref[:, cols])


def _softplus_neg(lam):
    return jnp.maximum(-lam, 0.0) + jnp.log(1.0 + jnp.exp(-jnp.abs(lam)))


def _decay_parts(r, lam):
    la = (-LRU_C * _softplus_neg(lam)) * r
    a = jnp.exp(la)
    th = -jnp.tanh(la)
    mult = jnp.sqrt(2.0 * th / (1.0 + th))
    return a, mult


def _z_group(zref, g, rows=slice(None)):
    lo = g * D_HALF
    blk, off = lo // W_IN_COLS, lo % W_IN_COLS
    if off + D_HALF <= W_IN_COLS:
        return zref[blk, rows, off:off + D_HALF]
    return jnp.concatenate([zref[blk, rows, off:W_IN_COLS], zref[blk + 1, rows, 0:off + D_HALF - W_IN_COLS]], axis=1)


def _inproj_local(x, pre_g, w_own, tm, token):
    t = x.shape[0]

    def body(x_ref, g_ref, w_ref, token_ref, hn_ref, zl_ref, wbf_s):
        @pl.when(pl.program_id(0) == 0)
        def _():
            wbf_s[...] = w_ref[...].astype(BF16)

        xv = x_ref[...]
        hn = (xv * lax.rsqrt(_lanemean(xv * xv) + EPS) * g_ref[...]).astype(BF16)
        hn_ref[...] = hn
        zl_ref[...] = _dot(hn, wbf_s[...]).astype(BF16)

    row = lambda n: pl.BlockSpec((tm, n), lambda i: (i, 0))
    const = lambda shp: pl.BlockSpec(shp, lambda i: (0, 0), pipeline_mode=pl.Buffered(1))
    return pl.pallas_call(
        body, name="inproj_local", grid=(t // tm,),
        in_specs=[row(D_MODEL), const((1, D_MODEL)), const((D_MODEL, W_IN_COLS)), const((SUBLANES, LANES))],
        out_specs=[row(D_MODEL), row(W_IN_COLS)],
        out_shape=[jax.ShapeDtypeStruct((t, D_MODEL), BF16), jax.ShapeDtypeStruct((t, W_IN_COLS), BF16)],
        scratch_shapes=[pltpu.VMEM((D_MODEL, W_IN_COLS), BF16)],
        compiler_params=_params("arbitrary"),
    )(x, pre_g, w_own, token)


def _inproj_branches_fwd(hn, z_own, wg_in, kc, prm, tm, token):
    t = hn.shape[0]
    nt = t // tm
    hb = tm // SUBLANES

    def body(kc_ref, hn_ref, zo_ref, w1_ref, w2_ref, w3_ref,
             lng_ref, lnb_ref, wt_ref, bsx_ref, cw_ref, cb_ref, wa_ref, wx_ref, ba_ref, bx_ref, lam_ref,
             oga_ref, ogb_ref, token_ref,
             z_ref, y_ref, h_ref,
             zbuf0, zbuf1, vn_s, mixed_s, xcbf_s, r_s, i_s, ug_s, halo_s, carry_s):
        s = pl.program_id(0)
        me = kc_ref[0]
        w_refs = (None, w1_ref, w2_ref, w3_ref)

        @pl.when(s == 0)
        def _():
            zbuf1[...] = jnp.zeros_like(zbuf1)

        @pl.when(s <= 1)
        def _():
            carry_s[...] = jnp.zeros_like(carry_s)
            halo_s[...] = jnp.zeros_like(halo_s)

        def step(zw, zr):
            def project(r):
                blk = (me + r) % N_CHIPS
                zb = zo_ref[...] if r == 0 else _dot(hn_ref[...], w_refs[r][...]).astype(BF16)
                z_ref[blk] = zb
                zw[blk] = zb

            zin = lambda g: _z_group(zr, g).astype(F32)
            always = [s >= 0] * 4

            @pl.when(always[0])
            def _():
                project(0)
                ug, _ = _gelu(zin(0))
                ug_s[...] = ug
                vg, _ = _gelu(zin(1))
                vhat, _ = _layernorm_parts(vg)
                vn_s[...] = (vhat * lng_ref[...] + lnb_ref[...]).astype(BF16)

            @pl.when(always[1])
            def _():
                project(1)
                _spatial_mix(wt_ref, vn_s, bsx_ref, mixed_s, tm)
                ga = zin(2)
                ya = ug_s[...] * mixed_s[...] * (ga * jax.nn.sigmoid(ga))
                ra = lax.rsqrt(_lanemean(ya * ya) + EPS)
                y_ref[:, 0:D_HALF] = (ya * ra * oga_ref[...]).astype(BF16)

            @pl.when(always[2])
            def _():
                project(2)
                xb = zin(3)
                taps = _conv_taps(xb, halo_s[...])
                halo_s[...] = xb[tm - SUBLANES:]
                xc = cb_ref[...] + taps[0] * cw_ref[0:1, :]
                for k in range(1, CONV_W):
                    xc = xc + taps[k] * cw_ref[k:k + 1, :]
                xcbf_s[...] = xc.astype(BF16)
                _lru_gates(xcbf_s, wa_ref, wx_ref, ba_ref, bx_ref, r_s, i_s)
                a, mult = _decay_parts(r_s[...], lam_ref[...])
                row = lax.broadcasted_iota(jnp.int32, a.shape, 0)
                mult = jnp.where(jnp.logical_and(s == 1, row == 0), 1.0, mult)
                r_s[...] = a
                i_s[...] = mult * (i_s[...] * xc)

            @pl.when(always[3])
            def _():
                project(3)
                a = r_s[...]
                b = i_s[...]
                r8 = lax.broadcasted_iota(jnp.int32, a.shape, 0) & (SUBLANES - 1)
                for d in (1, 2, 4):
                    a_sh = pltpu.roll(a, d, 0)
                    b_sh = pltpu.roll(b, d, 0)
                    m = r8 >= d
                    b = jnp.where(m, a * b_sh + b, b)
                    a = jnp.where(m, a * a_sh, a)
                carry = carry_s[...]
                for g in range(hb):
                    rows = slice(g * SUBLANES, (g + 1) * SUBLANES)
                    hg = a[rows] * carry + b[rows]
                    h_ref[rows, :] = hg
                    carry = jnp.broadcast_to(hg[SUBLANES - 1:SUBLANES, :], hg.shape)
                carry_s[...] = carry
                gb = zin(4)
                yb = h_ref[...] * (gb * jax.nn.sigmoid(gb))
                rb = lax.rsqrt(_lanemean(yb * yb) + EPS)
                y_ref[:, D_HALF:] = (yb * rb * ogb_ref[...]).astype(BF16)

        @pl.when(s % 2 == 0)
        def _():
            step(zbuf0, zbuf1)

        @pl.when(s % 2 == 1)
        def _():
            step(zbuf1, zbuf0)

    const = lambda a: pl.BlockSpec(a.shape, lambda s, kc, n=a.ndim: (0,) * n, pipeline_mode=pl.Buffered(1))
    proj = lambda n: pl.BlockSpec((tm, n), lambda s, kc: (jnp.minimum(s, nt - 1), 0))
    head = lambda n: pl.BlockSpec((tm, n), lambda s, kc: (jnp.maximum(s - 1, 0), 0))
    other = lambda r: pl.BlockSpec((None, D_MODEL, W_IN_COLS), lambda s, kc, r=r: ((kc[0] + r) % N_CHIPS, 0, 0),
                                   pipeline_mode=pl.Buffered(1))
    names = ("ln_g", "ln_b", "wt", "bsx", "conv_w", "conv_b", "w_a", "w_x", "b_a", "b_x", "lam", "oga", "ogb")
    pr = [prm[n] for n in names] + [token]
    big = lambda dt: pltpu.VMEM((tm, D_HALF), dt)
    zblocks = pltpu.VMEM((N_CHIPS, tm, W_IN_COLS), BF16)
    grid_spec = pltpu.PrefetchScalarGridSpec(
        num_scalar_prefetch=1, grid=(nt + 1,),
        in_specs=[proj(D_MODEL), proj(W_IN_COLS), other(1), other(2), other(3)] + [const(a) for a in pr],
        out_specs=[pl.BlockSpec((N_CHIPS, tm, W_IN_COLS), lambda s, kc: (0, jnp.minimum(s, nt - 1), 0)),
                   head(D_MODEL), head(D_HALF)],
        scratch_shapes=[zblocks, zblocks, big(BF16), big(F32), big(BF16), big(F32), big(F32), big(F32),
                        pltpu.VMEM((SUBLANES, D_HALF), F32), pltpu.VMEM((SUBLANES, D_HALF), F32)])
    return pl.pallas_call(
        body, name="inproj_branches_fwd", grid_spec=grid_spec,
        out_shape=[jax.ShapeDtypeStruct((N_CHIPS, t, W_IN_COLS), BF16), jax.ShapeDtypeStruct((t, D_MODEL), BF16),
                   jax.ShapeDtypeStruct((t, D_HALF), F32)],
        compiler_params=_params("arbitrary"),
    )(kc, hn, z_own, wg_in, wg_in, wg_in, *pr)


def _outproj_fwd(x, y, p, tgt, post_g, w_out, w_pg, wg_pe, tm):
    t = x.shape[0]

    def body(x_ref, y_ref, p_ref, tgt_ref, pg_ref, wo_ref, wpg_ref, wpe_ref,
             o_ref, h1_ref, gt_ref, dout_ref, loss_ref):
        @pl.when(pl.program_id(0) == 0)
        def _():
            loss_ref[...] = jnp.zeros_like(loss_ref)

        o = _dot(y_ref[...], wo_ref[...])
        o_ref[...] = o
        r3 = lax.rsqrt(_lanemean(o * o) + EPS)
        h1 = x_ref[...] + (o * r3) * pg_ref[...]
        h1b = h1.astype(BF16)
        h1_ref[...] = h1b
        gt = jax.nn.sigmoid(_dot(h1b, wpg_ref[...]))
        gt_ref[...] = gt
        pb = p_ref[...].astype(BF16)
        for k in range(N_CHIPS):
            cols = slice(k * W_PE_COLS, (k + 1) * W_PE_COLS)
            pe = _dot(pb, wpe_ref[k])
            d = h1[:, cols] + pe * gt[:, cols] - tgt_ref[:, cols]
            dout_ref[:, cols] = d * (1.0 / D_MODEL)
            loss_ref[...] += jnp.sum(d * d) * (0.5 / D_MODEL)

    row = lambda n: pl.BlockSpec((tm, n), lambda i: (i, 0))
    const = lambda shp: pl.BlockSpec(shp, lambda i, n=len(shp): (0,) * n, pipeline_mode=pl.Buffered(1))
    return pl.pallas_call(
        body, name="outproj_fwd", grid=(t // tm,),
        in_specs=[row(D_MODEL), row(D_MODEL), row(D_PLE), row(D_MODEL), const((1, D_MODEL)),
                  const((D_MODEL, D_MODEL)), const((D_MODEL, D_MODEL)), const((N_CHIPS, D_PLE, W_PE_COLS))],
        out_specs=[row(D_MODEL), row(D_MODEL), row(D_MODEL), row(D_MODEL),
                   pl.BlockSpec((SUBLANES, LANES), lambda i: (0, 0))],
        out_shape=[jax.ShapeDtypeStruct((t, D_MODEL), F32), jax.ShapeDtypeStruct((t, D_MODEL), BF16),
                   jax.ShapeDtypeStruct((t, D_MODEL), F32), jax.ShapeDtypeStruct((t, D_MODEL), F32),
                   jax.ShapeDtypeStruct((SUBLANES, LANES), F32)],
        compiler_params=_params("arbitrary"),
    )(x, y, p, tgt, post_g, w_out, w_pg, wg_pe)


def _head_bwd(dout, gt, p, o, post_g, w_out, w_pg, wg_pe, tm):
    t = dout.shape[0]

    def body(dout_ref, gt_ref, p_ref, o_ref, pg_ref, wo_ref, wpg_ref, wpe_ref,
             gwpe_ref, dq_ref, dh1_ref, do_ref, dy_ref, gpost_ref):
        i = pl.program_id(0)

        @pl.when(i == 0)
        def _():
            gpost_ref[...] = jnp.zeros_like(gpost_ref)
            gwpe_ref[...] = jnp.zeros_like(gwpe_ref)

        dout = dout_ref[...]
        gt = gt_ref[...]
        pb = p_ref[...].astype(BF16)
        for k in range(N_CHIPS):
            cols = slice(k * W_PE_COLS, (k + 1) * W_PE_COLS)
            pe = _dot(pb, wpe_ref[k])
            g = gt[:, cols]
            dg = dout[:, cols] * g
            gwpe_ref[k] += _dot_tn(pb, dg.astype(BF16))
            dq_ref[:, cols] = (dg * pe * (1.0 - g)).astype(BF16)
        dh1 = dout + _dot_nt(dq_ref[...], wpg_ref[...])
        dh1_ref[...] = dh1
        o = o_ref[...]
        r3 = lax.rsqrt(_lanemean(o * o) + EPS)
        on = o * r3
        gpost_ref[...] += _rowsum8(dh1 * on)
        don = dh1 * pg_ref[...]
        do = r3 * (don - on * _lanemean(don * on))
        dob = do.astype(BF16)
        do_ref[...] = dob
        dy_ref[...] = _dot_nt(dob, wo_ref[...])

        @pl.when(i == pl.num_programs(0) - 1)
        def _():
            gpost_ref[...] = jnp.broadcast_to(jnp.sum(gpost_ref[...], axis=0, keepdims=True), gpost_ref.shape)

    row = lambda n: pl.BlockSpec((tm, n), lambda i: (i, 0))
    const = lambda shp: pl.BlockSpec(shp, lambda i, n=len(shp): (0,) * n, pipeline_mode=pl.Buffered(1))
    return pl.pallas_call(
        body, name="head_bwd", grid=(t // tm,),
        in_specs=[row(D_MODEL), row(D_MODEL), row(D_PLE), row(D_MODEL), const((1, D_MODEL)),
                  const((D_MODEL, D_MODEL)), const((D_MODEL, D_MODEL)), const((N_CHIPS, D_PLE, W_PE_COLS))],
        out_specs=[pl.BlockSpec((N_CHIPS, D_PLE, W_PE_COLS), lambda i: (0, 0, 0)),
                   row(D_MODEL), row(D_MODEL), row(D_MODEL), row(D_MODEL),
                   pl.BlockSpec((SUBLANES, D_MODEL), lambda i: (0, 0))],
        out_shape=[jax.ShapeDtypeStruct((N_CHIPS, D_PLE, W_PE_COLS), F32), jax.ShapeDtypeStruct((t, D_MODEL), BF16),
                   jax.ShapeDtypeStruct((t, D_MODEL), F32), jax.ShapeDtypeStruct((t, D_MODEL), BF16),
                   jax.ShapeDtypeStruct((t, D_MODEL), F32), jax.ShapeDtypeStruct((SUBLANES, D_MODEL), F32)],
        compiler_params=_params("arbitrary"),
    )(dout, gt, p, o, post_g, w_out, w_pg, wg_pe)


def _branches_bwd(z, h, dy, prm, tm, token):
    t = h.shape[0]
    nt = t // tm
    hb = tm // SUBLANES

    def body(z_ref, zh_ref, h_ref, hh_ref, dy_ref,
             lng_ref, lnb_ref, wt_ref, wtt_ref, bsx_ref, cw_ref, cb_ref, wa_ref, wx_ref, ba_ref, bx_ref, lam_ref,
             oga_ref, ogb_ref, token_ref,
             dz_ref, g_oga, g_ogb, g_lng, g_lnb, g_bsx, g_ws, g_cw, g_cb, g_wa, g_ba, g_wx, g_bx, g_lam,
             vn_s, mixed_s, dm_s, dvn_s, xcbf_s, r_s, i_s, a_s, b_s, dh_s, dpr_s, dpi_s, dxc_s,
             ca_s, cd_s, cx_s):
        step_i = pl.program_id(0)
        tile = nt - 1 - step_i
        accs = (g_oga, g_ogb, g_lng, g_lnb, g_bsx, g_ws, g_cw, g_cb, g_wa, g_ba, g_wx, g_bx, g_lam)

        @pl.when(step_i == 0)
        def _():
            for r in accs + (ca_s, cd_s, cx_s):
                r[...] = jnp.zeros_like(r)

        dy_a = dy_ref[:, 0:D_HALF]
        dy_b = dy_ref[:, D_HALF:]

        u = _z_group(z_ref, 0).astype(F32)
        ug, tu = _gelu(u)
        v = _z_group(z_ref, 1).astype(F32)
        vg, tv = _gelu(v)
        vhat, rstd = _layernorm_parts(vg)
        vn_s[...] = (vhat * lng_ref[...] + lnb_ref[...]).astype(BF16)
        _spatial_mix(wt_ref, vn_s, bsx_ref, mixed_s, tm)
        mixed = mixed_s[...]
        ga = _z_group(z_ref, 2).astype(F32)
        sga = jax.nn.sigmoid(ga)
        sa = ga * sga
        um = ug * mixed
        ya = um * sa
        ra = lax.rsqrt(_lanemean(ya * ya) + EPS)
        yahat = ya * ra
        g_oga[...] += _rowsum8(dy_a * yahat)
        dn = dy_a * oga_ref[...]
        dya = ra * (dn - yahat * _lanemean(dn * yahat))
        dz_ref[:, 2 * D_HALF:3 * D_HALF] = (dya * um * (sga * (1.0 + ga * (1.0 - sga)))).astype(BF16)
        dz_ref[:, 0:D_HALF] = (dya * mixed * sa * _gelu_grad(u, tu)).astype(BF16)
        dmixed = dya * ug * sa
        g_bsx[...] += jnp.sum(dmixed.reshape(tm // CHUNK, CHUNK, D_HALF), axis=0)
        dm_s[...] = dmixed.astype(BF16)
        for c in range(tm // CHUNK):
            rows = slice(c * CHUNK, (c + 1) * CHUNK)
            for hd in range(N_HEADS):
                cols = slice(hd * CHUNK, (hd + 1) * CHUNK)
                dmh = dm_s[rows, cols]
                dvn_s[rows, cols] = _dot(wtt_ref[hd], dmh)
                g_ws[hd] += _dot_nt(dmh, vn_s[rows, cols])
        dvn = dvn_s[...]
        g_lng[...] += _rowsum8(dvn * vhat)
        g_lnb[...] += _rowsum8(dvn)
        dvh = dvn * lng_ref[...]
        dvg = rstd * (dvh - _lanemean(dvh) - vhat * _lanemean(dvh * vhat))
        dz_ref[:, D_HALF:2 * D_HALF] = (dvg * _gelu_grad(v, tv)).astype(BF16)

        xb = _z_group(z_ref, 3).astype(F32)
        halo = jnp.where(tile == 0, 0.0, _z_group(zh_ref, 3).astype(F32)[SUBLANES:])
        taps = _conv_taps(xb, halo)
        xc = cb_ref[...] + taps[0] * cw_ref[0:1, :]
        for k in range(1, CONV_W):
            xc = xc + taps[k] * cw_ref[k:k + 1, :]
        xcbf_s[...] = xc.astype(BF16)
        _lru_gates(xcbf_s, wa_ref, wx_ref, ba_ref, bx_ref, r_s, i_s)
        rg = r_s[...]
        ig = i_s[...]
        lam = lam_ref[...]
        a, mult_true = _decay_parts(rg, lam)
        row = lax.broadcasted_iota(jnp.int32, a.shape, 0)
        first = jnp.logical_and(tile == 0, row == 0)
        mult = jnp.where(first, 1.0, mult_true)
        hcur = h_ref[...]
        hprev = _shift_down(hcur, jnp.where(tile == 0, 0.0, hh_ref[...]), 1)
        gb = _z_group(z_ref, 4).astype(F32)
        sgb = jax.nn.sigmoid(gb)
        sb = gb * sgb
        yb = hcur * sb
        rb = lax.rsqrt(_lanemean(yb * yb) + EPS)
        ybhat = yb * rb
        g_ogb[...] += _rowsum8(dy_b * ybhat)
        dn = dy_b * ogb_ref[...]
        dyb = rb * (dn - ybhat * _lanemean(dn * ybhat))
        dz_ref[:, 4 * D_HALF:5 * D_HALF] = (dyb * hcur * (sgb * (1.0 + gb * (1.0 - sgb)))).astype(BF16)

        an = _shift_up(a, ca_s[...], 1)
        bb = dyb * sb
        r8 = row & (SUBLANES - 1)
        for d in (1, 2, 4):
            a_sh = pltpu.roll(an, tm - d, 0)
            b_sh = pltpu.roll(bb, tm - d, 0)
            m = r8 + d < SUBLANES
            bb = jnp.where(m, an * b_sh + bb, bb)
            an = jnp.where(m, an * a_sh, an)
        a_s[...] = an
        b_s[...] = bb

        def step(g, carry):
            sl = pl.ds(pl.multiple_of((hb - 1 - g) * SUBLANES, SUBLANES), SUBLANES)
            dg = a_s[sl, :] * carry + b_s[sl, :]
            dh_s[sl, :] = dg
            return jnp.broadcast_to(dg[0:1, :], dg.shape)

        cd_s[...] = lax.fori_loop(0, hb, step, cd_s[...])
        ca_s[...] = jnp.broadcast_to(a[0:1, :], ca_s.shape)
        dh = dh_s[...]
        da = dh * hprev
        gx = ig * xc
        dla = da * a - jnp.where(first, 0.0, dh * gx * (a * a / mult_true))
        g_lam[...] += _rowsum8(dla * rg)
        dr = dla * (-LRU_C * _softplus_neg(lam))
        dpr = dr * rg * (1.0 - rg)
        dpi = (dh * mult * xc) * ig * (1.0 - ig)
        g_ba[...] += _rowsum8(dpr)
        g_bx[...] += _rowsum8(dpi)
        dpr_s[...] = dpr.astype(BF16)
        dpi_s[...] = dpi.astype(BF16)
        for hd in range(N_HEADS):
            cols = slice(hd * CHUNK, (hd + 1) * CHUNK)
            xh = xcbf_s[:, cols]
            dprh = dpr_s[:, cols]
            dpih = dpi_s[:, cols]
            g_wa[hd] += _dot_tn(xh, dprh)
            g_wx[hd] += _dot_tn(xh, dpih)
            dxc_s[:, cols] = _dot_nt(dprh, wa_ref[hd]) + _dot_nt(dpih, wx_ref[hd])
        dxc = dxc_s[...] + dh * mult * ig
        g_cb[...] += _rowsum8(dxc)
        for k in range(CONV_W):
            g_cw[k * SUBLANES:(k + 1) * SUBLANES, :] += _rowsum8(dxc * taps[k])
        nxt = cx_s[...]
        dxb = dxc * cw_ref[CONV_W - 1:CONV_W, :]
        for j in range(1, CONV_W):
            dxb = dxb + _shift_up(dxc, nxt, j) * cw_ref[CONV_W - 1 - j:CONV_W - j, :]
        dz_ref[:, 3 * D_HALF:4 * D_HALF] = dxb.astype(BF16)
        cx_s[...] = dxc[0:SUBLANES]

        @pl.when(step_i == nt - 1)
        def _():
            for r in (g_oga, g_ogb, g_lng, g_lnb, g_cb, g_ba, g_bx):
                r[...] = jnp.broadcast_to(jnp.sum(r[...], axis=0, keepdims=True), r.shape)
            lam_f = LRU_C * jax.nn.sigmoid(-lam_ref[...])
            g_lam[...] = jnp.broadcast_to(jnp.sum(g_lam[...], axis=0, keepdims=True) * lam_f, g_lam.shape)
            for k in range(CONV_W):
                blk = g_cw[k * SUBLANES:(k + 1) * SUBLANES, :]
                g_cw[k * SUBLANES:(k + 1) * SUBLANES, :] = jnp.broadcast_to(jnp.sum(blk, axis=0, keepdims=True), blk.shape)
            tri = (lax.broadcasted_iota(jnp.int32, (CHUNK, CHUNK), 0) >= lax.broadcasted_iota(jnp.int32, (CHUNK, CHUNK), 1))
            for hd in range(N_HEADS):
                cols = slice(hd * CHUNK, (hd + 1) * CHUNK)
                g_ws[hd] = jnp.where(tri, g_ws[hd], 0.0)
                blk = g_bsx[:, cols]
                g_bsx[:, cols] = jnp.broadcast_to(jnp.sum(blk, axis=1, keepdims=True), blk.shape)

    rev = lambda i: nt - 1 - i
    zspec = pl.BlockSpec((N_CHIPS, tm, W_IN_COLS), lambda i: (0, rev(i), 0))
    halo = lambda col: pl.BlockSpec((SUBLANES, D_HALF), lambda i: (jnp.maximum(rev(i) * hb - 1, 0), col))
    zhalo = pl.BlockSpec((N_CHIPS, 2 * SUBLANES, W_IN_COLS), lambda i: (0, jnp.maximum(rev(i) * (hb // 2) - 1, 0), 0))
    full = lambda a: pl.BlockSpec(a.shape, lambda i, n=a.ndim: (0,) * n)
    acc = lambda shp: pl.BlockSpec(shp, lambda i, n=len(shp): (0,) * n)
    names = ("ln_g", "ln_b", "wt", "wtt", "bsx", "conv_w", "conv_b", "w_a", "w_x", "b_a", "b_x", "lam", "oga", "ogb")
    pr = [prm[n] for n in names] + [token]
    vec = (SUBLANES, D_HALF)
    mat = (N_HEADS, CHUNK, CHUNK)
    acc_shapes = [vec, vec, vec, vec, (CHUNK, D_HALF), mat, (CONV_W * SUBLANES, D_HALF), vec, mat, vec, mat, vec, vec]
    big = lambda dt: pltpu.VMEM((tm, D_HALF), dt)
    return pl.pallas_call(
        body, name="branches_bwd", grid=(nt,),
        in_specs=[zspec, zhalo,
                  pl.BlockSpec((tm, D_HALF), lambda i: (rev(i), 0)), halo(0),
                  pl.BlockSpec((tm, D_MODEL), lambda i: (rev(i), 0))] + [full(a) for a in pr],
        out_specs=[pl.BlockSpec((tm, D_Z), lambda i: (rev(i), 0))] + [acc(s) for s in acc_shapes],
        out_shape=[jax.ShapeDtypeStruct((t, D_Z), BF16)] + [jax.ShapeDtypeStruct(s, F32) for s in acc_shapes],
        scratch_shapes=[big(BF16), big(F32), big(BF16), big(F32), big(BF16), big(F32), big(F32), big(F32), big(F32),
                        big(F32), big(BF16), big(BF16), big(F32),
                        pltpu.VMEM(vec, F32), pltpu.VMEM(vec, F32), pltpu.VMEM(vec, F32)],
        compiler_params=_params("arbitrary"),
    )(z, z, h, h, dy, *pr)


def _inproj_bwd(dz, wg_in, x, dh1, pre_g, tm, tile0, nt, prev, last, token, name):
    t = x.shape[0]

    def body(*refs):
        dz_ref, w_ref, x_ref, dh1_ref, g_ref = refs[:5]
        gx_ref, gpre_ref, acc_s = refs[-3:]
        i = pl.program_id(0)

        @pl.when(i == 0)
        def _():
            gpre_ref[...] = jnp.zeros_like(gpre_ref) if prev is None else refs[7][...]

        acc = _dot_nt(dz_ref[:, 0:W_IN_COLS], w_ref[0])
        for k in range(1, N_CHIPS):
            acc = acc + _dot_nt(dz_ref[:, k * W_IN_COLS:(k + 1) * W_IN_COLS], w_ref[k])
        acc_s[...] = acc
        for s in range(tm // CHUNK):
            rows = slice(s * CHUNK, (s + 1) * CHUNK)
            xv = x_ref[rows, :]
            r = lax.rsqrt(_lanemean(xv * xv) + EPS)
            xhat = xv * r
            dhn = acc_s[rows, :]
            gpre_ref[...] += _rowsum8(dhn * xhat)
            dxh = dhn * g_ref[...]
            gx_ref[rows, :] = dh1_ref[rows, :] + r * (dxh - xhat * _lanemean(dxh * xhat))

        if last:
            @pl.when(i == nt - 1)
            def _():
                gpre_ref[...] = jnp.broadcast_to(jnp.sum(gpre_ref[...], axis=0, keepdims=True), gpre_ref.shape)

    row = lambda n: pl.BlockSpec((tm, n), lambda i: (tile0 + i, 0))
    small = lambda r: pl.BlockSpec((r, D_MODEL), lambda i: (0, 0))
    tok = pl.BlockSpec((SUBLANES, LANES), lambda i: (0, 0))
    in_specs = [row(D_Z), pl.BlockSpec(wg_in.shape, lambda i: (0, 0, 0), pipeline_mode=pl.Buffered(1)),
                row(D_MODEL), row(D_MODEL), small(1), tok]
    args = [dz, wg_in, x, dh1, pre_g, token]
    aliases = {}
    if prev is not None:
        in_specs += [ANY, small(SUBLANES)]
        args += list(prev)
        aliases = {6: 0}
    return pl.pallas_call(
        body, name=name, grid=(nt,), in_specs=in_specs, out_specs=[row(D_MODEL), small(SUBLANES)],
        out_shape=[jax.ShapeDtypeStruct((t, D_MODEL), F32), jax.ShapeDtypeStruct((SUBLANES, D_MODEL), F32)],
        input_output_aliases=aliases,
        scratch_shapes=[pltpu.VMEM((tm, D_MODEL), F32)],
        compiler_params=_params("arbitrary"),
    )(*args)


def _weight_grad(a, b, name, kb, nb, tk, tn, tt, token):
    t = a.shape[0]
    tt = min(tt, t)

    def body(a_ref, b_ref, token_ref, o_ref):
        @pl.when(pl.program_id(2) == 0)
        def _():
            o_ref[...] = jnp.zeros_like(o_ref)

        o_ref[...] += _dot_tn(a_ref[...], b_ref[...])

    return pl.pallas_call(
        body, name=name, grid=(nb, kb, t // tt),
        in_specs=[pl.BlockSpec((tt, tk), lambda j, i, s: (s, i)), pl.BlockSpec((tt, tn), lambda j, i, s: (s, j)),
                  pl.BlockSpec((SUBLANES, LANES), lambda j, i, s: (0, 0))],
        out_specs=pl.BlockSpec((None, None, tk, tn), lambda j, i, s: (j, i, 0, 0)),
        out_shape=jax.ShapeDtypeStruct((nb, kb, tk, tn), F32),
        compiler_params=_params("parallel", "parallel", "arbitrary"),
    )(a, b, token)


def _place():
    x, y, c = lax.axis_index("x"), lax.axis_index("y"), lax.axis_index("c")
    return x, y, c


def _chip_of(x, y):
    return 2 * x + y


HBM = pl.BlockSpec(memory_space=pltpu.HBM)
SEM = pl.BlockSpec(memory_space=pltpu.SEMAPHORE)
EFFECT = pltpu.SideEffectType.DATAFLOW_SIDE_EFFECTING


def _hbm(a):
    return pltpu.with_memory_space_constraint(a, pltpu.HBM)


def _landing(shape, dtype):
    return _hbm(lax.empty(shape, dtype))


def _exchange_start(name, arrays, ncopies, build, after=None):
    n = len(arrays)
    extra = [] if after is None else [after]

    def body(*refs):
        ins, token = refs[:n], refs[-1]
        send_sems, recv_sems = refs[n + len(extra)], refs[n + len(extra) + 1]
        for cp in build(ins, send_sems, recv_sems):
            cp.start()
        token[...] = jnp.zeros_like(token)

    outs = pl.pallas_call(
        body, name=name,
        out_shape=(pltpu.SemaphoreType.DMA((ncopies,)), pltpu.SemaphoreType.DMA((ncopies,)),
                   *[pltpu.HBM(a.shape, a.dtype) for a in arrays], jax.ShapeDtypeStruct((SUBLANES, LANES), F32)),
        in_specs=[HBM] * n + [ANY] * len(extra),
        out_specs=(SEM, SEM, *[HBM] * n, pl.BlockSpec(memory_space=pltpu.VMEM)),
        input_output_aliases={q: q + 2 for q in range(n)},
        compiler_params=pltpu.CompilerParams(has_side_effects=EFFECT),
    )(*[_hbm(a) for a in arrays], *extra)
    return (outs[0], outs[1], list(outs[2:2 + n])), outs[-1]


def _exchange_wait(name, started, after, build):
    send, recv, arrays = started
    n = len(arrays)

    def body(*refs):
        ins, send_sems, recv_sems = refs[:n], refs[n], refs[n + 1]
        for cp in build(ins, send_sems, recv_sems):
            cp.wait_send()
            cp.wait_recv()

    return pl.pallas_call(
        body, name=name, out_shape=tuple(pltpu.HBM(a.shape, a.dtype) for a in arrays),
        in_specs=[HBM] * n + [SEM, SEM, ANY], out_specs=tuple([HBM] * n),
        input_output_aliases={q: q for q in range(n)},
        compiler_params=pltpu.CompilerParams(has_side_effects=EFFECT),
    )(*arrays, send, recv, after)


def _exchange_wait_start(name, started, after, build_wait, ncopies, build_start):
    send, recv, arrays = started
    n = len(arrays)

    def body(*refs):
        ins, send_sems, recv_sems = refs[:n], refs[n], refs[n + 1]
        send2, recv2, token = refs[n + 3], refs[n + 4], refs[-1]
        for cp in build_wait(ins, send_sems, recv_sems):
            cp.wait_send()
            cp.wait_recv()
        for cp in build_start(ins, send2, recv2):
            cp.start()
        token[...] = jnp.zeros_like(token)

    outs = pl.pallas_call(
        body, name=name,
        out_shape=(pltpu.SemaphoreType.DMA((ncopies,)), pltpu.SemaphoreType.DMA((ncopies,)),
                   *[pltpu.HBM(a.shape, a.dtype) for a in arrays], jax.ShapeDtypeStruct((SUBLANES, LANES), F32)),
        in_specs=[HBM] * n + [SEM, SEM, ANY], out_specs=(SEM, SEM, *[HBM] * n, pl.BlockSpec(memory_space=pltpu.VMEM)),
        input_output_aliases={q: q + 2 for q in range(n)},
        compiler_params=pltpu.CompilerParams(has_side_effects=EFFECT),
    )(*arrays, send, recv, after)
    return (outs[0], outs[1], list(outs[2:2 + n])), outs[-1]


def _cast_into_slot(w, kc, name, dtype=BF16, token=None):
    rows, cols = w.shape
    tr = min(rows, 256)
    extra = [] if token is None else [token]

    def body(kc_ref, w_ref, *rest):
        rest[-1][...] = w_ref[...].astype(dtype)

    grid_spec = pltpu.PrefetchScalarGridSpec(
        num_scalar_prefetch=1, grid=(rows // tr,),
        in_specs=[pl.BlockSpec((tr, cols), lambda r, kc: (r, 0))]
                 + [pl.BlockSpec((SUBLANES, LANES), lambda r, kc: (0, 0))] * len(extra),
        out_specs=pl.BlockSpec((None, tr, cols), lambda r, kc: (kc[0], r, 0)))
    return pl.pallas_call(
        body, name=name, grid_spec=grid_spec, out_shape=jax.ShapeDtypeStruct((N_CHIPS, rows, cols), dtype),
        compiler_params=_params("arbitrary"),
    )(kc, w, *extra)


def _gather_ici_copies(n):
    def build(refs, send_sems, recv_sems):
        x, y, c = _place()
        mine = lambda b: refs[b].at[_chip_of(x, y), c]
        chips = [(1 - x, y), (x, 1 - y), (1 - x, 1 - y)]
        return [pltpu.make_async_remote_copy(
            src_ref=mine(b), dst_ref=mine(b), send_sem=send_sems.at[3 * b + j], recv_sem=recv_sems.at[3 * b + j],
            device_id=(*chip, c), device_id_type=MESH) for b in range(n) for j, chip in enumerate(chips)]
    return build


def _gather_relay_copies(n):
    def build(refs, send_sems, recv_sems):
        x, y, c = _place()
        chips = [(1 - x, y), (x, 1 - y), (1 - x, 1 - y)]
        cps = []
        for b in range(n):
            for j, chip in enumerate(chips):
                got = refs[b].at[_chip_of(*chip), c]
                cps.append(pltpu.make_async_remote_copy(
                    src_ref=got, dst_ref=got, send_sem=send_sems.at[3 * b + j], recv_sem=recv_sems.at[3 * b + j],
                    device_id=(x, y, 1 - c), device_id_type=MESH))
        return cps
    return build


def _sibling_copies(n):
    def build(refs, send_sems, recv_sems):
        x, y, c = _place()
        return [pltpu.make_async_remote_copy(
            src_ref=refs[b].at[:, 1 - c], dst_ref=refs[n + b], send_sem=send_sems.at[b], recv_sem=recv_sems.at[b],
            device_id=(x, y, 1 - c), device_id_type=MESH) for b in range(n)]
    return build


def _chip_copies(n):
    def build(refs, send_sems, recv_sems):
        x, y, c = _place()
        chips = [(1 - x, y), (x, 1 - y), (1 - x, 1 - y)]
        return [pltpu.make_async_remote_copy(
            src_ref=refs[b].at[_chip_of(*chip)], dst_ref=refs[n + b].at[j],
            send_sem=send_sems.at[3 * b + j], recv_sem=recv_sems.at[3 * b + j],
            device_id=(*chip, c), device_id_type=MESH) for b in range(n) for j, chip in enumerate(chips)]
    return build


def _finish_copies(n, n_all):
    def build(refs, send_sems, recv_sems):
        x, y, c = _place()
        cps = [pltpu.make_async_remote_copy(
            src_ref=refs[b].at[c], dst_ref=refs[b].at[c], send_sem=send_sems.at[b], recv_sem=recv_sems.at[b],
            device_id=(x, y, 1 - c), device_id_type=MESH) for b in range(n)]
        flips = [(fx, fy, fc) for fx in (0, 1) for fy in (0, 1) for fc in (0, 1)][1:]
        for b in range(n_all):
            mine = refs[n + b].at[_chip_of(x, y), c]
            cps += [pltpu.make_async_remote_copy(
                src_ref=mine, dst_ref=mine, send_sem=send_sems.at[n + 7 * b + q], recv_sem=recv_sems.at[n + 7 * b + q],
                device_id=(x ^ fx, y ^ fy, c ^ fc), device_id_type=MESH) for q, (fx, fy, fc) in enumerate(flips)]
        return cps
    return build


def _pair_sum(g, r1, kc, name, tr, send_dtype):
    nk, _, rows, cols = g.shape

    def body(kc_ref, g_ref, r_ref, p_ref, own_ref):
        s = g_ref[...] + r_ref[...]
        p_ref[...] = s.astype(send_dtype)

        @pl.when(pl.program_id(1) == kc_ref[0])
        def _():
            own_ref[...] = s

    grid_spec = pltpu.PrefetchScalarGridSpec(
        num_scalar_prefetch=1, grid=(rows // tr, nk),
        in_specs=[pl.BlockSpec((None, None, tr, cols), lambda r, k, kc: (k, kc[1], r, 0)),
                  pl.BlockSpec((None, tr, cols), lambda r, k, kc: (k, r, 0))],
        out_specs=[pl.BlockSpec((None, tr, cols), lambda r, k, kc: (k, r, 0)),
                   pl.BlockSpec((tr, cols), lambda r, k, kc: (r, 0))])
    return pl.pallas_call(
        body, name=name, grid_spec=grid_spec,
        out_shape=[jax.ShapeDtypeStruct((nk, rows, cols), send_dtype), jax.ShapeDtypeStruct((rows, cols), F32)],
        compiler_params=_params("arbitrary", "arbitrary"),
    )(kc, g, r1)


def _chip_sum(own, r2, slot, lead, name, tr):
    rows, cols = own.shape
    nl = len(lead)

    def body(slot_ref, o_ref, r_ref, s_ref):
        s = o_ref[...]
        for j in range(3):
            s = s + r_ref[j].astype(F32)
        s_ref[...] = s

    grid_spec = pltpu.PrefetchScalarGridSpec(
        num_scalar_prefetch=1, grid=(rows // tr,),
        in_specs=[pl.BlockSpec((tr, cols), lambda r, sl: (r, 0)), pl.BlockSpec((3, tr, cols), lambda r, sl: (0, r, 0))],
        out_specs=pl.BlockSpec((None,) * nl + (tr, cols), lambda r, sl: tuple(sl[q] for q in range(nl)) + (r, 0)))
    return pl.pallas_call(
        body, name=name, grid_spec=grid_spec, out_shape=jax.ShapeDtypeStruct(tuple(lead) + (rows, cols), F32),
        compiler_params=_params("arbitrary"),
    )(slot, own, r2)


def _adam_update(w, g, m, v):
    nm = ADAM_B1 * m + (1.0 - ADAM_B1) * g
    nv = ADAM_B2 * v + (1.0 - ADAM_B2) * (g * g)
    m_hat = nm / (1.0 - ADAM_B1 ** ADAM_STEP)
    v_hat = nv / (1.0 - ADAM_B2 ** ADAM_STEP)
    return -ADAM_LR * (m_hat / (jnp.sqrt(v_hat) + ADAM_EPS) + ADAM_WD * w), nm, nv


def _adamw(w, g, m, v, name, tr, token):
    rows, cols = w.shape

    def body(w_ref, g_ref, m_ref, v_ref, token_ref, go_ref, d_ref, nm_ref, nv_ref):
        gv = g_ref[...]
        go_ref[...] = gv
        d_ref[...], nm_ref[...], nv_ref[...] = _adam_update(w_ref[...], gv, m_ref[...], v_ref[...])

    spec = pl.BlockSpec((tr, cols), lambda r: (r, 0))
    return pl.pallas_call(
        body, name=name, grid=(rows // tr,),
        in_specs=[spec] * 4 + [pl.BlockSpec((SUBLANES, LANES), lambda r: (0, 0))], out_specs=[spec] * 4,
        out_shape=[jax.ShapeDtypeStruct((rows, cols), F32)] * 4,
        compiler_params=_params("parallel"),
    )(w, g, m, v, token)


def _adamw_small(packed_g, pre_g_parts, ws, ms, vs):
    names = ["pre_g"] + [n for n, _ in SMALL_ROWS if n != "conv_w"]
    rows = dict(SMALL_ROWS)
    offset, at = {}, 0
    for n, r in SMALL_ROWS:
        offset[n] = at
        at += r
    k = len(names)

    def body(*refs):
        g_ref, pg_ref = refs[0], refs[1]
        w_refs, m_refs, v_refs = refs[2:2 + k], refs[2 + k:2 + 2 * k], refs[2 + 2 * k:2 + 3 * k]
        outs = refs[2 + 3 * k:]
        go, do, mo, vo = outs[:k], outs[k:2 * k], outs[2 * k:3 * k], outs[3 * k:4 * k]
        pre = pg_ref[0]
        for dev in range(1, 8):
            pre = pre + pg_ref[dev]
        outs[4 * k][...] = pre[D_MODEL // LANES:, :]
        for i, n in enumerate(names):
            shp = w_refs[i].shape
            if len(shp) == 2 and shp[0] == 1:
                for r in range(shp[1] // LANES):
                    cols = slice(r * LANES, (r + 1) * LANES)
                    g = pre[r:r + 1, :] if n == "pre_g" else g_ref[offset[n] + r:offset[n] + r + 1, :]
                    go[i][:, cols] = g
                    do[i][:, cols], mo[i][:, cols], vo[i][:, cols] = _adam_update(
                        w_refs[i][:, cols], g, m_refs[i][:, cols], v_refs[i][:, cols])
            else:
                g = g_ref[offset[n]:offset[n] + rows[n], :].reshape(shp)
                go[i][...] = g
                do[i][...], mo[i][...], vo[i][...] = _adam_update(w_refs[i][...], g, m_refs[i][...], v_refs[i][...])

    vm = pl.BlockSpec(memory_space=pltpu.VMEM)
    args = [packed_g, pre_g_parts] + [src[n] for src in (ws, ms, vs) for n in names]
    out_shape = [jax.ShapeDtypeStruct(ws[n].shape, F32) for _ in range(4) for n in names]
    out_shape.append(jax.ShapeDtypeStruct((SUBLANES, LANES), F32))
    outs = pl.pallas_call(
        body, name="adamw_small", in_specs=[vm] * len(args), out_specs=[vm] * (4 * k + 1), out_shape=out_shape,
    )(*args)
    return [dict(zip(names, outs[q * k:(q + 1) * k])) for q in range(4)], outs[4 * k]


def _into_slot(v, tail, slot, lead, name):
    n = v.shape[1]
    nl = len(lead)
    rows = n // LANES + SUBLANES

    def body(slot_ref, v_ref, t_ref, o_ref):
        for r in range(n // LANES):
            o_ref[r:r + 1, :] = v_ref[0:1, r * LANES:(r + 1) * LANES]
        o_ref[n // LANES:, :] = t_ref[...]

    grid_spec = pltpu.PrefetchScalarGridSpec(
        num_scalar_prefetch=1, grid=(1,),
        in_specs=[pl.BlockSpec(v.shape, lambda i, sl: (0, 0)), pl.BlockSpec(tail.shape, lambda i, sl: (0, 0))],
        out_specs=pl.BlockSpec((None,) * nl + (rows, LANES), lambda i, sl: tuple(sl[q] for q in range(nl)) + (0, 0)))
    return pl.pallas_call(
        body, name=name, grid_spec=grid_spec, out_shape=jax.ShapeDtypeStruct(tuple(lead) + (rows, LANES), F32),
    )(slot, v, tail)


def _rows128(a):
    return a.reshape(-1, LANES)


def _pack_small(parts):
    pieces = [_rows128(parts[n]) for n, _ in SMALL_ROWS]
    pieces.append(jnp.zeros((SMALL_TOTAL - SMALL_USED, LANES), F32))
    return jnp.concatenate(pieces, axis=0)


def kernel(x, p, pre_g, w_in, gmlp_ln_g, gmlp_ln_b, gmlp_ws, gmlp_bs, conv_w, conv_b, w_a, b_a, w_x, b_x, lam, gmlp_out_g, lru_out_g, w_out, post_g, w_pe, w_pg, loss_target, m_pre_g, m_w_in, m_gmlp_ln_g, m_gmlp_ln_b, m_gmlp_ws, m_gmlp_bs, m_conv_w, m_conv_b, m_w_a, m_b_a, m_w_x, m_b_x, m_lam, m_gmlp_out_g, m_lru_out_g, m_w_out, m_post_g, m_w_pe, m_w_pg, v_pre_g, v_w_in, v_gmlp_ln_g, v_gmlp_ln_b, v_gmlp_ws, v_gmlp_bs, v_conv_w, v_conv_b, v_w_a, v_b_a, v_w_x, v_b_x, v_lam, v_gmlp_out_g, v_lru_out_g, v_w_out, v_post_g, v_w_pe, v_w_pg):
    weights = dict(pre_g=pre_g, w_in=w_in, gmlp_ln_g=gmlp_ln_g, gmlp_ln_b=gmlp_ln_b, gmlp_ws=gmlp_ws, gmlp_bs=gmlp_bs,
                   conv_w=conv_w, conv_b=conv_b, w_a=w_a, b_a=b_a, w_x=w_x, b_x=b_x, lam=lam, gmlp_out_g=gmlp_out_g,
                   lru_out_g=lru_out_g, w_out=w_out, post_g=post_g, w_pe=w_pe, w_pg=w_pg)
    mom_m = dict(pre_g=m_pre_g, w_in=m_w_in, gmlp_ln_g=m_gmlp_ln_g, gmlp_ln_b=m_gmlp_ln_b, gmlp_ws=m_gmlp_ws,
                 gmlp_bs=m_gmlp_bs, conv_w=m_conv_w, conv_b=m_conv_b, w_a=m_w_a, b_a=m_b_a, w_x=m_w_x, b_x=m_b_x,
                 lam=m_lam, gmlp_out_g=m_gmlp_out_g, lru_out_g=m_lru_out_g, w_out=m_w_out, post_g=m_post_g,
                 w_pe=m_w_pe, w_pg=m_w_pg)
    mom_v = dict(pre_g=v_pre_g, w_in=v_w_in, gmlp_ln_g=v_gmlp_ln_g, gmlp_ln_b=v_gmlp_ln_b, gmlp_ws=v_gmlp_ws,
                 gmlp_bs=v_gmlp_bs, conv_w=v_conv_w, conv_b=v_conv_b, w_a=v_w_a, b_a=v_b_a, w_x=v_w_x, b_x=v_b_x,
                 lam=v_lam, gmlp_out_g=v_gmlp_out_g, lru_out_g=v_lru_out_g, w_out=v_w_out, post_g=v_post_g,
                 w_pe=v_w_pe, w_pg=v_w_pg)
    order = list(weights)
    xi, yi, ci = _place()
    me = _chip_of(xi, yi)
    kc = jnp.stack([me, ci]).astype(jnp.int32)

    x2 = x[0]
    p2 = p[0, 0]
    tgt = loss_target[0]

    first = [_cast_into_slot(w_in[0], kc, "cast_w_in").reshape(N_CHIPS, 2, D_MODEL // 2, W_IN_COLS),
             _cast_into_slot(conv_w[0, :, 0, :], kc, "conv_w_into_slot", F32).reshape(N_CHIPS, 2, CONV_W // 2, CONV_COLS)]
    in_st, in_tok = _exchange_start("gather_in_start", first, 6, _gather_ici_copies(2))
    later = [_cast_into_slot(w_out[0], kc, "cast_w_out", token=in_tok).reshape(N_CHIPS, 2, W_ROWS // 2, D_MODEL),
             _cast_into_slot(w_pg[0], kc, "cast_w_pg", token=in_tok).reshape(N_CHIPS, 2, W_ROWS // 2, D_MODEL),
             _cast_into_slot(w_pe[0], kc, "cast_w_pe", token=in_tok).reshape(N_CHIPS, 2, D_PLE // 2, W_PE_COLS)]
    gather_st, gather_tok = _exchange_start("gather_start", later, 9, _gather_ici_copies(3), after=in_tok)
    hn, z_own = _inproj_local(x2, pre_g, w_in[0], ROW_TILE, gather_tok)
    in_st, in_tok = _exchange_wait_start("gather_in_relay", in_st, z_own, _gather_ici_copies(2), 6,
                                         _gather_relay_copies(2))
    g_in, g_cw = _exchange_wait("gather_in_wait", in_st, in_tok, _gather_relay_copies(2))
    wg_in = g_in.reshape(N_CHIPS, D_MODEL, W_IN_COLS)
    cw_full = jnp.transpose(g_cw.reshape(N_CHIPS, CONV_W, CONV_COLS), (1, 0, 2)).reshape(CONV_W, D_HALF)

    causal = jnp.tril(jnp.ones((CHUNK, CHUNK), dtype=bool))
    ws_m = jnp.where(causal[None], gmlp_ws[0], 0.0)
    prm = dict(
        ln_g=gmlp_ln_g, ln_b=gmlp_ln_b, wt=ws_m.astype(BF16), wtt=jnp.transpose(ws_m, (0, 2, 1)).astype(BF16),
        bsx=jnp.repeat(jnp.transpose(gmlp_bs[0]), CHUNK, axis=1),
        conv_w=cw_full, conv_b=conv_b, w_a=w_a[0].astype(BF16), w_x=w_x[0].astype(BF16),
        b_a=b_a[0].reshape(1, D_HALF), b_x=b_x[0].reshape(1, D_HALF), lam=lam, oga=gmlp_out_g, ogb=lru_out_g)

    z, y, h = _inproj_branches_fwd(hn, z_own, wg_in, kc, prm, ROW_TILE, gather_tok)
    gather_st, gather_tok = _exchange_wait_start("gather_relay", gather_st, y, _gather_ici_copies(3), 9,
                                                 _gather_relay_copies(3))
    g_out, g_pg, g_pe = _exchange_wait("gather_wait", gather_st, gather_tok, _gather_relay_copies(3))
    wg_out = g_out.reshape(D_MODEL, D_MODEL)
    wg_pg = g_pg.reshape(D_MODEL, D_MODEL)
    wg_pe = g_pe.reshape(N_CHIPS, D_PLE, W_PE_COLS)
    o, h1, gt, dout, loss_acc = _outproj_fwd(x2, y, p2, tgt, post_g, wg_out, wg_pg, wg_pe, ROW_TILE)

    def sibling_start(tag, bufs):
        lands = [_landing((b.shape[0],) + b.shape[2:], b.dtype) for b in bufs]
        return _exchange_start("sibling_start_" + tag, bufs + lands, len(bufs), _sibling_copies(len(bufs)))

    def pair_then_chip_start(tag, started, after, names, tiles, dtypes):
        n = len(names)
        got = _exchange_wait("sibling_wait_" + tag, started, after, _sibling_copies(n))
        pairs = [_pair_sum(got[b], got[n + b], kc, "pair_sum_" + names[b], tiles[b], dtypes[b]) for b in range(n)]
        lands = [_landing((3,) + pr[0].shape[1:], pr[0].dtype) for pr in pairs]
        return _exchange_start("chip_start_" + tag, [pr[0] for pr in pairs] + lands, 3 * n, _chip_copies(n)), pairs

    def sum_then_finish_start(tag, started, pairs, after, names, tiles, small, to_all=()):
        n = len(names)
        got = _exchange_wait("chip_wait_" + tag, started, after, _chip_copies(n))
        sums = [_chip_sum(pairs[b][1], got[n + b], kc if small and b == n - 1 else kc[1:],
                          (N_CHIPS, 2) if small and b == n - 1 else (2,), "chip_sum_" + names[b], tiles[b])
                for b in range(n)]
        nbig = n - 1 if small else n
        n_all = n - nbig + len(to_all)
        return _exchange_start("finish_start_" + tag, sums + list(to_all), nbig + 7 * n_all,
                               _finish_copies(nbig, n_all))

    gw_pe, dq, dh1, do, dy, g_post = _head_bwd(dout, gt, p2, o, post_g, wg_out, wg_pg, wg_pe, ROW_TILE)
    gw_pe = gw_pe.reshape(N_CHIPS, 2, D_PLE // 2, W_PE_COLS)
    token0 = jnp.zeros((SUBLANES, LANES), F32)
    gw_out = _weight_grad(y, do, "grad_w_out", 2, 1, D_MODEL // 2, D_MODEL, CONTRACT_TILE, token0)
    gw_pg = _weight_grad(h1, dq, "grad_w_pg", 2, 1, D_MODEL // 2, D_MODEL, CONTRACT_TILE, token0)
    gw_out = gw_out.reshape(N_CHIPS, 2, W_ROWS // 2, D_MODEL)
    gw_pg = gw_pg.reshape(N_CHIPS, 2, W_ROWS // 2, D_MODEL)

    names_a, tiles_a = ["w_out", "w_pg", "w_pe"], [SUM_TILE] * 3
    st, tok = sibling_start("a", [gw_out, gw_pg, gw_pe])
    (dz, g_oga, g_ogb, g_lng, g_lnb, g_bsx, g_ws, g_cw, g_cb, g_wa, g_ba, g_wx, g_bx, g_lam) = _branches_bwd(
        z, h, dy, prm, ROW_TILE, tok)
    (st, tok), pairs_a = pair_then_chip_start("a", st, dz, names_a, tiles_a, [BF16] * 3)
    gw_in = _weight_grad(hn, dz, "grad_w_in", 2, N_CHIPS, D_MODEL // 2, W_IN_COLS, CONTRACT_TILE, tok)
    fin_a, tok = sum_then_finish_start("a", st, pairs_a, gw_in, names_a, tiles_a, False)

    small_g = dict(
        gmlp_ln_g=g_lng[0:1], gmlp_ln_b=g_lnb[0:1], gmlp_ws=g_ws,
        gmlp_bs=jnp.transpose(g_bsx[:, ::CHUNK]), conv_w=g_cw[::SUBLANES], conv_b=g_cb[0:1], w_a=g_wa, b_a=g_ba[0:1],
        w_x=g_wx, b_x=g_bx[0:1], lam=g_lam[0:1], gmlp_out_g=g_oga[0:1], lru_out_g=g_ogb[0:1], post_g=g_post[0:1])
    gsm = _pack_small(small_g).reshape(N_CHIPS, 2, SMALL_PIECE, LANES)

    names_b, tiles_b = ["w_in", "small"], [2 * SUM_TILE, SMALL_PIECE]
    n_tiles = x2.shape[0] // ROW_TILE
    n_lo = max(1, (5 * n_tiles) // 16)
    st, tok_b = _exchange_start(
        "sibling_start_b", [gw_in, gsm] + [_landing((N_CHIPS,) + b.shape[2:], F32) for b in (gw_in, gsm)], 2,
        _sibling_copies(2), after=tok)
    part = _inproj_bwd(dz, wg_in, x2, dh1, pre_g, ROW_TILE, 0, n_lo, None, False, tok_b, "inproj_bwd_lo")
    f_out, f_pg, f_pe = _exchange_wait("finish_wait_a", fin_a, part[1], _finish_copies(3, 0))
    (st, tok_b), pairs_b = pair_then_chip_start("b", st, part[1], names_b, tiles_b, [BF16, F32])
    grad_x, g_pre = _inproj_bwd(dz, wg_in, x2, dh1, pre_g, ROW_TILE, n_lo, n_tiles - n_lo, part, True, tok_b,
                                "inproj_bwd_hi")
    pre_parts = _into_slot(g_pre, loss_acc, kc, (N_CHIPS, 2), "pre_g_into_slot")
    fin_b, tok_b = sum_then_finish_start("b", st, pairs_b, g_pre, names_b, tiles_b, True, to_all=[pre_parts])

    grads, deltas, new_m, new_v = {}, {}, {}, {}

    def adam_big(n, g2d, tr, token):
        shp = weights[n].shape
        g, d, nm, nv = _adamw(weights[n][0], g2d, mom_m[n][0], mom_v[n][0], "adamw_" + n, tr, token)
        grads[n], deltas[n], new_m[n], new_v[n] = g.reshape(shp), d.reshape(shp), nm.reshape(shp), nv.reshape(shp)
        return d

    as_token = lambda d: d[:SUBLANES, :LANES]
    last = adam_big("w_out", f_out.reshape(W_ROWS, D_MODEL), SUM_TILE, tok_b)
    last = adam_big("w_pg", f_pg.reshape(W_ROWS, D_MODEL), SUM_TILE, as_token(last))
    last = adam_big("w_pe", f_pe.reshape(D_PLE, W_PE_COLS), SUM_TILE, as_token(last))
    f_in, f_sm, pre_parts = _exchange_wait("finish_wait_b", fin_b, last, _finish_copies(1, 2))
    adam_big("w_in", f_in.reshape(D_MODEL, W_IN_COLS), 2 * SUM_TILE, tok_b)

    packed_g = f_sm.reshape(SMALL_TOTAL, LANES)
    small_names = ["pre_g"] + [n for n, _ in SMALL_ROWS if n != "conv_w"]
    natural = lambda src: {n: (src[n] if src[n].ndim == 2 else src[n][0]) for n in small_names}
    outs, loss_block = _adamw_small(packed_g, pre_parts.reshape(8, D_MODEL // LANES + SUBLANES, LANES),
                                    natural(weights), natural(mom_m), natural(mom_v))
    loss = loss_block[0, 0]
    for dst, got in zip((grads, deltas, new_m, new_v), outs):
        for n in small_names:
            dst[n] = got[n].reshape(weights[n].shape)
    at = sum(r for n, r in SMALL_ROWS[:[n for n, _ in SMALL_ROWS].index("conv_w")])
    g_cw_all = packed_g[at:at + CONV_W * D_HALF // LANES].reshape(CONV_W, D_HALF)
    g_conv = lax.dynamic_slice_in_dim(g_cw_all, me * CONV_COLS, CONV_COLS, axis=1)
    g, d, nm, nv = _adamw(conv_w[0, :, 0, :], g_conv, m_conv_w[0, :, 0, :], v_conv_w[0, :, 0, :], "adamw_conv_w", CONV_W,
                          tok_b)
    cshape = conv_w.shape
    grads["conv_w"], deltas["conv_w"] = g.reshape(cshape), d.reshape(cshape)
    new_m["conv_w"], new_v["conv_w"] = nm.reshape(cshape), nv.reshape(cshape)

    return (loss, grad_x.reshape(x.shape), *[grads[n] for n in order], *[deltas[n] for n in order],
            *[new_m[n] for n in order], *[new_v[n] for n in order])
```

```python
import math

import jax
import jax.numpy as jnp
from jax import lax
from jax.experimental import pallas as pl
from jax.experimental.pallas import tpu as pltpu

F32 = jnp.float32
BF16 = jnp.bfloat16

D_MODEL = 2048
D_HALF = 1024
D_Z = 5120
D_PLE = 256
CHUNK = 128
N_HEADS = 8
N_CHIPS = 4
W_IN_COLS = D_Z // N_CHIPS
W_ROWS = D_MODEL // N_CHIPS
W_PE_COLS = D_MODEL // N_CHIPS
CONV_W = 4
CONV_COLS = D_HALF // N_CHIPS
EPS = 1e-6
LRU_C = 8.0
ADAM_LR, ADAM_B1, ADAM_B2, ADAM_EPS, ADAM_WD, ADAM_STEP = 0.001, 0.9, 0.999, 1e-08, 0.01, 10

SUBLANES = 8
LANES = 128
VMEM_LIMIT = 56 * 1024 * 1024
ROW_TILE = 256
CONTRACT_TILE = 1024
SUM_TILE = 128

SMALL_ROWS = (("gmlp_ln_g", 8), ("gmlp_ln_b", 8), ("gmlp_ws", 1024), ("gmlp_bs", 8),
              ("conv_w", 32), ("conv_b", 8), ("w_a", 1024), ("b_a", 8), ("w_x", 1024), ("b_x", 8),
              ("lam", 8), ("gmlp_out_g", 8), ("lru_out_g", 8), ("post_g", 16))
SMALL_USED = sum(r for _, r in SMALL_ROWS)
SMALL_PIECE = 400
SMALL_TOTAL = 8 * SMALL_PIECE

MESH = pl.DeviceIdType.MESH
ANY = pl.BlockSpec(memory_space=pl.ANY)

_GELU_C0 = math.sqrt(2.0 / math.pi)
_GELU_C1 = 0.044715


def _params(*sem):
    return pltpu.CompilerParams(dimension_semantics=sem, vmem_limit_bytes=VMEM_LIMIT)


def _dot(a, b):
    return jnp.dot(a, b, preferred_element_type=F32)


def _dot_nt(a, b):
    return lax.dot_general(a, b, (((1,), (1,)), ((), ())), preferred_element_type=F32)


def _dot_tn(a, b):
    return lax.dot_general(a, b, (((0,), (0,)), ((), ())), preferred_element_type=F32)


def _gelu(x):
    t = jnp.tanh(_GELU_C0 * (x + _GELU_C1 * (x * x * x)))
    return 0.5 * x * (1.0 + t), t


def _gelu_grad(x, t):
    return 0.5 * (1.0 + t) + 0.5 * x * (1.0 - t * t) * (_GELU_C0 * (1.0 + 3.0 * _GELU_C1 * x * x))


def _rowsum8(v):
    r, n = v.shape
    return jnp.sum(v.reshape(r // SUBLANES, SUBLANES, n), axis=0)


def _lanemean(v):
    return jnp.mean(v, axis=-1, keepdims=True)


def _shift_down(v, halo8, k):
    if k == 0:
        return v
    r = pltpu.roll(v, k, 0)
    hr = pltpu.roll(halo8, k, 0)
    row = lax.broadcasted_iota(jnp.int32, halo8.shape, 0)
    top = jnp.where(row < k, hr, r[0:SUBLANES])
    return jnp.concatenate([top, r[SUBLANES:]], axis=0)


def _shift_up(v, next8, k):
    if k == 0:
        return v
    n = v.shape[0]
    r = pltpu.roll(v, n - k, 0)
    nr = pltpu.roll(next8, SUBLANES - k, 0)
    row = lax.broadcasted_iota(jnp.int32, next8.shape, 0)
    bot = jnp.where(row >= SUBLANES - k, nr, r[n - SUBLANES:])
    return jnp.concatenate([r[:n - SUBLANES], bot], axis=0)


def _layernorm_parts(vg):
    mu = _lanemean(vg)
    xc = vg - mu
    rstd = lax.rsqrt(_lanemean(xc * xc) + EPS)
    return xc * rstd, rstd


def _spatial_mix(wt_ref, vn_ref, bsx_ref, mixed_ref, tm):
    for c in range(tm // CHUNK):
        rows = slice(c * CHUNK, (c + 1) * CHUNK)
        for h in range(N_HEADS):
            cols = slice(h * CHUNK, (h + 1) * CHUNK)
            mixed_ref[rows, cols] = _dot(wt_ref[h], vn_ref[rows, cols]) + bsx_ref[:, cols]


def _conv_taps(xb, halo8):
    return [_shift_down(xb, halo8, CONV_W - 1 - k) for k in range(CONV_W)]


def _lru_gates(xc_bf_ref, wa_ref, wx_ref, ba_ref, bx_ref, r_ref, i_ref):
    for h in range(N_HEADS):
        cols = slice(h * CHUNK, (h + 1) * CHUNK)
        xh = xc_bf_ref[:, cols]
        r_ref[:, cols] = jax.nn.sigmoid(_dot(xh, wa_ref[h]) + ba_ref[:, cols])
        i_ref[:, cols] = jax.nn.sigmoid(_dot(xh, wx_ref[h]) + bx_ref[:, cols])


def _softplus_neg(lam):
    return jnp.maximum(-lam, 0.0) + jnp.log(1.0 + jnp.exp(-jnp.abs(lam)))


def _decay_parts(r, lam):
    la = (-LRU_C * _softplus_neg(lam)) * r
    a = jnp.exp(la)
    th = -jnp.tanh(la)
    mult = jnp.sqrt(2.0 * th / (1.0 + th))
    return a, mult


def _z_group(zref, g, rows=slice(None)):
    lo = g * D_HALF
    blk, off = lo // W_IN_COLS, lo % W_IN_COLS
    if off + D_HALF <= W_IN_COLS:
        return zref[blk, rows, off:off + D_HALF]
    return jnp.concatenate([zref[blk, rows, off:W_IN_COLS], zref[blk + 1, rows, 0:off + D_HALF - W_IN_COLS]], axis=1)


def _inproj_local(x, pre_g, w_own, tm, token):
    t = x.shape[0]

    def body(x_ref, g_ref, w_ref, token_ref, hn_ref, zl_ref, wbf_s):
        @pl.when(pl.program_id(0) == 0)
        def _():
            wbf_s[...] = w_ref[...].astype(BF16)

        xv = x_ref[...]
        hn = (xv * lax.rsqrt(_lanemean(xv * xv) + EPS) * g_ref[...]).astype(BF16)
        hn_ref[...] = hn
        zl_ref[...] = _dot(hn, wbf_s[...]).astype(BF16)

    row = lambda n: pl.BlockSpec((tm, n), lambda i: (i, 0))
    const = lambda shp: pl.BlockSpec(shp, lambda i: (0, 0), pipeline_mode=pl.Buffered(1))
    return pl.pallas_call(
        body, name="inproj_local", grid=(t // tm,),
        in_specs=[row(D_MODEL), const((1, D_MODEL)), const((D_MODEL, W_IN_COLS)), const((SUBLANES, LANES))],
        out_specs=[row(D_MODEL), row(W_IN_COLS)],
        out_shape=[jax.ShapeDtypeStruct((t, D_MODEL), BF16), jax.ShapeDtypeStruct((t, W_IN_COLS), BF16)],
        scratch_shapes=[pltpu.VMEM((D_MODEL, W_IN_COLS), BF16)],
        compiler_params=_params("arbitrary"),
    )(x, pre_g, w_own, token)


def _inproj_branches_fwd(hn, z_own, wg_in, kc, prm, tm, token):
    t = hn.shape[0]
    nt = t // tm
    hb = tm // SUBLANES

    def body(kc_ref, hn_ref, zo_ref, w1_ref, w2_ref, w3_ref,
             lng_ref, lnb_ref, wt_ref, bsx_ref, cw_ref, cb_ref, wa_ref, wx_ref, ba_ref, bx_ref, lam_ref,
             oga_ref, ogb_ref, token_ref,
             z_ref, y_ref, h_ref,
             zbuf0, zbuf1, vn_s, mixed_s, xcbf_s, r_s, i_s, ug_s, halo_s, carry_s):
        s = pl.program_id(0)
        me = kc_ref[0]
        w_refs = (None, w1_ref, w2_ref, w3_ref)

        @pl.when(s == 0)
        def _():
            zbuf1[...] = jnp.zeros_like(zbuf1)

        @pl.when(s <= 1)
        def _():
            carry_s[...] = jnp.zeros_like(carry_s)
            halo_s[...] = jnp.zeros_like(halo_s)

        def step(zw, zr):
            def project(r):
                blk = (me + r) % N_CHIPS
                zb = zo_ref[...] if r == 0 else _dot(hn_ref[...], w_refs[r][...]).astype(BF16)
                z_ref[blk] = zb
                zw[blk] = zb

            zin = lambda g: _z_group(zr, g).astype(F32)
            always = [s >= 0] * 4

            @pl.when(always[0])
            def _():
                project(0)
                ug, _ = _gelu(zin(0))
                ug_s[...] = ug
                vg, _ = _gelu(zin(1))
                vhat, _ = _layernorm_parts(vg)
                vn_s[...] = (vhat * lng_ref[...] + lnb_ref[...]).astype(BF16)

            @pl.when(always[1])
            def _():
                project(1)
                _spatial_mix(wt_ref, vn_s, bsx_ref, mixed_s, tm)
                ga = zin(2)
                ya = ug_s[...] * mixed_s[...] * (ga * jax.nn.sigmoid(ga))
                ra = lax.rsqrt(_lanemean(ya * ya) + EPS)
                y_ref[:, 0:D_HALF] = (ya * ra * oga_ref[...]).astype(BF16)

            @pl.when(always[2])
            def _():
                project(2)
                xb = zin(3)
                taps = _conv_taps(xb, halo_s[...])
                halo_s[...] = xb[tm - SUBLANES:]
                xc = cb_ref[...] + taps[0] * cw_ref[0:1, :]
                for k in range(1, CONV_W):
                    xc = xc + taps[k] * cw_ref[k:k + 1, :]
                xcbf_s[...] = xc.astype(BF16)
                _lru_gates(xcbf_s, wa_ref, wx_ref, ba_ref, bx_ref, r_s, i_s)
                a, mult = _decay_parts(r_s[...], lam_ref[...])
                row = lax.broadcasted_iota(jnp.int32, a.shape, 0)
                mult = jnp.where(jnp.logical_and(s == 1, row == 0), 1.0, mult)
                r_s[...] = a
                i_s[...] = mult * (i_s[...] * xc)

            @pl.when(always[3])
            def _():
                project(3)
                a = r_s[...]
                b = i_s[...]
                r8 = lax.broadcasted_iota(jnp.int32, a.shape, 0) & (SUBLANES - 1)
                for d in (1, 2, 4):
                    a_sh = pltpu.roll(a, d, 0)
                    b_sh = pltpu.roll(b, d, 0)
                    m = r8 >= d
                    b = jnp.where(m, a * b_sh + b, b)
                    a = jnp.where(m, a * a_sh, a)
                carry = carry_s[...]
                for g in range(hb):
                    rows = slice(g * SUBLANES, (g + 1) * SUBLANES)
                    hg = a[rows] * carry + b[rows]
                    h_ref[rows, :] = hg
                    carry = jnp.broadcast_to(hg[SUBLANES - 1:SUBLANES, :], hg.shape)
                carry_s[...] = carry
                gb = zin(4)
                yb = h_ref[...] * (gb * jax.nn.sigmoid(gb))
                rb = lax.rsqrt(_lanemean(yb * yb) + EPS)
                y_ref[:, D_HALF:] = (yb * rb * ogb_ref[...]).astype(BF16)

        @pl.when(s % 2 == 0)
        def _():
            step(zbuf0, zbuf1)

        @pl.when(s % 2 == 1)
        def _():
            step(zbuf1, zbuf0)

    const = lambda a: pl.BlockSpec(a.shape, lambda s, kc, n=a.ndim: (0,) * n, pipeline_mode=pl.Buffered(1))
    proj = lambda n: pl.BlockSpec((tm, n), lambda s, kc: (jnp.minimum(s, nt - 1), 0))
    head = lambda n: pl.BlockSpec((tm, n), lambda s, kc: (jnp.maximum(s - 1, 0), 0))
    other = lambda r: pl.BlockSpec((None, D_MODEL, W_IN_COLS), lambda s, kc, r=r: ((kc[0] + r) % N_CHIPS, 0, 0),
                                   pipeline_mode=pl.Buffered(1))
    names = ("ln_g", "ln_b", "wt", "bsx", "conv_w", "conv_b", "w_a", "w_x", "b_a", "b_x", "lam", "oga", "ogb")
    pr = [prm[n] for n in names] + [token]
    big = lambda dt: pltpu.VMEM((tm, D_HALF), dt)
    zblocks = pltpu.VMEM((N_CHIPS, tm, W_IN_COLS), BF16)
    grid_spec = pltpu.PrefetchScalarGridSpec(
        num_scalar_prefetch=1, grid=(nt + 1,),
        in_specs=[proj(D_MODEL), proj(W_IN_COLS), other(1), other(2), other(3)] + [const(a) for a in pr],
        out_specs=[pl.BlockSpec((N_CHIPS, tm, W_IN_COLS), lambda s, kc: (0, jnp.minimum(s, nt - 1), 0)),
                   head(D_MODEL), head(D_HALF)],
        scratch_shapes=[zblocks, zblocks, big(BF16), big(F32), big(BF16), big(F32), big(F32), big(F32),
                        pltpu.VMEM((SUBLANES, D_HALF), F32), pltpu.VMEM((SUBLANES, D_HALF), F32)])
    return pl.pallas_call(
        body, name="inproj_branches_fwd", grid_spec=grid_spec,
        out_shape=[jax.ShapeDtypeStruct((N_CHIPS, t, W_IN_COLS), BF16), jax.ShapeDtypeStruct((t, D_MODEL), BF16),
                   jax.ShapeDtypeStruct((t, D_HALF), F32)],
        compiler_params=_params("arbitrary"),
    )(kc, hn, z_own, wg_in, wg_in, wg_in, *pr)


def _outproj_fwd(x, y, p, tgt, post_g, w_out, w_pg, wg_pe, tm):
    t = x.shape[0]

    def body(x_ref, y_ref, p_ref, tgt_ref, pg_ref, wo_ref, wpg_ref, wpe_ref,
             o_ref, h1_ref, gt_ref, dout_ref, loss_ref):
        @pl.when(pl.program_id(0) == 0)
        def _():
            loss_ref[...] = jnp.zeros_like(loss_ref)

        o = _dot(y_ref[...], wo_ref[...])
        o_ref[...] = o
        r3 = lax.rsqrt(_lanemean(o * o) + EPS)
        h1 = x_ref[...] + (o * r3) * pg_ref[...]
        h1b = h1.astype(BF16)
        h1_ref[...] = h1b
        gt = jax.nn.sigmoid(_dot(h1b, wpg_ref[...]))
        gt_ref[...] = gt
        pb = p_ref[...].astype(BF16)
        for k in range(N_CHIPS):
            cols = slice(k * W_PE_COLS, (k + 1) * W_PE_COLS)
            pe = _dot(pb, wpe_ref[k])
            d = h1[:, cols] + pe * gt[:, cols] - tgt_ref[:, cols]
            dout_ref[:, cols] = d * (1.0 / D_MODEL)
            loss_ref[...] += jnp.sum(d * d) * (0.5 / D_MODEL)

    row = lambda n: pl.BlockSpec((tm, n), lambda i: (i, 0))
    const = lambda shp: pl.BlockSpec(shp, lambda i, n=len(shp): (0,) * n, pipeline_mode=pl.Buffered(1))
    return pl.pallas_call(
        body, name="outproj_fwd", grid=(t // tm,),
        in_specs=[row(D_MODEL), row(D_MODEL), row(D_PLE), row(D_MODEL), const((1, D_MODEL)),
                  const((D_MODEL, D_MODEL)), const((D_MODEL, D_MODEL)), const((N_CHIPS, D_PLE, W_PE_COLS))],
        out_specs=[row(D_MODEL), row(D_MODEL), row(D_MODEL), row(D_MODEL),
                   pl.BlockSpec((SUBLANES, LANES), lambda i: (0, 0))],
        out_shape=[jax.ShapeDtypeStruct((t, D_MODEL), F32), jax.ShapeDtypeStruct((t, D_MODEL), BF16),
                   jax.ShapeDtypeStruct((t, D_MODEL), F32), jax.ShapeDtypeStruct((t, D_MODEL), F32),
                   jax.ShapeDtypeStruct((SUBLANES, LANES), F32)],
        compiler_params=_params("arbitrary"),
    )(x, y, p, tgt, post_g, w_out, w_pg, wg_pe)


def _head_bwd(dout, gt, p, o, post_g, w_out, w_pg, wg_pe, tm):
    t = dout.shape[0]

    def body(dout_ref, gt_ref, p_ref, o_ref, pg_ref, wo_ref, wpg_ref, wpe_ref,
             gwpe_ref, dq_ref, dh1_ref, do_ref, dy_ref, gpost_ref):
        i = pl.program_id(0)

        @pl.when(i == 0)
        def _():
            gpost_ref[...] = jnp.zeros_like(gpost_ref)
            gwpe_ref[...] = jnp.zeros_like(gwpe_ref)

        dout = dout_ref[...]
        gt = gt_ref[...]
        pb = p_ref[...].astype(BF16)
        for k in range(N_CHIPS):
            cols = slice(k * W_PE_COLS, (k + 1) * W_PE_COLS)
            pe = _dot(pb, wpe_ref[k])
            g = gt[:, cols]
            dg = dout[:, cols] * g
            gwpe_ref[k] += _dot_tn(pb, dg.astype(BF16))
            dq_ref[:, cols] = (dg * pe * (1.0 - g)).astype(BF16)
        dh1 = dout + _dot_nt(dq_ref[...], wpg_ref[...])
        dh1_ref[...] = dh1
        o = o_ref[...]
        r3 = lax.rsqrt(_lanemean(o * o) + EPS)
        on = o * r3
        gpost_ref[...] += _rowsum8(dh1 * on)
        don = dh1 * pg_ref[...]
        do = r3 * (don - on * _lanemean(don * on))
        dob = do.astype(BF16)
        do_ref[...] = dob
        dy_ref[...] = _dot_nt(dob, wo_ref[...])

        @pl.when(i == pl.num_programs(0) - 1)
        def _():
            gpost_ref[...] = jnp.broadcast_to(jnp.sum(gpost_ref[...], axis=0, keepdims=True), gpost_ref.shape)

    row = lambda n: pl.BlockSpec((tm, n), lambda i: (i, 0))
    const = lambda shp: pl.BlockSpec(shp, lambda i, n=len(shp): (0,) * n, pipeline_mode=pl.Buffered(1))
    return pl.pallas_call(
        body, name="head_bwd", grid=(t // tm,),
        in_specs=[row(D_MODEL), row(D_MODEL), row(D_PLE), row(D_MODEL), const((1, D_MODEL)),
                  const((D_MODEL, D_MODEL)), const((D_MODEL, D_MODEL)), const((N_CHIPS, D_PLE, W_PE_COLS))],
        out_specs=[pl.BlockSpec((N_CHIPS, D_PLE, W_PE_COLS), lambda i: (0, 0, 0)),
                   row(D_MODEL), row(D_MODEL), row(D_MODEL), row(D_MODEL),
                   pl.BlockSpec((SUBLANES, D_MODEL), lambda i: (0, 0))],
        out_shape=[jax.ShapeDtypeStruct((N_CHIPS, D_PLE, W_PE_COLS), F32), jax.ShapeDtypeStruct((t, D_MODEL), BF16),
                   jax.ShapeDtypeStruct((t, D_MODEL), F32), jax.ShapeDtypeStruct((t, D_MODEL), BF16),
                   jax.ShapeDtypeStruct((t, D_MODEL), F32), jax.ShapeDtypeStruct((SUBLANES, D_MODEL), F32)],
        compiler_params=_params("arbitrary"),
    )(dout, gt, p, o, post_g, w_out, w_pg, wg_pe)


def _branches_bwd(z, h, dy, prm, tm, token):
    t = h.shape[0]
    nt = t // tm
    hb = tm // SUBLANES

    def body(z_ref, zh_ref, h_ref, hh_ref, dy_ref,
             lng_ref, lnb_ref, wt_ref, wtt_ref, bsx_ref, cw_ref, cb_ref, wa_ref, wx_ref, ba_ref, bx_ref, lam_ref,
             oga_ref, ogb_ref, token_ref,
             dz_ref, g_oga, g_ogb, g_lng, g_lnb, g_bsx, g_ws, g_cw, g_cb, g_wa, g_ba, g_wx, g_bx, g_lam,
             vn_s, mixed_s, dm_s, dvn_s, xcbf_s, r_s, i_s, a_s, b_s, dh_s, dpr_s, dpi_s, dxc_s,
             ca_s, cd_s, cx_s):
        step_i = pl.program_id(0)
        tile = nt - 1 - step_i
        accs = (g_oga, g_ogb, g_lng, g_lnb, g_bsx, g_ws, g_cw, g_cb, g_wa, g_ba, g_wx, g_bx, g_lam)

        @pl.when(step_i == 0)
        def _():
            for r in accs + (ca_s, cd_s, cx_s):
                r[...] = jnp.zeros_like(r)

        dy_a = dy_ref[:, 0:D_HALF]
        dy_b = dy_ref[:, D_HALF:]

        u = _z_group(z_ref, 0).astype(F32)
        ug, tu = _gelu(u)
        v = _z_group(z_ref, 1).astype(F32)
        vg, tv = _gelu(v)
        vhat, rstd = _layernorm_parts(vg)
        vn_s[...] = (vhat * lng_ref[...] + lnb_ref[...]).astype(BF16)
        _spatial_mix(wt_ref, vn_s, bsx_ref, mixed_s, tm)
        mixed = mixed_s[...]
        ga = _z_group(z_ref, 2).astype(F32)
        sga = jax.nn.sigmoid(ga)
        sa = ga * sga
        um = ug * mixed
        ya = um * sa
        ra = lax.rsqrt(_lanemean(ya * ya) + EPS)
        yahat = ya * ra
        g_oga[...] += _rowsum8(dy_a * yahat)
        dn = dy_a * oga_ref[...]
        dya = ra * (dn - yahat * _lanemean(dn * yahat))
        dz_ref[:, 2 * D_HALF:3 * D_HALF] = (dya * um * (sga * (1.0 + ga * (1.0 - sga)))).astype(BF16)
        dz_ref[:, 0:D_HALF] = (dya * mixed * sa * _gelu_grad(u, tu)).astype(BF16)
        dmixed = dya * ug * sa
        g_bsx[...] += jnp.sum(dmixed.reshape(tm // CHUNK, CHUNK, D_HALF), axis=0)
        dm_s[...] = dmixed.astype(BF16)
        for c in range(tm // CHUNK):
            rows = slice(c * CHUNK, (c + 1) * CHUNK)
            for hd in range(N_HEADS):
                cols = slice(hd * CHUNK, (hd + 1) * CHUNK)
                dmh = dm_s[rows, cols]
                dvn_s[rows, cols] = _dot(wtt_ref[hd], dmh)
                g_ws[hd] += _dot_nt(dmh, vn_s[rows, cols])
        dvn = dvn_s[...]
        g_lng[...] += _rowsum8(dvn * vhat)
        g_lnb[...] += _rowsum8(dvn)
        dvh = dvn * lng_ref[...]
        dvg = rstd * (dvh - _lanemean(dvh) - vhat * _lanemean(dvh * vhat))
        dz_ref[:, D_HALF:2 * D_HALF] = (dvg * _gelu_grad(v, tv)).astype(BF16)

        xb = _z_group(z_ref, 3).astype(F32)
        halo = jnp.where(tile == 0, 0.0, _z_group(zh_ref, 3).astype(F32)[SUBLANES:])
        taps = _conv_taps(xb, halo)
        xc = cb_ref[...] + taps[0] * cw_ref[0:1, :]
        for k in range(1, CONV_W):
            xc = xc + taps[k] * cw_ref[k:k + 1, :]
        xcbf_s[...] = xc.astype(BF16)
        _lru_gates(xcbf_s, wa_ref, wx_ref, ba_ref, bx_ref, r_s, i_s)
        rg = r_s[...]
        ig = i_s[...]
        lam = lam_ref[...]
        a, mult_true = _decay_parts(rg, lam)
        row = lax.broadcasted_iota(jnp.int32, a.shape, 0)
        first = jnp.logical_and(tile == 0, row == 0)
        mult = jnp.where(first, 1.0, mult_true)
        hcur = h_ref[...]
        hprev = _shift_down(hcur, jnp.where(tile == 0, 0.0, hh_ref[...]), 1)
        gb = _z_group(z_ref, 4).astype(F32)
        sgb = jax.nn.sigmoid(gb)
        sb = gb * sgb
        yb = hcur * sb
        rb = lax.rsqrt(_lanemean(yb * yb) + EPS)
        ybhat = yb * rb
        g_ogb[...] += _rowsum8(dy_b * ybhat)
        dn = dy_b * ogb_ref[...]
        dyb = rb * (dn - ybhat * _lanemean(dn * ybhat))
        dz_ref[:, 4 * D_HALF:5 * D_HALF] = (dyb * hcur * (sgb * (1.0 + gb * (1.0 - sgb)))).astype(BF16)

        an = _shift_up(a, ca_s[...], 1)
        bb = dyb * sb
        r8 = row & (SUBLANES - 1)
        for d in (1, 2, 4):
            a_sh = pltpu.roll(an, tm - d, 0)
            b_sh = pltpu.roll(bb, tm - d, 0)
            m = r8 + d < SUBLANES
            bb = jnp.where(m, an * b_sh + bb, bb)
            an = jnp.where(m, an * a_sh, an)
        a_s[...] = an
        b_s[...] = bb

        def step(g, carry):
            sl = pl.ds(pl.multiple_of((hb - 1 - g) * SUBLANES, SUBLANES), SUBLANES)
            dg = a_s[sl, :] * carry + b_s[sl, :]
            dh_s[sl, :] = dg
            return jnp.broadcast_to(dg[0:1, :], dg.shape)

        cd_s[...] = lax.fori_loop(0, hb, step, cd_s[...])
        ca_s[...] = jnp.broadcast_to(a[0:1, :], ca_s.shape)
        dh = dh_s[...]
        da = dh * hprev
        gx = ig * xc
        dla = da * a - jnp.where(first, 0.0, dh * gx * (a * a / mult_true))
        g_lam[...] += _rowsum8(dla * rg)
        dr = dla * (-LRU_C * _softplus_neg(lam))
        dpr = dr * rg * (1.0 - rg)
        dpi = (dh * mult * xc) * ig * (1.0 - ig)
        g_ba[...] += _rowsum8(dpr)
        g_bx[...] += _rowsum8(dpi)
        dpr_s[...] = dpr.astype(BF16)
        dpi_s[...] = dpi.astype(BF16)
        for hd in range(N_HEADS):
            cols = slice(hd * CHUNK, (hd + 1) * CHUNK)
            xh = xcbf_s[:, cols]
            dprh = dpr_s[:, cols]
            dpih = dpi_s[:, cols]
            g_wa[hd] += _dot_tn(xh, dprh)
            g_wx[hd] += _dot_tn(xh, dpih)
            dxc_s[:, cols] = _dot_nt(dprh, wa_ref[hd]) + _dot_nt(dpih, wx_ref[hd])
        dxc = dxc_s[...] + dh * mult * ig
        g_cb[...] += _rowsum8(dxc)
        for k in range(CONV_W):
            g_cw[k * SUBLANES:(k + 1) * SUBLANES, :] += _rowsum8(dxc * taps[k])
        nxt = cx_s[...]
        dxb = dxc * cw_ref[CONV_W - 1:CONV_W, :]
        for j in range(1, CONV_W):
            dxb = dxb + _shift_up(dxc, nxt, j) * cw_ref[CONV_W - 1 - j:CONV_W - j, :]
        dz_ref[:, 3 * D_HALF:4 * D_HALF] = dxb.astype(BF16)
        cx_s[...] = dxc[0:SUBLANES]

        @pl.when(step_i == nt - 1)
        def _():
            for r in (g_oga, g_ogb, g_lng, g_lnb, g_cb, g_ba, g_bx):
                r[...] = jnp.broadcast_to(jnp.sum(r[...], axis=0, keepdims=True), r.shape)
            lam_f = LRU_C * jax.nn.sigmoid(-lam_ref[...])
            g_lam[...] = jnp.broadcast_to(jnp.sum(g_lam[...], axis=0, keepdims=True) * lam_f, g_lam.shape)
            for k in range(CONV_W):
                blk = g_cw[k * SUBLANES:(k + 1) * SUBLANES, :]
                g_cw[k * SUBLANES:(k + 1) * SUBLANES, :] = jnp.broadcast_to(jnp.sum(blk, axis=0, keepdims=True), blk.shape)
            tri = (lax.broadcasted_iota(jnp.int32, (CHUNK, CHUNK), 0) >= lax.broadcasted_iota(jnp.int32, (CHUNK, CHUNK), 1))
            for hd in range(N_HEADS):
                cols = slice(hd * CHUNK, (hd + 1) * CHUNK)
                g_ws[hd] = jnp.where(tri, g_ws[hd], 0.0)
                blk = g_bsx[:, cols]
                g_bsx[:, cols] = jnp.broadcast_to(jnp.sum(blk, axis=1, keepdims=True), blk.shape)

    rev = lambda i: nt - 1 - i
    zspec = pl.BlockSpec((N_CHIPS, tm, W_IN_COLS), lambda i: (0, rev(i), 0))
    halo = lambda col: pl.BlockSpec((SUBLANES, D_HALF), lambda i: (jnp.maximum(rev(i) * hb - 1, 0), col))
    zhalo = pl.BlockSpec((N_CHIPS, 2 * SUBLANES, W_IN_COLS), lambda i: (0, jnp.maximum(rev(i) * (hb // 2) - 1, 0), 0))
    full = lambda a: pl.BlockSpec(a.shape, lambda i, n=a.ndim: (0,) * n)
    acc = lambda shp: pl.BlockSpec(shp, lambda i, n=len(shp): (0,) * n)
    names = ("ln_g", "ln_b", "wt", "wtt", "bsx", "conv_w", "conv_b", "w_a", "w_x", "b_a", "b_x", "lam", "oga", "ogb")
    pr = [prm[n] for n in names] + [token]
    vec = (SUBLANES, D_HALF)
    mat = (N_HEADS, CHUNK, CHUNK)
    acc_shapes = [vec, vec, vec, vec, (CHUNK, D_HALF), mat, (CONV_W * SUBLANES, D_HALF), vec, mat, vec, mat, vec, vec]
    big = lambda dt: pltpu.VMEM((tm, D_HALF), dt)
    return pl.pallas_call(
        body, name="branches_bwd", grid=(nt,),
        in_specs=[zspec, zhalo,
                  pl.BlockSpec((tm, D_HALF), lambda i: (rev(i), 0)), halo(0),
                  pl.BlockSpec((tm, D_MODEL), lambda i: (rev(i), 0))] + [full(a) for a in pr],
        out_specs=[pl.BlockSpec((tm, D_Z), lambda i: (rev(i), 0))] + [acc(s) for s in acc_shapes],
        out_shape=[jax.ShapeDtypeStruct((t, D_Z), BF16)] + [jax.ShapeDtypeStruct(s, F32) for s in acc_shapes],
        scratch_shapes=[big(BF16), big(F32), big(BF16), big(F32), big(BF16), big(F32), big(F32), big(F32), big(F32),
                        big(F32), big(BF16), big(BF16), big(F32),
                        pltpu.VMEM(vec, F32), pltpu.VMEM(vec, F32), pltpu.VMEM(vec, F32)],
        compiler_params=_params("arbitrary"),
    )(z, z, h, h, dy, *pr)


def _inproj_bwd(dz, wg_in, x, dh1, pre_g, tm, tile0, nt, prev, last, token, name):
    t = x.shape[0]

    def body(*refs):
        dz_ref, w_ref, x_ref, dh1_ref, g_ref = refs[:5]
        gx_ref, gpre_ref, acc_s = refs[-3:]
        i = pl.program_id(0)

        @pl.when(i == 0)
        def _():
            gpre_ref[...] = jnp.zeros_like(gpre_ref) if prev is None else refs[7][...]

        acc = _dot_nt(dz_ref[:, 0:W_IN_COLS], w_ref[0])
        for k in range(1, N_CHIPS):
            acc = acc + _dot_nt(dz_ref[:, k * W_IN_COLS:(k + 1) * W_IN_COLS], w_ref[k])
        acc_s[...] = acc
        for s in range(tm // CHUNK):
            rows = slice(s * CHUNK, (s + 1) * CHUNK)
            xv = x_ref[rows, :]
            r = lax.rsqrt(_lanemean(xv * xv) + EPS)
            xhat = xv * r
            dhn = acc_s[rows, :]
            gpre_ref[...] += _rowsum8(dhn * xhat)
            dxh = dhn * g_ref[...]
            gx_ref[rows, :] = dh1_ref[rows, :] + r * (dxh - xhat * _lanemean(dxh * xhat))

        if last:
            @pl.when(i == nt - 1)
            def _():
                gpre_ref[...] = jnp.broadcast_to(jnp.sum(gpre_ref[...], axis=0, keepdims=True), gpre_ref.shape)

    row = lambda n: pl.BlockSpec((tm, n), lambda i: (tile0 + i, 0))
    small = lambda r: pl.BlockSpec((r, D_MODEL), lambda i: (0, 0))
    tok = pl.BlockSpec((SUBLANES, LANES), lambda i: (0, 0))
    in_specs = [row(D_Z), pl.BlockSpec(wg_in.shape, lambda i: (0, 0, 0), pipeline_mode=pl.Buffered(1)),
                row(D_MODEL), row(D_MODEL), small(1), tok]
    args = [dz, wg_in, x, dh1, pre_g, token]
    aliases = {}
    if prev is not None:
        in_specs += [ANY, small(SUBLANES)]
        args += list(prev)
        aliases = {6: 0}
    return pl.pallas_call(
        body, name=name, grid=(nt,), in_specs=in_specs, out_specs=[row(D_MODEL), small(SUBLANES)],
        out_shape=[jax.ShapeDtypeStruct((t, D_MODEL), F32), jax.ShapeDtypeStruct((SUBLANES, D_MODEL), F32)],
        input_output_aliases=aliases,
        scratch_shapes=[pltpu.VMEM((tm, D_MODEL), F32)],
        compiler_params=_params("arbitrary"),
    )(*args)


def _weight_grad(a, b, name, kb, nb, tk, tn, tt, token):
    t = a.shape[0]
    tt = min(tt, t)

    def body(a_ref, b_ref, token_ref, o_ref):
        @pl.when(pl.program_id(2) == 0)
        def _():
            o_ref[...] = jnp.zeros_like(o_ref)

        o_ref[...] += _dot_tn(a_ref[...], b_ref[...])

    return pl.pallas_call(
        body, name=name, grid=(nb, kb, t // tt),
        in_specs=[pl.BlockSpec((tt, tk), lambda j, i, s: (s, i)), pl.BlockSpec((tt, tn), lambda j, i, s: (s, j)),
                  pl.BlockSpec((SUBLANES, LANES), lambda j, i, s: (0, 0))],
        out_specs=pl.BlockSpec((None, None, tk, tn), lambda j, i, s: (j, i, 0, 0)),
        out_shape=jax.ShapeDtypeStruct((nb, kb, tk, tn), F32),
        compiler_params=_params("parallel", "parallel", "arbitrary"),
    )(a, b, token)


def _place():
    x, y, c = lax.axis_index("x"), lax.axis_index("y"), lax.axis_index("c")
    return x, y, c


def _chip_of(x, y):
    return 2 * x + y


HBM = pl.BlockSpec(memory_space=pltpu.HBM)
SEM = pl.BlockSpec(memory_space=pltpu.SEMAPHORE)
EFFECT = pltpu.SideEffectType.DATAFLOW_SIDE_EFFECTING


def _hbm(a):
    return pltpu.with_memory_space_constraint(a, pltpu.HBM)


def _landing(shape, dtype):
    return _hbm(lax.empty(shape, dtype))


def _exchange_start(name, arrays, ncopies, build, after=None):
    n = len(arrays)
    extra = [] if after is None else [after]

    def body(*refs):
        ins, token = refs[:n], refs[-1]
        send_sems, recv_sems = refs[n + len(extra)], refs[n + len(extra) + 1]
        for cp in build(ins, send_sems, recv_sems):
            cp.start()
        token[...] = jnp.zeros_like(token)

    outs = pl.pallas_call(
        body, name=name,
        out_shape=(pltpu.SemaphoreType.DMA((ncopies,)), pltpu.SemaphoreType.DMA((ncopies,)),
                   *[pltpu.HBM(a.shape, a.dtype) for a in arrays], jax.ShapeDtypeStruct((SUBLANES, LANES), F32)),
        in_specs=[HBM] * n + [ANY] * len(extra),
        out_specs=(SEM, SEM, *[HBM] * n, pl.BlockSpec(memory_space=pltpu.VMEM)),
        input_output_aliases={q: q + 2 for q in range(n)},
        compiler_params=pltpu.CompilerParams(has_side_effects=EFFECT),
    )(*[_hbm(a) for a in arrays], *extra)
    return (outs[0], outs[1], list(outs[2:2 + n])), outs[-1]


def _exchange_wait(name, started, after, build):
    send, recv, arrays = started
    n = len(arrays)

    def body(*refs):
        ins, send_sems, recv_sems = refs[:n], refs[n], refs[n + 1]
        for cp in build(ins, send_sems, recv_sems):
            cp.wait_send()
            cp.wait_recv()

    return pl.pallas_call(
        body, name=name, out_shape=tuple(pltpu.HBM(a.shape, a.dtype) for a in arrays),
        in_specs=[HBM] * n + [SEM, SEM, ANY], out_specs=tuple([HBM] * n),
        input_output_aliases={q: q for q in range(n)},
        compiler_params=pltpu.CompilerParams(has_side_effects=EFFECT),
    )(*arrays, send, recv, after)


def _exchange_wait_start(name, started, after, build_wait, ncopies, build_start):
    send, recv, arrays = started
    n = len(arrays)

    def body(*refs):
        ins, send_sems, recv_sems = refs[:n], refs[n], refs[n + 1]
        send2, recv2, token = refs[n + 3], refs[n + 4], refs[-1]
        arrived = build_wait(ins, send_sems, recv_sems)
        for cp, onward in zip(arrived, build_start(ins, send2, recv2)):
            cp.wait_recv()
            onward.start()
        for cp in arrived:
            cp.wait_send()
        token[...] = jnp.zeros_like(token)

    outs = pl.pallas_call(
        body, name=name,
        out_shape=(pltpu.SemaphoreType.DMA((ncopies,)), pltpu.SemaphoreType.DMA((ncopies,)),
                   *[pltpu.HBM(a.shape, a.dtype) for a in arrays], jax.ShapeDtypeStruct((SUBLANES, LANES), F32)),
        in_specs=[HBM] * n + [SEM, SEM, ANY], out_specs=(SEM, SEM, *[HBM] * n, pl.BlockSpec(memory_space=pltpu.VMEM)),
        input_output_aliases={q: q + 2 for q in range(n)},
        compiler_params=pltpu.CompilerParams(has_side_effects=EFFECT),
    )(*arrays, send, recv, after)
    return (outs[0], outs[1], list(outs[2:2 + n])), outs[-1]


def _cast_into_slot(w, kc, name, dtype=BF16, token=None):
    rows, cols = w.shape
    tr = min(rows, 256)
    extra = [] if token is None else [token]

    def body(kc_ref, w_ref, *rest):
        rest[-1][...] = w_ref[...].astype(dtype)

    grid_spec = pltpu.PrefetchScalarGridSpec(
        num_scalar_prefetch=1, grid=(rows // tr,),
        in_specs=[pl.BlockSpec((tr, cols), lambda r, kc: (r, 0))]
                 + [pl.BlockSpec((SUBLANES, LANES), lambda r, kc: (0, 0))] * len(extra),
        out_specs=pl.BlockSpec((None, tr, cols), lambda r, kc: (kc[0], r, 0)))
    return pl.pallas_call(
        body, name=name, grid_spec=grid_spec, out_shape=jax.ShapeDtypeStruct((N_CHIPS, rows, cols), dtype),
        compiler_params=_params("arbitrary"),
    )(kc, w, *extra)


def _gather_ici_copies(n):
    def build(refs, send_sems, recv_sems):
        x, y, c = _place()
        mine = lambda b: refs[b].at[_chip_of(x, y), c]
        chips = [(1 - x, y), (x, 1 - y), (1 - x, 1 - y)]
        return [pltpu.make_async_remote_copy(
            src_ref=mine(b), dst_ref=mine(b), send_sem=send_sems.at[3 * b + j], recv_sem=recv_sems.at[3 * b + j],
            device_id=(*chip, c), device_id_type=MESH) for b in range(n) for j, chip in enumerate(chips)]
    return build


def _gather_relay_copies(n):
    def build(refs, send_sems, recv_sems):
        x, y, c = _place()
        chips = [(1 - x, y), (x, 1 - y), (1 - x, 1 - y)]
        cps = []
        for b in range(n):
            for j, chip in enumerate(chips):
                got = refs[b].at[_chip_of(*chip), c]
                cps.append(pltpu.make_async_remote_copy(
                    src_ref=got, dst_ref=got, send_sem=send_sems.at[3 * b + j], recv_sem=recv_sems.at[3 * b + j],
                    device_id=(x, y, 1 - c), device_id_type=MESH))
        return cps
    return build


def _sibling_copies(n):
    def build(refs, send_sems, recv_sems):
        x, y, c = _place()
        return [pltpu.make_async_remote_copy(
            src_ref=refs[b].at[:, 1 - c], dst_ref=refs[n + b], send_sem=send_sems.at[b], recv_sem=recv_sems.at[b],
            device_id=(x, y, 1 - c), device_id_type=MESH) for b in range(n)]
    return build


def _chip_copies(n):
    def build(refs, send_sems, recv_sems):
        x, y, c = _place()
        chips = [(1 - x, y), (x, 1 - y), (1 - x, 1 - y)]
        return [pltpu.make_async_remote_copy(
            src_ref=refs[b].at[_chip_of(*chip)], dst_ref=refs[n + b].at[j],
            send_sem=send_sems.at[3 * b + j], recv_sem=recv_sems.at[3 * b + j],
            device_id=(*chip, c), device_id_type=MESH) for b in range(n) for j, chip in enumerate(chips)]
    return build


def _finish_copies(n, n_all):
    def build(refs, send_sems, recv_sems):
        x, y, c = _place()
        cps = [pltpu.make_async_remote_copy(
            src_ref=refs[b].at[c], dst_ref=refs[b].at[c], send_sem=send_sems.at[b], recv_sem=recv_sems.at[b],
            device_id=(x, y, 1 - c), device_id_type=MESH) for b in range(n)]
        flips = [(fx, fy, fc) for fx in (0, 1) for fy in (0, 1) for fc in (0, 1)][1:]
        for b in range(n_all):
            mine = refs[n + b].at[_chip_of(x, y), c]
            cps += [pltpu.make_async_remote_copy(
                src_ref=mine, dst_ref=mine, send_sem=send_sems.at[n + 7 * b + q], recv_sem=recv_sems.at[n + 7 * b + q],
                device_id=(x ^ fx, y ^ fy, c ^ fc), device_id_type=MESH) for q, (fx, fy, fc) in enumerate(flips)]
        return cps
    return build


def _pair_sum(g, r1, kc, name, tr, send_dtype):
    nk, _, rows, cols = g.shape

    def body(kc_ref, g_ref, r_ref, p_ref, own_ref):
        s = g_ref[...] + r_ref[...]
        p_ref[...] = s.astype(send_dtype)

        @pl.when(pl.program_id(1) == kc_ref[0])
        def _():
            own_ref[...] = s

    grid_spec = pltpu.PrefetchScalarGridSpec(
        num_scalar_prefetch=1, grid=(rows // tr, nk),
        in_specs=[pl.BlockSpec((None, None, tr, cols), lambda r, k, kc: (k, kc[1], r, 0)),
                  pl.BlockSpec((None, tr, cols), lambda r, k, kc: (k, r, 0))],
        out_specs=[pl.BlockSpec((None, tr, cols), lambda r, k, kc: (k, r, 0)),
                   pl.BlockSpec((tr, cols), lambda r, k, kc: (r, 0))])
    return pl.pallas_call(
        body, name=name, grid_spec=grid_spec,
        out_shape=[jax.ShapeDtypeStruct((nk, rows, cols), send_dtype), jax.ShapeDtypeStruct((rows, cols), F32)],
        compiler_params=_params("arbitrary", "arbitrary"),
    )(kc, g, r1)


def _chip_sum(own, r2, slot, lead, name, tr):
    rows, cols = own.shape
    nl = len(lead)

    def body(slot_ref, o_ref, r_ref, s_ref):
        s = o_ref[...]
        for j in range(3):
            s = s + r_ref[j].astype(F32)
        s_ref[...] = s

    grid_spec = pltpu.PrefetchScalarGridSpec(
        num_scalar_prefetch=1, grid=(rows // tr,),
        in_specs=[pl.BlockSpec((tr, cols), lambda r, sl: (r, 0)), pl.BlockSpec((3, tr, cols), lambda r, sl: (0, r, 0))],
        out_specs=pl.BlockSpec((None,) * nl + (tr, cols), lambda r, sl: tuple(sl[q] for q in range(nl)) + (r, 0)))
    return pl.pallas_call(
        body, name=name, grid_spec=grid_spec, out_shape=jax.ShapeDtypeStruct(tuple(lead) + (rows, cols), F32),
        compiler_params=_params("arbitrary"),
    )(slot, own, r2)


def _adam_update(w, g, m, v):
    nm = ADAM_B1 * m + (1.0 - ADAM_B1) * g
    nv = ADAM_B2 * v + (1.0 - ADAM_B2) * (g * g)
    m_hat = nm / (1.0 - ADAM_B1 ** ADAM_STEP)
    v_hat = nv / (1.0 - ADAM_B2 ** ADAM_STEP)
    return -ADAM_LR * (m_hat / (jnp.sqrt(v_hat) + ADAM_EPS) + ADAM_WD * w), nm, nv


def _adamw(w, g, m, v, name, tr, token):
    rows, cols = w.shape

    def body(w_ref, g_ref, m_ref, v_ref, token_ref, go_ref, d_ref, nm_ref, nv_ref):
        gv = g_ref[...]
        go_ref[...] = gv
        d_ref[...], nm_ref[...], nv_ref[...] = _adam_update(w_ref[...], gv, m_ref[...], v_ref[...])

    spec = pl.BlockSpec((tr, cols), lambda r: (r, 0))
    return pl.pallas_call(
        body, name=name, grid=(rows // tr,),
        in_specs=[spec] * 4 + [pl.BlockSpec((SUBLANES, LANES), lambda r: (0, 0))], out_specs=[spec] * 4,
        out_shape=[jax.ShapeDtypeStruct((rows, cols), F32)] * 4,
        compiler_params=_params("parallel"),
    )(w, g, m, v, token)


def _adamw_small(packed_g, pre_g_parts, ws, ms, vs):
    names = ["pre_g"] + [n for n, _ in SMALL_ROWS if n != "conv_w"]
    rows = dict(SMALL_ROWS)
    offset, at = {}, 0
    for n, r in SMALL_ROWS:
        offset[n] = at
        at += r
    k = len(names)

    def body(*refs):
        g_ref, pg_ref = refs[0], refs[1]
        w_refs, m_refs, v_refs = refs[2:2 + k], refs[2 + k:2 + 2 * k], refs[2 + 2 * k:2 + 3 * k]
        outs = refs[2 + 3 * k:]
        go, do, mo, vo = outs[:k], outs[k:2 * k], outs[2 * k:3 * k], outs[3 * k:4 * k]
        pre = pg_ref[0]
        for dev in range(1, 8):
            pre = pre + pg_ref[dev]
        outs[4 * k][...] = pre[D_MODEL // LANES:, :]
        for i, n in enumerate(names):
            shp = w_refs[i].shape
            if len(shp) == 2 and shp[0] == 1:
                for r in range(shp[1] // LANES):
                    cols = slice(r * LANES, (r + 1) * LANES)
                    g = pre[r:r + 1, :] if n == "pre_g" else g_ref[offset[n] + r:offset[n] + r + 1, :]
                    go[i][:, cols] = g
                    do[i][:, cols], mo[i][:, cols], vo[i][:, cols] = _adam_update(
                        w_refs[i][:, cols], g, m_refs[i][:, cols], v_refs[i][:, cols])
            else:
                g = g_ref[offset[n]:offset[n] + rows[n], :].reshape(shp)
                go[i][...] = g
                do[i][...], mo[i][...], vo[i][...] = _adam_update(w_refs[i][...], g, m_refs[i][...], v_refs[i][...])

    vm = pl.BlockSpec(memory_space=pltpu.VMEM)
    args = [packed_g, pre_g_parts] + [src[n] for src in (ws, ms, vs) for n in names]
    out_shape = [jax.ShapeDtypeStruct(ws[n].shape, F32) for _ in range(4) for n in names]
    out_shape.append(jax.ShapeDtypeStruct((SUBLANES, LANES), F32))
    outs = pl.pallas_call(
        body, name="adamw_small", in_specs=[vm] * len(args), out_specs=[vm] * (4 * k + 1), out_shape=out_shape,
    )(*args)
    return [dict(zip(names, outs[q * k:(q + 1) * k])) for q in range(4)], outs[4 * k]


def _into_slot(v, tail, slot, lead, name):
    n = v.shape[1]
    nl = len(lead)
    rows = n // LANES + SUBLANES

    def body(slot_ref, v_ref, t_ref, o_ref):
        for r in range(n // LANES):
            o_ref[r:r + 1, :] = v_ref[0:1, r * LANES:(r + 1) * LANES]
        o_ref[n // LANES:, :] = t_ref[...]

    grid_spec = pltpu.PrefetchScalarGridSpec(
        num_scalar_prefetch=1, grid=(1,),
        in_specs=[pl.BlockSpec(v.shape, lambda i, sl: (0, 0)), pl.BlockSpec(tail.shape, lambda i, sl: (0, 0))],
        out_specs=pl.BlockSpec((None,) * nl + (rows, LANES), lambda i, sl: tuple(sl[q] for q in range(nl)) + (0, 0)))
    return pl.pallas_call(
        body, name=name, grid_spec=grid_spec, out_shape=jax.ShapeDtypeStruct(tuple(lead) + (rows, LANES), F32),
    )(slot, v, tail)


def _rows128(a):
    return a.reshape(-1, LANES)


def _pack_small(parts):
    pieces = [_rows128(parts[n]) for n, _ in SMALL_ROWS]
    pieces.append(jnp.zeros((SMALL_TOTAL - SMALL_USED, LANES), F32))
    return jnp.concatenate(pieces, axis=0)


def kernel(x, p, pre_g, w_in, gmlp_ln_g, gmlp_ln_b, gmlp_ws, gmlp_bs, conv_w, conv_b, w_a, b_a, w_x, b_x, lam, gmlp_out_g, lru_out_g, w_out, post_g, w_pe, w_pg, loss_target, m_pre_g, m_w_in, m_gmlp_ln_g, m_gmlp_ln_b, m_gmlp_ws, m_gmlp_bs, m_conv_w, m_conv_b, m_w_a, m_b_a, m_w_x, m_b_x, m_lam, m_gmlp_out_g, m_lru_out_g, m_w_out, m_post_g, m_w_pe, m_w_pg, v_pre_g, v_w_in, v_gmlp_ln_g, v_gmlp_ln_b, v_gmlp_ws, v_gmlp_bs, v_conv_w, v_conv_b, v_w_a, v_b_a, v_w_x, v_b_x, v_lam, v_gmlp_out_g, v_lru_out_g, v_w_out, v_post_g, v_w_pe, v_w_pg):
    weights = dict(pre_g=pre_g, w_in=w_in, gmlp_ln_g=gmlp_ln_g, gmlp_ln_b=gmlp_ln_b, gmlp_ws=gmlp_ws, gmlp_bs=gmlp_bs,
                   conv_w=conv_w, conv_b=conv_b, w_a=w_a, b_a=b_a, w_x=w_x, b_x=b_x, lam=lam, gmlp_out_g=gmlp_out_g,
                   lru_out_g=lru_out_g, w_out=w_out, post_g=post_g, w_pe=w_pe, w_pg=w_pg)
    mom_m = dict(pre_g=m_pre_g, w_in=m_w_in, gmlp_ln_g=m_gmlp_ln_g, gmlp_ln_b=m_gmlp_ln_b, gmlp_ws=m_gmlp_ws,
                 gmlp_bs=m_gmlp_bs, conv_w=m_conv_w, conv_b=m_conv_b, w_a=m_w_a, b_a=m_b_a, w_x=m_w_x, b_x=m_b_x,
                 lam=m_lam, gmlp_out_g=m_gmlp_out_g, lru_out_g=m_lru_out_g, w_out=m_w_out, post_g=m_post_g,
                 w_pe=m_w_pe, w_pg=m_w_pg)
    mom_v = dict(pre_g=v_pre_g, w_in=v_w_in, gmlp_ln_g=v_gmlp_ln_g, gmlp_ln_b=v_gmlp_ln_b, gmlp_ws=v_gmlp_ws,
                 gmlp_bs=v_gmlp_bs, conv_w=v_conv_w, conv_b=v_conv_b, w_a=v_w_a, b_a=v_b_a, w_x=v_w_x, b_x=v_b_x,
                 lam=v_lam, gmlp_out_g=v_gmlp_out_g, lru_out_g=v_lru_out_g, w_out=v_w_out, post_g=v_post_g,
                 w_pe=v_w_pe, w_pg=v_w_pg)
    order = list(weights)
    xi, yi, ci = _place()
    me = _chip_of(xi, yi)
    kc = jnp.stack([me, ci]).astype(jnp.int32)

    x2 = x[0]
    p2 = p[0, 0]
    tgt = loss_target[0]

    first = [_cast_into_slot(w_in[0], kc, "cast_w_in").reshape(N_CHIPS, 2, D_MODEL // 2, W_IN_COLS),
             _cast_into_slot(conv_w[0, :, 0, :], kc, "conv_w_into_slot", F32).reshape(N_CHIPS, 2, CONV_W // 2, CONV_COLS)]
    in_st, in_tok = _exchange_start("gather_in_start", first, 6, _gather_ici_copies(2))
    later = [_cast_into_slot(w_out[0], kc, "cast_w_out", token=in_tok).reshape(N_CHIPS, 2, W_ROWS // 2, D_MODEL),
             _cast_into_slot(w_pg[0], kc, "cast_w_pg", token=in_tok).reshape(N_CHIPS, 2, W_ROWS // 2, D_MODEL),
             _cast_into_slot(w_pe[0], kc, "cast_w_pe", token=in_tok).reshape(N_CHIPS, 2, D_PLE // 2, W_PE_COLS)]
    gather_st, gather_tok = _exchange_start("gather_start", later, 9, _gather_ici_copies(3), after=in_tok)
    hn, z_own = _inproj_local(x2, pre_g, w_in[0], ROW_TILE, gather_tok)
    in_st, in_tok = _exchange_wait_start("gather_in_relay", in_st, z_own, _gather_ici_copies(2), 6,
                                         _gather_relay_copies(2))
    g_in, g_cw = _exchange_wait("gather_in_wait", in_st, in_tok, _gather_relay_copies(2))
    wg_in = g_in.reshape(N_CHIPS, D_MODEL, W_IN_COLS)
    cw_full = jnp.transpose(g_cw.reshape(N_CHIPS, CONV_W, CONV_COLS), (1, 0, 2)).reshape(CONV_W, D_HALF)

    causal = jnp.tril(jnp.ones((CHUNK, CHUNK), dtype=bool))
    ws_m = jnp.where(causal[None], gmlp_ws[0], 0.0)
    prm = dict(
        ln_g=gmlp_ln_g, ln_b=gmlp_ln_b, wt=ws_m.astype(BF16), wtt=jnp.transpose(ws_m, (0, 2, 1)).astype(BF16),
        bsx=jnp.repeat(jnp.transpose(gmlp_bs[0]), CHUNK, axis=1),
        conv_w=cw_full, conv_b=conv_b, w_a=w_a[0].astype(BF16), w_x=w_x[0].astype(BF16),
        b_a=b_a[0].reshape(1, D_HALF), b_x=b_x[0].reshape(1, D_HALF), lam=lam, oga=gmlp_out_g, ogb=lru_out_g)

    z, y, h = _inproj_branches_fwd(hn, z_own, wg_in, kc, prm, ROW_TILE, gather_tok)
    gather_st, gather_tok = _exchange_wait_start("gather_relay", gather_st, y, _gather_ici_copies(3), 9,
                                                 _gather_relay_copies(3))
    g_out, g_pg, g_pe = _exchange_wait("gather_wait", gather_st, gather_tok, _gather_relay_copies(3))
    wg_out = g_out.reshape(D_MODEL, D_MODEL)
    wg_pg = g_pg.reshape(D_MODEL, D_MODEL)
    wg_pe = g_pe.reshape(N_CHIPS, D_PLE, W_PE_COLS)
    o, h1, gt, dout, loss_acc = _outproj_fwd(x2, y, p2, tgt, post_g, wg_out, wg_pg, wg_pe, ROW_TILE)

    def sibling_start(tag, bufs):
        lands = [_landing((b.shape[0],) + b.shape[2:], b.dtype) for b in bufs]
        return _exchange_start("sibling_start_" + tag, bufs + lands, len(bufs), _sibling_copies(len(bufs)))

    def pair_then_chip_start(tag, started, after, names, tiles, dtypes):
        n = len(names)
        got = _exchange_wait("sibling_wait_" + tag, started, after, _sibling_copies(n))
        pairs = [_pair_sum(got[b], got[n + b], kc, "pair_sum_" + names[b], tiles[b], dtypes[b]) for b in range(n)]
        lands = [_landing((3,) + pr[0].shape[1:], pr[0].dtype) for pr in pairs]
        return _exchange_start("chip_start_" + tag, [pr[0] for pr in pairs] + lands, 3 * n, _chip_copies(n)), pairs

    def sum_then_finish_start(tag, started, pairs, after, names, tiles, small, to_all=()):
        n = len(names)
        got = _exchange_wait("chip_wait_" + tag, started, after, _chip_copies(n))
        sums = [_chip_sum(pairs[b][1], got[n + b], kc if small and b == n - 1 else kc[1:],
                          (N_CHIPS, 2) if small and b == n - 1 else (2,), "chip_sum_" + names[b], tiles[b])
                for b in range(n)]
        nbig = n - 1 if small else n
        n_all = n - nbig + len(to_all)
        return _exchange_start("finish_start_" + tag, sums + list(to_all), nbig + 7 * n_all,
                               _finish_copies(nbig, n_all))

    gw_pe, dq, dh1, do, dy, g_post = _head_bwd(dout, gt, p2, o, post_g, wg_out, wg_pg, wg_pe, ROW_TILE)
    gw_pe = gw_pe.reshape(N_CHIPS, 2, D_PLE // 2, W_PE_COLS)
    token0 = jnp.zeros((SUBLANES, LANES), F32)
    gw_out = _weight_grad(y, do, "grad_w_out", 2, 1, D_MODEL // 2, D_MODEL, CONTRACT_TILE, token0)
    gw_pg = _weight_grad(h1, dq, "grad_w_pg", 2, 1, D_MODEL // 2, D_MODEL, CONTRACT_TILE, token0)
    gw_out = gw_out.reshape(N_CHIPS, 2, W_ROWS // 2, D_MODEL)
    gw_pg = gw_pg.reshape(N_CHIPS, 2, W_ROWS // 2, D_MODEL)

    names_a, tiles_a = ["w_out", "w_pg", "w_pe"], [SUM_TILE] * 3
    st, tok = sibling_start("a", [gw_out, gw_pg, gw_pe])
    (dz, g_oga, g_ogb, g_lng, g_lnb, g_bsx, g_ws, g_cw, g_cb, g_wa, g_ba, g_wx, g_bx, g_lam) = _branches_bwd(
        z, h, dy, prm, ROW_TILE, tok)
    (st, tok), pairs_a = pair_then_chip_start("a", st, dz, names_a, tiles_a, [BF16] * 3)
    gw_in = _weight_grad(hn, dz, "grad_w_in", 2, N_CHIPS, D_MODEL // 2, W_IN_COLS, CONTRACT_TILE, tok)
    fin_a, tok = sum_then_finish_start("a", st, pairs_a, gw_in, names_a, tiles_a, False)

    small_g = dict(
        gmlp_ln_g=g_lng[0:1], gmlp_ln_b=g_lnb[0:1], gmlp_ws=g_ws,
        gmlp_bs=jnp.transpose(g_bsx[:, ::CHUNK]), conv_w=g_cw[::SUBLANES], conv_b=g_cb[0:1], w_a=g_wa, b_a=g_ba[0:1],
        w_x=g_wx, b_x=g_bx[0:1], lam=g_lam[0:1], gmlp_out_g=g_oga[0:1], lru_out_g=g_ogb[0:1], post_g=g_post[0:1])
    gsm = _pack_small(small_g).reshape(N_CHIPS, 2, SMALL_PIECE, LANES)

    names_b, tiles_b = ["w_in", "small"], [2 * SUM_TILE, SMALL_PIECE]
    n_tiles = x2.shape[0] // ROW_TILE
    n_lo = max(1, (5 * n_tiles) // 16)
    st, tok_b = _exchange_start(
        "sibling_start_b", [gw_in, gsm] + [_landing((N_CHIPS,) + b.shape[2:], F32) for b in (gw_in, gsm)], 2,
        _sibling_copies(2), after=tok)
    part = _inproj_bwd(dz, wg_in, x2, dh1, pre_g, ROW_TILE, 0, n_lo, None, False, tok_b, "inproj_bwd_lo")
    f_out, f_pg, f_pe = _exchange_wait("finish_wait_a", fin_a, part[1], _finish_copies(3, 0))
    (st, tok_b), pairs_b = pair_then_chip_start("b", st, part[1], names_b, tiles_b, [BF16, F32])
    grad_x, g_pre = _inproj_bwd(dz, wg_in, x2, dh1, pre_g, ROW_TILE, n_lo, n_tiles - n_lo, part, True, tok_b,
                                "inproj_bwd_hi")
    pre_parts = _into_slot(g_pre, loss_acc, kc, (N_CHIPS, 2), "pre_g_into_slot")
    fin_b, tok_b = sum_then_finish_start("b", st, pairs_b, g_pre, names_b, tiles_b, True, to_all=[pre_parts])

    grads, deltas, new_m, new_v = {}, {}, {}, {}

    def adam_big(n, g2d, tr, token):
        shp = weights[n].shape
        g, d, nm, nv = _adamw(weights[n][0], g2d, mom_m[n][0], mom_v[n][0], "adamw_" + n, tr, token)
        grads[n], deltas[n], new_m[n], new_v[n] = g.reshape(shp), d.reshape(shp), nm.reshape(shp), nv.reshape(shp)
        return d

    as_token = lambda d: d[:SUBLANES, :LANES]
    last = adam_big("w_out", f_out.reshape(W_ROWS, D_MODEL), SUM_TILE, tok_b)
    last = adam_big("w_pg", f_pg.reshape(W_ROWS, D_MODEL), SUM_TILE, as_token(last))
    last = adam_big("w_pe", f_pe.reshape(D_PLE, W_PE_COLS), SUM_TILE, as_token(last))
    f_in, f_sm, pre_parts = _exchange_wait("finish_wait_b", fin_b, last, _finish_copies(1, 2))
    adam_big("w_in", f_in.reshape(D_MODEL, W_IN_COLS), 2 * SUM_TILE, tok_b)

    packed_g = f_sm.reshape(SMALL_TOTAL, LANES)
    small_names = ["pre_g"] + [n for n, _ in SMALL_ROWS if n != "conv_w"]
    natural = lambda src: {n: (src[n] if src[n].ndim == 2 else src[n][0]) for n in small_names}
    outs, loss_block = _adamw_small(packed_g, pre_parts.reshape(8, D_MODEL // LANES + SUBLANES, LANES),
                                    natural(weights), natural(mom_m), natural(mom_v))
    loss = loss_block[0, 0]
    for dst, got in zip((grads, deltas, new_m, new_v), outs):
        for n in small_names:
            dst[n] = got[n].reshape(weights[n].shape)
    at = sum(r for n, r in SMALL_ROWS[:[n for n, _ in SMALL_ROWS].index("conv_w")])
    g_cw_all = packed_g[at:at + CONV_W * D_HALF // LANES].reshape(CONV_W, D_HALF)
    g_conv = lax.dynamic_slice_in_dim(g_cw_all, me * CONV_COLS, CONV_COLS, axis=1)
    g, d, nm, nv = _adamw(conv_w[0, :, 0, :], g_conv, m_conv_w[0, :, 0, :], v_conv_w[0, :, 0, :], "adamw_conv_w", CONV_W,
                          tok_b)
    cshape = conv_w.shape
    grads["conv_w"], deltas["conv_w"] = g.reshape(cshape), d.reshape(cshape)
    new_m["conv_w"], new_v["conv_w"] = nm.reshape(cshape), nv.reshape(cshape)

    return (loss, grad_x.reshape(x.shape), *[grads[n] for n in order], *[deltas[n] for n in order],
            *[new_m[n] for n in order], *[new_v[n] for n in order])
```

```python
import math

import jax
import jax.numpy as jnp
from jax import lax
from jax.experimental import pallas as pl
from jax.experimental.pallas import tpu as pltpu

F32 = jnp.float32
BF16 = jnp.bfloat16

D_MODEL = 2048
D_HALF = 1024
D_Z = 5120
D_PLE = 256
CHUNK = 128
N_HEADS = 8
N_CHIPS = 4
W_IN_COLS = D_Z // N_CHIPS
W_ROWS = D_MODEL // N_CHIPS
W_PE_COLS = D_MODEL // N_CHIPS
CONV_W = 4
CONV_COLS = D_HALF // N_CHIPS
EPS = 1e-6
LRU_C = 8.0
ADAM_LR, ADAM_B1, ADAM_B2, ADAM_EPS, ADAM_WD, ADAM_STEP = 0.001, 0.9, 0.999, 1e-08, 0.01, 10

SUBLANES = 8
LANES = 128
VMEM_LIMIT = 56 * 1024 * 1024
ROW_TILE = 256
CONTRACT_TILE = 2048
SUM_TILE = 128

SMALL_ROWS = (("gmlp_ln_g", 8), ("gmlp_ln_b", 8), ("gmlp_ws", 1024), ("gmlp_bs", 8),
              ("conv_w", 32), ("conv_b", 8), ("w_a", 1024), ("b_a", 8), ("w_x", 1024), ("b_x", 8),
              ("lam", 8), ("gmlp_out_g", 8), ("lru_out_g", 8), ("post_g", 16))
SMALL_USED = sum(r for _, r in SMALL_ROWS)
SMALL_PIECE = 400
SMALL_TOTAL = 8 * SMALL_PIECE

MESH = pl.DeviceIdType.MESH
ANY = pl.BlockSpec(memory_space=pl.ANY)

_GELU_C0 = math.sqrt(2.0 / math.pi)
_GELU_C1 = 0.044715


def _params(*sem):
    return pltpu.CompilerParams(dimension_semantics=sem, vmem_limit_bytes=VMEM_LIMIT)


def _dot(a, b):
    return jnp.dot(a, b, preferred_element_type=F32)


def _dot_nt(a, b):
    return lax.dot_general(a, b, (((1,), (1,)), ((), ())), preferred_element_type=F32)


def _dot_tn(a, b):
    return lax.dot_general(a, b, (((0,), (0,)), ((), ())), preferred_element_type=F32)


def _gelu(x):
    t = jnp.tanh(_GELU_C0 * (x + _GELU_C1 * (x * x * x)))
    return 0.5 * x * (1.0 + t), t


def _gelu_grad(x, t):
    return 0.5 * (1.0 + t) + 0.5 * x * (1.0 - t * t) * (_GELU_C0 * (1.0 + 3.0 * _GELU_C1 * x * x))


def _rowsum8(v):
    r, n = v.shape
    return jnp.sum(v.reshape(r // SUBLANES, SUBLANES, n), axis=0)


def _lanemean(v):
    return jnp.mean(v, axis=-1, keepdims=True)


def _shift_down(v, halo8, k):
    if k == 0:
        return v
    r = pltpu.roll(v, k, 0)
    hr = pltpu.roll(halo8, k, 0)
    row = lax.broadcasted_iota(jnp.int32, halo8.shape, 0)
    top = jnp.where(row < k, hr, r[0:SUBLANES])
    return jnp.concatenate([top, r[SUBLANES:]], axis=0)


def _shift_up(v, next8, k):
    if k == 0:
        return v
    n = v.shape[0]
    r = pltpu.roll(v, n - k, 0)
    nr = pltpu.roll(next8, SUBLANES - k, 0)
    row = lax.broadcasted_iota(jnp.int32, next8.shape, 0)
    bot = jnp.where(row >= SUBLANES - k, nr, r[n - SUBLANES:])
    return jnp.concatenate([r[:n - SUBLANES], bot], axis=0)


def _layernorm_parts(vg):
    mu = _lanemean(vg)
    xc = vg - mu
    rstd = lax.rsqrt(_lanemean(xc * xc) + EPS)
    return xc * rstd, rstd


def _spatial_mix(wt_ref, vn_ref, bsx_ref, mixed_ref, tm):
    for c in range(tm // CHUNK):
        rows = slice(c * CHUNK, (c + 1) * CHUNK)
        for h in range(N_HEADS):
            cols = slice(h * CHUNK, (h + 1) * CHUNK)
            mixed_ref[rows, cols] = _dot(wt_ref[h], vn_ref[rows, cols]) + bsx_ref[:, cols]


def _conv_taps(xb, halo8):
    return [_shift_down(xb, halo8, CONV_W - 1 - k) for k in range(CONV_W)]


def _lru_gates(xc_bf_ref, wa_ref, wx_ref, ba_ref, bx_ref, r_ref, i_ref):
    for h in range(N_HEADS):
        cols = slice(h * CHUNK, (h + 1) * CHUNK)
        xh = xc_bf_ref[:, cols]
        r_ref[:, cols] = jax.nn.sigmoid(_dot(xh, wa_ref[h]) + ba_ref[:, cols])
        i_ref[:, cols] = jax.nn.sigmoid(_dot(xh, wx_ref[h]) + bx_ref[:, cols])


def _softplus_neg(lam):
    return jnp.maximum(-lam, 0.0) + jnp.log(1.0 + jnp.exp(-jnp.abs(lam)))


def _decay_parts(r, lam):
    la = (-LRU_C * _softplus_neg(lam)) * r
    a = jnp.exp(la)
    th = -jnp.tanh(la)
    mult = jnp.sqrt(2.0 * th / (1.0 + th))
    return a, mult


def _z_group(zref, g, rows=slice(None)):
    lo = g * D_HALF
    blk, off = lo // W_IN_COLS, lo % W_IN_COLS
    if off + D_HALF <= W_IN_COLS:
        return zref[blk, rows, off:off + D_HALF]
    return jnp.concatenate([zref[blk, rows, off:W_IN_COLS], zref[blk + 1, rows, 0:off + D_HALF - W_IN_COLS]], axis=1)


def _inproj_local(x, pre_g, w_own, tm, token):
    t = x.shape[0]

    def body(x_ref, g_ref, w_ref, token_ref, hn_ref, zl_ref, hnt_ref, wbf_s):
        @pl.when(pl.program_id(0) == 0)
        def _():
            wbf_s[...] = w_ref[...].astype(BF16)

        xv = x_ref[...]
        hnf = xv * lax.rsqrt(_lanemean(xv * xv) + EPS) * g_ref[...]
        hn = hnf.astype(BF16)
        hn_ref[...] = hn
        hnt_ref[...] = hnf.T.astype(BF16)
        zl_ref[...] = _dot(hn, wbf_s[...]).astype(BF16)

    row = lambda n: pl.BlockSpec((tm, n), lambda i: (i, 0))
    const = lambda shp: pl.BlockSpec(shp, lambda i: (0, 0), pipeline_mode=pl.Buffered(1))
    return pl.pallas_call(
        body, name="inproj_local", grid=(t // tm,),
        in_specs=[row(D_MODEL), const((1, D_MODEL)), const((D_MODEL, W_IN_COLS)), const((SUBLANES, LANES))],
        out_specs=[row(D_MODEL), row(W_IN_COLS), pl.BlockSpec((D_MODEL, tm), lambda i: (0, i))],
        out_shape=[jax.ShapeDtypeStruct((t, D_MODEL), BF16), jax.ShapeDtypeStruct((t, W_IN_COLS), BF16),
                   jax.ShapeDtypeStruct((D_MODEL, t), BF16)],
        scratch_shapes=[pltpu.VMEM((D_MODEL, W_IN_COLS), BF16)],
        compiler_params=_params("arbitrary"),
    )(x, pre_g, w_own, token)


def _inproj_branches_fwd(hn, z_own, wg_in, kc, prm, tm, token):
    t = hn.shape[0]
    nt = t // tm
    hb = tm // SUBLANES

    def body(kc_ref, hn_ref, zo_ref, w1_ref, w2_ref, w3_ref,
             lng_ref, lnb_ref, wt_ref, bsx_ref, cw_ref, cb_ref, wa_ref, wx_ref, ba_ref, bx_ref, lam_ref,
             oga_ref, ogb_ref, token_ref,
             z_ref, y_ref, h_ref,
             zbuf0, zbuf1, vn_s, mixed_s, xcbf_s, r_s, i_s, ug_s, halo_s, carry_s):
        s = pl.program_id(0)
        me = kc_ref[0]
        w_refs = (None, w1_ref, w2_ref, w3_ref)

        @pl.when(s == 0)
        def _():
            zbuf1[...] = jnp.zeros_like(zbuf1)

        @pl.when(s <= 1)
        def _():
            carry_s[...] = jnp.zeros_like(carry_s)
            halo_s[...] = jnp.zeros_like(halo_s)

        def step(zw, zr):
            def project(r):
                blk = (me + r) % N_CHIPS
                zb = zo_ref[...] if r == 0 else _dot(hn_ref[...], w_refs[r][...]).astype(BF16)
                z_ref[blk] = zb
                zw[blk] = zb

            zin = lambda g: _z_group(zr, g).astype(F32)
            always = [s >= 0] * 4

            @pl.when(always[0])
            def _():
                project(0)
                ug, _ = _gelu(zin(0))
                ug_s[...] = ug
                vg, _ = _gelu(zin(1))
                vhat, _ = _layernorm_parts(vg)
                vn_s[...] = (vhat * lng_ref[...] + lnb_ref[...]).astype(BF16)

            @pl.when(always[1])
            def _():
                project(1)
                _spatial_mix(wt_ref, vn_s, bsx_ref, mixed_s, tm)
                ga = zin(2)
                ya = ug_s[...] * mixed_s[...] * (ga * jax.nn.sigmoid(ga))
                ra = lax.rsqrt(_lanemean(ya * ya) + EPS)
                y_ref[:, 0:D_HALF] = (ya * ra * oga_ref[...]).astype(BF16)

            @pl.when(always[2])
            def _():
                project(2)
                xb = zin(3)
                taps = _conv_taps(xb, halo_s[...])
                halo_s[...] = xb[tm - SUBLANES:]
                xc = cb_ref[...] + taps[0] * cw_ref[0:1, :]
                for k in range(1, CONV_W):
                    xc = xc + taps[k] * cw_ref[k:k + 1, :]
                xcbf_s[...] = xc.astype(BF16)
                _lru_gates(xcbf_s, wa_ref, wx_ref, ba_ref, bx_ref, r_s, i_s)
                a, mult = _decay_parts(r_s[...], lam_ref[...])
                row = lax.broadcasted_iota(jnp.int32, a.shape, 0)
                mult = jnp.where(jnp.logical_and(s == 1, row == 0), 1.0, mult)
                r_s[...] = a
                i_s[...] = mult * (i_s[...] * xc)

            @pl.when(always[3])
            def _():
                project(3)
                a = r_s[...]
                b = i_s[...]
                r8 = lax.broadcasted_iota(jnp.int32, a.shape, 0) & (SUBLANES - 1)
                for d in (1, 2, 4):
                    a_sh = pltpu.roll(a, d, 0)
                    b_sh = pltpu.roll(b, d, 0)
                    m = r8 >= d
                    b = jnp.where(m, a * b_sh + b, b)
                    a = jnp.where(m, a * a_sh, a)
                carry = carry_s[...]
                for g in range(hb):
                    rows = slice(g * SUBLANES, (g + 1) * SUBLANES)
                    hg = a[rows] * carry + b[rows]
                    h_ref[rows, :] = hg
                    carry = jnp.broadcast_to(hg[SUBLANES - 1:SUBLANES, :], hg.shape)
                carry_s[...] = carry
                gb = zin(4)
                yb = h_ref[...] * (gb * jax.nn.sigmoid(gb))
                rb = lax.rsqrt(_lanemean(yb * yb) + EPS)
                y_ref[:, D_HALF:] = (yb * rb * ogb_ref[...]).astype(BF16)

        @pl.when(s % 2 == 0)
        def _():
            step(zbuf0, zbuf1)

        @pl.when(s % 2 == 1)
        def _():
            step(zbuf1, zbuf0)

    const = lambda a: pl.BlockSpec(a.shape, lambda s, kc, n=a.ndim: (0,) * n, pipeline_mode=pl.Buffered(1))
    proj = lambda n: pl.BlockSpec((tm, n), lambda s, kc: (jnp.minimum(s, nt - 1), 0))
    head = lambda n: pl.BlockSpec((tm, n), lambda s, kc: (jnp.maximum(s - 1, 0), 0))
    other = lambda r: pl.BlockSpec((None, D_MODEL, W_IN_COLS), lambda s, kc, r=r: ((kc[0] + r) % N_CHIPS, 0, 0),
                                   pipeline_mode=pl.Buffered(1))
    names = ("ln_g", "ln_b", "wt", "bsx", "conv_w", "conv_b", "w_a", "w_x", "b_a", "b_x", "lam", "oga", "ogb")
    pr = [prm[n] for n in names] + [token]
    big = lambda dt: pltpu.VMEM((tm, D_HALF), dt)
    zblocks = pltpu.VMEM((N_CHIPS, tm, W_IN_COLS), BF16)
    grid_spec = pltpu.PrefetchScalarGridSpec(
        num_scalar_prefetch=1, grid=(nt + 1,),
        in_specs=[proj(D_MODEL), proj(W_IN_COLS), other(1), other(2), other(3)] + [const(a) for a in pr],
        out_specs=[pl.BlockSpec((N_CHIPS, tm, W_IN_COLS), lambda s, kc: (0, jnp.minimum(s, nt - 1), 0)),
                   head(D_MODEL), head(D_HALF)],
        scratch_shapes=[zblocks, zblocks, big(BF16), big(F32), big(BF16), big(F32), big(F32), big(F32),
                        pltpu.VMEM((SUBLANES, D_HALF), F32), pltpu.VMEM((SUBLANES, D_HALF), F32)])
    return pl.pallas_call(
        body, name="inproj_branches_fwd", grid_spec=grid_spec,
        out_shape=[jax.ShapeDtypeStruct((N_CHIPS, t, W_IN_COLS), BF16), jax.ShapeDtypeStruct((t, D_MODEL), BF16),
                   jax.ShapeDtypeStruct((t, D_HALF), F32)],
        compiler_params=_params("arbitrary"),
    )(kc, hn, z_own, wg_in, wg_in, wg_in, *pr)


def _outproj_fwd(x, y, p, tgt, post_g, w_out, w_pg, wg_pe, tm):
    t = x.shape[0]

    def body(x_ref, y_ref, p_ref, tgt_ref, pg_ref, wo_ref, wpg_ref, wpe_ref,
             o_ref, h1_ref, gt_ref, dout_ref, loss_ref):
        @pl.when(pl.program_id(0) == 0)
        def _():
            loss_ref[...] = jnp.zeros_like(loss_ref)

        o = _dot(y_ref[...], wo_ref[...])
        o_ref[...] = o
        r3 = lax.rsqrt(_lanemean(o * o) + EPS)
        h1 = x_ref[...] + (o * r3) * pg_ref[...]
        h1b = h1.astype(BF16)
        h1_ref[...] = h1b
        gt = jax.nn.sigmoid(_dot(h1b, wpg_ref[...]))
        gt_ref[...] = gt
        pb = p_ref[...].astype(BF16)
        for k in range(N_CHIPS):
            cols = slice(k * W_PE_COLS, (k + 1) * W_PE_COLS)
            pe = _dot(pb, wpe_ref[k])
            d = h1[:, cols] + pe * gt[:, cols] - tgt_ref[:, cols]
            dout_ref[:, cols] = d * (1.0 / D_MODEL)
            loss_ref[...] += jnp.sum(d * d) * (0.5 / D_MODEL)

    row = lambda n: pl.BlockSpec((tm, n), lambda i: (i, 0))
    const = lambda shp: pl.BlockSpec(shp, lambda i, n=len(shp): (0,) * n, pipeline_mode=pl.Buffered(1))
    return pl.pallas_call(
        body, name="outproj_fwd", grid=(t // tm,),
        in_specs=[row(D_MODEL), row(D_MODEL), row(D_PLE), row(D_MODEL), const((1, D_MODEL)),
                  const((D_MODEL, D_MODEL)), const((D_MODEL, D_MODEL)), const((N_CHIPS, D_PLE, W_PE_COLS))],
        out_specs=[row(D_MODEL), row(D_MODEL), row(D_MODEL), row(D_MODEL),
                   pl.BlockSpec((SUBLANES, LANES), lambda i: (0, 0))],
        out_shape=[jax.ShapeDtypeStruct((t, D_MODEL), F32), jax.ShapeDtypeStruct((t, D_MODEL), BF16),
                   jax.ShapeDtypeStruct((t, D_MODEL), F32), jax.ShapeDtypeStruct((t, D_MODEL), F32),
                   jax.ShapeDtypeStruct((SUBLANES, LANES), F32)],
        compiler_params=_params("arbitrary"),
    )(x, y, p, tgt, post_g, w_out, w_pg, wg_pe)


def _head_bwd(dout, gt, p, o, post_g, w_out, w_pg, wg_pe, tm):
    t = dout.shape[0]

    def body(dout_ref, gt_ref, p_ref, o_ref, pg_ref, wo_ref, wpg_ref, wpe_ref,
             gwpe_ref, dq_ref, dh1_ref, do_ref, dy_ref, gpost_ref):
        i = pl.program_id(0)

        @pl.when(i == 0)
        def _():
            gpost_ref[...] = jnp.zeros_like(gpost_ref)
            gwpe_ref[...] = jnp.zeros_like(gwpe_ref)

        dout = dout_ref[...]
        gt = gt_ref[...]
        pb = p_ref[...].astype(BF16)
        for k in range(N_CHIPS):
            cols = slice(k * W_PE_COLS, (k + 1) * W_PE_COLS)
            pe = _dot(pb, wpe_ref[k])
            g = gt[:, cols]
            dg = dout[:, cols] * g
            gwpe_ref[k] += _dot_tn(pb, dg.astype(BF16))
            dq_ref[:, cols] = (dg * pe * (1.0 - g)).astype(BF16)
        dh1 = dout + _dot_nt(dq_ref[...], wpg_ref[...])
        dh1_ref[...] = dh1
        o = o_ref[...]
        r3 = lax.rsqrt(_lanemean(o * o) + EPS)
        on = o * r3
        gpost_ref[...] += _rowsum8(dh1 * on)
        don = dh1 * pg_ref[...]
        do = r3 * (don - on * _lanemean(don * on))
        dob = do.astype(BF16)
        do_ref[...] = dob
        dy_ref[...] = _dot_nt(dob, wo_ref[...])

        @pl.when(i == pl.num_programs(0) - 1)
        def _():
            gpost_ref[...] = jnp.broadcast_to(jnp.sum(gpost_ref[...], axis=0, keepdims=True), gpost_ref.shape)

    row = lambda n: pl.BlockSpec((tm, n), lambda i: (i, 0))
    const = lambda shp: pl.BlockSpec(shp, lambda i, n=len(shp): (0,) * n, pipeline_mode=pl.Buffered(1))
    return pl.pallas_call(
        body, name="head_bwd", grid=(t // tm,),
        in_specs=[row(D_MODEL), row(D_MODEL), row(D_PLE), row(D_MODEL), const((1, D_MODEL)),
                  const((D_MODEL, D_MODEL)), const((D_MODEL, D_MODEL)), const((N_CHIPS, D_PLE, W_PE_COLS))],
        out_specs=[pl.BlockSpec((N_CHIPS, D_PLE, W_PE_COLS), lambda i: (0, 0, 0)),
                   row(D_MODEL), row(D_MODEL), row(D_MODEL), row(D_MODEL),
                   pl.BlockSpec((SUBLANES, D_MODEL), lambda i: (0, 0))],
        out_shape=[jax.ShapeDtypeStruct((N_CHIPS, D_PLE, W_PE_COLS), F32), jax.ShapeDtypeStruct((t, D_MODEL), BF16),
                   jax.ShapeDtypeStruct((t, D_MODEL), F32), jax.ShapeDtypeStruct((t, D_MODEL), BF16),
                   jax.ShapeDtypeStruct((t, D_MODEL), F32), jax.ShapeDtypeStruct((SUBLANES, D_MODEL), F32)],
        compiler_params=_params("arbitrary"),
    )(dout, gt, p, o, post_g, w_out, w_pg, wg_pe)


def _branches_bwd(z, h, dy, prm, tm, token):
    t = h.shape[0]
    nt = t // tm
    hb = tm // SUBLANES

    def body(z_ref, zh_ref, h_ref, hh_ref, dy_ref,
             lng_ref, lnb_ref, wt_ref, wtt_ref, bsx_ref, cw_ref, cb_ref, wa_ref, wx_ref, ba_ref, bx_ref, lam_ref,
             oga_ref, ogb_ref, token_ref,
             dz_ref, g_oga, g_ogb, g_lng, g_lnb, g_bsx, g_ws, g_cw, g_cb, g_wa, g_ba, g_wx, g_bx, g_lam,
             vn_s, mixed_s, dm_s, dvn_s, xcbf_s, r_s, i_s, a_s, b_s, dh_s, dpr_s, dpi_s, dxc_s,
             ca_s, cd_s, cx_s):
        step_i = pl.program_id(0)
        tile = nt - 1 - step_i
        accs = (g_oga, g_ogb, g_lng, g_lnb, g_bsx, g_ws, g_cw, g_cb, g_wa, g_ba, g_wx, g_bx, g_lam)

        @pl.when(step_i == 0)
        def _():
            for r in accs + (ca_s, cd_s, cx_s):
                r[...] = jnp.zeros_like(r)

        dy_a = dy_ref[:, 0:D_HALF]
        dy_b = dy_ref[:, D_HALF:]

        u = _z_group(z_ref, 0).astype(F32)
        ug, tu = _gelu(u)
        v = _z_group(z_ref, 1).astype(F32)
        vg, tv = _gelu(v)
        vhat, rstd = _layernorm_parts(vg)
        vn_s[...] = (vhat * lng_ref[...] + lnb_ref[...]).astype(BF16)
        _spatial_mix(wt_ref, vn_s, bsx_ref, mixed_s, tm)
        mixed = mixed_s[...]
        ga = _z_group(z_ref, 2).astype(F32)
        sga = jax.nn.sigmoid(ga)
        sa = ga * sga
        um = ug * mixed
        ya = um * sa
        ra = lax.rsqrt(_lanemean(ya * ya) + EPS)
        yahat = ya * ra
        g_oga[...] += _rowsum8(dy_a * yahat)
        dn = dy_a * oga_ref[...]
        dya = ra * (dn - yahat * _lanemean(dn * yahat))
        dz_ref[:, 2 * D_HALF:3 * D_HALF] = (dya * um * (sga * (1.0 + ga * (1.0 - sga)))).astype(BF16)
        dz_ref[:, 0:D_HALF] = (dya * mixed * sa * _gelu_grad(u, tu)).astype(BF16)
        dmixed = dya * ug * sa
        g_bsx[...] += jnp.sum(dmixed.reshape(tm // CHUNK, CHUNK, D_HALF), axis=0)
        dm_s[...] = dmixed.astype(BF16)
        for c in range(tm // CHUNK):
            rows = slice(c * CHUNK, (c + 1) * CHUNK)
            for hd in range(N_HEADS):
                cols = slice(hd * CHUNK, (hd + 1) * CHUNK)
                dmh = dm_s[rows, cols]
                dvn_s[rows, cols] = _dot(wtt_ref[hd], dmh)
                g_ws[hd] += _dot_nt(dmh, vn_s[rows, cols])
        dvn = dvn_s[...]
        g_lng[...] += _rowsum8(dvn * vhat)
        g_lnb[...] += _rowsum8(dvn)
        dvh = dvn * lng_ref[...]
        dvg = rstd * (dvh - _lanemean(dvh) - vhat * _lanemean(dvh * vhat))
        dz_ref[:, D_HALF:2 * D_HALF] = (dvg * _gelu_grad(v, tv)).astype(BF16)

        xb = _z_group(z_ref, 3).astype(F32)
        halo = jnp.where(tile == 0, 0.0, _z_group(zh_ref, 3).astype(F32)[SUBLANES:])
        taps = _conv_taps(xb, halo)
        xc = cb_ref[...] + taps[0] * cw_ref[0:1, :]
        for k in range(1, CONV_W):
            xc = xc + taps[k] * cw_ref[k:k + 1, :]
        xcbf_s[...] = xc.astype(BF16)
        _lru_gates(xcbf_s, wa_ref, wx_ref, ba_ref, bx_ref, r_s, i_s)
        rg = r_s[...]
        ig = i_s[...]
        lam = lam_ref[...]
        a, mult_true = _decay_parts(rg, lam)
        row = lax.broadcasted_iota(jnp.int32, a.shape, 0)
        first = jnp.logical_and(tile == 0, row == 0)
        mult = jnp.where(first, 1.0, mult_true)
        hcur = h_ref[...]
        hprev = _shift_down(hcur, jnp.where(tile == 0, 0.0, hh_ref[...]), 1)
        gb = _z_group(z_ref, 4).astype(F32)
        sgb = jax.nn.sigmoid(gb)
        sb = gb * sgb
        yb = hcur * sb
        rb = lax.rsqrt(_lanemean(yb * yb) + EPS)
        ybhat = yb * rb
        g_ogb[...] += _rowsum8(dy_b * ybhat)
        dn = dy_b * ogb_ref[...]
        dyb = rb * (dn - ybhat * _lanemean(dn * ybhat))
        dz_ref[:, 4 * D_HALF:5 * D_HALF] = (dyb * hcur * (sgb * (1.0 + gb * (1.0 - sgb)))).astype(BF16)

        an = _shift_up(a, ca_s[...], 1)
        bb = dyb * sb
        r8 = row & (SUBLANES - 1)
        for d in (1, 2, 4):
            a_sh = pltpu.roll(an, tm - d, 0)
            b_sh = pltpu.roll(bb, tm - d, 0)
            m = r8 + d < SUBLANES
            bb = jnp.where(m, an * b_sh + bb, bb)
            an = jnp.where(m, an * a_sh, an)
        a_s[...] = an
        b_s[...] = bb

        def step(g, carry):
            sl = pl.ds(pl.multiple_of((hb - 1 - g) * SUBLANES, SUBLANES), SUBLANES)
            dg = a_s[sl, :] * carry + b_s[sl, :]
            dh_s[sl, :] = dg
            return jnp.broadcast_to(dg[0:1, :], dg.shape)

        cd_s[...] = lax.fori_loop(0, hb, step, cd_s[...])
        ca_s[...] = jnp.broadcast_to(a[0:1, :], ca_s.shape)
        dh = dh_s[...]
        da = dh * hprev
        gx = ig * xc
        dla = da * a - jnp.where(first, 0.0, dh * gx * (a * a / mult_true))
        g_lam[...] += _rowsum8(dla * rg)
        dr = dla * (-LRU_C * _softplus_neg(lam))
        dpr = dr * rg * (1.0 - rg)
        dpi = (dh * mult * xc) * ig * (1.0 - ig)
        g_ba[...] += _rowsum8(dpr)
        g_bx[...] += _rowsum8(dpi)
        dpr_s[...] = dpr.astype(BF16)
        dpi_s[...] = dpi.astype(BF16)
        for hd in range(N_HEADS):
            cols = slice(hd * CHUNK, (hd + 1) * CHUNK)
            xh = xcbf_s[:, cols]
            dprh = dpr_s[:, cols]
            dpih = dpi_s[:, cols]
            g_wa[hd] += _dot_tn(xh, dprh)
            g_wx[hd] += _dot_tn(xh, dpih)
            dxc_s[:, cols] = _dot_nt(dprh, wa_ref[hd]) + _dot_nt(dpih, wx_ref[hd])
        dxc = dxc_s[...] + dh * mult * ig
        g_cb[...] += _rowsum8(dxc)
        for k in range(CONV_W):
            g_cw[k * SUBLANES:(k + 1) * SUBLANES, :] += _rowsum8(dxc * taps[k])
        nxt = cx_s[...]
        dxb = dxc * cw_ref[CONV_W - 1:CONV_W, :]
        for j in range(1, CONV_W):
            dxb = dxb + _shift_up(dxc, nxt, j) * cw_ref[CONV_W - 1 - j:CONV_W - j, :]
        dz_ref[:, 3 * D_HALF:4 * D_HALF] = dxb.astype(BF16)
        cx_s[...] = dxc[0:SUBLANES]

        @pl.when(step_i == nt - 1)
        def _():
            for r in (g_oga, g_ogb, g_lng, g_lnb, g_cb, g_ba, g_bx):
                r[...] = jnp.broadcast_to(jnp.sum(r[...], axis=0, keepdims=True), r.shape)
            lam_f = LRU_C * jax.nn.sigmoid(-lam_ref[...])
            g_lam[...] = jnp.broadcast_to(jnp.sum(g_lam[...], axis=0, keepdims=True) * lam_f, g_lam.shape)
            for k in range(CONV_W):
                blk = g_cw[k * SUBLANES:(k + 1) * SUBLANES, :]
                g_cw[k * SUBLANES:(k + 1) * SUBLANES, :] = jnp.broadcast_to(jnp.sum(blk, axis=0, keepdims=True), blk.shape)
            tri = (lax.broadcasted_iota(jnp.int32, (CHUNK, CHUNK), 0) >= lax.broadcasted_iota(jnp.int32, (CHUNK, CHUNK), 1))
            for hd in range(N_HEADS):
                cols = slice(hd * CHUNK, (hd + 1) * CHUNK)
                g_ws[hd] = jnp.where(tri, g_ws[hd], 0.0)
                blk = g_bsx[:, cols]
                g_bsx[:, cols] = jnp.broadcast_to(jnp.sum(blk, axis=1, keepdims=True), blk.shape)

    rev = lambda i: nt - 1 - i
    zspec = pl.BlockSpec((N_CHIPS, tm, W_IN_COLS), lambda i: (0, rev(i), 0))
    halo = lambda col: pl.BlockSpec((SUBLANES, D_HALF), lambda i: (jnp.maximum(rev(i) * hb - 1, 0), col))
    zhalo = pl.BlockSpec((N_CHIPS, 2 * SUBLANES, W_IN_COLS), lambda i: (0, jnp.maximum(rev(i) * (hb // 2) - 1, 0), 0))
    full = lambda a: pl.BlockSpec(a.shape, lambda i, n=a.ndim: (0,) * n)
    acc = lambda shp: pl.BlockSpec(shp, lambda i, n=len(shp): (0,) * n)
    names = ("ln_g", "ln_b", "wt", "wtt", "bsx", "conv_w", "conv_b", "w_a", "w_x", "b_a", "b_x", "lam", "oga", "ogb")
    pr = [prm[n] for n in names] + [token]
    vec = (SUBLANES, D_HALF)
    mat = (N_HEADS, CHUNK, CHUNK)
    acc_shapes = [vec, vec, vec, vec, (CHUNK, D_HALF), mat, (CONV_W * SUBLANES, D_HALF), vec, mat, vec, mat, vec, vec]
    big = lambda dt: pltpu.VMEM((tm, D_HALF), dt)
    return pl.pallas_call(
        body, name="branches_bwd", grid=(nt,),
        in_specs=[zspec, zhalo,
                  pl.BlockSpec((tm, D_HALF), lambda i: (rev(i), 0)), halo(0),
                  pl.BlockSpec((tm, D_MODEL), lambda i: (rev(i), 0))] + [full(a) for a in pr],
        out_specs=[pl.BlockSpec((tm, D_Z), lambda i: (rev(i), 0))] + [acc(s) for s in acc_shapes],
        out_shape=[jax.ShapeDtypeStruct((t, D_Z), BF16)] + [jax.ShapeDtypeStruct(s, F32) for s in acc_shapes],
        scratch_shapes=[big(BF16), big(F32), big(BF16), big(F32), big(BF16), big(F32), big(F32), big(F32), big(F32),
                        big(F32), big(BF16), big(BF16), big(F32),
                        pltpu.VMEM(vec, F32), pltpu.VMEM(vec, F32), pltpu.VMEM(vec, F32)],
        compiler_params=_params("arbitrary"),
    )(z, z, h, h, dy, *pr)


def _inproj_bwd(dz, wg_in, x, dh1, pre_g, tm, tile0, nt, prev, last, token, name):
    t = x.shape[0]

    def body(*refs):
        dz_ref, w_ref, x_ref, dh1_ref, g_ref = refs[:5]
        gx_ref, gpre_ref, acc_s = refs[-3:]
        i = pl.program_id(0)

        @pl.when(i == 0)
        def _():
            gpre_ref[...] = jnp.zeros_like(gpre_ref) if prev is None else refs[7][...]

        acc = _dot_nt(dz_ref[:, 0:W_IN_COLS], w_ref[0])
        for k in range(1, N_CHIPS):
            acc = acc + _dot_nt(dz_ref[:, k * W_IN_COLS:(k + 1) * W_IN_COLS], w_ref[k])
        acc_s[...] = acc
        for s in range(tm // CHUNK):
            rows = slice(s * CHUNK, (s + 1) * CHUNK)
            xv = x_ref[rows, :]
            r = lax.rsqrt(_lanemean(xv * xv) + EPS)
            xhat = xv * r
            dhn = acc_s[rows, :]
            gpre_ref[...] += _rowsum8(dhn * xhat)
            dxh = dhn * g_ref[...]
            gx_ref[rows, :] = dh1_ref[rows, :] + r * (dxh - xhat * _lanemean(dxh * xhat))

        if last:
            @pl.when(i == nt - 1)
            def _():
                gpre_ref[...] = jnp.broadcast_to(jnp.sum(gpre_ref[...], axis=0, keepdims=True), gpre_ref.shape)

    row = lambda n: pl.BlockSpec((tm, n), lambda i: (tile0 + i, 0))
    small = lambda r: pl.BlockSpec((r, D_MODEL), lambda i: (0, 0))
    tok = pl.BlockSpec((SUBLANES, LANES), lambda i: (0, 0))
    in_specs = [row(D_Z), pl.BlockSpec(wg_in.shape, lambda i: (0, 0, 0), pipeline_mode=pl.Buffered(1)),
                row(D_MODEL), row(D_MODEL), small(1), tok]
    args = [dz, wg_in, x, dh1, pre_g, token]
    aliases = {}
    if prev is not None:
        in_specs += [ANY, small(SUBLANES)]
        args += list(prev)
        aliases = {6: 0}
    return pl.pallas_call(
        body, name=name, grid=(nt,), in_specs=in_specs, out_specs=[row(D_MODEL), small(SUBLANES)],
        out_shape=[jax.ShapeDtypeStruct((t, D_MODEL), F32), jax.ShapeDtypeStruct((SUBLANES, D_MODEL), F32)],
        input_output_aliases=aliases,
        scratch_shapes=[pltpu.VMEM((tm, D_MODEL), F32)],
        compiler_params=_params("arbitrary"),
    )(*args)


def _weight_grad(a, b, name, kb, nb, tk, tn, tt, token, a_transposed=False):
    t = b.shape[0]
    tt = min(tt, t)

    def body(a_ref, b_ref, token_ref, o_ref):
        @pl.when(pl.program_id(2) == 0)
        def _():
            o_ref[...] = jnp.zeros_like(o_ref)

        o_ref[...] += (_dot if a_transposed else _dot_tn)(a_ref[...], b_ref[...])

    a_spec = (pl.BlockSpec((tk, tt), lambda j, i, s: (i, s)) if a_transposed
              else pl.BlockSpec((tt, tk), lambda j, i, s: (s, i)))
    return pl.pallas_call(
        body, name=name, grid=(nb, kb, t // tt),
        in_specs=[a_spec, pl.BlockSpec((tt, tn), lambda j, i, s: (s, j)),
                  pl.BlockSpec((SUBLANES, LANES), lambda j, i, s: (0, 0))],
        out_specs=pl.BlockSpec((None, None, tk, tn), lambda j, i, s: (j, i, 0, 0)),
        out_shape=jax.ShapeDtypeStruct((nb, kb, tk, tn), F32),
        compiler_params=_params("parallel", "parallel", "arbitrary"),
    )(a, b, token)


def _place():
    x, y, c = lax.axis_index("x"), lax.axis_index("y"), lax.axis_index("c")
    return x, y, c


def _chip_of(x, y):
    return 2 * x + y


HBM = pl.BlockSpec(memory_space=pltpu.HBM)
SEM = pl.BlockSpec(memory_space=pltpu.SEMAPHORE)
EFFECT = pltpu.SideEffectType.DATAFLOW_SIDE_EFFECTING


def _hbm(a):
    return pltpu.with_memory_space_constraint(a, pltpu.HBM)


def _landing(shape, dtype):
    return _hbm(lax.empty(shape, dtype))


def _exchange_start(name, arrays, ncopies, build, after=None):
    n = len(arrays)
    extra = [] if after is None else [after]

    def body(*refs):
        ins, token = refs[:n], refs[-1]
        send_sems, recv_sems = refs[n + len(extra)], refs[n + len(extra) + 1]
        for cp in build(ins, send_sems, recv_sems):
            cp.start()
        token[...] = jnp.zeros_like(token)

    outs = pl.pallas_call(
        body, name=name,
        out_shape=(pltpu.SemaphoreType.DMA((ncopies,)), pltpu.SemaphoreType.DMA((ncopies,)),
                   *[pltpu.HBM(a.shape, a.dtype) for a in arrays], jax.ShapeDtypeStruct((SUBLANES, LANES), F32)),
        in_specs=[HBM] * n + [ANY] * len(extra),
        out_specs=(SEM, SEM, *[HBM] * n, pl.BlockSpec(memory_space=pltpu.VMEM)),
        input_output_aliases={q: q + 2 for q in range(n)},
        compiler_params=pltpu.CompilerParams(has_side_effects=EFFECT),
    )(*[_hbm(a) for a in arrays], *extra)
    return (outs[0], outs[1], list(outs[2:2 + n])), outs[-1]


def _exchange_wait(name, started, after, build):
    send, recv, arrays = started
    n = len(arrays)

    def body(*refs):
        ins, send_sems, recv_sems = refs[:n], refs[n], refs[n + 1]
        for cp in build(ins, send_sems, recv_sems):
            cp.wait_send()
            cp.wait_recv()

    return pl.pallas_call(
        body, name=name, out_shape=tuple(pltpu.HBM(a.shape, a.dtype) for a in arrays),
        in_specs=[HBM] * n + [SEM, SEM, ANY], out_specs=tuple([HBM] * n),
        input_output_aliases={q: q for q in range(n)},
        compiler_params=pltpu.CompilerParams(has_side_effects=EFFECT),
    )(*arrays, send, recv, after)


def _exchange_wait_start(name, started, after, build_wait, ncopies, build_start):
    send, recv, arrays = started
    n = len(arrays)

    def body(*refs):
        ins, send_sems, recv_sems = refs[:n], refs[n], refs[n + 1]
        send2, recv2, token = refs[n + 3], refs[n + 4], refs[-1]
        arrived = build_wait(ins, send_sems, recv_sems)
        for cp, onward in zip(arrived, build_start(ins, send2, recv2)):
            cp.wait_recv()
            onward.start()
        for cp in arrived:
            cp.wait_send()
        token[...] = jnp.zeros_like(token)

    outs = pl.pallas_call(
        body, name=name,
        out_shape=(pltpu.SemaphoreType.DMA((ncopies,)), pltpu.SemaphoreType.DMA((ncopies,)),
                   *[pltpu.HBM(a.shape, a.dtype) for a in arrays], jax.ShapeDtypeStruct((SUBLANES, LANES), F32)),
        in_specs=[HBM] * n + [SEM, SEM, ANY], out_specs=(SEM, SEM, *[HBM] * n, pl.BlockSpec(memory_space=pltpu.VMEM)),
        input_output_aliases={q: q + 2 for q in range(n)},
        compiler_params=pltpu.CompilerParams(has_side_effects=EFFECT),
    )(*arrays, send, recv, after)
    return (outs[0], outs[1], list(outs[2:2 + n])), outs[-1]


def _cast_into_slot(w, kc, name, dtype=BF16, token=None):
    rows, cols = w.shape
    tr = min(rows, 256)
    extra = [] if token is None else [token]

    def body(kc_ref, w_ref, *rest):
        rest[-1][...] = w_ref[...].astype(dtype)

    grid_spec = pltpu.PrefetchScalarGridSpec(
        num_scalar_prefetch=1, grid=(rows // tr,),
        in_specs=[pl.BlockSpec((tr, cols), lambda r, kc: (r, 0))]
                 + [pl.BlockSpec((SUBLANES, LANES), lambda r, kc: (0, 0))] * len(extra),
        out_specs=pl.BlockSpec((None, tr, cols), lambda r, kc: (kc[0], r, 0)))
    return pl.pallas_call(
        body, name=name, grid_spec=grid_spec, out_shape=jax.ShapeDtypeStruct((N_CHIPS, rows, cols), dtype),
        compiler_params=_params("arbitrary"),
    )(kc, w, *extra)


def _gather_ici_copies(n):
    def build(refs, send_sems, recv_sems):
        x, y, c = _place()
        mine = lambda b: refs[b].at[_chip_of(x, y), c]
        chips = [(1 - x, y), (x, 1 - y), (1 - x, 1 - y)]
        return [pltpu.make_async_remote_copy(
            src_ref=mine(b), dst_ref=mine(b), send_sem=send_sems.at[3 * b + j], recv_sem=recv_sems.at[3 * b + j],
            device_id=(*chip, c), device_id_type=MESH) for b in range(n) for j, chip in enumerate(chips)]
    return build


def _gather_relay_copies(n):
    def build(refs, send_sems, recv_sems):
        x, y, c = _place()
        chips = [(1 - x, y), (x, 1 - y), (1 - x, 1 - y)]
        cps = []
        for b in range(n):
            for j, chip in enumerate(chips):
                got = refs[b].at[_chip_of(*chip), c]
                cps.append(pltpu.make_async_remote_copy(
                    src_ref=got, dst_ref=got, send_sem=send_sems.at[3 * b + j], recv_sem=recv_sems.at[3 * b + j],
                    device_id=(x, y, 1 - c), device_id_type=MESH))
        return cps
    return build


def _sibling_copies(n):
    def build(refs, send_sems, recv_sems):
        x, y, c = _place()
        return [pltpu.make_async_remote_copy(
            src_ref=refs[b].at[:, 1 - c], dst_ref=refs[n + b], send_sem=send_sems.at[b], recv_sem=recv_sems.at[b],
            device_id=(x, y, 1 - c), device_id_type=MESH) for b in range(n)]
    return build


def _chip_copies(n):
    def build(refs, send_sems, recv_sems):
        x, y, c = _place()
        chips = [(1 - x, y), (x, 1 - y), (1 - x, 1 - y)]
        return [pltpu.make_async_remote_copy(
            src_ref=refs[b].at[_chip_of(*chip)], dst_ref=refs[n + b].at[j],
            send_sem=send_sems.at[3 * b + j], recv_sem=recv_sems.at[3 * b + j],
            device_id=(*chip, c), device_id_type=MESH) for b in range(n) for j, chip in enumerate(chips)]
    return build


def _finish_copies(n, n_all):
    def build(refs, send_sems, recv_sems):
        x, y, c = _place()
        cps = [pltpu.make_async_remote_copy(
            src_ref=refs[b].at[c], dst_ref=refs[b].at[c], send_sem=send_sems.at[b], recv_sem=recv_sems.at[b],
            device_id=(x, y, 1 - c), device_id_type=MESH) for b in range(n)]
        flips = [(fx, fy, fc) for fx in (0, 1) for fy in (0, 1) for fc in (0, 1)][1:]
        for b in range(n_all):
            mine = refs[n + b].at[_chip_of(x, y), c]
            cps += [pltpu.make_async_remote_copy(
                src_ref=mine, dst_ref=mine, send_sem=send_sems.at[n + 7 * b + q], recv_sem=recv_sems.at[n + 7 * b + q],
                device_id=(x ^ fx, y ^ fy, c ^ fc), device_id_type=MESH) for q, (fx, fy, fc) in enumerate(flips)]
        return cps
    return build


def _pair_sum(g, r1, kc, name, tr, send_dtype):
    nk, _, rows, cols = g.shape

    def body(kc_ref, g_ref, r_ref, p_ref, own_ref):
        s = g_ref[...] + r_ref[...]
        p_ref[...] = s.astype(send_dtype)

        @pl.when(pl.program_id(1) == kc_ref[0])
        def _():
            own_ref[...] = s

    grid_spec = pltpu.PrefetchScalarGridSpec(
        num_scalar_prefetch=1, grid=(rows // tr, nk),
        in_specs=[pl.BlockSpec((None, None, tr, cols), lambda r, k, kc: (k, kc[1], r, 0)),
                  pl.BlockSpec((None, tr, cols), lambda r, k, kc: (k, r, 0))],
        out_specs=[pl.BlockSpec((None, tr, cols), lambda r, k, kc: (k, r, 0)),
                   pl.BlockSpec((tr, cols), lambda r, k, kc: (r, 0))])
    return pl.pallas_call(
        body, name=name, grid_spec=grid_spec,
        out_shape=[jax.ShapeDtypeStruct((nk, rows, cols), send_dtype), jax.ShapeDtypeStruct((rows, cols), F32)],
        compiler_params=_params("arbitrary", "arbitrary"),
    )(kc, g, r1)


def _chip_sum(own, r2, slot, lead, name, tr):
    rows, cols = own.shape
    nl = len(lead)

    def body(slot_ref, o_ref, r_ref, s_ref):
        s = o_ref[...]
        for j in range(3):
            s = s + r_ref[j].astype(F32)
        s_ref[...] = s

    grid_spec = pltpu.PrefetchScalarGridSpec(
        num_scalar_prefetch=1, grid=(rows // tr,),
        in_specs=[pl.BlockSpec((tr, cols), lambda r, sl: (r, 0)), pl.BlockSpec((3, tr, cols), lambda r, sl: (0, r, 0))],
        out_specs=pl.BlockSpec((None,) * nl + (tr, cols), lambda r, sl: tuple(sl[q] for q in range(nl)) + (r, 0)))
    return pl.pallas_call(
        body, name=name, grid_spec=grid_spec, out_shape=jax.ShapeDtypeStruct(tuple(lead) + (rows, cols), F32),
        compiler_params=_params("arbitrary"),
    )(slot, own, r2)


def _adam_update(w, g, m, v):
    nm = ADAM_B1 * m + (1.0 - ADAM_B1) * g
    nv = ADAM_B2 * v + (1.0 - ADAM_B2) * (g * g)
    m_hat = nm / (1.0 - ADAM_B1 ** ADAM_STEP)
    v_hat = nv / (1.0 - ADAM_B2 ** ADAM_STEP)
    return -ADAM_LR * (m_hat / (jnp.sqrt(v_hat) + ADAM_EPS) + ADAM_WD * w), nm, nv


def _adamw(w, g, m, v, name, tr, token):
    rows, cols = w.shape

    def body(w_ref, g_ref, m_ref, v_ref, token_ref, go_ref, d_ref, nm_ref, nv_ref):
        gv = g_ref[...]
        go_ref[...] = gv
        d_ref[...], nm_ref[...], nv_ref[...] = _adam_update(w_ref[...], gv, m_ref[...], v_ref[...])

    spec = pl.BlockSpec((tr, cols), lambda r: (r, 0))
    return pl.pallas_call(
        body, name=name, grid=(rows // tr,),
        in_specs=[spec] * 4 + [pl.BlockSpec((SUBLANES, LANES), lambda r: (0, 0))], out_specs=[spec] * 4,
        out_shape=[jax.ShapeDtypeStruct((rows, cols), F32)] * 4,
        compiler_params=_params("parallel"),
    )(w, g, m, v, token)


def _adamw_small(packed_g, pre_g_parts, ws, ms, vs):
    names = ["pre_g"] + [n for n, _ in SMALL_ROWS if n != "conv_w"]
    rows = dict(SMALL_ROWS)
    offset, at = {}, 0
    for n, r in SMALL_ROWS:
        offset[n] = at
        at += r
    k = len(names)

    def body(*refs):
        g_ref, pg_ref = refs[0], refs[1]
        w_refs, m_refs, v_refs = refs[2:2 + k], refs[2 + k:2 + 2 * k], refs[2 + 2 * k:2 + 3 * k]
        outs = refs[2 + 3 * k:]
        go, do, mo, vo = outs[:k], outs[k:2 * k], outs[2 * k:3 * k], outs[3 * k:4 * k]
        pre = pg_ref[0]
        for dev in range(1, 8):
            pre = pre + pg_ref[dev]
        outs[4 * k][...] = pre[D_MODEL // LANES:, :]
        for i, n in enumerate(names):
            shp = w_refs[i].shape
            if len(shp) == 2 and shp[0] == 1:
                for r in range(shp[1] // LANES):
                    cols = slice(r * LANES, (r + 1) * LANES)
                    g = pre[r:r + 1, :] if n == "pre_g" else g_ref[offset[n] + r:offset[n] + r + 1, :]
                    go[i][:, cols] = g
                    do[i][:, cols], mo[i][:, cols], vo[i][:, cols] = _adam_update(
                        w_refs[i][:, cols], g, m_refs[i][:, cols], v_refs[i][:, cols])
            else:
                g = g_ref[offset[n]:offset[n] + rows[n], :].reshape(shp)
                go[i][...] = g
                do[i][...], mo[i][...], vo[i][...] = _adam_update(w_refs[i][...], g, m_refs[i][...], v_refs[i][...])

    vm = pl.BlockSpec(memory_space=pltpu.VMEM)
    args = [packed_g, pre_g_parts] + [src[n] for src in (ws, ms, vs) for n in names]
    out_shape = [jax.ShapeDtypeStruct(ws[n].shape, F32) for _ in range(4) for n in names]
    out_shape.append(jax.ShapeDtypeStruct((SUBLANES, LANES), F32))
    outs = pl.pallas_call(
        body, name="adamw_small", in_specs=[vm] * len(args), out_specs=[vm] * (4 * k + 1), out_shape=out_shape,
    )(*args)
    return [dict(zip(names, outs[q * k:(q + 1) * k])) for q in range(4)], outs[4 * k]


def _into_slot(v, tail, slot, lead, name):
    n = v.shape[1]
    nl = len(lead)
    rows = n // LANES + SUBLANES

    def body(slot_ref, v_ref, t_ref, o_ref):
        for r in range(n // LANES):
            o_ref[r:r + 1, :] = v_ref[0:1, r * LANES:(r + 1) * LANES]
        o_ref[n // LANES:, :] = t_ref[...]

    grid_spec = pltpu.PrefetchScalarGridSpec(
        num_scalar_prefetch=1, grid=(1,),
        in_specs=[pl.BlockSpec(v.shape, lambda i, sl: (0, 0)), pl.BlockSpec(tail.shape, lambda i, sl: (0, 0))],
        out_specs=pl.BlockSpec((None,) * nl + (rows, LANES), lambda i, sl: tuple(sl[q] for q in range(nl)) + (0, 0)))
    return pl.pallas_call(
        body, name=name, grid_spec=grid_spec, out_shape=jax.ShapeDtypeStruct(tuple(lead) + (rows, LANES), F32),
    )(slot, v, tail)


def _rows128(a):
    return a.reshape(-1, LANES)


def _pack_small(parts):
    pieces = [_rows128(parts[n]) for n, _ in SMALL_ROWS]
    pieces.append(jnp.zeros((SMALL_TOTAL - SMALL_USED, LANES), F32))
    return jnp.concatenate(pieces, axis=0)


def kernel(x, p, pre_g, w_in, gmlp_ln_g, gmlp_ln_b, gmlp_ws, gmlp_bs, conv_w, conv_b, w_a, b_a, w_x, b_x, lam, gmlp_out_g, lru_out_g, w_out, post_g, w_pe, w_pg, loss_target, m_pre_g, m_w_in, m_gmlp_ln_g, m_gmlp_ln_b, m_gmlp_ws, m_gmlp_bs, m_conv_w, m_conv_b, m_w_a, m_b_a, m_w_x, m_b_x, m_lam, m_gmlp_out_g, m_lru_out_g, m_w_out, m_post_g, m_w_pe, m_w_pg, v_pre_g, v_w_in, v_gmlp_ln_g, v_gmlp_ln_b, v_gmlp_ws, v_gmlp_bs, v_conv_w, v_conv_b, v_w_a, v_b_a, v_w_x, v_b_x, v_lam, v_gmlp_out_g, v_lru_out_g, v_w_out, v_post_g, v_w_pe, v_w_pg):
    weights = dict(pre_g=pre_g, w_in=w_in, gmlp_ln_g=gmlp_ln_g, gmlp_ln_b=gmlp_ln_b, gmlp_ws=gmlp_ws, gmlp_bs=gmlp_bs,
                   conv_w=conv_w, conv_b=conv_b, w_a=w_a, b_a=b_a, w_x=w_x, b_x=b_x, lam=lam, gmlp_out_g=gmlp_out_g,
                   lru_out_g=lru_out_g, w_out=w_out, post_g=post_g, w_pe=w_pe, w_pg=w_pg)
    mom_m = dict(pre_g=m_pre_g, w_in=m_w_in, gmlp_ln_g=m_gmlp_ln_g, gmlp_ln_b=m_gmlp_ln_b, gmlp_ws=m_gmlp_ws,
                 gmlp_bs=m_gmlp_bs, conv_w=m_conv_w, conv_b=m_conv_b, w_a=m_w_a, b_a=m_b_a, w_x=m_w_x, b_x=m_b_x,
                 lam=m_lam, gmlp_out_g=m_gmlp_out_g, lru_out_g=m_lru_out_g, w_out=m_w_out, post_g=m_post_g,
                 w_pe=m_w_pe, w_pg=m_w_pg)
    mom_v = dict(pre_g=v_pre_g, w_in=v_w_in, gmlp_ln_g=v_gmlp_ln_g, gmlp_ln_b=v_gmlp_ln_b, gmlp_ws=v_gmlp_ws,
                 gmlp_bs=v_gmlp_bs, conv_w=v_conv_w, conv_b=v_conv_b, w_a=v_w_a, b_a=v_b_a, w_x=v_w_x, b_x=v_b_x,
                 lam=v_lam, gmlp_out_g=v_gmlp_out_g, lru_out_g=v_lru_out_g, w_out=v_w_out, post_g=v_post_g,
                 w_pe=v_w_pe, w_pg=v_w_pg)
    order = list(weights)
    xi, yi, ci = _place()
    me = _chip_of(xi, yi)
    kc = jnp.stack([me, ci]).astype(jnp.int32)

    x2 = x[0]
    p2 = p[0, 0]
    tgt = loss_target[0]

    first = [_cast_into_slot(w_in[0], kc, "cast_w_in").reshape(N_CHIPS, 2, D_MODEL // 2, W_IN_COLS),
             _cast_into_slot(conv_w[0, :, 0, :], kc, "conv_w_into_slot", F32).reshape(N_CHIPS, 2, CONV_W // 2, CONV_COLS)]
    in_st, in_tok = _exchange_start("gather_in_start", first, 6, _gather_ici_copies(2))
    later = [_cast_into_slot(w_out[0], kc, "cast_w_out", token=in_tok).reshape(N_CHIPS, 2, W_ROWS // 2, D_MODEL),
             _cast_into_slot(w_pg[0], kc, "cast_w_pg", token=in_tok).reshape(N_CHIPS, 2, W_ROWS // 2, D_MODEL),
             _cast_into_slot(w_pe[0], kc, "cast_w_pe", token=in_tok).reshape(N_CHIPS, 2, D_PLE // 2, W_PE_COLS)]
    gather_st, gather_tok = _exchange_start("gather_start", later, 9, _gather_ici_copies(3), after=in_tok)
    hn, z_own, hn_t = _inproj_local(x2, pre_g, w_in[0], ROW_TILE, gather_tok)
    in_st, in_tok = _exchange_wait_start("gather_in_relay", in_st, z_own, _gather_ici_copies(2), 6,
                                         _gather_relay_copies(2))
    g_in, g_cw = _exchange_wait("gather_in_wait", in_st, in_tok, _gather_relay_copies(2))
    wg_in = g_in.reshape(N_CHIPS, D_MODEL, W_IN_COLS)
    cw_full = jnp.transpose(g_cw.reshape(N_CHIPS, CONV_W, CONV_COLS), (1, 0, 2)).reshape(CONV_W, D_HALF)

    causal = jnp.tril(jnp.ones((CHUNK, CHUNK), dtype=bool))
    ws_m = jnp.where(causal[None], gmlp_ws[0], 0.0)
    prm = dict(
        ln_g=gmlp_ln_g, ln_b=gmlp_ln_b, wt=ws_m.astype(BF16), wtt=jnp.transpose(ws_m, (0, 2, 1)).astype(BF16),
        bsx=jnp.repeat(jnp.transpose(gmlp_bs[0]), CHUNK, axis=1),
        conv_w=cw_full, conv_b=conv_b, w_a=w_a[0].astype(BF16), w_x=w_x[0].astype(BF16),
        b_a=b_a[0].reshape(1, D_HALF), b_x=b_x[0].reshape(1, D_HALF), lam=lam, oga=gmlp_out_g, ogb=lru_out_g)

    z, y, h = _inproj_branches_fwd(hn, z_own, wg_in, kc, prm, ROW_TILE, gather_tok)
    gather_st, gather_tok = _exchange_wait_start("gather_relay", gather_st, y, _gather_ici_copies(3), 9,
                                                 _gather_relay_copies(3))
    g_out, g_pg, g_pe = _exchange_wait("gather_wait", gather_st, gather_tok, _gather_relay_copies(3))
    wg_out = g_out.reshape(D_MODEL, D_MODEL)
    wg_pg = g_pg.reshape(D_MODEL, D_MODEL)
    wg_pe = g_pe.reshape(N_CHIPS, D_PLE, W_PE_COLS)
    o, h1, gt, dout, loss_acc = _outproj_fwd(x2, y, p2, tgt, post_g, wg_out, wg_pg, wg_pe, ROW_TILE)

    def sibling_start(tag, bufs):
        lands = [_landing((b.shape[0],) + b.shape[2:], b.dtype) for b in bufs]
        return _exchange_start("sibling_start_" + tag, bufs + lands, len(bufs), _sibling_copies(len(bufs)))

    def pair_then_chip_start(tag, started, after, names, tiles, dtypes):
        n = len(names)
        got = _exchange_wait("sibling_wait_" + tag, started, after, _sibling_copies(n))
        pairs = [_pair_sum(got[b], got[n + b], kc, "pair_sum_" + names[b], tiles[b], dtypes[b]) for b in range(n)]
        lands = [_landing((3,) + pr[0].shape[1:], pr[0].dtype) for pr in pairs]
        return _exchange_start("chip_start_" + tag, [pr[0] for pr in pairs] + lands, 3 * n, _chip_copies(n)), pairs

    def sum_then_finish_start(tag, started, pairs, after, names, tiles, small, to_all=()):
        n = len(names)
        got = _exchange_wait("chip_wait_" + tag, started, after, _chip_copies(n))
        sums = [_chip_sum(pairs[b][1], got[n + b], kc if small and b == n - 1 else kc[1:],
                          (N_CHIPS, 2) if small and b == n - 1 else (2,), "chip_sum_" + names[b], tiles[b])
                for b in range(n)]
        nbig = n - 1 if small else n
        n_all = n - nbig + len(to_all)
        return _exchange_start("finish_start_" + tag, sums + list(to_all), nbig + 7 * n_all,
                               _finish_copies(nbig, n_all))

    gw_pe, dq, dh1, do, dy, g_post = _head_bwd(dout, gt, p2, o, post_g, wg_out, wg_pg, wg_pe, ROW_TILE)
    gw_pe = gw_pe.reshape(N_CHIPS, 2, D_PLE // 2, W_PE_COLS)
    token0 = jnp.zeros((SUBLANES, LANES), F32)
    gw_out = _weight_grad(y, do, "grad_w_out", 2, 1, D_MODEL // 2, D_MODEL, CONTRACT_TILE, token0)
    gw_pg = _weight_grad(h1, dq, "grad_w_pg", 2, 1, D_MODEL // 2, D_MODEL, CONTRACT_TILE, token0)
    gw_out = gw_out.reshape(N_CHIPS, 2, W_ROWS // 2, D_MODEL)
    gw_pg = gw_pg.reshape(N_CHIPS, 2, W_ROWS // 2, D_MODEL)

    names_a, tiles_a = ["w_out", "w_pg", "w_pe"], [SUM_TILE] * 3
    st, tok = sibling_start("a", [gw_out, gw_pg, gw_pe])
    (dz, g_oga, g_ogb, g_lng, g_lnb, g_bsx, g_ws, g_cw, g_cb, g_wa, g_ba, g_wx, g_bx, g_lam) = _branches_bwd(
        z, h, dy, prm, ROW_TILE, tok)
    (st, tok), pairs_a = pair_then_chip_start("a", st, dz, names_a, tiles_a, [BF16] * 3)
    gw_in = _weight_grad(hn_t, dz, "grad_w_in", 2, N_CHIPS, D_MODEL // 2, W_IN_COLS, CONTRACT_TILE, tok,
                         a_transposed=True)
    fin_a, tok = sum_then_finish_start("a", st, pairs_a, gw_in, names_a, tiles_a, False)

    small_g = dict(
        gmlp_ln_g=g_lng[0:1], gmlp_ln_b=g_lnb[0:1], gmlp_ws=g_ws,
        gmlp_bs=jnp.transpose(g_bsx[:, ::CHUNK]), conv_w=g_cw[::SUBLANES], conv_b=g_cb[0:1], w_a=g_wa, b_a=g_ba[0:1],
        w_x=g_wx, b_x=g_bx[0:1], lam=g_lam[0:1], gmlp_out_g=g_oga[0:1], lru_out_g=g_ogb[0:1], post_g=g_post[0:1])
    gsm = _pack_small(small_g).reshape(N_CHIPS, 2, SMALL_PIECE, LANES)

    names_b, tiles_b = ["w_in", "small"], [2 * SUM_TILE, SMALL_PIECE]
    n_tiles = x2.shape[0] // ROW_TILE
    n_lo = max(1, (5 * n_tiles) // 16)
    st, tok_b = _exchange_start(
        "sibling_start_b", [gw_in, gsm] + [_landing((N_CHIPS,) + b.shape[2:], F32) for b in (gw_in, gsm)], 2,
        _sibling_copies(2), after=tok)
    part = _inproj_bwd(dz, wg_in, x2, dh1, pre_g, ROW_TILE, 0, n_lo, None, False, tok_b, "inproj_bwd_lo")
    f_out, f_pg, f_pe = _exchange_wait("finish_wait_a", fin_a, part[1], _finish_copies(3, 0))
    (st, tok_b), pairs_b = pair_then_chip_start("b", st, part[1], names_b, tiles_b, [BF16, F32])
    grad_x, g_pre = _inproj_bwd(dz, wg_in, x2, dh1, pre_g, ROW_TILE, n_lo, n_tiles - n_lo, part, True, tok_b,
                                "inproj_bwd_hi")
    pre_parts = _into_slot(g_pre, loss_acc, kc, (N_CHIPS, 2), "pre_g_into_slot")
    fin_b, tok_b = sum_then_finish_start("b", st, pairs_b, g_pre, names_b, tiles_b, True, to_all=[pre_parts])

    grads, deltas, new_m, new_v = {}, {}, {}, {}

    def adam_big(n, g2d, tr, token):
        shp = weights[n].shape
        g, d, nm, nv = _adamw(weights[n][0], g2d, mom_m[n][0], mom_v[n][0], "adamw_" + n, tr, token)
        grads[n], deltas[n], new_m[n], new_v[n] = g.reshape(shp), d.reshape(shp), nm.reshape(shp), nv.reshape(shp)
        return d

    as_token = lambda d: d[:SUBLANES, :LANES]
    last = adam_big("w_out", f_out.reshape(W_ROWS, D_MODEL), SUM_TILE, tok_b)
    last = adam_big("w_pg", f_pg.reshape(W_ROWS, D_MODEL), SUM_TILE, as_token(last))
    last = adam_big("w_pe", f_pe.reshape(D_PLE, W_PE_COLS), SUM_TILE, as_token(last))
    f_in, f_sm, pre_parts = _exchange_wait("finish_wait_b", fin_b, last, _finish_copies(1, 2))
    adam_big("w_in", f_in.reshape(D_MODEL, W_IN_COLS), 2 * SUM_TILE, tok_b)

    packed_g = f_sm.reshape(SMALL_TOTAL, LANES)
    small_names = ["pre_g"] + [n for n, _ in SMALL_ROWS if n != "conv_w"]
    natural = lambda src: {n: (src[n] if src[n].ndim == 2 else src[n][0]) for n in small_names}
    outs, loss_block = _adamw_small(packed_g, pre_parts.reshape(8, D_MODEL // LANES + SUBLANES, LANES),
                                    natural(weights), natural(mom_m), natural(mom_v))
    loss = loss_block[0, 0]
    for dst, got in zip((grads, deltas, new_m, new_v), outs):
        for n in small_names:
            dst[n] = got[n].reshape(weights[n].shape)
    at = sum(r for n, r in SMALL_ROWS[:[n for n, _ in SMALL_ROWS].index("conv_w")])
    g_cw_all = packed_g[at:at + CONV_W * D_HALF // LANES].reshape(CONV_W, D_HALF)
    g_conv = lax.dynamic_slice_in_dim(g_cw_all, me * CONV_COLS, CONV_COLS, axis=1)
    g, d, nm, nv = _adamw(conv_w[0, :, 0, :], g_conv, m_conv_w[0, :, 0, :], v_conv_w[0, :, 0, :], "adamw_conv_w", CONV_W,
                          tok_b)
    cshape = conv_w.shape
    grads["conv_w"], deltas["conv_w"] = g.reshape(cshape), d.reshape(cshape)
    new_m["conv_w"], new_v["conv_w"] = nm.reshape(cshape), nv.reshape(cshape)

    return (loss, grad_x.reshape(x.shape), *[grads[n] for n in order], *[deltas[n] for n in order],
            *[new_m[n] for n in order], *[new_v[n] for n in order])
```

```python
import math

import jax
import jax.numpy as jnp
from jax import lax
from jax.experimental import pallas as pl
from jax.experimental.pallas import tpu as pltpu

F32 = jnp.float32
BF16 = jnp.bfloat16

D_MODEL = 2048
D_HALF = 1024
D_Z = 5120
D_PLE = 256
CHUNK = 128
N_HEADS = 8
N_CHIPS = 4
W_IN_COLS = D_Z // N_CHIPS
W_ROWS = D_MODEL // N_CHIPS
W_PE_COLS = D_MODEL // N_CHIPS
CONV_W = 4
CONV_COLS = D_HALF // N_CHIPS
EPS = 1e-6
LRU_C = 8.0
ADAM_LR, ADAM_B1, ADAM_B2, ADAM_EPS, ADAM_WD, ADAM_STEP = 0.001, 0.9, 0.999, 1e-08, 0.01, 10

SUBLANES = 8
LANES = 128
VMEM_LIMIT = 56 * 1024 * 1024
ROW_TILE = 256
CONTRACT_TILE = 2048
SUM_TILE = 128

SMALL_ROWS = (("gmlp_ln_g", 8), ("gmlp_ln_b", 8), ("gmlp_ws", 1024), ("gmlp_bs", 8),
              ("conv_w", 32), ("conv_b", 8), ("w_a", 1024), ("b_a", 8), ("w_x", 1024), ("b_x", 8),
              ("lam", 8), ("gmlp_out_g", 8), ("lru_out_g", 8), ("post_g", 16))
SMALL_USED = sum(r for _, r in SMALL_ROWS)
SMALL_PIECE = 400
SMALL_TOTAL = 8 * SMALL_PIECE

MESH = pl.DeviceIdType.MESH
ANY = pl.BlockSpec(memory_space=pl.ANY)

_GELU_C0 = math.sqrt(2.0 / math.pi)
_GELU_C1 = 0.044715


def _params(*sem):
    return pltpu.CompilerParams(dimension_semantics=sem, vmem_limit_bytes=VMEM_LIMIT)


def _dot(a, b):
    return jnp.dot(a, b, preferred_element_type=F32)


def _dot_nt(a, b):
    return lax.dot_general(a, b, (((1,), (1,)), ((), ())), preferred_element_type=F32)


def _dot_tn(a, b):
    return lax.dot_general(a, b, (((0,), (0,)), ((), ())), preferred_element_type=F32)


def _gelu(x):
    t = jnp.tanh(_GELU_C0 * (x + _GELU_C1 * (x * x * x)))
    return 0.5 * x * (1.0 + t), t


def _gelu_grad(x, t):
    return 0.5 * (1.0 + t) + 0.5 * x * (1.0 - t * t) * (_GELU_C0 * (1.0 + 3.0 * _GELU_C1 * x * x))


def _rowsum8(v):
    r, n = v.shape
    return jnp.sum(v.reshape(r // SUBLANES, SUBLANES, n), axis=0)


def _lanemean(v):
    return jnp.mean(v, axis=-1, keepdims=True)


def _shift_down(v, halo8, k):
    if k == 0:
        return v
    r = pltpu.roll(v, k, 0)
    hr = pltpu.roll(halo8, k, 0)
    row = lax.broadcasted_iota(jnp.int32, halo8.shape, 0)
    top = jnp.where(row < k, hr, r[0:SUBLANES])
    return jnp.concatenate([top, r[SUBLANES:]], axis=0)


def _shift_up(v, next8, k):
    if k == 0:
        return v
    n = v.shape[0]
    r = pltpu.roll(v, n - k, 0)
    nr = pltpu.roll(next8, SUBLANES - k, 0)
    row = lax.broadcasted_iota(jnp.int32, next8.shape, 0)
    bot = jnp.where(row >= SUBLANES - k, nr, r[n - SUBLANES:])
    return jnp.concatenate([r[:n - SUBLANES], bot], axis=0)


def _layernorm_parts(vg):
    mu = _lanemean(vg)
    xc = vg - mu
    rstd = lax.rsqrt(_lanemean(xc * xc) + EPS)
    return xc * rstd, rstd


def _spatial_mix(wt_ref, vn_ref, bsx_ref, mixed_ref, tm):
    for c in range(tm // CHUNK):
        rows = slice(c * CHUNK, (c + 1) * CHUNK)
        for h in range(N_HEADS):
            cols = slice(h * CHUNK, (h + 1) * CHUNK)
            mixed_ref[rows, cols] = _dot(wt_ref[h], vn_ref[rows, cols]) + bsx_ref[:, cols]


def _conv_taps(xb, halo8):
    return [_shift_down(xb, halo8, CONV_W - 1 - k) for k in range(CONV_W)]


def _lru_gates(xc_bf_ref, wa_ref, wx_ref, ba_ref, bx_ref, r_ref, i_ref):
    for h in range(N_HEADS):
        cols = slice(h * CHUNK, (h + 1) * CHUNK)
        xh = xc_bf_ref[:, cols]
        r_ref[:, cols] = jax.nn.sigmoid(_dot(xh, wa_ref[h]) + ba_ref[:, cols])
        i_ref[:, cols] = jax.nn.sigmoid(_dot(xh, wx_ref[h]) + bx_ref[:, cols])


def _softplus_neg(lam):
    return jnp.maximum(-lam, 0.0) + jnp.log(1.0 + jnp.exp(-jnp.abs(lam)))


def _decay_parts(r, lam):
    la = (-LRU_C * _softplus_neg(lam)) * r
    a = jnp.exp(la)
    th = -jnp.tanh(la)
    mult = jnp.sqrt(2.0 * th / (1.0 + th))
    return a, mult


def _z_group(zref, g, rows=slice(None)):
    lo = g * D_HALF
    blk, off = lo // W_IN_COLS, lo % W_IN_COLS
    if off + D_HALF <= W_IN_COLS:
        return zref[blk, rows, off:off + D_HALF]
    return jnp.concatenate([zref[blk, rows, off:W_IN_COLS], zref[blk + 1, rows, 0:off + D_HALF - W_IN_COLS]], axis=1)


def _inproj_local(x, pre_g, w_own, tm, token):
    t = x.shape[0]

    def body(x_ref, g_ref, w_ref, token_ref, hn_ref, zl_ref, hnt_ref, wbf_s):
        @pl.when(pl.program_id(0) == 0)
        def _():
            wbf_s[...] = w_ref[...].astype(BF16)

        xv = x_ref[...]
        hnf = xv * lax.rsqrt(_lanemean(xv * xv) + EPS) * g_ref[...]
        hn = hnf.astype(BF16)
        hn_ref[...] = hn
        hnt_ref[...] = hnf.T.astype(BF16)
        zl_ref[...] = _dot(hn, wbf_s[...]).astype(BF16)

    row = lambda n: pl.BlockSpec((tm, n), lambda i: (i, 0))
    const = lambda shp: pl.BlockSpec(shp, lambda i: (0, 0), pipeline_mode=pl.Buffered(1))
    return pl.pallas_call(
        body, name="inproj_local", grid=(t // tm,),
        in_specs=[row(D_MODEL), const((1, D_MODEL)), const((D_MODEL, W_IN_COLS)), const((SUBLANES, LANES))],
        out_specs=[row(D_MODEL), row(W_IN_COLS), pl.BlockSpec((D_MODEL, tm), lambda i: (0, i))],
        out_shape=[jax.ShapeDtypeStruct((t, D_MODEL), BF16), jax.ShapeDtypeStruct((t, W_IN_COLS), BF16),
                   jax.ShapeDtypeStruct((D_MODEL, t), BF16)],
        scratch_shapes=[pltpu.VMEM((D_MODEL, W_IN_COLS), BF16)],
        compiler_params=_params("arbitrary"),
    )(x, pre_g, w_own, token)


def _inproj_branches_fwd(hn, z_own, wg_in, kc, prm, tm, token):
    t = hn.shape[0]
    nt = t // tm
    hb = tm // SUBLANES

    def body(kc_ref, hn_ref, zo_ref, w1_ref, w2_ref, w3_ref,
             lng_ref, lnb_ref, wt_ref, bsx_ref, cw_ref, cb_ref, wa_ref, wx_ref, ba_ref, bx_ref, lam_ref,
             oga_ref, ogb_ref, token_ref,
             z_ref, y_ref, h_ref,
             zbuf0, zbuf1, vn_s, mixed_s, xcbf_s, r_s, i_s, ug_s, halo_s, carry_s):
        s = pl.program_id(0)
        me = kc_ref[0]
        w_refs = (None, w1_ref, w2_ref, w3_ref)

        @pl.when(s == 0)
        def _():
            zbuf1[...] = jnp.zeros_like(zbuf1)

        @pl.when(s <= 1)
        def _():
            carry_s[...] = jnp.zeros_like(carry_s)
            halo_s[...] = jnp.zeros_like(halo_s)

        def step(zw, zr):
            def project(r):
                blk = (me + r) % N_CHIPS
                zb = zo_ref[...] if r == 0 else _dot(hn_ref[...], w_refs[r][...]).astype(BF16)
                z_ref[blk] = zb
                zw[blk] = zb

            zin = lambda g: _z_group(zr, g).astype(F32)
            always = [s >= 0] * 4

            @pl.when(always[0])
            def _():
                project(0)
                ug, _ = _gelu(zin(0))
                ug_s[...] = ug
                vg, _ = _gelu(zin(1))
                vhat, _ = _layernorm_parts(vg)
                vn_s[...] = (vhat * lng_ref[...] + lnb_ref[...]).astype(BF16)

            @pl.when(always[1])
            def _():
                project(1)
                _spatial_mix(wt_ref, vn_s, bsx_ref, mixed_s, tm)
                ga = zin(2)
                ya = ug_s[...] * mixed_s[...] * (ga * jax.nn.sigmoid(ga))
                ra = lax.rsqrt(_lanemean(ya * ya) + EPS)
                y_ref[:, 0:D_HALF] = (ya * ra * oga_ref[...]).astype(BF16)

            @pl.when(always[2])
            def _():
                project(2)
                xb = zin(3)
                taps = _conv_taps(xb, halo_s[...])
                halo_s[...] = xb[tm - SUBLANES:]
                xc = cb_ref[...] + taps[0] * cw_ref[0:1, :]
                for k in range(1, CONV_W):
                    xc = xc + taps[k] * cw_ref[k:k + 1, :]
                xcbf_s[...] = xc.astype(BF16)
                _lru_gates(xcbf_s, wa_ref, wx_ref, ba_ref, bx_ref, r_s, i_s)
                a, mult = _decay_parts(r_s[...], lam_ref[...])
                row = lax.broadcasted_iota(jnp.int32, a.shape, 0)
                mult = jnp.where(jnp.logical_and(s == 1, row == 0), 1.0, mult)
                r_s[...] = a
                i_s[...] = mult * (i_s[...] * xc)

            @pl.when(always[3])
            def _():
                project(3)
                a = r_s[...]
                b = i_s[...]
                r8 = lax.broadcasted_iota(jnp.int32, a.shape, 0) & (SUBLANES - 1)
                for d in (1, 2, 4):
                    a_sh = pltpu.roll(a, d, 0)
                    b_sh = pltpu.roll(b, d, 0)
                    m = r8 >= d
                    b = jnp.where(m, a * b_sh + b, b)
                    a = jnp.where(m, a * a_sh, a)
                carry = carry_s[...]
                for g in range(hb):
                    rows = slice(g * SUBLANES, (g + 1) * SUBLANES)
                    hg = a[rows] * carry + b[rows]
                    h_ref[rows, :] = hg
                    carry = jnp.broadcast_to(hg[SUBLANES - 1:SUBLANES, :], hg.shape)
                carry_s[...] = carry
                gb = zin(4)
                yb = h_ref[...] * (gb * jax.nn.sigmoid(gb))
                rb = lax.rsqrt(_lanemean(yb * yb) + EPS)
                y_ref[:, D_HALF:] = (yb * rb * ogb_ref[...]).astype(BF16)

        @pl.when(s % 2 == 0)
        def _():
            step(zbuf0, zbuf1)

        @pl.when(s % 2 == 1)
        def _():
            step(zbuf1, zbuf0)

    const = lambda a: pl.BlockSpec(a.shape, lambda s, kc, n=a.ndim: (0,) * n, pipeline_mode=pl.Buffered(1))
    proj = lambda n: pl.BlockSpec((tm, n), lambda s, kc: (jnp.minimum(s, nt - 1), 0))
    head = lambda n: pl.BlockSpec((tm, n), lambda s, kc: (jnp.maximum(s - 1, 0), 0))
    other = lambda r: pl.BlockSpec((None, D_MODEL, W_IN_COLS), lambda s, kc, r=r: ((kc[0] + r) % N_CHIPS, 0, 0),
                                   pipeline_mode=pl.Buffered(1))
    names = ("ln_g", "ln_b", "wt", "bsx", "conv_w", "conv_b", "w_a", "w_x", "b_a", "b_x", "lam", "oga", "ogb")
    pr = [prm[n] for n in names] + [token]
    big = lambda dt: pltpu.VMEM((tm, D_HALF), dt)
    zblocks = pltpu.VMEM((N_CHIPS, tm, W_IN_COLS), BF16)
    grid_spec = pltpu.PrefetchScalarGridSpec(
        num_scalar_prefetch=1, grid=(nt + 1,),
        in_specs=[proj(D_MODEL), proj(W_IN_COLS), other(1), other(2), other(3)] + [const(a) for a in pr],
        out_specs=[pl.BlockSpec((N_CHIPS, tm, W_IN_COLS), lambda s, kc: (0, jnp.minimum(s, nt - 1), 0)),
                   head(D_MODEL), head(D_HALF)],
        scratch_shapes=[zblocks, zblocks, big(BF16), big(F32), big(BF16), big(F32), big(F32), big(F32),
                        pltpu.VMEM((SUBLANES, D_HALF), F32), pltpu.VMEM((SUBLANES, D_HALF), F32)])
    return pl.pallas_call(
        body, name="inproj_branches_fwd", grid_spec=grid_spec,
        out_shape=[jax.ShapeDtypeStruct((N_CHIPS, t, W_IN_COLS), BF16), jax.ShapeDtypeStruct((t, D_MODEL), BF16),
                   jax.ShapeDtypeStruct((t, D_HALF), F32)],
        compiler_params=_params("arbitrary"),
    )(kc, hn, z_own, wg_in, wg_in, wg_in, *pr)


def _outproj_fwd(x, y, p, tgt, post_g, w_out, w_pg, wg_pe, tm):
    t = x.shape[0]

    def body(x_ref, y_ref, p_ref, tgt_ref, pg_ref, wo_ref, wpg_ref, wpe_ref,
             o_ref, h1_ref, gt_ref, dout_ref, loss_ref):
        @pl.when(pl.program_id(0) == 0)
        def _():
            loss_ref[...] = jnp.zeros_like(loss_ref)

        o = _dot(y_ref[...], wo_ref[...])
        o_ref[...] = o
        r3 = lax.rsqrt(_lanemean(o * o) + EPS)
        h1 = x_ref[...] + (o * r3) * pg_ref[...]
        h1b = h1.astype(BF16)
        h1_ref[...] = h1b
        gt = jax.nn.sigmoid(_dot(h1b, wpg_ref[...]))
        gt_ref[...] = gt
        pb = p_ref[...].astype(BF16)
        for k in range(N_CHIPS):
            cols = slice(k * W_PE_COLS, (k + 1) * W_PE_COLS)
            pe = _dot(pb, wpe_ref[k])
            d = h1[:, cols] + pe * gt[:, cols] - tgt_ref[:, cols]
            dout_ref[:, cols] = d * (1.0 / D_MODEL)
            loss_ref[...] += jnp.sum(d * d) * (0.5 / D_MODEL)

    row = lambda n: pl.BlockSpec((tm, n), lambda i: (i, 0))
    const = lambda shp: pl.BlockSpec(shp, lambda i, n=len(shp): (0,) * n, pipeline_mode=pl.Buffered(1))
    return pl.pallas_call(
        body, name="outproj_fwd", grid=(t // tm,),
        in_specs=[row(D_MODEL), row(D_MODEL), row(D_PLE), row(D_MODEL), const((1, D_MODEL)),
                  const((D_MODEL, D_MODEL)), const((D_MODEL, D_MODEL)), const((N_CHIPS, D_PLE, W_PE_COLS))],
        out_specs=[row(D_MODEL), row(D_MODEL), row(D_MODEL), row(D_MODEL),
                   pl.BlockSpec((SUBLANES, LANES), lambda i: (0, 0))],
        out_shape=[jax.ShapeDtypeStruct((t, D_MODEL), F32), jax.ShapeDtypeStruct((t, D_MODEL), BF16),
                   jax.ShapeDtypeStruct((t, D_MODEL), F32), jax.ShapeDtypeStruct((t, D_MODEL), F32),
                   jax.ShapeDtypeStruct((SUBLANES, LANES), F32)],
        compiler_params=_params("arbitrary"),
    )(x, y, p, tgt, post_g, w_out, w_pg, wg_pe)


def _head_bwd(dout, gt, p, o, post_g, w_out, w_pg, wg_pe, tm):
    t = dout.shape[0]

    def body(dout_ref, gt_ref, p_ref, o_ref, pg_ref, wo_ref, wpg_ref, wpe_ref,
             gwpe_ref, dq_ref, dh1_ref, do_ref, dy_ref, gpost_ref):
        i = pl.program_id(0)

        @pl.when(i == 0)
        def _():
            gpost_ref[...] = jnp.zeros_like(gpost_ref)
            gwpe_ref[...] = jnp.zeros_like(gwpe_ref)

        dout = dout_ref[...]
        gt = gt_ref[...]
        pb = p_ref[...].astype(BF16)
        for k in range(N_CHIPS):
            cols = slice(k * W_PE_COLS, (k + 1) * W_PE_COLS)
            pe = _dot(pb, wpe_ref[k])
            g = gt[:, cols]
            dg = dout[:, cols] * g
            gwpe_ref[k] += _dot_tn(pb, dg.astype(BF16))
            dq_ref[:, cols] = (dg * pe * (1.0 - g)).astype(BF16)
        dh1 = dout + _dot_nt(dq_ref[...], wpg_ref[...])
        dh1_ref[...] = dh1
        o = o_ref[...]
        r3 = lax.rsqrt(_lanemean(o * o) + EPS)
        on = o * r3
        gpost_ref[...] += _rowsum8(dh1 * on)
        don = dh1 * pg_ref[...]
        do = r3 * (don - on * _lanemean(don * on))
        dob = do.astype(BF16)
        do_ref[...] = dob
        dy_ref[...] = _dot_nt(dob, wo_ref[...])

        @pl.when(i == pl.num_programs(0) - 1)
        def _():
            gpost_ref[...] = jnp.broadcast_to(jnp.sum(gpost_ref[...], axis=0, keepdims=True), gpost_ref.shape)

    row = lambda n: pl.BlockSpec((tm, n), lambda i: (i, 0))
    const = lambda shp: pl.BlockSpec(shp, lambda i, n=len(shp): (0,) * n, pipeline_mode=pl.Buffered(1))
    return pl.pallas_call(
        body, name="head_bwd", grid=(t // tm,),
        in_specs=[row(D_MODEL), row(D_MODEL), row(D_PLE), row(D_MODEL), const((1, D_MODEL)),
                  const((D_MODEL, D_MODEL)), const((D_MODEL, D_MODEL)), const((N_CHIPS, D_PLE, W_PE_COLS))],
        out_specs=[pl.BlockSpec((N_CHIPS, D_PLE, W_PE_COLS), lambda i: (0, 0, 0)),
                   row(D_MODEL), row(D_MODEL), row(D_MODEL), row(D_MODEL),
                   pl.BlockSpec((SUBLANES, D_MODEL), lambda i: (0, 0))],
        out_shape=[jax.ShapeDtypeStruct((N_CHIPS, D_PLE, W_PE_COLS), F32), jax.ShapeDtypeStruct((t, D_MODEL), BF16),
                   jax.ShapeDtypeStruct((t, D_MODEL), F32), jax.ShapeDtypeStruct((t, D_MODEL), BF16),
                   jax.ShapeDtypeStruct((t, D_MODEL), F32), jax.ShapeDtypeStruct((SUBLANES, D_MODEL), F32)],
        compiler_params=_params("arbitrary"),
    )(dout, gt, p, o, post_g, w_out, w_pg, wg_pe)


def _branches_bwd(z, h, dy, prm, tm, token):
    t = h.shape[0]
    nt = t // tm
    hb = tm // SUBLANES

    def body(z_ref, zh_ref, h_ref, hh_ref, dy_ref,
             lng_ref, lnb_ref, wt_ref, wtt_ref, bsx_ref, cw_ref, cb_ref, wa_ref, wx_ref, ba_ref, bx_ref, lam_ref,
             oga_ref, ogb_ref, token_ref,
             dz_ref, g_oga, g_ogb, g_lng, g_lnb, g_bsx, g_ws, g_cw, g_cb, g_wa, g_ba, g_wx, g_bx, g_lam,
             vn_s, mixed_s, dm_s, dvn_s, xcbf_s, r_s, i_s, a_s, b_s, dh_s, dpr_s, dpi_s, dxc_s,
             ca_s, cd_s, cx_s):
        step_i = pl.program_id(0)
        tile = nt - 1 - step_i
        accs = (g_oga, g_ogb, g_lng, g_lnb, g_bsx, g_ws, g_cw, g_cb, g_wa, g_ba, g_wx, g_bx, g_lam)

        @pl.when(step_i == 0)
        def _():
            for r in accs + (ca_s, cd_s, cx_s):
                r[...] = jnp.zeros_like(r)

        dy_a = dy_ref[:, 0:D_HALF]
        dy_b = dy_ref[:, D_HALF:]

        u = _z_group(z_ref, 0).astype(F32)
        ug, tu = _gelu(u)
        v = _z_group(z_ref, 1).astype(F32)
        vg, tv = _gelu(v)
        vhat, rstd = _layernorm_parts(vg)
        vn_s[...] = (vhat * lng_ref[...] + lnb_ref[...]).astype(BF16)
        _spatial_mix(wt_ref, vn_s, bsx_ref, mixed_s, tm)
        mixed = mixed_s[...]
        ga = _z_group(z_ref, 2).astype(F32)
        sga = jax.nn.sigmoid(ga)
        sa = ga * sga
        um = ug * mixed
        ya = um * sa
        ra = lax.rsqrt(_lanemean(ya * ya) + EPS)
        yahat = ya * ra
        g_oga[...] += _rowsum8(dy_a * yahat)
        dn = dy_a * oga_ref[...]
        dya = ra * (dn - yahat * _lanemean(dn * yahat))
        dz_ref[:, 2 * D_HALF:3 * D_HALF] = (dya * um * (sga * (1.0 + ga * (1.0 - sga)))).astype(BF16)
        dz_ref[:, 0:D_HALF] = (dya * mixed * sa * _gelu_grad(u, tu)).astype(BF16)
        dmixed = dya * ug * sa
        g_bsx[...] += jnp.sum(dmixed.reshape(tm // CHUNK, CHUNK, D_HALF), axis=0)
        dm_s[...] = dmixed.astype(BF16)
        for c in range(tm // CHUNK):
            rows = slice(c * CHUNK, (c + 1) * CHUNK)
            for hd in range(N_HEADS):
                cols = slice(hd * CHUNK, (hd + 1) * CHUNK)
                dmh = dm_s[rows, cols]
                dvn_s[rows, cols] = _dot(wtt_ref[hd], dmh)
                g_ws[hd] += _dot_nt(dmh, vn_s[rows, cols])
        dvn = dvn_s[...]
        g_lng[...] += _rowsum8(dvn * vhat)
        g_lnb[...] += _rowsum8(dvn)
        dvh = dvn * lng_ref[...]
        dvg = rstd * (dvh - _lanemean(dvh) - vhat * _lanemean(dvh * vhat))
        dz_ref[:, D_HALF:2 * D_HALF] = (dvg * _gelu_grad(v, tv)).astype(BF16)

        xb = _z_group(z_ref, 3).astype(F32)
        halo = jnp.where(tile == 0, 0.0, _z_group(zh_ref, 3).astype(F32)[SUBLANES:])
        taps = _conv_taps(xb, halo)
        xc = cb_ref[...] + taps[0] * cw_ref[0:1, :]
        for k in range(1, CONV_W):
            xc = xc + taps[k] * cw_ref[k:k + 1, :]
        xcbf_s[...] = xc.astype(BF16)
        _lru_gates(xcbf_s, wa_ref, wx_ref, ba_ref, bx_ref, r_s, i_s)
        rg = r_s[...]
        ig = i_s[...]
        lam = lam_ref[...]
        a, mult_true = _decay_parts(rg, lam)
        row = lax.broadcasted_iota(jnp.int32, a.shape, 0)
        first = jnp.logical_and(tile == 0, row == 0)
        mult = jnp.where(first, 1.0, mult_true)
        hcur = h_ref[...]
        hprev = _shift_down(hcur, jnp.where(tile == 0, 0.0, hh_ref[...]), 1)
        gb = _z_group(z_ref, 4).astype(F32)
        sgb = jax.nn.sigmoid(gb)
        sb = gb * sgb
        yb = hcur * sb
        rb = lax.rsqrt(_lanemean(yb * yb) + EPS)
        ybhat = yb * rb
        g_ogb[...] += _rowsum8(dy_b * ybhat)
        dn = dy_b * ogb_ref[...]
        dyb = rb * (dn - ybhat * _lanemean(dn * ybhat))
        dz_ref[:, 4 * D_HALF:5 * D_HALF] = (dyb * hcur * (sgb * (1.0 + gb * (1.0 - sgb)))).astype(BF16)

        an = _shift_up(a, ca_s[...], 1)
        bb = dyb * sb
        r8 = row & (SUBLANES - 1)
        for d in (1, 2, 4):
            a_sh = pltpu.roll(an, tm - d, 0)
            b_sh = pltpu.roll(bb, tm - d, 0)
            m = r8 + d < SUBLANES
            bb = jnp.where(m, an * b_sh + bb, bb)
            an = jnp.where(m, an * a_sh, an)
        a_s[...] = an
        b_s[...] = bb

        def step(g, carry):
            sl = pl.ds(pl.multiple_of((hb - 1 - g) * SUBLANES, SUBLANES), SUBLANES)
            dg = a_s[sl, :] * carry + b_s[sl, :]
            dh_s[sl, :] = dg
            return jnp.broadcast_to(dg[0:1, :], dg.shape)

        cd_s[...] = lax.fori_loop(0, hb, step, cd_s[...])
        ca_s[...] = jnp.broadcast_to(a[0:1, :], ca_s.shape)
        dh = dh_s[...]
        da = dh * hprev
        gx = ig * xc
        dla = da * a - jnp.where(first, 0.0, dh * gx * (a * a / mult_true))
        g_lam[...] += _rowsum8(dla * rg)
        dr = dla * (-LRU_C * _softplus_neg(lam))
        dpr = dr * rg * (1.0 - rg)
        dpi = (dh * mult * xc) * ig * (1.0 - ig)
        g_ba[...] += _rowsum8(dpr)
        g_bx[...] += _rowsum8(dpi)
        dpr_s[...] = dpr.astype(BF16)
        dpi_s[...] = dpi.astype(BF16)
        for hd in range(N_HEADS):
            cols = slice(hd * CHUNK, (hd + 1) * CHUNK)
            xh = xcbf_s[:, cols]
            dprh = dpr_s[:, cols]
            dpih = dpi_s[:, cols]
            g_wa[hd] += _dot_tn(xh, dprh)
            g_wx[hd] += _dot_tn(xh, dpih)
            dxc_s[:, cols] = _dot_nt(dprh, wa_ref[hd]) + _dot_nt(dpih, wx_ref[hd])
        dxc = dxc_s[...] + dh * mult * ig
        g_cb[...] += _rowsum8(dxc)
        for k in range(CONV_W):
            g_cw[k * SUBLANES:(k + 1) * SUBLANES, :] += _rowsum8(dxc * taps[k])
        nxt = cx_s[...]
        dxb = dxc * cw_ref[CONV_W - 1:CONV_W, :]
        for j in range(1, CONV_W):
            dxb = dxb + _shift_up(dxc, nxt, j) * cw_ref[CONV_W - 1 - j:CONV_W - j, :]
        dz_ref[:, 3 * D_HALF:4 * D_HALF] = dxb.astype(BF16)
        cx_s[...] = dxc[0:SUBLANES]

        @pl.when(step_i == nt - 1)
        def _():
            for r in (g_oga, g_ogb, g_lng, g_lnb, g_cb, g_ba, g_bx):
                r[...] = jnp.broadcast_to(jnp.sum(r[...], axis=0, keepdims=True), r.shape)
            lam_f = LRU_C * jax.nn.sigmoid(-lam_ref[...])
            g_lam[...] = jnp.broadcast_to(jnp.sum(g_lam[...], axis=0, keepdims=True) * lam_f, g_lam.shape)
            for k in range(CONV_W):
                blk = g_cw[k * SUBLANES:(k + 1) * SUBLANES, :]
                g_cw[k * SUBLANES:(k + 1) * SUBLANES, :] = jnp.broadcast_to(jnp.sum(blk, axis=0, keepdims=True), blk.shape)
            tri = (lax.broadcasted_iota(jnp.int32, (CHUNK, CHUNK), 0) >= lax.broadcasted_iota(jnp.int32, (CHUNK, CHUNK), 1))
            for hd in range(N_HEADS):
                cols = slice(hd * CHUNK, (hd + 1) * CHUNK)
                g_ws[hd] = jnp.where(tri, g_ws[hd], 0.0)
                blk = g_bsx[:, cols]
                g_bsx[:, cols] = jnp.broadcast_to(jnp.sum(blk, axis=1, keepdims=True), blk.shape)

    rev = lambda i: nt - 1 - i
    zspec = pl.BlockSpec((N_CHIPS, tm, W_IN_COLS), lambda i: (0, rev(i), 0))
    halo = lambda col: pl.BlockSpec((SUBLANES, D_HALF), lambda i: (jnp.maximum(rev(i) * hb - 1, 0), col))
    zhalo = pl.BlockSpec((N_CHIPS, 2 * SUBLANES, W_IN_COLS), lambda i: (0, jnp.maximum(rev(i) * (hb // 2) - 1, 0), 0))
    full = lambda a: pl.BlockSpec(a.shape, lambda i, n=a.ndim: (0,) * n)
    acc = lambda shp: pl.BlockSpec(shp, lambda i, n=len(shp): (0,) * n)
    names = ("ln_g", "ln_b", "wt", "wtt", "bsx", "conv_w", "conv_b", "w_a", "w_x", "b_a", "b_x", "lam", "oga", "ogb")
    pr = [prm[n] for n in names] + [token]
    vec = (SUBLANES, D_HALF)
    mat = (N_HEADS, CHUNK, CHUNK)
    acc_shapes = [vec, vec, vec, vec, (CHUNK, D_HALF), mat, (CONV_W * SUBLANES, D_HALF), vec, mat, vec, mat, vec, vec]
    big = lambda dt: pltpu.VMEM((tm, D_HALF), dt)
    return pl.pallas_call(
        body, name="branches_bwd", grid=(nt,),
        in_specs=[zspec, zhalo,
                  pl.BlockSpec((tm, D_HALF), lambda i: (rev(i), 0)), halo(0),
                  pl.BlockSpec((tm, D_MODEL), lambda i: (rev(i), 0))] + [full(a) for a in pr],
        out_specs=[pl.BlockSpec((tm, D_Z), lambda i: (rev(i), 0))] + [acc(s) for s in acc_shapes],
        out_shape=[jax.ShapeDtypeStruct((t, D_Z), BF16)] + [jax.ShapeDtypeStruct(s, F32) for s in acc_shapes],
        scratch_shapes=[big(BF16), big(F32), big(BF16), big(F32), big(BF16), big(F32), big(F32), big(F32), big(F32),
                        big(F32), big(BF16), big(BF16), big(F32),
                        pltpu.VMEM(vec, F32), pltpu.VMEM(vec, F32), pltpu.VMEM(vec, F32)],
        compiler_params=_params("arbitrary"),
    )(z, z, h, h, dy, *pr)


def _inproj_bwd(dz, wg_in, x, dh1, pre_g, tm, tile0, nt, prev, last, token, name):
    t = x.shape[0]

    def body(*refs):
        dz_ref, w_ref, x_ref, dh1_ref, g_ref = refs[:5]
        gx_ref, gpre_ref, acc_s = refs[-3:]
        i = pl.program_id(0)

        @pl.when(i == 0)
        def _():
            gpre_ref[...] = jnp.zeros_like(gpre_ref) if prev is None else refs[7][...]

        acc = _dot_nt(dz_ref[:, 0:W_IN_COLS], w_ref[0])
        for k in range(1, N_CHIPS):
            acc = acc + _dot_nt(dz_ref[:, k * W_IN_COLS:(k + 1) * W_IN_COLS], w_ref[k])
        acc_s[...] = acc
        for s in range(tm // CHUNK):
            rows = slice(s * CHUNK, (s + 1) * CHUNK)
            xv = x_ref[rows, :]
            r = lax.rsqrt(_lanemean(xv * xv) + EPS)
            xhat = xv * r
            dhn = acc_s[rows, :]
            gpre_ref[...] += _rowsum8(dhn * xhat)
            dxh = dhn * g_ref[...]
            gx_ref[rows, :] = dh1_ref[rows, :] + r * (dxh - xhat * _lanemean(dxh * xhat))

        if last:
            @pl.when(i == nt - 1)
            def _():
                gpre_ref[...] = jnp.broadcast_to(jnp.sum(gpre_ref[...], axis=0, keepdims=True), gpre_ref.shape)

    row = lambda n: pl.BlockSpec((tm, n), lambda i: (tile0 + i, 0))
    small = lambda r: pl.BlockSpec((r, D_MODEL), lambda i: (0, 0))
    tok = pl.BlockSpec((SUBLANES, LANES), lambda i: (0, 0))
    in_specs = [row(D_Z), pl.BlockSpec(wg_in.shape, lambda i: (0, 0, 0), pipeline_mode=pl.Buffered(1)),
                row(D_MODEL), row(D_MODEL), small(1), tok]
    args = [dz, wg_in, x, dh1, pre_g, token]
    aliases = {}
    if prev is not None:
        in_specs += [ANY, small(SUBLANES)]
        args += list(prev)
        aliases = {6: 0}
    return pl.pallas_call(
        body, name=name, grid=(nt,), in_specs=in_specs, out_specs=[row(D_MODEL), small(SUBLANES)],
        out_shape=[jax.ShapeDtypeStruct((t, D_MODEL), F32), jax.ShapeDtypeStruct((SUBLANES, D_MODEL), F32)],
        input_output_aliases=aliases,
        scratch_shapes=[pltpu.VMEM((tm, D_MODEL), F32)],
        compiler_params=_params("arbitrary"),
    )(*args)


def _weight_grad(a, b, name, kb, nb, tk, tn, tt, token, a_transposed=False):
    t = b.shape[0]
    tt = min(tt, t)

    def body(a_ref, b_ref, token_ref, o_ref):
        @pl.when(pl.program_id(2) == 0)
        def _():
            o_ref[...] = jnp.zeros_like(o_ref)

        o_ref[...] += (_dot if a_transposed else _dot_tn)(a_ref[...], b_ref[...])

    a_spec = (pl.BlockSpec((tk, tt), lambda j, i, s: (i, s)) if a_transposed
              else pl.BlockSpec((tt, tk), lambda j, i, s: (s, i)))
    return pl.pallas_call(
        body, name=name, grid=(nb, kb, t // tt),
        in_specs=[a_spec, pl.BlockSpec((tt, tn), lambda j, i, s: (s, j)),
                  pl.BlockSpec((SUBLANES, LANES), lambda j, i, s: (0, 0))],
        out_specs=pl.BlockSpec((None, None, tk, tn), lambda j, i, s: (j, i, 0, 0)),
        out_shape=jax.ShapeDtypeStruct((nb, kb, tk, tn), F32),
        compiler_params=_params("parallel", "parallel", "arbitrary"),
    )(a, b, token)


def _place():
    x, y, c = lax.axis_index("x"), lax.axis_index("y"), lax.axis_index("c")
    return x, y, c


def _chip_of(x, y):
    return 2 * x + y


HBM = pl.BlockSpec(memory_space=pltpu.HBM)
SEM = pl.BlockSpec(memory_space=pltpu.SEMAPHORE)
EFFECT = pltpu.SideEffectType.DATAFLOW_SIDE_EFFECTING


def _hbm(a):
    return pltpu.with_memory_space_constraint(a, pltpu.HBM)


def _landing(shape, dtype):
    return _hbm(lax.empty(shape, dtype))


def _exchange_start(name, arrays, ncopies, build, after=None):
    n = len(arrays)
    extra = [] if after is None else [after]

    def body(*refs):
        ins, token = refs[:n], refs[-1]
        send_sems, recv_sems = refs[n + len(extra)], refs[n + len(extra) + 1]
        for cp in build(ins, send_sems, recv_sems):
            cp.start()
        token[...] = jnp.zeros_like(token)

    outs = pl.pallas_call(
        body, name=name,
        out_shape=(pltpu.SemaphoreType.DMA((ncopies,)), pltpu.SemaphoreType.DMA((ncopies,)),
                   *[pltpu.HBM(a.shape, a.dtype) for a in arrays], jax.ShapeDtypeStruct((SUBLANES, LANES), F32)),
        in_specs=[HBM] * n + [ANY] * len(extra),
        out_specs=(SEM, SEM, *[HBM] * n, pl.BlockSpec(memory_space=pltpu.VMEM)),
        input_output_aliases={q: q + 2 for q in range(n)},
        compiler_params=pltpu.CompilerParams(has_side_effects=EFFECT),
    )(*[_hbm(a) for a in arrays], *extra)
    return (outs[0], outs[1], list(outs[2:2 + n])), outs[-1]


def _exchange_wait(name, started, after, build):
    send, recv, arrays = started
    n = len(arrays)

    def body(*refs):
        ins, send_sems, recv_sems = refs[:n], refs[n], refs[n + 1]
        for cp in build(ins, send_sems, recv_sems):
            cp.wait_send()
            cp.wait_recv()

    return pl.pallas_call(
        body, name=name, out_shape=tuple(pltpu.HBM(a.shape, a.dtype) for a in arrays),
        in_specs=[HBM] * n + [SEM, SEM, ANY], out_specs=tuple([HBM] * n),
        input_output_aliases={q: q for q in range(n)},
        compiler_params=pltpu.CompilerParams(has_side_effects=EFFECT),
    )(*arrays, send, recv, after)


def _exchange_wait_start(name, started, after, build_wait, ncopies, build_start):
    send, recv, arrays = started
    n = len(arrays)

    def body(*refs):
        ins, send_sems, recv_sems = refs[:n], refs[n], refs[n + 1]
        send2, recv2, token = refs[n + 3], refs[n + 4], refs[-1]
        arrived = build_wait(ins, send_sems, recv_sems)
        for cp, onward in zip(arrived, build_start(ins, send2, recv2)):
            cp.wait_recv()
            onward.start()
        for cp in arrived:
            cp.wait_send()
        token[...] = jnp.zeros_like(token)

    outs = pl.pallas_call(
        body, name=name,
        out_shape=(pltpu.SemaphoreType.DMA((ncopies,)), pltpu.SemaphoreType.DMA((ncopies,)),
                   *[pltpu.HBM(a.shape, a.dtype) for a in arrays], jax.ShapeDtypeStruct((SUBLANES, LANES), F32)),
        in_specs=[HBM] * n + [SEM, SEM, ANY], out_specs=(SEM, SEM, *[HBM] * n, pl.BlockSpec(memory_space=pltpu.VMEM)),
        input_output_aliases={q: q + 2 for q in range(n)},
        compiler_params=pltpu.CompilerParams(has_side_effects=EFFECT),
    )(*arrays, send, recv, after)
    return (outs[0], outs[1], list(outs[2:2 + n])), outs[-1]


def _cast_into_slot(w, kc, name, dtype=BF16, token=None):
    rows, cols = w.shape
    tr = min(rows, 256)
    extra = [] if token is None else [token]

    def body(kc_ref, w_ref, *rest):
        rest[-1][...] = w_ref[...].astype(dtype)

    grid_spec = pltpu.PrefetchScalarGridSpec(
        num_scalar_prefetch=1, grid=(rows // tr,),
        in_specs=[pl.BlockSpec((tr, cols), lambda r, kc: (r, 0))]
                 + [pl.BlockSpec((SUBLANES, LANES), lambda r, kc: (0, 0))] * len(extra),
        out_specs=pl.BlockSpec((None, tr, cols), lambda r, kc: (kc[0], r, 0)))
    return pl.pallas_call(
        body, name=name, grid_spec=grid_spec, out_shape=jax.ShapeDtypeStruct((N_CHIPS, rows, cols), dtype),
        compiler_params=_params("arbitrary"),
    )(kc, w, *extra)


def _gather_ici_copies(n):
    def build(refs, send_sems, recv_sems):
        x, y, c = _place()
        mine = lambda b: refs[b].at[_chip_of(x, y), c]
        chips = [(1 - x, y), (x, 1 - y), (1 - x, 1 - y)]
        return [pltpu.make_async_remote_copy(
            src_ref=mine(b), dst_ref=mine(b), send_sem=send_sems.at[3 * b + j], recv_sem=recv_sems.at[3 * b + j],
            device_id=(*chip, c), device_id_type=MESH) for b in range(n) for j, chip in enumerate(chips)]
    return build


def _gather_relay_copies(n):
    def build(refs, send_sems, recv_sems):
        x, y, c = _place()
        chips = [(1 - x, y), (x, 1 - y), (1 - x, 1 - y)]
        cps = []
        for b in range(n):
            for j, chip in enumerate(chips):
                got = refs[b].at[_chip_of(*chip), c]
                cps.append(pltpu.make_async_remote_copy(
                    src_ref=got, dst_ref=got, send_sem=send_sems.at[3 * b + j], recv_sem=recv_sems.at[3 * b + j],
                    device_id=(x, y, 1 - c), device_id_type=MESH))
        return cps
    return build


def _sibling_copies(n):
    def build(refs, send_sems, recv_sems):
        x, y, c = _place()
        return [pltpu.make_async_remote_copy(
            src_ref=refs[b].at[:, 1 - c], dst_ref=refs[n + b], send_sem=send_sems.at[b], recv_sem=recv_sems.at[b],
            device_id=(x, y, 1 - c), device_id_type=MESH) for b in range(n)]
    return build


def _chip_copies(n):
    def build(refs, send_sems, recv_sems):
        x, y, c = _place()
        chips = [(1 - x, y), (x, 1 - y), (1 - x, 1 - y)]
        return [pltpu.make_async_remote_copy(
            src_ref=refs[b].at[_chip_of(*chip)], dst_ref=refs[n + b].at[j],
            send_sem=send_sems.at[3 * b + j], recv_sem=recv_sems.at[3 * b + j],
            device_id=(*chip, c), device_id_type=MESH) for b in range(n) for j, chip in enumerate(chips)]
    return build


def _finish_copies(n, n_all):
    def build(refs, send_sems, recv_sems):
        x, y, c = _place()
        cps = [pltpu.make_async_remote_copy(
            src_ref=refs[b].at[c], dst_ref=refs[b].at[c], send_sem=send_sems.at[b], recv_sem=recv_sems.at[b],
            device_id=(x, y, 1 - c), device_id_type=MESH) for b in range(n)]
        flips = [(fx, fy, fc) for fx in (0, 1) for fy in (0, 1) for fc in (0, 1)][1:]
        for b in range(n_all):
            mine = refs[n + b].at[_chip_of(x, y), c]
            cps += [pltpu.make_async_remote_copy(
                src_ref=mine, dst_ref=mine, send_sem=send_sems.at[n + 7 * b + q], recv_sem=recv_sems.at[n + 7 * b + q],
                device_id=(x ^ fx, y ^ fy, c ^ fc), device_id_type=MESH) for q, (fx, fy, fc) in enumerate(flips)]
        return cps
    return build


def _pair_sum(g, r1, kc, name, tr, send_dtype):
    nk, _, rows, cols = g.shape

    def body(kc_ref, g_ref, r_ref, p_ref, own_ref):
        s = g_ref[...] + r_ref[...]
        p_ref[...] = s.astype(send_dtype)

        @pl.when(pl.program_id(1) == kc_ref[0])
        def _():
            own_ref[...] = s

    grid_spec = pltpu.PrefetchScalarGridSpec(
        num_scalar_prefetch=1, grid=(rows // tr, nk),
        in_specs=[pl.BlockSpec((None, None, tr, cols), lambda r, k, kc: (k, kc[1], r, 0)),
                  pl.BlockSpec((None, tr, cols), lambda r, k, kc: (k, r, 0))],
        out_specs=[pl.BlockSpec((None, tr, cols), lambda r, k, kc: (k, r, 0)),
                   pl.BlockSpec((tr, cols), lambda r, k, kc: (r, 0))])
    return pl.pallas_call(
        body, name=name, grid_spec=grid_spec,
        out_shape=[jax.ShapeDtypeStruct((nk, rows, cols), send_dtype), jax.ShapeDtypeStruct((rows, cols), F32)],
        compiler_params=_params("arbitrary", "arbitrary"),
    )(kc, g, r1)


def _chip_sum(own, r2, slot, lead, name, tr):
    rows, cols = own.shape
    nl = len(lead)

    def body(slot_ref, o_ref, r_ref, s_ref):
        s = o_ref[...]
        for j in range(3):
            s = s + r_ref[j].astype(F32)
        s_ref[...] = s

    grid_spec = pltpu.PrefetchScalarGridSpec(
        num_scalar_prefetch=1, grid=(rows // tr,),
        in_specs=[pl.BlockSpec((tr, cols), lambda r, sl: (r, 0)), pl.BlockSpec((3, tr, cols), lambda r, sl: (0, r, 0))],
        out_specs=pl.BlockSpec((None,) * nl + (tr, cols), lambda r, sl: tuple(sl[q] for q in range(nl)) + (r, 0)))
    return pl.pallas_call(
        body, name=name, grid_spec=grid_spec, out_shape=jax.ShapeDtypeStruct(tuple(lead) + (rows, cols), F32),
        compiler_params=_params("arbitrary"),
    )(slot, own, r2)


def _adam_update(w, g, m, v):
    nm = ADAM_B1 * m + (1.0 - ADAM_B1) * g
    nv = ADAM_B2 * v + (1.0 - ADAM_B2) * (g * g)
    m_hat = nm / (1.0 - ADAM_B1 ** ADAM_STEP)
    v_hat = nv / (1.0 - ADAM_B2 ** ADAM_STEP)
    return -ADAM_LR * (m_hat / (jnp.sqrt(v_hat) + ADAM_EPS) + ADAM_WD * w), nm, nv


def _adamw(w, g, m, v, name, tr, token):
    rows, cols = w.shape

    def body(w_ref, g_ref, m_ref, v_ref, token_ref, go_ref, d_ref, nm_ref, nv_ref):
        gv = g_ref[...]
        go_ref[...] = gv
        d_ref[...], nm_ref[...], nv_ref[...] = _adam_update(w_ref[...], gv, m_ref[...], v_ref[...])

    spec = pl.BlockSpec((tr, cols), lambda r: (r, 0))
    return pl.pallas_call(
        body, name=name, grid=(rows // tr,),
        in_specs=[spec] * 4 + [pl.BlockSpec((SUBLANES, LANES), lambda r: (0, 0))], out_specs=[spec] * 4,
        out_shape=[jax.ShapeDtypeStruct((rows, cols), F32)] * 4,
        compiler_params=_params("parallel"),
    )(w, g, m, v, token)


def _adamw_small(packed_g, pre_g_parts, ws, ms, vs):
    names = ["pre_g"] + [n for n, _ in SMALL_ROWS if n != "conv_w"]
    rows = dict(SMALL_ROWS)
    offset, at = {}, 0
    for n, r in SMALL_ROWS:
        offset[n] = at
        at += r
    k = len(names)

    def body(*refs):
        g_ref, pg_ref = refs[0], refs[1]
        w_refs, m_refs, v_refs = refs[2:2 + k], refs[2 + k:2 + 2 * k], refs[2 + 2 * k:2 + 3 * k]
        outs = refs[2 + 3 * k:]
        go, do, mo, vo = outs[:k], outs[k:2 * k], outs[2 * k:3 * k], outs[3 * k:4 * k]
        pre = pg_ref[0]
        for dev in range(1, 8):
            pre = pre + pg_ref[dev]
        outs[4 * k][...] = pre[D_MODEL // LANES:, :]
        for i, n in enumerate(names):
            shp = w_refs[i].shape
            if len(shp) == 2 and shp[0] == 1:
                for r in range(shp[1] // LANES):
                    cols = slice(r * LANES, (r + 1) * LANES)
                    g = pre[r:r + 1, :] if n == "pre_g" else g_ref[offset[n] + r:offset[n] + r + 1, :]
                    go[i][:, cols] = g
                    do[i][:, cols], mo[i][:, cols], vo[i][:, cols] = _adam_update(
                        w_refs[i][:, cols], g, m_refs[i][:, cols], v_refs[i][:, cols])
            else:
                g = g_ref[offset[n]:offset[n] + rows[n], :].reshape(shp)
                go[i][...] = g
                do[i][...], mo[i][...], vo[i][...] = _adam_update(w_refs[i][...], g, m_refs[i][...], v_refs[i][...])

    vm = pl.BlockSpec(memory_space=pltpu.VMEM)
    args = [packed_g, pre_g_parts] + [src[n] for src in (ws, ms, vs) for n in names]
    out_shape = [jax.ShapeDtypeStruct(ws[n].shape, F32) for _ in range(4) for n in names]
    out_shape.append(jax.ShapeDtypeStruct((SUBLANES, LANES), F32))
    outs = pl.pallas_call(
        body, name="adamw_small", in_specs=[vm] * len(args), out_specs=[vm] * (4 * k + 1), out_shape=out_shape,
    )(*args)
    return [dict(zip(names, outs[q * k:(q + 1) * k])) for q in range(4)], outs[4 * k]


def _into_slot(v, tail, slot, lead, name):
    n = v.shape[1]
    nl = len(lead)
    rows = n // LANES + SUBLANES

    def body(slot_ref, v_ref, t_ref, o_ref):
        for r in range(n // LANES):
            o_ref[r:r + 1, :] = v_ref[0:1, r * LANES:(r + 1) * LANES]
        o_ref[n // LANES:, :] = t_ref[...]

    grid_spec = pltpu.PrefetchScalarGridSpec(
        num_scalar_prefetch=1, grid=(1,),
        in_specs=[pl.BlockSpec(v.shape, lambda i, sl: (0, 0)), pl.BlockSpec(tail.shape, lambda i, sl: (0, 0))],
        out_specs=pl.BlockSpec((None,) * nl + (rows, LANES), lambda i, sl: tuple(sl[q] for q in range(nl)) + (0, 0)))
    return pl.pallas_call(
        body, name=name, grid_spec=grid_spec, out_shape=jax.ShapeDtypeStruct(tuple(lead) + (rows, LANES), F32),
    )(slot, v, tail)


def _rows128(a):
    return a.reshape(-1, LANES)


def _pack_small(parts):
    pieces = [_rows128(parts[n]) for n, _ in SMALL_ROWS]
    pieces.append(jnp.zeros((SMALL_TOTAL - SMALL_USED, LANES), F32))
    return jnp.concatenate(pieces, axis=0)


def kernel(x, p, pre_g, w_in, gmlp_ln_g, gmlp_ln_b, gmlp_ws, gmlp_bs, conv_w, conv_b, w_a, b_a, w_x, b_x, lam, gmlp_out_g, lru_out_g, w_out, post_g, w_pe, w_pg, loss_target, m_pre_g, m_w_in, m_gmlp_ln_g, m_gmlp_ln_b, m_gmlp_ws, m_gmlp_bs, m_conv_w, m_conv_b, m_w_a, m_b_a, m_w_x, m_b_x, m_lam, m_gmlp_out_g, m_lru_out_g, m_w_out, m_post_g, m_w_pe, m_w_pg, v_pre_g, v_w_in, v_gmlp_ln_g, v_gmlp_ln_b, v_gmlp_ws, v_gmlp_bs, v_conv_w, v_conv_b, v_w_a, v_b_a, v_w_x, v_b_x, v_lam, v_gmlp_out_g, v_lru_out_g, v_w_out, v_post_g, v_w_pe, v_w_pg):
    weights = dict(pre_g=pre_g, w_in=w_in, gmlp_ln_g=gmlp_ln_g, gmlp_ln_b=gmlp_ln_b, gmlp_ws=gmlp_ws, gmlp_bs=gmlp_bs,
                   conv_w=conv_w, conv_b=conv_b, w_a=w_a, b_a=b_a, w_x=w_x, b_x=b_x, lam=lam, gmlp_out_g=gmlp_out_g,
                   lru_out_g=lru_out_g, w_out=w_out, post_g=post_g, w_pe=w_pe, w_pg=w_pg)
    mom_m = dict(pre_g=m_pre_g, w_in=m_w_in, gmlp_ln_g=m_gmlp_ln_g, gmlp_ln_b=m_gmlp_ln_b, gmlp_ws=m_gmlp_ws,
                 gmlp_bs=m_gmlp_bs, conv_w=m_conv_w, conv_b=m_conv_b, w_a=m_w_a, b_a=m_b_a, w_x=m_w_x, b_x=m_b_x,
                 lam=m_lam, gmlp_out_g=m_gmlp_out_g, lru_out_g=m_lru_out_g, w_out=m_w_out, post_g=m_post_g,
                 w_pe=m_w_pe, w_pg=m_w_pg)
    mom_v = dict(pre_g=v_pre_g, w_in=v_w_in, gmlp_ln_g=v_gmlp_ln_g, gmlp_ln_b=v_gmlp_ln_b, gmlp_ws=v_gmlp_ws,
                 gmlp_bs=v_gmlp_bs, conv_w=v_conv_w, conv_b=v_conv_b, w_a=v_w_a, b_a=v_b_a, w_x=v_w_x, b_x=v_b_x,
                 lam=v_lam, gmlp_out_g=v_gmlp_out_g, lru_out_g=v_lru_out_g, w_out=v_w_out, post_g=v_post_g,
                 w_pe=v_w_pe, w_pg=v_w_pg)
    order = list(weights)
    xi, yi, ci = _place()
    me = _chip_of(xi, yi)
    kc = jnp.stack([me, ci]).astype(jnp.int32)

    x2 = x[0]
    p2 = p[0, 0]
    tgt = loss_target[0]

    first = [_cast_into_slot(w_in[0], kc, "cast_w_in").reshape(N_CHIPS, 2, D_MODEL // 2, W_IN_COLS),
             _cast_into_slot(conv_w[0, :, 0, :], kc, "conv_w_into_slot", F32).reshape(N_CHIPS, 2, CONV_W // 2, CONV_COLS)]
    in_st, in_tok = _exchange_start("gather_in_start", first, 6, _gather_ici_copies(2))
    later = [_cast_into_slot(w_out[0], kc, "cast_w_out", token=in_tok).reshape(N_CHIPS, 2, W_ROWS // 2, D_MODEL),
             _cast_into_slot(w_pg[0], kc, "cast_w_pg", token=in_tok).reshape(N_CHIPS, 2, W_ROWS // 2, D_MODEL),
             _cast_into_slot(w_pe[0], kc, "cast_w_pe", token=in_tok).reshape(N_CHIPS, 2, D_PLE // 2, W_PE_COLS)]
    gather_st, gather_tok = _exchange_start("gather_start", later, 9, _gather_ici_copies(3), after=in_tok)
    hn, z_own, hn_t = _inproj_local(x2, pre_g, w_in[0], ROW_TILE, gather_tok)
    in_st, in_tok = _exchange_wait_start("gather_in_relay", in_st, z_own, _gather_ici_copies(2), 6,
                                         _gather_relay_copies(2))
    g_in, g_cw = _exchange_wait("gather_in_wait", in_st, in_tok, _gather_relay_copies(2))
    wg_in = g_in.reshape(N_CHIPS, D_MODEL, W_IN_COLS)
    cw_full = jnp.transpose(g_cw.reshape(N_CHIPS, CONV_W, CONV_COLS), (1, 0, 2)).reshape(CONV_W, D_HALF)

    causal = jnp.tril(jnp.ones((CHUNK, CHUNK), dtype=bool))
    ws_m = jnp.where(causal[None], gmlp_ws[0], 0.0)
    prm = dict(
        ln_g=gmlp_ln_g, ln_b=gmlp_ln_b, wt=ws_m.astype(BF16), wtt=jnp.transpose(ws_m, (0, 2, 1)).astype(BF16),
        bsx=jnp.repeat(jnp.transpose(gmlp_bs[0]), CHUNK, axis=1),
        conv_w=cw_full, conv_b=conv_b, w_a=w_a[0].astype(BF16), w_x=w_x[0].astype(BF16),
        b_a=b_a[0].reshape(1, D_HALF), b_x=b_x[0].reshape(1, D_HALF), lam=lam, oga=gmlp_out_g, ogb=lru_out_g)

    z, y, h = _inproj_branches_fwd(hn, z_own, wg_in, kc, prm, ROW_TILE, gather_tok)
    gather_st, gather_tok = _exchange_wait_start("gather_relay", gather_st, y, _gather_ici_copies(3), 9,
                                                 _gather_relay_copies(3))
    g_out, g_pg, g_pe = _exchange_wait("gather_wait", gather_st, gather_tok, _gather_relay_copies(3))
    wg_out = g_out.reshape(D_MODEL, D_MODEL)
    wg_pg = g_pg.reshape(D_MODEL, D_MODEL)
    wg_pe = g_pe.reshape(N_CHIPS, D_PLE, W_PE_COLS)
    o, h1, gt, dout, loss_acc = _outproj_fwd(x2, y, p2, tgt, post_g, wg_out, wg_pg, wg_pe, ROW_TILE)

    def sibling_start(tag, bufs):
        lands = [_landing((b.shape[0],) + b.shape[2:], b.dtype) for b in bufs]
        return _exchange_start("sibling_start_" + tag, bufs + lands, len(bufs), _sibling_copies(len(bufs)))

    def pair_then_chip_start(tag, started, after, names, tiles, dtypes):
        n = len(names)
        got = _exchange_wait("sibling_wait_" + tag, started, after, _sibling_copies(n))
        pairs = [_pair_sum(got[b], got[n + b], kc, "pair_sum_" + names[b], tiles[b], dtypes[b]) for b in range(n)]
        lands = [_landing((3,) + pr[0].shape[1:], pr[0].dtype) for pr in pairs]
        return _exchange_start("chip_start_" + tag, [pr[0] for pr in pairs] + lands, 3 * n, _chip_copies(n)), pairs

    def sum_then_finish_start(tag, started, pairs, after, names, tiles, small, to_all=()):
        n = len(names)
        got = _exchange_wait("chip_wait_" + tag, started, after, _chip_copies(n))
        sums = [_chip_sum(pairs[b][1], got[n + b], kc if small and b == n - 1 else kc[1:],
                          (N_CHIPS, 2) if small and b == n - 1 else (2,), "chip_sum_" + names[b], tiles[b])
                for b in range(n)]
        nbig = n - 1 if small else n
        n_all = n - nbig + len(to_all)
        return _exchange_start("finish_start_" + tag, sums + list(to_all), nbig + 7 * n_all,
                               _finish_copies(nbig, n_all))

    gw_pe, dq, dh1, do, dy, g_post = _head_bwd(dout, gt, p2, o, post_g, wg_out, wg_pg, wg_pe, ROW_TILE)
    gw_pe = gw_pe.reshape(N_CHIPS, 2, D_PLE // 2, W_PE_COLS)
    token0 = jnp.zeros((SUBLANES, LANES), F32)
    gw_out = _weight_grad(y, do, "grad_w_out", 2, 1, D_MODEL // 2, D_MODEL, CONTRACT_TILE, token0)
    gw_pg = _weight_grad(h1, dq, "grad_w_pg", 2, 1, D_MODEL // 2, D_MODEL, CONTRACT_TILE, token0)
    gw_out = gw_out.reshape(N_CHIPS, 2, W_ROWS // 2, D_MODEL)
    gw_pg = gw_pg.reshape(N_CHIPS, 2, W_ROWS // 2, D_MODEL)

    names_a, tiles_a = ["w_out", "w_pg", "w_pe"], [SUM_TILE] * 3
    st, tok = sibling_start("a", [gw_out, gw_pg, gw_pe])
    (dz, g_oga, g_ogb, g_lng, g_lnb, g_bsx, g_ws, g_cw, g_cb, g_wa, g_ba, g_wx, g_bx, g_lam) = _branches_bwd(
        z, h, dy, prm, ROW_TILE, tok)
    (st, tok), pairs_a = pair_then_chip_start("a", st, dz, names_a, tiles_a, [BF16] * 3)
    gw_in = _weight_grad(hn_t, dz, "grad_w_in", 2, N_CHIPS, D_MODEL // 2, W_IN_COLS, 2 * CONTRACT_TILE, tok,
                         a_transposed=True)
    fin_a, tok = sum_then_finish_start("a", st, pairs_a, gw_in, names_a, tiles_a, False)

    small_g = dict(
        gmlp_ln_g=g_lng[0:1], gmlp_ln_b=g_lnb[0:1], gmlp_ws=g_ws,
        gmlp_bs=jnp.transpose(g_bsx[:, ::CHUNK]), conv_w=g_cw[::SUBLANES], conv_b=g_cb[0:1], w_a=g_wa, b_a=g_ba[0:1],
        w_x=g_wx, b_x=g_bx[0:1], lam=g_lam[0:1], gmlp_out_g=g_oga[0:1], lru_out_g=g_ogb[0:1], post_g=g_post[0:1])
    gsm = _pack_small(small_g).reshape(N_CHIPS, 2, SMALL_PIECE, LANES)

    names_b, tiles_b = ["w_in", "small"], [2 * SUM_TILE, SMALL_PIECE]
    n_tiles = x2.shape[0] // ROW_TILE
    n_lo = max(1, (5 * n_tiles) // 16)
    st, tok_b = _exchange_start(
        "sibling_start_b", [gw_in, gsm] + [_landing((N_CHIPS,) + b.shape[2:], F32) for b in (gw_in, gsm)], 2,
        _sibling_copies(2), after=tok)
    part = _inproj_bwd(dz, wg_in, x2, dh1, pre_g, ROW_TILE, 0, n_lo, None, False, tok_b, "inproj_bwd_lo")
    f_out, f_pg, f_pe = _exchange_wait("finish_wait_a", fin_a, part[1], _finish_copies(3, 0))
    (st, tok_b), pairs_b = pair_then_chip_start("b", st, part[1], names_b, tiles_b, [BF16, F32])
    grad_x, g_pre = _inproj_bwd(dz, wg_in, x2, dh1, pre_g, ROW_TILE, n_lo, n_tiles - n_lo, part, True, tok_b,
                                "inproj_bwd_hi")
    pre_parts = _into_slot(g_pre, loss_acc, kc, (N_CHIPS, 2), "pre_g_into_slot")
    fin_b, tok_b = sum_then_finish_start("b", st, pairs_b, g_pre, names_b, tiles_b, True, to_all=[pre_parts])

    grads, deltas, new_m, new_v = {}, {}, {}, {}

    def adam_big(n, g2d, tr, token):
        shp = weights[n].shape
        g, d, nm, nv = _adamw(weights[n][0], g2d, mom_m[n][0], mom_v[n][0], "adamw_" + n, tr, token)
        grads[n], deltas[n], new_m[n], new_v[n] = g.reshape(shp), d.reshape(shp), nm.reshape(shp), nv.reshape(shp)
        return d

    as_token = lambda d: d[:SUBLANES, :LANES]
    last = adam_big("w_out", f_out.reshape(W_ROWS, D_MODEL), SUM_TILE, tok_b)
    last = adam_big("w_pg", f_pg.reshape(W_ROWS, D_MODEL), SUM_TILE, as_token(last))
    last = adam_big("w_pe", f_pe.reshape(D_PLE, W_PE_COLS), SUM_TILE, as_token(last))
    f_in, f_sm, pre_parts = _exchange_wait("finish_wait_b", fin_b, last, _finish_copies(1, 2))
    adam_big("w_in", f_in.reshape(D_MODEL, W_IN_COLS), 2 * SUM_TILE, tok_b)

    packed_g = f_sm.reshape(SMALL_TOTAL, LANES)
    small_names = ["pre_g"] + [n for n, _ in SMALL_ROWS if n != "conv_w"]
    natural = lambda src: {n: (src[n] if src[n].ndim == 2 else src[n][0]) for n in small_names}
    outs, loss_block = _adamw_small(packed_g, pre_parts.reshape(8, D_MODEL // LANES + SUBLANES, LANES),
                                    natural(weights), natural(mom_m), natural(mom_v))
    loss = loss_block[0, 0]
    for dst, got in zip((grads, deltas, new_m, new_v), outs):
        for n in small_names:
            dst[n] = got[n].reshape(weights[n].shape)
    at = sum(r for n, r in SMALL_ROWS[:[n for n, _ in SMALL_ROWS].index("conv_w")])
    g_cw_all = packed_g[at:at + CONV_W * D_HALF // LANES].reshape(CONV_W, D_HALF)
    g_conv = lax.dynamic_slice_in_dim(g_cw_all, me * CONV_COLS, CONV_COLS, axis=1)
    g, d, nm, nv = _adamw(conv_w[0, :, 0, :], g_conv, m_conv_w[0, :, 0, :], v_conv_w[0, :, 0, :], "adamw_conv_w", CONV_W,
                          tok_b)
    cshape = conv_w.shape
    grads["conv_w"], deltas["conv_w"] = g.reshape(cshape), d.reshape(cshape)
    new_m["conv_w"], new_v["conv_w"] = nm.reshape(cshape), nv.reshape(cshape)

    return (loss, grad_x.reshape(x.shape), *[grads[n] for n in order], *[deltas[n] for n in order],
            *[new_m[n] for n in order], *[new_v[n] for n in order])
```

```python
import math

import jax
import jax.numpy as jnp
from jax import lax
from jax.experimental import pallas as pl
from jax.experimental.pallas import tpu as pltpu

F32 = jnp.float32
BF16 = jnp.bfloat16

D_MODEL = 2048
D_HALF = 1024
D_Z = 5120
D_PLE = 256
CHUNK = 128
N_HEADS = 8
N_CHIPS = 4
W_IN_COLS = D_Z // N_CHIPS
W_ROWS = D_MODEL // N_CHIPS
W_PE_COLS = D_MODEL // N_CHIPS
CONV_W = 4
CONV_COLS = D_HALF // N_CHIPS
EPS = 1e-6
LRU_C = 8.0
ADAM_LR, ADAM_B1, ADAM_B2, ADAM_EPS, ADAM_WD, ADAM_STEP = 0.001, 0.9, 0.999, 1e-08, 0.01, 10

SUBLANES = 8
LANES = 128
VMEM_LIMIT = 56 * 1024 * 1024
ROW_TILE = 256
CONTRACT_TILE = 2048
SUM_TILE = 128

SMALL_ROWS = (("gmlp_ln_g", 8), ("gmlp_ln_b", 8), ("gmlp_ws", 1024), ("gmlp_bs", 8),
              ("conv_w", 32), ("conv_b", 8), ("w_a", 1024), ("b_a", 8), ("w_x", 1024), ("b_x", 8),
              ("lam", 8), ("gmlp_out_g", 8), ("lru_out_g", 8), ("post_g", 16))
SMALL_USED = sum(r for _, r in SMALL_ROWS)
SMALL_PIECE = 400
SMALL_TOTAL = 8 * SMALL_PIECE

MESH = pl.DeviceIdType.MESH
ANY = pl.BlockSpec(memory_space=pl.ANY)

_GELU_C0 = math.sqrt(2.0 / math.pi)
_GELU_C1 = 0.044715


def _params(*sem):
    return pltpu.CompilerParams(dimension_semantics=sem, vmem_limit_bytes=VMEM_LIMIT)


def _dot(a, b):
    return jnp.dot(a, b, preferred_element_type=F32)


def _dot_nt(a, b):
    return lax.dot_general(a, b, (((1,), (1,)), ((), ())), preferred_element_type=F32)


def _dot_tn(a, b):
    return lax.dot_general(a, b, (((0,), (0,)), ((), ())), preferred_element_type=F32)


def _gelu(x):
    t = jnp.tanh(_GELU_C0 * (x + _GELU_C1 * (x * x * x)))
    return 0.5 * x * (1.0 + t), t


def _gelu_grad(x, t):
    return 0.5 * (1.0 + t) + 0.5 * x * (1.0 - t * t) * (_GELU_C0 * (1.0 + 3.0 * _GELU_C1 * x * x))


def _rowsum8(v):
    r, n = v.shape
    return jnp.sum(v.reshape(r // SUBLANES, SUBLANES, n), axis=0)


def _lanemean(v):
    return jnp.mean(v, axis=-1, keepdims=True)


def _shift_down(v, halo8, k):
    if k == 0:
        return v
    r = pltpu.roll(v, k, 0)
    hr = pltpu.roll(halo8, k, 0)
    row = lax.broadcasted_iota(jnp.int32, halo8.shape, 0)
    top = jnp.where(row < k, hr, r[0:SUBLANES])
    return jnp.concatenate([top, r[SUBLANES:]], axis=0)


def _shift_up(v, next8, k):
    if k == 0:
        return v
    n = v.shape[0]
    r = pltpu.roll(v, n - k, 0)
    nr = pltpu.roll(next8, SUBLANES - k, 0)
    row = lax.broadcasted_iota(jnp.int32, next8.shape, 0)
    bot = jnp.where(row >= SUBLANES - k, nr, r[n - SUBLANES:])
    return jnp.concatenate([r[:n - SUBLANES], bot], axis=0)


def _layernorm_parts(vg):
    mu = _lanemean(vg)
    xc = vg - mu
    rstd = lax.rsqrt(_lanemean(xc * xc) + EPS)
    return xc * rstd, rstd


def _spatial_mix(wt_ref, vn_ref, bsx_ref, mixed_ref, tm):
    for c in range(tm // CHUNK):
        rows = slice(c * CHUNK, (c + 1) * CHUNK)
        for h in range(N_HEADS):
            cols = slice(h * CHUNK, (h + 1) * CHUNK)
            mixed_ref[rows, cols] = _dot(wt_ref[h], vn_ref[rows, cols]) + bsx_ref[:, cols]


def _conv_taps(xb, halo8):
    return [_shift_down(xb, halo8, CONV_W - 1 - k) for k in range(CONV_W)]


def _lru_gates(xc_bf_ref, wa_ref, wx_ref, ba_ref, bx_ref, r_ref, i_ref):
    for h in range(N_HEADS):
        cols = slice(h * CHUNK, (h + 1) * CHUNK)
        xh = xc_bf_ref[:, cols]
        r_ref[:, cols] = jax.nn.sigmoid(_dot(xh, wa_ref[h]) + ba_ref[:, cols])
        i_ref[:, cols] = jax.nn.sigmoid(_dot(xh, wx_ref[h]) + bx_ref[:, cols])


def _softplus_neg(lam):
    return jnp.maximum(-lam, 0.0) + jnp.log(1.0 + jnp.exp(-jnp.abs(lam)))


def _decay_parts(r, lam):
    la = (-LRU_C * _softplus_neg(lam)) * r
    a = jnp.exp(la)
    th = -jnp.tanh(la)
    mult = jnp.sqrt(2.0 * th / (1.0 + th))
    return a, mult


def _z_group(zref, g, rows=slice(None)):
    lo = g * D_HALF
    blk, off = lo // W_IN_COLS, lo % W_IN_COLS
    if off + D_HALF <= W_IN_COLS:
        return zref[blk, rows, off:off + D_HALF]
    return jnp.concatenate([zref[blk, rows, off:W_IN_COLS], zref[blk + 1, rows, 0:off + D_HALF - W_IN_COLS]], axis=1)


def _inproj_local(x, pre_g, w_own, tm, token):
    t = x.shape[0]

    def body(x_ref, g_ref, w_ref, token_ref, hn_ref, zl_ref, hnt_ref, wbf_s):
        @pl.when(pl.program_id(0) == 0)
        def _():
            wbf_s[...] = w_ref[...].astype(BF16)

        xv = x_ref[...]
        hnf = xv * lax.rsqrt(_lanemean(xv * xv) + EPS) * g_ref[...]
        hn = hnf.astype(BF16)
        hn_ref[...] = hn
        hnt_ref[...] = hnf.T.astype(BF16)
        zl_ref[...] = _dot(hn, wbf_s[...]).astype(BF16)

    row = lambda n: pl.BlockSpec((tm, n), lambda i: (i, 0))
    const = lambda shp: pl.BlockSpec(shp, lambda i: (0, 0), pipeline_mode=pl.Buffered(1))
    return pl.pallas_call(
        body, name="inproj_local", grid=(t // tm,),
        in_specs=[row(D_MODEL), const((1, D_MODEL)), const((D_MODEL, W_IN_COLS)), const((SUBLANES, LANES))],
        out_specs=[row(D_MODEL), row(W_IN_COLS), pl.BlockSpec((D_MODEL, tm), lambda i: (0, i))],
        out_shape=[jax.ShapeDtypeStruct((t, D_MODEL), BF16), jax.ShapeDtypeStruct((t, W_IN_COLS), BF16),
                   jax.ShapeDtypeStruct((D_MODEL, t), BF16)],
        scratch_shapes=[pltpu.VMEM((D_MODEL, W_IN_COLS), BF16)],
        compiler_params=_params("arbitrary"),
    )(x, pre_g, w_own, token)


def _inproj_branches_fwd(hn, z_own, wg_in, kc, prm, tm, token):
    t = hn.shape[0]
    nt = t // tm
    hb = tm // SUBLANES

    def body(kc_ref, hn_ref, zo_ref, w1_ref, w2_ref, w3_ref,
             lng_ref, lnb_ref, wt_ref, bsx_ref, cw_ref, cb_ref, wa_ref, wx_ref, ba_ref, bx_ref, lam_ref,
             oga_ref, ogb_ref, token_ref,
             z_ref, y_ref, h_ref,
             zbuf0, zbuf1, vn_s, mixed_s, xcbf_s, r_s, i_s, ug_s, halo_s, carry_s):
        s = pl.program_id(0)
        me = kc_ref[0]
        w_refs = (None, w1_ref, w2_ref, w3_ref)

        @pl.when(s == 0)
        def _():
            zbuf1[...] = jnp.zeros_like(zbuf1)

        @pl.when(s <= 1)
        def _():
            carry_s[...] = jnp.zeros_like(carry_s)
            halo_s[...] = jnp.zeros_like(halo_s)

        def step(zw, zr):
            def project(r):
                blk = (me + r) % N_CHIPS
                zb = zo_ref[...] if r == 0 else _dot(hn_ref[...], w_refs[r][...]).astype(BF16)
                z_ref[blk] = zb
                zw[blk] = zb

            zin = lambda g: _z_group(zr, g).astype(F32)
            always = [s >= 0] * 4

            @pl.when(always[0])
            def _():
                project(0)
                ug, _ = _gelu(zin(0))
                ug_s[...] = ug
                vg, _ = _gelu(zin(1))
                vhat, _ = _layernorm_parts(vg)
                vn_s[...] = (vhat * lng_ref[...] + lnb_ref[...]).astype(BF16)

            @pl.when(always[1])
            def _():
                project(1)
                _spatial_mix(wt_ref, vn_s, bsx_ref, mixed_s, tm)
                ga = zin(2)
                ya = ug_s[...] * mixed_s[...] * (ga * jax.nn.sigmoid(ga))
                ra = lax.rsqrt(_lanemean(ya * ya) + EPS)
                y_ref[:, 0:D_HALF] = (ya * ra * oga_ref[...]).astype(BF16)

            @pl.when(always[2])
            def _():
                project(2)
                xb = zin(3)
                taps = _conv_taps(xb, halo_s[...])
                halo_s[...] = xb[tm - SUBLANES:]
                xc = cb_ref[...] + taps[0] * cw_ref[0:1, :]
                for k in range(1, CONV_W):
                    xc = xc + taps[k] * cw_ref[k:k + 1, :]
                xcbf_s[...] = xc.astype(BF16)
                _lru_gates(xcbf_s, wa_ref, wx_ref, ba_ref, bx_ref, r_s, i_s)
                a, mult = _decay_parts(r_s[...], lam_ref[...])
                row = lax.broadcasted_iota(jnp.int32, a.shape, 0)
                mult = jnp.where(jnp.logical_and(s == 1, row == 0), 1.0, mult)
                r_s[...] = a
                i_s[...] = mult * (i_s[...] * xc)

            @pl.when(always[3])
            def _():
                project(3)
                a = r_s[...]
                b = i_s[...]
                r8 = lax.broadcasted_iota(jnp.int32, a.shape, 0) & (SUBLANES - 1)
                for d in (1, 2, 4):
                    a_sh = pltpu.roll(a, d, 0)
                    b_sh = pltpu.roll(b, d, 0)
                    m = r8 >= d
                    b = jnp.where(m, a * b_sh + b, b)
                    a = jnp.where(m, a * a_sh, a)
                carry = carry_s[...]
                for g in range(hb):
                    rows = slice(g * SUBLANES, (g + 1) * SUBLANES)
                    hg = a[rows] * carry + b[rows]
                    h_ref[rows, :] = hg
                    carry = jnp.broadcast_to(hg[SUBLANES - 1:SUBLANES, :], hg.shape)
                carry_s[...] = carry
                gb = zin(4)
                yb = h_ref[...] * (gb * jax.nn.sigmoid(gb))
                rb = lax.rsqrt(_lanemean(yb * yb) + EPS)
                y_ref[:, D_HALF:] = (yb * rb * ogb_ref[...]).astype(BF16)

        @pl.when(s % 2 == 0)
        def _():
            step(zbuf0, zbuf1)

        @pl.when(s % 2 == 1)
        def _():
            step(zbuf1, zbuf0)

    const = lambda a: pl.BlockSpec(a.shape, lambda s, kc, n=a.ndim: (0,) * n, pipeline_mode=pl.Buffered(1))
    proj = lambda n: pl.BlockSpec((tm, n), lambda s, kc: (jnp.minimum(s, nt - 1), 0))
    head = lambda n: pl.BlockSpec((tm, n), lambda s, kc: (jnp.maximum(s - 1, 0), 0))
    other = lambda r: pl.BlockSpec((None, D_MODEL, W_IN_COLS), lambda s, kc, r=r: ((kc[0] + r) % N_CHIPS, 0, 0),
                                   pipeline_mode=pl.Buffered(1))
    names = ("ln_g", "ln_b", "wt", "bsx", "conv_w", "conv_b", "w_a", "w_x", "b_a", "b_x", "lam", "oga", "ogb")
    pr = [prm[n] for n in names] + [token]
    big = lambda dt: pltpu.VMEM((tm, D_HALF), dt)
    zblocks = pltpu.VMEM((N_CHIPS, tm, W_IN_COLS), BF16)
    grid_spec = pltpu.PrefetchScalarGridSpec(
        num_scalar_prefetch=1, grid=(nt + 1,),
        in_specs=[proj(D_MODEL), proj(W_IN_COLS), other(1), other(2), other(3)] + [const(a) for a in pr],
        out_specs=[pl.BlockSpec((N_CHIPS, tm, W_IN_COLS), lambda s, kc: (0, jnp.minimum(s, nt - 1), 0)),
                   head(D_MODEL), head(D_HALF)],
        scratch_shapes=[zblocks, zblocks, big(BF16), big(F32), big(BF16), big(F32), big(F32), big(F32),
                        pltpu.VMEM((SUBLANES, D_HALF), F32), pltpu.VMEM((SUBLANES, D_HALF), F32)])
    return pl.pallas_call(
        body, name="inproj_branches_fwd", grid_spec=grid_spec,
        out_shape=[jax.ShapeDtypeStruct((N_CHIPS, t, W_IN_COLS), BF16), jax.ShapeDtypeStruct((t, D_MODEL), BF16),
                   jax.ShapeDtypeStruct((t, D_HALF), F32)],
        compiler_params=_params("arbitrary"),
    )(kc, hn, z_own, wg_in, wg_in, wg_in, *pr)


def _outproj_fwd(x, y, p, tgt, post_g, w_out, w_pg, wg_pe, tm):
    t = x.shape[0]

    def body(x_ref, y_ref, p_ref, tgt_ref, pg_ref, wo_ref, wpg_ref, wpe_ref,
             o_ref, h1_ref, gt_ref, dout_ref, loss_ref):
        @pl.when(pl.program_id(0) == 0)
        def _():
            loss_ref[...] = jnp.zeros_like(loss_ref)

        o = _dot(y_ref[...], wo_ref[...])
        o_ref[...] = o
        r3 = lax.rsqrt(_lanemean(o * o) + EPS)
        h1 = x_ref[...] + (o * r3) * pg_ref[...]
        h1b = h1.astype(BF16)
        h1_ref[...] = h1b
        gt = jax.nn.sigmoid(_dot(h1b, wpg_ref[...]))
        gt_ref[...] = gt
        pb = p_ref[...].astype(BF16)
        for k in range(N_CHIPS):
            cols = slice(k * W_PE_COLS, (k + 1) * W_PE_COLS)
            pe = _dot(pb, wpe_ref[k])
            d = h1[:, cols] + pe * gt[:, cols] - tgt_ref[:, cols]
            dout_ref[:, cols] = d * (1.0 / D_MODEL)
            loss_ref[...] += jnp.sum(d * d) * (0.5 / D_MODEL)

    row = lambda n: pl.BlockSpec((tm, n), lambda i: (i, 0))
    const = lambda shp: pl.BlockSpec(shp, lambda i, n=len(shp): (0,) * n, pipeline_mode=pl.Buffered(1))
    return pl.pallas_call(
        body, name="outproj_fwd", grid=(t // tm,),
        in_specs=[row(D_MODEL), row(D_MODEL), row(D_PLE), row(D_MODEL), const((1, D_MODEL)),
                  const((D_MODEL, D_MODEL)), const((D_MODEL, D_MODEL)), const((N_CHIPS, D_PLE, W_PE_COLS))],
        out_specs=[row(D_MODEL), row(D_MODEL), row(D_MODEL), row(D_MODEL),
                   pl.BlockSpec((SUBLANES, LANES), lambda i: (0, 0))],
        out_shape=[jax.ShapeDtypeStruct((t, D_MODEL), F32), jax.ShapeDtypeStruct((t, D_MODEL), BF16),
                   jax.ShapeDtypeStruct((t, D_MODEL), F32), jax.ShapeDtypeStruct((t, D_MODEL), F32),
                   jax.ShapeDtypeStruct((SUBLANES, LANES), F32)],
        compiler_params=_params("arbitrary"),
    )(x, y, p, tgt, post_g, w_out, w_pg, wg_pe)


def _head_bwd(dout, gt, p, o, post_g, w_out, w_pg, wg_pe, tm):
    t = dout.shape[0]

    def body(dout_ref, gt_ref, p_ref, o_ref, pg_ref, wo_ref, wpg_ref, wpe_ref,
             gwpe_ref, dq_ref, dh1_ref, do_ref, dy_ref, gpost_ref):
        i = pl.program_id(0)

        @pl.when(i == 0)
        def _():
            gpost_ref[...] = jnp.zeros_like(gpost_ref)
            gwpe_ref[...] = jnp.zeros_like(gwpe_ref)

        dout = dout_ref[...]
        gt = gt_ref[...]
        pb = p_ref[...].astype(BF16)
        for k in range(N_CHIPS):
            cols = slice(k * W_PE_COLS, (k + 1) * W_PE_COLS)
            pe = _dot(pb, wpe_ref[k])
            g = gt[:, cols]
            dg = dout[:, cols] * g
            gwpe_ref[k] += _dot_tn(pb, dg.astype(BF16))
            dq_ref[:, cols] = (dg * pe * (1.0 - g)).astype(BF16)
        dh1 = dout + _dot_nt(dq_ref[...], wpg_ref[...])
        dh1_ref[...] = dh1
        o = o_ref[...]
        r3 = lax.rsqrt(_lanemean(o * o) + EPS)
        on = o * r3
        gpost_ref[...] += _rowsum8(dh1 * on)
        don = dh1 * pg_ref[...]
        do = r3 * (don - on * _lanemean(don * on))
        dob = do.astype(BF16)
        do_ref[...] = dob
        dy_ref[...] = _dot_nt(dob, wo_ref[...])

        @pl.when(i == pl.num_programs(0) - 1)
        def _():
            gpost_ref[...] = jnp.broadcast_to(jnp.sum(gpost_ref[...], axis=0, keepdims=True), gpost_ref.shape)

    row = lambda n: pl.BlockSpec((tm, n), lambda i: (i, 0))
    const = lambda shp: pl.BlockSpec(shp, lambda i, n=len(shp): (0,) * n, pipeline_mode=pl.Buffered(1))
    return pl.pallas_call(
        body, name="head_bwd", grid=(t // tm,),
        in_specs=[row(D_MODEL), row(D_MODEL), row(D_PLE), row(D_MODEL), const((1, D_MODEL)),
                  const((D_MODEL, D_MODEL)), const((D_MODEL, D_MODEL)), const((N_CHIPS, D_PLE, W_PE_COLS))],
        out_specs=[pl.BlockSpec((N_CHIPS, D_PLE, W_PE_COLS), lambda i: (0, 0, 0)),
                   row(D_MODEL), row(D_MODEL), row(D_MODEL), row(D_MODEL),
                   pl.BlockSpec((SUBLANES, D_MODEL), lambda i: (0, 0))],
        out_shape=[jax.ShapeDtypeStruct((N_CHIPS, D_PLE, W_PE_COLS), F32), jax.ShapeDtypeStruct((t, D_MODEL), BF16),
                   jax.ShapeDtypeStruct((t, D_MODEL), F32), jax.ShapeDtypeStruct((t, D_MODEL), BF16),
                   jax.ShapeDtypeStruct((t, D_MODEL), F32), jax.ShapeDtypeStruct((SUBLANES, D_MODEL), F32)],
        compiler_params=_params("arbitrary"),
    )(dout, gt, p, o, post_g, w_out, w_pg, wg_pe)


def _branches_bwd(z, h, dy, prm, tm, token):
    t = h.shape[0]
    nt = t // tm
    hb = tm // SUBLANES

    def body(z_ref, zh_ref, h_ref, hh_ref, dy_ref,
             lng_ref, lnb_ref, wt_ref, wtt_ref, bsx_ref, cw_ref, cb_ref, wa_ref, wx_ref, ba_ref, bx_ref, lam_ref,
             oga_ref, ogb_ref, token_ref,
             dz_ref, g_oga, g_ogb, g_lng, g_lnb, g_bsx, g_ws, g_cw, g_cb, g_wa, g_ba, g_wx, g_bx, g_lam,
             vn_s, mixed_s, dm_s, dvn_s, xcbf_s, r_s, i_s, a_s, b_s, dh_s, dpr_s, dpi_s, dxc_s,
             ca_s, cd_s, cx_s):
        step_i = pl.program_id(0)
        tile = nt - 1 - step_i
        accs = (g_oga, g_ogb, g_lng, g_lnb, g_bsx, g_ws, g_cw, g_cb, g_wa, g_ba, g_wx, g_bx, g_lam)

        @pl.when(step_i == 0)
        def _():
            for r in accs + (ca_s, cd_s, cx_s):
                r[...] = jnp.zeros_like(r)

        dy_a = dy_ref[:, 0:D_HALF]
        dy_b = dy_ref[:, D_HALF:]

        u = _z_group(z_ref, 0).astype(F32)
        ug, tu = _gelu(u)
        v = _z_group(z_ref, 1).astype(F32)
        vg, tv = _gelu(v)
        vhat, rstd = _layernorm_parts(vg)
        vn_s[...] = (vhat * lng_ref[...] + lnb_ref[...]).astype(BF16)
        _spatial_mix(wt_ref, vn_s, bsx_ref, mixed_s, tm)
        mixed = mixed_s[...]
        ga = _z_group(z_ref, 2).astype(F32)
        sga = jax.nn.sigmoid(ga)
        sa = ga * sga
        um = ug * mixed
        ya = um * sa
        ra = lax.rsqrt(_lanemean(ya * ya) + EPS)
        yahat = ya * ra
        g_oga[...] += _rowsum8(dy_a * yahat)
        dn = dy_a * oga_ref[...]
        dya = ra * (dn - yahat * _lanemean(dn * yahat))
        dz_ref[:, 2 * D_HALF:3 * D_HALF] = (dya * um * (sga * (1.0 + ga * (1.0 - sga)))).astype(BF16)
        dz_ref[:, 0:D_HALF] = (dya * mixed * sa * _gelu_grad(u, tu)).astype(BF16)
        dmixed = dya * ug * sa
        g_bsx[...] += jnp.sum(dmixed.reshape(tm // CHUNK, CHUNK, D_HALF), axis=0)
        dm_s[...] = dmixed.astype(BF16)
        for c in range(tm // CHUNK):
            rows = slice(c * CHUNK, (c + 1) * CHUNK)
            for hd in range(N_HEADS):
                cols = slice(hd * CHUNK, (hd + 1) * CHUNK)
                dmh = dm_s[rows, cols]
                dvn_s[rows, cols] = _dot(wtt_ref[hd], dmh)
                g_ws[hd] += _dot_nt(dmh, vn_s[rows, cols])
        dvn = dvn_s[...]
        g_lng[...] += _rowsum8(dvn * vhat)
        g_lnb[...] += _rowsum8(dvn)
        dvh = dvn * lng_ref[...]
        dvg = rstd * (dvh - _lanemean(dvh) - vhat * _lanemean(dvh * vhat))
        dz_ref[:, D_HALF:2 * D_HALF] = (dvg * _gelu_grad(v, tv)).astype(BF16)

        xb = _z_group(z_ref, 3).astype(F32)
        halo = jnp.where(tile == 0, 0.0, _z_group(zh_ref, 3).astype(F32)[SUBLANES:])
        taps = _conv_taps(xb, halo)
        xc = cb_ref[...] + taps[0] * cw_ref[0:1, :]
        for k in range(1, CONV_W):
            xc = xc + taps[k] * cw_ref[k:k + 1, :]
        xcbf_s[...] = xc.astype(BF16)
        _lru_gates(xcbf_s, wa_ref, wx_ref, ba_ref, bx_ref, r_s, i_s)
        rg = r_s[...]
        ig = i_s[...]
        lam = lam_ref[...]
        a, mult_true = _decay_parts(rg, lam)
        row = lax.broadcasted_iota(jnp.int32, a.shape, 0)
        first = jnp.logical_and(tile == 0, row == 0)
        mult = jnp.where(first, 1.0, mult_true)
        hcur = h_ref[...]
        hprev = _shift_down(hcur, jnp.where(tile == 0, 0.0, hh_ref[...]), 1)
        gb = _z_group(z_ref, 4).astype(F32)
        sgb = jax.nn.sigmoid(gb)
        sb = gb * sgb
        yb = hcur * sb
        rb = lax.rsqrt(_lanemean(yb * yb) + EPS)
        ybhat = yb * rb
        g_ogb[...] += _rowsum8(dy_b * ybhat)
        dn = dy_b * ogb_ref[...]
        dyb = rb * (dn - ybhat * _lanemean(dn * ybhat))
        dz_ref[:, 4 * D_HALF:5 * D_HALF] = (dyb * hcur * (sgb * (1.0 + gb * (1.0 - sgb)))).astype(BF16)

        an = _shift_up(a, ca_s[...], 1)
        bb = dyb * sb
        r8 = row & (SUBLANES - 1)
        for d in (1, 2, 4):
            a_sh = pltpu.roll(an, tm - d, 0)
            b_sh = pltpu.roll(bb, tm - d, 0)
            m = r8 + d < SUBLANES
            bb = jnp.where(m, an * b_sh + bb, bb)
            an = jnp.where(m, an * a_sh, an)
        a_s[...] = an
        b_s[...] = bb

        def step(g, carry):
            sl = pl.ds(pl.multiple_of((hb - 1 - g) * SUBLANES, SUBLANES), SUBLANES)
            dg = a_s[sl, :] * carry + b_s[sl, :]
            dh_s[sl, :] = dg
            return jnp.broadcast_to(dg[0:1, :], dg.shape)

        cd_s[...] = lax.fori_loop(0, hb, step, cd_s[...])
        ca_s[...] = jnp.broadcast_to(a[0:1, :], ca_s.shape)
        dh = dh_s[...]
        da = dh * hprev
        gx = ig * xc
        dla = da * a - jnp.where(first, 0.0, dh * gx * (a * a / mult_true))
        g_lam[...] += _rowsum8(dla * rg)
        dr = dla * (-LRU_C * _softplus_neg(lam))
        dpr = dr * rg * (1.0 - rg)
        dpi = (dh * mult * xc) * ig * (1.0 - ig)
        g_ba[...] += _rowsum8(dpr)
        g_bx[...] += _rowsum8(dpi)
        dpr_s[...] = dpr.astype(BF16)
        dpi_s[...] = dpi.astype(BF16)
        for hd in range(N_HEADS):
            cols = slice(hd * CHUNK, (hd + 1) * CHUNK)
            xh = xcbf_s[:, cols]
            dprh = dpr_s[:, cols]
            dpih = dpi_s[:, cols]
            g_wa[hd] += _dot_tn(xh, dprh)
            g_wx[hd] += _dot_tn(xh, dpih)
            dxc_s[:, cols] = _dot_nt(dprh, wa_ref[hd]) + _dot_nt(dpih, wx_ref[hd])
        dxc = dxc_s[...] + dh * mult * ig
        g_cb[...] += _rowsum8(dxc)
        for k in range(CONV_W):
            g_cw[k * SUBLANES:(k + 1) * SUBLANES, :] += _rowsum8(dxc * taps[k])
        nxt = cx_s[...]
        dxb = dxc * cw_ref[CONV_W - 1:CONV_W, :]
        for j in range(1, CONV_W):
            dxb = dxb + _shift_up(dxc, nxt, j) * cw_ref[CONV_W - 1 - j:CONV_W - j, :]
        dz_ref[:, 3 * D_HALF:4 * D_HALF] = dxb.astype(BF16)
        cx_s[...] = dxc[0:SUBLANES]

        @pl.when(step_i == nt - 1)
        def _():
            for r in (g_oga, g_ogb, g_lng, g_lnb, g_cb, g_ba, g_bx):
                r[...] = jnp.broadcast_to(jnp.sum(r[...], axis=0, keepdims=True), r.shape)
            lam_f = LRU_C * jax.nn.sigmoid(-lam_ref[...])
            g_lam[...] = jnp.broadcast_to(jnp.sum(g_lam[...], axis=0, keepdims=True) * lam_f, g_lam.shape)
            for k in range(CONV_W):
                blk = g_cw[k * SUBLANES:(k + 1) * SUBLANES, :]
                g_cw[k * SUBLANES:(k + 1) * SUBLANES, :] = jnp.broadcast_to(jnp.sum(blk, axis=0, keepdims=True), blk.shape)
            tri = (lax.broadcasted_iota(jnp.int32, (CHUNK, CHUNK), 0) >= lax.broadcasted_iota(jnp.int32, (CHUNK, CHUNK), 1))
            for hd in range(N_HEADS):
                cols = slice(hd * CHUNK, (hd + 1) * CHUNK)
                g_ws[hd] = jnp.where(tri, g_ws[hd], 0.0)
                blk = g_bsx[:, cols]
                g_bsx[:, cols] = jnp.broadcast_to(jnp.sum(blk, axis=1, keepdims=True), blk.shape)

    rev = lambda i: nt - 1 - i
    zspec = pl.BlockSpec((N_CHIPS, tm, W_IN_COLS), lambda i: (0, rev(i), 0))
    halo = lambda col: pl.BlockSpec((SUBLANES, D_HALF), lambda i: (jnp.maximum(rev(i) * hb - 1, 0), col))
    zhalo = pl.BlockSpec((N_CHIPS, 2 * SUBLANES, W_IN_COLS), lambda i: (0, jnp.maximum(rev(i) * (hb // 2) - 1, 0), 0))
    full = lambda a: pl.BlockSpec(a.shape, lambda i, n=a.ndim: (0,) * n)
    acc = lambda shp: pl.BlockSpec(shp, lambda i, n=len(shp): (0,) * n)
    names = ("ln_g", "ln_b", "wt", "wtt", "bsx", "conv_w", "conv_b", "w_a", "w_x", "b_a", "b_x", "lam", "oga", "ogb")
    pr = [prm[n] for n in names] + [token]
    vec = (SUBLANES, D_HALF)
    mat = (N_HEADS, CHUNK, CHUNK)
    acc_shapes = [vec, vec, vec, vec, (CHUNK, D_HALF), mat, (CONV_W * SUBLANES, D_HALF), vec, mat, vec, mat, vec, vec]
    big = lambda dt: pltpu.VMEM((tm, D_HALF), dt)
    return pl.pallas_call(
        body, name="branches_bwd", grid=(nt,),
        in_specs=[zspec, zhalo,
                  pl.BlockSpec((tm, D_HALF), lambda i: (rev(i), 0)), halo(0),
                  pl.BlockSpec((tm, D_MODEL), lambda i: (rev(i), 0))] + [full(a) for a in pr],
        out_specs=[pl.BlockSpec((tm, D_Z), lambda i: (rev(i), 0))] + [acc(s) for s in acc_shapes],
        out_shape=[jax.ShapeDtypeStruct((t, D_Z), BF16)] + [jax.ShapeDtypeStruct(s, F32) for s in acc_shapes],
        scratch_shapes=[big(BF16), big(F32), big(BF16), big(F32), big(BF16), big(F32), big(F32), big(F32), big(F32),
                        big(F32), big(BF16), big(BF16), big(F32),
                        pltpu.VMEM(vec, F32), pltpu.VMEM(vec, F32), pltpu.VMEM(vec, F32)],
        compiler_params=_params("arbitrary"),
    )(z, z, h, h, dy, *pr)


def _inproj_bwd(dz, wg_in, x, dh1, pre_g, tm, tile0, nt, prev, last, token, name):
    t = x.shape[0]

    def body(*refs):
        dz_ref, w_ref, x_ref, dh1_ref, g_ref = refs[:5]
        gx_ref, gpre_ref, acc_s = refs[-3:]
        i = pl.program_id(0)

        @pl.when(i == 0)
        def _():
            gpre_ref[...] = jnp.zeros_like(gpre_ref) if prev is None else refs[7][...]

        acc = _dot_nt(dz_ref[:, 0:W_IN_COLS], w_ref[0])
        for k in range(1, N_CHIPS):
            acc = acc + _dot_nt(dz_ref[:, k * W_IN_COLS:(k + 1) * W_IN_COLS], w_ref[k])
        acc_s[...] = acc
        for s in range(tm // CHUNK):
            rows = slice(s * CHUNK, (s + 1) * CHUNK)
            xv = x_ref[rows, :]
            r = lax.rsqrt(_lanemean(xv * xv) + EPS)
            xhat = xv * r
            dhn = acc_s[rows, :]
            gpre_ref[...] += _rowsum8(dhn * xhat)
            dxh = dhn * g_ref[...]
            gx_ref[rows, :] = dh1_ref[rows, :] + r * (dxh - xhat * _lanemean(dxh * xhat))

        if last:
            @pl.when(i == nt - 1)
            def _():
                gpre_ref[...] = jnp.broadcast_to(jnp.sum(gpre_ref[...], axis=0, keepdims=True), gpre_ref.shape)

    row = lambda n: pl.BlockSpec((tm, n), lambda i: (tile0 + i, 0))
    small = lambda r: pl.BlockSpec((r, D_MODEL), lambda i: (0, 0))
    tok = pl.BlockSpec((SUBLANES, LANES), lambda i: (0, 0))
    in_specs = [row(D_Z), pl.BlockSpec(wg_in.shape, lambda i: (0, 0, 0), pipeline_mode=pl.Buffered(1)),
                row(D_MODEL), row(D_MODEL), small(1), tok]
    args = [dz, wg_in, x, dh1, pre_g, token]
    aliases = {}
    if prev is not None:
        in_specs += [ANY, small(SUBLANES)]
        args += list(prev)
        aliases = {6: 0}
    return pl.pallas_call(
        body, name=name, grid=(nt,), in_specs=in_specs, out_specs=[row(D_MODEL), small(SUBLANES)],
        out_shape=[jax.ShapeDtypeStruct((t, D_MODEL), F32), jax.ShapeDtypeStruct((SUBLANES, D_MODEL), F32)],
        input_output_aliases=aliases,
        scratch_shapes=[pltpu.VMEM((tm, D_MODEL), F32)],
        compiler_params=_params("arbitrary"),
    )(*args)


def _weight_grad(a, b, name, kb, nb, tk, tn, tt, token, a_transposed=False):
    t = b.shape[0]
    tt = min(tt, t)

    def body(a_ref, b_ref, token_ref, o_ref):
        @pl.when(pl.program_id(2) == 0)
        def _():
            o_ref[...] = jnp.zeros_like(o_ref)

        o_ref[...] += (_dot if a_transposed else _dot_tn)(a_ref[...], b_ref[...])

    a_spec = (pl.BlockSpec((tk, tt), lambda j, i, s: (i, s)) if a_transposed
              else pl.BlockSpec((tt, tk), lambda j, i, s: (s, i)))
    return pl.pallas_call(
        body, name=name, grid=(nb, kb, t // tt),
        in_specs=[a_spec, pl.BlockSpec((tt, tn), lambda j, i, s: (s, j)),
                  pl.BlockSpec((SUBLANES, LANES), lambda j, i, s: (0, 0))],
        out_specs=pl.BlockSpec((None, None, tk, tn), lambda j, i, s: (j, i, 0, 0)),
        out_shape=jax.ShapeDtypeStruct((nb, kb, tk, tn), F32),
        compiler_params=_params("parallel", "parallel", "arbitrary"),
    )(a, b, token)


def _place():
    x, y, c = lax.axis_index("x"), lax.axis_index("y"), lax.axis_index("c")
    return x, y, c


def _chip_of(x, y):
    return 2 * x + y


HBM = pl.BlockSpec(memory_space=pltpu.HBM)
SEM = pl.BlockSpec(memory_space=pltpu.SEMAPHORE)
EFFECT = pltpu.SideEffectType.DATAFLOW_SIDE_EFFECTING


def _hbm(a):
    return pltpu.with_memory_space_constraint(a, pltpu.HBM)


def _landing(shape, dtype):
    return _hbm(lax.empty(shape, dtype))


def _exchange_start(name, arrays, ncopies, build, after=None):
    n = len(arrays)
    extra = [] if after is None else [after]

    def body(*refs):
        ins, token = refs[:n], refs[-1]
        send_sems, recv_sems = refs[n + len(extra)], refs[n + len(extra) + 1]
        for cp in build(ins, send_sems, recv_sems):
            cp.start()
        token[...] = jnp.zeros_like(token)

    outs = pl.pallas_call(
        body, name=name,
        out_shape=(pltpu.SemaphoreType.DMA((ncopies,)), pltpu.SemaphoreType.DMA((ncopies,)),
                   *[pltpu.HBM(a.shape, a.dtype) for a in arrays], jax.ShapeDtypeStruct((SUBLANES, LANES), F32)),
        in_specs=[HBM] * n + [ANY] * len(extra),
        out_specs=(SEM, SEM, *[HBM] * n, pl.BlockSpec(memory_space=pltpu.VMEM)),
        input_output_aliases={q: q + 2 for q in range(n)},
        compiler_params=pltpu.CompilerParams(has_side_effects=EFFECT),
    )(*[_hbm(a) for a in arrays], *extra)
    return (outs[0], outs[1], list(outs[2:2 + n])), outs[-1]


def _exchange_wait(name, started, after, build):
    send, recv, arrays = started
    n = len(arrays)

    def body(*refs):
        ins, send_sems, recv_sems = refs[:n], refs[n], refs[n + 1]
        for cp in build(ins, send_sems, recv_sems):
            cp.wait_send()
            cp.wait_recv()

    return pl.pallas_call(
        body, name=name, out_shape=tuple(pltpu.HBM(a.shape, a.dtype) for a in arrays),
        in_specs=[HBM] * n + [SEM, SEM, ANY], out_specs=tuple([HBM] * n),
        input_output_aliases={q: q for q in range(n)},
        compiler_params=pltpu.CompilerParams(has_side_effects=EFFECT),
    )(*arrays, send, recv, after)


def _exchange_wait_start(name, started, after, build_wait, ncopies, build_start):
    send, recv, arrays = started
    n = len(arrays)

    def body(*refs):
        ins, send_sems, recv_sems = refs[:n], refs[n], refs[n + 1]
        send2, recv2, token = refs[n + 3], refs[n + 4], refs[-1]
        arrived = build_wait(ins, send_sems, recv_sems)
        for cp, onward in zip(arrived, build_start(ins, send2, recv2)):
            cp.wait_recv()
            onward.start()
        for cp in arrived:
            cp.wait_send()
        token[...] = jnp.zeros_like(token)

    outs = pl.pallas_call(
        body, name=name,
        out_shape=(pltpu.SemaphoreType.DMA((ncopies,)), pltpu.SemaphoreType.DMA((ncopies,)),
                   *[pltpu.HBM(a.shape, a.dtype) for a in arrays], jax.ShapeDtypeStruct((SUBLANES, LANES), F32)),
        in_specs=[HBM] * n + [SEM, SEM, ANY], out_specs=(SEM, SEM, *[HBM] * n, pl.BlockSpec(memory_space=pltpu.VMEM)),
        input_output_aliases={q: q + 2 for q in range(n)},
        compiler_params=pltpu.CompilerParams(has_side_effects=EFFECT),
    )(*arrays, send, recv, after)
    return (outs[0], outs[1], list(outs[2:2 + n])), outs[-1]


def _cast_into_slot(w, kc, name, dtype=BF16, token=None):
    rows, cols = w.shape
    tr = min(rows, 256)
    extra = [] if token is None else [token]

    def body(kc_ref, w_ref, *rest):
        rest[-1][...] = w_ref[...].astype(dtype)

    grid_spec = pltpu.PrefetchScalarGridSpec(
        num_scalar_prefetch=1, grid=(rows // tr,),
        in_specs=[pl.BlockSpec((tr, cols), lambda r, kc: (r, 0))]
                 + [pl.BlockSpec((SUBLANES, LANES), lambda r, kc: (0, 0))] * len(extra),
        out_specs=pl.BlockSpec((None, tr, cols), lambda r, kc: (kc[0], r, 0)))
    return pl.pallas_call(
        body, name=name, grid_spec=grid_spec, out_shape=jax.ShapeDtypeStruct((N_CHIPS, rows, cols), dtype),
        compiler_params=_params("arbitrary"),
    )(kc, w, *extra)


def _gather_ici_copies(n):
    def build(refs, send_sems, recv_sems):
        x, y, c = _place()
        mine = lambda b: refs[b].at[_chip_of(x, y), c]
        chips = [(1 - x, y), (x, 1 - y), (1 - x, 1 - y)]
        return [pltpu.make_async_remote_copy(
            src_ref=mine(b), dst_ref=mine(b), send_sem=send_sems.at[3 * b + j], recv_sem=recv_sems.at[3 * b + j],
            device_id=(*chip, c), device_id_type=MESH) for b in range(n) for j, chip in enumerate(chips)]
    return build


def _gather_direct_copies(n):
    def build(refs, send_sems, recv_sems):
        x, y, c = _place()
        mine = lambda b: refs[b].at[_chip_of(x, y)]
        chips = [(1 - x, y), (x, 1 - y), (1 - x, 1 - y)]
        return [pltpu.make_async_remote_copy(
            src_ref=mine(b), dst_ref=mine(b), send_sem=send_sems.at[3 * b + j], recv_sem=recv_sems.at[3 * b + j],
            device_id=(*chip, c), device_id_type=MESH) for b in range(n) for j, chip in enumerate(chips)]
    return build


def _gather_relay_copies(n):
    def build(refs, send_sems, recv_sems):
        x, y, c = _place()
        chips = [(1 - x, y), (x, 1 - y), (1 - x, 1 - y)]
        cps = []
        for b in range(n):
            for j, chip in enumerate(chips):
                got = refs[b].at[_chip_of(*chip), c]
                cps.append(pltpu.make_async_remote_copy(
                    src_ref=got, dst_ref=got, send_sem=send_sems.at[3 * b + j], recv_sem=recv_sems.at[3 * b + j],
                    device_id=(x, y, 1 - c), device_id_type=MESH))
        return cps
    return build


def _sibling_copies(n):
    def build(refs, send_sems, recv_sems):
        x, y, c = _place()
        return [pltpu.make_async_remote_copy(
            src_ref=refs[b].at[:, 1 - c], dst_ref=refs[n + b], send_sem=send_sems.at[b], recv_sem=recv_sems.at[b],
            device_id=(x, y, 1 - c), device_id_type=MESH) for b in range(n)]
    return build


def _chip_copies(n):
    def build(refs, send_sems, recv_sems):
        x, y, c = _place()
        chips = [(1 - x, y), (x, 1 - y), (1 - x, 1 - y)]
        return [pltpu.make_async_remote_copy(
            src_ref=refs[b].at[_chip_of(*chip)], dst_ref=refs[n + b].at[j],
            send_sem=send_sems.at[3 * b + j], recv_sem=recv_sems.at[3 * b + j],
            device_id=(*chip, c), device_id_type=MESH) for b in range(n) for j, chip in enumerate(chips)]
    return build


def _finish_copies(n, n_all):
    def build(refs, send_sems, recv_sems):
        x, y, c = _place()
        cps = [pltpu.make_async_remote_copy(
            src_ref=refs[b].at[c], dst_ref=refs[b].at[c], send_sem=send_sems.at[b], recv_sem=recv_sems.at[b],
            device_id=(x, y, 1 - c), device_id_type=MESH) for b in range(n)]
        flips = [(fx, fy, fc) for fx in (0, 1) for fy in (0, 1) for fc in (0, 1)][1:]
        for b in range(n_all):
            mine = refs[n + b].at[_chip_of(x, y), c]
            cps += [pltpu.make_async_remote_copy(
                src_ref=mine, dst_ref=mine, send_sem=send_sems.at[n + 7 * b + q], recv_sem=recv_sems.at[n + 7 * b + q],
                device_id=(x ^ fx, y ^ fy, c ^ fc), device_id_type=MESH) for q, (fx, fy, fc) in enumerate(flips)]
        return cps
    return build


def _pair_sum(g, r1, kc, name, tr, send_dtype):
    nk, _, rows, cols = g.shape

    def body(kc_ref, g_ref, r_ref, p_ref, own_ref):
        s = g_ref[...] + r_ref[...]
        p_ref[...] = s.astype(send_dtype)

        @pl.when(pl.program_id(1) == kc_ref[0])
        def _():
            own_ref[...] = s

    grid_spec = pltpu.PrefetchScalarGridSpec(
        num_scalar_prefetch=1, grid=(rows // tr, nk),
        in_specs=[pl.BlockSpec((None, None, tr, cols), lambda r, k, kc: (k, kc[1], r, 0)),
                  pl.BlockSpec((None, tr, cols), lambda r, k, kc: (k, r, 0))],
        out_specs=[pl.BlockSpec((None, tr, cols), lambda r, k, kc: (k, r, 0)),
                   pl.BlockSpec((tr, cols), lambda r, k, kc: (r, 0))])
    return pl.pallas_call(
        body, name=name, grid_spec=grid_spec,
        out_shape=[jax.ShapeDtypeStruct((nk, rows, cols), send_dtype), jax.ShapeDtypeStruct((rows, cols), F32)],
        compiler_params=_params("arbitrary", "arbitrary"),
    )(kc, g, r1)


def _chip_sum(own, r2, slot, lead, name, tr):
    rows, cols = own.shape
    nl = len(lead)

    def body(slot_ref, o_ref, r_ref, s_ref):
        s = o_ref[...]
        for j in range(3):
            s = s + r_ref[j].astype(F32)
        s_ref[...] = s

    grid_spec = pltpu.PrefetchScalarGridSpec(
        num_scalar_prefetch=1, grid=(rows // tr,),
        in_specs=[pl.BlockSpec((tr, cols), lambda r, sl: (r, 0)), pl.BlockSpec((3, tr, cols), lambda r, sl: (0, r, 0))],
        out_specs=pl.BlockSpec((None,) * nl + (tr, cols), lambda r, sl: tuple(sl[q] for q in range(nl)) + (r, 0)))
    return pl.pallas_call(
        body, name=name, grid_spec=grid_spec, out_shape=jax.ShapeDtypeStruct(tuple(lead) + (rows, cols), F32),
        compiler_params=_params("arbitrary"),
    )(slot, own, r2)


def _adam_update(w, g, m, v):
    nm = ADAM_B1 * m + (1.0 - ADAM_B1) * g
    nv = ADAM_B2 * v + (1.0 - ADAM_B2) * (g * g)
    m_hat = nm / (1.0 - ADAM_B1 ** ADAM_STEP)
    v_hat = nv / (1.0 - ADAM_B2 ** ADAM_STEP)
    return -ADAM_LR * (m_hat / (jnp.sqrt(v_hat) + ADAM_EPS) + ADAM_WD * w), nm, nv


def _adamw(w, g, m, v, name, tr, token):
    rows, cols = w.shape

    def body(w_ref, g_ref, m_ref, v_ref, token_ref, go_ref, d_ref, nm_ref, nv_ref):
        gv = g_ref[...]
        go_ref[...] = gv
        d_ref[...], nm_ref[...], nv_ref[...] = _adam_update(w_ref[...], gv, m_ref[...], v_ref[...])

    spec = pl.BlockSpec((tr, cols), lambda r: (r, 0))
    return pl.pallas_call(
        body, name=name, grid=(rows // tr,),
        in_specs=[spec] * 4 + [pl.BlockSpec((SUBLANES, LANES), lambda r: (0, 0))], out_specs=[spec] * 4,
        out_shape=[jax.ShapeDtypeStruct((rows, cols), F32)] * 4,
        compiler_params=_params("parallel"),
    )(w, g, m, v, token)


def _adamw_small(packed_g, pre_g_parts, ws, ms, vs):
    names = ["pre_g"] + [n for n, _ in SMALL_ROWS if n != "conv_w"]
    rows = dict(SMALL_ROWS)
    offset, at = {}, 0
    for n, r in SMALL_ROWS:
        offset[n] = at
        at += r
    k = len(names)

    def body(*refs):
        g_ref, pg_ref = refs[0], refs[1]
        w_refs, m_refs, v_refs = refs[2:2 + k], refs[2 + k:2 + 2 * k], refs[2 + 2 * k:2 + 3 * k]
        outs = refs[2 + 3 * k:]
        go, do, mo, vo = outs[:k], outs[k:2 * k], outs[2 * k:3 * k], outs[3 * k:4 * k]
        pre = pg_ref[0]
        for dev in range(1, 8):
            pre = pre + pg_ref[dev]
        outs[4 * k][...] = pre[D_MODEL // LANES:, :]
        for i, n in enumerate(names):
            shp = w_refs[i].shape
            if len(shp) == 2 and shp[0] == 1:
                for r in range(shp[1] // LANES):
                    cols = slice(r * LANES, (r + 1) * LANES)
                    g = pre[r:r + 1, :] if n == "pre_g" else g_ref[offset[n] + r:offset[n] + r + 1, :]
                    go[i][:, cols] = g
                    do[i][:, cols], mo[i][:, cols], vo[i][:, cols] = _adam_update(
                        w_refs[i][:, cols], g, m_refs[i][:, cols], v_refs[i][:, cols])
            else:
                g = g_ref[offset[n]:offset[n] + rows[n], :].reshape(shp)
                go[i][...] = g
                do[i][...], mo[i][...], vo[i][...] = _adam_update(w_refs[i][...], g, m_refs[i][...], v_refs[i][...])

    vm = pl.BlockSpec(memory_space=pltpu.VMEM)
    args = [packed_g, pre_g_parts] + [src[n] for src in (ws, ms, vs) for n in names]
    out_shape = [jax.ShapeDtypeStruct(ws[n].shape, F32) for _ in range(4) for n in names]
    out_shape.append(jax.ShapeDtypeStruct((SUBLANES, LANES), F32))
    outs = pl.pallas_call(
        body, name="adamw_small", in_specs=[vm] * len(args), out_specs=[vm] * (4 * k + 1), out_shape=out_shape,
    )(*args)
    return [dict(zip(names, outs[q * k:(q + 1) * k])) for q in range(4)], outs[4 * k]


def _into_slot(v, tail, slot, lead, name):
    n = v.shape[1]
    nl = len(lead)
    rows = n // LANES + SUBLANES

    def body(slot_ref, v_ref, t_ref, o_ref):
        for r in range(n // LANES):
            o_ref[r:r + 1, :] = v_ref[0:1, r * LANES:(r + 1) * LANES]
        o_ref[n // LANES:, :] = t_ref[...]

    grid_spec = pltpu.PrefetchScalarGridSpec(
        num_scalar_prefetch=1, grid=(1,),
        in_specs=[pl.BlockSpec(v.shape, lambda i, sl: (0, 0)), pl.BlockSpec(tail.shape, lambda i, sl: (0, 0))],
        out_specs=pl.BlockSpec((None,) * nl + (rows, LANES), lambda i, sl: tuple(sl[q] for q in range(nl)) + (0, 0)))
    return pl.pallas_call(
        body, name=name, grid_spec=grid_spec, out_shape=jax.ShapeDtypeStruct(tuple(lead) + (rows, LANES), F32),
    )(slot, v, tail)


def _rows128(a):
    return a.reshape(-1, LANES)


def _pack_small(parts):
    pieces = [_rows128(parts[n]) for n, _ in SMALL_ROWS]
    pieces.append(jnp.zeros((SMALL_TOTAL - SMALL_USED, LANES), F32))
    return jnp.concatenate(pieces, axis=0)


def kernel(x, p, pre_g, w_in, gmlp_ln_g, gmlp_ln_b, gmlp_ws, gmlp_bs, conv_w, conv_b, w_a, b_a, w_x, b_x, lam, gmlp_out_g, lru_out_g, w_out, post_g, w_pe, w_pg, loss_target, m_pre_g, m_w_in, m_gmlp_ln_g, m_gmlp_ln_b, m_gmlp_ws, m_gmlp_bs, m_conv_w, m_conv_b, m_w_a, m_b_a, m_w_x, m_b_x, m_lam, m_gmlp_out_g, m_lru_out_g, m_w_out, m_post_g, m_w_pe, m_w_pg, v_pre_g, v_w_in, v_gmlp_ln_g, v_gmlp_ln_b, v_gmlp_ws, v_gmlp_bs, v_conv_w, v_conv_b, v_w_a, v_b_a, v_w_x, v_b_x, v_lam, v_gmlp_out_g, v_lru_out_g, v_w_out, v_post_g, v_w_pe, v_w_pg):
    weights = dict(pre_g=pre_g, w_in=w_in, gmlp_ln_g=gmlp_ln_g, gmlp_ln_b=gmlp_ln_b, gmlp_ws=gmlp_ws, gmlp_bs=gmlp_bs,
                   conv_w=conv_w, conv_b=conv_b, w_a=w_a, b_a=b_a, w_x=w_x, b_x=b_x, lam=lam, gmlp_out_g=gmlp_out_g,
                   lru_out_g=lru_out_g, w_out=w_out, post_g=post_g, w_pe=w_pe, w_pg=w_pg)
    mom_m = dict(pre_g=m_pre_g, w_in=m_w_in, gmlp_ln_g=m_gmlp_ln_g, gmlp_ln_b=m_gmlp_ln_b, gmlp_ws=m_gmlp_ws,
                 gmlp_bs=m_gmlp_bs, conv_w=m_conv_w, conv_b=m_conv_b, w_a=m_w_a, b_a=m_b_a, w_x=m_w_x, b_x=m_b_x,
                 lam=m_lam, gmlp_out_g=m_gmlp_out_g, lru_out_g=m_lru_out_g, w_out=m_w_out, post_g=m_post_g,
                 w_pe=m_w_pe, w_pg=m_w_pg)
    mom_v = dict(pre_g=v_pre_g, w_in=v_w_in, gmlp_ln_g=v_gmlp_ln_g, gmlp_ln_b=v_gmlp_ln_b, gmlp_ws=v_gmlp_ws,
                 gmlp_bs=v_gmlp_bs, conv_w=v_conv_w, conv_b=v_conv_b, w_a=v_w_a, b_a=v_b_a, w_x=v_w_x, b_x=v_b_x,
                 lam=v_lam, gmlp_out_g=v_gmlp_out_g, lru_out_g=v_lru_out_g, w_out=v_w_out, post_g=v_post_g,
                 w_pe=v_w_pe, w_pg=v_w_pg)
    order = list(weights)
    xi, yi, ci = _place()
    me = _chip_of(xi, yi)
    kc = jnp.stack([me, ci]).astype(jnp.int32)

    x2 = x[0]
    p2 = p[0, 0]
    tgt = loss_target[0]

    first = [_cast_into_slot(w_in[0], kc, "cast_w_in").reshape(N_CHIPS, 2, D_MODEL // 2, W_IN_COLS),
             _cast_into_slot(conv_w[0, :, 0, :], kc, "conv_w_into_slot", F32).reshape(N_CHIPS, 2, CONV_W // 2, CONV_COLS)]
    in_st, in_tok = _exchange_start("gather_in_start", first, 6, _gather_ici_copies(2))
    later = [_cast_into_slot(w_out[0], kc, "cast_w_out", token=in_tok).reshape(N_CHIPS, 2, W_ROWS // 2, D_MODEL),
             _cast_into_slot(w_pg[0], kc, "cast_w_pg", token=in_tok).reshape(N_CHIPS, 2, W_ROWS // 2, D_MODEL),
             _cast_into_slot(w_pe[0], kc, "cast_w_pe", token=in_tok).reshape(N_CHIPS, 2, D_PLE // 2, W_PE_COLS)]
    gather_st, gather_tok = _exchange_start("gather_start", later, 9, _gather_direct_copies(3), after=in_tok)
    hn, z_own, hn_t = _inproj_local(x2, pre_g, w_in[0], ROW_TILE, gather_tok)
    in_st, in_tok = _exchange_wait_start("gather_in_relay", in_st, z_own, _gather_ici_copies(2), 6,
                                         _gather_relay_copies(2))
    g_in, g_cw = _exchange_wait("gather_in_wait", in_st, in_tok, _gather_relay_copies(2))
    wg_in = g_in.reshape(N_CHIPS, D_MODEL, W_IN_COLS)
    cw_full = jnp.transpose(g_cw.reshape(N_CHIPS, CONV_W, CONV_COLS), (1, 0, 2)).reshape(CONV_W, D_HALF)

    causal = jnp.tril(jnp.ones((CHUNK, CHUNK), dtype=bool))
    ws_m = jnp.where(causal[None], gmlp_ws[0], 0.0)
    prm = dict(
        ln_g=gmlp_ln_g, ln_b=gmlp_ln_b, wt=ws_m.astype(BF16), wtt=jnp.transpose(ws_m, (0, 2, 1)).astype(BF16),
        bsx=jnp.repeat(jnp.transpose(gmlp_bs[0]), CHUNK, axis=1),
        conv_w=cw_full, conv_b=conv_b, w_a=w_a[0].astype(BF16), w_x=w_x[0].astype(BF16),
        b_a=b_a[0].reshape(1, D_HALF), b_x=b_x[0].reshape(1, D_HALF), lam=lam, oga=gmlp_out_g, ogb=lru_out_g)

    z, y, h = _inproj_branches_fwd(hn, z_own, wg_in, kc, prm, ROW_TILE, gather_tok)
    g_out, g_pg, g_pe = _exchange_wait("gather_wait", gather_st, y, _gather_direct_copies(3))
    wg_out = g_out.reshape(D_MODEL, D_MODEL)
    wg_pg = g_pg.reshape(D_MODEL, D_MODEL)
    wg_pe = g_pe.reshape(N_CHIPS, D_PLE, W_PE_COLS)
    o, h1, gt, dout, loss_acc = _outproj_fwd(x2, y, p2, tgt, post_g, wg_out, wg_pg, wg_pe, ROW_TILE)

    def sibling_start(tag, bufs):
        lands = [_landing((b.shape[0],) + b.shape[2:], b.dtype) for b in bufs]
        return _exchange_start("sibling_start_" + tag, bufs + lands, len(bufs), _sibling_copies(len(bufs)))

    def pair_then_chip_start(tag, started, after, names, tiles, dtypes):
        n = len(names)
        got = _exchange_wait("sibling_wait_" + tag, started, after, _sibling_copies(n))
        pairs = [_pair_sum(got[b], got[n + b], kc, "pair_sum_" + names[b], tiles[b], dtypes[b]) for b in range(n)]
        lands = [_landing((3,) + pr[0].shape[1:], pr[0].dtype) for pr in pairs]
        return _exchange_start("chip_start_" + tag, [pr[0] for pr in pairs] + lands, 3 * n, _chip_copies(n)), pairs

    def sum_then_finish_start(tag, started, pairs, after, names, tiles, small, to_all=()):
        n = len(names)
        got = _exchange_wait("chip_wait_" + tag, started, after, _chip_copies(n))
        sums = [_chip_sum(pairs[b][1], got[n + b], kc if small and b == n - 1 else kc[1:],
                          (N_CHIPS, 2) if small and b == n - 1 else (2,), "chip_sum_" + names[b], tiles[b])
                for b in range(n)]
        nbig = n - 1 if small else n
        n_all = n - nbig + len(to_all)
        return _exchange_start("finish_start_" + tag, sums + list(to_all), nbig + 7 * n_all,
                               _finish_copies(nbig, n_all))

    gw_pe, dq, dh1, do, dy, g_post = _head_bwd(dout, gt, p2, o, post_g, wg_out, wg_pg, wg_pe, ROW_TILE)
    gw_pe = gw_pe.reshape(N_CHIPS, 2, D_PLE // 2, W_PE_COLS)
    token0 = jnp.zeros((SUBLANES, LANES), F32)
    gw_out = _weight_grad(y, do, "grad_w_out", 2, 1, D_MODEL // 2, D_MODEL, CONTRACT_TILE, token0)
    gw_pg = _weight_grad(h1, dq, "grad_w_pg", 2, 1, D_MODEL // 2, D_MODEL, CONTRACT_TILE, token0)
    gw_out = gw_out.reshape(N_CHIPS, 2, W_ROWS // 2, D_MODEL)
    gw_pg = gw_pg.reshape(N_CHIPS, 2, W_ROWS // 2, D_MODEL)

    names_a, tiles_a = ["w_out", "w_pg", "w_pe"], [SUM_TILE] * 3
    st, tok = sibling_start("a", [gw_out, gw_pg, gw_pe])
    (dz, g_oga, g_ogb, g_lng, g_lnb, g_bsx, g_ws, g_cw, g_cb, g_wa, g_ba, g_wx, g_bx, g_lam) = _branches_bwd(
        z, h, dy, prm, ROW_TILE, tok)
    (st, tok), pairs_a = pair_then_chip_start("a", st, dz, names_a, tiles_a, [BF16] * 3)
    gw_in = _weight_grad(hn_t, dz, "grad_w_in", 2, N_CHIPS, D_MODEL // 2, W_IN_COLS, CONTRACT_TILE, tok,
                         a_transposed=True)
    fin_a, tok = sum_then_finish_start("a", st, pairs_a, gw_in, names_a, tiles_a, False)

    small_g = dict(
        gmlp_ln_g=g_lng[0:1], gmlp_ln_b=g_lnb[0:1], gmlp_ws=g_ws,
        gmlp_bs=jnp.transpose(g_bsx[:, ::CHUNK]), conv_w=g_cw[::SUBLANES], conv_b=g_cb[0:1], w_a=g_wa, b_a=g_ba[0:1],
        w_x=g_wx, b_x=g_bx[0:1], lam=g_lam[0:1], gmlp_out_g=g_oga[0:1], lru_out_g=g_ogb[0:1], post_g=g_post[0:1])
    gsm = _pack_small(small_g).reshape(N_CHIPS, 2, SMALL_PIECE, LANES)

    names_b, tiles_b = ["w_in", "small"], [2 * SUM_TILE, SMALL_PIECE]
    n_tiles = x2.shape[0] // ROW_TILE
    n_lo = max(1, (5 * n_tiles) // 16)
    st, tok_b = _exchange_start(
        "sibling_start_b", [gw_in, gsm] + [_landing((N_CHIPS,) + b.shape[2:], F32) for b in (gw_in, gsm)], 2,
        _sibling_copies(2), after=tok)
    part = _inproj_bwd(dz, wg_in, x2, dh1, pre_g, ROW_TILE, 0, n_lo, None, False, tok_b, "inproj_bwd_lo")
    f_out, f_pg, f_pe = _exchange_wait("finish_wait_a", fin_a, part[1], _finish_copies(3, 0))
    (st, tok_b), pairs_b = pair_then_chip_start("b", st, part[1], names_b, tiles_b, [BF16, F32])
    grad_x, g_pre = _inproj_bwd(dz, wg_in, x2, dh1, pre_g, ROW_TILE, n_lo, n_tiles - n_lo, part, True, tok_b,
                                "inproj_bwd_hi")
    pre_parts = _into_slot(g_pre, loss_acc, kc, (N_CHIPS, 2), "pre_g_into_slot")
    fin_b, tok_b = sum_then_finish_start("b", st, pairs_b, g_pre, names_b, tiles_b, True, to_all=[pre_parts])

    grads, deltas, new_m, new_v = {}, {}, {}, {}

    def adam_big(n, g2d, tr, token):
        shp = weights[n].shape
        g, d, nm, nv = _adamw(weights[n][0], g2d, mom_m[n][0], mom_v[n][0], "adamw_" + n, tr, token)
        grads[n], deltas[n], new_m[n], new_v[n] = g.reshape(shp), d.reshape(shp), nm.reshape(shp), nv.reshape(shp)
        return d

    as_token = lambda d: d[:SUBLANES, :LANES]
    last = adam_big("w_out", f_out.reshape(W_ROWS, D_MODEL), SUM_TILE, tok_b)
    last = adam_big("w_pg", f_pg.reshape(W_ROWS, D_MODEL), SUM_TILE, as_token(last))
    last = adam_big("w_pe", f_pe.reshape(D_PLE, W_PE_COLS), SUM_TILE, as_token(last))
    f_in, f_sm, pre_parts = _exchange_wait("finish_wait_b", fin_b, last, _finish_copies(1, 2))
    adam_big("w_in", f_in.reshape(D_MODEL, W_IN_COLS), 2 * SUM_TILE, tok_b)

    packed_g = f_sm.reshape(SMALL_TOTAL, LANES)
    small_names = ["pre_g"] + [n for n, _ in SMALL_ROWS if n != "conv_w"]
    natural = lambda src: {n: (src[n] if src[n].ndim == 2 else src[n][0]) for n in small_names}
    outs, loss_block = _adamw_small(packed_g, pre_parts.reshape(8, D_MODEL // LANES + SUBLANES, LANES),
                                    natural(weights), natural(mom_m), natural(mom_v))
    loss = loss_block[0, 0]
    for dst, got in zip((grads, deltas, new_m, new_v), outs):
        for n in small_names:
            dst[n] = got[n].reshape(weights[n].shape)
    at = sum(r for n, r in SMALL_ROWS[:[n for n, _ in SMALL_ROWS].index("conv_w")])
    g_cw_all = packed_g[at:at + CONV_W * D_HALF // LANES].reshape(CONV_W, D_HALF)
    g_conv = lax.dynamic_slice_in_dim(g_cw_all, me * CONV_COLS, CONV_COLS, axis=1)
    g, d, nm, nv = _adamw(conv_w[0, :, 0, :], g_conv, m_conv_w[0, :, 0, :], v_conv_w[0, :, 0, :], "adamw_conv_w", CONV_W,
                          tok_b)
    cshape = conv_w.shape
    grads["conv_w"], deltas["conv_w"] = g.reshape(cshape), d.reshape(cshape)
    new_m["conv_w"], new_v["conv_w"] = nm.reshape(cshape), nv.reshape(cshape)

    return (loss, grad_x.reshape(x.shape), *[grads[n] for n in order], *[deltas[n] for n in order],
            *[new_m[n] for n in order], *[new_v[n] for n in order])
```

```python
import math

import jax
import jax.numpy as jnp
from jax import lax
from jax.experimental import pallas as pl
from jax.experimental.pallas import tpu as pltpu

F32 = jnp.float32
BF16 = jnp.bfloat16

D_MODEL = 2048
D_HALF = 1024
D_Z = 5120
D_PLE = 256
CHUNK = 128
N_HEADS = 8
N_CHIPS = 4
W_IN_COLS = D_Z // N_CHIPS
W_ROWS = D_MODEL // N_CHIPS
W_PE_COLS = D_MODEL // N_CHIPS
CONV_W = 4
CONV_COLS = D_HALF // N_CHIPS
EPS = 1e-6
LRU_C = 8.0
ADAM_LR, ADAM_B1, ADAM_B2, ADAM_EPS, ADAM_WD, ADAM_STEP = 0.001, 0.9, 0.999, 1e-08, 0.01, 10

SUBLANES = 8
LANES = 128
VMEM_LIMIT = 56 * 1024 * 1024
ROW_TILE = 256
CONTRACT_TILE = 2048
SUM_TILE = 128

SMALL_ROWS = (("gmlp_ln_g", 8), ("gmlp_ln_b", 8), ("gmlp_ws", 1024), ("gmlp_bs", 8),
              ("conv_w", 32), ("conv_b", 8), ("w_a", 1024), ("b_a", 8), ("w_x", 1024), ("b_x", 8),
              ("lam", 8), ("gmlp_out_g", 8), ("lru_out_g", 8), ("post_g", 16))
SMALL_USED = sum(r for _, r in SMALL_ROWS)
SMALL_PIECE = 400
SMALL_TOTAL = 8 * SMALL_PIECE

MESH = pl.DeviceIdType.MESH
ANY = pl.BlockSpec(memory_space=pl.ANY)

_GELU_C0 = math.sqrt(2.0 / math.pi)
_GELU_C1 = 0.044715


def _params(*sem):
    return pltpu.CompilerParams(dimension_semantics=sem, vmem_limit_bytes=VMEM_LIMIT)


def _dot(a, b):
    return jnp.dot(a, b, preferred_element_type=F32)


def _dot_nt(a, b):
    return lax.dot_general(a, b, (((1,), (1,)), ((), ())), preferred_element_type=F32)


def _dot_tn(a, b):
    return lax.dot_general(a, b, (((0,), (0,)), ((), ())), preferred_element_type=F32)


def _gelu(x):
    t = jnp.tanh(_GELU_C0 * (x + _GELU_C1 * (x * x * x)))
    return 0.5 * x * (1.0 + t), t


def _gelu_grad(x, t):
    return 0.5 * (1.0 + t) + 0.5 * x * (1.0 - t * t) * (_GELU_C0 * (1.0 + 3.0 * _GELU_C1 * x * x))


def _rowsum8(v):
    r, n = v.shape
    return jnp.sum(v.reshape(r // SUBLANES, SUBLANES, n), axis=0)


def _lanemean(v):
    return jnp.mean(v, axis=-1, keepdims=True)


def _shift_down(v, halo8, k):
    if k == 0:
        return v
    r = pltpu.roll(v, k, 0)
    hr = pltpu.roll(halo8, k, 0)
    row = lax.broadcasted_iota(jnp.int32, halo8.shape, 0)
    top = jnp.where(row < k, hr, r[0:SUBLANES])
    return jnp.concatenate([top, r[SUBLANES:]], axis=0)


def _shift_up(v, next8, k):
    if k == 0:
        return v
    n = v.shape[0]
    r = pltpu.roll(v, n - k, 0)
    nr = pltpu.roll(next8, SUBLANES - k, 0)
    row = lax.broadcasted_iota(jnp.int32, next8.shape, 0)
    bot = jnp.where(row >= SUBLANES - k, nr, r[n - SUBLANES:])
    return jnp.concatenate([r[:n - SUBLANES], bot], axis=0)


def _layernorm_parts(vg):
    mu = _lanemean(vg)
    xc = vg - mu
    rstd = lax.rsqrt(_lanemean(xc * xc) + EPS)
    return xc * rstd, rstd


def _spatial_mix(wt_ref, vn_ref, bsx_ref, mixed_ref, tm):
    for c in range(tm // CHUNK):
        rows = slice(c * CHUNK, (c + 1) * CHUNK)
        for h in range(N_HEADS):
            cols = slice(h * CHUNK, (h + 1) * CHUNK)
            mixed_ref[rows, cols] = _dot(wt_ref[h], vn_ref[rows, cols]) + bsx_ref[:, cols]


def _conv_taps(xb, halo8):
    return [_shift_down(xb, halo8, CONV_W - 1 - k) for k in range(CONV_W)]


def _lru_gates(xc_bf_ref, wa_ref, wx_ref, ba_ref, bx_ref, r_ref, i_ref):
    for h in range(N_HEADS):
        cols = slice(h * CHUNK, (h + 1) * CHUNK)
        xh = xc_bf_ref[:, cols]
        r_ref[:, cols] = jax.nn.sigmoid(_dot(xh, wa_ref[h]) + ba_ref[:, cols])
        i_ref[:, cols] = jax.nn.sigmoid(_dot(xh, wx_ref[h]) + bx_ref[:, cols])


def _softplus_neg(lam):
    return jnp.maximum(-lam, 0.0) + jnp.log(1.0 + jnp.exp(-jnp.abs(lam)))


def _decay_parts(r, lam):
    la = (-LRU_C * _softplus_neg(lam)) * r
    a = jnp.exp(la)
    th = -jnp.tanh(la)
    mult = jnp.sqrt(2.0 * th / (1.0 + th))
    return a, mult


def _z_group(zref, g, rows=slice(None)):
    lo = g * D_HALF
    blk, off = lo // W_IN_COLS, lo % W_IN_COLS
    if off + D_HALF <= W_IN_COLS:
        return zref[blk, rows, off:off + D_HALF]
    return jnp.concatenate([zref[blk, rows, off:W_IN_COLS], zref[blk + 1, rows, 0:off + D_HALF - W_IN_COLS]], axis=1)


def _inproj_local(x, pre_g, w_own, tm, token):
    t = x.shape[0]

    def body(x_ref, g_ref, w_ref, token_ref, hn_ref, zl_ref, hnt_ref, wbf_s):
        @pl.when(pl.program_id(0) == 0)
        def _():
            wbf_s[...] = w_ref[...].astype(BF16)

        xv = x_ref[...]
        hnf = xv * lax.rsqrt(_lanemean(xv * xv) + EPS) * g_ref[...]
        hn = hnf.astype(BF16)
        hn_ref[...] = hn
        hnt_ref[...] = hnf.T.astype(BF16)
        zl_ref[...] = _dot(hn, wbf_s[...]).astype(BF16)

    row = lambda n: pl.BlockSpec((tm, n), lambda i: (i, 0))
    const = lambda shp: pl.BlockSpec(shp, lambda i: (0, 0), pipeline_mode=pl.Buffered(1))
    return pl.pallas_call(
        body, name="inproj_local", grid=(t // tm,),
        in_specs=[row(D_MODEL), const((1, D_MODEL)), const((D_MODEL, W_IN_COLS)), const((SUBLANES, LANES))],
        out_specs=[row(D_MODEL), row(W_IN_COLS), pl.BlockSpec((D_MODEL, tm), lambda i: (0, i))],
        out_shape=[jax.ShapeDtypeStruct((t, D_MODEL), BF16), jax.ShapeDtypeStruct((t, W_IN_COLS), BF16),
                   jax.ShapeDtypeStruct((D_MODEL, t), BF16)],
        scratch_shapes=[pltpu.VMEM((D_MODEL, W_IN_COLS), BF16)],
        compiler_params=_params("arbitrary"),
    )(x, pre_g, w_own, token)


def _inproj_branches_fwd(hn, z_own, wg_in, kc, prm, tm, token):
    t = hn.shape[0]
    nt = t // tm
    hb = tm // SUBLANES

    def body(kc_ref, hn_ref, zo_ref, w1_ref, w2_ref, w3_ref,
             lng_ref, lnb_ref, wt_ref, bsx_ref, cw_ref, cb_ref, wa_ref, wx_ref, ba_ref, bx_ref, lam_ref,
             oga_ref, ogb_ref, token_ref,
             z_ref, y_ref, h_ref,
             zbuf0, zbuf1, vn_s, mixed_s, xcbf_s, r_s, i_s, ug_s, halo_s, carry_s):
        s = pl.program_id(0)
        me = kc_ref[0]
        w_refs = (None, w1_ref, w2_ref, w3_ref)

        @pl.when(s == 0)
        def _():
            zbuf1[...] = jnp.zeros_like(zbuf1)

        @pl.when(s <= 1)
        def _():
            carry_s[...] = jnp.zeros_like(carry_s)
            halo_s[...] = jnp.zeros_like(halo_s)

        def step(zw, zr):
            def project(r):
                blk = (me + r) % N_CHIPS
                zb = zo_ref[...] if r == 0 else _dot(hn_ref[...], w_refs[r][...]).astype(BF16)
                z_ref[blk] = zb
                zw[blk] = zb

            zin = lambda g: _z_group(zr, g).astype(F32)
            always = [s >= 0] * 4

            @pl.when(always[0])
            def _():
                project(0)
                ug, _ = _gelu(zin(0))
                ug_s[...] = ug
                vg, _ = _gelu(zin(1))
                vhat, _ = _layernorm_parts(vg)
                vn_s[...] = (vhat * lng_ref[...] + lnb_ref[...]).astype(BF16)

            @pl.when(always[1])
            def _():
                project(1)
                _spatial_mix(wt_ref, vn_s, bsx_ref, mixed_s, tm)
                ga = zin(2)
                ya = ug_s[...] * mixed_s[...] * (ga * jax.nn.sigmoid(ga))
                ra = lax.rsqrt(_lanemean(ya * ya) + EPS)
                y_ref[:, 0:D_HALF] = (ya * ra * oga_ref[...]).astype(BF16)

            @pl.when(always[2])
            def _():
                project(2)
                xb = zin(3)
                taps = _conv_taps(xb, halo_s[...])
                halo_s[...] = xb[tm - SUBLANES:]
                xc = cb_ref[...] + taps[0] * cw_ref[0:1, :]
                for k in range(1, CONV_W):
                    xc = xc + taps[k] * cw_ref[k:k + 1, :]
                xcbf_s[...] = xc.astype(BF16)
                _lru_gates(xcbf_s, wa_ref, wx_ref, ba_ref, bx_ref, r_s, i_s)
                a, mult = _decay_parts(r_s[...], lam_ref[...])
                row = lax.broadcasted_iota(jnp.int32, a.shape, 0)
                mult = jnp.where(jnp.logical_and(s == 1, row == 0), 1.0, mult)
                r_s[...] = a
                i_s[...] = mult * (i_s[...] * xc)

            @pl.when(always[3])
            def _():
                project(3)
                a = r_s[...]
                b = i_s[...]
                r8 = lax.broadcasted_iota(jnp.int32, a.shape, 0) & (SUBLANES - 1)
                for d in (1, 2, 4):
                    a_sh = pltpu.roll(a, d, 0)
                    b_sh = pltpu.roll(b, d, 0)
                    m = r8 >= d
                    b = jnp.where(m, a * b_sh + b, b)
                    a = jnp.where(m, a * a_sh, a)
                carry = carry_s[...]
                for g in range(hb):
                    rows = slice(g * SUBLANES, (g + 1) * SUBLANES)
                    hg = a[rows] * carry + b[rows]
                    h_ref[rows, :] = hg
                    carry = jnp.broadcast_to(hg[SUBLANES - 1:SUBLANES, :], hg.shape)
                carry_s[...] = carry
                gb = zin(4)
                yb = h_ref[...] * (gb * jax.nn.sigmoid(gb))
                rb = lax.rsqrt(_lanemean(yb * yb) + EPS)
                y_ref[:, D_HALF:] = (yb * rb * ogb_ref[...]).astype(BF16)

        @pl.when(s % 2 == 0)
        def _():
            step(zbuf0, zbuf1)

        @pl.when(s % 2 == 1)
        def _():
            step(zbuf1, zbuf0)

    const = lambda a: pl.BlockSpec(a.shape, lambda s, kc, n=a.ndim: (0,) * n, pipeline_mode=pl.Buffered(1))
    proj = lambda n: pl.BlockSpec((tm, n), lambda s, kc: (jnp.minimum(s, nt - 1), 0))
    head = lambda n: pl.BlockSpec((tm, n), lambda s, kc: (jnp.maximum(s - 1, 0), 0))
    other = lambda r: pl.BlockSpec((None, D_MODEL, W_IN_COLS), lambda s, kc, r=r: ((kc[0] + r) % N_CHIPS, 0, 0),
                                   pipeline_mode=pl.Buffered(1))
    names = ("ln_g", "ln_b", "wt", "bsx", "conv_w", "conv_b", "w_a", "w_x", "b_a", "b_x", "lam", "oga", "ogb")
    pr = [prm[n] for n in names] + [token]
    big = lambda dt: pltpu.VMEM((tm, D_HALF), dt)
    zblocks = pltpu.VMEM((N_CHIPS, tm, W_IN_COLS), BF16)
    grid_spec = pltpu.PrefetchScalarGridSpec(
        num_scalar_prefetch=1, grid=(nt + 1,),
        in_specs=[proj(D_MODEL), proj(W_IN_COLS), other(1), other(2), other(3)] + [const(a) for a in pr],
        out_specs=[pl.BlockSpec((N_CHIPS, tm, W_IN_COLS), lambda s, kc: (0, jnp.minimum(s, nt - 1), 0)),
                   head(D_MODEL), head(D_HALF)],
        scratch_shapes=[zblocks, zblocks, big(BF16), big(F32), big(BF16), big(F32), big(F32), big(F32),
                        pltpu.VMEM((SUBLANES, D_HALF), F32), pltpu.VMEM((SUBLANES, D_HALF), F32)])
    return pl.pallas_call(
        body, name="inproj_branches_fwd", grid_spec=grid_spec,
        out_shape=[jax.ShapeDtypeStruct((N_CHIPS, t, W_IN_COLS), BF16), jax.ShapeDtypeStruct((t, D_MODEL), BF16),
                   jax.ShapeDtypeStruct((t, D_HALF), F32)],
        compiler_params=_params("arbitrary"),
    )(kc, hn, z_own, wg_in, wg_in, wg_in, *pr)


def _outproj_fwd(x, y, p, tgt, post_g, w_out, w_pg, wg_pe, tm):
    t = x.shape[0]

    def body(x_ref, y_ref, p_ref, tgt_ref, pg_ref, wo_ref, wpg_ref, wpe_ref,
             o_ref, h1_ref, gt_ref, dout_ref, loss_ref):
        @pl.when(pl.program_id(0) == 0)
        def _():
            loss_ref[...] = jnp.zeros_like(loss_ref)

        o = _dot(y_ref[...], wo_ref[...])
        o_ref[...] = o
        r3 = lax.rsqrt(_lanemean(o * o) + EPS)
        h1 = x_ref[...] + (o * r3) * pg_ref[...]
        h1b = h1.astype(BF16)
        h1_ref[...] = h1b
        gt = jax.nn.sigmoid(_dot(h1b, wpg_ref[...]))
        gt_ref[...] = gt
        pb = p_ref[...].astype(BF16)
        for k in range(N_CHIPS):
            cols = slice(k * W_PE_COLS, (k + 1) * W_PE_COLS)
            pe = _dot(pb, wpe_ref[k])
            d = h1[:, cols] + pe * gt[:, cols] - tgt_ref[:, cols]
            dout_ref[:, cols] = d * (1.0 / D_MODEL)
            loss_ref[...] += jnp.sum(d * d) * (0.5 / D_MODEL)

    row = lambda n: pl.BlockSpec((tm, n), lambda i: (i, 0))
    const = lambda shp: pl.BlockSpec(shp, lambda i, n=len(shp): (0,) * n, pipeline_mode=pl.Buffered(1))
    return pl.pallas_call(
        body, name="outproj_fwd", grid=(t // tm,),
        in_specs=[row(D_MODEL), row(D_MODEL), row(D_PLE), row(D_MODEL), const((1, D_MODEL)),
                  const((D_MODEL, D_MODEL)), const((D_MODEL, D_MODEL)), const((N_CHIPS, D_PLE, W_PE_COLS))],
        out_specs=[row(D_MODEL), row(D_MODEL), row(D_MODEL), row(D_MODEL),
                   pl.BlockSpec((SUBLANES, LANES), lambda i: (0, 0))],
        out_shape=[jax.ShapeDtypeStruct((t, D_MODEL), F32), jax.ShapeDtypeStruct((t, D_MODEL), BF16),
                   jax.ShapeDtypeStruct((t, D_MODEL), F32), jax.ShapeDtypeStruct((t, D_MODEL), F32),
                   jax.ShapeDtypeStruct((SUBLANES, LANES), F32)],
        compiler_params=_params("arbitrary"),
    )(x, y, p, tgt, post_g, w_out, w_pg, wg_pe)


def _head_bwd(dout, gt, p, o, post_g, w_out, w_pg, wg_pe, tm):
    t = dout.shape[0]

    def body(dout_ref, gt_ref, p_ref, o_ref, pg_ref, wo_ref, wpg_ref, wpe_ref,
             gwpe_ref, dq_ref, dh1_ref, do_ref, dy_ref, gpost_ref):
        i = pl.program_id(0)

        @pl.when(i == 0)
        def _():
            gpost_ref[...] = jnp.zeros_like(gpost_ref)
            gwpe_ref[...] = jnp.zeros_like(gwpe_ref)

        dout = dout_ref[...]
        gt = gt_ref[...]
        pb = p_ref[...].astype(BF16)
        for k in range(N_CHIPS):
            cols = slice(k * W_PE_COLS, (k + 1) * W_PE_COLS)
            pe = _dot(pb, wpe_ref[k])
            g = gt[:, cols]
            dg = dout[:, cols] * g
            gwpe_ref[k] += _dot_tn(pb, dg.astype(BF16))
            dq_ref[:, cols] = (dg * pe * (1.0 - g)).astype(BF16)
        dh1 = dout + _dot_nt(dq_ref[...], wpg_ref[...])
        dh1_ref[...] = dh1
        o = o_ref[...]
        r3 = lax.rsqrt(_lanemean(o * o) + EPS)
        on = o * r3
        gpost_ref[...] += _rowsum8(dh1 * on)
        don = dh1 * pg_ref[...]
        do = r3 * (don - on * _lanemean(don * on))
        dob = do.astype(BF16)
        do_ref[...] = dob
        dy_ref[...] = _dot_nt(dob, wo_ref[...])

        @pl.when(i == pl.num_programs(0) - 1)
        def _():
            gpost_ref[...] = jnp.broadcast_to(jnp.sum(gpost_ref[...], axis=0, keepdims=True), gpost_ref.shape)

    row = lambda n: pl.BlockSpec((tm, n), lambda i: (i, 0))
    const = lambda shp: pl.BlockSpec(shp, lambda i, n=len(shp): (0,) * n, pipeline_mode=pl.Buffered(1))
    return pl.pallas_call(
        body, name="head_bwd", grid=(t // tm,),
        in_specs=[row(D_MODEL), row(D_MODEL), row(D_PLE), row(D_MODEL), const((1, D_MODEL)),
                  const((D_MODEL, D_MODEL)), const((D_MODEL, D_MODEL)), const((N_CHIPS, D_PLE, W_PE_COLS))],
        out_specs=[pl.BlockSpec((N_CHIPS, D_PLE, W_PE_COLS), lambda i: (0, 0, 0)),
                   row(D_MODEL), row(D_MODEL), row(D_MODEL), row(D_MODEL),
                   pl.BlockSpec((SUBLANES, D_MODEL), lambda i: (0, 0))],
        out_shape=[jax.ShapeDtypeStruct((N_CHIPS, D_PLE, W_PE_COLS), F32), jax.ShapeDtypeStruct((t, D_MODEL), BF16),
                   jax.ShapeDtypeStruct((t, D_MODEL), F32), jax.ShapeDtypeStruct((t, D_MODEL), BF16),
                   jax.ShapeDtypeStruct((t, D_MODEL), F32), jax.ShapeDtypeStruct((SUBLANES, D_MODEL), F32)],
        compiler_params=_params("arbitrary"),
    )(dout, gt, p, o, post_g, w_out, w_pg, wg_pe)


def _branches_bwd(z, h, dy, prm, tm, token):
    t = h.shape[0]
    nt = t // tm
    hb = tm // SUBLANES

    def body(z_ref, zh_ref, h_ref, hh_ref, dy_ref,
             lng_ref, lnb_ref, wt_ref, wtt_ref, bsx_ref, cw_ref, cb_ref, wa_ref, wx_ref, ba_ref, bx_ref, lam_ref,
             oga_ref, ogb_ref, token_ref,
             dz_ref, g_oga, g_ogb, g_lng, g_lnb, g_bsx, g_ws, g_cw, g_cb, g_wa, g_ba, g_wx, g_bx, g_lam,
             vn_s, mixed_s, dm_s, dvn_s, xcbf_s, r_s, i_s, a_s, b_s, dh_s, dpr_s, dpi_s, dxc_s,
             ca_s, cd_s, cx_s):
        step_i = pl.program_id(0)
        tile = nt - 1 - step_i
        accs = (g_oga, g_ogb, g_lng, g_lnb, g_bsx, g_ws, g_cw, g_cb, g_wa, g_ba, g_wx, g_bx, g_lam)

        @pl.when(step_i == 0)
        def _():
            for r in accs + (ca_s, cd_s, cx_s):
                r[...] = jnp.zeros_like(r)

        dy_a = dy_ref[:, 0:D_HALF]
        dy_b = dy_ref[:, D_HALF:]

        u = _z_group(z_ref, 0).astype(F32)
        ug, tu = _gelu(u)
        v = _z_group(z_ref, 1).astype(F32)
        vg, tv = _gelu(v)
        vhat, rstd = _layernorm_parts(vg)
        vn_s[...] = (vhat * lng_ref[...] + lnb_ref[...]).astype(BF16)
        _spatial_mix(wt_ref, vn_s, bsx_ref, mixed_s, tm)
        mixed = mixed_s[...]
        ga = _z_group(z_ref, 2).astype(F32)
        sga = jax.nn.sigmoid(ga)
        sa = ga * sga
        um = ug * mixed
        ya = um * sa
        ra = lax.rsqrt(_lanemean(ya * ya) + EPS)
        yahat = ya * ra
        g_oga[...] += _rowsum8(dy_a * yahat)
        dn = dy_a * oga_ref[...]
        dya = ra * (dn - yahat * _lanemean(dn * yahat))
        dz_ref[:, 2 * D_HALF:3 * D_HALF] = (dya * um * (sga * (1.0 + ga * (1.0 - sga)))).astype(BF16)
        dz_ref[:, 0:D_HALF] = (dya * mixed * sa * _gelu_grad(u, tu)).astype(BF16)
        dmixed = dya * ug * sa
        g_bsx[...] += jnp.sum(dmixed.reshape(tm // CHUNK, CHUNK, D_HALF), axis=0)
        dm_s[...] = dmixed.astype(BF16)
        for c in range(tm // CHUNK):
            rows = slice(c * CHUNK, (c + 1) * CHUNK)
            for hd in range(N_HEADS):
                cols = slice(hd * CHUNK, (hd + 1) * CHUNK)
                dmh = dm_s[rows, cols]
                dvn_s[rows, cols] = _dot(wtt_ref[hd], dmh)
                g_ws[hd] += _dot_nt(dmh, vn_s[rows, cols])
        dvn = dvn_s[...]
        g_lng[...] += _rowsum8(dvn * vhat)
        g_lnb[...] += _rowsum8(dvn)
        dvh = dvn * lng_ref[...]
        dvg = rstd * (dvh - _lanemean(dvh) - vhat * _lanemean(dvh * vhat))
        dz_ref[:, D_HALF:2 * D_HALF] = (dvg * _gelu_grad(v, tv)).astype(BF16)

        xb = _z_group(z_ref, 3).astype(F32)
        halo = jnp.where(tile == 0, 0.0, _z_group(zh_ref, 3).astype(F32)[SUBLANES:])
        taps = _conv_taps(xb, halo)
        xc = cb_ref[...] + taps[0] * cw_ref[0:1, :]
        for k in range(1, CONV_W):
            xc = xc + taps[k] * cw_ref[k:k + 1, :]
        xcbf_s[...] = xc.astype(BF16)
        _lru_gates(xcbf_s, wa_ref, wx_ref, ba_ref, bx_ref, r_s, i_s)
        rg = r_s[...]
        ig = i_s[...]
        lam = lam_ref[...]
        a, mult_true = _decay_parts(rg, lam)
        row = lax.broadcasted_iota(jnp.int32, a.shape, 0)
        first = jnp.logical_and(tile == 0, row == 0)
        mult = jnp.where(first, 1.0, mult_true)
        hcur = h_ref[...]
        hprev = _shift_down(hcur, jnp.where(tile == 0, 0.0, hh_ref[...]), 1)
        gb = _z_group(z_ref, 4).astype(F32)
        sgb = jax.nn.sigmoid(gb)
        sb = gb * sgb
        yb = hcur * sb
        rb = lax.rsqrt(_lanemean(yb * yb) + EPS)
        ybhat = yb * rb
        g_ogb[...] += _rowsum8(dy_b * ybhat)
        dn = dy_b * ogb_ref[...]
        dyb = rb * (dn - ybhat * _lanemean(dn * ybhat))
        dz_ref[:, 4 * D_HALF:5 * D_HALF] = (dyb * hcur * (sgb * (1.0 + gb * (1.0 - sgb)))).astype(BF16)

        an = _shift_up(a, ca_s[...], 1)
        bb = dyb * sb
        r8 = row & (SUBLANES - 1)
        for d in (1, 2, 4):
            a_sh = pltpu.roll(an, tm - d, 0)
            b_sh = pltpu.roll(bb, tm - d, 0)
            m = r8 + d < SUBLANES
            bb = jnp.where(m, an * b_sh + bb, bb)
            an = jnp.where(m, an * a_sh, an)
        a_s[...] = an
        b_s[...] = bb

        def step(g, carry):
            sl = pl.ds(pl.multiple_of((hb - 1 - g) * SUBLANES, SUBLANES), SUBLANES)
            dg = a_s[sl, :] * carry + b_s[sl, :]
            dh_s[sl, :] = dg
            return jnp.broadcast_to(dg[0:1, :], dg.shape)

        cd_s[...] = lax.fori_loop(0, hb, step, cd_s[...])
        ca_s[...] = jnp.broadcast_to(a[0:1, :], ca_s.shape)
        dh = dh_s[...]
        da = dh * hprev
        gx = ig * xc
        dla = da * a - jnp.where(first, 0.0, dh * gx * (a * a / mult_true))
        g_lam[...] += _rowsum8(dla * rg)
        dr = dla * (-LRU_C * _softplus_neg(lam))
        dpr = dr * rg * (1.0 - rg)
        dpi = (dh * mult * xc) * ig * (1.0 - ig)
        g_ba[...] += _rowsum8(dpr)
        g_bx[...] += _rowsum8(dpi)
        dpr_s[...] = dpr.astype(BF16)
        dpi_s[...] = dpi.astype(BF16)
        for hd in range(N_HEADS):
            cols = slice(hd * CHUNK, (hd + 1) * CHUNK)
            xh = xcbf_s[:, cols]
            dprh = dpr_s[:, cols]
            dpih = dpi_s[:, cols]
            g_wa[hd] += _dot_tn(xh, dprh)
            g_wx[hd] += _dot_tn(xh, dpih)
            dxc_s[:, cols] = _dot_nt(dprh, wa_ref[hd]) + _dot_nt(dpih, wx_ref[hd])
        dxc = dxc_s[...] + dh * mult * ig
        g_cb[...] += _rowsum8(dxc)
        for k in range(CONV_W):
            g_cw[k * SUBLANES:(k + 1) * SUBLANES, :] += _rowsum8(dxc * taps[k])
        nxt = cx_s[...]
        dxb = dxc * cw_ref[CONV_W - 1:CONV_W, :]
        for j in range(1, CONV_W):
            dxb = dxb + _shift_up(dxc, nxt, j) * cw_ref[CONV_W - 1 - j:CONV_W - j, :]
        dz_ref[:, 3 * D_HALF:4 * D_HALF] = dxb.astype(BF16)
        cx_s[...] = dxc[0:SUBLANES]

        @pl.when(step_i == nt - 1)
        def _():
            for r in (g_oga, g_ogb, g_lng, g_lnb, g_cb, g_ba, g_bx):
                r[...] = jnp.broadcast_to(jnp.sum(r[...], axis=0, keepdims=True), r.shape)
            lam_f = LRU_C * jax.nn.sigmoid(-lam_ref[...])
            g_lam[...] = jnp.broadcast_to(jnp.sum(g_lam[...], axis=0, keepdims=True) * lam_f, g_lam.shape)
            for k in range(CONV_W):
                blk = g_cw[k * SUBLANES:(k + 1) * SUBLANES, :]
                g_cw[k * SUBLANES:(k + 1) * SUBLANES, :] = jnp.broadcast_to(jnp.sum(blk, axis=0, keepdims=True), blk.shape)
            tri = (lax.broadcasted_iota(jnp.int32, (CHUNK, CHUNK), 0) >= lax.broadcasted_iota(jnp.int32, (CHUNK, CHUNK), 1))
            for hd in range(N_HEADS):
                cols = slice(hd * CHUNK, (hd + 1) * CHUNK)
                g_ws[hd] = jnp.where(tri, g_ws[hd], 0.0)
                blk = g_bsx[:, cols]
                g_bsx[:, cols] = jnp.broadcast_to(jnp.sum(blk, axis=1, keepdims=True), blk.shape)

    rev = lambda i: nt - 1 - i
    zspec = pl.BlockSpec((N_CHIPS, tm, W_IN_COLS), lambda i: (0, rev(i), 0))
    halo = lambda col: pl.BlockSpec((SUBLANES, D_HALF), lambda i: (jnp.maximum(rev(i) * hb - 1, 0), col))
    zhalo = pl.BlockSpec((N_CHIPS, 2 * SUBLANES, W_IN_COLS), lambda i: (0, jnp.maximum(rev(i) * (hb // 2) - 1, 0), 0))
    full = lambda a: pl.BlockSpec(a.shape, lambda i, n=a.ndim: (0,) * n)
    acc = lambda shp: pl.BlockSpec(shp, lambda i, n=len(shp): (0,) * n)
    names = ("ln_g", "ln_b", "wt", "wtt", "bsx", "conv_w", "conv_b", "w_a", "w_x", "b_a", "b_x", "lam", "oga", "ogb")
    pr = [prm[n] for n in names] + [token]
    vec = (SUBLANES, D_HALF)
    mat = (N_HEADS, CHUNK, CHUNK)
    acc_shapes = [vec, vec, vec, vec, (CHUNK, D_HALF), mat, (CONV_W * SUBLANES, D_HALF), vec, mat, vec, mat, vec, vec]
    big = lambda dt: pltpu.VMEM((tm, D_HALF), dt)
    return pl.pallas_call(
        body, name="branches_bwd", grid=(nt,),
        in_specs=[zspec, zhalo,
                  pl.BlockSpec((tm, D_HALF), lambda i: (rev(i), 0)), halo(0),
                  pl.BlockSpec((tm, D_MODEL), lambda i: (rev(i), 0))] + [full(a) for a in pr],
        out_specs=[pl.BlockSpec((tm, D_Z), lambda i: (rev(i), 0))] + [acc(s) for s in acc_shapes],
        out_shape=[jax.ShapeDtypeStruct((t, D_Z), BF16)] + [jax.ShapeDtypeStruct(s, F32) for s in acc_shapes],
        scratch_shapes=[big(BF16), big(F32), big(BF16), big(F32), big(BF16), big(F32), big(F32), big(F32), big(F32),
                        big(F32), big(BF16), big(BF16), big(F32),
                        pltpu.VMEM(vec, F32), pltpu.VMEM(vec, F32), pltpu.VMEM(vec, F32)],
        compiler_params=_params("arbitrary"),
    )(z, z, h, h, dy, *pr)


def _inproj_bwd(dz, wg_in, x, dh1, pre_g, tm, tile0, nt, prev, last, token, name):
    t = x.shape[0]

    def body(*refs):
        dz_ref, w_ref, x_ref, dh1_ref, g_ref = refs[:5]
        gx_ref, gpre_ref, acc_s = refs[-3:]
        i = pl.program_id(0)

        @pl.when(i == 0)
        def _():
            gpre_ref[...] = jnp.zeros_like(gpre_ref) if prev is None else refs[7][...]

        acc = _dot_nt(dz_ref[:, 0:W_IN_COLS], w_ref[0])
        for k in range(1, N_CHIPS):
            acc = acc + _dot_nt(dz_ref[:, k * W_IN_COLS:(k + 1) * W_IN_COLS], w_ref[k])
        acc_s[...] = acc
        for s in range(tm // CHUNK):
            rows = slice(s * CHUNK, (s + 1) * CHUNK)
            xv = x_ref[rows, :]
            r = lax.rsqrt(_lanemean(xv * xv) + EPS)
            xhat = xv * r
            dhn = acc_s[rows, :]
            gpre_ref[...] += _rowsum8(dhn * xhat)
            dxh = dhn * g_ref[...]
            gx_ref[rows, :] = dh1_ref[rows, :] + r * (dxh - xhat * _lanemean(dxh * xhat))

        if last:
            @pl.when(i == nt - 1)
            def _():
                gpre_ref[...] = jnp.broadcast_to(jnp.sum(gpre_ref[...], axis=0, keepdims=True), gpre_ref.shape)

    row = lambda n: pl.BlockSpec((tm, n), lambda i: (tile0 + i, 0))
    small = lambda r: pl.BlockSpec((r, D_MODEL), lambda i: (0, 0))
    tok = pl.BlockSpec((SUBLANES, LANES), lambda i: (0, 0))
    in_specs = [row(D_Z), pl.BlockSpec(wg_in.shape, lambda i: (0, 0, 0), pipeline_mode=pl.Buffered(1)),
                row(D_MODEL), row(D_MODEL), small(1), tok]
    args = [dz, wg_in, x, dh1, pre_g, token]
    aliases = {}
    if prev is not None:
        in_specs += [ANY, small(SUBLANES)]
        args += list(prev)
        aliases = {6: 0}
    return pl.pallas_call(
        body, name=name, grid=(nt,), in_specs=in_specs, out_specs=[row(D_MODEL), small(SUBLANES)],
        out_shape=[jax.ShapeDtypeStruct((t, D_MODEL), F32), jax.ShapeDtypeStruct((SUBLANES, D_MODEL), F32)],
        input_output_aliases=aliases,
        scratch_shapes=[pltpu.VMEM((tm, D_MODEL), F32)],
        compiler_params=_params("arbitrary"),
    )(*args)


def _weight_grad(a, b, name, kb, nb, tk, tn, tt, token, a_transposed=False):
    t = b.shape[0]
    tt = min(tt, t)

    def body(a_ref, b_ref, token_ref, o_ref):
        @pl.when(pl.program_id(2) == 0)
        def _():
            o_ref[...] = jnp.zeros_like(o_ref)

        o_ref[...] += (_dot if a_transposed else _dot_tn)(a_ref[...], b_ref[...])

    a_spec = (pl.BlockSpec((tk, tt), lambda j, i, s: (i, s)) if a_transposed
              else pl.BlockSpec((tt, tk), lambda j, i, s: (s, i)))
    return pl.pallas_call(
        body, name=name, grid=(nb, kb, t // tt),
        in_specs=[a_spec, pl.BlockSpec((tt, tn), lambda j, i, s: (s, j)),
                  pl.BlockSpec((SUBLANES, LANES), lambda j, i, s: (0, 0))],
        out_specs=pl.BlockSpec((None, None, tk, tn), lambda j, i, s: (j, i, 0, 0)),
        out_shape=jax.ShapeDtypeStruct((nb, kb, tk, tn), F32),
        compiler_params=_params("parallel", "parallel", "arbitrary"),
    )(a, b, token)


def _place():
    x, y, c = lax.axis_index("x"), lax.axis_index("y"), lax.axis_index("c")
    return x, y, c


def _chip_of(x, y):
    return 2 * x + y


HBM = pl.BlockSpec(memory_space=pltpu.HBM)
SEM = pl.BlockSpec(memory_space=pltpu.SEMAPHORE)
EFFECT = pltpu.SideEffectType.DATAFLOW_SIDE_EFFECTING


def _hbm(a):
    return pltpu.with_memory_space_constraint(a, pltpu.HBM)


def _landing(shape, dtype):
    return _hbm(lax.empty(shape, dtype))


def _exchange_start(name, arrays, ncopies, build, after=None):
    n = len(arrays)
    extra = [] if after is None else [after]

    def body(*refs):
        ins, token = refs[:n], refs[-1]
        send_sems, recv_sems = refs[n + len(extra)], refs[n + len(extra) + 1]
        for cp in build(ins, send_sems, recv_sems):
            cp.start()
        token[...] = jnp.zeros_like(token)

    outs = pl.pallas_call(
        body, name=name,
        out_shape=(pltpu.SemaphoreType.DMA((ncopies,)), pltpu.SemaphoreType.DMA((ncopies,)),
                   *[pltpu.HBM(a.shape, a.dtype) for a in arrays], jax.ShapeDtypeStruct((SUBLANES, LANES), F32)),
        in_specs=[HBM] * n + [ANY] * len(extra),
        out_specs=(SEM, SEM, *[HBM] * n, pl.BlockSpec(memory_space=pltpu.VMEM)),
        input_output_aliases={q: q + 2 for q in range(n)},
        compiler_params=pltpu.CompilerParams(has_side_effects=EFFECT),
    )(*[_hbm(a) for a in arrays], *extra)
    return (outs[0], outs[1], list(outs[2:2 + n])), outs[-1]


def _exchange_wait(name, started, after, build):
    send, recv, arrays = started
    n = len(arrays)

    def body(*refs):
        ins, send_sems, recv_sems = refs[:n], refs[n], refs[n + 1]
        for cp in build(ins, send_sems, recv_sems):
            cp.wait_send()
            cp.wait_recv()

    return pl.pallas_call(
        body, name=name, out_shape=tuple(pltpu.HBM(a.shape, a.dtype) for a in arrays),
        in_specs=[HBM] * n + [SEM, SEM, ANY], out_specs=tuple([HBM] * n),
        input_output_aliases={q: q for q in range(n)},
        compiler_params=pltpu.CompilerParams(has_side_effects=EFFECT),
    )(*arrays, send, recv, after)


def _exchange_wait_start(name, started, after, build_wait, ncopies, build_start):
    send, recv, arrays = started
    n = len(arrays)

    def body(*refs):
        ins, send_sems, recv_sems = refs[:n], refs[n], refs[n + 1]
        send2, recv2, token = refs[n + 3], refs[n + 4], refs[-1]
        arrived = build_wait(ins, send_sems, recv_sems)
        for cp, onward in zip(arrived, build_start(ins, send2, recv2)):
            cp.wait_recv()
            onward.start()
        for cp in arrived:
            cp.wait_send()
        token[...] = jnp.zeros_like(token)

    outs = pl.pallas_call(
        body, name=name,
        out_shape=(pltpu.SemaphoreType.DMA((ncopies,)), pltpu.SemaphoreType.DMA((ncopies,)),
                   *[pltpu.HBM(a.shape, a.dtype) for a in arrays], jax.ShapeDtypeStruct((SUBLANES, LANES), F32)),
        in_specs=[HBM] * n + [SEM, SEM, ANY], out_specs=(SEM, SEM, *[HBM] * n, pl.BlockSpec(memory_space=pltpu.VMEM)),
        input_output_aliases={q: q + 2 for q in range(n)},
        compiler_params=pltpu.CompilerParams(has_side_effects=EFFECT),
    )(*arrays, send, recv, after)
    return (outs[0], outs[1], list(outs[2:2 + n])), outs[-1]


def _cast_into_slot(w, kc, name, dtype=BF16, token=None):
    rows, cols = w.shape
    tr = min(rows, 4 * SUM_TILE)
    extra = [] if token is None else [token]

    def body(kc_ref, w_ref, *rest):
        rest[-1][...] = w_ref[...].astype(dtype)

    grid_spec = pltpu.PrefetchScalarGridSpec(
        num_scalar_prefetch=1, grid=(rows // tr,),
        in_specs=[pl.BlockSpec((tr, cols), lambda r, kc: (r, 0))]
                 + [pl.BlockSpec((SUBLANES, LANES), lambda r, kc: (0, 0))] * len(extra),
        out_specs=pl.BlockSpec((None, tr, cols), lambda r, kc: (kc[0], r, 0)))
    return pl.pallas_call(
        body, name=name, grid_spec=grid_spec, out_shape=jax.ShapeDtypeStruct((N_CHIPS, rows, cols), dtype),
        compiler_params=_params("arbitrary"),
    )(kc, w, *extra)


def _gather_ici_copies(n):
    def build(refs, send_sems, recv_sems):
        x, y, c = _place()
        mine = lambda b: refs[b].at[_chip_of(x, y), c]
        chips = [(1 - x, y), (x, 1 - y), (1 - x, 1 - y)]
        return [pltpu.make_async_remote_copy(
            src_ref=mine(b), dst_ref=mine(b), send_sem=send_sems.at[3 * b + j], recv_sem=recv_sems.at[3 * b + j],
            device_id=(*chip, c), device_id_type=MESH) for b in range(n) for j, chip in enumerate(chips)]
    return build


def _gather_direct_copies(n):
    def build(refs, send_sems, recv_sems):
        x, y, c = _place()
        mine = lambda b: refs[b].at[_chip_of(x, y)]
        chips = [(1 - x, y), (x, 1 - y), (1 - x, 1 - y)]
        return [pltpu.make_async_remote_copy(
            src_ref=mine(b), dst_ref=mine(b), send_sem=send_sems.at[3 * b + j], recv_sem=recv_sems.at[3 * b + j],
            device_id=(*chip, c), device_id_type=MESH) for b in range(n) for j, chip in enumerate(chips)]
    return build


def _gather_relay_copies(n):
    def build(refs, send_sems, recv_sems):
        x, y, c = _place()
        chips = [(1 - x, y), (x, 1 - y), (1 - x, 1 - y)]
        cps = []
        for b in range(n):
            for j, chip in enumerate(chips):
                got = refs[b].at[_chip_of(*chip), c]
                cps.append(pltpu.make_async_remote_copy(
                    src_ref=got, dst_ref=got, send_sem=send_sems.at[3 * b + j], recv_sem=recv_sems.at[3 * b + j],
                    device_id=(x, y, 1 - c), device_id_type=MESH))
        return cps
    return build


def _sibling_copies(n):
    def build(refs, send_sems, recv_sems):
        x, y, c = _place()
        return [pltpu.make_async_remote_copy(
            src_ref=refs[b].at[:, 1 - c], dst_ref=refs[n + b], send_sem=send_sems.at[b], recv_sem=recv_sems.at[b],
            device_id=(x, y, 1 - c), device_id_type=MESH) for b in range(n)]
    return build


def _chip_copies(n):
    def build(refs, send_sems, recv_sems):
        x, y, c = _place()
        chips = [(1 - x, y), (x, 1 - y), (1 - x, 1 - y)]
        return [pltpu.make_async_remote_copy(
            src_ref=refs[b].at[_chip_of(*chip)], dst_ref=refs[n + b].at[j],
            send_sem=send_sems.at[3 * b + j], recv_sem=recv_sems.at[3 * b + j],
            device_id=(*chip, c), device_id_type=MESH) for b in range(n) for j, chip in enumerate(chips)]
    return build


def _finish_copies(n, n_all):
    def build(refs, send_sems, recv_sems):
        x, y, c = _place()
        cps = [pltpu.make_async_remote_copy(
            src_ref=refs[b].at[c], dst_ref=refs[b].at[c], send_sem=send_sems.at[b], recv_sem=recv_sems.at[b],
            device_id=(x, y, 1 - c), device_id_type=MESH) for b in range(n)]
        flips = [(fx, fy, fc) for fx in (0, 1) for fy in (0, 1) for fc in (0, 1)][1:]
        for b in range(n_all):
            mine = refs[n + b].at[_chip_of(x, y), c]
            cps += [pltpu.make_async_remote_copy(
                src_ref=mine, dst_ref=mine, send_sem=send_sems.at[n + 7 * b + q], recv_sem=recv_sems.at[n + 7 * b + q],
                device_id=(x ^ fx, y ^ fy, c ^ fc), device_id_type=MESH) for q, (fx, fy, fc) in enumerate(flips)]
        return cps
    return build


def _pair_sum(g, r1, kc, name, tr, send_dtype):
    nk, _, rows, cols = g.shape

    def body(kc_ref, g_ref, r_ref, p_ref, own_ref):
        s = g_ref[...] + r_ref[...]
        p_ref[...] = s.astype(send_dtype)

        @pl.when(pl.program_id(1) == kc_ref[0])
        def _():
            own_ref[...] = s

    grid_spec = pltpu.PrefetchScalarGridSpec(
        num_scalar_prefetch=1, grid=(rows // tr, nk),
        in_specs=[pl.BlockSpec((None, None, tr, cols), lambda r, k, kc: (k, kc[1], r, 0)),
                  pl.BlockSpec((None, tr, cols), lambda r, k, kc: (k, r, 0))],
        out_specs=[pl.BlockSpec((None, tr, cols), lambda r, k, kc: (k, r, 0)),
                   pl.BlockSpec((tr, cols), lambda r, k, kc: (r, 0))])
    return pl.pallas_call(
        body, name=name, grid_spec=grid_spec,
        out_shape=[jax.ShapeDtypeStruct((nk, rows, cols), send_dtype), jax.ShapeDtypeStruct((rows, cols), F32)],
        compiler_params=_params("arbitrary", "arbitrary"),
    )(kc, g, r1)


def _chip_sum(own, r2, slot, lead, name, tr):
    rows, cols = own.shape
    nl = len(lead)

    def body(slot_ref, o_ref, r_ref, s_ref):
        s = o_ref[...]
        for j in range(3):
            s = s + r_ref[j].astype(F32)
        s_ref[...] = s

    grid_spec = pltpu.PrefetchScalarGridSpec(
        num_scalar_prefetch=1, grid=(rows // tr,),
        in_specs=[pl.BlockSpec((tr, cols), lambda r, sl: (r, 0)), pl.BlockSpec((3, tr, cols), lambda r, sl: (0, r, 0))],
        out_specs=pl.BlockSpec((None,) * nl + (tr, cols), lambda r, sl: tuple(sl[q] for q in range(nl)) + (r, 0)))
    return pl.pallas_call(
        body, name=name, grid_spec=grid_spec, out_shape=jax.ShapeDtypeStruct(tuple(lead) + (rows, cols), F32),
        compiler_params=_params("arbitrary"),
    )(slot, own, r2)


def _adam_update(w, g, m, v):
    nm = ADAM_B1 * m + (1.0 - ADAM_B1) * g
    nv = ADAM_B2 * v + (1.0 - ADAM_B2) * (g * g)
    m_hat = nm / (1.0 - ADAM_B1 ** ADAM_STEP)
    v_hat = nv / (1.0 - ADAM_B2 ** ADAM_STEP)
    return -ADAM_LR * (m_hat / (jnp.sqrt(v_hat) + ADAM_EPS) + ADAM_WD * w), nm, nv


def _adamw(w, g, m, v, name, tr, token):
    rows, cols = w.shape

    def body(w_ref, g_ref, m_ref, v_ref, token_ref, go_ref, d_ref, nm_ref, nv_ref):
        gv = g_ref[...]
        go_ref[...] = gv
        d_ref[...], nm_ref[...], nv_ref[...] = _adam_update(w_ref[...], gv, m_ref[...], v_ref[...])

    spec = pl.BlockSpec((tr, cols), lambda r: (r, 0))
    return pl.pallas_call(
        body, name=name, grid=(rows // tr,),
        in_specs=[spec] * 4 + [pl.BlockSpec((SUBLANES, LANES), lambda r: (0, 0))], out_specs=[spec] * 4,
        out_shape=[jax.ShapeDtypeStruct((rows, cols), F32)] * 4,
        compiler_params=_params("parallel"),
    )(w, g, m, v, token)


def _adamw_small(packed_g, pre_g_parts, ws, ms, vs):
    names = ["pre_g"] + [n for n, _ in SMALL_ROWS if n != "conv_w"]
    rows = dict(SMALL_ROWS)
    offset, at = {}, 0
    for n, r in SMALL_ROWS:
        offset[n] = at
        at += r
    k = len(names)

    def body(*refs):
        g_ref, pg_ref = refs[0], refs[1]
        w_refs, m_refs, v_refs = refs[2:2 + k], refs[2 + k:2 + 2 * k], refs[2 + 2 * k:2 + 3 * k]
        outs = refs[2 + 3 * k:]
        go, do, mo, vo = outs[:k], outs[k:2 * k], outs[2 * k:3 * k], outs[3 * k:4 * k]
        pre = pg_ref[0]
        for dev in range(1, 8):
            pre = pre + pg_ref[dev]
        outs[4 * k][...] = pre[D_MODEL // LANES:, :]
        for i, n in enumerate(names):
            shp = w_refs[i].shape
            if len(shp) == 2 and shp[0] == 1:
                for r in range(shp[1] // LANES):
                    cols = slice(r * LANES, (r + 1) * LANES)
                    g = pre[r:r + 1, :] if n == "pre_g" else g_ref[offset[n] + r:offset[n] + r + 1, :]
                    go[i][:, cols] = g
                    do[i][:, cols], mo[i][:, cols], vo[i][:, cols] = _adam_update(
                        w_refs[i][:, cols], g, m_refs[i][:, cols], v_refs[i][:, cols])
            else:
                g = g_ref[offset[n]:offset[n] + rows[n], :].reshape(shp)
                go[i][...] = g
                do[i][...], mo[i][...], vo[i][...] = _adam_update(w_refs[i][...], g, m_refs[i][...], v_refs[i][...])

    vm = pl.BlockSpec(memory_space=pltpu.VMEM)
    args = [packed_g, pre_g_parts] + [src[n] for src in (ws, ms, vs) for n in names]
    out_shape = [jax.ShapeDtypeStruct(ws[n].shape, F32) for _ in range(4) for n in names]
    out_shape.append(jax.ShapeDtypeStruct((SUBLANES, LANES), F32))
    outs = pl.pallas_call(
        body, name="adamw_small", in_specs=[vm] * len(args), out_specs=[vm] * (4 * k + 1), out_shape=out_shape,
    )(*args)
    return [dict(zip(names, outs[q * k:(q + 1) * k])) for q in range(4)], outs[4 * k]


def _into_slot(v, tail, slot, lead, name):
    n = v.shape[1]
    nl = len(lead)
    rows = n // LANES + SUBLANES

    def body(slot_ref, v_ref, t_ref, o_ref):
        for r in range(n // LANES):
            o_ref[r:r + 1, :] = v_ref[0:1, r * LANES:(r + 1) * LANES]
        o_ref[n // LANES:, :] = t_ref[...]

    grid_spec = pltpu.PrefetchScalarGridSpec(
        num_scalar_prefetch=1, grid=(1,),
        in_specs=[pl.BlockSpec(v.shape, lambda i, sl: (0, 0)), pl.BlockSpec(tail.shape, lambda i, sl: (0, 0))],
        out_specs=pl.BlockSpec((None,) * nl + (rows, LANES), lambda i, sl: tuple(sl[q] for q in range(nl)) + (0, 0)))
    return pl.pallas_call(
        body, name=name, grid_spec=grid_spec, out_shape=jax.ShapeDtypeStruct(tuple(lead) + (rows, LANES), F32),
    )(slot, v, tail)


def _pack_small(parts):
    names = [n for n, _ in SMALL_ROWS]
    offset, at = {}, 0
    for n, r in SMALL_ROWS:
        offset[n] = at
        at += r

    def body(*refs):
        ins, o_ref = dict(zip(names, refs[:-1])), refs[-1]
        o_ref[SMALL_USED:, :] = jnp.zeros((SMALL_TOTAL - SMALL_USED, LANES), F32)
        for n, rows in SMALL_ROWS:
            ref, at = ins[n], offset[n]
            if n == "gmlp_bs":
                for h in range(N_HEADS):
                    o_ref[at + h:at + h + 1, :] = jnp.transpose(ref[:, h * CHUNK:(h + 1) * CHUNK])[0:1, :]
            elif n == "conv_w":
                for k in range(CONV_W):
                    for r in range(D_HALF // LANES):
                        row = at + k * (D_HALF // LANES) + r
                        o_ref[row:row + 1, :] = ref[k * SUBLANES:k * SUBLANES + 1, r * LANES:(r + 1) * LANES]
            elif ref.ndim == 3:
                o_ref[at:at + rows, :] = ref[...].reshape(rows, LANES)
            else:
                for r in range(rows):
                    o_ref[at + r:at + r + 1, :] = ref[0:1, r * LANES:(r + 1) * LANES]

    vm = pl.BlockSpec(memory_space=pltpu.VMEM)
    return pl.pallas_call(
        body, name="pack_small", in_specs=[vm] * len(names), out_specs=vm,
        out_shape=jax.ShapeDtypeStruct((SMALL_TOTAL, LANES), F32),
    )(*[parts[n] for n in names])


def kernel(x, p, pre_g, w_in, gmlp_ln_g, gmlp_ln_b, gmlp_ws, gmlp_bs, conv_w, conv_b, w_a, b_a, w_x, b_x, lam, gmlp_out_g, lru_out_g, w_out, post_g, w_pe, w_pg, loss_target, m_pre_g, m_w_in, m_gmlp_ln_g, m_gmlp_ln_b, m_gmlp_ws, m_gmlp_bs, m_conv_w, m_conv_b, m_w_a, m_b_a, m_w_x, m_b_x, m_lam, m_gmlp_out_g, m_lru_out_g, m_w_out, m_post_g, m_w_pe, m_w_pg, v_pre_g, v_w_in, v_gmlp_ln_g, v_gmlp_ln_b, v_gmlp_ws, v_gmlp_bs, v_conv_w, v_conv_b, v_w_a, v_b_a, v_w_x, v_b_x, v_lam, v_gmlp_out_g, v_lru_out_g, v_w_out, v_post_g, v_w_pe, v_w_pg):
    weights = dict(pre_g=pre_g, w_in=w_in, gmlp_ln_g=gmlp_ln_g, gmlp_ln_b=gmlp_ln_b, gmlp_ws=gmlp_ws, gmlp_bs=gmlp_bs,
                   conv_w=conv_w, conv_b=conv_b, w_a=w_a, b_a=b_a, w_x=w_x, b_x=b_x, lam=lam, gmlp_out_g=gmlp_out_g,
                   lru_out_g=lru_out_g, w_out=w_out, post_g=post_g, w_pe=w_pe, w_pg=w_pg)
    mom_m = dict(pre_g=m_pre_g, w_in=m_w_in, gmlp_ln_g=m_gmlp_ln_g, gmlp_ln_b=m_gmlp_ln_b, gmlp_ws=m_gmlp_ws,
                 gmlp_bs=m_gmlp_bs, conv_w=m_conv_w, conv_b=m_conv_b, w_a=m_w_a, b_a=m_b_a, w_x=m_w_x, b_x=m_b_x,
                 lam=m_lam, gmlp_out_g=m_gmlp_out_g, lru_out_g=m_lru_out_g, w_out=m_w_out, post_g=m_post_g,
                 w_pe=m_w_pe, w_pg=m_w_pg)
    mom_v = dict(pre_g=v_pre_g, w_in=v_w_in, gmlp_ln_g=v_gmlp_ln_g, gmlp_ln_b=v_gmlp_ln_b, gmlp_ws=v_gmlp_ws,
                 gmlp_bs=v_gmlp_bs, conv_w=v_conv_w, conv_b=v_conv_b, w_a=v_w_a, b_a=v_b_a, w_x=v_w_x, b_x=v_b_x,
                 lam=v_lam, gmlp_out_g=v_gmlp_out_g, lru_out_g=v_lru_out_g, w_out=v_w_out, post_g=v_post_g,
                 w_pe=v_w_pe, w_pg=v_w_pg)
    order = list(weights)
    xi, yi, ci = _place()
    me = _chip_of(xi, yi)
    kc = jnp.stack([me, ci]).astype(jnp.int32)

    x2 = x[0]
    p2 = p[0, 0]
    tgt = loss_target[0]

    first = [_cast_into_slot(w_in[0], kc, "cast_w_in").reshape(N_CHIPS, 2, D_MODEL // 2, W_IN_COLS),
             _cast_into_slot(conv_w[0, :, 0, :], kc, "conv_w_into_slot", F32).reshape(N_CHIPS, 2, CONV_W // 2, CONV_COLS)]
    in_st, in_tok = _exchange_start("gather_in_start", first, 6, _gather_ici_copies(2))
    later = [_cast_into_slot(w_out[0], kc, "cast_w_out", token=in_tok).reshape(N_CHIPS, 2, W_ROWS // 2, D_MODEL),
             _cast_into_slot(w_pg[0], kc, "cast_w_pg", token=in_tok).reshape(N_CHIPS, 2, W_ROWS // 2, D_MODEL),
             _cast_into_slot(w_pe[0], kc, "cast_w_pe", token=in_tok).reshape(N_CHIPS, 2, D_PLE // 2, W_PE_COLS)]
    gather_st, gather_tok = _exchange_start("gather_start", later, 9, _gather_direct_copies(3), after=in_tok)
    hn, z_own, hn_t = _inproj_local(x2, pre_g, w_in[0], ROW_TILE, gather_tok)
    in_st, in_tok = _exchange_wait_start("gather_in_relay", in_st, z_own, _gather_ici_copies(2), 6,
                                         _gather_relay_copies(2))
    g_in, g_cw = _exchange_wait("gather_in_wait", in_st, in_tok, _gather_relay_copies(2))
    wg_in = g_in.reshape(N_CHIPS, D_MODEL, W_IN_COLS)
    cw_full = jnp.transpose(g_cw.reshape(N_CHIPS, CONV_W, CONV_COLS), (1, 0, 2)).reshape(CONV_W, D_HALF)

    causal = jnp.tril(jnp.ones((CHUNK, CHUNK), dtype=bool))
    ws_m = jnp.where(causal[None], gmlp_ws[0], 0.0)
    prm = dict(
        ln_g=gmlp_ln_g, ln_b=gmlp_ln_b, wt=ws_m.astype(BF16), wtt=jnp.transpose(ws_m, (0, 2, 1)).astype(BF16),
        bsx=jnp.repeat(jnp.transpose(gmlp_bs[0]), CHUNK, axis=1),
        conv_w=cw_full, conv_b=conv_b, w_a=w_a[0].astype(BF16), w_x=w_x[0].astype(BF16),
        b_a=b_a[0].reshape(1, D_HALF), b_x=b_x[0].reshape(1, D_HALF), lam=lam, oga=gmlp_out_g, ogb=lru_out_g)

    z, y, h = _inproj_branches_fwd(hn, z_own, wg_in, kc, prm, ROW_TILE, gather_tok)
    g_out, g_pg, g_pe = _exchange_wait("gather_wait", gather_st, y, _gather_direct_copies(3))
    wg_out = g_out.reshape(D_MODEL, D_MODEL)
    wg_pg = g_pg.reshape(D_MODEL, D_MODEL)
    wg_pe = g_pe.reshape(N_CHIPS, D_PLE, W_PE_COLS)
    o, h1, gt, dout, loss_acc = _outproj_fwd(x2, y, p2, tgt, post_g, wg_out, wg_pg, wg_pe, ROW_TILE)

    def sibling_start(tag, bufs):
        lands = [_landing((b.shape[0],) + b.shape[2:], b.dtype) for b in bufs]
        return _exchange_start("sibling_start_" + tag, bufs + lands, len(bufs), _sibling_copies(len(bufs)))

    def pair_then_chip_start(tag, started, after, names, tiles, dtypes):
        n = len(names)
        got = _exchange_wait("sibling_wait_" + tag, started, after, _sibling_copies(n))
        pairs = [_pair_sum(got[b], got[n + b], kc, "pair_sum_" + names[b], tiles[b], dtypes[b]) for b in range(n)]
        lands = [_landing((3,) + pr[0].shape[1:], pr[0].dtype) for pr in pairs]
        return _exchange_start("chip_start_" + tag, [pr[0] for pr in pairs] + lands, 3 * n, _chip_copies(n)), pairs

    def sum_then_finish_start(tag, started, pairs, after, names, tiles, small, to_all=()):
        n = len(names)
        got = _exchange_wait("chip_wait_" + tag, started, after, _chip_copies(n))
        sums = [_chip_sum(pairs[b][1], got[n + b], kc if small and b == n - 1 else kc[1:],
                          (N_CHIPS, 2) if small and b == n - 1 else (2,), "chip_sum_" + names[b], tiles[b])
                for b in range(n)]
        nbig = n - 1 if small else n
        n_all = n - nbig + len(to_all)
        return _exchange_start("finish_start_" + tag, sums + list(to_all), nbig + 7 * n_all,
                               _finish_copies(nbig, n_all))

    gw_pe, dq, dh1, do, dy, g_post = _head_bwd(dout, gt, p2, o, post_g, wg_out, wg_pg, wg_pe, ROW_TILE)
    gw_pe = gw_pe.reshape(N_CHIPS, 2, D_PLE // 2, W_PE_COLS)
    token0 = jnp.zeros((SUBLANES, LANES), F32)
    gw_out = _weight_grad(y, do, "grad_w_out", 2, 1, D_MODEL // 2, D_MODEL, CONTRACT_TILE, token0)
    gw_pg = _weight_grad(h1, dq, "grad_w_pg", 2, 1, D_MODEL // 2, D_MODEL, CONTRACT_TILE, token0)
    gw_out = gw_out.reshape(N_CHIPS, 2, W_ROWS // 2, D_MODEL)
    gw_pg = gw_pg.reshape(N_CHIPS, 2, W_ROWS // 2, D_MODEL)

    names_a, tiles_a = ["w_out", "w_pg", "w_pe"], [SUM_TILE] * 3
    st, tok = sibling_start("a", [gw_out, gw_pg, gw_pe])
    (dz, g_oga, g_ogb, g_lng, g_lnb, g_bsx, g_ws, g_cw, g_cb, g_wa, g_ba, g_wx, g_bx, g_lam) = _branches_bwd(
        z, h, dy, prm, ROW_TILE, tok)
    (st, tok), pairs_a = pair_then_chip_start("a", st, dz, names_a, tiles_a, [BF16] * 3)
    gw_in = _weight_grad(hn_t, dz, "grad_w_in", 2, N_CHIPS, D_MODEL // 2, W_IN_COLS, CONTRACT_TILE, tok,
                         a_transposed=True)
    fin_a, tok = sum_then_finish_start("a", st, pairs_a, gw_in, names_a, tiles_a, False)

    small_g = dict(
        gmlp_ln_g=g_lng, gmlp_ln_b=g_lnb, gmlp_ws=g_ws, gmlp_bs=g_bsx, conv_w=g_cw, conv_b=g_cb, w_a=g_wa, b_a=g_ba,
        w_x=g_wx, b_x=g_bx, lam=g_lam, gmlp_out_g=g_oga, lru_out_g=g_ogb, post_g=g_post)
    gsm = _pack_small(small_g).reshape(N_CHIPS, 2, SMALL_PIECE, LANES)

    names_b, tiles_b = ["w_in", "small"], [2 * SUM_TILE, SMALL_PIECE]
    n_tiles = x2.shape[0] // ROW_TILE
    n_lo = max(1, (5 * n_tiles) // 16)
    st, tok_b = _exchange_start(
        "sibling_start_b", [gw_in, gsm] + [_landing((N_CHIPS,) + b.shape[2:], F32) for b in (gw_in, gsm)], 2,
        _sibling_copies(2), after=tok)
    part = _inproj_bwd(dz, wg_in, x2, dh1, pre_g, ROW_TILE, 0, n_lo, None, False, tok_b, "inproj_bwd_lo")
    f_out, f_pg, f_pe = _exchange_wait("finish_wait_a", fin_a, part[1], _finish_copies(3, 0))
    (st, tok_b), pairs_b = pair_then_chip_start("b", st, part[1], names_b, tiles_b, [BF16, F32])
    grad_x, g_pre = _inproj_bwd(dz, wg_in, x2, dh1, pre_g, ROW_TILE, n_lo, n_tiles - n_lo, part, True, tok_b,
                                "inproj_bwd_hi")
    pre_parts = _into_slot(g_pre, loss_acc, kc, (N_CHIPS, 2), "pre_g_into_slot")
    fin_b, tok_b = sum_then_finish_start("b", st, pairs_b, g_pre, names_b, tiles_b, True, to_all=[pre_parts])

    grads, deltas, new_m, new_v = {}, {}, {}, {}

    def adam_big(n, g2d, tr, token):
        shp = weights[n].shape
        g, d, nm, nv = _adamw(weights[n][0], g2d, mom_m[n][0], mom_v[n][0], "adamw_" + n, tr, token)
        grads[n], deltas[n], new_m[n], new_v[n] = g.reshape(shp), d.reshape(shp), nm.reshape(shp), nv.reshape(shp)
        return d

    as_token = lambda d: d[:SUBLANES, :LANES]
    last = adam_big("w_out", f_out.reshape(W_ROWS, D_MODEL), SUM_TILE, tok_b)
    last = adam_big("w_pg", f_pg.reshape(W_ROWS, D_MODEL), SUM_TILE, as_token(last))
    last = adam_big("w_pe", f_pe.reshape(D_PLE, W_PE_COLS), SUM_TILE, as_token(last))
    f_in, f_sm, pre_parts = _exchange_wait("finish_wait_b", fin_b, last, _finish_copies(1, 2))
    adam_big("w_in", f_in.reshape(D_MODEL, W_IN_COLS), 2 * SUM_TILE, tok_b)

    packed_g = f_sm.reshape(SMALL_TOTAL, LANES)
    small_names = ["pre_g"] + [n for n, _ in SMALL_ROWS if n != "conv_w"]
    natural = lambda src: {n: (src[n] if src[n].ndim == 2 else src[n][0]) for n in small_names}
    outs, loss_block = _adamw_small(packed_g, pre_parts.reshape(8, D_MODEL // LANES + SUBLANES, LANES),
                                    natural(weights), natural(mom_m), natural(mom_v))
    loss = loss_block[0, 0]
    for dst, got in zip((grads, deltas, new_m, new_v), outs):
        for n in small_names:
            dst[n] = got[n].reshape(weights[n].shape)
    at = sum(r for n, r in SMALL_ROWS[:[n for n, _ in SMALL_ROWS].index("conv_w")])
    g_cw_all = packed_g[at:at + CONV_W * D_HALF // LANES].reshape(CONV_W, D_HALF)
    g_conv = lax.dynamic_slice_in_dim(g_cw_all, me * CONV_COLS, CONV_COLS, axis=1)
    g, d, nm, nv = _adamw(conv_w[0, :, 0, :], g_conv, m_conv_w[0, :, 0, :], v_conv_w[0, :, 0, :], "adamw_conv_w", CONV_W,
                          tok_b)
    cshape = conv_w.shape
    grads["conv_w"], deltas["conv_w"] = g.reshape(cshape), d.reshape(cshape)
    new_m["conv_w"], new_v["conv_w"] = nm.reshape(cshape), nv.reshape(cshape)

    return (loss, grad_x.reshape(x.shape), *[grads[n] for n in order], *[deltas[n] for n in order],
            *[new_m[n] for n in order], *[new_v[n] for n in order])
```

```python
import math

import jax
import jax.numpy as jnp
from jax import lax
from jax.experimental import pallas as pl
from jax.experimental.pallas import tpu as pltpu

F32 = jnp.float32
BF16 = jnp.bfloat16

D_MODEL = 2048
D_HALF = 1024
D_Z = 5120
D_PLE = 256
CHUNK = 128
N_HEADS = 8
N_CHIPS = 4
W_IN_COLS = D_Z // N_CHIPS
W_ROWS = D_MODEL // N_CHIPS
W_PE_COLS = D_MODEL // N_CHIPS
CONV_W = 4
CONV_COLS = D_HALF // N_CHIPS
EPS = 1e-6
LRU_C = 8.0
ADAM_LR, ADAM_B1, ADAM_B2, ADAM_EPS, ADAM_WD, ADAM_STEP = 0.001, 0.9, 0.999, 1e-08, 0.01, 10

SUBLANES = 8
LANES = 128
VMEM_LIMIT = 56 * 1024 * 1024
ROW_TILE = 256
CONTRACT_TILE = 2048
SUM_TILE = 128

SMALL_ROWS = (("gmlp_ln_g", 8), ("gmlp_ln_b", 8), ("gmlp_ws", 1024), ("gmlp_bs", 8),
              ("conv_w", 32), ("conv_b", 8), ("w_a", 1024), ("b_a", 8), ("w_x", 1024), ("b_x", 8),
              ("lam", 8), ("gmlp_out_g", 8), ("lru_out_g", 8), ("post_g", 16))
SMALL_USED = sum(r for _, r in SMALL_ROWS)
SMALL_PIECE = 400
SMALL_TOTAL = 8 * SMALL_PIECE

MESH = pl.DeviceIdType.MESH
ANY = pl.BlockSpec(memory_space=pl.ANY)

_GELU_C0 = math.sqrt(2.0 / math.pi)
_GELU_C1 = 0.044715


def _params(*sem):
    return pltpu.CompilerParams(dimension_semantics=sem, vmem_limit_bytes=VMEM_LIMIT)


def _dot(a, b):
    return jnp.dot(a, b, preferred_element_type=F32)


def _dot_nt(a, b):
    return lax.dot_general(a, b, (((1,), (1,)), ((), ())), preferred_element_type=F32)


def _dot_tn(a, b):
    return lax.dot_general(a, b, (((0,), (0,)), ((), ())), preferred_element_type=F32)


def _gelu(x):
    t = jnp.tanh(_GELU_C0 * (x + _GELU_C1 * (x * x * x)))
    return 0.5 * x * (1.0 + t), t


def _gelu_grad(x, t):
    return 0.5 * (1.0 + t) + 0.5 * x * (1.0 - t * t) * (_GELU_C0 * (1.0 + 3.0 * _GELU_C1 * x * x))


def _rowsum8(v):
    r, n = v.shape
    return jnp.sum(v.reshape(r // SUBLANES, SUBLANES, n), axis=0)


def _lanemean(v):
    return jnp.mean(v, axis=-1, keepdims=True)


def _shift_down(v, halo8, k):
    if k == 0:
        return v
    r = pltpu.roll(v, k, 0)
    hr = pltpu.roll(halo8, k, 0)
    row = lax.broadcasted_iota(jnp.int32, halo8.shape, 0)
    top = jnp.where(row < k, hr, r[0:SUBLANES])
    return jnp.concatenate([top, r[SUBLANES:]], axis=0)


def _shift_up(v, next8, k):
    if k == 0:
        return v
    n = v.shape[0]
    r = pltpu.roll(v, n - k, 0)
    nr = pltpu.roll(next8, SUBLANES - k, 0)
    row = lax.broadcasted_iota(jnp.int32, next8.shape, 0)
    bot = jnp.where(row >= SUBLANES - k, nr, r[n - SUBLANES:])
    return jnp.concatenate([r[:n - SUBLANES], bot], axis=0)


def _layernorm_parts(vg):
    mu = _lanemean(vg)
    xc = vg - mu
    rstd = lax.rsqrt(_lanemean(xc * xc) + EPS)
    return xc * rstd, rstd


def _spatial_mix(wt_ref, vn_ref, bsx_ref, mixed_ref, tm):
    for c in range(tm // CHUNK):
        rows = slice(c * CHUNK, (c + 1) * CHUNK)
        for h in range(N_HEADS):
            cols = slice(h * CHUNK, (h + 1) * CHUNK)
            mixed_ref[rows, cols] = _dot(wt_ref[h], vn_ref[rows, cols]) + bsx_ref[:, cols]


def _conv_taps(xb, halo8):
    return [_shift_down(xb, halo8, CONV_W - 1 - k) for k in range(CONV_W)]


def _lru_gates(xc_bf_ref, wa_ref, wx_ref, ba_ref, bx_ref, r_ref, i_ref):
    for h in range(N_HEADS):
        cols = slice(h * CHUNK, (h + 1) * CHUNK)
        xh = xc_bf_ref[:, cols]
        r_ref[:, cols] = jax.nn.sigmoid(_dot(xh, wa_ref[h]) + ba_ref[:, cols])
        i_ref[:, cols] = jax.nn.sigmoid(_dot(xh, wx_ref[h]) + bx_ref[:, cols])


def _softplus_neg(lam):
    return jnp.maximum(-lam, 0.0) + jnp.log(1.0 + jnp.exp(-jnp.abs(lam)))


def _decay_parts(r, lam):
    la = (-LRU_C * _softplus_neg(lam)) * r
    a = jnp.exp(la)
    th = -jnp.tanh(la)
    mult = jnp.sqrt(2.0 * th / (1.0 + th))
    return a, mult


def _z_group(zref, g, rows=slice(None)):
    lo = g * D_HALF
    blk, off = lo // W_IN_COLS, lo % W_IN_COLS
    if off + D_HALF <= W_IN_COLS:
        return zref[blk, rows, off:off + D_HALF]
    return jnp.concatenate([zref[blk, rows, off:W_IN_COLS], zref[blk + 1, rows, 0:off + D_HALF - W_IN_COLS]], axis=1)


def _inproj_local(x, pre_g, w_own, tm, token):
    t = x.shape[0]

    def body(x_ref, g_ref, w_ref, token_ref, hn_ref, zl_ref, hnt_ref, wbf_s):
        @pl.when(pl.program_id(0) == 0)
        def _():
            wbf_s[...] = w_ref[...].astype(BF16)

        xv = x_ref[...]
        hnf = xv * lax.rsqrt(_lanemean(xv * xv) + EPS) * g_ref[...]
        hn = hnf.astype(BF16)
        hn_ref[...] = hn
        hnt_ref[...] = hnf.T.astype(BF16)
        zl_ref[...] = _dot(hn, wbf_s[...]).astype(BF16)

    row = lambda n: pl.BlockSpec((tm, n), lambda i: (i, 0))
    const = lambda shp: pl.BlockSpec(shp, lambda i: (0, 0), pipeline_mode=pl.Buffered(1))
    return pl.pallas_call(
        body, name="inproj_local", grid=(t // tm,),
        in_specs=[row(D_MODEL), const((1, D_MODEL)), const((D_MODEL, W_IN_COLS)), const((SUBLANES, LANES))],
        out_specs=[row(D_MODEL), row(W_IN_COLS), pl.BlockSpec((D_MODEL, tm), lambda i: (0, i))],
        out_shape=[jax.ShapeDtypeStruct((t, D_MODEL), BF16), jax.ShapeDtypeStruct((t, W_IN_COLS), BF16),
                   jax.ShapeDtypeStruct((D_MODEL, t), BF16)],
        scratch_shapes=[pltpu.VMEM((D_MODEL, W_IN_COLS), BF16)],
        compiler_params=_params("arbitrary"),
    )(x, pre_g, w_own, token)


def _inproj_branches_fwd(hn, z_own, wg_in, kc, prm, tm, token):
    t = hn.shape[0]
    nt = t // tm
    hb = tm // SUBLANES

    def body(kc_ref, hn_ref, zo_ref, w1_ref, w2_ref, w3_ref,
             lng_ref, lnb_ref, wt_ref, bsx_ref, cw_ref, cb_ref, wa_ref, wx_ref, ba_ref, bx_ref, lam_ref,
             oga_ref, ogb_ref, token_ref,
             z_ref, y_ref, h_ref,
             zbuf0, zbuf1, vn_s, mixed_s, xcbf_s, r_s, i_s, ug_s, halo_s, carry_s):
        s = pl.program_id(0)
        me = kc_ref[0]
        w_refs = (None, w1_ref, w2_ref, w3_ref)

        @pl.when(s == 0)
        def _():
            zbuf1[...] = jnp.zeros_like(zbuf1)

        @pl.when(s <= 1)
        def _():
            carry_s[...] = jnp.zeros_like(carry_s)
            halo_s[...] = jnp.zeros_like(halo_s)

        def step(zw, zr):
            def project(r):
                blk = (me + r) % N_CHIPS
                zb = zo_ref[...] if r == 0 else _dot(hn_ref[...], w_refs[r][...]).astype(BF16)
                z_ref[blk] = zb
                zw[blk] = zb

            zin = lambda g: _z_group(zr, g).astype(F32)
            always = [s >= 0] * 4

            @pl.when(always[0])
            def _():
                project(0)
                ug, _ = _gelu(zin(0))
                ug_s[...] = ug
                vg, _ = _gelu(zin(1))
                vhat, _ = _layernorm_parts(vg)
                vn_s[...] = (vhat * lng_ref[...] + lnb_ref[...]).astype(BF16)

            @pl.when(always[1])
            def _():
                project(1)
                _spatial_mix(wt_ref, vn_s, bsx_ref, mixed_s, tm)
                ga = zin(2)
                ya = ug_s[...] * mixed_s[...] * (ga * jax.nn.sigmoid(ga))
                ra = lax.rsqrt(_lanemean(ya * ya) + EPS)
                y_ref[:, 0:D_HALF] = (ya * ra * oga_ref[...]).astype(BF16)

            @pl.when(always[2])
            def _():
                project(2)
                xb = zin(3)
                taps = _conv_taps(xb, halo_s[...])
                halo_s[...] = xb[tm - SUBLANES:]
                xc = cb_ref[...] + taps[0] * cw_ref[0:1, :]
                for k in range(1, CONV_W):
                    xc = xc + taps[k] * cw_ref[k:k + 1, :]
                xcbf_s[...] = xc.astype(BF16)
                _lru_gates(xcbf_s, wa_ref, wx_ref, ba_ref, bx_ref, r_s, i_s)
                a, mult = _decay_parts(r_s[...], lam_ref[...])
                row = lax.broadcasted_iota(jnp.int32, a.shape, 0)
                mult = jnp.where(jnp.logical_and(s == 1, row == 0), 1.0, mult)
                r_s[...] = a
                i_s[...] = mult * (i_s[...] * xc)

            @pl.when(always[3])
            def _():
                project(3)
                a = r_s[...]
                b = i_s[...]
                r8 = lax.broadcasted_iota(jnp.int32, a.shape, 0) & (SUBLANES - 1)
                for d in (1, 2, 4):
                    a_sh = pltpu.roll(a, d, 0)
                    b_sh = pltpu.roll(b, d, 0)
                    m = r8 >= d
                    b = jnp.where(m, a * b_sh + b, b)
                    a = jnp.where(m, a * a_sh, a)
                carry = carry_s[...]
                for g in range(hb):
                    rows = slice(g * SUBLANES, (g + 1) * SUBLANES)
                    hg = a[rows] * carry + b[rows]
                    h_ref[rows, :] = hg
                    carry = jnp.broadcast_to(hg[SUBLANES - 1:SUBLANES, :], hg.shape)
                carry_s[...] = carry
                gb = zin(4)
                yb = h_ref[...] * (gb * jax.nn.sigmoid(gb))
                rb = lax.rsqrt(_lanemean(yb * yb) + EPS)
                y_ref[:, D_HALF:] = (yb * rb * ogb_ref[...]).astype(BF16)

        @pl.when(s % 2 == 0)
        def _():
            step(zbuf0, zbuf1)

        @pl.when(s % 2 == 1)
        def _():
            step(zbuf1, zbuf0)

    const = lambda a: pl.BlockSpec(a.shape, lambda s, kc, n=a.ndim: (0,) * n, pipeline_mode=pl.Buffered(1))
    proj = lambda n: pl.BlockSpec((tm, n), lambda s, kc: (jnp.minimum(s, nt - 1), 0))
    head = lambda n: pl.BlockSpec((tm, n), lambda s, kc: (jnp.maximum(s - 1, 0), 0))
    other = lambda r: pl.BlockSpec((None, D_MODEL, W_IN_COLS), lambda s, kc, r=r: ((kc[0] + r) % N_CHIPS, 0, 0),
                                   pipeline_mode=pl.Buffered(1))
    names = ("ln_g", "ln_b", "wt", "bsx", "conv_w", "conv_b", "w_a", "w_x", "b_a", "b_x", "lam", "oga", "ogb")
    pr = [prm[n] for n in names] + [token]
    big = lambda dt: pltpu.VMEM((tm, D_HALF), dt)
    zblocks = pltpu.VMEM((N_CHIPS, tm, W_IN_COLS), BF16)
    grid_spec = pltpu.PrefetchScalarGridSpec(
        num_scalar_prefetch=1, grid=(nt + 1,),
        in_specs=[proj(D_MODEL), proj(W_IN_COLS), other(1), other(2), other(3)] + [const(a) for a in pr],
        out_specs=[pl.BlockSpec((N_CHIPS, tm, W_IN_COLS), lambda s, kc: (0, jnp.minimum(s, nt - 1), 0)),
                   head(D_MODEL), head(D_HALF)],
        scratch_shapes=[zblocks, zblocks, big(BF16), big(F32), big(BF16), big(F32), big(F32), big(F32),
                        pltpu.VMEM((SUBLANES, D_HALF), F32), pltpu.VMEM((SUBLANES, D_HALF), F32)])
    return pl.pallas_call(
        body, name="inproj_branches_fwd", grid_spec=grid_spec,
        out_shape=[jax.ShapeDtypeStruct((N_CHIPS, t, W_IN_COLS), BF16), jax.ShapeDtypeStruct((t, D_MODEL), BF16),
                   jax.ShapeDtypeStruct((t, D_HALF), F32)],
        compiler_params=_params("arbitrary"),
    )(kc, hn, z_own, wg_in, wg_in, wg_in, *pr)


def _outproj_fwd(x, y, p, tgt, post_g, w_out, w_pg, wg_pe, tm):
    t = x.shape[0]

    def body(x_ref, y_ref, p_ref, tgt_ref, pg_ref, wo_ref, wpg_ref, wpe_ref,
             o_ref, h1_ref, gt_ref, dout_ref, loss_ref):
        @pl.when(pl.program_id(0) == 0)
        def _():
            loss_ref[...] = jnp.zeros_like(loss_ref)

        o = _dot(y_ref[...], wo_ref[...])
        o_ref[...] = o
        r3 = lax.rsqrt(_lanemean(o * o) + EPS)
        h1 = x_ref[...] + (o * r3) * pg_ref[...]
        h1b = h1.astype(BF16)
        h1_ref[...] = h1b
        gt = jax.nn.sigmoid(_dot(h1b, wpg_ref[...]))
        gt_ref[...] = gt
        pb = p_ref[...].astype(BF16)
        for k in range(N_CHIPS):
            cols = slice(k * W_PE_COLS, (k + 1) * W_PE_COLS)
            pe = _dot(pb, wpe_ref[k])
            d = h1[:, cols] + pe * gt[:, cols] - tgt_ref[:, cols]
            dout_ref[:, cols] = d * (1.0 / D_MODEL)
            loss_ref[...] += jnp.sum(d * d) * (0.5 / D_MODEL)

    row = lambda n: pl.BlockSpec((tm, n), lambda i: (i, 0))
    const = lambda shp: pl.BlockSpec(shp, lambda i, n=len(shp): (0,) * n, pipeline_mode=pl.Buffered(1))
    return pl.pallas_call(
        body, name="outproj_fwd", grid=(t // tm,),
        in_specs=[row(D_MODEL), row(D_MODEL), row(D_PLE), row(D_MODEL), const((1, D_MODEL)),
                  const((D_MODEL, D_MODEL)), const((D_MODEL, D_MODEL)), const((N_CHIPS, D_PLE, W_PE_COLS))],
        out_specs=[row(D_MODEL), row(D_MODEL), row(D_MODEL), row(D_MODEL),
                   pl.BlockSpec((SUBLANES, LANES), lambda i: (0, 0))],
        out_shape=[jax.ShapeDtypeStruct((t, D_MODEL), F32), jax.ShapeDtypeStruct((t, D_MODEL), BF16),
                   jax.ShapeDtypeStruct((t, D_MODEL), F32), jax.ShapeDtypeStruct((t, D_MODEL), F32),
                   jax.ShapeDtypeStruct((SUBLANES, LANES), F32)],
        compiler_params=_params("arbitrary"),
    )(x, y, p, tgt, post_g, w_out, w_pg, wg_pe)


def _head_fwd_bwd(x, y, p, tgt, post_g, w_out, w_pg, wg_pe, tm):
    t = x.shape[0]

    def body(x_ref, y_ref, p_ref, tgt_ref, pg_ref, wo_ref, wpg_ref, wpe_ref,
             h1_ref, dq_ref, dh1_ref, do_ref, dy_ref, gwpe_ref, gpost_ref, loss_ref, dout_s):
        i = pl.program_id(0)

        @pl.when(i == 0)
        def _():
            gpost_ref[...] = jnp.zeros_like(gpost_ref)
            gwpe_ref[...] = jnp.zeros_like(gwpe_ref)
            loss_ref[...] = jnp.zeros_like(loss_ref)

        o = _dot(y_ref[...], wo_ref[...])
        r3 = lax.rsqrt(_lanemean(o * o) + EPS)
        on = o * r3
        h1 = x_ref[...] + on * pg_ref[...]
        h1b = h1.astype(BF16)
        h1_ref[...] = h1b
        gt = jax.nn.sigmoid(_dot(h1b, wpg_ref[...]))
        pb = p_ref[...].astype(BF16)
        for k in range(N_CHIPS):
            cols = slice(k * W_PE_COLS, (k + 1) * W_PE_COLS)
            pe = _dot(pb, wpe_ref[k])
            g = gt[:, cols]
            d = h1[:, cols] + pe * g - tgt_ref[:, cols]
            loss_ref[...] += jnp.sum(d * d) * (0.5 / D_MODEL)
            dout = d * (1.0 / D_MODEL)
            dout_s[:, cols] = dout
            dg = dout * g
            gwpe_ref[k] += _dot_tn(pb, dg.astype(BF16))
            dq_ref[:, cols] = (dg * pe * (1.0 - g)).astype(BF16)
        dh1 = dout_s[...] + _dot_nt(dq_ref[...], wpg_ref[...])
        dh1_ref[...] = dh1
        gpost_ref[...] += _rowsum8(dh1 * on)
        don = dh1 * pg_ref[...]
        dob = (r3 * (don - on * _lanemean(don * on))).astype(BF16)
        do_ref[...] = dob
        dy_ref[...] = _dot_nt(dob, wo_ref[...])

        @pl.when(i == pl.num_programs(0) - 1)
        def _():
            gpost_ref[...] = jnp.broadcast_to(jnp.sum(gpost_ref[...], axis=0, keepdims=True), gpost_ref.shape)

    row = lambda n: pl.BlockSpec((tm, n), lambda i: (i, 0))
    const = lambda shp: pl.BlockSpec(shp, lambda i, n=len(shp): (0,) * n, pipeline_mode=pl.Buffered(1))
    acc = lambda shp: pl.BlockSpec(shp, lambda i, n=len(shp): (0,) * n)
    return pl.pallas_call(
        body, name="head_fwd_bwd", grid=(t // tm,),
        in_specs=[row(D_MODEL), row(D_MODEL), row(D_PLE), row(D_MODEL), const((1, D_MODEL)),
                  const((D_MODEL, D_MODEL)), const((D_MODEL, D_MODEL)), const((N_CHIPS, D_PLE, W_PE_COLS))],
        out_specs=[row(D_MODEL), row(D_MODEL), row(D_MODEL), row(D_MODEL), row(D_MODEL),
                   acc((N_CHIPS, D_PLE, W_PE_COLS)), acc((SUBLANES, D_MODEL)), acc((SUBLANES, LANES))],
        out_shape=[jax.ShapeDtypeStruct((t, D_MODEL), BF16), jax.ShapeDtypeStruct((t, D_MODEL), BF16),
                   jax.ShapeDtypeStruct((t, D_MODEL), F32), jax.ShapeDtypeStruct((t, D_MODEL), BF16),
                   jax.ShapeDtypeStruct((t, D_MODEL), F32),
                   jax.ShapeDtypeStruct((N_CHIPS, D_PLE, W_PE_COLS), F32),
                   jax.ShapeDtypeStruct((SUBLANES, D_MODEL), F32), jax.ShapeDtypeStruct((SUBLANES, LANES), F32)],
        scratch_shapes=[pltpu.VMEM((tm, D_MODEL), F32)],
        compiler_params=_params("arbitrary"),
    )(x, y, p, tgt, post_g, w_out, w_pg, wg_pe)


def _head_bwd(dout, gt, p, o, post_g, w_out, w_pg, wg_pe, tm):
    t = dout.shape[0]

    def body(dout_ref, gt_ref, p_ref, o_ref, pg_ref, wo_ref, wpg_ref, wpe_ref,
             gwpe_ref, dq_ref, dh1_ref, do_ref, dy_ref, gpost_ref):
        i = pl.program_id(0)

        @pl.when(i == 0)
        def _():
            gpost_ref[...] = jnp.zeros_like(gpost_ref)
            gwpe_ref[...] = jnp.zeros_like(gwpe_ref)

        dout = dout_ref[...]
        gt = gt_ref[...]
        pb = p_ref[...].astype(BF16)
        for k in range(N_CHIPS):
            cols = slice(k * W_PE_COLS, (k + 1) * W_PE_COLS)
            pe = _dot(pb, wpe_ref[k])
            g = gt[:, cols]
            dg = dout[:, cols] * g
            gwpe_ref[k] += _dot_tn(pb, dg.astype(BF16))
            dq_ref[:, cols] = (dg * pe * (1.0 - g)).astype(BF16)
        dh1 = dout + _dot_nt(dq_ref[...], wpg_ref[...])
        dh1_ref[...] = dh1
        o = o_ref[...]
        r3 = lax.rsqrt(_lanemean(o * o) + EPS)
        on = o * r3
        gpost_ref[...] += _rowsum8(dh1 * on)
        don = dh1 * pg_ref[...]
        do = r3 * (don - on * _lanemean(don * on))
        dob = do.astype(BF16)
        do_ref[...] = dob
        dy_ref[...] = _dot_nt(dob, wo_ref[...])

        @pl.when(i == pl.num_programs(0) - 1)
        def _():
            gpost_ref[...] = jnp.broadcast_to(jnp.sum(gpost_ref[...], axis=0, keepdims=True), gpost_ref.shape)

    row = lambda n: pl.BlockSpec((tm, n), lambda i: (i, 0))
    const = lambda shp: pl.BlockSpec(shp, lambda i, n=len(shp): (0,) * n, pipeline_mode=pl.Buffered(1))
    return pl.pallas_call(
        body, name="head_bwd", grid=(t // tm,),
        in_specs=[row(D_MODEL), row(D_MODEL), row(D_PLE), row(D_MODEL), const((1, D_MODEL)),
                  const((D_MODEL, D_MODEL)), const((D_MODEL, D_MODEL)), const((N_CHIPS, D_PLE, W_PE_COLS))],
        out_specs=[pl.BlockSpec((N_CHIPS, D_PLE, W_PE_COLS), lambda i: (0, 0, 0)),
                   row(D_MODEL), row(D_MODEL), row(D_MODEL), row(D_MODEL),
                   pl.BlockSpec((SUBLANES, D_MODEL), lambda i: (0, 0))],
        out_shape=[jax.ShapeDtypeStruct((N_CHIPS, D_PLE, W_PE_COLS), F32), jax.ShapeDtypeStruct((t, D_MODEL), BF16),
                   jax.ShapeDtypeStruct((t, D_MODEL), F32), jax.ShapeDtypeStruct((t, D_MODEL), BF16),
                   jax.ShapeDtypeStruct((t, D_MODEL), F32), jax.ShapeDtypeStruct((SUBLANES, D_MODEL), F32)],
        compiler_params=_params("arbitrary"),
    )(dout, gt, p, o, post_g, w_out, w_pg, wg_pe)


def _branches_bwd(z, h, dy, prm, tm, token):
    t = h.shape[0]
    nt = t // tm
    hb = tm // SUBLANES

    def body(z_ref, zh_ref, h_ref, hh_ref, dy_ref,
             lng_ref, lnb_ref, wt_ref, wtt_ref, bsx_ref, cw_ref, cb_ref, wa_ref, wx_ref, ba_ref, bx_ref, lam_ref,
             oga_ref, ogb_ref, token_ref,
             dz_ref, g_oga, g_ogb, g_lng, g_lnb, g_bsx, g_ws, g_cw, g_cb, g_wa, g_ba, g_wx, g_bx, g_lam,
             vn_s, mixed_s, dm_s, dvn_s, xcbf_s, r_s, i_s, a_s, b_s, dh_s, dpr_s, dpi_s, dxc_s,
             ca_s, cd_s, cx_s):
        step_i = pl.program_id(0)
        tile = nt - 1 - step_i
        accs = (g_oga, g_ogb, g_lng, g_lnb, g_bsx, g_ws, g_cw, g_cb, g_wa, g_ba, g_wx, g_bx, g_lam)

        @pl.when(step_i == 0)
        def _():
            for r in accs + (ca_s, cd_s, cx_s):
                r[...] = jnp.zeros_like(r)

        dy_a = dy_ref[:, 0:D_HALF]
        dy_b = dy_ref[:, D_HALF:]

        u = _z_group(z_ref, 0).astype(F32)
        ug, tu = _gelu(u)
        v = _z_group(z_ref, 1).astype(F32)
        vg, tv = _gelu(v)
        vhat, rstd = _layernorm_parts(vg)
        vn_s[...] = (vhat * lng_ref[...] + lnb_ref[...]).astype(BF16)
        _spatial_mix(wt_ref, vn_s, bsx_ref, mixed_s, tm)
        mixed = mixed_s[...]
        ga = _z_group(z_ref, 2).astype(F32)
        sga = jax.nn.sigmoid(ga)
        sa = ga * sga
        um = ug * mixed
        ya = um * sa
        ra = lax.rsqrt(_lanemean(ya * ya) + EPS)
        yahat = ya * ra
        g_oga[...] += _rowsum8(dy_a * yahat)
        dn = dy_a * oga_ref[...]
        dya = ra * (dn - yahat * _lanemean(dn * yahat))
        dz_ref[:, 2 * D_HALF:3 * D_HALF] = (dya * um * (sga * (1.0 + ga * (1.0 - sga)))).astype(BF16)
        dz_ref[:, 0:D_HALF] = (dya * mixed * sa * _gelu_grad(u, tu)).astype(BF16)
        dmixed = dya * ug * sa
        g_bsx[...] += jnp.sum(dmixed.reshape(tm // CHUNK, CHUNK, D_HALF), axis=0)
        dm_s[...] = dmixed.astype(BF16)
        for c in range(tm // CHUNK):
            rows = slice(c * CHUNK, (c + 1) * CHUNK)
            for hd in range(N_HEADS):
                cols = slice(hd * CHUNK, (hd + 1) * CHUNK)
                dmh = dm_s[rows, cols]
                dvn_s[rows, cols] = _dot(wtt_ref[hd], dmh)
                g_ws[hd] += _dot_nt(dmh, vn_s[rows, cols])
        dvn = dvn_s[...]
        g_lng[...] += _rowsum8(dvn * vhat)
        g_lnb[...] += _rowsum8(dvn)
        dvh = dvn * lng_ref[...]
        dvg = rstd * (dvh - _lanemean(dvh) - vhat * _lanemean(dvh * vhat))
        dz_ref[:, D_HALF:2 * D_HALF] = (dvg * _gelu_grad(v, tv)).astype(BF16)

        xb = _z_group(z_ref, 3).astype(F32)
        halo = jnp.where(tile == 0, 0.0, _z_group(zh_ref, 3).astype(F32)[SUBLANES:])
        taps = _conv_taps(xb, halo)
        xc = cb_ref[...] + taps[0] * cw_ref[0:1, :]
        for k in range(1, CONV_W):
            xc = xc + taps[k] * cw_ref[k:k + 1, :]
        xcbf_s[...] = xc.astype(BF16)
        _lru_gates(xcbf_s, wa_ref, wx_ref, ba_ref, bx_ref, r_s, i_s)
        rg = r_s[...]
        ig = i_s[...]
        lam = lam_ref[...]
        a, mult_true = _decay_parts(rg, lam)
        row = lax.broadcasted_iota(jnp.int32, a.shape, 0)
        first = jnp.logical_and(tile == 0, row == 0)
        mult = jnp.where(first, 1.0, mult_true)
        hcur = h_ref[...]
        hprev = _shift_down(hcur, jnp.where(tile == 0, 0.0, hh_ref[...]), 1)
        gb = _z_group(z_ref, 4).astype(F32)
        sgb = jax.nn.sigmoid(gb)
        sb = gb * sgb
        yb = hcur * sb
        rb = lax.rsqrt(_lanemean(yb * yb) + EPS)
        ybhat = yb * rb
        g_ogb[...] += _rowsum8(dy_b * ybhat)
        dn = dy_b * ogb_ref[...]
        dyb = rb * (dn - ybhat * _lanemean(dn * ybhat))
        dz_ref[:, 4 * D_HALF:5 * D_HALF] = (dyb * hcur * (sgb * (1.0 + gb * (1.0 - sgb)))).astype(BF16)

        an = _shift_up(a, ca_s[...], 1)
        bb = dyb * sb
        r8 = row & (SUBLANES - 1)
        for d in (1, 2, 4):
            a_sh = pltpu.roll(an, tm - d, 0)
            b_sh = pltpu.roll(bb, tm - d, 0)
            m = r8 + d < SUBLANES
            bb = jnp.where(m, an * b_sh + bb, bb)
            an = jnp.where(m, an * a_sh, an)
        a_s[...] = an
        b_s[...] = bb

        def step(g, carry):
            sl = pl.ds(pl.multiple_of((hb - 1 - g) * SUBLANES, SUBLANES), SUBLANES)
            dg = a_s[sl, :] * carry + b_s[sl, :]
            dh_s[sl, :] = dg
            return jnp.broadcast_to(dg[0:1, :], dg.shape)

        cd_s[...] = lax.fori_loop(0, hb, step, cd_s[...])
        ca_s[...] = jnp.broadcast_to(a[0:1, :], ca_s.shape)
        dh = dh_s[...]
        da = dh * hprev
        gx = ig * xc
        dla = da * a - jnp.where(first, 0.0, dh * gx * (a * a / mult_true))
        g_lam[...] += _rowsum8(dla * rg)
        dr = dla * (-LRU_C * _softplus_neg(lam))
        dpr = dr * rg * (1.0 - rg)
        dpi = (dh * mult * xc) * ig * (1.0 - ig)
        g_ba[...] += _rowsum8(dpr)
        g_bx[...] += _rowsum8(dpi)
        dpr_s[...] = dpr.astype(BF16)
        dpi_s[...] = dpi.astype(BF16)
        for hd in range(N_HEADS):
            cols = slice(hd * CHUNK, (hd + 1) * CHUNK)
            xh = xcbf_s[:, cols]
            dprh = dpr_s[:, cols]
            dpih = dpi_s[:, cols]
            g_wa[hd] += _dot_tn(xh, dprh)
            g_wx[hd] += _dot_tn(xh, dpih)
            dxc_s[:, cols] = _dot_nt(dprh, wa_ref[hd]) + _dot_nt(dpih, wx_ref[hd])
        dxc = dxc_s[...] + dh * mult * ig
        g_cb[...] += _rowsum8(dxc)
        for k in range(CONV_W):
            g_cw[k * SUBLANES:(k + 1) * SUBLANES, :] += _rowsum8(dxc * taps[k])
        nxt = cx_s[...]
        dxb = dxc * cw_ref[CONV_W - 1:CONV_W, :]
        for j in range(1, CONV_W):
            dxb = dxb + _shift_up(dxc, nxt, j) * cw_ref[CONV_W - 1 - j:CONV_W - j, :]
        dz_ref[:, 3 * D_HALF:4 * D_HALF] = dxb.astype(BF16)
        cx_s[...] = dxc[0:SUBLANES]

        @pl.when(step_i == nt - 1)
        def _():
            for r in (g_oga, g_ogb, g_lng, g_lnb, g_cb, g_ba, g_bx):
                r[...] = jnp.broadcast_to(jnp.sum(r[...], axis=0, keepdims=True), r.shape)
            lam_f = LRU_C * jax.nn.sigmoid(-lam_ref[...])
            g_lam[...] = jnp.broadcast_to(jnp.sum(g_lam[...], axis=0, keepdims=True) * lam_f, g_lam.shape)
            for k in range(CONV_W):
                blk = g_cw[k * SUBLANES:(k + 1) * SUBLANES, :]
                g_cw[k * SUBLANES:(k + 1) * SUBLANES, :] = jnp.broadcast_to(jnp.sum(blk, axis=0, keepdims=True), blk.shape)
            tri = (lax.broadcasted_iota(jnp.int32, (CHUNK, CHUNK), 0) >= lax.broadcasted_iota(jnp.int32, (CHUNK, CHUNK), 1))
            for hd in range(N_HEADS):
                cols = slice(hd * CHUNK, (hd + 1) * CHUNK)
                g_ws[hd] = jnp.where(tri, g_ws[hd], 0.0)
                blk = g_bsx[:, cols]
                g_bsx[:, cols] = jnp.broadcast_to(jnp.sum(blk, axis=1, keepdims=True), blk.shape)

    rev = lambda i: nt - 1 - i
    zspec = pl.BlockSpec((N_CHIPS, tm, W_IN_COLS), lambda i: (0, rev(i), 0))
    halo = lambda col: pl.BlockSpec((SUBLANES, D_HALF), lambda i: (jnp.maximum(rev(i) * hb - 1, 0), col))
    zhalo = pl.BlockSpec((N_CHIPS, 2 * SUBLANES, W_IN_COLS), lambda i: (0, jnp.maximum(rev(i) * (hb // 2) - 1, 0), 0))
    full = lambda a: pl.BlockSpec(a.shape, lambda i, n=a.ndim: (0,) * n)
    acc = lambda shp: pl.BlockSpec(shp, lambda i, n=len(shp): (0,) * n)
    names = ("ln_g", "ln_b", "wt", "wtt", "bsx", "conv_w", "conv_b", "w_a", "w_x", "b_a", "b_x", "lam", "oga", "ogb")
    pr = [prm[n] for n in names] + [token]
    vec = (SUBLANES, D_HALF)
    mat = (N_HEADS, CHUNK, CHUNK)
    acc_shapes = [vec, vec, vec, vec, (CHUNK, D_HALF), mat, (CONV_W * SUBLANES, D_HALF), vec, mat, vec, mat, vec, vec]
    big = lambda dt: pltpu.VMEM((tm, D_HALF), dt)
    return pl.pallas_call(
        body, name="branches_bwd", grid=(nt,),
        in_specs=[zspec, zhalo,
                  pl.BlockSpec((tm, D_HALF), lambda i: (rev(i), 0)), halo(0),
                  pl.BlockSpec((tm, D_MODEL), lambda i: (rev(i), 0))] + [full(a) for a in pr],
        out_specs=[pl.BlockSpec((tm, D_Z), lambda i: (rev(i), 0))] + [acc(s) for s in acc_shapes],
        out_shape=[jax.ShapeDtypeStruct((t, D_Z), BF16)] + [jax.ShapeDtypeStruct(s, F32) for s in acc_shapes],
        scratch_shapes=[big(BF16), big(F32), big(BF16), big(F32), big(BF16), big(F32), big(F32), big(F32), big(F32),
                        big(F32), big(BF16), big(BF16), big(F32),
                        pltpu.VMEM(vec, F32), pltpu.VMEM(vec, F32), pltpu.VMEM(vec, F32)],
        compiler_params=_params("arbitrary"),
    )(z, z, h, h, dy, *pr)


def _inproj_bwd(dz, wg_in, x, dh1, pre_g, tm, tile0, nt, prev, last, token, name):
    t = x.shape[0]

    def body(*refs):
        dz_ref, w_ref, x_ref, dh1_ref, g_ref = refs[:5]
        gx_ref, gpre_ref, acc_s = refs[-3:]
        i = pl.program_id(0)

        @pl.when(i == 0)
        def _():
            gpre_ref[...] = jnp.zeros_like(gpre_ref) if prev is None else refs[7][...]

        acc = _dot_nt(dz_ref[:, 0:W_IN_COLS], w_ref[0])
        for k in range(1, N_CHIPS):
            acc = acc + _dot_nt(dz_ref[:, k * W_IN_COLS:(k + 1) * W_IN_COLS], w_ref[k])
        acc_s[...] = acc
        for s in range(tm // CHUNK):
            rows = slice(s * CHUNK, (s + 1) * CHUNK)
            xv = x_ref[rows, :]
            r = lax.rsqrt(_lanemean(xv * xv) + EPS)
            xhat = xv * r
            dhn = acc_s[rows, :]
            gpre_ref[...] += _rowsum8(dhn * xhat)
            dxh = dhn * g_ref[...]
            gx_ref[rows, :] = dh1_ref[rows, :] + r * (dxh - xhat * _lanemean(dxh * xhat))

        if last:
            @pl.when(i == nt - 1)
            def _():
                gpre_ref[...] = jnp.broadcast_to(jnp.sum(gpre_ref[...], axis=0, keepdims=True), gpre_ref.shape)

    row = lambda n: pl.BlockSpec((tm, n), lambda i: (tile0 + i, 0))
    small = lambda r: pl.BlockSpec((r, D_MODEL), lambda i: (0, 0))
    tok = pl.BlockSpec((SUBLANES, LANES), lambda i: (0, 0))
    in_specs = [row(D_Z), pl.BlockSpec(wg_in.shape, lambda i: (0, 0, 0), pipeline_mode=pl.Buffered(1)),
                row(D_MODEL), row(D_MODEL), small(1), tok]
    args = [dz, wg_in, x, dh1, pre_g, token]
    aliases = {}
    if prev is not None:
        in_specs += [ANY, small(SUBLANES)]
        args += list(prev)
        aliases = {6: 0}
    return pl.pallas_call(
        body, name=name, grid=(nt,), in_specs=in_specs, out_specs=[row(D_MODEL), small(SUBLANES)],
        out_shape=[jax.ShapeDtypeStruct((t, D_MODEL), F32), jax.ShapeDtypeStruct((SUBLANES, D_MODEL), F32)],
        input_output_aliases=aliases,
        scratch_shapes=[pltpu.VMEM((tm, D_MODEL), F32)],
        compiler_params=_params("arbitrary"),
    )(*args)


def _weight_grad(a, b, name, kb, nb, tk, tn, tt, token, a_transposed=False):
    t = b.shape[0]
    tt = min(tt, t)

    def body(a_ref, b_ref, token_ref, o_ref):
        @pl.when(pl.program_id(2) == 0)
        def _():
            o_ref[...] = jnp.zeros_like(o_ref)

        o_ref[...] += (_dot if a_transposed else _dot_tn)(a_ref[...], b_ref[...])

    a_spec = (pl.BlockSpec((tk, tt), lambda j, i, s: (i, s)) if a_transposed
              else pl.BlockSpec((tt, tk), lambda j, i, s: (s, i)))
    return pl.pallas_call(
        body, name=name, grid=(nb, kb, t // tt),
        in_specs=[a_spec, pl.BlockSpec((tt, tn), lambda j, i, s: (s, j)),
                  pl.BlockSpec((SUBLANES, LANES), lambda j, i, s: (0, 0))],
        out_specs=pl.BlockSpec((None, None, tk, tn), lambda j, i, s: (j, i, 0, 0)),
        out_shape=jax.ShapeDtypeStruct((nb, kb, tk, tn), F32),
        compiler_params=_params("parallel", "parallel", "arbitrary"),
    )(a, b, token)


def _place():
    x, y, c = lax.axis_index("x"), lax.axis_index("y"), lax.axis_index("c")
    return x, y, c


def _chip_of(x, y):
    return 2 * x + y


HBM = pl.BlockSpec(memory_space=pltpu.HBM)
SEM = pl.BlockSpec(memory_space=pltpu.SEMAPHORE)
EFFECT = pltpu.SideEffectType.DATAFLOW_SIDE_EFFECTING


def _hbm(a):
    return pltpu.with_memory_space_constraint(a, pltpu.HBM)


def _landing(shape, dtype):
    return _hbm(lax.empty(shape, dtype))


def _exchange_start(name, arrays, ncopies, build, after=None):
    n = len(arrays)
    extra = [] if after is None else [after]

    def body(*refs):
        ins, token = refs[:n], refs[-1]
        send_sems, recv_sems = refs[n + len(extra)], refs[n + len(extra) + 1]
        for cp in build(ins, send_sems, recv_sems):
            cp.start()
        token[...] = jnp.zeros_like(token)

    outs = pl.pallas_call(
        body, name=name,
        out_shape=(pltpu.SemaphoreType.DMA((ncopies,)), pltpu.SemaphoreType.DMA((ncopies,)),
                   *[pltpu.HBM(a.shape, a.dtype) for a in arrays], jax.ShapeDtypeStruct((SUBLANES, LANES), F32)),
        in_specs=[HBM] * n + [ANY] * len(extra),
        out_specs=(SEM, SEM, *[HBM] * n, pl.BlockSpec(memory_space=pltpu.VMEM)),
        input_output_aliases={q: q + 2 for q in range(n)},
        compiler_params=pltpu.CompilerParams(has_side_effects=EFFECT),
    )(*[_hbm(a) for a in arrays], *extra)
    return (outs[0], outs[1], list(outs[2:2 + n])), outs[-1]


def _exchange_wait(name, started, after, build):
    send, recv, arrays = started
    n = len(arrays)

    def body(*refs):
        ins, send_sems, recv_sems = refs[:n], refs[n], refs[n + 1]
        for cp in build(ins, send_sems, recv_sems):
            cp.wait_send()
            cp.wait_recv()

    return pl.pallas_call(
        body, name=name, out_shape=tuple(pltpu.HBM(a.shape, a.dtype) for a in arrays),
        in_specs=[HBM] * n + [SEM, SEM, ANY], out_specs=tuple([HBM] * n),
        input_output_aliases={q: q for q in range(n)},
        compiler_params=pltpu.CompilerParams(has_side_effects=EFFECT),
    )(*arrays, send, recv, after)


def _exchange_wait_start(name, started, after, build_wait, ncopies, build_start):
    send, recv, arrays = started
    n = len(arrays)

    def body(*refs):
        ins, send_sems, recv_sems = refs[:n], refs[n], refs[n + 1]
        send2, recv2, token = refs[n + 3], refs[n + 4], refs[-1]
        arrived = build_wait(ins, send_sems, recv_sems)
        for cp, onward in zip(arrived, build_start(ins, send2, recv2)):
            cp.wait_recv()
            onward.start()
        for cp in arrived:
            cp.wait_send()
        token[...] = jnp.zeros_like(token)

    outs = pl.pallas_call(
        body, name=name,
        out_shape=(pltpu.SemaphoreType.DMA((ncopies,)), pltpu.SemaphoreType.DMA((ncopies,)),
                   *[pltpu.HBM(a.shape, a.dtype) for a in arrays], jax.ShapeDtypeStruct((SUBLANES, LANES), F32)),
        in_specs=[HBM] * n + [SEM, SEM, ANY], out_specs=(SEM, SEM, *[HBM] * n, pl.BlockSpec(memory_space=pltpu.VMEM)),
        input_output_aliases={q: q + 2 for q in range(n)},
        compiler_params=pltpu.CompilerParams(has_side_effects=EFFECT),
    )(*arrays, send, recv, after)
    return (outs[0], outs[1], list(outs[2:2 + n])), outs[-1]


def _cast_into_slot(w, kc, name, dtype=BF16, token=None):
    rows, cols = w.shape
    tr = min(rows, 4 * SUM_TILE)
    extra = [] if token is None else [token]

    def body(kc_ref, w_ref, *rest):
        rest[-1][...] = w_ref[...].astype(dtype)

    grid_spec = pltpu.PrefetchScalarGridSpec(
        num_scalar_prefetch=1, grid=(rows // tr,),
        in_specs=[pl.BlockSpec((tr, cols), lambda r, kc: (r, 0))]
                 + [pl.BlockSpec((SUBLANES, LANES), lambda r, kc: (0, 0))] * len(extra),
        out_specs=pl.BlockSpec((None, tr, cols), lambda r, kc: (kc[0], r, 0)))
    return pl.pallas_call(
        body, name=name, grid_spec=grid_spec, out_shape=jax.ShapeDtypeStruct((N_CHIPS, rows, cols), dtype),
        compiler_params=_params("arbitrary"),
    )(kc, w, *extra)


def _gather_ici_copies(n):
    def build(refs, send_sems, recv_sems):
        x, y, c = _place()
        mine = lambda b: refs[b].at[_chip_of(x, y), c]
        chips = [(1 - x, y), (x, 1 - y), (1 - x, 1 - y)]
        return [pltpu.make_async_remote_copy(
            src_ref=mine(b), dst_ref=mine(b), send_sem=send_sems.at[3 * b + j], recv_sem=recv_sems.at[3 * b + j],
            device_id=(*chip, c), device_id_type=MESH) for b in range(n) for j, chip in enumerate(chips)]
    return build


def _gather_direct_copies(n):
    def build(refs, send_sems, recv_sems):
        x, y, c = _place()
        mine = lambda b: refs[b].at[_chip_of(x, y)]
        chips = [(1 - x, y), (x, 1 - y), (1 - x, 1 - y)]
        return [pltpu.make_async_remote_copy(
            src_ref=mine(b), dst_ref=mine(b), send_sem=send_sems.at[3 * b + j], recv_sem=recv_sems.at[3 * b + j],
            device_id=(*chip, c), device_id_type=MESH) for b in range(n) for j, chip in enumerate(chips)]
    return build


def _gather_relay_copies(n):
    def build(refs, send_sems, recv_sems):
        x, y, c = _place()
        chips = [(1 - x, y), (x, 1 - y), (1 - x, 1 - y)]
        cps = []
        for b in range(n):
            for j, chip in enumerate(chips):
                got = refs[b].at[_chip_of(*chip), c]
                cps.append(pltpu.make_async_remote_copy(
                    src_ref=got, dst_ref=got, send_sem=send_sems.at[3 * b + j], recv_sem=recv_sems.at[3 * b + j],
                    device_id=(x, y, 1 - c), device_id_type=MESH))
        return cps
    return build


def _sibling_copies(n):
    def build(refs, send_sems, recv_sems):
        x, y, c = _place()
        return [pltpu.make_async_remote_copy(
            src_ref=refs[b].at[:, 1 - c], dst_ref=refs[n + b], send_sem=send_sems.at[b], recv_sem=recv_sems.at[b],
            device_id=(x, y, 1 - c), device_id_type=MESH) for b in range(n)]
    return build


def _chip_copies(n):
    def build(refs, send_sems, recv_sems):
        x, y, c = _place()
        chips = [(1 - x, y), (x, 1 - y), (1 - x, 1 - y)]
        return [pltpu.make_async_remote_copy(
            src_ref=refs[b].at[_chip_of(*chip)], dst_ref=refs[n + b].at[j],
            send_sem=send_sems.at[3 * b + j], recv_sem=recv_sems.at[3 * b + j],
            device_id=(*chip, c), device_id_type=MESH) for b in range(n) for j, chip in enumerate(chips)]
    return build


def _finish_copies(n, n_all):
    def build(refs, send_sems, recv_sems):
        x, y, c = _place()
        cps = [pltpu.make_async_remote_copy(
            src_ref=refs[b].at[c], dst_ref=refs[b].at[c], send_sem=send_sems.at[b], recv_sem=recv_sems.at[b],
            device_id=(x, y, 1 - c), device_id_type=MESH) for b in range(n)]
        flips = [(fx, fy, fc) for fx in (0, 1) for fy in (0, 1) for fc in (0, 1)][1:]
        for b in range(n_all):
            mine = refs[n + b].at[_chip_of(x, y), c]
            cps += [pltpu.make_async_remote_copy(
                src_ref=mine, dst_ref=mine, send_sem=send_sems.at[n + 7 * b + q], recv_sem=recv_sems.at[n + 7 * b + q],
                device_id=(x ^ fx, y ^ fy, c ^ fc), device_id_type=MESH) for q, (fx, fy, fc) in enumerate(flips)]
        return cps
    return build


def _pair_sum(g, r1, kc, name, tr, send_dtype):
    nk, _, rows, cols = g.shape

    def body(kc_ref, g_ref, r_ref, p_ref, own_ref):
        s = g_ref[...] + r_ref[...]
        p_ref[...] = s.astype(send_dtype)

        @pl.when(pl.program_id(1) == kc_ref[0])
        def _():
            own_ref[...] = s

    grid_spec = pltpu.PrefetchScalarGridSpec(
        num_scalar_prefetch=1, grid=(rows // tr, nk),
        in_specs=[pl.BlockSpec((None, None, tr, cols), lambda r, k, kc: (k, kc[1], r, 0)),
                  pl.BlockSpec((None, tr, cols), lambda r, k, kc: (k, r, 0))],
        out_specs=[pl.BlockSpec((None, tr, cols), lambda r, k, kc: (k, r, 0)),
                   pl.BlockSpec((tr, cols), lambda r, k, kc: (r, 0))])
    return pl.pallas_call(
        body, name=name, grid_spec=grid_spec,
        out_shape=[jax.ShapeDtypeStruct((nk, rows, cols), send_dtype), jax.ShapeDtypeStruct((rows, cols), F32)],
        compiler_params=_params("arbitrary", "arbitrary"),
    )(kc, g, r1)


def _chip_sum(own, r2, slot, lead, name, tr):
    rows, cols = own.shape
    nl = len(lead)

    def body(slot_ref, o_ref, r_ref, s_ref):
        s = o_ref[...]
        for j in range(3):
            s = s + r_ref[j].astype(F32)
        s_ref[...] = s

    grid_spec = pltpu.PrefetchScalarGridSpec(
        num_scalar_prefetch=1, grid=(rows // tr,),
        in_specs=[pl.BlockSpec((tr, cols), lambda r, sl: (r, 0)), pl.BlockSpec((3, tr, cols), lambda r, sl: (0, r, 0))],
        out_specs=pl.BlockSpec((None,) * nl + (tr, cols), lambda r, sl: tuple(sl[q] for q in range(nl)) + (r, 0)))
    return pl.pallas_call(
        body, name=name, grid_spec=grid_spec, out_shape=jax.ShapeDtypeStruct(tuple(lead) + (rows, cols), F32),
        compiler_params=_params("arbitrary"),
    )(slot, own, r2)


def _adam_update(w, g, m, v):
    nm = ADAM_B1 * m + (1.0 - ADAM_B1) * g
    nv = ADAM_B2 * v + (1.0 - ADAM_B2) * (g * g)
    m_hat = nm / (1.0 - ADAM_B1 ** ADAM_STEP)
    v_hat = nv / (1.0 - ADAM_B2 ** ADAM_STEP)
    return -ADAM_LR * (m_hat / (jnp.sqrt(v_hat) + ADAM_EPS) + ADAM_WD * w), nm, nv


def _adamw(w, g, m, v, name, tr, token):
    rows, cols = w.shape

    def body(w_ref, g_ref, m_ref, v_ref, token_ref, go_ref, d_ref, nm_ref, nv_ref):
        gv = g_ref[...]
        go_ref[...] = gv
        d_ref[...], nm_ref[...], nv_ref[...] = _adam_update(w_ref[...], gv, m_ref[...], v_ref[...])

    spec = pl.BlockSpec((tr, cols), lambda r: (r, 0))
    return pl.pallas_call(
        body, name=name, grid=(rows // tr,),
        in_specs=[spec] * 4 + [pl.BlockSpec((SUBLANES, LANES), lambda r: (0, 0))], out_specs=[spec] * 4,
        out_shape=[jax.ShapeDtypeStruct((rows, cols), F32)] * 4,
        compiler_params=_params("parallel"),
    )(w, g, m, v, token)


def _adamw_small(packed_g, pre_g_parts, ws, ms, vs):
    names = ["pre_g"] + [n for n, _ in SMALL_ROWS if n != "conv_w"]
    rows = dict(SMALL_ROWS)
    offset, at = {}, 0
    for n, r in SMALL_ROWS:
        offset[n] = at
        at += r
    k = len(names)

    def body(*refs):
        g_ref, pg_ref = refs[0], refs[1]
        w_refs, m_refs, v_refs = refs[2:2 + k], refs[2 + k:2 + 2 * k], refs[2 + 2 * k:2 + 3 * k]
        outs = refs[2 + 3 * k:]
        go, do, mo, vo = outs[:k], outs[k:2 * k], outs[2 * k:3 * k], outs[3 * k:4 * k]
        pre = pg_ref[0]
        for dev in range(1, 8):
            pre = pre + pg_ref[dev]
        outs[4 * k][...] = pre[D_MODEL // LANES:, :]
        for i, n in enumerate(names):
            shp = w_refs[i].shape
            if len(shp) == 2 and shp[0] == 1:
                for r in range(shp[1] // LANES):
                    cols = slice(r * LANES, (r + 1) * LANES)
                    g = pre[r:r + 1, :] if n == "pre_g" else g_ref[offset[n] + r:offset[n] + r + 1, :]
                    go[i][:, cols] = g
                    do[i][:, cols], mo[i][:, cols], vo[i][:, cols] = _adam_update(
                        w_refs[i][:, cols], g, m_refs[i][:, cols], v_refs[i][:, cols])
            else:
                g = g_ref[offset[n]:offset[n] + rows[n], :].reshape(shp)
                go[i][...] = g
                do[i][...], mo[i][...], vo[i][...] = _adam_update(w_refs[i][...], g, m_refs[i][...], v_refs[i][...])

    vm = pl.BlockSpec(memory_space=pltpu.VMEM)
    args = [packed_g, pre_g_parts] + [src[n] for src in (ws, ms, vs) for n in names]
    out_shape = [jax.ShapeDtypeStruct(ws[n].shape, F32) for _ in range(4) for n in names]
    out_shape.append(jax.ShapeDtypeStruct((SUBLANES, LANES), F32))
    outs = pl.pallas_call(
        body, name="adamw_small", in_specs=[vm] * len(args), out_specs=[vm] * (4 * k + 1), out_shape=out_shape,
    )(*args)
    return [dict(zip(names, outs[q * k:(q + 1) * k])) for q in range(4)], outs[4 * k]


def _into_slot(v, tail, slot, lead, name):
    n = v.shape[1]
    nl = len(lead)
    rows = n // LANES + SUBLANES

    def body(slot_ref, v_ref, t_ref, o_ref):
        for r in range(n // LANES):
            o_ref[r:r + 1, :] = v_ref[0:1, r * LANES:(r + 1) * LANES]
        o_ref[n // LANES:, :] = t_ref[...]

    grid_spec = pltpu.PrefetchScalarGridSpec(
        num_scalar_prefetch=1, grid=(1,),
        in_specs=[pl.BlockSpec(v.shape, lambda i, sl: (0, 0)), pl.BlockSpec(tail.shape, lambda i, sl: (0, 0))],
        out_specs=pl.BlockSpec((None,) * nl + (rows, LANES), lambda i, sl: tuple(sl[q] for q in range(nl)) + (0, 0)))
    return pl.pallas_call(
        body, name=name, grid_spec=grid_spec, out_shape=jax.ShapeDtypeStruct(tuple(lead) + (rows, LANES), F32),
    )(slot, v, tail)


def _pack_small(parts):
    names = [n for n, _ in SMALL_ROWS]
    offset, at = {}, 0
    for n, r in SMALL_ROWS:
        offset[n] = at
        at += r

    def body(*refs):
        ins, o_ref = dict(zip(names, refs[:-1])), refs[-1]
        o_ref[SMALL_USED:, :] = jnp.zeros((SMALL_TOTAL - SMALL_USED, LANES), F32)
        for n, rows in SMALL_ROWS:
            ref, at = ins[n], offset[n]
            if n == "gmlp_bs":
                for h in range(N_HEADS):
                    o_ref[at + h:at + h + 1, :] = jnp.transpose(ref[:, h * CHUNK:(h + 1) * CHUNK])[0:1, :]
            elif n == "conv_w":
                for k in range(CONV_W):
                    for r in range(D_HALF // LANES):
                        row = at + k * (D_HALF // LANES) + r
                        o_ref[row:row + 1, :] = ref[k * SUBLANES:k * SUBLANES + 1, r * LANES:(r + 1) * LANES]
            elif ref.ndim == 3:
                o_ref[at:at + rows, :] = ref[...].reshape(rows, LANES)
            else:
                for r in range(rows):
                    o_ref[at + r:at + r + 1, :] = ref[0:1, r * LANES:(r + 1) * LANES]

    vm = pl.BlockSpec(memory_space=pltpu.VMEM)
    return pl.pallas_call(
        body, name="pack_small", in_specs=[vm] * len(names), out_specs=vm,
        out_shape=jax.ShapeDtypeStruct((SMALL_TOTAL, LANES), F32),
    )(*[parts[n] for n in names])


def kernel(x, p, pre_g, w_in, gmlp_ln_g, gmlp_ln_b, gmlp_ws, gmlp_bs, conv_w, conv_b, w_a, b_a, w_x, b_x, lam, gmlp_out_g, lru_out_g, w_out, post_g, w_pe, w_pg, loss_target, m_pre_g, m_w_in, m_gmlp_ln_g, m_gmlp_ln_b, m_gmlp_ws, m_gmlp_bs, m_conv_w, m_conv_b, m_w_a, m_b_a, m_w_x, m_b_x, m_lam, m_gmlp_out_g, m_lru_out_g, m_w_out, m_post_g, m_w_pe, m_w_pg, v_pre_g, v_w_in, v_gmlp_ln_g, v_gmlp_ln_b, v_gmlp_ws, v_gmlp_bs, v_conv_w, v_conv_b, v_w_a, v_b_a, v_w_x, v_b_x, v_lam, v_gmlp_out_g, v_lru_out_g, v_w_out, v_post_g, v_w_pe, v_w_pg):
    weights = dict(pre_g=pre_g, w_in=w_in, gmlp_ln_g=gmlp_ln_g, gmlp_ln_b=gmlp_ln_b, gmlp_ws=gmlp_ws, gmlp_bs=gmlp_bs,
                   conv_w=conv_w, conv_b=conv_b, w_a=w_a, b_a=b_a, w_x=w_x, b_x=b_x, lam=lam, gmlp_out_g=gmlp_out_g,
                   lru_out_g=lru_out_g, w_out=w_out, post_g=post_g, w_pe=w_pe, w_pg=w_pg)
    mom_m = dict(pre_g=m_pre_g, w_in=m_w_in, gmlp_ln_g=m_gmlp_ln_g, gmlp_ln_b=m_gmlp_ln_b, gmlp_ws=m_gmlp_ws,
                 gmlp_bs=m_gmlp_bs, conv_w=m_conv_w, conv_b=m_conv_b, w_a=m_w_a, b_a=m_b_a, w_x=m_w_x, b_x=m_b_x,
                 lam=m_lam, gmlp_out_g=m_gmlp_out_g, lru_out_g=m_lru_out_g, w_out=m_w_out, post_g=m_post_g,
                 w_pe=m_w_pe, w_pg=m_w_pg)
    mom_v = dict(pre_g=v_pre_g, w_in=v_w_in, gmlp_ln_g=v_gmlp_ln_g, gmlp_ln_b=v_gmlp_ln_b, gmlp_ws=v_gmlp_ws,
                 gmlp_bs=v_gmlp_bs, conv_w=v_conv_w, conv_b=v_conv_b, w_a=v_w_a, b_a=v_b_a, w_x=v_w_x, b_x=v_b_x,
                 lam=v_lam, gmlp_out_g=v_gmlp_out_g, lru_out_g=v_lru_out_g, w_out=v_w_out, post_g=v_post_g,
                 w_pe=v_w_pe, w_pg=v_w_pg)
    order = list(weights)
    xi, yi, ci = _place()
    me = _chip_of(xi, yi)
    kc = jnp.stack([me, ci]).astype(jnp.int32)

    x2 = x[0]
    p2 = p[0, 0]
    tgt = loss_target[0]

    first = [_cast_into_slot(w_in[0], kc, "cast_w_in").reshape(N_CHIPS, 2, D_MODEL // 2, W_IN_COLS),
             _cast_into_slot(conv_w[0, :, 0, :], kc, "conv_w_into_slot", F32).reshape(N_CHIPS, 2, CONV_W // 2, CONV_COLS)]
    in_st, in_tok = _exchange_start("gather_in_start", first, 6, _gather_ici_copies(2))
    later = [_cast_into_slot(w_out[0], kc, "cast_w_out", token=in_tok).reshape(N_CHIPS, 2, W_ROWS // 2, D_MODEL),
             _cast_into_slot(w_pg[0], kc, "cast_w_pg", token=in_tok).reshape(N_CHIPS, 2, W_ROWS // 2, D_MODEL),
             _cast_into_slot(w_pe[0], kc, "cast_w_pe", token=in_tok).reshape(N_CHIPS, 2, D_PLE // 2, W_PE_COLS)]
    gather_st, gather_tok = _exchange_start("gather_start", later, 9, _gather_direct_copies(3), after=in_tok)
    hn, z_own, hn_t = _inproj_local(x2, pre_g, w_in[0], ROW_TILE, gather_tok)
    in_st, in_tok = _exchange_wait_start("gather_in_relay", in_st, z_own, _gather_ici_copies(2), 6,
                                         _gather_relay_copies(2))
    g_in, g_cw = _exchange_wait("gather_in_wait", in_st, in_tok, _gather_relay_copies(2))
    wg_in = g_in.reshape(N_CHIPS, D_MODEL, W_IN_COLS)
    cw_full = jnp.transpose(g_cw.reshape(N_CHIPS, CONV_W, CONV_COLS), (1, 0, 2)).reshape(CONV_W, D_HALF)

    causal = jnp.tril(jnp.ones((CHUNK, CHUNK), dtype=bool))
    ws_m = jnp.where(causal[None], gmlp_ws[0], 0.0)
    prm = dict(
        ln_g=gmlp_ln_g, ln_b=gmlp_ln_b, wt=ws_m.astype(BF16), wtt=jnp.transpose(ws_m, (0, 2, 1)).astype(BF16),
        bsx=jnp.repeat(jnp.transpose(gmlp_bs[0]), CHUNK, axis=1),
        conv_w=cw_full, conv_b=conv_b, w_a=w_a[0].astype(BF16), w_x=w_x[0].astype(BF16),
        b_a=b_a[0].reshape(1, D_HALF), b_x=b_x[0].reshape(1, D_HALF), lam=lam, oga=gmlp_out_g, ogb=lru_out_g)

    z, y, h = _inproj_branches_fwd(hn, z_own, wg_in, kc, prm, ROW_TILE, gather_tok)
    g_out, g_pg, g_pe = _exchange_wait("gather_wait", gather_st, y, _gather_direct_copies(3))
    wg_out = g_out.reshape(D_MODEL, D_MODEL)
    wg_pg = g_pg.reshape(D_MODEL, D_MODEL)
    wg_pe = g_pe.reshape(N_CHIPS, D_PLE, W_PE_COLS)
    h1, dq, dh1, do, dy, gw_pe, g_post, loss_acc = _head_fwd_bwd(x2, y, p2, tgt, post_g, wg_out, wg_pg, wg_pe,
                                                                 ROW_TILE)

    def sibling_start(tag, bufs):
        lands = [_landing((b.shape[0],) + b.shape[2:], b.dtype) for b in bufs]
        return _exchange_start("sibling_start_" + tag, bufs + lands, len(bufs), _sibling_copies(len(bufs)))

    def pair_then_chip_start(tag, started, after, names, tiles, dtypes):
        n = len(names)
        got = _exchange_wait("sibling_wait_" + tag, started, after, _sibling_copies(n))
        pairs = [_pair_sum(got[b], got[n + b], kc, "pair_sum_" + names[b], tiles[b], dtypes[b]) for b in range(n)]
        lands = [_landing((3,) + pr[0].shape[1:], pr[0].dtype) for pr in pairs]
        return _exchange_start("chip_start_" + tag, [pr[0] for pr in pairs] + lands, 3 * n, _chip_copies(n)), pairs

    def sum_then_finish_start(tag, started, pairs, after, names, tiles, small, to_all=()):
        n = len(names)
        got = _exchange_wait("chip_wait_" + tag, started, after, _chip_copies(n))
        sums = [_chip_sum(pairs[b][1], got[n + b], kc if small and b == n - 1 else kc[1:],
                          (N_CHIPS, 2) if small and b == n - 1 else (2,), "chip_sum_" + names[b], tiles[b])
                for b in range(n)]
        nbig = n - 1 if small else n
        n_all = n - nbig + len(to_all)
        return _exchange_start("finish_start_" + tag, sums + list(to_all), nbig + 7 * n_all,
                               _finish_copies(nbig, n_all))

    gw_pe = gw_pe.reshape(N_CHIPS, 2, D_PLE // 2, W_PE_COLS)
    token0 = jnp.zeros((SUBLANES, LANES), F32)
    gw_out = _weight_grad(y, do, "grad_w_out", 2, 1, D_MODEL // 2, D_MODEL, CONTRACT_TILE, token0)
    gw_pg = _weight_grad(h1, dq, "grad_w_pg", 2, 1, D_MODEL // 2, D_MODEL, CONTRACT_TILE, token0)
    gw_out = gw_out.reshape(N_CHIPS, 2, W_ROWS // 2, D_MODEL)
    gw_pg = gw_pg.reshape(N_CHIPS, 2, W_ROWS // 2, D_MODEL)

    names_a, tiles_a = ["w_out", "w_pg", "w_pe"], [SUM_TILE] * 3
    st, tok = sibling_start("a", [gw_out, gw_pg, gw_pe])
    (dz, g_oga, g_ogb, g_lng, g_lnb, g_bsx, g_ws, g_cw, g_cb, g_wa, g_ba, g_wx, g_bx, g_lam) = _branches_bwd(
        z, h, dy, prm, ROW_TILE, tok)
    (st, tok), pairs_a = pair_then_chip_start("a", st, dz, names_a, tiles_a, [BF16] * 3)
    gw_in = _weight_grad(hn_t, dz, "grad_w_in", 2, N_CHIPS, D_MODEL // 2, W_IN_COLS, CONTRACT_TILE, tok,
                         a_transposed=True)
    fin_a, tok = sum_then_finish_start("a", st, pairs_a, gw_in, names_a, tiles_a, False)

    small_g = dict(
        gmlp_ln_g=g_lng, gmlp_ln_b=g_lnb, gmlp_ws=g_ws, gmlp_bs=g_bsx, conv_w=g_cw, conv_b=g_cb, w_a=g_wa, b_a=g_ba,
        w_x=g_wx, b_x=g_bx, lam=g_lam, gmlp_out_g=g_oga, lru_out_g=g_ogb, post_g=g_post)
    gsm = _pack_small(small_g).reshape(N_CHIPS, 2, SMALL_PIECE, LANES)

    names_b, tiles_b = ["w_in", "small"], [2 * SUM_TILE, SMALL_PIECE]
    n_tiles = x2.shape[0] // ROW_TILE
    n_lo = max(1, (5 * n_tiles) // 16)
    st, tok_b = _exchange_start(
        "sibling_start_b", [gw_in, gsm] + [_landing((N_CHIPS,) + b.shape[2:], F32) for b in (gw_in, gsm)], 2,
        _sibling_copies(2), after=tok)
    part = _inproj_bwd(dz, wg_in, x2, dh1, pre_g, ROW_TILE, 0, n_lo, None, False, tok_b, "inproj_bwd_lo")
    f_out, f_pg, f_pe = _exchange_wait("finish_wait_a", fin_a, part[1], _finish_copies(3, 0))
    (st, tok_b), pairs_b = pair_then_chip_start("b", st, part[1], names_b, tiles_b, [BF16, F32])
    grad_x, g_pre = _inproj_bwd(dz, wg_in, x2, dh1, pre_g, ROW_TILE, n_lo, n_tiles - n_lo, part, True, tok_b,
                                "inproj_bwd_hi")
    pre_parts = _into_slot(g_pre, loss_acc, kc, (N_CHIPS, 2), "pre_g_into_slot")
    fin_b, tok_b = sum_then_finish_start("b", st, pairs_b, g_pre, names_b, tiles_b, True, to_all=[pre_parts])

    grads, deltas, new_m, new_v = {}, {}, {}, {}

    def adam_big(n, g2d, tr, token):
        shp = weights[n].shape
        g, d, nm, nv = _adamw(weights[n][0], g2d, mom_m[n][0], mom_v[n][0], "adamw_" + n, tr, token)
        grads[n], deltas[n], new_m[n], new_v[n] = g.reshape(shp), d.reshape(shp), nm.reshape(shp), nv.reshape(shp)
        return d

    as_token = lambda d: d[:SUBLANES, :LANES]
    last = adam_big("w_out", f_out.reshape(W_ROWS, D_MODEL), SUM_TILE, tok_b)
    last = adam_big("w_pg", f_pg.reshape(W_ROWS, D_MODEL), SUM_TILE, as_token(last))
    last = adam_big("w_pe", f_pe.reshape(D_PLE, W_PE_COLS), SUM_TILE, as_token(last))
    f_in, f_sm, pre_parts = _exchange_wait("finish_wait_b", fin_b, last, _finish_copies(1, 2))
    adam_big("w_in", f_in.reshape(D_MODEL, W_IN_COLS), 2 * SUM_TILE, tok_b)

    packed_g = f_sm.reshape(SMALL_TOTAL, LANES)
    small_names = ["pre_g"] + [n for n, _ in SMALL_ROWS if n != "conv_w"]
    natural = lambda src: {n: (src[n] if src[n].ndim == 2 else src[n][0]) for n in small_names}
    outs, loss_block = _adamw_small(packed_g, pre_parts.reshape(8, D_MODEL // LANES + SUBLANES, LANES),
                                    natural(weights), natural(mom_m), natural(mom_v))
    loss = loss_block[0, 0]
    for dst, got in zip((grads, deltas, new_m, new_v), outs):
        for n in small_names:
            dst[n] = got[n].reshape(weights[n].shape)
    at = sum(r for n, r in SMALL_ROWS[:[n for n, _ in SMALL_ROWS].index("conv_w")])
    g_cw_all = packed_g[at:at + CONV_W * D_HALF // LANES].reshape(CONV_W, D_HALF)
    g_conv = lax.dynamic_slice_in_dim(g_cw_all, me * CONV_COLS, CONV_COLS, axis=1)
    g, d, nm, nv = _adamw(conv_w[0, :, 0, :], g_conv, m_conv_w[0, :, 0, :], v_conv_w[0, :, 0, :], "adamw_conv_w", CONV_W,
                          tok_b)
    cshape = conv_w.shape
    grads["conv_w"], deltas["conv_w"] = g.reshape(cshape), d.reshape(cshape)
    new_m["conv_w"], new_v["conv_w"] = nm.reshape(cshape), nv.reshape(cshape)

    return (loss, grad_x.reshape(x.shape), *[grads[n] for n in order], *[deltas[n] for n in order],
            *[new_m[n] for n in order], *[new_v[n] for n in order])
```

```python
import math

import jax
import jax.numpy as jnp
from jax import lax
from jax.experimental import pallas as pl
from jax.experimental.pallas import tpu as pltpu

F32 = jnp.float32
BF16 = jnp.bfloat16

D_MODEL = 2048
D_HALF = 1024
D_Z = 5120
D_PLE = 256
CHUNK = 128
N_HEADS = 8
N_CHIPS = 4
W_IN_COLS = D_Z // N_CHIPS
W_ROWS = D_MODEL // N_CHIPS
W_PE_COLS = D_MODEL // N_CHIPS
CONV_W = 4
CONV_COLS = D_HALF // N_CHIPS
EPS = 1e-6
LRU_C = 8.0
ADAM_LR, ADAM_B1, ADAM_B2, ADAM_EPS, ADAM_WD, ADAM_STEP = 0.001, 0.9, 0.999, 1e-08, 0.01, 10

SUBLANES = 8
LANES = 128
VMEM_LIMIT = 56 * 1024 * 1024
ROW_TILE = 256
CONTRACT_TILE = 2048
SUM_TILE = 128

SMALL_ROWS = (("gmlp_ln_g", 8), ("gmlp_ln_b", 8), ("gmlp_ws", 1024), ("gmlp_bs", 8),
              ("conv_w", 32), ("conv_b", 8), ("w_a", 1024), ("b_a", 8), ("w_x", 1024), ("b_x", 8),
              ("lam", 8), ("gmlp_out_g", 8), ("lru_out_g", 8), ("post_g", 16))
SMALL_USED = sum(r for _, r in SMALL_ROWS)
SMALL_PIECE = 400
SMALL_TOTAL = 8 * SMALL_PIECE

MESH = pl.DeviceIdType.MESH
ANY = pl.BlockSpec(memory_space=pl.ANY)

_GELU_C0 = math.sqrt(2.0 / math.pi)
_GELU_C1 = 0.044715


def _params(*sem):
    return pltpu.CompilerParams(dimension_semantics=sem, vmem_limit_bytes=VMEM_LIMIT)


def _dot(a, b):
    return jnp.dot(a, b, preferred_element_type=F32)


def _dot_nt(a, b):
    return lax.dot_general(a, b, (((1,), (1,)), ((), ())), preferred_element_type=F32)


def _dot_tn(a, b):
    return lax.dot_general(a, b, (((0,), (0,)), ((), ())), preferred_element_type=F32)


def _gelu(x):
    t = jnp.tanh(_GELU_C0 * (x + _GELU_C1 * (x * x * x)))
    return 0.5 * x * (1.0 + t), t


def _gelu_grad(x, t):
    return 0.5 * (1.0 + t) + 0.5 * x * (1.0 - t * t) * (_GELU_C0 * (1.0 + 3.0 * _GELU_C1 * x * x))


def _rowsum8(v):
    r, n = v.shape
    return jnp.sum(v.reshape(r // SUBLANES, SUBLANES, n), axis=0)


def _lanemean(v):
    return jnp.mean(v, axis=-1, keepdims=True)


def _shift_down(v, halo8, k):
    if k == 0:
        return v
    r = pltpu.roll(v, k, 0)
    hr = pltpu.roll(halo8, k, 0)
    row = lax.broadcasted_iota(jnp.int32, halo8.shape, 0)
    top = jnp.where(row < k, hr, r[0:SUBLANES])
    return jnp.concatenate([top, r[SUBLANES:]], axis=0)


def _shift_up(v, next8, k):
    if k == 0:
        return v
    n = v.shape[0]
    r = pltpu.roll(v, n - k, 0)
    nr = pltpu.roll(next8, SUBLANES - k, 0)
    row = lax.broadcasted_iota(jnp.int32, next8.shape, 0)
    bot = jnp.where(row >= SUBLANES - k, nr, r[n - SUBLANES:])
    return jnp.concatenate([r[:n - SUBLANES], bot], axis=0)


def _layernorm_parts(vg):
    mu = _lanemean(vg)
    xc = vg - mu
    rstd = lax.rsqrt(_lanemean(xc * xc) + EPS)
    return xc * rstd, rstd


def _spatial_mix(wt_ref, vn_ref, bsx_ref, mixed_ref, tm):
    for c in range(tm // CHUNK):
        rows = slice(c * CHUNK, (c + 1) * CHUNK)
        for h in range(N_HEADS):
            cols = slice(h * CHUNK, (h + 1) * CHUNK)
            mixed_ref[rows, cols] = _dot(wt_ref[h], vn_ref[rows, cols]) + bsx_ref[:, cols]


def _conv_taps(xb, halo8):
    return [_shift_down(xb, halo8, CONV_W - 1 - k) for k in range(CONV_W)]


def _lru_gates(xc_bf_ref, wa_ref, wx_ref, ba_ref, bx_ref, r_ref, i_ref):
    for h in range(N_HEADS):
        cols = slice(h * CHUNK, (h + 1) * CHUNK)
        xh = xc_bf_ref[:, cols]
        r_ref[:, cols] = jax.nn.sigmoid(_dot(xh, wa_ref[h]) + ba_ref[:, cols])
        i_ref[:, cols] = jax.nn.sigmoid(_dot(xh, wx_ref[h]) + bx_ref[:, cols])


def _softplus_neg(lam):
    return jnp.maximum(-lam, 0.0) + jnp.log(1.0 + jnp.exp(-jnp.abs(lam)))


def _decay_parts(r, lam):
    la = (-LRU_C * _softplus_neg(lam)) * r
    a = jnp.exp(la)
    th = -jnp.tanh(la)
    mult = jnp.sqrt(2.0 * th / (1.0 + th))
    return a, mult


def _z_group(zref, g, rows=slice(None)):
    lo = g * D_HALF
    blk, off = lo // W_IN_COLS, lo % W_IN_COLS
    if off + D_HALF <= W_IN_COLS:
        return zref[blk, rows, off:off + D_HALF]
    return jnp.concatenate([zref[blk, rows, off:W_IN_COLS], zref[blk + 1, rows, 0:off + D_HALF - W_IN_COLS]], axis=1)


def _inproj_local(x, pre_g, w_own, tm, token):
    t = x.shape[0]

    def body(x_ref, g_ref, w_ref, token_ref, hn_ref, zl_ref, hnt_ref, wbf_s):
        @pl.when(pl.program_id(0) == 0)
        def _():
            wbf_s[...] = w_ref[...].astype(BF16)

        xv = x_ref[...]
        hnf = xv * lax.rsqrt(_lanemean(xv * xv) + EPS) * g_ref[...]
        hn = hnf.astype(BF16)
        hn_ref[...] = hn
        hnt_ref[...] = hnf.T.astype(BF16)
        zl_ref[...] = _dot(hn, wbf_s[...]).astype(BF16)

    row = lambda n: pl.BlockSpec((tm, n), lambda i: (i, 0))
    const = lambda shp: pl.BlockSpec(shp, lambda i: (0, 0), pipeline_mode=pl.Buffered(1))
    return pl.pallas_call(
        body, name="inproj_local", grid=(t // tm,),
        in_specs=[row(D_MODEL), const((1, D_MODEL)), const((D_MODEL, W_IN_COLS)), const((SUBLANES, LANES))],
        out_specs=[row(D_MODEL), row(W_IN_COLS), pl.BlockSpec((D_MODEL, tm), lambda i: (0, i))],
        out_shape=[jax.ShapeDtypeStruct((t, D_MODEL), BF16), jax.ShapeDtypeStruct((t, W_IN_COLS), BF16),
                   jax.ShapeDtypeStruct((D_MODEL, t), BF16)],
        scratch_shapes=[pltpu.VMEM((D_MODEL, W_IN_COLS), BF16)],
        compiler_params=_params("arbitrary"),
    )(x, pre_g, w_own, token)


def _inproj_branches_fwd(hn, z_own, wg_in, kc, prm, tm, token):
    t = hn.shape[0]
    nt = t // tm
    hb = tm // SUBLANES

    def body(kc_ref, hn_ref, zo_ref, w1_ref, w2_ref, w3_ref,
             lng_ref, lnb_ref, wt_ref, bsx_ref, cw_ref, cb_ref, wa_ref, wx_ref, ba_ref, bx_ref, lam_ref,
             oga_ref, ogb_ref, token_ref,
             z_ref, y_ref, h_ref,
             zbuf0, zbuf1, vn_s, mixed_s, xcbf_s, r_s, i_s, ug_s, halo_s, carry_s):
        s = pl.program_id(0)
        me = kc_ref[0]
        w_refs = (None, w1_ref, w2_ref, w3_ref)

        @pl.when(s <= 1)
        def _():
            carry_s[...] = jnp.zeros_like(carry_s)
            halo_s[...] = jnp.zeros_like(halo_s)

        def step(zw, zr, projecting=True, heads=True):
            def project(r):
                if not projecting:
                    return
                blk = (me + r) % N_CHIPS
                zb = zo_ref[...] if r == 0 else _dot(hn_ref[...], w_refs[r][...]).astype(BF16)
                z_ref[blk] = zb
                zw[blk] = zb

            zin = lambda g: _z_group(zr, g).astype(F32)
            always = [s >= 0] * 4

            @pl.when(always[0])
            def _():
                project(0)
                if not heads:
                    return
                ug, _ = _gelu(zin(0))
                ug_s[...] = ug
                vg, _ = _gelu(zin(1))
                vhat, _ = _layernorm_parts(vg)
                vn_s[...] = (vhat * lng_ref[...] + lnb_ref[...]).astype(BF16)

            @pl.when(always[1])
            def _():
                project(1)
                if not heads:
                    return
                _spatial_mix(wt_ref, vn_s, bsx_ref, mixed_s, tm)
                ga = zin(2)
                ya = ug_s[...] * mixed_s[...] * (ga * jax.nn.sigmoid(ga))
                ra = lax.rsqrt(_lanemean(ya * ya) + EPS)
                y_ref[:, 0:D_HALF] = (ya * ra * oga_ref[...]).astype(BF16)

            @pl.when(always[2])
            def _():
                project(2)
                if not heads:
                    return
                xb = zin(3)
                taps = _conv_taps(xb, halo_s[...])
                halo_s[...] = xb[tm - SUBLANES:]
                xc = cb_ref[...] + taps[0] * cw_ref[0:1, :]
                for k in range(1, CONV_W):
                    xc = xc + taps[k] * cw_ref[k:k + 1, :]
                xcbf_s[...] = xc.astype(BF16)
                _lru_gates(xcbf_s, wa_ref, wx_ref, ba_ref, bx_ref, r_s, i_s)
                a, mult = _decay_parts(r_s[...], lam_ref[...])
                row = lax.broadcasted_iota(jnp.int32, a.shape, 0)
                mult = jnp.where(jnp.logical_and(s == 1, row == 0), 1.0, mult)
                r_s[...] = a
                i_s[...] = mult * (i_s[...] * xc)

            @pl.when(always[3])
            def _():
                project(3)
                if not heads:
                    return
                a = r_s[...]
                b = i_s[...]
                r8 = lax.broadcasted_iota(jnp.int32, a.shape, 0) & (SUBLANES - 1)
                for d in (1, 2, 4):
                    a_sh = pltpu.roll(a, d, 0)
                    b_sh = pltpu.roll(b, d, 0)
                    m = r8 >= d
                    b = jnp.where(m, a * b_sh + b, b)
                    a = jnp.where(m, a * a_sh, a)
                carry = carry_s[...]
                for g in range(hb):
                    rows = slice(g * SUBLANES, (g + 1) * SUBLANES)
                    hg = a[rows] * carry + b[rows]
                    h_ref[rows, :] = hg
                    carry = jnp.broadcast_to(hg[SUBLANES - 1:SUBLANES, :], hg.shape)
                carry_s[...] = carry
                gb = zin(4)
                yb = h_ref[...] * (gb * jax.nn.sigmoid(gb))
                rb = lax.rsqrt(_lanemean(yb * yb) + EPS)
                y_ref[:, D_HALF:] = (yb * rb * ogb_ref[...]).astype(BF16)

        bufs = lambda parity: (zbuf0, zbuf1) if parity == 0 else (zbuf1, zbuf0)
        middle = jnp.logical_and(s > 0, s < nt)

        @pl.when(s == 0)
        def _():
            step(*bufs(0), heads=False)

        @pl.when(jnp.logical_and(middle, s % 2 == 0))
        def _():
            step(*bufs(0))

        @pl.when(jnp.logical_and(middle, s % 2 == 1))
        def _():
            step(*bufs(1))

        @pl.when(s == nt)
        def _():
            step(*bufs(nt % 2), projecting=False)

    const = lambda a: pl.BlockSpec(a.shape, lambda s, kc, n=a.ndim: (0,) * n, pipeline_mode=pl.Buffered(1))
    proj = lambda n: pl.BlockSpec((tm, n), lambda s, kc: (jnp.minimum(s, nt - 1), 0))
    head = lambda n: pl.BlockSpec((tm, n), lambda s, kc: (jnp.maximum(s - 1, 0), 0))
    other = lambda r: pl.BlockSpec((None, D_MODEL, W_IN_COLS), lambda s, kc, r=r: ((kc[0] + r) % N_CHIPS, 0, 0),
                                   pipeline_mode=pl.Buffered(1))
    names = ("ln_g", "ln_b", "wt", "bsx", "conv_w", "conv_b", "w_a", "w_x", "b_a", "b_x", "lam", "oga", "ogb")
    pr = [prm[n] for n in names] + [token]
    big = lambda dt: pltpu.VMEM((tm, D_HALF), dt)
    zblocks = pltpu.VMEM((N_CHIPS, tm, W_IN_COLS), BF16)
    grid_spec = pltpu.PrefetchScalarGridSpec(
        num_scalar_prefetch=1, grid=(nt + 1,),
        in_specs=[proj(D_MODEL), proj(W_IN_COLS), other(1), other(2), other(3)] + [const(a) for a in pr],
        out_specs=[pl.BlockSpec((N_CHIPS, tm, W_IN_COLS), lambda s, kc: (0, jnp.minimum(s, nt - 1), 0)),
                   head(D_MODEL), head(D_HALF)],
        scratch_shapes=[zblocks, zblocks, big(BF16), big(F32), big(BF16), big(F32), big(F32), big(F32),
                        pltpu.VMEM((SUBLANES, D_HALF), F32), pltpu.VMEM((SUBLANES, D_HALF), F32)])
    return pl.pallas_call(
        body, name="inproj_branches_fwd", grid_spec=grid_spec,
        out_shape=[jax.ShapeDtypeStruct((N_CHIPS, t, W_IN_COLS), BF16), jax.ShapeDtypeStruct((t, D_MODEL), BF16),
                   jax.ShapeDtypeStruct((t, D_HALF), F32)],
        compiler_params=_params("arbitrary"),
    )(kc, hn, z_own, wg_in, wg_in, wg_in, *pr)


def _head_fwd_bwd(x, y, p, tgt, post_g, w_out, w_pg, wg_pe, tm):
    t = x.shape[0]

    def body(x_ref, y_ref, p_ref, tgt_ref, pg_ref, wo_ref, wpg_ref, wpe_ref,
             h1_ref, dq_ref, dh1_ref, do_ref, dy_ref, gwpe_ref, gpost_ref, loss_ref, dout_s):
        i = pl.program_id(0)

        @pl.when(i == 0)
        def _():
            gpost_ref[...] = jnp.zeros_like(gpost_ref)
            gwpe_ref[...] = jnp.zeros_like(gwpe_ref)
            loss_ref[...] = jnp.zeros_like(loss_ref)

        o = _dot(y_ref[...], wo_ref[...])
        r3 = lax.rsqrt(_lanemean(o * o) + EPS)
        on = o * r3
        h1 = x_ref[...] + on * pg_ref[...]
        h1b = h1.astype(BF16)
        h1_ref[...] = h1b
        gt = jax.nn.sigmoid(_dot(h1b, wpg_ref[...]))
        pb = p_ref[...].astype(BF16)
        for k in range(N_CHIPS):
            cols = slice(k * W_PE_COLS, (k + 1) * W_PE_COLS)
            pe = _dot(pb, wpe_ref[k])
            g = gt[:, cols]
            d = h1[:, cols] + pe * g - tgt_ref[:, cols]
            loss_ref[...] += jnp.sum(d * d) * (0.5 / D_MODEL)
            dout = d * (1.0 / D_MODEL)
            dout_s[:, cols] = dout
            dg = dout * g
            gwpe_ref[k] += _dot_tn(pb, dg.astype(BF16))
            dq_ref[:, cols] = (dg * pe * (1.0 - g)).astype(BF16)
        dh1 = dout_s[...] + _dot_nt(dq_ref[...], wpg_ref[...])
        dh1_ref[...] = dh1
        gpost_ref[...] += _rowsum8(dh1 * on)
        don = dh1 * pg_ref[...]
        dob = (r3 * (don - on * _lanemean(don * on))).astype(BF16)
        do_ref[...] = dob
        dy_ref[...] = _dot_nt(dob, wo_ref[...])

        @pl.when(i == pl.num_programs(0) - 1)
        def _():
            gpost_ref[...] = jnp.broadcast_to(jnp.sum(gpost_ref[...], axis=0, keepdims=True), gpost_ref.shape)

    row = lambda n: pl.BlockSpec((tm, n), lambda i: (i, 0))
    const = lambda shp: pl.BlockSpec(shp, lambda i, n=len(shp): (0,) * n, pipeline_mode=pl.Buffered(1))
    acc = lambda shp: pl.BlockSpec(shp, lambda i, n=len(shp): (0,) * n)
    return pl.pallas_call(
        body, name="head_fwd_bwd", grid=(t // tm,),
        in_specs=[row(D_MODEL), row(D_MODEL), row(D_PLE), row(D_MODEL), const((1, D_MODEL)),
                  const((D_MODEL, D_MODEL)), const((D_MODEL, D_MODEL)), const((N_CHIPS, D_PLE, W_PE_COLS))],
        out_specs=[row(D_MODEL), row(D_MODEL), row(D_MODEL), row(D_MODEL), row(D_MODEL),
                   acc((N_CHIPS, D_PLE, W_PE_COLS)), acc((SUBLANES, D_MODEL)), acc((SUBLANES, LANES))],
        out_shape=[jax.ShapeDtypeStruct((t, D_MODEL), BF16), jax.ShapeDtypeStruct((t, D_MODEL), BF16),
                   jax.ShapeDtypeStruct((t, D_MODEL), F32), jax.ShapeDtypeStruct((t, D_MODEL), BF16),
                   jax.ShapeDtypeStruct((t, D_MODEL), F32),
                   jax.ShapeDtypeStruct((N_CHIPS, D_PLE, W_PE_COLS), F32),
                   jax.ShapeDtypeStruct((SUBLANES, D_MODEL), F32), jax.ShapeDtypeStruct((SUBLANES, LANES), F32)],
        scratch_shapes=[pltpu.VMEM((tm, D_MODEL), F32)],
        compiler_params=_params("arbitrary"),
    )(x, y, p, tgt, post_g, w_out, w_pg, wg_pe)


def _branches_bwd(z, h, dy, prm, tm, token):
    t = h.shape[0]
    nt = t // tm
    hb = tm // SUBLANES

    def body(z_ref, zh_ref, h_ref, hh_ref, dy_ref,
             lng_ref, lnb_ref, wt_ref, wtt_ref, bsx_ref, cw_ref, cb_ref, wa_ref, wx_ref, ba_ref, bx_ref, lam_ref,
             oga_ref, ogb_ref, token_ref,
             dz_ref, g_oga, g_ogb, g_lng, g_lnb, g_bsx, g_ws, g_cw, g_cb, g_wa, g_ba, g_wx, g_bx, g_lam,
             vn_s, mixed_s, dm_s, dvn_s, xcbf_s, r_s, i_s, a_s, b_s, dh_s, dpr_s, dpi_s, dxc_s,
             ca_s, cd_s, cx_s):
        step_i = pl.program_id(0)
        tile = nt - 1 - step_i
        accs = (g_oga, g_ogb, g_lng, g_lnb, g_bsx, g_ws, g_cw, g_cb, g_wa, g_ba, g_wx, g_bx, g_lam)

        @pl.when(step_i == 0)
        def _():
            for r in accs + (ca_s, cd_s, cx_s):
                r[...] = jnp.zeros_like(r)

        dy_a = dy_ref[:, 0:D_HALF]
        dy_b = dy_ref[:, D_HALF:]

        u = _z_group(z_ref, 0).astype(F32)
        ug, tu = _gelu(u)
        v = _z_group(z_ref, 1).astype(F32)
        vg, tv = _gelu(v)
        vhat, rstd = _layernorm_parts(vg)
        vn_s[...] = (vhat * lng_ref[...] + lnb_ref[...]).astype(BF16)
        _spatial_mix(wt_ref, vn_s, bsx_ref, mixed_s, tm)
        mixed = mixed_s[...]
        ga = _z_group(z_ref, 2).astype(F32)
        sga = jax.nn.sigmoid(ga)
        sa = ga * sga
        um = ug * mixed
        ya = um * sa
        ra = lax.rsqrt(_lanemean(ya * ya) + EPS)
        yahat = ya * ra
        g_oga[...] += _rowsum8(dy_a * yahat)
        dn = dy_a * oga_ref[...]
        dya = ra * (dn - yahat * _lanemean(dn * yahat))
        dz_ref[:, 2 * D_HALF:3 * D_HALF] = (dya * um * (sga * (1.0 + ga * (1.0 - sga)))).astype(BF16)
        dz_ref[:, 0:D_HALF] = (dya * mixed * sa * _gelu_grad(u, tu)).astype(BF16)
        dmixed = dya * ug * sa
        g_bsx[...] += jnp.sum(dmixed.reshape(tm // CHUNK, CHUNK, D_HALF), axis=0)
        dm_s[...] = dmixed.astype(BF16)
        for c in range(tm // CHUNK):
            rows = slice(c * CHUNK, (c + 1) * CHUNK)
            for hd in range(N_HEADS):
                cols = slice(hd * CHUNK, (hd + 1) * CHUNK)
                dmh = dm_s[rows, cols]
                dvn_s[rows, cols] = _dot(wtt_ref[hd], dmh)
                g_ws[hd] += _dot_nt(dmh, vn_s[rows, cols])
        dvn = dvn_s[...]
        g_lng[...] += _rowsum8(dvn * vhat)
        g_lnb[...] += _rowsum8(dvn)
        dvh = dvn * lng_ref[...]
        dvg = rstd * (dvh - _lanemean(dvh) - vhat * _lanemean(dvh * vhat))
        dz_ref[:, D_HALF:2 * D_HALF] = (dvg * _gelu_grad(v, tv)).astype(BF16)

        xb = _z_group(z_ref, 3).astype(F32)
        halo = jnp.where(tile == 0, 0.0, _z_group(zh_ref, 3).astype(F32)[SUBLANES:])
        taps = _conv_taps(xb, halo)
        xc = cb_ref[...] + taps[0] * cw_ref[0:1, :]
        for k in range(1, CONV_W):
            xc = xc + taps[k] * cw_ref[k:k + 1, :]
        xcbf_s[...] = xc.astype(BF16)
        _lru_gates(xcbf_s, wa_ref, wx_ref, ba_ref, bx_ref, r_s, i_s)
        rg = r_s[...]
        ig = i_s[...]
        lam = lam_ref[...]
        a, mult_true = _decay_parts(rg, lam)
        row = lax.broadcasted_iota(jnp.int32, a.shape, 0)
        first = jnp.logical_and(tile == 0, row == 0)
        mult = jnp.where(first, 1.0, mult_true)
        hcur = h_ref[...]
        hprev = _shift_down(hcur, jnp.where(tile == 0, 0.0, hh_ref[...]), 1)
        gb = _z_group(z_ref, 4).astype(F32)
        sgb = jax.nn.sigmoid(gb)
        sb = gb * sgb
        yb = hcur * sb
        rb = lax.rsqrt(_lanemean(yb * yb) + EPS)
        ybhat = yb * rb
        g_ogb[...] += _rowsum8(dy_b * ybhat)
        dn = dy_b * ogb_ref[...]
        dyb = rb * (dn - ybhat * _lanemean(dn * ybhat))
        dz_ref[:, 4 * D_HALF:5 * D_HALF] = (dyb * hcur * (sgb * (1.0 + gb * (1.0 - sgb)))).astype(BF16)

        an = _shift_up(a, ca_s[...], 1)
        bb = dyb * sb
        r8 = row & (SUBLANES - 1)
        for d in (1, 2, 4):
            a_sh = pltpu.roll(an, tm - d, 0)
            b_sh = pltpu.roll(bb, tm - d, 0)
            m = r8 + d < SUBLANES
            bb = jnp.where(m, an * b_sh + bb, bb)
            an = jnp.where(m, an * a_sh, an)
        a_s[...] = an
        b_s[...] = bb

        def step(g, carry):
            sl = pl.ds(pl.multiple_of((hb - 1 - g) * SUBLANES, SUBLANES), SUBLANES)
            dg = a_s[sl, :] * carry + b_s[sl, :]
            dh_s[sl, :] = dg
            return jnp.broadcast_to(dg[0:1, :], dg.shape)

        cd_s[...] = lax.fori_loop(0, hb, step, cd_s[...])
        ca_s[...] = jnp.broadcast_to(a[0:1, :], ca_s.shape)
        dh = dh_s[...]
        da = dh * hprev
        gx = ig * xc
        dla = da * a - jnp.where(first, 0.0, dh * gx * (a * a / mult_true))
        g_lam[...] += _rowsum8(dla * rg)
        dr = dla * (-LRU_C * _softplus_neg(lam))
        dpr = dr * rg * (1.0 - rg)
        dpi = (dh * mult * xc) * ig * (1.0 - ig)
        g_ba[...] += _rowsum8(dpr)
        g_bx[...] += _rowsum8(dpi)
        dpr_s[...] = dpr.astype(BF16)
        dpi_s[...] = dpi.astype(BF16)
        for hd in range(N_HEADS):
            cols = slice(hd * CHUNK, (hd + 1) * CHUNK)
            xh = xcbf_s[:, cols]
            dprh = dpr_s[:, cols]
            dpih = dpi_s[:, cols]
            g_wa[hd] += _dot_tn(xh, dprh)
            g_wx[hd] += _dot_tn(xh, dpih)
            dxc_s[:, cols] = _dot_nt(dprh, wa_ref[hd]) + _dot_nt(dpih, wx_ref[hd])
        dxc = dxc_s[...] + dh * mult * ig
        g_cb[...] += _rowsum8(dxc)
        for k in range(CONV_W):
            g_cw[k * SUBLANES:(k + 1) * SUBLANES, :] += _rowsum8(dxc * taps[k])
        nxt = cx_s[...]
        dxb = dxc * cw_ref[CONV_W - 1:CONV_W, :]
        for j in range(1, CONV_W):
            dxb = dxb + _shift_up(dxc, nxt, j) * cw_ref[CONV_W - 1 - j:CONV_W - j, :]
        dz_ref[:, 3 * D_HALF:4 * D_HALF] = dxb.astype(BF16)
        cx_s[...] = dxc[0:SUBLANES]

        @pl.when(step_i == nt - 1)
        def _():
            for r in (g_oga, g_ogb, g_lng, g_lnb, g_cb, g_ba, g_bx):
                r[...] = jnp.broadcast_to(jnp.sum(r[...], axis=0, keepdims=True), r.shape)
            lam_f = LRU_C * jax.nn.sigmoid(-lam_ref[...])
            g_lam[...] = jnp.broadcast_to(jnp.sum(g_lam[...], axis=0, keepdims=True) * lam_f, g_lam.shape)
            for k in range(CONV_W):
                blk = g_cw[k * SUBLANES:(k + 1) * SUBLANES, :]
                g_cw[k * SUBLANES:(k + 1) * SUBLANES, :] = jnp.broadcast_to(jnp.sum(blk, axis=0, keepdims=True), blk.shape)
            tri = (lax.broadcasted_iota(jnp.int32, (CHUNK, CHUNK), 0) >= lax.broadcasted_iota(jnp.int32, (CHUNK, CHUNK), 1))
            for hd in range(N_HEADS):
                cols = slice(hd * CHUNK, (hd + 1) * CHUNK)
                g_ws[hd] = jnp.where(tri, g_ws[hd], 0.0)
                blk = g_bsx[:, cols]
                g_bsx[:, cols] = jnp.broadcast_to(jnp.sum(blk, axis=1, keepdims=True), blk.shape)

    rev = lambda i: nt - 1 - i
    zspec = pl.BlockSpec((N_CHIPS, tm, W_IN_COLS), lambda i: (0, rev(i), 0))
    halo = lambda col: pl.BlockSpec((SUBLANES, D_HALF), lambda i: (jnp.maximum(rev(i) * hb - 1, 0), col))
    zhalo = pl.BlockSpec((N_CHIPS, 2 * SUBLANES, W_IN_COLS), lambda i: (0, jnp.maximum(rev(i) * (hb // 2) - 1, 0), 0))
    full = lambda a: pl.BlockSpec(a.shape, lambda i, n=a.ndim: (0,) * n)
    acc = lambda shp: pl.BlockSpec(shp, lambda i, n=len(shp): (0,) * n)
    names = ("ln_g", "ln_b", "wt", "wtt", "bsx", "conv_w", "conv_b", "w_a", "w_x", "b_a", "b_x", "lam", "oga", "ogb")
    pr = [prm[n] for n in names] + [token]
    vec = (SUBLANES, D_HALF)
    mat = (N_HEADS, CHUNK, CHUNK)
    acc_shapes = [vec, vec, vec, vec, (CHUNK, D_HALF), mat, (CONV_W * SUBLANES, D_HALF), vec, mat, vec, mat, vec, vec]
    big = lambda dt: pltpu.VMEM((tm, D_HALF), dt)
    return pl.pallas_call(
        body, name="branches_bwd", grid=(nt,),
        in_specs=[zspec, zhalo,
                  pl.BlockSpec((tm, D_HALF), lambda i: (rev(i), 0)), halo(0),
                  pl.BlockSpec((tm, D_MODEL), lambda i: (rev(i), 0))] + [full(a) for a in pr],
        out_specs=[pl.BlockSpec((tm, D_Z), lambda i: (rev(i), 0))] + [acc(s) for s in acc_shapes],
        out_shape=[jax.ShapeDtypeStruct((t, D_Z), BF16)] + [jax.ShapeDtypeStruct(s, F32) for s in acc_shapes],
        scratch_shapes=[big(BF16), big(F32), big(BF16), big(F32), big(BF16), big(F32), big(F32), big(F32), big(F32),
                        big(F32), big(BF16), big(BF16), big(F32),
                        pltpu.VMEM(vec, F32), pltpu.VMEM(vec, F32), pltpu.VMEM(vec, F32)],
        compiler_params=_params("arbitrary"),
    )(z, z, h, h, dy, *pr)


def _inproj_bwd(dz, wg_in, x, dh1, pre_g, tm, tile0, nt, prev, last, token, name):
    t = x.shape[0]

    def body(*refs):
        dz_ref, w_ref, x_ref, dh1_ref, g_ref = refs[:5]
        gx_ref, gpre_ref, acc_s = refs[-3:]
        i = pl.program_id(0)

        @pl.when(i == 0)
        def _():
            gpre_ref[...] = jnp.zeros_like(gpre_ref) if prev is None else refs[7][...]

        acc = _dot_nt(dz_ref[:, 0:W_IN_COLS], w_ref[0])
        for k in range(1, N_CHIPS):
            acc = acc + _dot_nt(dz_ref[:, k * W_IN_COLS:(k + 1) * W_IN_COLS], w_ref[k])
        acc_s[...] = acc
        for s in range(tm // CHUNK):
            rows = slice(s * CHUNK, (s + 1) * CHUNK)
            xv = x_ref[rows, :]
            r = lax.rsqrt(_lanemean(xv * xv) + EPS)
            xhat = xv * r
            dhn = acc_s[rows, :]
            gpre_ref[...] += _rowsum8(dhn * xhat)
            dxh = dhn * g_ref[...]
            gx_ref[rows, :] = dh1_ref[rows, :] + r * (dxh - xhat * _lanemean(dxh * xhat))

        if last:
            @pl.when(i == nt - 1)
            def _():
                gpre_ref[...] = jnp.broadcast_to(jnp.sum(gpre_ref[...], axis=0, keepdims=True), gpre_ref.shape)

    row = lambda n: pl.BlockSpec((tm, n), lambda i: (tile0 + i, 0))
    small = lambda r: pl.BlockSpec((r, D_MODEL), lambda i: (0, 0))
    tok = pl.BlockSpec((SUBLANES, LANES), lambda i: (0, 0))
    in_specs = [row(D_Z), pl.BlockSpec(wg_in.shape, lambda i: (0, 0, 0), pipeline_mode=pl.Buffered(1)),
                row(D_MODEL), row(D_MODEL), small(1), tok]
    args = [dz, wg_in, x, dh1, pre_g, token]
    aliases = {}
    if prev is not None:
        in_specs += [ANY, small(SUBLANES)]
        args += list(prev)
        aliases = {6: 0}
    return pl.pallas_call(
        body, name=name, grid=(nt,), in_specs=in_specs, out_specs=[row(D_MODEL), small(SUBLANES)],
        out_shape=[jax.ShapeDtypeStruct((t, D_MODEL), F32), jax.ShapeDtypeStruct((SUBLANES, D_MODEL), F32)],
        input_output_aliases=aliases,
        scratch_shapes=[pltpu.VMEM((tm, D_MODEL), F32)],
        compiler_params=_params("arbitrary"),
    )(*args)


def _weight_grad(a, b, name, kb, nb, tk, tn, tt, token, a_transposed=False):
    t = b.shape[0]
    tt = min(tt, t)

    def body(a_ref, b_ref, token_ref, o_ref):
        @pl.when(pl.program_id(2) == 0)
        def _():
            o_ref[...] = jnp.zeros_like(o_ref)

        o_ref[...] += (_dot if a_transposed else _dot_tn)(a_ref[...], b_ref[...])

    a_spec = (pl.BlockSpec((tk, tt), lambda j, i, s: (i, s)) if a_transposed
              else pl.BlockSpec((tt, tk), lambda j, i, s: (s, i)))
    return pl.pallas_call(
        body, name=name, grid=(nb, kb, t // tt),
        in_specs=[a_spec, pl.BlockSpec((tt, tn), lambda j, i, s: (s, j)),
                  pl.BlockSpec((SUBLANES, LANES), lambda j, i, s: (0, 0))],
        out_specs=pl.BlockSpec((None, None, tk, tn), lambda j, i, s: (j, i, 0, 0)),
        out_shape=jax.ShapeDtypeStruct((nb, kb, tk, tn), F32),
        compiler_params=_params("parallel", "parallel", "arbitrary"),
    )(a, b, token)


def _place():
    x, y, c = lax.axis_index("x"), lax.axis_index("y"), lax.axis_index("c")
    return x, y, c


def _chip_of(x, y):
    return 2 * x + y


HBM = pl.BlockSpec(memory_space=pltpu.HBM)
SEM = pl.BlockSpec(memory_space=pltpu.SEMAPHORE)
EFFECT = pltpu.SideEffectType.DATAFLOW_SIDE_EFFECTING


def _hbm(a):
    return pltpu.with_memory_space_constraint(a, pltpu.HBM)


def _landing(shape, dtype):
    return _hbm(lax.empty(shape, dtype))


def _exchange_start(name, arrays, ncopies, build, after=None):
    n = len(arrays)
    extra = [] if after is None else [after]

    def body(*refs):
        ins, token = refs[:n], refs[-1]
        send_sems, recv_sems = refs[n + len(extra)], refs[n + len(extra) + 1]
        for cp in build(ins, send_sems, recv_sems):
            cp.start()
        token[...] = jnp.zeros_like(token)

    outs = pl.pallas_call(
        body, name=name,
        out_shape=(pltpu.SemaphoreType.DMA((ncopies,)), pltpu.SemaphoreType.DMA((ncopies,)),
                   *[pltpu.HBM(a.shape, a.dtype) for a in arrays], jax.ShapeDtypeStruct((SUBLANES, LANES), F32)),
        in_specs=[HBM] * n + [ANY] * len(extra),
        out_specs=(SEM, SEM, *[HBM] * n, pl.BlockSpec(memory_space=pltpu.VMEM)),
        input_output_aliases={q: q + 2 for q in range(n)},
        compiler_params=pltpu.CompilerParams(has_side_effects=EFFECT),
    )(*[_hbm(a) for a in arrays], *extra)
    return (outs[0], outs[1], list(outs[2:2 + n])), outs[-1]


def _exchange_wait(name, started, after, build):
    send, recv, arrays = started
    n = len(arrays)

    def body(*refs):
        ins, send_sems, recv_sems = refs[:n], refs[n], refs[n + 1]
        for cp in build(ins, send_sems, recv_sems):
            cp.wait_send()
            cp.wait_recv()

    return pl.pallas_call(
        body, name=name, out_shape=tuple(pltpu.HBM(a.shape, a.dtype) for a in arrays),
        in_specs=[HBM] * n + [SEM, SEM, ANY], out_specs=tuple([HBM] * n),
        input_output_aliases={q: q for q in range(n)},
        compiler_params=pltpu.CompilerParams(has_side_effects=EFFECT),
    )(*arrays, send, recv, after)


def _exchange_wait_start(name, started, after, build_wait, ncopies, build_start):
    send, recv, arrays = started
    n = len(arrays)

    def body(*refs):
        ins, send_sems, recv_sems = refs[:n], refs[n], refs[n + 1]
        send2, recv2, token = refs[n + 3], refs[n + 4], refs[-1]
        arrived = build_wait(ins, send_sems, recv_sems)
        for cp, onward in zip(arrived, build_start(ins, send2, recv2)):
            cp.wait_recv()
            onward.start()
        for cp in arrived:
            cp.wait_send()
        token[...] = jnp.zeros_like(token)

    outs = pl.pallas_call(
        body, name=name,
        out_shape=(pltpu.SemaphoreType.DMA((ncopies,)), pltpu.SemaphoreType.DMA((ncopies,)),
                   *[pltpu.HBM(a.shape, a.dtype) for a in arrays], jax.ShapeDtypeStruct((SUBLANES, LANES), F32)),
        in_specs=[HBM] * n + [SEM, SEM, ANY], out_specs=(SEM, SEM, *[HBM] * n, pl.BlockSpec(memory_space=pltpu.VMEM)),
        input_output_aliases={q: q + 2 for q in range(n)},
        compiler_params=pltpu.CompilerParams(has_side_effects=EFFECT),
    )(*arrays, send, recv, after)
    return (outs[0], outs[1], list(outs[2:2 + n])), outs[-1]


def _cast_into_slot(w, kc, name, dtype=BF16, token=None):
    rows, cols = w.shape
    tr = min(rows, 4 * SUM_TILE)
    extra = [] if token is None else [token]

    def body(kc_ref, w_ref, *rest):
        rest[-1][...] = w_ref[...].astype(dtype)

    grid_spec = pltpu.PrefetchScalarGridSpec(
        num_scalar_prefetch=1, grid=(rows // tr,),
        in_specs=[pl.BlockSpec((tr, cols), lambda r, kc: (r, 0))]
                 + [pl.BlockSpec((SUBLANES, LANES), lambda r, kc: (0, 0))] * len(extra),
        out_specs=pl.BlockSpec((None, tr, cols), lambda r, kc: (kc[0], r, 0)))
    return pl.pallas_call(
        body, name=name, grid_spec=grid_spec, out_shape=jax.ShapeDtypeStruct((N_CHIPS, rows, cols), dtype),
        compiler_params=_params("arbitrary"),
    )(kc, w, *extra)


def _gather_ici_copies(n):
    def build(refs, send_sems, recv_sems):
        x, y, c = _place()
        mine = lambda b: refs[b].at[_chip_of(x, y), c]
        chips = [(1 - x, y), (x, 1 - y), (1 - x, 1 - y)]
        return [pltpu.make_async_remote_copy(
            src_ref=mine(b), dst_ref=mine(b), send_sem=send_sems.at[3 * b + j], recv_sem=recv_sems.at[3 * b + j],
            device_id=(*chip, c), device_id_type=MESH) for b in range(n) for j, chip in enumerate(chips)]
    return build


def _gather_direct_copies(n):
    def build(refs, send_sems, recv_sems):
        x, y, c = _place()
        mine = lambda b: refs[b].at[_chip_of(x, y)]
        chips = [(1 - x, y), (x, 1 - y), (1 - x, 1 - y)]
        return [pltpu.make_async_remote_copy(
            src_ref=mine(b), dst_ref=mine(b), send_sem=send_sems.at[3 * b + j], recv_sem=recv_sems.at[3 * b + j],
            device_id=(*chip, c), device_id_type=MESH) for b in range(n) for j, chip in enumerate(chips)]
    return build


def _gather_relay_copies(n):
    def build(refs, send_sems, recv_sems):
        x, y, c = _place()
        chips = [(1 - x, y), (x, 1 - y), (1 - x, 1 - y)]
        cps = []
        for b in range(n):
            for j, chip in enumerate(chips):
                got = refs[b].at[_chip_of(*chip), c]
                cps.append(pltpu.make_async_remote_copy(
                    src_ref=got, dst_ref=got, send_sem=send_sems.at[3 * b + j], recv_sem=recv_sems.at[3 * b + j],
                    device_id=(x, y, 1 - c), device_id_type=MESH))
        return cps
    return build


def _sibling_copies(n):
    def build(refs, send_sems, recv_sems):
        x, y, c = _place()
        return [pltpu.make_async_remote_copy(
            src_ref=refs[b].at[:, 1 - c], dst_ref=refs[n + b], send_sem=send_sems.at[b], recv_sem=recv_sems.at[b],
            device_id=(x, y, 1 - c), device_id_type=MESH) for b in range(n)]
    return build


def _chip_copies(n):
    def build(refs, send_sems, recv_sems):
        x, y, c = _place()
        chips = [(1 - x, y), (x, 1 - y), (1 - x, 1 - y)]
        return [pltpu.make_async_remote_copy(
            src_ref=refs[b].at[_chip_of(*chip)], dst_ref=refs[n + b].at[j],
            send_sem=send_sems.at[3 * b + j], recv_sem=recv_sems.at[3 * b + j],
            device_id=(*chip, c), device_id_type=MESH) for b in range(n) for j, chip in enumerate(chips)]
    return build


def _finish_copies(n, n_all):
    def build(refs, send_sems, recv_sems):
        x, y, c = _place()
        cps = [pltpu.make_async_remote_copy(
            src_ref=refs[b].at[c], dst_ref=refs[b].at[c], send_sem=send_sems.at[b], recv_sem=recv_sems.at[b],
            device_id=(x, y, 1 - c), device_id_type=MESH) for b in range(n)]
        flips = [(fx, fy, fc) for fx in (0, 1) for fy in (0, 1) for fc in (0, 1)][1:]
        for b in range(n_all):
            mine = refs[n + b].at[_chip_of(x, y), c]
            cps += [pltpu.make_async_remote_copy(
                src_ref=mine, dst_ref=mine, send_sem=send_sems.at[n + 7 * b + q], recv_sem=recv_sems.at[n + 7 * b + q],
                device_id=(x ^ fx, y ^ fy, c ^ fc), device_id_type=MESH) for q, (fx, fy, fc) in enumerate(flips)]
        return cps
    return build


def _pair_sum(g, r1, kc, name, tr, send_dtype):
    nk, _, rows, cols = g.shape

    def body(kc_ref, g_ref, r_ref, p_ref, own_ref):
        s = g_ref[...] + r_ref[...]
        p_ref[...] = s.astype(send_dtype)

        @pl.when(pl.program_id(1) == kc_ref[0])
        def _():
            own_ref[...] = s

    grid_spec = pltpu.PrefetchScalarGridSpec(
        num_scalar_prefetch=1, grid=(rows // tr, nk),
        in_specs=[pl.BlockSpec((None, None, tr, cols), lambda r, k, kc: (k, kc[1], r, 0)),
                  pl.BlockSpec((None, tr, cols), lambda r, k, kc: (k, r, 0))],
        out_specs=[pl.BlockSpec((None, tr, cols), lambda r, k, kc: (k, r, 0)),
                   pl.BlockSpec((tr, cols), lambda r, k, kc: (r, 0))])
    return pl.pallas_call(
        body, name=name, grid_spec=grid_spec,
        out_shape=[jax.ShapeDtypeStruct((nk, rows, cols), send_dtype), jax.ShapeDtypeStruct((rows, cols), F32)],
        compiler_params=_params("arbitrary", "arbitrary"),
    )(kc, g, r1)


def _chip_sum(own, r2, slot, lead, name, tr):
    rows, cols = own.shape
    nl = len(lead)

    def body(slot_ref, o_ref, r_ref, s_ref):
        s = o_ref[...]
        for j in range(3):
            s = s + r_ref[j].astype(F32)
        s_ref[...] = s

    grid_spec = pltpu.PrefetchScalarGridSpec(
        num_scalar_prefetch=1, grid=(rows // tr,),
        in_specs=[pl.BlockSpec((tr, cols), lambda r, sl: (r, 0)), pl.BlockSpec((3, tr, cols), lambda r, sl: (0, r, 0))],
        out_specs=pl.BlockSpec((None,) * nl + (tr, cols), lambda r, sl: tuple(sl[q] for q in range(nl)) + (r, 0)))
    return pl.pallas_call(
        body, name=name, grid_spec=grid_spec, out_shape=jax.ShapeDtypeStruct(tuple(lead) + (rows, cols), F32),
        compiler_params=_params("arbitrary"),
    )(slot, own, r2)


def _adam_update(w, g, m, v):
    nm = ADAM_B1 * m + (1.0 - ADAM_B1) * g
    nv = ADAM_B2 * v + (1.0 - ADAM_B2) * (g * g)
    m_hat = nm / (1.0 - ADAM_B1 ** ADAM_STEP)
    v_hat = nv / (1.0 - ADAM_B2 ** ADAM_STEP)
    return -ADAM_LR * (m_hat / (jnp.sqrt(v_hat) + ADAM_EPS) + ADAM_WD * w), nm, nv


def _adamw(w, g, m, v, name, tr, token):
    rows, cols = w.shape

    def body(w_ref, g_ref, m_ref, v_ref, token_ref, go_ref, d_ref, nm_ref, nv_ref):
        gv = g_ref[...]
        go_ref[...] = gv
        d_ref[...], nm_ref[...], nv_ref[...] = _adam_update(w_ref[...], gv, m_ref[...], v_ref[...])

    spec = pl.BlockSpec((tr, cols), lambda r: (r, 0))
    return pl.pallas_call(
        body, name=name, grid=(rows // tr,),
        in_specs=[spec] * 4 + [pl.BlockSpec((SUBLANES, LANES), lambda r: (0, 0))], out_specs=[spec] * 4,
        out_shape=[jax.ShapeDtypeStruct((rows, cols), F32)] * 4,
        compiler_params=_params("parallel"),
    )(w, g, m, v, token)


def _adamw_small(packed_g, pre_g_parts, ws, ms, vs):
    names = ["pre_g"] + [n for n, _ in SMALL_ROWS if n != "conv_w"]
    rows = dict(SMALL_ROWS)
    offset, at = {}, 0
    for n, r in SMALL_ROWS:
        offset[n] = at
        at += r
    k = len(names)

    def body(*refs):
        g_ref, pg_ref = refs[0], refs[1]
        w_refs, m_refs, v_refs = refs[2:2 + k], refs[2 + k:2 + 2 * k], refs[2 + 2 * k:2 + 3 * k]
        outs = refs[2 + 3 * k:]
        go, do, mo, vo = outs[:k], outs[k:2 * k], outs[2 * k:3 * k], outs[3 * k:4 * k]
        pre = pg_ref[0]
        for dev in range(1, 8):
            pre = pre + pg_ref[dev]
        outs[4 * k][...] = pre[D_MODEL // LANES:, :]
        for i, n in enumerate(names):
            shp = w_refs[i].shape
            if len(shp) == 2 and shp[0] == 1:
                for r in range(shp[1] // LANES):
                    cols = slice(r * LANES, (r + 1) * LANES)
                    g = pre[r:r + 1, :] if n == "pre_g" else g_ref[offset[n] + r:offset[n] + r + 1, :]
                    go[i][:, cols] = g
                    do[i][:, cols], mo[i][:, cols], vo[i][:, cols] = _adam_update(
                        w_refs[i][:, cols], g, m_refs[i][:, cols], v_refs[i][:, cols])
            else:
                g = g_ref[offset[n]:offset[n] + rows[n], :].reshape(shp)
                go[i][...] = g
                do[i][...], mo[i][...], vo[i][...] = _adam_update(w_refs[i][...], g, m_refs[i][...], v_refs[i][...])

    vm = pl.BlockSpec(memory_space=pltpu.VMEM)
    args = [packed_g, pre_g_parts] + [src[n] for src in (ws, ms, vs) for n in names]
    out_shape = [jax.ShapeDtypeStruct(ws[n].shape, F32) for _ in range(4) for n in names]
    out_shape.append(jax.ShapeDtypeStruct((SUBLANES, LANES), F32))
    outs = pl.pallas_call(
        body, name="adamw_small", in_specs=[vm] * len(args), out_specs=[vm] * (4 * k + 1), out_shape=out_shape,
    )(*args)
    return [dict(zip(names, outs[q * k:(q + 1) * k])) for q in range(4)], outs[4 * k]


def _into_slot(v, tail, slot, lead, name):
    n = v.shape[1]
    nl = len(lead)
    rows = n // LANES + SUBLANES

    def body(slot_ref, v_ref, t_ref, o_ref):
        for r in range(n // LANES):
            o_ref[r:r + 1, :] = v_ref[0:1, r * LANES:(r + 1) * LANES]
        o_ref[n // LANES:, :] = t_ref[...]

    grid_spec = pltpu.PrefetchScalarGridSpec(
        num_scalar_prefetch=1, grid=(1,),
        in_specs=[pl.BlockSpec(v.shape, lambda i, sl: (0, 0)), pl.BlockSpec(tail.shape, lambda i, sl: (0, 0))],
        out_specs=pl.BlockSpec((None,) * nl + (rows, LANES), lambda i, sl: tuple(sl[q] for q in range(nl)) + (0, 0)))
    return pl.pallas_call(
        body, name=name, grid_spec=grid_spec, out_shape=jax.ShapeDtypeStruct(tuple(lead) + (rows, LANES), F32),
    )(slot, v, tail)


def _pack_small(parts):
    names = [n for n, _ in SMALL_ROWS]
    offset, at = {}, 0
    for n, r in SMALL_ROWS:
        offset[n] = at
        at += r

    def body(*refs):
        ins, o_ref = dict(zip(names, refs[:-1])), refs[-1]
        o_ref[SMALL_USED:, :] = jnp.zeros((SMALL_TOTAL - SMALL_USED, LANES), F32)
        for n, rows in SMALL_ROWS:
            ref, at = ins[n], offset[n]
            if n == "gmlp_bs":
                for h in range(N_HEADS):
                    o_ref[at + h:at + h + 1, :] = jnp.transpose(ref[:, h * CHUNK:(h + 1) * CHUNK])[0:1, :]
            elif n == "conv_w":
                for k in range(CONV_W):
                    for r in range(D_HALF // LANES):
                        row = at + k * (D_HALF // LANES) + r
                        o_ref[row:row + 1, :] = ref[k * SUBLANES:k * SUBLANES + 1, r * LANES:(r + 1) * LANES]
            elif ref.ndim == 3:
                o_ref[at:at + rows, :] = ref[...].reshape(rows, LANES)
            else:
                for r in range(rows):
                    o_ref[at + r:at + r + 1, :] = ref[0:1, r * LANES:(r + 1) * LANES]

    vm = pl.BlockSpec(memory_space=pltpu.VMEM)
    return pl.pallas_call(
        body, name="pack_small", in_specs=[vm] * len(names), out_specs=vm,
        out_shape=jax.ShapeDtypeStruct((SMALL_TOTAL, LANES), F32),
    )(*[parts[n] for n in names])


def kernel(x, p, pre_g, w_in, gmlp_ln_g, gmlp_ln_b, gmlp_ws, gmlp_bs, conv_w, conv_b, w_a, b_a, w_x, b_x, lam, gmlp_out_g, lru_out_g, w_out, post_g, w_pe, w_pg, loss_target, m_pre_g, m_w_in, m_gmlp_ln_g, m_gmlp_ln_b, m_gmlp_ws, m_gmlp_bs, m_conv_w, m_conv_b, m_w_a, m_b_a, m_w_x, m_b_x, m_lam, m_gmlp_out_g, m_lru_out_g, m_w_out, m_post_g, m_w_pe, m_w_pg, v_pre_g, v_w_in, v_gmlp_ln_g, v_gmlp_ln_b, v_gmlp_ws, v_gmlp_bs, v_conv_w, v_conv_b, v_w_a, v_b_a, v_w_x, v_b_x, v_lam, v_gmlp_out_g, v_lru_out_g, v_w_out, v_post_g, v_w_pe, v_w_pg):
    weights = dict(pre_g=pre_g, w_in=w_in, gmlp_ln_g=gmlp_ln_g, gmlp_ln_b=gmlp_ln_b, gmlp_ws=gmlp_ws, gmlp_bs=gmlp_bs,
                   conv_w=conv_w, conv_b=conv_b, w_a=w_a, b_a=b_a, w_x=w_x, b_x=b_x, lam=lam, gmlp_out_g=gmlp_out_g,
                   lru_out_g=lru_out_g, w_out=w_out, post_g=post_g, w_pe=w_pe, w_pg=w_pg)
    mom_m = dict(pre_g=m_pre_g, w_in=m_w_in, gmlp_ln_g=m_gmlp_ln_g, gmlp_ln_b=m_gmlp_ln_b, gmlp_ws=m_gmlp_ws,
                 gmlp_bs=m_gmlp_bs, conv_w=m_conv_w, conv_b=m_conv_b, w_a=m_w_a, b_a=m_b_a, w_x=m_w_x, b_x=m_b_x,
                 lam=m_lam, gmlp_out_g=m_gmlp_out_g, lru_out_g=m_lru_out_g, w_out=m_w_out, post_g=m_post_g,
                 w_pe=m_w_pe, w_pg=m_w_pg)
    mom_v = dict(pre_g=v_pre_g, w_in=v_w_in, gmlp_ln_g=v_gmlp_ln_g, gmlp_ln_b=v_gmlp_ln_b, gmlp_ws=v_gmlp_ws,
                 gmlp_bs=v_gmlp_bs, conv_w=v_conv_w, conv_b=v_conv_b, w_a=v_w_a, b_a=v_b_a, w_x=v_w_x, b_x=v_b_x,
                 lam=v_lam, gmlp_out_g=v_gmlp_out_g, lru_out_g=v_lru_out_g, w_out=v_w_out, post_g=v_post_g,
                 w_pe=v_w_pe, w_pg=v_w_pg)
    order = list(weights)
    xi, yi, ci = _place()
    me = _chip_of(xi, yi)
    kc = jnp.stack([me, ci]).astype(jnp.int32)

    x2 = x[0]
    p2 = p[0, 0]
    tgt = loss_target[0]

    first = [_cast_into_slot(w_in[0], kc, "cast_w_in").reshape(N_CHIPS, 2, D_MODEL // 2, W_IN_COLS),
             _cast_into_slot(conv_w[0, :, 0, :], kc, "conv_w_into_slot", F32).reshape(N_CHIPS, 2, CONV_W // 2, CONV_COLS)]
    in_st, in_tok = _exchange_start("gather_in_start", first, 6, _gather_ici_copies(2))
    later = [_cast_into_slot(w_out[0], kc, "cast_w_out", token=in_tok).reshape(N_CHIPS, 2, W_ROWS // 2, D_MODEL),
             _cast_into_slot(w_pg[0], kc, "cast_w_pg", token=in_tok).reshape(N_CHIPS, 2, W_ROWS // 2, D_MODEL),
             _cast_into_slot(w_pe[0], kc, "cast_w_pe", token=in_tok).reshape(N_CHIPS, 2, D_PLE // 2, W_PE_COLS)]
    gather_st, gather_tok = _exchange_start("gather_start", later, 9, _gather_direct_copies(3), after=in_tok)
    hn, z_own, hn_t = _inproj_local(x2, pre_g, w_in[0], ROW_TILE, gather_tok)
    in_st, in_tok = _exchange_wait_start("gather_in_relay", in_st, z_own, _gather_ici_copies(2), 6,
                                         _gather_relay_copies(2))
    g_in, g_cw = _exchange_wait("gather_in_wait", in_st, in_tok, _gather_relay_copies(2))
    wg_in = g_in.reshape(N_CHIPS, D_MODEL, W_IN_COLS)
    cw_full = jnp.transpose(g_cw.reshape(N_CHIPS, CONV_W, CONV_COLS), (1, 0, 2)).reshape(CONV_W, D_HALF)

    causal = jnp.tril(jnp.ones((CHUNK, CHUNK), dtype=bool))
    ws_m = jnp.where(causal[None], gmlp_ws[0], 0.0)
    prm = dict(
        ln_g=gmlp_ln_g, ln_b=gmlp_ln_b, wt=ws_m.astype(BF16), wtt=jnp.transpose(ws_m, (0, 2, 1)).astype(BF16),
        bsx=jnp.repeat(jnp.transpose(gmlp_bs[0]), CHUNK, axis=1),
        conv_w=cw_full, conv_b=conv_b, w_a=w_a[0].astype(BF16), w_x=w_x[0].astype(BF16),
        b_a=b_a[0].reshape(1, D_HALF), b_x=b_x[0].reshape(1, D_HALF), lam=lam, oga=gmlp_out_g, ogb=lru_out_g)

    z, y, h = _inproj_branches_fwd(hn, z_own, wg_in, kc, prm, ROW_TILE, gather_tok)
    g_out, g_pg, g_pe = _exchange_wait("gather_wait", gather_st, y, _gather_direct_copies(3))
    wg_out = g_out.reshape(D_MODEL, D_MODEL)
    wg_pg = g_pg.reshape(D_MODEL, D_MODEL)
    wg_pe = g_pe.reshape(N_CHIPS, D_PLE, W_PE_COLS)
    h1, dq, dh1, do, dy, gw_pe, g_post, loss_acc = _head_fwd_bwd(x2, y, p2, tgt, post_g, wg_out, wg_pg, wg_pe,
                                                                 ROW_TILE)

    def sibling_start(tag, bufs):
        lands = [_landing((b.shape[0],) + b.shape[2:], b.dtype) for b in bufs]
        return _exchange_start("sibling_start_" + tag, bufs + lands, len(bufs), _sibling_copies(len(bufs)))

    def pair_then_chip_start(tag, started, after, names, tiles, dtypes):
        n = len(names)
        got = _exchange_wait("sibling_wait_" + tag, started, after, _sibling_copies(n))
        pairs = [_pair_sum(got[b], got[n + b], kc, "pair_sum_" + names[b], tiles[b], dtypes[b]) for b in range(n)]
        lands = [_landing((3,) + pr[0].shape[1:], pr[0].dtype) for pr in pairs]
        return _exchange_start("chip_start_" + tag, [pr[0] for pr in pairs] + lands, 3 * n, _chip_copies(n)), pairs

    def sum_then_finish_start(tag, started, pairs, after, names, tiles, small, to_all=()):
        n = len(names)
        got = _exchange_wait("chip_wait_" + tag, started, after, _chip_copies(n))
        sums = [_chip_sum(pairs[b][1], got[n + b], kc if small and b == n - 1 else kc[1:],
                          (N_CHIPS, 2) if small and b == n - 1 else (2,), "chip_sum_" + names[b], tiles[b])
                for b in range(n)]
        nbig = n - 1 if small else n
        n_all = n - nbig + len(to_all)
        return _exchange_start("finish_start_" + tag, sums + list(to_all), nbig + 7 * n_all,
                               _finish_copies(nbig, n_all))

    gw_pe = gw_pe.reshape(N_CHIPS, 2, D_PLE // 2, W_PE_COLS)
    token0 = jnp.zeros((SUBLANES, LANES), F32)
    gw_out = _weight_grad(y, do, "grad_w_out", 2, 1, D_MODEL // 2, D_MODEL, CONTRACT_TILE, token0)
    gw_pg = _weight_grad(h1, dq, "grad_w_pg", 2, 1, D_MODEL // 2, D_MODEL, CONTRACT_TILE, token0)
    gw_out = gw_out.reshape(N_CHIPS, 2, W_ROWS // 2, D_MODEL)
    gw_pg = gw_pg.reshape(N_CHIPS, 2, W_ROWS // 2, D_MODEL)

    names_a, tiles_a = ["w_out", "w_pg", "w_pe"], [SUM_TILE] * 3
    st, tok = sibling_start("a", [gw_out, gw_pg, gw_pe])
    (dz, g_oga, g_ogb, g_lng, g_lnb, g_bsx, g_ws, g_cw, g_cb, g_wa, g_ba, g_wx, g_bx, g_lam) = _branches_bwd(
        z, h, dy, prm, ROW_TILE, tok)
    (st, tok), pairs_a = pair_then_chip_start("a", st, dz, names_a, tiles_a, [BF16] * 3)
    gw_in = _weight_grad(hn_t, dz, "grad_w_in", 2, N_CHIPS, D_MODEL // 2, W_IN_COLS, CONTRACT_TILE, tok,
                         a_transposed=True)
    fin_a, tok = sum_then_finish_start("a", st, pairs_a, gw_in, names_a, tiles_a, False)

    small_g = dict(
        gmlp_ln_g=g_lng, gmlp_ln_b=g_lnb, gmlp_ws=g_ws, gmlp_bs=g_bsx, conv_w=g_cw, conv_b=g_cb, w_a=g_wa, b_a=g_ba,
        w_x=g_wx, b_x=g_bx, lam=g_lam, gmlp_out_g=g_oga, lru_out_g=g_ogb, post_g=g_post)
    gsm = _pack_small(small_g).reshape(N_CHIPS, 2, SMALL_PIECE, LANES)

    names_b, tiles_b = ["w_in", "small"], [2 * SUM_TILE, SMALL_PIECE]
    n_tiles = x2.shape[0] // ROW_TILE
    n_lo = max(1, (5 * n_tiles) // 16)
    st, tok_b = _exchange_start(
        "sibling_start_b", [gw_in, gsm] + [_landing((N_CHIPS,) + b.shape[2:], F32) for b in (gw_in, gsm)], 2,
        _sibling_copies(2), after=tok)
    part = _inproj_bwd(dz, wg_in, x2, dh1, pre_g, ROW_TILE, 0, n_lo, None, False, tok_b, "inproj_bwd_lo")
    f_out, f_pg, f_pe = _exchange_wait("finish_wait_a", fin_a, part[1], _finish_copies(3, 0))
    (st, tok_b), pairs_b = pair_then_chip_start("b", st, part[1], names_b, tiles_b, [BF16, F32])
    grad_x, g_pre = _inproj_bwd(dz, wg_in, x2, dh1, pre_g, ROW_TILE, n_lo, n_tiles - n_lo, part, True, tok_b,
                                "inproj_bwd_hi")
    pre_parts = _into_slot(g_pre, loss_acc, kc, (N_CHIPS, 2), "pre_g_into_slot")
    fin_b, tok_b = sum_then_finish_start("b", st, pairs_b, g_pre, names_b, tiles_b, True, to_all=[pre_parts])

    grads, deltas, new_m, new_v = {}, {}, {}, {}

    def adam_big(n, g2d, tr, token):
        shp = weights[n].shape
        g, d, nm, nv = _adamw(weights[n][0], g2d, mom_m[n][0], mom_v[n][0], "adamw_" + n, tr, token)
        grads[n], deltas[n], new_m[n], new_v[n] = g.reshape(shp), d.reshape(shp), nm.reshape(shp), nv.reshape(shp)
        return d

    as_token = lambda d: d[:SUBLANES, :LANES]
    last = adam_big("w_out", f_out.reshape(W_ROWS, D_MODEL), SUM_TILE, tok_b)
    last = adam_big("w_pg", f_pg.reshape(W_ROWS, D_MODEL), SUM_TILE, as_token(last))
    last = adam_big("w_pe", f_pe.reshape(D_PLE, W_PE_COLS), SUM_TILE, as_token(last))
    f_in, f_sm, pre_parts = _exchange_wait("finish_wait_b", fin_b, last, _finish_copies(1, 2))
    adam_big("w_in", f_in.reshape(D_MODEL, W_IN_COLS), 2 * SUM_TILE, tok_b)

    packed_g = f_sm.reshape(SMALL_TOTAL, LANES)
    small_names = ["pre_g"] + [n for n, _ in SMALL_ROWS if n != "conv_w"]
    natural = lambda src: {n: (src[n] if src[n].ndim == 2 else src[n][0]) for n in small_names}
    outs, loss_block = _adamw_small(packed_g, pre_parts.reshape(8, D_MODEL // LANES + SUBLANES, LANES),
                                    natural(weights), natural(mom_m), natural(mom_v))
    loss = loss_block[0, 0]
    for dst, got in zip((grads, deltas, new_m, new_v), outs):
        for n in small_names:
            dst[n] = got[n].reshape(weights[n].shape)
    at = sum(r for n, r in SMALL_ROWS[:[n for n, _ in SMALL_ROWS].index("conv_w")])
    g_cw_all = packed_g[at:at + CONV_W * D_HALF // LANES].reshape(CONV_W, D_HALF)
    g_conv = lax.dynamic_slice_in_dim(g_cw_all, me * CONV_COLS, CONV_COLS, axis=1)
    g, d, nm, nv = _adamw(conv_w[0, :, 0, :], g_conv, m_conv_w[0, :, 0, :], v_conv_w[0, :, 0, :], "adamw_conv_w", CONV_W,
                          tok_b)
    cshape = conv_w.shape
    grads["conv_w"], deltas["conv_w"] = g.reshape(cshape), d.reshape(cshape)
    new_m["conv_w"], new_v["conv_w"] = nm.reshape(cshape), nv.reshape(cshape)

    return (loss, grad_x.reshape(x.shape), *[grads[n] for n in order], *[deltas[n] for n in order],
            *[new_m[n] for n in order], *[new_v[n] for n in order])
```

```python
import math

import jax
import jax.numpy as jnp
from jax import lax
from jax.experimental import pallas as pl
from jax.experimental.pallas import tpu as pltpu

F32 = jnp.float32
BF16 = jnp.bfloat16

D_MODEL = 2048
D_HALF = 1024
D_Z = 5120
D_PLE = 256
CHUNK = 128
N_HEADS = 8
N_CHIPS = 4
W_IN_COLS = D_Z // N_CHIPS
W_ROWS = D_MODEL // N_CHIPS
W_PE_COLS = D_MODEL // N_CHIPS
CONV_W = 4
CONV_COLS = D_HALF // N_CHIPS
EPS = 1e-6
LRU_C = 8.0
ADAM_LR, ADAM_B1, ADAM_B2, ADAM_EPS, ADAM_WD, ADAM_STEP = 0.001, 0.9, 0.999, 1e-08, 0.01, 10

SUBLANES = 8
LANES = 128
VMEM_LIMIT = 56 * 1024 * 1024
ROW_TILE = 256
CONTRACT_TILE = 2048
SUM_TILE = 256

SMALL_ROWS = (("gmlp_ln_g", 8), ("gmlp_ln_b", 8), ("gmlp_ws", 1024), ("gmlp_bs", 8),
              ("conv_w", 32), ("conv_b", 8), ("w_a", 1024), ("b_a", 8), ("w_x", 1024), ("b_x", 8),
              ("lam", 8), ("gmlp_out_g", 8), ("lru_out_g", 8), ("post_g", 16))
SMALL_USED = sum(r for _, r in SMALL_ROWS)
SMALL_PIECE = 400
SMALL_TOTAL = 8 * SMALL_PIECE

MESH = pl.DeviceIdType.MESH
ANY = pl.BlockSpec(memory_space=pl.ANY)

_GELU_C0 = math.sqrt(2.0 / math.pi)
_GELU_C1 = 0.044715


def _params(*sem):
    return pltpu.CompilerParams(dimension_semantics=sem, vmem_limit_bytes=VMEM_LIMIT)


def _dot(a, b):
    return jnp.dot(a, b, preferred_element_type=F32)


def _dot_nt(a, b):
    return lax.dot_general(a, b, (((1,), (1,)), ((), ())), preferred_element_type=F32)


def _dot_tn(a, b):
    return lax.dot_general(a, b, (((0,), (0,)), ((), ())), preferred_element_type=F32)


def _gelu(x):
    t = jnp.tanh(_GELU_C0 * (x + _GELU_C1 * (x * x * x)))
    return 0.5 * x * (1.0 + t), t


def _gelu_grad(x, t):
    return 0.5 * (1.0 + t) + 0.5 * x * (1.0 - t * t) * (_GELU_C0 * (1.0 + 3.0 * _GELU_C1 * x * x))


def _rowsum8(v):
    r, n = v.shape
    return jnp.sum(v.reshape(r // SUBLANES, SUBLANES, n), axis=0)


def _lanemean(v):
    return jnp.mean(v, axis=-1, keepdims=True)


def _shift_down(v, halo8, k):
    if k == 0:
        return v
    r = pltpu.roll(v, k, 0)
    hr = pltpu.roll(halo8, k, 0)
    row = lax.broadcasted_iota(jnp.int32, halo8.shape, 0)
    top = jnp.where(row < k, hr, r[0:SUBLANES])
    return jnp.concatenate([top, r[SUBLANES:]], axis=0)


def _shift_up(v, next8, k):
    if k == 0:
        return v
    n = v.shape[0]
    r = pltpu.roll(v, n - k, 0)
    nr = pltpu.roll(next8, SUBLANES - k, 0)
    row = lax.broadcasted_iota(jnp.int32, next8.shape, 0)
    bot = jnp.where(row >= SUBLANES - k, nr, r[n - SUBLANES:])
    return jnp.concatenate([r[:n - SUBLANES], bot], axis=0)


def _layernorm_parts(vg):
    mu = _lanemean(vg)
    xc = vg - mu
    rstd = lax.rsqrt(_lanemean(xc * xc) + EPS)
    return xc * rstd, rstd


def _spatial_mix(wt_ref, vn_ref, bsx_ref, mixed_ref, tm):
    for c in range(tm // CHUNK):
        rows = slice(c * CHUNK, (c + 1) * CHUNK)
        for h in range(N_HEADS):
            cols = slice(h * CHUNK, (h + 1) * CHUNK)
            mixed_ref[rows, cols] = _dot(wt_ref[h], vn_ref[rows, cols]) + bsx_ref[:, cols]


def _conv_taps(xb, halo8):
    return [_shift_down(xb, halo8, CONV_W - 1 - k) for k in range(CONV_W)]


def _lru_gates(xc_bf_ref, wa_ref, wx_ref, ba_ref, bx_ref, r_ref, i_ref):
    for h in range(N_HEADS):
        cols = slice(h * CHUNK, (h + 1) * CHUNK)
        xh = xc_bf_ref[:, cols]
        r_ref[:, cols] = jax.nn.sigmoid(_dot(xh, wa_ref[h]) + ba_ref[:, cols])
        i_ref[:, cols] = jax.nn.sigmoid(_dot(xh, wx_ref[h]) + bx_ref[:, cols])


def _softplus_neg(lam):
    return jnp.maximum(-lam, 0.0) + jnp.log(1.0 + jnp.exp(-jnp.abs(lam)))


def _decay_parts(r, lam):
    la = (-LRU_C * _softplus_neg(lam)) * r
    a = jnp.exp(la)
    th = -jnp.tanh(la)
    mult = jnp.sqrt(2.0 * th / (1.0 + th))
    return a, mult


def _z_group(zref, g, rows=slice(None)):
    lo = g * D_HALF
    blk, off = lo // W_IN_COLS, lo % W_IN_COLS
    if off + D_HALF <= W_IN_COLS:
        return zref[blk, rows, off:off + D_HALF]
    return jnp.concatenate([zref[blk, rows, off:W_IN_COLS], zref[blk + 1, rows, 0:off + D_HALF - W_IN_COLS]], axis=1)


def _inproj_local(x, pre_g, w_own, tm, token):
    t = x.shape[0]

    def body(x_ref, g_ref, w_ref, token_ref, hn_ref, zl_ref, hnt_ref, wbf_s):
        @pl.when(pl.program_id(0) == 0)
        def _():
            wbf_s[...] = w_ref[...].astype(BF16)

        xv = x_ref[...]
        hnf = xv * lax.rsqrt(_lanemean(xv * xv) + EPS) * g_ref[...]
        hn = hnf.astype(BF16)
        hn_ref[...] = hn
        hnt_ref[...] = hnf.T.astype(BF16)
        zl_ref[...] = _dot(hn, wbf_s[...]).astype(BF16)

    row = lambda n: pl.BlockSpec((tm, n), lambda i: (i, 0))
    const = lambda shp: pl.BlockSpec(shp, lambda i: (0, 0), pipeline_mode=pl.Buffered(1))
    return pl.pallas_call(
        body, name="inproj_local", grid=(t // tm,),
        in_specs=[row(D_MODEL), const((1, D_MODEL)), const((D_MODEL, W_IN_COLS)), const((SUBLANES, LANES))],
        out_specs=[row(D_MODEL), row(W_IN_COLS), pl.BlockSpec((D_MODEL, tm), lambda i: (0, i))],
        out_shape=[jax.ShapeDtypeStruct((t, D_MODEL), BF16), jax.ShapeDtypeStruct((t, W_IN_COLS), BF16),
                   jax.ShapeDtypeStruct((D_MODEL, t), BF16)],
        scratch_shapes=[pltpu.VMEM((D_MODEL, W_IN_COLS), BF16)],
        compiler_params=_params("arbitrary"),
    )(x, pre_g, w_own, token)


def _inproj_branches_fwd(hn, z_own, wg_in, kc, prm, tm, token):
    t = hn.shape[0]
    nt = t // tm
    hb = tm // SUBLANES

    def body(kc_ref, hn_ref, zo_ref, w1_ref, w2_ref, w3_ref,
             lng_ref, lnb_ref, wt_ref, bsx_ref, cw_ref, cb_ref, wa_ref, wx_ref, ba_ref, bx_ref, lam_ref,
             oga_ref, ogb_ref, token_ref,
             z_ref, y_ref, h_ref,
             zbuf0, zbuf1, vn_s, mixed_s, xcbf_s, r_s, i_s, ug_s, halo_s, carry_s):
        s = pl.program_id(0)
        me = kc_ref[0]
        w_refs = (None, w1_ref, w2_ref, w3_ref)

        @pl.when(s == 0)
        def _():
            zbuf1[...] = jnp.zeros_like(zbuf1)

        @pl.when(s <= 1)
        def _():
            carry_s[...] = jnp.zeros_like(carry_s)
            halo_s[...] = jnp.zeros_like(halo_s)

        def step(zw, zr):
            def project(r):
                blk = (me + r) % N_CHIPS
                zb = zo_ref[...] if r == 0 else _dot(hn_ref[...], w_refs[r][...]).astype(BF16)
                z_ref[blk] = zb
                zw[blk] = zb

            zin = lambda g: _z_group(zr, g).astype(F32)
            always = [s >= 0] * 4

            @pl.when(always[0])
            def _():
                project(0)
                ug, _ = _gelu(zin(0))
                ug_s[...] = ug
                vg, _ = _gelu(zin(1))
                vhat, _ = _layernorm_parts(vg)
                vn_s[...] = (vhat * lng_ref[...] + lnb_ref[...]).astype(BF16)

            @pl.when(always[1])
            def _():
                project(1)
                _spatial_mix(wt_ref, vn_s, bsx_ref, mixed_s, tm)
                ga = zin(2)
                ya = ug_s[...] * mixed_s[...] * (ga * jax.nn.sigmoid(ga))
                ra = lax.rsqrt(_lanemean(ya * ya) + EPS)
                y_ref[:, 0:D_HALF] = (ya * ra * oga_ref[...]).astype(BF16)

            @pl.when(always[2])
            def _():
                project(2)
                xb = zin(3)
                taps = _conv_taps(xb, halo_s[...])
                halo_s[...] = xb[tm - SUBLANES:]
                xc = cb_ref[...] + taps[0] * cw_ref[0:1, :]
                for k in range(1, CONV_W):
                    xc = xc + taps[k] * cw_ref[k:k + 1, :]
                xcbf_s[...] = xc.astype(BF16)
                _lru_gates(xcbf_s, wa_ref, wx_ref, ba_ref, bx_ref, r_s, i_s)
                a, mult = _decay_parts(r_s[...], lam_ref[...])
                row = lax.broadcasted_iota(jnp.int32, a.shape, 0)
                mult = jnp.where(jnp.logical_and(s == 1, row == 0), 1.0, mult)
                r_s[...] = a
                i_s[...] = mult * (i_s[...] * xc)

            @pl.when(always[3])
            def _():
                project(3)
                a = r_s[...]
                b = i_s[...]
                r8 = lax.broadcasted_iota(jnp.int32, a.shape, 0) & (SUBLANES - 1)
                for d in (1, 2, 4):
                    a_sh = pltpu.roll(a, d, 0)
                    b_sh = pltpu.roll(b, d, 0)
                    m = r8 >= d
                    b = jnp.where(m, a * b_sh + b, b)
                    a = jnp.where(m, a * a_sh, a)
                carry = carry_s[...]
                for g in range(hb):
                    rows = slice(g * SUBLANES, (g + 1) * SUBLANES)
                    hg = a[rows] * carry + b[rows]
                    h_ref[rows, :] = hg
                    carry = jnp.broadcast_to(hg[SUBLANES - 1:SUBLANES, :], hg.shape)
                carry_s[...] = carry
                gb = zin(4)
                yb = h_ref[...] * (gb * jax.nn.sigmoid(gb))
                rb = lax.rsqrt(_lanemean(yb * yb) + EPS)
                y_ref[:, D_HALF:] = (yb * rb * ogb_ref[...]).astype(BF16)

        @pl.when(s % 2 == 0)
        def _():
            step(zbuf0, zbuf1)

        @pl.when(s % 2 == 1)
        def _():
            step(zbuf1, zbuf0)

    const = lambda a: pl.BlockSpec(a.shape, lambda s, kc, n=a.ndim: (0,) * n, pipeline_mode=pl.Buffered(1))
    proj = lambda n: pl.BlockSpec((tm, n), lambda s, kc: (jnp.minimum(s, nt - 1), 0))
    head = lambda n: pl.BlockSpec((tm, n), lambda s, kc: (jnp.maximum(s - 1, 0), 0))
    other = lambda r: pl.BlockSpec((None, D_MODEL, W_IN_COLS), lambda s, kc, r=r: ((kc[0] + r) % N_CHIPS, 0, 0),
                                   pipeline_mode=pl.Buffered(1))
    names = ("ln_g", "ln_b", "wt", "bsx", "conv_w", "conv_b", "w_a", "w_x", "b_a", "b_x", "lam", "oga", "ogb")
    pr = [prm[n] for n in names] + [token]
    big = lambda dt: pltpu.VMEM((tm, D_HALF), dt)
    zblocks = pltpu.VMEM((N_CHIPS, tm, W_IN_COLS), BF16)
    grid_spec = pltpu.PrefetchScalarGridSpec(
        num_scalar_prefetch=1, grid=(nt + 1,),
        in_specs=[proj(D_MODEL), proj(W_IN_COLS), other(1), other(2), other(3)] + [const(a) for a in pr],
        out_specs=[pl.BlockSpec((N_CHIPS, tm, W_IN_COLS), lambda s, kc: (0, jnp.minimum(s, nt - 1), 0)),
                   head(D_MODEL), head(D_HALF)],
        scratch_shapes=[zblocks, zblocks, big(BF16), big(F32), big(BF16), big(F32), big(F32), big(F32),
                        pltpu.VMEM((SUBLANES, D_HALF), F32), pltpu.VMEM((SUBLANES, D_HALF), F32)])
    return pl.pallas_call(
        body, name="inproj_branches_fwd", grid_spec=grid_spec,
        out_shape=[jax.ShapeDtypeStruct((N_CHIPS, t, W_IN_COLS), BF16), jax.ShapeDtypeStruct((t, D_MODEL), BF16),
                   jax.ShapeDtypeStruct((t, D_HALF), F32)],
        compiler_params=_params("arbitrary"),
    )(kc, hn, z_own, wg_in, wg_in, wg_in, *pr)


def _head_fwd_bwd(x, y, p, tgt, post_g, w_out, w_pg, wg_pe, tm):
    t = x.shape[0]

    def body(x_ref, y_ref, p_ref, tgt_ref, pg_ref, wo_ref, wpg_ref, wpe_ref,
             h1_ref, dq_ref, dh1_ref, do_ref, dy_ref, gwpe_ref, gpost_ref, loss_ref, dout_s):
        i = pl.program_id(0)

        @pl.when(i == 0)
        def _():
            gpost_ref[...] = jnp.zeros_like(gpost_ref)
            gwpe_ref[...] = jnp.zeros_like(gwpe_ref)
            loss_ref[...] = jnp.zeros_like(loss_ref)

        o = _dot(y_ref[...], wo_ref[...])
        r3 = lax.rsqrt(_lanemean(o * o) + EPS)
        on = o * r3
        h1 = x_ref[...] + on * pg_ref[...]
        h1b = h1.astype(BF16)
        h1_ref[...] = h1b
        gt = jax.nn.sigmoid(_dot(h1b, wpg_ref[...]))
        pb = p_ref[...].astype(BF16)
        for k in range(N_CHIPS):
            cols = slice(k * W_PE_COLS, (k + 1) * W_PE_COLS)
            pe = _dot(pb, wpe_ref[k])
            g = gt[:, cols]
            d = h1[:, cols] + pe * g - tgt_ref[:, cols]
            loss_ref[...] += jnp.sum(d * d) * (0.5 / D_MODEL)
            dout = d * (1.0 / D_MODEL)
            dout_s[:, cols] = dout
            dg = dout * g
            gwpe_ref[k] += _dot_tn(pb, dg.astype(BF16))
            dq_ref[:, cols] = (dg * pe * (1.0 - g)).astype(BF16)
        dh1 = dout_s[...] + _dot_nt(dq_ref[...], wpg_ref[...])
        dh1_ref[...] = dh1
        gpost_ref[...] += _rowsum8(dh1 * on)
        don = dh1 * pg_ref[...]
        dob = (r3 * (don - on * _lanemean(don * on))).astype(BF16)
        do_ref[...] = dob
        dy_ref[...] = _dot_nt(dob, wo_ref[...])

        @pl.when(i == pl.num_programs(0) - 1)
        def _():
            gpost_ref[...] = jnp.broadcast_to(jnp.sum(gpost_ref[...], axis=0, keepdims=True), gpost_ref.shape)

    row = lambda n: pl.BlockSpec((tm, n), lambda i: (i, 0))
    const = lambda shp: pl.BlockSpec(shp, lambda i, n=len(shp): (0,) * n, pipeline_mode=pl.Buffered(1))
    acc = lambda shp: pl.BlockSpec(shp, lambda i, n=len(shp): (0,) * n)
    return pl.pallas_call(
        body, name="head_fwd_bwd", grid=(t // tm,),
        in_specs=[row(D_MODEL), row(D_MODEL), row(D_PLE), row(D_MODEL), const((1, D_MODEL)),
                  const((D_MODEL, D_MODEL)), const((D_MODEL, D_MODEL)), const((N_CHIPS, D_PLE, W_PE_COLS))],
        out_specs=[row(D_MODEL), row(D_MODEL), row(D_MODEL), row(D_MODEL), row(D_MODEL),
                   acc((N_CHIPS, D_PLE, W_PE_COLS)), acc((SUBLANES, D_MODEL)), acc((SUBLANES, LANES))],
        out_shape=[jax.ShapeDtypeStruct((t, D_MODEL), BF16), jax.ShapeDtypeStruct((t, D_MODEL), BF16),
                   jax.ShapeDtypeStruct((t, D_MODEL), F32), jax.ShapeDtypeStruct((t, D_MODEL), BF16),
                   jax.ShapeDtypeStruct((t, D_MODEL), F32),
                   jax.ShapeDtypeStruct((N_CHIPS, D_PLE, W_PE_COLS), F32),
                   jax.ShapeDtypeStruct((SUBLANES, D_MODEL), F32), jax.ShapeDtypeStruct((SUBLANES, LANES), F32)],
        scratch_shapes=[pltpu.VMEM((tm, D_MODEL), F32)],
        compiler_params=_params("arbitrary"),
    )(x, y, p, tgt, post_g, w_out, w_pg, wg_pe)


def _branches_bwd(z, h, dy, prm, tm, token):
    t = h.shape[0]
    nt = t // tm
    hb = tm // SUBLANES

    def body(z_ref, zh_ref, h_ref, hh_ref, dy_ref,
             lng_ref, lnb_ref, wt_ref, wtt_ref, bsx_ref, cw_ref, cb_ref, wa_ref, wx_ref, ba_ref, bx_ref, lam_ref,
             oga_ref, ogb_ref, token_ref,
             dz_ref, g_oga, g_ogb, g_lng, g_lnb, g_bsx, g_ws, g_cw, g_cb, g_wa, g_ba, g_wx, g_bx, g_lam,
             vn_s, mixed_s, dm_s, dvn_s, xcbf_s, r_s, i_s, a_s, b_s, dh_s, dpr_s, dpi_s, dxc_s,
             ca_s, cd_s, cx_s):
        step_i = pl.program_id(0)
        tile = nt - 1 - step_i
        accs = (g_oga, g_ogb, g_lng, g_lnb, g_bsx, g_ws, g_cw, g_cb, g_wa, g_ba, g_wx, g_bx, g_lam)

        @pl.when(step_i == 0)
        def _():
            for r in accs + (ca_s, cd_s, cx_s):
                r[...] = jnp.zeros_like(r)

        dy_a = dy_ref[:, 0:D_HALF]
        dy_b = dy_ref[:, D_HALF:]

        u = _z_group(z_ref, 0).astype(F32)
        ug, tu = _gelu(u)
        v = _z_group(z_ref, 1).astype(F32)
        vg, tv = _gelu(v)
        vhat, rstd = _layernorm_parts(vg)
        vn_s[...] = (vhat * lng_ref[...] + lnb_ref[...]).astype(BF16)
        _spatial_mix(wt_ref, vn_s, bsx_ref, mixed_s, tm)
        mixed = mixed_s[...]
        ga = _z_group(z_ref, 2).astype(F32)
        sga = jax.nn.sigmoid(ga)
        sa = ga * sga
        um = ug * mixed
        ya = um * sa
        ra = lax.rsqrt(_lanemean(ya * ya) + EPS)
        yahat = ya * ra
        g_oga[...] += _rowsum8(dy_a * yahat)
        dn = dy_a * oga_ref[...]
        dya = ra * (dn - yahat * _lanemean(dn * yahat))
        dz_ref[:, 2 * D_HALF:3 * D_HALF] = (dya * um * (sga * (1.0 + ga * (1.0 - sga)))).astype(BF16)
        dz_ref[:, 0:D_HALF] = (dya * mixed * sa * _gelu_grad(u, tu)).astype(BF16)
        dmixed = dya * ug * sa
        g_bsx[...] += jnp.sum(dmixed.reshape(tm // CHUNK, CHUNK, D_HALF), axis=0)
        dm_s[...] = dmixed.astype(BF16)
        for c in range(tm // CHUNK):
            rows = slice(c * CHUNK, (c + 1) * CHUNK)
            for hd in range(N_HEADS):
                cols = slice(hd * CHUNK, (hd + 1) * CHUNK)
                dmh = dm_s[rows, cols]
                dvn_s[rows, cols] = _dot(wtt_ref[hd], dmh)
                g_ws[hd] += _dot_nt(dmh, vn_s[rows, cols])
        dvn = dvn_s[...]
        g_lng[...] += _rowsum8(dvn * vhat)
        g_lnb[...] += _rowsum8(dvn)
        dvh = dvn * lng_ref[...]
        dvg = rstd * (dvh - _lanemean(dvh) - vhat * _lanemean(dvh * vhat))
        dz_ref[:, D_HALF:2 * D_HALF] = (dvg * _gelu_grad(v, tv)).astype(BF16)

        xb = _z_group(z_ref, 3).astype(F32)
        halo = jnp.where(tile == 0, 0.0, _z_group(zh_ref, 3).astype(F32)[SUBLANES:])
        taps = _conv_taps(xb, halo)
        xc = cb_ref[...] + taps[0] * cw_ref[0:1, :]
        for k in range(1, CONV_W):
            xc = xc + taps[k] * cw_ref[k:k + 1, :]
        xcbf_s[...] = xc.astype(BF16)
        _lru_gates(xcbf_s, wa_ref, wx_ref, ba_ref, bx_ref, r_s, i_s)
        rg = r_s[...]
        ig = i_s[...]
        lam = lam_ref[...]
        a, mult_true = _decay_parts(rg, lam)
        row = lax.broadcasted_iota(jnp.int32, a.shape, 0)
        first = jnp.logical_and(tile == 0, row == 0)
        mult = jnp.where(first, 1.0, mult_true)
        hcur = h_ref[...]
        hprev = _shift_down(hcur, jnp.where(tile == 0, 0.0, hh_ref[...]), 1)
        gb = _z_group(z_ref, 4).astype(F32)
        sgb = jax.nn.sigmoid(gb)
        sb = gb * sgb
        yb = hcur * sb
        rb = lax.rsqrt(_lanemean(yb * yb) + EPS)
        ybhat = yb * rb
        g_ogb[...] += _rowsum8(dy_b * ybhat)
        dn = dy_b * ogb_ref[...]
        dyb = rb * (dn - ybhat * _lanemean(dn * ybhat))
        dz_ref[:, 4 * D_HALF:5 * D_HALF] = (dyb * hcur * (sgb * (1.0 + gb * (1.0 - sgb)))).astype(BF16)

        an = _shift_up(a, ca_s[...], 1)
        bb = dyb * sb
        r8 = row & (SUBLANES - 1)
        for d in (1, 2, 4):
            a_sh = pltpu.roll(an, tm - d, 0)
            b_sh = pltpu.roll(bb, tm - d, 0)
            m = r8 + d < SUBLANES
            bb = jnp.where(m, an * b_sh + bb, bb)
            an = jnp.where(m, an * a_sh, an)
        a_s[...] = an
        b_s[...] = bb

        def step(g, carry):
            sl = pl.ds(pl.multiple_of((hb - 1 - g) * SUBLANES, SUBLANES), SUBLANES)
            dg = a_s[sl, :] * carry + b_s[sl, :]
            dh_s[sl, :] = dg
            return jnp.broadcast_to(dg[0:1, :], dg.shape)

        cd_s[...] = lax.fori_loop(0, hb, step, cd_s[...])
        ca_s[...] = jnp.broadcast_to(a[0:1, :], ca_s.shape)
        dh = dh_s[...]
        da = dh * hprev
        gx = ig * xc
        dla = da * a - jnp.where(first, 0.0, dh * gx * (a * a / mult_true))
        g_lam[...] += _rowsum8(dla * rg)
        dr = dla * (-LRU_C * _softplus_neg(lam))
        dpr = dr * rg * (1.0 - rg)
        dpi = (dh * mult * xc) * ig * (1.0 - ig)
        g_ba[...] += _rowsum8(dpr)
        g_bx[...] += _rowsum8(dpi)
        dpr_s[...] = dpr.astype(BF16)
        dpi_s[...] = dpi.astype(BF16)
        for hd in range(N_HEADS):
            cols = slice(hd * CHUNK, (hd + 1) * CHUNK)
            xh = xcbf_s[:, cols]
            dprh = dpr_s[:, cols]
            dpih = dpi_s[:, cols]
            g_wa[hd] += _dot_tn(xh, dprh)
            g_wx[hd] += _dot_tn(xh, dpih)
            dxc_s[:, cols] = _dot_nt(dprh, wa_ref[hd]) + _dot_nt(dpih, wx_ref[hd])
        dxc = dxc_s[...] + dh * mult * ig
        g_cb[...] += _rowsum8(dxc)
        for k in range(CONV_W):
            g_cw[k * SUBLANES:(k + 1) * SUBLANES, :] += _rowsum8(dxc * taps[k])
        nxt = cx_s[...]
        dxb = dxc * cw_ref[CONV_W - 1:CONV_W, :]
        for j in range(1, CONV_W):
            dxb = dxb + _shift_up(dxc, nxt, j) * cw_ref[CONV_W - 1 - j:CONV_W - j, :]
        dz_ref[:, 3 * D_HALF:4 * D_HALF] = dxb.astype(BF16)
        cx_s[...] = dxc[0:SUBLANES]

        @pl.when(step_i == nt - 1)
        def _():
            for r in (g_oga, g_ogb, g_lng, g_lnb, g_cb, g_ba, g_bx):
                r[...] = jnp.broadcast_to(jnp.sum(r[...], axis=0, keepdims=True), r.shape)
            lam_f = LRU_C * jax.nn.sigmoid(-lam_ref[...])
            g_lam[...] = jnp.broadcast_to(jnp.sum(g_lam[...], axis=0, keepdims=True) * lam_f, g_lam.shape)
            for k in range(CONV_W):
                blk = g_cw[k * SUBLANES:(k + 1) * SUBLANES, :]
                g_cw[k * SUBLANES:(k + 1) * SUBLANES, :] = jnp.broadcast_to(jnp.sum(blk, axis=0, keepdims=True), blk.shape)
            tri = (lax.broadcasted_iota(jnp.int32, (CHUNK, CHUNK), 0) >= lax.broadcasted_iota(jnp.int32, (CHUNK, CHUNK), 1))
            for hd in range(N_HEADS):
                cols = slice(hd * CHUNK, (hd + 1) * CHUNK)
                g_ws[hd] = jnp.where(tri, g_ws[hd], 0.0)
                blk = g_bsx[:, cols]
                g_bsx[:, cols] = jnp.broadcast_to(jnp.sum(blk, axis=1, keepdims=True), blk.shape)

    rev = lambda i: nt - 1 - i
    zspec = pl.BlockSpec((N_CHIPS, tm, W_IN_COLS), lambda i: (0, rev(i), 0))
    halo = lambda col: pl.BlockSpec((SUBLANES, D_HALF), lambda i: (jnp.maximum(rev(i) * hb - 1, 0), col))
    zhalo = pl.BlockSpec((N_CHIPS, 2 * SUBLANES, W_IN_COLS), lambda i: (0, jnp.maximum(rev(i) * (hb // 2) - 1, 0), 0))
    full = lambda a: pl.BlockSpec(a.shape, lambda i, n=a.ndim: (0,) * n)
    acc = lambda shp: pl.BlockSpec(shp, lambda i, n=len(shp): (0,) * n)
    names = ("ln_g", "ln_b", "wt", "wtt", "bsx", "conv_w", "conv_b", "w_a", "w_x", "b_a", "b_x", "lam", "oga", "ogb")
    pr = [prm[n] for n in names] + [token]
    vec = (SUBLANES, D_HALF)
    mat = (N_HEADS, CHUNK, CHUNK)
    acc_shapes = [vec, vec, vec, vec, (CHUNK, D_HALF), mat, (CONV_W * SUBLANES, D_HALF), vec, mat, vec, mat, vec, vec]
    big = lambda dt: pltpu.VMEM((tm, D_HALF), dt)
    return pl.pallas_call(
        body, name="branches_bwd", grid=(nt,),
        in_specs=[zspec, zhalo,
                  pl.BlockSpec((tm, D_HALF), lambda i: (rev(i), 0)), halo(0),
                  pl.BlockSpec((tm, D_MODEL), lambda i: (rev(i), 0))] + [full(a) for a in pr],
        out_specs=[pl.BlockSpec((tm, D_Z), lambda i: (rev(i), 0))] + [acc(s) for s in acc_shapes],
        out_shape=[jax.ShapeDtypeStruct((t, D_Z), BF16)] + [jax.ShapeDtypeStruct(s, F32) for s in acc_shapes],
        scratch_shapes=[big(BF16), big(F32), big(BF16), big(F32), big(BF16), big(F32), big(F32), big(F32), big(F32),
                        big(F32), big(BF16), big(BF16), big(F32),
                        pltpu.VMEM(vec, F32), pltpu.VMEM(vec, F32), pltpu.VMEM(vec, F32)],
        compiler_params=_params("arbitrary"),
    )(z, z, h, h, dy, *pr)


def _inproj_bwd(dz, wg_in, x, dh1, pre_g, tm, tile0, nt, prev, last, token, name):
    t = x.shape[0]

    def body(*refs):
        dz_ref, w_ref, x_ref, dh1_ref, g_ref = refs[:5]
        gx_ref, gpre_ref, acc_s = refs[-3:]
        i = pl.program_id(0)

        @pl.when(i == 0)
        def _():
            gpre_ref[...] = jnp.zeros_like(gpre_ref) if prev is None else refs[7][...]

        acc = _dot_nt(dz_ref[:, 0:W_IN_COLS], w_ref[0])
        for k in range(1, N_CHIPS):
            acc = acc + _dot_nt(dz_ref[:, k * W_IN_COLS:(k + 1) * W_IN_COLS], w_ref[k])
        acc_s[...] = acc
        for s in range(tm // CHUNK):
            rows = slice(s * CHUNK, (s + 1) * CHUNK)
            xv = x_ref[rows, :]
            r = lax.rsqrt(_lanemean(xv * xv) + EPS)
            xhat = xv * r
            dhn = acc_s[rows, :]
            gpre_ref[...] += _rowsum8(dhn * xhat)
            dxh = dhn * g_ref[...]
            gx_ref[rows, :] = dh1_ref[rows, :] + r * (dxh - xhat * _lanemean(dxh * xhat))

        if last:
            @pl.when(i == nt - 1)
            def _():
                gpre_ref[...] = jnp.broadcast_to(jnp.sum(gpre_ref[...], axis=0, keepdims=True), gpre_ref.shape)

    row = lambda n: pl.BlockSpec((tm, n), lambda i: (tile0 + i, 0))
    small = lambda r: pl.BlockSpec((r, D_MODEL), lambda i: (0, 0))
    tok = pl.BlockSpec((SUBLANES, LANES), lambda i: (0, 0))
    in_specs = [row(D_Z), pl.BlockSpec(wg_in.shape, lambda i: (0, 0, 0), pipeline_mode=pl.Buffered(1)),
                row(D_MODEL), row(D_MODEL), small(1), tok]
    args = [dz, wg_in, x, dh1, pre_g, token]
    aliases = {}
    if prev is not None:
        in_specs += [ANY, small(SUBLANES)]
        args += list(prev)
        aliases = {6: 0}
    return pl.pallas_call(
        body, name=name, grid=(nt,), in_specs=in_specs, out_specs=[row(D_MODEL), small(SUBLANES)],
        out_shape=[jax.ShapeDtypeStruct((t, D_MODEL), F32), jax.ShapeDtypeStruct((SUBLANES, D_MODEL), F32)],
        input_output_aliases=aliases,
        scratch_shapes=[pltpu.VMEM((tm, D_MODEL), F32)],
        compiler_params=_params("arbitrary"),
    )(*args)


def _weight_grad(a, b, name, kb, nb, tk, tn, tt, token, a_transposed=False):
    t = b.shape[0]
    tt = min(tt, t)

    def body(a_ref, b_ref, token_ref, o_ref):
        @pl.when(pl.program_id(2) == 0)
        def _():
            o_ref[...] = jnp.zeros_like(o_ref)

        o_ref[...] += (_dot if a_transposed else _dot_tn)(a_ref[...], b_ref[...])

    a_spec = (pl.BlockSpec((tk, tt), lambda j, i, s: (i, s)) if a_transposed
              else pl.BlockSpec((tt, tk), lambda j, i, s: (s, i)))
    return pl.pallas_call(
        body, name=name, grid=(nb, kb, t // tt),
        in_specs=[a_spec, pl.BlockSpec((tt, tn), lambda j, i, s: (s, j)),
                  pl.BlockSpec((SUBLANES, LANES), lambda j, i, s: (0, 0))],
        out_specs=pl.BlockSpec((None, None, tk, tn), lambda j, i, s: (j, i, 0, 0)),
        out_shape=jax.ShapeDtypeStruct((nb, kb, tk, tn), F32),
        compiler_params=_params("parallel", "parallel", "arbitrary"),
    )(a, b, token)


def _place():
    x, y, c = lax.axis_index("x"), lax.axis_index("y"), lax.axis_index("c")
    return x, y, c


def _chip_of(x, y):
    return 2 * x + y


HBM = pl.BlockSpec(memory_space=pltpu.HBM)
SEM = pl.BlockSpec(memory_space=pltpu.SEMAPHORE)
EFFECT = pltpu.SideEffectType.DATAFLOW_SIDE_EFFECTING


def _hbm(a):
    return pltpu.with_memory_space_constraint(a, pltpu.HBM)


def _landing(shape, dtype):
    return _hbm(lax.empty(shape, dtype))


def _exchange_start(name, arrays, ncopies, build, after=None):
    n = len(arrays)
    extra = [] if after is None else [after]

    def body(*refs):
        ins, token = refs[:n], refs[-1]
        send_sems, recv_sems = refs[n + len(extra)], refs[n + len(extra) + 1]
        for cp in build(ins, send_sems, recv_sems):
            cp.start()
        token[...] = jnp.zeros_like(token)

    outs = pl.pallas_call(
        body, name=name,
        out_shape=(pltpu.SemaphoreType.DMA((ncopies,)), pltpu.SemaphoreType.DMA((ncopies,)),
                   *[pltpu.HBM(a.shape, a.dtype) for a in arrays], jax.ShapeDtypeStruct((SUBLANES, LANES), F32)),
        in_specs=[HBM] * n + [ANY] * len(extra),
        out_specs=(SEM, SEM, *[HBM] * n, pl.BlockSpec(memory_space=pltpu.VMEM)),
        input_output_aliases={q: q + 2 for q in range(n)},
        compiler_params=pltpu.CompilerParams(has_side_effects=EFFECT),
    )(*[_hbm(a) for a in arrays], *extra)
    return (outs[0], outs[1], list(outs[2:2 + n])), outs[-1]


def _exchange_wait(name, started, after, build):
    send, recv, arrays = started
    n = len(arrays)

    def body(*refs):
        ins, send_sems, recv_sems = refs[:n], refs[n], refs[n + 1]
        for cp in build(ins, send_sems, recv_sems):
            cp.wait_send()
            cp.wait_recv()

    return pl.pallas_call(
        body, name=name, out_shape=tuple(pltpu.HBM(a.shape, a.dtype) for a in arrays),
        in_specs=[HBM] * n + [SEM, SEM, ANY], out_specs=tuple([HBM] * n),
        input_output_aliases={q: q for q in range(n)},
        compiler_params=pltpu.CompilerParams(has_side_effects=EFFECT),
    )(*arrays, send, recv, after)


def _exchange_wait_start(name, started, after, build_wait, ncopies, build_start):
    send, recv, arrays = started
    n = len(arrays)

    def body(*refs):
        ins, send_sems, recv_sems = refs[:n], refs[n], refs[n + 1]
        send2, recv2, token = refs[n + 3], refs[n + 4], refs[-1]
        arrived = build_wait(ins, send_sems, recv_sems)
        for cp, onward in zip(arrived, build_start(ins, send2, recv2)):
            cp.wait_recv()
            onward.start()
        for cp in arrived:
            cp.wait_send()
        token[...] = jnp.zeros_like(token)

    outs = pl.pallas_call(
        body, name=name,
        out_shape=(pltpu.SemaphoreType.DMA((ncopies,)), pltpu.SemaphoreType.DMA((ncopies,)),
                   *[pltpu.HBM(a.shape, a.dtype) for a in arrays], jax.ShapeDtypeStruct((SUBLANES, LANES), F32)),
        in_specs=[HBM] * n + [SEM, SEM, ANY], out_specs=(SEM, SEM, *[HBM] * n, pl.BlockSpec(memory_space=pltpu.VMEM)),
        input_output_aliases={q: q + 2 for q in range(n)},
        compiler_params=pltpu.CompilerParams(has_side_effects=EFFECT),
    )(*arrays, send, recv, after)
    return (outs[0], outs[1], list(outs[2:2 + n])), outs[-1]


def _cast_into_slot(w, kc, name, dtype=BF16, token=None):
    rows, cols = w.shape
    tr = min(rows, 4 * SUM_TILE)
    extra = [] if token is None else [token]

    def body(kc_ref, w_ref, *rest):
        rest[-1][...] = w_ref[...].astype(dtype)

    grid_spec = pltpu.PrefetchScalarGridSpec(
        num_scalar_prefetch=1, grid=(rows // tr,),
        in_specs=[pl.BlockSpec((tr, cols), lambda r, kc: (r, 0))]
                 + [pl.BlockSpec((SUBLANES, LANES), lambda r, kc: (0, 0))] * len(extra),
        out_specs=pl.BlockSpec((None, tr, cols), lambda r, kc: (kc[0], r, 0)))
    return pl.pallas_call(
        body, name=name, grid_spec=grid_spec, out_shape=jax.ShapeDtypeStruct((N_CHIPS, rows, cols), dtype),
        compiler_params=_params("arbitrary"),
    )(kc, w, *extra)


def _gather_ici_copies(n):
    def build(refs, send_sems, recv_sems):
        x, y, c = _place()
        mine = lambda b: refs[b].at[_chip_of(x, y), c]
        chips = [(1 - x, y), (x, 1 - y), (1 - x, 1 - y)]
        return [pltpu.make_async_remote_copy(
            src_ref=mine(b), dst_ref=mine(b), send_sem=send_sems.at[3 * b + j], recv_sem=recv_sems.at[3 * b + j],
            device_id=(*chip, c), device_id_type=MESH) for b in range(n) for j, chip in enumerate(chips)]
    return build


def _gather_direct_copies(n):
    def build(refs, send_sems, recv_sems):
        x, y, c = _place()
        mine = lambda b: refs[b].at[_chip_of(x, y)]
        chips = [(1 - x, y), (x, 1 - y), (1 - x, 1 - y)]
        return [pltpu.make_async_remote_copy(
            src_ref=mine(b), dst_ref=mine(b), send_sem=send_sems.at[3 * b + j], recv_sem=recv_sems.at[3 * b + j],
            device_id=(*chip, c), device_id_type=MESH) for b in range(n) for j, chip in enumerate(chips)]
    return build


def _gather_relay_copies(n):
    def build(refs, send_sems, recv_sems):
        x, y, c = _place()
        chips = [(1 - x, y), (x, 1 - y), (1 - x, 1 - y)]
        cps = []
        for b in range(n):
            for j, chip in enumerate(chips):
                got = refs[b].at[_chip_of(*chip), c]
                cps.append(pltpu.make_async_remote_copy(
                    src_ref=got, dst_ref=got, send_sem=send_sems.at[3 * b + j], recv_sem=recv_sems.at[3 * b + j],
                    device_id=(x, y, 1 - c), device_id_type=MESH))
        return cps
    return build


def _sibling_copies(n):
    def build(refs, send_sems, recv_sems):
        x, y, c = _place()
        return [pltpu.make_async_remote_copy(
            src_ref=refs[b].at[:, 1 - c], dst_ref=refs[n + b], send_sem=send_sems.at[b], recv_sem=recv_sems.at[b],
            device_id=(x, y, 1 - c), device_id_type=MESH) for b in range(n)]
    return build


def _chip_copies(n):
    def build(refs, send_sems, recv_sems):
        x, y, c = _place()
        chips = [(1 - x, y), (x, 1 - y), (1 - x, 1 - y)]
        return [pltpu.make_async_remote_copy(
            src_ref=refs[b].at[_chip_of(*chip)], dst_ref=refs[n + b].at[j],
            send_sem=send_sems.at[3 * b + j], recv_sem=recv_sems.at[3 * b + j],
            device_id=(*chip, c), device_id_type=MESH) for b in range(n) for j, chip in enumerate(chips)]
    return build


def _finish_copies(n, n_all):
    def build(refs, send_sems, recv_sems):
        x, y, c = _place()
        cps = [pltpu.make_async_remote_copy(
            src_ref=refs[b].at[c], dst_ref=refs[b].at[c], send_sem=send_sems.at[b], recv_sem=recv_sems.at[b],
            device_id=(x, y, 1 - c), device_id_type=MESH) for b in range(n)]
        flips = [(fx, fy, fc) for fx in (0, 1) for fy in (0, 1) for fc in (0, 1)][1:]
        for b in range(n_all):
            mine = refs[n + b].at[_chip_of(x, y), c]
            cps += [pltpu.make_async_remote_copy(
                src_ref=mine, dst_ref=mine, send_sem=send_sems.at[n + 7 * b + q], recv_sem=recv_sems.at[n + 7 * b + q],
                device_id=(x ^ fx, y ^ fy, c ^ fc), device_id_type=MESH) for q, (fx, fy, fc) in enumerate(flips)]
        return cps
    return build


def _pair_sum(g, r1, kc, name, tr, send_dtype):
    nk, _, rows, cols = g.shape
    tr = min(tr, rows)

    def body(kc_ref, g_ref, r_ref, p_ref, own_ref):
        s = g_ref[...] + r_ref[...]
        p_ref[...] = s.astype(send_dtype)

        @pl.when(pl.program_id(1) == kc_ref[0])
        def _():
            own_ref[...] = s

    grid_spec = pltpu.PrefetchScalarGridSpec(
        num_scalar_prefetch=1, grid=(rows // tr, nk),
        in_specs=[pl.BlockSpec((None, None, tr, cols), lambda r, k, kc: (k, kc[1], r, 0)),
                  pl.BlockSpec((None, tr, cols), lambda r, k, kc: (k, r, 0))],
        out_specs=[pl.BlockSpec((None, tr, cols), lambda r, k, kc: (k, r, 0)),
                   pl.BlockSpec((tr, cols), lambda r, k, kc: (r, 0))])
    return pl.pallas_call(
        body, name=name, grid_spec=grid_spec,
        out_shape=[jax.ShapeDtypeStruct((nk, rows, cols), send_dtype), jax.ShapeDtypeStruct((rows, cols), F32)],
        compiler_params=_params("arbitrary", "arbitrary"),
    )(kc, g, r1)


def _chip_sum(own, r2, slot, lead, name, tr):
    rows, cols = own.shape
    tr = min(tr, rows)
    nl = len(lead)

    def body(slot_ref, o_ref, r_ref, s_ref):
        s = o_ref[...]
        for j in range(3):
            s = s + r_ref[j].astype(F32)
        s_ref[...] = s

    grid_spec = pltpu.PrefetchScalarGridSpec(
        num_scalar_prefetch=1, grid=(rows // tr,),
        in_specs=[pl.BlockSpec((tr, cols), lambda r, sl: (r, 0)), pl.BlockSpec((3, tr, cols), lambda r, sl: (0, r, 0))],
        out_specs=pl.BlockSpec((None,) * nl + (tr, cols), lambda r, sl: tuple(sl[q] for q in range(nl)) + (r, 0)))
    return pl.pallas_call(
        body, name=name, grid_spec=grid_spec, out_shape=jax.ShapeDtypeStruct(tuple(lead) + (rows, cols), F32),
        compiler_params=_params("arbitrary"),
    )(slot, own, r2)


def _adam_update(w, g, m, v):
    nm = ADAM_B1 * m + (1.0 - ADAM_B1) * g
    nv = ADAM_B2 * v + (1.0 - ADAM_B2) * (g * g)
    m_hat = nm / (1.0 - ADAM_B1 ** ADAM_STEP)
    v_hat = nv / (1.0 - ADAM_B2 ** ADAM_STEP)
    return -ADAM_LR * (m_hat / (jnp.sqrt(v_hat) + ADAM_EPS) + ADAM_WD * w), nm, nv


def _adamw(w, g, m, v, name, tr, token):
    rows, cols = w.shape
    tr = min(tr, rows)

    def body(w_ref, g_ref, m_ref, v_ref, token_ref, go_ref, d_ref, nm_ref, nv_ref):
        gv = g_ref[...]
        go_ref[...] = gv
        d_ref[...], nm_ref[...], nv_ref[...] = _adam_update(w_ref[...], gv, m_ref[...], v_ref[...])

    spec = pl.BlockSpec((tr, cols), lambda r: (r, 0))
    return pl.pallas_call(
        body, name=name, grid=(rows // tr,),
        in_specs=[spec] * 4 + [pl.BlockSpec((SUBLANES, LANES), lambda r: (0, 0))], out_specs=[spec] * 4,
        out_shape=[jax.ShapeDtypeStruct((rows, cols), F32)] * 4,
        compiler_params=_params("parallel"),
    )(w, g, m, v, token)


def _adamw_small(packed_g, pre_g_parts, ws, ms, vs):
    names = ["pre_g"] + [n for n, _ in SMALL_ROWS if n != "conv_w"]
    rows = dict(SMALL_ROWS)
    offset, at = {}, 0
    for n, r in SMALL_ROWS:
        offset[n] = at
        at += r
    k = len(names)

    def body(*refs):
        g_ref, pg_ref = refs[0], refs[1]
        w_refs, m_refs, v_refs = refs[2:2 + k], refs[2 + k:2 + 2 * k], refs[2 + 2 * k:2 + 3 * k]
        outs = refs[2 + 3 * k:]
        go, do, mo, vo = outs[:k], outs[k:2 * k], outs[2 * k:3 * k], outs[3 * k:4 * k]
        pre = pg_ref[0]
        for dev in range(1, 8):
            pre = pre + pg_ref[dev]
        outs[4 * k][...] = pre[D_MODEL // LANES:, :]
        for i, n in enumerate(names):
            shp = w_refs[i].shape
            if len(shp) == 2 and shp[0] == 1:
                for r in range(shp[1] // LANES):
                    cols = slice(r * LANES, (r + 1) * LANES)
                    g = pre[r:r + 1, :] if n == "pre_g" else g_ref[offset[n] + r:offset[n] + r + 1, :]
                    go[i][:, cols] = g
                    do[i][:, cols], mo[i][:, cols], vo[i][:, cols] = _adam_update(
                        w_refs[i][:, cols], g, m_refs[i][:, cols], v_refs[i][:, cols])
            else:
                g = g_ref[offset[n]:offset[n] + rows[n], :].reshape(shp)
                go[i][...] = g
                do[i][...], mo[i][...], vo[i][...] = _adam_update(w_refs[i][...], g, m_refs[i][...], v_refs[i][...])

    vm = pl.BlockSpec(memory_space=pltpu.VMEM)
    args = [packed_g, pre_g_parts] + [src[n] for src in (ws, ms, vs) for n in names]
    out_shape = [jax.ShapeDtypeStruct(ws[n].shape, F32) for _ in range(4) for n in names]
    out_shape.append(jax.ShapeDtypeStruct((SUBLANES, LANES), F32))
    outs = pl.pallas_call(
        body, name="adamw_small", in_specs=[vm] * len(args), out_specs=[vm] * (4 * k + 1), out_shape=out_shape,
    )(*args)
    return [dict(zip(names, outs[q * k:(q + 1) * k])) for q in range(4)], outs[4 * k]


def _into_slot(v, tail, slot, lead, name):
    n = v.shape[1]
    nl = len(lead)
    rows = n // LANES + SUBLANES

    def body(slot_ref, v_ref, t_ref, o_ref):
        for r in range(n // LANES):
            o_ref[r:r + 1, :] = v_ref[0:1, r * LANES:(r + 1) * LANES]
        o_ref[n // LANES:, :] = t_ref[...]

    grid_spec = pltpu.PrefetchScalarGridSpec(
        num_scalar_prefetch=1, grid=(1,),
        in_specs=[pl.BlockSpec(v.shape, lambda i, sl: (0, 0)), pl.BlockSpec(tail.shape, lambda i, sl: (0, 0))],
        out_specs=pl.BlockSpec((None,) * nl + (rows, LANES), lambda i, sl: tuple(sl[q] for q in range(nl)) + (0, 0)))
    return pl.pallas_call(
        body, name=name, grid_spec=grid_spec, out_shape=jax.ShapeDtypeStruct(tuple(lead) + (rows, LANES), F32),
    )(slot, v, tail)


def _pack_small(parts):
    names = [n for n, _ in SMALL_ROWS]
    offset, at = {}, 0
    for n, r in SMALL_ROWS:
        offset[n] = at
        at += r

    def body(*refs):
        ins, o_ref = dict(zip(names, refs[:-1])), refs[-1]
        o_ref[SMALL_USED:, :] = jnp.zeros((SMALL_TOTAL - SMALL_USED, LANES), F32)
        for n, rows in SMALL_ROWS:
            ref, at = ins[n], offset[n]
            if n == "gmlp_bs":
                for h in range(N_HEADS):
                    o_ref[at + h:at + h + 1, :] = jnp.transpose(ref[:, h * CHUNK:(h + 1) * CHUNK])[0:1, :]
            elif n == "conv_w":
                for k in range(CONV_W):
                    for r in range(D_HALF // LANES):
                        row = at + k * (D_HALF // LANES) + r
                        o_ref[row:row + 1, :] = ref[k * SUBLANES:k * SUBLANES + 1, r * LANES:(r + 1) * LANES]
            elif ref.ndim == 3:
                o_ref[at:at + rows, :] = ref[...].reshape(rows, LANES)
            else:
                for r in range(rows):
                    o_ref[at + r:at + r + 1, :] = ref[0:1, r * LANES:(r + 1) * LANES]

    vm = pl.BlockSpec(memory_space=pltpu.VMEM)
    return pl.pallas_call(
        body, name="pack_small", in_specs=[vm] * len(names), out_specs=vm,
        out_shape=jax.ShapeDtypeStruct((SMALL_TOTAL, LANES), F32),
    )(*[parts[n] for n in names])


def kernel(x, p, pre_g, w_in, gmlp_ln_g, gmlp_ln_b, gmlp_ws, gmlp_bs, conv_w, conv_b, w_a, b_a, w_x, b_x, lam, gmlp_out_g, lru_out_g, w_out, post_g, w_pe, w_pg, loss_target, m_pre_g, m_w_in, m_gmlp_ln_g, m_gmlp_ln_b, m_gmlp_ws, m_gmlp_bs, m_conv_w, m_conv_b, m_w_a, m_b_a, m_w_x, m_b_x, m_lam, m_gmlp_out_g, m_lru_out_g, m_w_out, m_post_g, m_w_pe, m_w_pg, v_pre_g, v_w_in, v_gmlp_ln_g, v_gmlp_ln_b, v_gmlp_ws, v_gmlp_bs, v_conv_w, v_conv_b, v_w_a, v_b_a, v_w_x, v_b_x, v_lam, v_gmlp_out_g, v_lru_out_g, v_w_out, v_post_g, v_w_pe, v_w_pg):
    weights = dict(pre_g=pre_g, w_in=w_in, gmlp_ln_g=gmlp_ln_g, gmlp_ln_b=gmlp_ln_b, gmlp_ws=gmlp_ws, gmlp_bs=gmlp_bs,
                   conv_w=conv_w, conv_b=conv_b, w_a=w_a, b_a=b_a, w_x=w_x, b_x=b_x, lam=lam, gmlp_out_g=gmlp_out_g,
                   lru_out_g=lru_out_g, w_out=w_out, post_g=post_g, w_pe=w_pe, w_pg=w_pg)
    mom_m = dict(pre_g=m_pre_g, w_in=m_w_in, gmlp_ln_g=m_gmlp_ln_g, gmlp_ln_b=m_gmlp_ln_b, gmlp_ws=m_gmlp_ws,
                 gmlp_bs=m_gmlp_bs, conv_w=m_conv_w, conv_b=m_conv_b, w_a=m_w_a, b_a=m_b_a, w_x=m_w_x, b_x=m_b_x,
                 lam=m_lam, gmlp_out_g=m_gmlp_out_g, lru_out_g=m_lru_out_g, w_out=m_w_out, post_g=m_post_g,
                 w_pe=m_w_pe, w_pg=m_w_pg)
    mom_v = dict(pre_g=v_pre_g, w_in=v_w_in, gmlp_ln_g=v_gmlp_ln_g, gmlp_ln_b=v_gmlp_ln_b, gmlp_ws=v_gmlp_ws,
                 gmlp_bs=v_gmlp_bs, conv_w=v_conv_w, conv_b=v_conv_b, w_a=v_w_a, b_a=v_b_a, w_x=v_w_x, b_x=v_b_x,
                 lam=v_lam, gmlp_out_g=v_gmlp_out_g, lru_out_g=v_lru_out_g, w_out=v_w_out, post_g=v_post_g,
                 w_pe=v_w_pe, w_pg=v_w_pg)
    order = list(weights)
    xi, yi, ci = _place()
    me = _chip_of(xi, yi)
    kc = jnp.stack([me, ci]).astype(jnp.int32)

    x2 = x[0]
    p2 = p[0, 0]
    tgt = loss_target[0]

    first = [_cast_into_slot(w_in[0], kc, "cast_w_in").reshape(N_CHIPS, 2, D_MODEL // 2, W_IN_COLS),
             _cast_into_slot(conv_w[0, :, 0, :], kc, "conv_w_into_slot", F32).reshape(N_CHIPS, 2, CONV_W // 2, CONV_COLS)]
    in_st, in_tok = _exchange_start("gather_in_start", first, 6, _gather_ici_copies(2))
    later = [_cast_into_slot(w_out[0], kc, "cast_w_out", token=in_tok).reshape(N_CHIPS, 2, W_ROWS // 2, D_MODEL),
             _cast_into_slot(w_pg[0], kc, "cast_w_pg", token=in_tok).reshape(N_CHIPS, 2, W_ROWS // 2, D_MODEL),
             _cast_into_slot(w_pe[0], kc, "cast_w_pe", token=in_tok).reshape(N_CHIPS, 2, D_PLE // 2, W_PE_COLS)]
    gather_st, gather_tok = _exchange_start("gather_start", later, 9, _gather_direct_copies(3), after=in_tok)
    hn, z_own, hn_t = _inproj_local(x2, pre_g, w_in[0], ROW_TILE, gather_tok)
    in_st, in_tok = _exchange_wait_start("gather_in_relay", in_st, z_own, _gather_ici_copies(2), 6,
                                         _gather_relay_copies(2))
    g_in, g_cw = _exchange_wait("gather_in_wait", in_st, in_tok, _gather_relay_copies(2))
    wg_in = g_in.reshape(N_CHIPS, D_MODEL, W_IN_COLS)
    cw_full = jnp.transpose(g_cw.reshape(N_CHIPS, CONV_W, CONV_COLS), (1, 0, 2)).reshape(CONV_W, D_HALF)

    causal = jnp.tril(jnp.ones((CHUNK, CHUNK), dtype=bool))
    ws_m = jnp.where(causal[None], gmlp_ws[0], 0.0)
    prm = dict(
        ln_g=gmlp_ln_g, ln_b=gmlp_ln_b, wt=ws_m.astype(BF16), wtt=jnp.transpose(ws_m, (0, 2, 1)).astype(BF16),
        bsx=jnp.repeat(jnp.transpose(gmlp_bs[0]), CHUNK, axis=1),
        conv_w=cw_full, conv_b=conv_b, w_a=w_a[0].astype(BF16), w_x=w_x[0].astype(BF16),
        b_a=b_a[0].reshape(1, D_HALF), b_x=b_x[0].reshape(1, D_HALF), lam=lam, oga=gmlp_out_g, ogb=lru_out_g)

    z, y, h = _inproj_branches_fwd(hn, z_own, wg_in, kc, prm, ROW_TILE, gather_tok)
    g_out, g_pg, g_pe = _exchange_wait("gather_wait", gather_st, y, _gather_direct_copies(3))
    wg_out = g_out.reshape(D_MODEL, D_MODEL)
    wg_pg = g_pg.reshape(D_MODEL, D_MODEL)
    wg_pe = g_pe.reshape(N_CHIPS, D_PLE, W_PE_COLS)
    h1, dq, dh1, do, dy, gw_pe, g_post, loss_acc = _head_fwd_bwd(x2, y, p2, tgt, post_g, wg_out, wg_pg, wg_pe,
                                                                 ROW_TILE)

    def sibling_start(tag, bufs):
        lands = [_landing((b.shape[0],) + b.shape[2:], b.dtype) for b in bufs]
        return _exchange_start("sibling_start_" + tag, bufs + lands, len(bufs), _sibling_copies(len(bufs)))

    def pair_then_chip_start(tag, started, after, names, tiles, dtypes):
        n = len(names)
        got = _exchange_wait("sibling_wait_" + tag, started, after, _sibling_copies(n))
        pairs = [_pair_sum(got[b], got[n + b], kc, "pair_sum_" + names[b], tiles[b], dtypes[b]) for b in range(n)]
        lands = [_landing((3,) + pr[0].shape[1:], pr[0].dtype) for pr in pairs]
        return _exchange_start("chip_start_" + tag, [pr[0] for pr in pairs] + lands, 3 * n, _chip_copies(n)), pairs

    def sum_then_finish_start(tag, started, pairs, after, names, tiles, small, to_all=()):
        n = len(names)
        got = _exchange_wait("chip_wait_" + tag, started, after, _chip_copies(n))
        sums = [_chip_sum(pairs[b][1], got[n + b], kc if small and b == n - 1 else kc[1:],
                          (N_CHIPS, 2) if small and b == n - 1 else (2,), "chip_sum_" + names[b], tiles[b])
                for b in range(n)]
        nbig = n - 1 if small else n
        n_all = n - nbig + len(to_all)
        return _exchange_start("finish_start_" + tag, sums + list(to_all), nbig + 7 * n_all,
                               _finish_copies(nbig, n_all))

    gw_pe = gw_pe.reshape(N_CHIPS, 2, D_PLE // 2, W_PE_COLS)
    token0 = jnp.zeros((SUBLANES, LANES), F32)
    gw_out = _weight_grad(y, do, "grad_w_out", 2, 1, D_MODEL // 2, D_MODEL, CONTRACT_TILE, token0)
    gw_pg = _weight_grad(h1, dq, "grad_w_pg", 2, 1, D_MODEL // 2, D_MODEL, CONTRACT_TILE, token0)
    gw_out = gw_out.reshape(N_CHIPS, 2, W_ROWS // 2, D_MODEL)
    gw_pg = gw_pg.reshape(N_CHIPS, 2, W_ROWS // 2, D_MODEL)

    names_a, tiles_a = ["w_out", "w_pg", "w_pe"], [SUM_TILE] * 3
    st, tok = sibling_start("a", [gw_out, gw_pg, gw_pe])
    (dz, g_oga, g_ogb, g_lng, g_lnb, g_bsx, g_ws, g_cw, g_cb, g_wa, g_ba, g_wx, g_bx, g_lam) = _branches_bwd(
        z, h, dy, prm, ROW_TILE, tok)
    (st, tok), pairs_a = pair_then_chip_start("a", st, dz, names_a, tiles_a, [BF16] * 3)
    gw_in = _weight_grad(hn_t, dz, "grad_w_in", 2, N_CHIPS, D_MODEL // 2, W_IN_COLS, CONTRACT_TILE, tok,
                         a_transposed=True)
    fin_a, tok = sum_then_finish_start("a", st, pairs_a, gw_in, names_a, tiles_a, False)

    small_g = dict(
        gmlp_ln_g=g_lng, gmlp_ln_b=g_lnb, gmlp_ws=g_ws, gmlp_bs=g_bsx, conv_w=g_cw, conv_b=g_cb, w_a=g_wa, b_a=g_ba,
        w_x=g_wx, b_x=g_bx, lam=g_lam, gmlp_out_g=g_oga, lru_out_g=g_ogb, post_g=g_post)
    gsm = _pack_small(small_g).reshape(N_CHIPS, 2, SMALL_PIECE, LANES)

    names_b, tiles_b = ["w_in", "small"], [2 * SUM_TILE, SMALL_PIECE]
    n_tiles = x2.shape[0] // ROW_TILE
    n_lo = max(1, (5 * n_tiles) // 16)
    st, tok_b = _exchange_start(
        "sibling_start_b", [gw_in, gsm] + [_landing((N_CHIPS,) + b.shape[2:], F32) for b in (gw_in, gsm)], 2,
        _sibling_copies(2), after=tok)
    part = _inproj_bwd(dz, wg_in, x2, dh1, pre_g, ROW_TILE, 0, n_lo, None, False, tok_b, "inproj_bwd_lo")
    f_out, f_pg, f_pe = _exchange_wait("finish_wait_a", fin_a, part[1], _finish_copies(3, 0))
    (st, tok_b), pairs_b = pair_then_chip_start("b", st, part[1], names_b, tiles_b, [BF16, F32])
    grad_x, g_pre = _inproj_bwd(dz, wg_in, x2, dh1, pre_g, ROW_TILE, n_lo, n_tiles - n_lo, part, True, tok_b,
                                "inproj_bwd_hi")
    pre_parts = _into_slot(g_pre, loss_acc, kc, (N_CHIPS, 2), "pre_g_into_slot")
    fin_b, tok_b = sum_then_finish_start("b", st, pairs_b, g_pre, names_b, tiles_b, True, to_all=[pre_parts])

    grads, deltas, new_m, new_v = {}, {}, {}, {}

    def adam_big(n, g2d, tr, token):
        shp = weights[n].shape
        g, d, nm, nv = _adamw(weights[n][0], g2d, mom_m[n][0], mom_v[n][0], "adamw_" + n, tr, token)
        grads[n], deltas[n], new_m[n], new_v[n] = g.reshape(shp), d.reshape(shp), nm.reshape(shp), nv.reshape(shp)
        return d

    as_token = lambda d: d[:SUBLANES, :LANES]
    last = adam_big("w_out", f_out.reshape(W_ROWS, D_MODEL), SUM_TILE, tok_b)
    last = adam_big("w_pg", f_pg.reshape(W_ROWS, D_MODEL), SUM_TILE, as_token(last))
    last = adam_big("w_pe", f_pe.reshape(D_PLE, W_PE_COLS), SUM_TILE, as_token(last))
    f_in, f_sm, pre_parts = _exchange_wait("finish_wait_b", fin_b, last, _finish_copies(1, 2))
    adam_big("w_in", f_in.reshape(D_MODEL, W_IN_COLS), 2 * SUM_TILE, tok_b)

    packed_g = f_sm.reshape(SMALL_TOTAL, LANES)
    small_names = ["pre_g"] + [n for n, _ in SMALL_ROWS if n != "conv_w"]
    natural = lambda src: {n: (src[n] if src[n].ndim == 2 else src[n][0]) for n in small_names}
    outs, loss_block = _adamw_small(packed_g, pre_parts.reshape(8, D_MODEL // LANES + SUBLANES, LANES),
                                    natural(weights), natural(mom_m), natural(mom_v))
    loss = loss_block[0, 0]
    for dst, got in zip((grads, deltas, new_m, new_v), outs):
        for n in small_names:
            dst[n] = got[n].reshape(weights[n].shape)
    at = sum(r for n, r in SMALL_ROWS[:[n for n, _ in SMALL_ROWS].index("conv_w")])
    g_cw_all = packed_g[at:at + CONV_W * D_HALF // LANES].reshape(CONV_W, D_HALF)
    g_conv = lax.dynamic_slice_in_dim(g_cw_all, me * CONV_COLS, CONV_COLS, axis=1)
    g, d, nm, nv = _adamw(conv_w[0, :, 0, :], g_conv, m_conv_w[0, :, 0, :], v_conv_w[0, :, 0, :], "adamw_conv_w", CONV_W,
                          tok_b)
    cshape = conv_w.shape
    grads["conv_w"], deltas["conv_w"] = g.reshape(cshape), d.reshape(cshape)
    new_m["conv_w"], new_v["conv_w"] = nm.reshape(cshape), nv.reshape(cshape)

    return (loss, grad_x.reshape(x.shape), *[grads[n] for n in order], *[deltas[n] for n in order],
            *[new_m[n] for n in order], *[new_v[n] for n in order])
```

```python
import math

import jax
import jax.numpy as jnp
from jax import lax
from jax.experimental import pallas as pl
from jax.experimental.pallas import tpu as pltpu

F32 = jnp.float32
BF16 = jnp.bfloat16

D_MODEL = 2048
D_HALF = 1024
D_Z = 5120
D_PLE = 256
CHUNK = 128
N_HEADS = 8
N_CHIPS = 4
W_IN_COLS = D_Z // N_CHIPS
W_ROWS = D_MODEL // N_CHIPS
W_PE_COLS = D_MODEL // N_CHIPS
CONV_W = 4
CONV_COLS = D_HALF // N_CHIPS
EPS = 1e-6
LRU_C = 8.0
ADAM_LR, ADAM_B1, ADAM_B2, ADAM_EPS, ADAM_WD, ADAM_STEP = 0.001, 0.9, 0.999, 1e-08, 0.01, 10

SUBLANES = 8
LANES = 128
VMEM_LIMIT = 56 * 1024 * 1024
ROW_TILE = 256
CONTRACT_TILE = 2048
SUM_TILE = 256

SMALL_ROWS = (("gmlp_ln_g", 8), ("gmlp_ln_b", 8), ("gmlp_ws", 1024), ("gmlp_bs", 8),
              ("conv_w", 32), ("conv_b", 8), ("w_a", 1024), ("b_a", 8), ("w_x", 1024), ("b_x", 8),
              ("lam", 8), ("gmlp_out_g", 8), ("lru_out_g", 8), ("post_g", 16))
SMALL_USED = sum(r for _, r in SMALL_ROWS)
SMALL_PIECE = 400
SMALL_TOTAL = 8 * SMALL_PIECE

MESH = pl.DeviceIdType.MESH
ANY = pl.BlockSpec(memory_space=pl.ANY)

_GELU_C0 = math.sqrt(2.0 / math.pi)
_GELU_C1 = 0.044715


def _params(*sem):
    return pltpu.CompilerParams(dimension_semantics=sem, vmem_limit_bytes=VMEM_LIMIT)


def _dot(a, b):
    return jnp.dot(a, b, preferred_element_type=F32)


def _dot_nt(a, b):
    return lax.dot_general(a, b, (((1,), (1,)), ((), ())), preferred_element_type=F32)


def _dot_tn(a, b):
    return lax.dot_general(a, b, (((0,), (0,)), ((), ())), preferred_element_type=F32)


def _gelu(x):
    t = jnp.tanh(_GELU_C0 * (x + _GELU_C1 * (x * x * x)))
    return 0.5 * x * (1.0 + t), t


def _gelu_grad(x, t):
    return 0.5 * (1.0 + t) + 0.5 * x * (1.0 - t * t) * (_GELU_C0 * (1.0 + 3.0 * _GELU_C1 * x * x))


def _rowsum8(v):
    r, n = v.shape
    return jnp.sum(v.reshape(r // SUBLANES, SUBLANES, n), axis=0)


def _lanemean(v):
    return jnp.mean(v, axis=-1, keepdims=True)


def _shift_down(v, halo8, k):
    if k == 0:
        return v
    r = pltpu.roll(v, k, 0)
    hr = pltpu.roll(halo8, k, 0)
    row = lax.broadcasted_iota(jnp.int32, halo8.shape, 0)
    top = jnp.where(row < k, hr, r[0:SUBLANES])
    return jnp.concatenate([top, r[SUBLANES:]], axis=0)


def _shift_up(v, next8, k):
    if k == 0:
        return v
    n = v.shape[0]
    r = pltpu.roll(v, n - k, 0)
    nr = pltpu.roll(next8, SUBLANES - k, 0)
    row = lax.broadcasted_iota(jnp.int32, next8.shape, 0)
    bot = jnp.where(row >= SUBLANES - k, nr, r[n - SUBLANES:])
    return jnp.concatenate([r[:n - SUBLANES], bot], axis=0)


def _layernorm_parts(vg):
    mu = _lanemean(vg)
    xc = vg - mu
    rstd = lax.rsqrt(_lanemean(xc * xc) + EPS)
    return xc * rstd, rstd


def _spatial_mix(wt_ref, vn_ref, bsx_ref, mixed_ref, tm):
    for c in range(tm // CHUNK):
        rows = slice(c * CHUNK, (c + 1) * CHUNK)
        for h in range(N_HEADS):
            cols = slice(h * CHUNK, (h + 1) * CHUNK)
            mixed_ref[rows, cols] = _dot(wt_ref[h], vn_ref[rows, cols]) + bsx_ref[:, cols]


def _conv_taps(xb, halo8):
    return [_shift_down(xb, halo8, CONV_W - 1 - k) for k in range(CONV_W)]


def _lru_gates(xc_bf_ref, wa_ref, wx_ref, ba_ref, bx_ref, r_ref, i_ref):
    for h in range(N_HEADS):
        cols = slice(h * CHUNK, (h + 1) * CHUNK)
        xh = xc_bf_ref[:, cols]
        r_ref[:, cols] = jax.nn.sigmoid(_dot(xh, wa_ref[h]) + ba_ref[:, cols])
        i_ref[:, cols] = jax.nn.sigmoid(_dot(xh, wx_ref[h]) + bx_ref[:, cols])


def _softplus_neg(lam):
    return jnp.maximum(-lam, 0.0) + jnp.log(1.0 + jnp.exp(-jnp.abs(lam)))


def _decay_parts(r, lam):
    la = (-LRU_C * _softplus_neg(lam)) * r
    a = jnp.exp(la)
    th = -jnp.tanh(la)
    mult = jnp.sqrt(2.0 * th / (1.0 + th))
    return a, mult


def _z_group(zref, g, rows=slice(None)):
    lo = g * D_HALF
    blk, off = lo // W_IN_COLS, lo % W_IN_COLS
    if off + D_HALF <= W_IN_COLS:
        return zref[blk, rows, off:off + D_HALF]
    return jnp.concatenate([zref[blk, rows, off:W_IN_COLS], zref[blk + 1, rows, 0:off + D_HALF - W_IN_COLS]], axis=1)


def _inproj_local(x, pre_g, w_own, tm, token):
    t = x.shape[0]

    def body(x_ref, g_ref, w_ref, token_ref, hn_ref, zl_ref, hnt_ref, wbf_s):
        @pl.when(pl.program_id(0) == 0)
        def _():
            wbf_s[...] = w_ref[...].astype(BF16)

        xv = x_ref[...]
        hnf = xv * lax.rsqrt(_lanemean(xv * xv) + EPS) * g_ref[...]
        hn = hnf.astype(BF16)
        hn_ref[...] = hn
        hnt_ref[...] = hnf.T.astype(BF16)
        zl_ref[...] = _dot(hn, wbf_s[...]).astype(BF16)

    row = lambda n: pl.BlockSpec((tm, n), lambda i: (i, 0))
    const = lambda shp: pl.BlockSpec(shp, lambda i: (0, 0), pipeline_mode=pl.Buffered(1))
    return pl.pallas_call(
        body, name="inproj_local", grid=(t // tm,),
        in_specs=[row(D_MODEL), const((1, D_MODEL)), const((D_MODEL, W_IN_COLS)), const((SUBLANES, LANES))],
        out_specs=[row(D_MODEL), row(W_IN_COLS), pl.BlockSpec((D_MODEL, tm), lambda i: (0, i))],
        out_shape=[jax.ShapeDtypeStruct((t, D_MODEL), BF16), jax.ShapeDtypeStruct((t, W_IN_COLS), BF16),
                   jax.ShapeDtypeStruct((D_MODEL, t), BF16)],
        scratch_shapes=[pltpu.VMEM((D_MODEL, W_IN_COLS), BF16)],
        compiler_params=_params("arbitrary"),
    )(x, pre_g, w_own, token)


def _inproj_branches_fwd(hn, z_own, wg_in, kc, prm, tm, token):
    t = hn.shape[0]
    nt = t // tm
    hb = tm // SUBLANES

    def body(kc_ref, hn_ref, zo_ref, w1_ref, w2_ref, w3_ref,
             lng_ref, lnb_ref, wt_ref, bsx_ref, cw_ref, cb_ref, wa_ref, wx_ref, ba_ref, bx_ref, lam_ref,
             oga_ref, ogb_ref, token_ref,
             z_ref, y_ref, h_ref,
             zbuf0, zbuf1, vn_s, mixed_s, xcbf_s, r_s, i_s, ug_s, halo_s, carry_s):
        s = pl.program_id(0)
        me = kc_ref[0]
        w_refs = (None, w1_ref, w2_ref, w3_ref)

        @pl.when(s == 0)
        def _():
            zbuf1[...] = jnp.zeros_like(zbuf1)

        @pl.when(s <= 1)
        def _():
            carry_s[...] = jnp.zeros_like(carry_s)
            halo_s[...] = jnp.zeros_like(halo_s)

        def step(zw, zr):
            def project(r):
                blk = (me + r) % N_CHIPS
                zb = zo_ref[...] if r == 0 else _dot(hn_ref[...], w_refs[r][...]).astype(BF16)
                z_ref[blk] = zb
                zw[blk] = zb

            zin = lambda g: _z_group(zr, g).astype(F32)
            always = [s >= 0] * 4

            @pl.when(always[0])
            def _():
                project(0)
                ug, _ = _gelu(zin(0))
                ug_s[...] = ug
                vg, _ = _gelu(zin(1))
                vhat, _ = _layernorm_parts(vg)
                vn_s[...] = (vhat * lng_ref[...] + lnb_ref[...]).astype(BF16)

            @pl.when(always[1])
            def _():
                project(1)
                _spatial_mix(wt_ref, vn_s, bsx_ref, mixed_s, tm)
                ga = zin(2)
                ya = ug_s[...] * mixed_s[...] * (ga * jax.nn.sigmoid(ga))
                ra = lax.rsqrt(_lanemean(ya * ya) + EPS)
                y_ref[:, 0:D_HALF] = (ya * ra * oga_ref[...]).astype(BF16)

            @pl.when(always[2])
            def _():
                project(2)
                xb = zin(3)
                taps = _conv_taps(xb, halo_s[...])
                halo_s[...] = xb[tm - SUBLANES:]
                xc = cb_ref[...] + taps[0] * cw_ref[0:1, :]
                for k in range(1, CONV_W):
                    xc = xc + taps[k] * cw_ref[k:k + 1, :]
                xcbf_s[...] = xc.astype(BF16)
                _lru_gates(xcbf_s, wa_ref, wx_ref, ba_ref, bx_ref, r_s, i_s)
                a, mult = _decay_parts(r_s[...], lam_ref[...])
                row = lax.broadcasted_iota(jnp.int32, a.shape, 0)
                mult = jnp.where(jnp.logical_and(s == 1, row == 0), 1.0, mult)
                r_s[...] = a
                i_s[...] = mult * (i_s[...] * xc)

            @pl.when(always[3])
            def _():
                project(3)
                a = r_s[...]
                b = i_s[...]
                r8 = lax.broadcasted_iota(jnp.int32, a.shape, 0) & (SUBLANES - 1)
                for d in (1, 2, 4):
                    a_sh = pltpu.roll(a, d, 0)
                    b_sh = pltpu.roll(b, d, 0)
                    m = r8 >= d
                    b = jnp.where(m, a * b_sh + b, b)
                    a = jnp.where(m, a * a_sh, a)
                carry = carry_s[...]
                for g in range(hb):
                    rows = slice(g * SUBLANES, (g + 1) * SUBLANES)
                    hg = a[rows] * carry + b[rows]
                    h_ref[rows, :] = hg
                    carry = jnp.broadcast_to(hg[SUBLANES - 1:SUBLANES, :], hg.shape)
                carry_s[...] = carry
                gb = zin(4)
                yb = h_ref[...] * (gb * jax.nn.sigmoid(gb))
                rb = lax.rsqrt(_lanemean(yb * yb) + EPS)
                y_ref[:, D_HALF:] = (yb * rb * ogb_ref[...]).astype(BF16)

        @pl.when(s % 2 == 0)
        def _():
            step(zbuf0, zbuf1)

        @pl.when(s % 2 == 1)
        def _():
            step(zbuf1, zbuf0)

    const = lambda a: pl.BlockSpec(a.shape, lambda s, kc, n=a.ndim: (0,) * n, pipeline_mode=pl.Buffered(1))
    proj = lambda n: pl.BlockSpec((tm, n), lambda s, kc: (jnp.minimum(s, nt - 1), 0))
    head = lambda n: pl.BlockSpec((tm, n), lambda s, kc: (jnp.maximum(s - 1, 0), 0))
    other = lambda r: pl.BlockSpec((None, D_MODEL, W_IN_COLS), lambda s, kc, r=r: ((kc[0] + r) % N_CHIPS, 0, 0),
                                   pipeline_mode=pl.Buffered(1))
    names = ("ln_g", "ln_b", "wt", "bsx", "conv_w", "conv_b", "w_a", "w_x", "b_a", "b_x", "lam", "oga", "ogb")
    pr = [prm[n] for n in names] + [token]
    big = lambda dt: pltpu.VMEM((tm, D_HALF), dt)
    zblocks = pltpu.VMEM((N_CHIPS, tm, W_IN_COLS), BF16)
    grid_spec = pltpu.PrefetchScalarGridSpec(
        num_scalar_prefetch=1, grid=(nt + 1,),
        in_specs=[proj(D_MODEL), proj(W_IN_COLS), other(1), other(2), other(3)] + [const(a) for a in pr],
        out_specs=[pl.BlockSpec((N_CHIPS, tm, W_IN_COLS), lambda s, kc: (0, jnp.minimum(s, nt - 1), 0)),
                   head(D_MODEL), head(D_HALF)],
        scratch_shapes=[zblocks, zblocks, big(BF16), big(F32), big(BF16), big(F32), big(F32), big(F32),
                        pltpu.VMEM((SUBLANES, D_HALF), F32), pltpu.VMEM((SUBLANES, D_HALF), F32)])
    return pl.pallas_call(
        body, name="inproj_branches_fwd", grid_spec=grid_spec,
        out_shape=[jax.ShapeDtypeStruct((N_CHIPS, t, W_IN_COLS), BF16), jax.ShapeDtypeStruct((t, D_MODEL), BF16),
                   jax.ShapeDtypeStruct((t, D_HALF), F32)],
        compiler_params=_params("arbitrary"),
    )(kc, hn, z_own, wg_in, wg_in, wg_in, *pr)


def _head_fwd_bwd(x, y, p, tgt, post_g, w_out, w_pg, wg_pe, tm):
    t = x.shape[0]

    def body(x_ref, y_ref, p_ref, tgt_ref, pg_ref, wo_ref, wpg_ref, wpe_ref,
             h1_ref, dq_ref, dh1_ref, do_ref, dy_ref, gwpe_ref, gpost_ref, loss_ref, dout_s):
        i = pl.program_id(0)

        @pl.when(i == 0)
        def _():
            gpost_ref[...] = jnp.zeros_like(gpost_ref)
            gwpe_ref[...] = jnp.zeros_like(gwpe_ref)
            loss_ref[...] = jnp.zeros_like(loss_ref)

        pb = p_ref[...].astype(BF16)
        pes = [_dot(pb, wpe_ref[k]) for k in range(N_CHIPS)]
        o = _dot(y_ref[...], wo_ref[...])
        r3 = lax.rsqrt(_lanemean(o * o) + EPS)
        on = o * r3
        h1 = x_ref[...] + on * pg_ref[...]
        h1b = h1.astype(BF16)
        h1_ref[...] = h1b
        gt = jax.nn.sigmoid(_dot(h1b, wpg_ref[...]))
        for k in range(N_CHIPS):
            cols = slice(k * W_PE_COLS, (k + 1) * W_PE_COLS)
            pe = pes[k]
            g = gt[:, cols]
            d = h1[:, cols] + pe * g - tgt_ref[:, cols]
            loss_ref[...] += jnp.sum(d * d) * (0.5 / D_MODEL)
            dout = d * (1.0 / D_MODEL)
            dout_s[:, cols] = dout
            dg = dout * g
            gwpe_ref[k] += _dot_tn(pb, dg.astype(BF16))
            dq_ref[:, cols] = (dg * pe * (1.0 - g)).astype(BF16)
        dh1 = dout_s[...] + _dot_nt(dq_ref[...], wpg_ref[...])
        dh1_ref[...] = dh1
        gpost_ref[...] += _rowsum8(dh1 * on)
        don = dh1 * pg_ref[...]
        dob = (r3 * (don - on * _lanemean(don * on))).astype(BF16)
        do_ref[...] = dob
        dy_ref[...] = _dot_nt(dob, wo_ref[...])

        @pl.when(i == pl.num_programs(0) - 1)
        def _():
            gpost_ref[...] = jnp.broadcast_to(jnp.sum(gpost_ref[...], axis=0, keepdims=True), gpost_ref.shape)

    row = lambda n: pl.BlockSpec((tm, n), lambda i: (i, 0))
    const = lambda shp: pl.BlockSpec(shp, lambda i, n=len(shp): (0,) * n, pipeline_mode=pl.Buffered(1))
    acc = lambda shp: pl.BlockSpec(shp, lambda i, n=len(shp): (0,) * n)
    return pl.pallas_call(
        body, name="head_fwd_bwd", grid=(t // tm,),
        in_specs=[row(D_MODEL), row(D_MODEL), row(D_PLE), row(D_MODEL), const((1, D_MODEL)),
                  const((D_MODEL, D_MODEL)), const((D_MODEL, D_MODEL)), const((N_CHIPS, D_PLE, W_PE_COLS))],
        out_specs=[row(D_MODEL), row(D_MODEL), row(D_MODEL), row(D_MODEL), row(D_MODEL),
                   acc((N_CHIPS, D_PLE, W_PE_COLS)), acc((SUBLANES, D_MODEL)), acc((SUBLANES, LANES))],
        out_shape=[jax.ShapeDtypeStruct((t, D_MODEL), BF16), jax.ShapeDtypeStruct((t, D_MODEL), BF16),
                   jax.ShapeDtypeStruct((t, D_MODEL), F32), jax.ShapeDtypeStruct((t, D_MODEL), BF16),
                   jax.ShapeDtypeStruct((t, D_MODEL), F32),
                   jax.ShapeDtypeStruct((N_CHIPS, D_PLE, W_PE_COLS), F32),
                   jax.ShapeDtypeStruct((SUBLANES, D_MODEL), F32), jax.ShapeDtypeStruct((SUBLANES, LANES), F32)],
        scratch_shapes=[pltpu.VMEM((tm, D_MODEL), F32)],
        compiler_params=_params("arbitrary"),
    )(x, y, p, tgt, post_g, w_out, w_pg, wg_pe)


def _branches_bwd(z, h, dy, prm, tm, token):
    t = h.shape[0]
    nt = t // tm
    hb = tm // SUBLANES

    def body(z_ref, zh_ref, h_ref, hh_ref, dy_ref,
             lng_ref, lnb_ref, wt_ref, wtt_ref, bsx_ref, cw_ref, cb_ref, wa_ref, wx_ref, ba_ref, bx_ref, lam_ref,
             oga_ref, ogb_ref, token_ref,
             dz_ref, g_oga, g_ogb, g_lng, g_lnb, g_bsx, g_ws, g_cw, g_cb, g_wa, g_ba, g_wx, g_bx, g_lam,
             vn_s, mixed_s, dm_s, dvn_s, xcbf_s, r_s, i_s, a_s, b_s, dh_s, dpr_s, dpi_s, dxc_s,
             ca_s, cd_s, cx_s):
        step_i = pl.program_id(0)
        tile = nt - 1 - step_i
        accs = (g_oga, g_ogb, g_lng, g_lnb, g_bsx, g_ws, g_cw, g_cb, g_wa, g_ba, g_wx, g_bx, g_lam)

        @pl.when(step_i == 0)
        def _():
            for r in accs + (ca_s, cd_s, cx_s):
                r[...] = jnp.zeros_like(r)

        dy_a = dy_ref[:, 0:D_HALF]
        dy_b = dy_ref[:, D_HALF:]

        u = _z_group(z_ref, 0).astype(F32)
        ug, tu = _gelu(u)
        v = _z_group(z_ref, 1).astype(F32)
        vg, tv = _gelu(v)
        vhat, rstd = _layernorm_parts(vg)
        vn_s[...] = (vhat * lng_ref[...] + lnb_ref[...]).astype(BF16)
        _spatial_mix(wt_ref, vn_s, bsx_ref, mixed_s, tm)
        mixed = mixed_s[...]
        ga = _z_group(z_ref, 2).astype(F32)
        sga = jax.nn.sigmoid(ga)
        sa = ga * sga
        um = ug * mixed
        ya = um * sa
        ra = lax.rsqrt(_lanemean(ya * ya) + EPS)
        yahat = ya * ra
        g_oga[...] += _rowsum8(dy_a * yahat)
        dn = dy_a * oga_ref[...]
        dya = ra * (dn - yahat * _lanemean(dn * yahat))
        dz_ref[:, 2 * D_HALF:3 * D_HALF] = (dya * um * (sga * (1.0 + ga * (1.0 - sga)))).astype(BF16)
        dz_ref[:, 0:D_HALF] = (dya * mixed * sa * _gelu_grad(u, tu)).astype(BF16)
        dmixed = dya * ug * sa
        g_bsx[...] += jnp.sum(dmixed.reshape(tm // CHUNK, CHUNK, D_HALF), axis=0)
        dm_s[...] = dmixed.astype(BF16)
        for c in range(tm // CHUNK):
            rows = slice(c * CHUNK, (c + 1) * CHUNK)
            for hd in range(N_HEADS):
                cols = slice(hd * CHUNK, (hd + 1) * CHUNK)
                dmh = dm_s[rows, cols]
                dvn_s[rows, cols] = _dot(wtt_ref[hd], dmh)
                g_ws[hd] += _dot_nt(dmh, vn_s[rows, cols])
        dvn = dvn_s[...]
        g_lng[...] += _rowsum8(dvn * vhat)
        g_lnb[...] += _rowsum8(dvn)
        dvh = dvn * lng_ref[...]
        dvg = rstd * (dvh - _lanemean(dvh) - vhat * _lanemean(dvh * vhat))
        dz_ref[:, D_HALF:2 * D_HALF] = (dvg * _gelu_grad(v, tv)).astype(BF16)

        xb = _z_group(z_ref, 3).astype(F32)
        halo = jnp.where(tile == 0, 0.0, _z_group(zh_ref, 3).astype(F32)[SUBLANES:])
        taps = _conv_taps(xb, halo)
        xc = cb_ref[...] + taps[0] * cw_ref[0:1, :]
        for k in range(1, CONV_W):
            xc = xc + taps[k] * cw_ref[k:k + 1, :]
        xcbf_s[...] = xc.astype(BF16)
        _lru_gates(xcbf_s, wa_ref, wx_ref, ba_ref, bx_ref, r_s, i_s)
        rg = r_s[...]
        ig = i_s[...]
        lam = lam_ref[...]
        a, mult_true = _decay_parts(rg, lam)
        row = lax.broadcasted_iota(jnp.int32, a.shape, 0)
        first = jnp.logical_and(tile == 0, row == 0)
        mult = jnp.where(first, 1.0, mult_true)
        hcur = h_ref[...]
        hprev = _shift_down(hcur, jnp.where(tile == 0, 0.0, hh_ref[...]), 1)
        gb = _z_group(z_ref, 4).astype(F32)
        sgb = jax.nn.sigmoid(gb)
        sb = gb * sgb
        yb = hcur * sb
        rb = lax.rsqrt(_lanemean(yb * yb) + EPS)
        ybhat = yb * rb
        g_ogb[...] += _rowsum8(dy_b * ybhat)
        dn = dy_b * ogb_ref[...]
        dyb = rb * (dn - ybhat * _lanemean(dn * ybhat))
        dz_ref[:, 4 * D_HALF:5 * D_HALF] = (dyb * hcur * (sgb * (1.0 + gb * (1.0 - sgb)))).astype(BF16)

        an = _shift_up(a, ca_s[...], 1)
        bb = dyb * sb
        r8 = row & (SUBLANES - 1)
        for d in (1, 2, 4):
            a_sh = pltpu.roll(an, tm - d, 0)
            b_sh = pltpu.roll(bb, tm - d, 0)
            m = r8 + d < SUBLANES
            bb = jnp.where(m, an * b_sh + bb, bb)
            an = jnp.where(m, an * a_sh, an)
        a_s[...] = an
        b_s[...] = bb

        def step(g, carry):
            sl = pl.ds(pl.multiple_of((hb - 1 - g) * SUBLANES, SUBLANES), SUBLANES)
            dg = a_s[sl, :] * carry + b_s[sl, :]
            dh_s[sl, :] = dg
            return jnp.broadcast_to(dg[0:1, :], dg.shape)

        cd_s[...] = lax.fori_loop(0, hb, step, cd_s[...])
        ca_s[...] = jnp.broadcast_to(a[0:1, :], ca_s.shape)
        dh = dh_s[...]
        da = dh * hprev
        gx = ig * xc
        dla = da * a - jnp.where(first, 0.0, dh * gx * (a * a / mult_true))
        g_lam[...] += _rowsum8(dla * rg)
        dr = dla * (-LRU_C * _softplus_neg(lam))
        dpr = dr * rg * (1.0 - rg)
        dpi = (dh * mult * xc) * ig * (1.0 - ig)
        g_ba[...] += _rowsum8(dpr)
        g_bx[...] += _rowsum8(dpi)
        dpr_s[...] = dpr.astype(BF16)
        dpi_s[...] = dpi.astype(BF16)
        for hd in range(N_HEADS):
            cols = slice(hd * CHUNK, (hd + 1) * CHUNK)
            xh = xcbf_s[:, cols]
            dprh = dpr_s[:, cols]
            dpih = dpi_s[:, cols]
            g_wa[hd] += _dot_tn(xh, dprh)
            g_wx[hd] += _dot_tn(xh, dpih)
            dxc_s[:, cols] = _dot_nt(dprh, wa_ref[hd]) + _dot_nt(dpih, wx_ref[hd])
        dxc = dxc_s[...] + dh * mult * ig
        g_cb[...] += _rowsum8(dxc)
        for k in range(CONV_W):
            g_cw[k * SUBLANES:(k + 1) * SUBLANES, :] += _rowsum8(dxc * taps[k])
        nxt = cx_s[...]
        dxb = dxc * cw_ref[CONV_W - 1:CONV_W, :]
        for j in range(1, CONV_W):
            dxb = dxb + _shift_up(dxc, nxt, j) * cw_ref[CONV_W - 1 - j:CONV_W - j, :]
        dz_ref[:, 3 * D_HALF:4 * D_HALF] = dxb.astype(BF16)
        cx_s[...] = dxc[0:SUBLANES]

        @pl.when(step_i == nt - 1)
        def _():
            for r in (g_oga, g_ogb, g_lng, g_lnb, g_cb, g_ba, g_bx):
                r[...] = jnp.broadcast_to(jnp.sum(r[...], axis=0, keepdims=True), r.shape)
            lam_f = LRU_C * jax.nn.sigmoid(-lam_ref[...])
            g_lam[...] = jnp.broadcast_to(jnp.sum(g_lam[...], axis=0, keepdims=True) * lam_f, g_lam.shape)
            for k in range(CONV_W):
                blk = g_cw[k * SUBLANES:(k + 1) * SUBLANES, :]
                g_cw[k * SUBLANES:(k + 1) * SUBLANES, :] = jnp.broadcast_to(jnp.sum(blk, axis=0, keepdims=True), blk.shape)
            tri = (lax.broadcasted_iota(jnp.int32, (CHUNK, CHUNK), 0) >= lax.broadcasted_iota(jnp.int32, (CHUNK, CHUNK), 1))
            for hd in range(N_HEADS):
                cols = slice(hd * CHUNK, (hd + 1) * CHUNK)
                g_ws[hd] = jnp.where(tri, g_ws[hd], 0.0)
                blk = g_bsx[:, cols]
                g_bsx[:, cols] = jnp.broadcast_to(jnp.sum(blk, axis=1, keepdims=True), blk.shape)

    rev = lambda i: nt - 1 - i
    zspec = pl.BlockSpec((N_CHIPS, tm, W_IN_COLS), lambda i: (0, rev(i), 0))
    halo = lambda col: pl.BlockSpec((SUBLANES, D_HALF), lambda i: (jnp.maximum(rev(i) * hb - 1, 0), col))
    zhalo = pl.BlockSpec((N_CHIPS, 2 * SUBLANES, W_IN_COLS), lambda i: (0, jnp.maximum(rev(i) * (hb // 2) - 1, 0), 0))
    full = lambda a: pl.BlockSpec(a.shape, lambda i, n=a.ndim: (0,) * n)
    acc = lambda shp: pl.BlockSpec(shp, lambda i, n=len(shp): (0,) * n)
    names = ("ln_g", "ln_b", "wt", "wtt", "bsx", "conv_w", "conv_b", "w_a", "w_x", "b_a", "b_x", "lam", "oga", "ogb")
    pr = [prm[n] for n in names] + [token]
    vec = (SUBLANES, D_HALF)
    mat = (N_HEADS, CHUNK, CHUNK)
    acc_shapes = [vec, vec, vec, vec, (CHUNK, D_HALF), mat, (CONV_W * SUBLANES, D_HALF), vec, mat, vec, mat, vec, vec]
    big = lambda dt: pltpu.VMEM((tm, D_HALF), dt)
    return pl.pallas_call(
        body, name="branches_bwd", grid=(nt,),
        in_specs=[zspec, zhalo,
                  pl.BlockSpec((tm, D_HALF), lambda i: (rev(i), 0)), halo(0),
                  pl.BlockSpec((tm, D_MODEL), lambda i: (rev(i), 0))] + [full(a) for a in pr],
        out_specs=[pl.BlockSpec((tm, D_Z), lambda i: (rev(i), 0))] + [acc(s) for s in acc_shapes],
        out_shape=[jax.ShapeDtypeStruct((t, D_Z), BF16)] + [jax.ShapeDtypeStruct(s, F32) for s in acc_shapes],
        scratch_shapes=[big(BF16), big(F32), big(BF16), big(F32), big(BF16), big(F32), big(F32), big(F32), big(F32),
                        big(F32), big(BF16), big(BF16), big(F32),
                        pltpu.VMEM(vec, F32), pltpu.VMEM(vec, F32), pltpu.VMEM(vec, F32)],
        compiler_params=_params("arbitrary"),
    )(z, z, h, h, dy, *pr)


def _inproj_bwd(dz, wg_in, x, dh1, pre_g, tm, tile0, nt, prev, last, token, name):
    t = x.shape[0]

    def body(*refs):
        dz_ref, w_ref, x_ref, dh1_ref, g_ref = refs[:5]
        gx_ref, gpre_ref, acc_s = refs[-3:]
        i = pl.program_id(0)

        @pl.when(i == 0)
        def _():
            gpre_ref[...] = jnp.zeros_like(gpre_ref) if prev is None else refs[7][...]

        acc = _dot_nt(dz_ref[:, 0:W_IN_COLS], w_ref[0])
        for k in range(1, N_CHIPS):
            acc = acc + _dot_nt(dz_ref[:, k * W_IN_COLS:(k + 1) * W_IN_COLS], w_ref[k])
        acc_s[...] = acc
        for s in range(tm // CHUNK):
            rows = slice(s * CHUNK, (s + 1) * CHUNK)
            xv = x_ref[rows, :]
            r = lax.rsqrt(_lanemean(xv * xv) + EPS)
            xhat = xv * r
            dhn = acc_s[rows, :]
            gpre_ref[...] += _rowsum8(dhn * xhat)
            dxh = dhn * g_ref[...]
            gx_ref[rows, :] = dh1_ref[rows, :] + r * (dxh - xhat * _lanemean(dxh * xhat))

        if last:
            @pl.when(i == nt - 1)
            def _():
                gpre_ref[...] = jnp.broadcast_to(jnp.sum(gpre_ref[...], axis=0, keepdims=True), gpre_ref.shape)

    row = lambda n: pl.BlockSpec((tm, n), lambda i: (tile0 + i, 0))
    small = lambda r: pl.BlockSpec((r, D_MODEL), lambda i: (0, 0))
    tok = pl.BlockSpec((SUBLANES, LANES), lambda i: (0, 0))
    in_specs = [row(D_Z), pl.BlockSpec(wg_in.shape, lambda i: (0, 0, 0), pipeline_mode=pl.Buffered(1)),
                row(D_MODEL), row(D_MODEL), small(1), tok]
    args = [dz, wg_in, x, dh1, pre_g, token]
    aliases = {}
    if prev is not None:
        in_specs += [ANY, small(SUBLANES)]
        args += list(prev)
        aliases = {6: 0}
    return pl.pallas_call(
        body, name=name, grid=(nt,), in_specs=in_specs, out_specs=[row(D_MODEL), small(SUBLANES)],
        out_shape=[jax.ShapeDtypeStruct((t, D_MODEL), F32), jax.ShapeDtypeStruct((SUBLANES, D_MODEL), F32)],
        input_output_aliases=aliases,
        scratch_shapes=[pltpu.VMEM((tm, D_MODEL), F32)],
        compiler_params=_params("arbitrary"),
    )(*args)


def _weight_grad(a, b, name, kb, nb, tk, tn, tt, token, a_transposed=False):
    t = b.shape[0]
    tt = min(tt, t)

    def body(a_ref, b_ref, token_ref, o_ref):
        @pl.when(pl.program_id(2) == 0)
        def _():
            o_ref[...] = jnp.zeros_like(o_ref)

        o_ref[...] += (_dot if a_transposed else _dot_tn)(a_ref[...], b_ref[...])

    a_spec = (pl.BlockSpec((tk, tt), lambda j, i, s: (i, s)) if a_transposed
              else pl.BlockSpec((tt, tk), lambda j, i, s: (s, i)))
    return pl.pallas_call(
        body, name=name, grid=(nb, kb, t // tt),
        in_specs=[a_spec, pl.BlockSpec((tt, tn), lambda j, i, s: (s, j)),
                  pl.BlockSpec((SUBLANES, LANES), lambda j, i, s: (0, 0))],
        out_specs=pl.BlockSpec((None, None, tk, tn), lambda j, i, s: (j, i, 0, 0)),
        out_shape=jax.ShapeDtypeStruct((nb, kb, tk, tn), F32),
        compiler_params=_params("parallel", "parallel", "arbitrary"),
    )(a, b, token)


def _place():
    x, y, c = lax.axis_index("x"), lax.axis_index("y"), lax.axis_index("c")
    return x, y, c


def _chip_of(x, y):
    return 2 * x + y


HBM = pl.BlockSpec(memory_space=pltpu.HBM)
SEM = pl.BlockSpec(memory_space=pltpu.SEMAPHORE)
EFFECT = pltpu.SideEffectType.DATAFLOW_SIDE_EFFECTING


def _hbm(a):
    return pltpu.with_memory_space_constraint(a, pltpu.HBM)


def _landing(shape, dtype):
    return _hbm(lax.empty(shape, dtype))


def _exchange_start(name, arrays, ncopies, build, after=None):
    n = len(arrays)
    extra = [] if after is None else [after]

    def body(*refs):
        ins, token = refs[:n], refs[-1]
        send_sems, recv_sems = refs[n + len(extra)], refs[n + len(extra) + 1]
        for cp in build(ins, send_sems, recv_sems):
            cp.start()
        token[...] = jnp.zeros_like(token)

    outs = pl.pallas_call(
        body, name=name,
        out_shape=(pltpu.SemaphoreType.DMA((ncopies,)), pltpu.SemaphoreType.DMA((ncopies,)),
                   *[pltpu.HBM(a.shape, a.dtype) for a in arrays], jax.ShapeDtypeStruct((SUBLANES, LANES), F32)),
        in_specs=[HBM] * n + [ANY] * len(extra),
        out_specs=(SEM, SEM, *[HBM] * n, pl.BlockSpec(memory_space=pltpu.VMEM)),
        input_output_aliases={q: q + 2 for q in range(n)},
        compiler_params=pltpu.CompilerParams(has_side_effects=EFFECT),
    )(*[_hbm(a) for a in arrays], *extra)
    return (outs[0], outs[1], list(outs[2:2 + n])), outs[-1]


def _exchange_wait(name, started, after, build):
    send, recv, arrays = started
    n = len(arrays)

    def body(*refs):
        ins, send_sems, recv_sems = refs[:n], refs[n], refs[n + 1]
        for cp in build(ins, send_sems, recv_sems):
            cp.wait_send()
            cp.wait_recv()

    return pl.pallas_call(
        body, name=name, out_shape=tuple(pltpu.HBM(a.shape, a.dtype) for a in arrays),
        in_specs=[HBM] * n + [SEM, SEM, ANY], out_specs=tuple([HBM] * n),
        input_output_aliases={q: q for q in range(n)},
        compiler_params=pltpu.CompilerParams(has_side_effects=EFFECT),
    )(*arrays, send, recv, after)


def _exchange_wait_start(name, started, after, build_wait, ncopies, build_start):
    send, recv, arrays = started
    n = len(arrays)

    def body(*refs):
        ins, send_sems, recv_sems = refs[:n], refs[n], refs[n + 1]
        send2, recv2, token = refs[n + 3], refs[n + 4], refs[-1]
        arrived = build_wait(ins, send_sems, recv_sems)
        for cp, onward in zip(arrived, build_start(ins, send2, recv2)):
            cp.wait_recv()
            onward.start()
        for cp in arrived:
            cp.wait_send()
        token[...] = jnp.zeros_like(token)

    outs = pl.pallas_call(
        body, name=name,
        out_shape=(pltpu.SemaphoreType.DMA((ncopies,)), pltpu.SemaphoreType.DMA((ncopies,)),
                   *[pltpu.HBM(a.shape, a.dtype) for a in arrays], jax.ShapeDtypeStruct((SUBLANES, LANES), F32)),
        in_specs=[HBM] * n + [SEM, SEM, ANY], out_specs=(SEM, SEM, *[HBM] * n, pl.BlockSpec(memory_space=pltpu.VMEM)),
        input_output_aliases={q: q + 2 for q in range(n)},
        compiler_params=pltpu.CompilerParams(has_side_effects=EFFECT),
    )(*arrays, send, recv, after)
    return (outs[0], outs[1], list(outs[2:2 + n])), outs[-1]


def _cast_into_slot(w, kc, name, dtype=BF16, token=None):
    rows, cols = w.shape
    tr = min(rows, 4 * SUM_TILE)
    extra = [] if token is None else [token]

    def body(kc_ref, w_ref, *rest):
        rest[-1][...] = w_ref[...].astype(dtype)

    grid_spec = pltpu.PrefetchScalarGridSpec(
        num_scalar_prefetch=1, grid=(rows // tr,),
        in_specs=[pl.BlockSpec((tr, cols), lambda r, kc: (r, 0))]
                 + [pl.BlockSpec((SUBLANES, LANES), lambda r, kc: (0, 0))] * len(extra),
        out_specs=pl.BlockSpec((None, tr, cols), lambda r, kc: (kc[0], r, 0)))
    return pl.pallas_call(
        body, name=name, grid_spec=grid_spec, out_shape=jax.ShapeDtypeStruct((N_CHIPS, rows, cols), dtype),
        compiler_params=_params("arbitrary"),
    )(kc, w, *extra)


def _gather_ici_copies(n):
    def build(refs, send_sems, recv_sems):
        x, y, c = _place()
        mine = lambda b: refs[b].at[_chip_of(x, y), c]
        chips = [(1 - x, y), (x, 1 - y), (1 - x, 1 - y)]
        return [pltpu.make_async_remote_copy(
            src_ref=mine(b), dst_ref=mine(b), send_sem=send_sems.at[3 * b + j], recv_sem=recv_sems.at[3 * b + j],
            device_id=(*chip, c), device_id_type=MESH) for b in range(n) for j, chip in enumerate(chips)]
    return build


def _gather_direct_copies(n):
    def build(refs, send_sems, recv_sems):
        x, y, c = _place()
        mine = lambda b: refs[b].at[_chip_of(x, y)]
        chips = [(1 - x, y), (x, 1 - y), (1 - x, 1 - y)]
        return [pltpu.make_async_remote_copy(
            src_ref=mine(b), dst_ref=mine(b), send_sem=send_sems.at[3 * b + j], recv_sem=recv_sems.at[3 * b + j],
            device_id=(*chip, c), device_id_type=MESH) for b in range(n) for j, chip in enumerate(chips)]
    return build


def _gather_relay_copies(n):
    def build(refs, send_sems, recv_sems):
        x, y, c = _place()
        chips = [(1 - x, y), (x, 1 - y), (1 - x, 1 - y)]
        cps = []
        for b in range(n):
            for j, chip in enumerate(chips):
                got = refs[b].at[_chip_of(*chip), c]
                cps.append(pltpu.make_async_remote_copy(
                    src_ref=got, dst_ref=got, send_sem=send_sems.at[3 * b + j], recv_sem=recv_sems.at[3 * b + j],
                    device_id=(x, y, 1 - c), device_id_type=MESH))
        return cps
    return build


def _sibling_copies(n):
    def build(refs, send_sems, recv_sems):
        x, y, c = _place()
        return [pltpu.make_async_remote_copy(
            src_ref=refs[b].at[:, 1 - c], dst_ref=refs[n + b], send_sem=send_sems.at[b], recv_sem=recv_sems.at[b],
            device_id=(x, y, 1 - c), device_id_type=MESH) for b in range(n)]
    return build


def _chip_copies(n):
    def build(refs, send_sems, recv_sems):
        x, y, c = _place()
        chips = [(1 - x, y), (x, 1 - y), (1 - x, 1 - y)]
        return [pltpu.make_async_remote_copy(
            src_ref=refs[b].at[_chip_of(*chip)], dst_ref=refs[n + b].at[j],
            send_sem=send_sems.at[3 * b + j], recv_sem=recv_sems.at[3 * b + j],
            device_id=(*chip, c), device_id_type=MESH) for b in range(n) for j, chip in enumerate(chips)]
    return build


def _finish_copies(n, n_all):
    def build(refs, send_sems, recv_sems):
        x, y, c = _place()
        cps = [pltpu.make_async_remote_copy(
            src_ref=refs[b].at[c], dst_ref=refs[b].at[c], send_sem=send_sems.at[b], recv_sem=recv_sems.at[b],
            device_id=(x, y, 1 - c), device_id_type=MESH) for b in range(n)]
        flips = [(fx, fy, fc) for fx in (0, 1) for fy in (0, 1) for fc in (0, 1)][1:]
        for b in range(n_all):
            mine = refs[n + b].at[_chip_of(x, y), c]
            cps += [pltpu.make_async_remote_copy(
                src_ref=mine, dst_ref=mine, send_sem=send_sems.at[n + 7 * b + q], recv_sem=recv_sems.at[n + 7 * b + q],
                device_id=(x ^ fx, y ^ fy, c ^ fc), device_id_type=MESH) for q, (fx, fy, fc) in enumerate(flips)]
        return cps
    return build


def _pair_sum(g, r1, kc, name, tr, send_dtype):
    nk, _, rows, cols = g.shape
    tr = min(tr, rows)

    def body(kc_ref, g_ref, r_ref, p_ref, own_ref):
        s = g_ref[...] + r_ref[...]
        p_ref[...] = s.astype(send_dtype)

        @pl.when(pl.program_id(1) == kc_ref[0])
        def _():
            own_ref[...] = s

    grid_spec = pltpu.PrefetchScalarGridSpec(
        num_scalar_prefetch=1, grid=(rows // tr, nk),
        in_specs=[pl.BlockSpec((None, None, tr, cols), lambda r, k, kc: (k, kc[1], r, 0)),
                  pl.BlockSpec((None, tr, cols), lambda r, k, kc: (k, r, 0))],
        out_specs=[pl.BlockSpec((None, tr, cols), lambda r, k, kc: (k, r, 0)),
                   pl.BlockSpec((tr, cols), lambda r, k, kc: (r, 0))])
    return pl.pallas_call(
        body, name=name, grid_spec=grid_spec,
        out_shape=[jax.ShapeDtypeStruct((nk, rows, cols), send_dtype), jax.ShapeDtypeStruct((rows, cols), F32)],
        compiler_params=_params("arbitrary", "arbitrary"),
    )(kc, g, r1)


def _chip_sum(own, r2, slot, lead, name, tr):
    rows, cols = own.shape
    tr = min(tr, rows)
    nl = len(lead)

    def body(slot_ref, o_ref, r_ref, s_ref):
        s = o_ref[...]
        for j in range(3):
            s = s + r_ref[j].astype(F32)
        s_ref[...] = s

    grid_spec = pltpu.PrefetchScalarGridSpec(
        num_scalar_prefetch=1, grid=(rows // tr,),
        in_specs=[pl.BlockSpec((tr, cols), lambda r, sl: (r, 0)), pl.BlockSpec((3, tr, cols), lambda r, sl: (0, r, 0))],
        out_specs=pl.BlockSpec((None,) * nl + (tr, cols), lambda r, sl: tuple(sl[q] for q in range(nl)) + (r, 0)))
    return pl.pallas_call(
        body, name=name, grid_spec=grid_spec, out_shape=jax.ShapeDtypeStruct(tuple(lead) + (rows, cols), F32),
        compiler_params=_params("arbitrary"),
    )(slot, own, r2)


def _adam_update(w, g, m, v):
    nm = ADAM_B1 * m + (1.0 - ADAM_B1) * g
    nv = ADAM_B2 * v + (1.0 - ADAM_B2) * (g * g)
    m_hat = nm / (1.0 - ADAM_B1 ** ADAM_STEP)
    v_hat = nv / (1.0 - ADAM_B2 ** ADAM_STEP)
    return -ADAM_LR * (m_hat / (jnp.sqrt(v_hat) + ADAM_EPS) + ADAM_WD * w), nm, nv


def _adamw(w, g, m, v, name, tr, token):
    rows, cols = w.shape
    tr = min(tr, rows)

    def body(w_ref, g_ref, m_ref, v_ref, token_ref, go_ref, d_ref, nm_ref, nv_ref):
        gv = g_ref[...]
        go_ref[...] = gv
        d_ref[...], nm_ref[...], nv_ref[...] = _adam_update(w_ref[...], gv, m_ref[...], v_ref[...])

    spec = pl.BlockSpec((tr, cols), lambda r: (r, 0))
    return pl.pallas_call(
        body, name=name, grid=(rows // tr,),
        in_specs=[spec] * 4 + [pl.BlockSpec((SUBLANES, LANES), lambda r: (0, 0))], out_specs=[spec] * 4,
        out_shape=[jax.ShapeDtypeStruct((rows, cols), F32)] * 4,
        compiler_params=_params("parallel"),
    )(w, g, m, v, token)


def _adamw_small(packed_g, pre_g_parts, ws, ms, vs):
    names = ["pre_g"] + [n for n, _ in SMALL_ROWS if n != "conv_w"]
    rows = dict(SMALL_ROWS)
    offset, at = {}, 0
    for n, r in SMALL_ROWS:
        offset[n] = at
        at += r
    k = len(names)

    def body(*refs):
        g_ref, pg_ref = refs[0], refs[1]
        w_refs, m_refs, v_refs = refs[2:2 + k], refs[2 + k:2 + 2 * k], refs[2 + 2 * k:2 + 3 * k]
        outs = refs[2 + 3 * k:]
        go, do, mo, vo = outs[:k], outs[k:2 * k], outs[2 * k:3 * k], outs[3 * k:4 * k]
        pre = pg_ref[0]
        for dev in range(1, 8):
            pre = pre + pg_ref[dev]
        outs[4 * k][...] = pre[D_MODEL // LANES:, :]
        for i, n in enumerate(names):
            shp = w_refs[i].shape
            if len(shp) == 2 and shp[0] == 1:
                for r in range(shp[1] // LANES):
                    cols = slice(r * LANES, (r + 1) * LANES)
                    g = pre[r:r + 1, :] if n == "pre_g" else g_ref[offset[n] + r:offset[n] + r + 1, :]
                    go[i][:, cols] = g
                    do[i][:, cols], mo[i][:, cols], vo[i][:, cols] = _adam_update(
                        w_refs[i][:, cols], g, m_refs[i][:, cols], v_refs[i][:, cols])
            else:
                g = g_ref[offset[n]:offset[n] + rows[n], :].reshape(shp)
                go[i][...] = g
                do[i][...], mo[i][...], vo[i][...] = _adam_update(w_refs[i][...], g, m_refs[i][...], v_refs[i][...])

    vm = pl.BlockSpec(memory_space=pltpu.VMEM)
    args = [packed_g, pre_g_parts] + [src[n] for src in (ws, ms, vs) for n in names]
    out_shape = [jax.ShapeDtypeStruct(ws[n].shape, F32) for _ in range(4) for n in names]
    out_shape.append(jax.ShapeDtypeStruct((SUBLANES, LANES), F32))
    outs = pl.pallas_call(
        body, name="adamw_small", in_specs=[vm] * len(args), out_specs=[vm] * (4 * k + 1), out_shape=out_shape,
    )(*args)
    return [dict(zip(names, outs[q * k:(q + 1) * k])) for q in range(4)], outs[4 * k]


def _into_slot(v, tail, slot, lead, name):
    n = v.shape[1]
    nl = len(lead)
    rows = n // LANES + SUBLANES

    def body(slot_ref, v_ref, t_ref, o_ref):
        for r in range(n // LANES):
            o_ref[r:r + 1, :] = v_ref[0:1, r * LANES:(r + 1) * LANES]
        o_ref[n // LANES:, :] = t_ref[...]

    grid_spec = pltpu.PrefetchScalarGridSpec(
        num_scalar_prefetch=1, grid=(1,),
        in_specs=[pl.BlockSpec(v.shape, lambda i, sl: (0, 0)), pl.BlockSpec(tail.shape, lambda i, sl: (0, 0))],
        out_specs=pl.BlockSpec((None,) * nl + (rows, LANES), lambda i, sl: tuple(sl[q] for q in range(nl)) + (0, 0)))
    return pl.pallas_call(
        body, name=name, grid_spec=grid_spec, out_shape=jax.ShapeDtypeStruct(tuple(lead) + (rows, LANES), F32),
    )(slot, v, tail)


def _pack_small(parts):
    names = [n for n, _ in SMALL_ROWS]
    offset, at = {}, 0
    for n, r in SMALL_ROWS:
        offset[n] = at
        at += r

    def body(*refs):
        ins, o_ref = dict(zip(names, refs[:-1])), refs[-1]
        o_ref[SMALL_USED:, :] = jnp.zeros((SMALL_TOTAL - SMALL_USED, LANES), F32)
        for n, rows in SMALL_ROWS:
            ref, at = ins[n], offset[n]
            if n == "gmlp_bs":
                for h in range(N_HEADS):
                    o_ref[at + h:at + h + 1, :] = jnp.transpose(ref[:, h * CHUNK:(h + 1) * CHUNK])[0:1, :]
            elif n == "conv_w":
                for k in range(CONV_W):
                    for r in range(D_HALF // LANES):
                        row = at + k * (D_HALF // LANES) + r
                        o_ref[row:row + 1, :] = ref[k * SUBLANES:k * SUBLANES + 1, r * LANES:(r + 1) * LANES]
            elif ref.ndim == 3:
                o_ref[at:at + rows, :] = ref[...].reshape(rows, LANES)
            else:
                for r in range(rows):
                    o_ref[at + r:at + r + 1, :] = ref[0:1, r * LANES:(r + 1) * LANES]

    vm = pl.BlockSpec(memory_space=pltpu.VMEM)
    return pl.pallas_call(
        body, name="pack_small", in_specs=[vm] * len(names), out_specs=vm,
        out_shape=jax.ShapeDtypeStruct((SMALL_TOTAL, LANES), F32),
    )(*[parts[n] for n in names])


def kernel(x, p, pre_g, w_in, gmlp_ln_g, gmlp_ln_b, gmlp_ws, gmlp_bs, conv_w, conv_b, w_a, b_a, w_x, b_x, lam, gmlp_out_g, lru_out_g, w_out, post_g, w_pe, w_pg, loss_target, m_pre_g, m_w_in, m_gmlp_ln_g, m_gmlp_ln_b, m_gmlp_ws, m_gmlp_bs, m_conv_w, m_conv_b, m_w_a, m_b_a, m_w_x, m_b_x, m_lam, m_gmlp_out_g, m_lru_out_g, m_w_out, m_post_g, m_w_pe, m_w_pg, v_pre_g, v_w_in, v_gmlp_ln_g, v_gmlp_ln_b, v_gmlp_ws, v_gmlp_bs, v_conv_w, v_conv_b, v_w_a, v_b_a, v_w_x, v_b_x, v_lam, v_gmlp_out_g, v_lru_out_g, v_w_out, v_post_g, v_w_pe, v_w_pg):
    weights = dict(pre_g=pre_g, w_in=w_in, gmlp_ln_g=gmlp_ln_g, gmlp_ln_b=gmlp_ln_b, gmlp_ws=gmlp_ws, gmlp_bs=gmlp_bs,
                   conv_w=conv_w, conv_b=conv_b, w_a=w_a, b_a=b_a, w_x=w_x, b_x=b_x, lam=lam, gmlp_out_g=gmlp_out_g,
                   lru_out_g=lru_out_g, w_out=w_out, post_g=post_g, w_pe=w_pe, w_pg=w_pg)
    mom_m = dict(pre_g=m_pre_g, w_in=m_w_in, gmlp_ln_g=m_gmlp_ln_g, gmlp_ln_b=m_gmlp_ln_b, gmlp_ws=m_gmlp_ws,
                 gmlp_bs=m_gmlp_bs, conv_w=m_conv_w, conv_b=m_conv_b, w_a=m_w_a, b_a=m_b_a, w_x=m_w_x, b_x=m_b_x,
                 lam=m_lam, gmlp_out_g=m_gmlp_out_g, lru_out_g=m_lru_out_g, w_out=m_w_out, post_g=m_post_g,
                 w_pe=m_w_pe, w_pg=m_w_pg)
    mom_v = dict(pre_g=v_pre_g, w_in=v_w_in, gmlp_ln_g=v_gmlp_ln_g, gmlp_ln_b=v_gmlp_ln_b, gmlp_ws=v_gmlp_ws,
                 gmlp_bs=v_gmlp_bs, conv_w=v_conv_w, conv_b=v_conv_b, w_a=v_w_a, b_a=v_b_a, w_x=v_w_x, b_x=v_b_x,
                 lam=v_lam, gmlp_out_g=v_gmlp_out_g, lru_out_g=v_lru_out_g, w_out=v_w_out, post_g=v_post_g,
                 w_pe=v_w_pe, w_pg=v_w_pg)
    order = list(weights)
    xi, yi, ci = _place()
    me = _chip_of(xi, yi)
    kc = jnp.stack([me, ci]).astype(jnp.int32)

    x2 = x[0]
    p2 = p[0, 0]
    tgt = loss_target[0]

    first = [_cast_into_slot(w_in[0], kc, "cast_w_in").reshape(N_CHIPS, 2, D_MODEL // 2, W_IN_COLS),
             _cast_into_slot(conv_w[0, :, 0, :], kc, "conv_w_into_slot", F32).reshape(N_CHIPS, 2, CONV_W // 2, CONV_COLS)]
    in_st, in_tok = _exchange_start("gather_in_start", first, 6, _gather_ici_copies(2))
    later = [_cast_into_slot(w_out[0], kc, "cast_w_out", token=in_tok).reshape(N_CHIPS, 2, W_ROWS // 2, D_MODEL),
             _cast_into_slot(w_pg[0], kc, "cast_w_pg", token=in_tok).reshape(N_CHIPS, 2, W_ROWS // 2, D_MODEL),
             _cast_into_slot(w_pe[0], kc, "cast_w_pe", token=in_tok).reshape(N_CHIPS, 2, D_PLE // 2, W_PE_COLS)]
    gather_st, gather_tok = _exchange_start("gather_start", later, 9, _gather_direct_copies(3), after=in_tok)
    hn, z_own, hn_t = _inproj_local(x2, pre_g, w_in[0], ROW_TILE, gather_tok)
    in_st, in_tok = _exchange_wait_start("gather_in_relay", in_st, z_own, _gather_ici_copies(2), 6,
                                         _gather_relay_copies(2))
    g_in, g_cw = _exchange_wait("gather_in_wait", in_st, in_tok, _gather_relay_copies(2))
    wg_in = g_in.reshape(N_CHIPS, D_MODEL, W_IN_COLS)
    cw_full = jnp.transpose(g_cw.reshape(N_CHIPS, CONV_W, CONV_COLS), (1, 0, 2)).reshape(CONV_W, D_HALF)

    causal = jnp.tril(jnp.ones((CHUNK, CHUNK), dtype=bool))
    ws_m = jnp.where(causal[None], gmlp_ws[0], 0.0)
    prm = dict(
        ln_g=gmlp_ln_g, ln_b=gmlp_ln_b, wt=ws_m.astype(BF16), wtt=jnp.transpose(ws_m, (0, 2, 1)).astype(BF16),
        bsx=jnp.repeat(jnp.transpose(gmlp_bs[0]), CHUNK, axis=1),
        conv_w=cw_full, conv_b=conv_b, w_a=w_a[0].astype(BF16), w_x=w_x[0].astype(BF16),
        b_a=b_a[0].reshape(1, D_HALF), b_x=b_x[0].reshape(1, D_HALF), lam=lam, oga=gmlp_out_g, ogb=lru_out_g)

    z, y, h = _inproj_branches_fwd(hn, z_own, wg_in, kc, prm, ROW_TILE, gather_tok)
    g_out, g_pg, g_pe = _exchange_wait("gather_wait", gather_st, y, _gather_direct_copies(3))
    wg_out = g_out.reshape(D_MODEL, D_MODEL)
    wg_pg = g_pg.reshape(D_MODEL, D_MODEL)
    wg_pe = g_pe.reshape(N_CHIPS, D_PLE, W_PE_COLS)
    h1, dq, dh1, do, dy, gw_pe, g_post, loss_acc = _head_fwd_bwd(x2, y, p2, tgt, post_g, wg_out, wg_pg, wg_pe,
                                                                 ROW_TILE)

    def sibling_start(tag, bufs):
        lands = [_landing((b.shape[0],) + b.shape[2:], b.dtype) for b in bufs]
        return _exchange_start("sibling_start_" + tag, bufs + lands, len(bufs), _sibling_copies(len(bufs)))

    def pair_then_chip_start(tag, started, after, names, tiles, dtypes):
        n = len(names)
        got = _exchange_wait("sibling_wait_" + tag, started, after, _sibling_copies(n))
        pairs = [_pair_sum(got[b], got[n + b], kc, "pair_sum_" + names[b], tiles[b], dtypes[b]) for b in range(n)]
        lands = [_landing((3,) + pr[0].shape[1:], pr[0].dtype) for pr in pairs]
        return _exchange_start("chip_start_" + tag, [pr[0] for pr in pairs] + lands, 3 * n, _chip_copies(n)), pairs

    def sum_then_finish_start(tag, started, pairs, after, names, tiles, small, to_all=()):
        n = len(names)
        got = _exchange_wait("chip_wait_" + tag, started, after, _chip_copies(n))
        sums = [_chip_sum(pairs[b][1], got[n + b], kc if small and b == n - 1 else kc[1:],
                          (N_CHIPS, 2) if small and b == n - 1 else (2,), "chip_sum_" + names[b], tiles[b])
                for b in range(n)]
        nbig = n - 1 if small else n
        n_all = n - nbig + len(to_all)
        return _exchange_start("finish_start_" + tag, sums + list(to_all), nbig + 7 * n_all,
                               _finish_copies(nbig, n_all))

    gw_pe = gw_pe.reshape(N_CHIPS, 2, D_PLE // 2, W_PE_COLS)
    token0 = jnp.zeros((SUBLANES, LANES), F32)
    gw_out = _weight_grad(y, do, "grad_w_out", 2, 1, D_MODEL // 2, D_MODEL, CONTRACT_TILE, token0)
    gw_pg = _weight_grad(h1, dq, "grad_w_pg", 2, 1, D_MODEL // 2, D_MODEL, CONTRACT_TILE, token0)
    gw_out = gw_out.reshape(N_CHIPS, 2, W_ROWS // 2, D_MODEL)
    gw_pg = gw_pg.reshape(N_CHIPS, 2, W_ROWS // 2, D_MODEL)

    names_a, tiles_a = ["w_out", "w_pg", "w_pe"], [SUM_TILE] * 3
    st, tok = sibling_start("a", [gw_out, gw_pg, gw_pe])
    (dz, g_oga, g_ogb, g_lng, g_lnb, g_bsx, g_ws, g_cw, g_cb, g_wa, g_ba, g_wx, g_bx, g_lam) = _branches_bwd(
        z, h, dy, prm, ROW_TILE, tok)
    (st, tok), pairs_a = pair_then_chip_start("a", st, dz, names_a, tiles_a, [BF16] * 3)
    gw_in = _weight_grad(hn_t, dz, "grad_w_in", 2, N_CHIPS, D_MODEL // 2, W_IN_COLS, CONTRACT_TILE, tok,
                         a_transposed=True)
    fin_a, tok = sum_then_finish_start("a", st, pairs_a, gw_in, names_a, tiles_a, False)

    small_g = dict(
        gmlp_ln_g=g_lng, gmlp_ln_b=g_lnb, gmlp_ws=g_ws, gmlp_bs=g_bsx, conv_w=g_cw, conv_b=g_cb, w_a=g_wa, b_a=g_ba,
        w_x=g_wx, b_x=g_bx, lam=g_lam, gmlp_out_g=g_oga, lru_out_g=g_ogb, post_g=g_post)
    gsm = _pack_small(small_g).reshape(N_CHIPS, 2, SMALL_PIECE, LANES)

    names_b, tiles_b = ["w_in", "small"], [2 * SUM_TILE, SMALL_PIECE]
    n_tiles = x2.shape[0] // ROW_TILE
    n_lo = max(1, (5 * n_tiles) // 16)
    st, tok_b = _exchange_start(
        "sibling_start_b", [gw_in, gsm] + [_landing((N_CHIPS,) + b.shape[2:], F32) for b in (gw_in, gsm)], 2,
        _sibling_copies(2), after=tok)
    part = _inproj_bwd(dz, wg_in, x2, dh1, pre_g, ROW_TILE, 0, n_lo, None, False, tok_b, "inproj_bwd_lo")
    f_out, f_pg, f_pe = _exchange_wait("finish_wait_a", fin_a, part[1], _finish_copies(3, 0))
    (st, tok_b), pairs_b = pair_then_chip_start("b", st, part[1], names_b, tiles_b, [BF16, F32])
    grad_x, g_pre = _inproj_bwd(dz, wg_in, x2, dh1, pre_g, ROW_TILE, n_lo, n_tiles - n_lo, part, True, tok_b,
                                "inproj_bwd_hi")
    pre_parts = _into_slot(g_pre, loss_acc, kc, (N_CHIPS, 2), "pre_g_into_slot")
    fin_b, tok_b = sum_then_finish_start("b", st, pairs_b, g_pre, names_b, tiles_b, True, to_all=[pre_parts])

    grads, deltas, new_m, new_v = {}, {}, {}, {}

    def adam_big(n, g2d, tr, token):
        shp = weights[n].shape
        g, d, nm, nv = _adamw(weights[n][0], g2d, mom_m[n][0], mom_v[n][0], "adamw_" + n, tr, token)
        grads[n], deltas[n], new_m[n], new_v[n] = g.reshape(shp), d.reshape(shp), nm.reshape(shp), nv.reshape(shp)
        return d

    as_token = lambda d: d[:SUBLANES, :LANES]
    last = adam_big("w_out", f_out.reshape(W_ROWS, D_MODEL), SUM_TILE, tok_b)
    last = adam_big("w_pg", f_pg.reshape(W_ROWS, D_MODEL), SUM_TILE, as_token(last))
    last = adam_big("w_pe", f_pe.reshape(D_PLE, W_PE_COLS), SUM_TILE, as_token(last))
    f_in, f_sm, pre_parts = _exchange_wait("finish_wait_b", fin_b, last, _finish_copies(1, 2))
    adam_big("w_in", f_in.reshape(D_MODEL, W_IN_COLS), 2 * SUM_TILE, tok_b)

    packed_g = f_sm.reshape(SMALL_TOTAL, LANES)
    small_names = ["pre_g"] + [n for n, _ in SMALL_ROWS if n != "conv_w"]
    natural = lambda src: {n: (src[n] if src[n].ndim == 2 else src[n][0]) for n in small_names}
    outs, loss_block = _adamw_small(packed_g, pre_parts.reshape(8, D_MODEL // LANES + SUBLANES, LANES),
                                    natural(weights), natural(mom_m), natural(mom_v))
    loss = loss_block[0, 0]
    for dst, got in zip((grads, deltas, new_m, new_v), outs):
        for n in small_names:
            dst[n] = got[n].reshape(weights[n].shape)
    at = sum(r for n, r in SMALL_ROWS[:[n for n, _ in SMALL_ROWS].index("conv_w")])
    g_cw_all = packed_g[at:at + CONV_W * D_HALF // LANES].reshape(CONV_W, D_HALF)
    g_conv = lax.dynamic_slice_in_dim(g_cw_all, me * CONV_COLS, CONV_COLS, axis=1)
    g, d, nm, nv = _adamw(conv_w[0, :, 0, :], g_conv, m_conv_w[0, :, 0, :], v_conv_w[0, :, 0, :], "adamw_conv_w", CONV_W,
                          tok_b)
    cshape = conv_w.shape
    grads["conv_w"], deltas["conv_w"] = g.reshape(cshape), d.reshape(cshape)
    new_m["conv_w"], new_v["conv_w"] = nm.reshape(cshape), nv.reshape(cshape)

    return (loss, grad_x.reshape(x.shape), *[grads[n] for n in order], *[deltas[n] for n in order],
            *[new_m[n] for n in order], *[new_v[n] for n in order])
```

```python
import math

import jax
import jax.numpy as jnp
from jax import lax
from jax.experimental import pallas as pl
from jax.experimental.pallas import tpu as pltpu

F32 = jnp.float32
BF16 = jnp.bfloat16

D_MODEL = 2048
D_HALF = 1024
D_Z = 5120
D_PLE = 256
CHUNK = 128
N_HEADS = 8
N_CHIPS = 4
W_IN_COLS = D_Z // N_CHIPS
W_ROWS = D_MODEL // N_CHIPS
W_PE_COLS = D_MODEL // N_CHIPS
CONV_W = 4
CONV_COLS = D_HALF // N_CHIPS
EPS = 1e-6
LRU_C = 8.0
ADAM_LR, ADAM_B1, ADAM_B2, ADAM_EPS, ADAM_WD, ADAM_STEP = 0.001, 0.9, 0.999, 1e-08, 0.01, 10

SUBLANES = 8
LANES = 128
VMEM_LIMIT = 56 * 1024 * 1024
ROW_TILE = 256
CONTRACT_TILE = 2048
SUM_TILE = 256

SMALL_ROWS = (("gmlp_ln_g", 8), ("gmlp_ln_b", 8), ("gmlp_ws", 1024), ("gmlp_bs", 8),
              ("conv_w", 32), ("conv_b", 8), ("w_a", 1024), ("b_a", 8), ("w_x", 1024), ("b_x", 8),
              ("lam", 8), ("gmlp_out_g", 8), ("lru_out_g", 8), ("post_g", 16))
SMALL_USED = sum(r for _, r in SMALL_ROWS)
SMALL_PIECE = 400
SMALL_TOTAL = 8 * SMALL_PIECE

MESH = pl.DeviceIdType.MESH
ANY = pl.BlockSpec(memory_space=pl.ANY)

_GELU_C0 = math.sqrt(2.0 / math.pi)
_GELU_C1 = 0.044715


def _params(*sem):
    return pltpu.CompilerParams(dimension_semantics=sem, vmem_limit_bytes=VMEM_LIMIT)


def _dot(a, b):
    return jnp.dot(a, b, preferred_element_type=F32)


def _dot_nt(a, b):
    return lax.dot_general(a, b, (((1,), (1,)), ((), ())), preferred_element_type=F32)


def _dot_tn(a, b):
    return lax.dot_general(a, b, (((0,), (0,)), ((), ())), preferred_element_type=F32)


def _gelu(x):
    t = jnp.tanh(_GELU_C0 * (x + _GELU_C1 * (x * x * x)))
    return 0.5 * x * (1.0 + t), t


def _gelu_grad(x, t):
    return 0.5 * (1.0 + t) + 0.5 * x * (1.0 - t * t) * (_GELU_C0 * (1.0 + 3.0 * _GELU_C1 * x * x))


def _rowsum8(v):
    r, n = v.shape
    return jnp.sum(v.reshape(r // SUBLANES, SUBLANES, n), axis=0)


def _lanemean(v):
    return jnp.mean(v, axis=-1, keepdims=True)


def _shift_down(v, halo8, k):
    if k == 0:
        return v
    r = pltpu.roll(v, k, 0)
    hr = pltpu.roll(halo8, k, 0)
    row = lax.broadcasted_iota(jnp.int32, halo8.shape, 0)
    top = jnp.where(row < k, hr, r[0:SUBLANES])
    return jnp.concatenate([top, r[SUBLANES:]], axis=0)


def _shift_up(v, next8, k):
    if k == 0:
        return v
    n = v.shape[0]
    r = pltpu.roll(v, n - k, 0)
    nr = pltpu.roll(next8, SUBLANES - k, 0)
    row = lax.broadcasted_iota(jnp.int32, next8.shape, 0)
    bot = jnp.where(row >= SUBLANES - k, nr, r[n - SUBLANES:])
    return jnp.concatenate([r[:n - SUBLANES], bot], axis=0)


def _layernorm_parts(vg):
    mu = _lanemean(vg)
    xc = vg - mu
    rstd = lax.rsqrt(_lanemean(xc * xc) + EPS)
    return xc * rstd, rstd


def _spatial_mix(wt_ref, vn_ref, bsx_ref, mixed_ref, tm):
    for c in range(tm // CHUNK):
        rows = slice(c * CHUNK, (c + 1) * CHUNK)
        for h in range(N_HEADS):
            cols = slice(h * CHUNK, (h + 1) * CHUNK)
            mixed_ref[rows, cols] = _dot(wt_ref[h], vn_ref[rows, cols]) + bsx_ref[:, cols]


def _conv_taps(xb, halo8):
    return [_shift_down(xb, halo8, CONV_W - 1 - k) for k in range(CONV_W)]


def _lru_gates(xc_bf_ref, wa_ref, wx_ref, ba_ref, bx_ref, r_ref, i_ref):
    for h in range(N_HEADS):
        cols = slice(h * CHUNK, (h + 1) * CHUNK)
        xh = xc_bf_ref[:, cols]
        r_ref[:, cols] = jax.nn.sigmoid(_dot(xh, wa_ref[h]) + ba_ref[:, cols])
        i_ref[:, cols] = jax.nn.sigmoid(_dot(xh, wx_ref[h]) + bx_ref[:, cols])


def _softplus_neg(lam):
    return jnp.maximum(-lam, 0.0) + jnp.log(1.0 + jnp.exp(-jnp.abs(lam)))


def _decay_parts(r, lam):
    la = (-LRU_C * _softplus_neg(lam)) * r
    a = jnp.exp(la)
    th = -jnp.tanh(la)
    mult = jnp.sqrt(2.0 * th / (1.0 + th))
    return a, mult


def _z_group(zref, g, rows=slice(None)):
    lo = g * D_HALF
    blk, off = lo // W_IN_COLS, lo % W_IN_COLS
    if off + D_HALF <= W_IN_COLS:
        return zref[blk, rows, off:off + D_HALF]
    return jnp.concatenate([zref[blk, rows, off:W_IN_COLS], zref[blk + 1, rows, 0:off + D_HALF - W_IN_COLS]], axis=1)


def _inproj_local(x, pre_g, w_own, tm, token):
    t = x.shape[0]

    def body(x_ref, g_ref, w_ref, token_ref, hn_ref, zl_ref, hnt_ref, wbf_s):
        @pl.when(pl.program_id(0) == 0)
        def _():
            wbf_s[...] = w_ref[...].astype(BF16)

        xv = x_ref[...]
        hnf = xv * lax.rsqrt(_lanemean(xv * xv) + EPS) * g_ref[...]
        hn = hnf.astype(BF16)
        hn_ref[...] = hn
        hnt_ref[...] = hnf.T.astype(BF16)
        zl_ref[...] = _dot(hn, wbf_s[...]).astype(BF16)

    row = lambda n: pl.BlockSpec((tm, n), lambda i: (i, 0))
    const = lambda shp: pl.BlockSpec(shp, lambda i: (0, 0), pipeline_mode=pl.Buffered(1))
    return pl.pallas_call(
        body, name="inproj_local", grid=(t // tm,),
        in_specs=[row(D_MODEL), const((1, D_MODEL)), const((D_MODEL, W_IN_COLS)), const((SUBLANES, LANES))],
        out_specs=[row(D_MODEL), row(W_IN_COLS), pl.BlockSpec((D_MODEL, tm), lambda i: (0, i))],
        out_shape=[jax.ShapeDtypeStruct((t, D_MODEL), BF16), jax.ShapeDtypeStruct((t, W_IN_COLS), BF16),
                   jax.ShapeDtypeStruct((D_MODEL, t), BF16)],
        scratch_shapes=[pltpu.VMEM((D_MODEL, W_IN_COLS), BF16)],
        compiler_params=_params("arbitrary"),
    )(x, pre_g, w_own, token)


def _inproj_branches_fwd(hn, z_own, wg_in, kc, prm, tm, token):
    t = hn.shape[0]
    nt = t // tm
    hb = tm // SUBLANES

    def body(kc_ref, hn_ref, zo_ref, w1_ref, w2_ref, w3_ref,
             lng_ref, lnb_ref, wt_ref, bsx_ref, cw_ref, cb_ref, wa_ref, wx_ref, ba_ref, bx_ref, lam_ref,
             oga_ref, ogb_ref, token_ref,
             z_ref, y_ref, h_ref,
             zbuf0, zbuf1, vn_s, mixed_s, xcbf_s, r_s, i_s, ug_s, halo_s, carry_s):
        s = pl.program_id(0)
        me = kc_ref[0]
        w_refs = (None, w1_ref, w2_ref, w3_ref)

        @pl.when(s == 0)
        def _():
            zbuf1[...] = jnp.zeros_like(zbuf1)

        @pl.when(s <= 1)
        def _():
            carry_s[...] = jnp.zeros_like(carry_s)
            halo_s[...] = jnp.zeros_like(halo_s)

        def step(zw, zr):
            def project(r):
                blk = (me + r) % N_CHIPS
                zb = zo_ref[...] if r == 0 else _dot(hn_ref[...], w_refs[r][...]).astype(BF16)
                z_ref[blk] = zb
                zw[blk] = zb

            zin = lambda g: _z_group(zr, g).astype(F32)
            always = [s >= 0] * 4

            @pl.when(always[0])
            def _():
                project(0)
                ug, _ = _gelu(zin(0))
                ug_s[...] = ug
                vg, _ = _gelu(zin(1))
                vhat, _ = _layernorm_parts(vg)
                vn_s[...] = (vhat * lng_ref[...] + lnb_ref[...]).astype(BF16)

            @pl.when(always[1])
            def _():
                project(1)
                _spatial_mix(wt_ref, vn_s, bsx_ref, mixed_s, tm)
                ga = zin(2)
                ya = ug_s[...] * mixed_s[...] * (ga * jax.nn.sigmoid(ga))
                ra = lax.rsqrt(_lanemean(ya * ya) + EPS)
                y_ref[:, 0:D_HALF] = (ya * ra * oga_ref[...]).astype(BF16)

            @pl.when(always[2])
            def _():
                project(2)
                xb = zin(3)
                taps = _conv_taps(xb, halo_s[...])
                halo_s[...] = xb[tm - SUBLANES:]
                xc = cb_ref[...] + taps[0] * cw_ref[0:1, :]
                for k in range(1, CONV_W):
                    xc = xc + taps[k] * cw_ref[k:k + 1, :]
                xcbf_s[...] = xc.astype(BF16)
                _lru_gates(xcbf_s, wa_ref, wx_ref, ba_ref, bx_ref, r_s, i_s)
                a, mult = _decay_parts(r_s[...], lam_ref[...])
                row = lax.broadcasted_iota(jnp.int32, a.shape, 0)
                mult = jnp.where(jnp.logical_and(s == 1, row == 0), 1.0, mult)
                r_s[...] = a
                i_s[...] = mult * (i_s[...] * xc)

            @pl.when(always[3])
            def _():
                project(3)
                a = r_s[...]
                b = i_s[...]
                r8 = lax.broadcasted_iota(jnp.int32, a.shape, 0) & (SUBLANES - 1)
                for d in (1, 2, 4):
                    a_sh = pltpu.roll(a, d, 0)
                    b_sh = pltpu.roll(b, d, 0)
                    m = r8 >= d
                    b = jnp.where(m, a * b_sh + b, b)
                    a = jnp.where(m, a * a_sh, a)
                carry = carry_s[...]
                for g in range(hb):
                    rows = slice(g * SUBLANES, (g + 1) * SUBLANES)
                    hg = a[rows] * carry + b[rows]
                    h_ref[rows, :] = hg
                    carry = jnp.broadcast_to(hg[SUBLANES - 1:SUBLANES, :], hg.shape)
                carry_s[...] = carry
                gb = zin(4)
                yb = h_ref[...] * (gb * jax.nn.sigmoid(gb))
                rb = lax.rsqrt(_lanemean(yb * yb) + EPS)
                y_ref[:, D_HALF:] = (yb * rb * ogb_ref[...]).astype(BF16)

        @pl.when(s % 2 == 0)
        def _():
            step(zbuf0, zbuf1)

        @pl.when(s % 2 == 1)
        def _():
            step(zbuf1, zbuf0)

    const = lambda a: pl.BlockSpec(a.shape, lambda s, kc, n=a.ndim: (0,) * n, pipeline_mode=pl.Buffered(1))
    proj = lambda n: pl.BlockSpec((tm, n), lambda s, kc: (jnp.minimum(s, nt - 1), 0))
    head = lambda n: pl.BlockSpec((tm, n), lambda s, kc: (jnp.maximum(s - 1, 0), 0))
    other = lambda r: pl.BlockSpec((None, D_MODEL, W_IN_COLS), lambda s, kc, r=r: ((kc[0] + r) % N_CHIPS, 0, 0),
                                   pipeline_mode=pl.Buffered(1))
    names = ("ln_g", "ln_b", "wt", "bsx", "conv_w", "conv_b", "w_a", "w_x", "b_a", "b_x", "lam", "oga", "ogb")
    pr = [prm[n] for n in names] + [token]
    big = lambda dt: pltpu.VMEM((tm, D_HALF), dt)
    zblocks = pltpu.VMEM((N_CHIPS, tm, W_IN_COLS), BF16)
    grid_spec = pltpu.PrefetchScalarGridSpec(
        num_scalar_prefetch=1, grid=(nt + 1,),
        in_specs=[proj(D_MODEL), proj(W_IN_COLS), other(1), other(2), other(3)] + [const(a) for a in pr],
        out_specs=[pl.BlockSpec((N_CHIPS, tm, W_IN_COLS), lambda s, kc: (0, jnp.minimum(s, nt - 1), 0)),
                   head(D_MODEL), head(D_HALF)],
        scratch_shapes=[zblocks, zblocks, big(BF16), big(F32), big(BF16), big(F32), big(F32), big(F32),
                        pltpu.VMEM((SUBLANES, D_HALF), F32), pltpu.VMEM((SUBLANES, D_HALF), F32)])
    return pl.pallas_call(
        body, name="inproj_branches_fwd", grid_spec=grid_spec,
        out_shape=[jax.ShapeDtypeStruct((N_CHIPS, t, W_IN_COLS), BF16), jax.ShapeDtypeStruct((t, D_MODEL), BF16),
                   jax.ShapeDtypeStruct((t, D_HALF), F32)],
        compiler_params=_params("arbitrary"),
    )(kc, hn, z_own, wg_in, wg_in, wg_in, *pr)


def _head_fwd_bwd(x, y, p, tgt, post_g, w_out, w_pg, wg_pe, tm):
    t = x.shape[0]

    def body(x_ref, y_ref, p_ref, tgt_ref, pg_ref, wo_ref, wpg_ref, wpe_ref,
             h1_ref, dq_ref, dh1_ref, do_ref, dy_ref, gwpe_ref, gpost_ref, loss_ref, dout_s):
        i = pl.program_id(0)

        @pl.when(i == 0)
        def _():
            gpost_ref[...] = jnp.zeros_like(gpost_ref)
            gwpe_ref[...] = jnp.zeros_like(gwpe_ref)
            loss_ref[...] = jnp.zeros_like(loss_ref)

        pb = p_ref[...].astype(BF16)
        pes = [_dot(pb, wpe_ref[k]) for k in range(N_CHIPS)]
        o = _dot(y_ref[...], wo_ref[...])
        r3 = lax.rsqrt(_lanemean(o * o) + EPS)
        on = o * r3
        h1 = x_ref[...] + on * pg_ref[...]
        h1b = h1.astype(BF16)
        h1_ref[...] = h1b
        gt = jax.nn.sigmoid(_dot(h1b, wpg_ref[...]))
        for k in range(N_CHIPS):
            cols = slice(k * W_PE_COLS, (k + 1) * W_PE_COLS)
            pe = pes[k]
            g = gt[:, cols]
            d = h1[:, cols] + pe * g - tgt_ref[:, cols]
            loss_ref[...] += jnp.sum(d * d) * (0.5 / D_MODEL)
            dout = d * (1.0 / D_MODEL)
            dout_s[:, cols] = dout
            dg = dout * g
            gwpe_ref[k] += _dot_tn(pb, dg.astype(BF16))
            dq_ref[:, cols] = (dg * pe * (1.0 - g)).astype(BF16)
        dh1 = dout_s[...] + _dot_nt(dq_ref[...], wpg_ref[...])
        dh1_ref[...] = dh1
        gpost_ref[...] += _rowsum8(dh1 * on)
        don = dh1 * pg_ref[...]
        dob = (r3 * (don - on * _lanemean(don * on))).astype(BF16)
        do_ref[...] = dob
        dy_ref[...] = _dot_nt(dob, wo_ref[...])

        @pl.when(i == pl.num_programs(0) - 1)
        def _():
            gpost_ref[...] = jnp.broadcast_to(jnp.sum(gpost_ref[...], axis=0, keepdims=True), gpost_ref.shape)

    row = lambda n: pl.BlockSpec((tm, n), lambda i: (i, 0))
    const = lambda shp: pl.BlockSpec(shp, lambda i, n=len(shp): (0,) * n, pipeline_mode=pl.Buffered(1))
    acc = lambda shp: pl.BlockSpec(shp, lambda i, n=len(shp): (0,) * n)
    return pl.pallas_call(
        body, name="head_fwd_bwd", grid=(t // tm,),
        in_specs=[row(D_MODEL), row(D_MODEL), row(D_PLE), row(D_MODEL), const((1, D_MODEL)),
                  const((D_MODEL, D_MODEL)), const((D_MODEL, D_MODEL)), const((N_CHIPS, D_PLE, W_PE_COLS))],
        out_specs=[row(D_MODEL), row(D_MODEL), row(D_MODEL), row(D_MODEL), row(D_MODEL),
                   acc((N_CHIPS, D_PLE, W_PE_COLS)), acc((SUBLANES, D_MODEL)), acc((SUBLANES, LANES))],
        out_shape=[jax.ShapeDtypeStruct((t, D_MODEL), BF16), jax.ShapeDtypeStruct((t, D_MODEL), BF16),
                   jax.ShapeDtypeStruct((t, D_MODEL), F32), jax.ShapeDtypeStruct((t, D_MODEL), BF16),
                   jax.ShapeDtypeStruct((t, D_MODEL), F32),
                   jax.ShapeDtypeStruct((N_CHIPS, D_PLE, W_PE_COLS), F32),
                   jax.ShapeDtypeStruct((SUBLANES, D_MODEL), F32), jax.ShapeDtypeStruct((SUBLANES, LANES), F32)],
        scratch_shapes=[pltpu.VMEM((tm, D_MODEL), F32)],
        compiler_params=_params("arbitrary"),
    )(x, y, p, tgt, post_g, w_out, w_pg, wg_pe)


def _branches_bwd(z, h, dy, prm, tm, token):
    t = h.shape[0]
    nt = t // tm
    hb = tm // SUBLANES

    def body(z_ref, zh_ref, h_ref, hh_ref, dy_ref,
             lng_ref, lnb_ref, wt_ref, wtt_ref, bsx_ref, cw_ref, cb_ref, wa_ref, wx_ref, ba_ref, bx_ref, lam_ref,
             oga_ref, ogb_ref, token_ref,
             dz_ref, g_oga, g_ogb, g_lng, g_lnb, g_bsx, g_ws, g_cw, g_cb, g_wa, g_ba, g_wx, g_bx, g_lam,
             vn_s, mixed_s, dm_s, dvn_s, xcbf_s, r_s, i_s, a_s, b_s, dh_s, dpr_s, dpi_s, dxc_s,
             ca_s, cd_s, cx_s):
        step_i = pl.program_id(0)
        tile = nt - 1 - step_i
        accs = (g_oga, g_ogb, g_lng, g_lnb, g_bsx, g_ws, g_cw, g_cb, g_wa, g_ba, g_wx, g_bx, g_lam)

        @pl.when(step_i == 0)
        def _():
            for r in accs + (ca_s, cd_s, cx_s):
                r[...] = jnp.zeros_like(r)

        dy_a = dy_ref[:, 0:D_HALF]
        dy_b = dy_ref[:, D_HALF:]

        u = _z_group(z_ref, 0).astype(F32)
        ug, tu = _gelu(u)
        v = _z_group(z_ref, 1).astype(F32)
        vg, tv = _gelu(v)
        vhat, rstd = _layernorm_parts(vg)
        vn_s[...] = (vhat * lng_ref[...] + lnb_ref[...]).astype(BF16)
        _spatial_mix(wt_ref, vn_s, bsx_ref, mixed_s, tm)
        mixed = mixed_s[...]
        ga = _z_group(z_ref, 2).astype(F32)
        sga = jax.nn.sigmoid(ga)
        sa = ga * sga
        um = ug * mixed
        ya = um * sa
        ra = lax.rsqrt(_lanemean(ya * ya) + EPS)
        yahat = ya * ra
        g_oga[...] += _rowsum8(dy_a * yahat)
        dn = dy_a * oga_ref[...]
        dya = ra * (dn - yahat * _lanemean(dn * yahat))
        dz_ref[:, 2 * D_HALF:3 * D_HALF] = (dya * um * (sga * (1.0 + ga * (1.0 - sga)))).astype(BF16)
        dz_ref[:, 0:D_HALF] = (dya * mixed * sa * _gelu_grad(u, tu)).astype(BF16)
        dmixed = dya * ug * sa
        g_bsx[...] += jnp.sum(dmixed.reshape(tm // CHUNK, CHUNK, D_HALF), axis=0)
        dm_s[...] = dmixed.astype(BF16)
        for c in range(tm // CHUNK):
            rows = slice(c * CHUNK, (c + 1) * CHUNK)
            for hd in range(N_HEADS):
                cols = slice(hd * CHUNK, (hd + 1) * CHUNK)
                dmh = dm_s[rows, cols]
                dvn_s[rows, cols] = _dot(wtt_ref[hd], dmh)
                g_ws[hd] += _dot_nt(dmh, vn_s[rows, cols])
        dvn = dvn_s[...]
        g_lng[...] += _rowsum8(dvn * vhat)
        g_lnb[...] += _rowsum8(dvn)
        dvh = dvn * lng_ref[...]
        dvg = rstd * (dvh - _lanemean(dvh) - vhat * _lanemean(dvh * vhat))
        dz_ref[:, D_HALF:2 * D_HALF] = (dvg * _gelu_grad(v, tv)).astype(BF16)

        xb = _z_group(z_ref, 3).astype(F32)
        halo = jnp.where(tile == 0, 0.0, _z_group(zh_ref, 3).astype(F32)[SUBLANES:])
        taps = _conv_taps(xb, halo)
        xc = cb_ref[...] + taps[0] * cw_ref[0:1, :]
        for k in range(1, CONV_W):
            xc = xc + taps[k] * cw_ref[k:k + 1, :]
        xcbf_s[...] = xc.astype(BF16)
        _lru_gates(xcbf_s, wa_ref, wx_ref, ba_ref, bx_ref, r_s, i_s)
        rg = r_s[...]
        ig = i_s[...]
        lam = lam_ref[...]
        a, mult_true = _decay_parts(rg, lam)
        row = lax.broadcasted_iota(jnp.int32, a.shape, 0)
        first = jnp.logical_and(tile == 0, row == 0)
        mult = jnp.where(first, 1.0, mult_true)
        hcur = h_ref[...]
        hprev = _shift_down(hcur, jnp.where(tile == 0, 0.0, hh_ref[...]), 1)
        gb = _z_group(z_ref, 4).astype(F32)
        sgb = jax.nn.sigmoid(gb)
        sb = gb * sgb
        yb = hcur * sb
        rb = lax.rsqrt(_lanemean(yb * yb) + EPS)
        ybhat = yb * rb
        g_ogb[...] += _rowsum8(dy_b * ybhat)
        dn = dy_b * ogb_ref[...]
        dyb = rb * (dn - ybhat * _lanemean(dn * ybhat))
        dz_ref[:, 4 * D_HALF:5 * D_HALF] = (dyb * hcur * (sgb * (1.0 + gb * (1.0 - sgb)))).astype(BF16)

        an = _shift_up(a, ca_s[...], 1)
        bb = dyb * sb
        r8 = row & (SUBLANES - 1)
        for d in (1, 2, 4):
            a_sh = pltpu.roll(an, tm - d, 0)
            b_sh = pltpu.roll(bb, tm - d, 0)
            m = r8 + d < SUBLANES
            bb = jnp.where(m, an * b_sh + bb, bb)
            an = jnp.where(m, an * a_sh, an)
        a_s[...] = an
        b_s[...] = bb

        def step(g, carry):
            sl = pl.ds(pl.multiple_of((hb - 1 - g) * SUBLANES, SUBLANES), SUBLANES)
            dg = a_s[sl, :] * carry + b_s[sl, :]
            dh_s[sl, :] = dg
            return jnp.broadcast_to(dg[0:1, :], dg.shape)

        cd_s[...] = lax.fori_loop(0, hb, step, cd_s[...])
        ca_s[...] = jnp.broadcast_to(a[0:1, :], ca_s.shape)
        dh = dh_s[...]
        da = dh * hprev
        gx = ig * xc
        dla = da * a - jnp.where(first, 0.0, dh * gx * (a * a / mult_true))
        g_lam[...] += _rowsum8(dla * rg)
        dr = dla * (-LRU_C * _softplus_neg(lam))
        dpr = dr * rg * (1.0 - rg)
        dpi = (dh * mult * xc) * ig * (1.0 - ig)
        g_ba[...] += _rowsum8(dpr)
        g_bx[...] += _rowsum8(dpi)
        dpr_s[...] = dpr.astype(BF16)
        dpi_s[...] = dpi.astype(BF16)
        for hd in range(N_HEADS):
            cols = slice(hd * CHUNK, (hd + 1) * CHUNK)
            xh = xcbf_s[:, cols]
            dprh = dpr_s[:, cols]
            dpih = dpi_s[:, cols]
            g_wa[hd] += _dot_tn(xh, dprh)
            g_wx[hd] += _dot_tn(xh, dpih)
            dxc_s[:, cols] = _dot_nt(dprh, wa_ref[hd]) + _dot_nt(dpih, wx_ref[hd])
        dxc = dxc_s[...] + dh * mult * ig
        g_cb[...] += _rowsum8(dxc)
        for k in range(CONV_W):
            g_cw[k * SUBLANES:(k + 1) * SUBLANES, :] += _rowsum8(dxc * taps[k])
        nxt = cx_s[...]
        dxb = dxc * cw_ref[CONV_W - 1:CONV_W, :]
        for j in range(1, CONV_W):
            dxb = dxb + _shift_up(dxc, nxt, j) * cw_ref[CONV_W - 1 - j:CONV_W - j, :]
        dz_ref[:, 3 * D_HALF:4 * D_HALF] = dxb.astype(BF16)
        cx_s[...] = dxc[0:SUBLANES]

        @pl.when(step_i == nt - 1)
        def _():
            for r in (g_oga, g_ogb, g_lng, g_lnb, g_cb, g_ba, g_bx):
                r[...] = jnp.broadcast_to(jnp.sum(r[...], axis=0, keepdims=True), r.shape)
            lam_f = LRU_C * jax.nn.sigmoid(-lam_ref[...])
            g_lam[...] = jnp.broadcast_to(jnp.sum(g_lam[...], axis=0, keepdims=True) * lam_f, g_lam.shape)
            for k in range(CONV_W):
                blk = g_cw[k * SUBLANES:(k + 1) * SUBLANES, :]
                g_cw[k * SUBLANES:(k + 1) * SUBLANES, :] = jnp.broadcast_to(jnp.sum(blk, axis=0, keepdims=True), blk.shape)
            tri = (lax.broadcasted_iota(jnp.int32, (CHUNK, CHUNK), 0) >= lax.broadcasted_iota(jnp.int32, (CHUNK, CHUNK), 1))
            for hd in range(N_HEADS):
                cols = slice(hd * CHUNK, (hd + 1) * CHUNK)
                g_ws[hd] = jnp.where(tri, g_ws[hd], 0.0)
                blk = g_bsx[:, cols]
                g_bsx[:, cols] = jnp.broadcast_to(jnp.sum(blk, axis=1, keepdims=True), blk.shape)

    rev = lambda i: nt - 1 - i
    zspec = pl.BlockSpec((N_CHIPS, tm, W_IN_COLS), lambda i: (0, rev(i), 0))
    halo = lambda col: pl.BlockSpec((SUBLANES, D_HALF), lambda i: (jnp.maximum(rev(i) * hb - 1, 0), col))
    zhalo = pl.BlockSpec((N_CHIPS, 2 * SUBLANES, W_IN_COLS), lambda i: (0, jnp.maximum(rev(i) * (hb // 2) - 1, 0), 0))
    full = lambda a: pl.BlockSpec(a.shape, lambda i, n=a.ndim: (0,) * n)
    acc = lambda shp: pl.BlockSpec(shp, lambda i, n=len(shp): (0,) * n)
    names = ("ln_g", "ln_b", "wt", "wtt", "bsx", "conv_w", "conv_b", "w_a", "w_x", "b_a", "b_x", "lam", "oga", "ogb")
    pr = [prm[n] for n in names] + [token]
    vec = (SUBLANES, D_HALF)
    mat = (N_HEADS, CHUNK, CHUNK)
    acc_shapes = [vec, vec, vec, vec, (CHUNK, D_HALF), mat, (CONV_W * SUBLANES, D_HALF), vec, mat, vec, mat, vec, vec]
    big = lambda dt: pltpu.VMEM((tm, D_HALF), dt)
    return pl.pallas_call(
        body, name="branches_bwd", grid=(nt,),
        in_specs=[zspec, zhalo,
                  pl.BlockSpec((tm, D_HALF), lambda i: (rev(i), 0)), halo(0),
                  pl.BlockSpec((tm, D_MODEL), lambda i: (rev(i), 0))] + [full(a) for a in pr],
        out_specs=[pl.BlockSpec((tm, D_Z), lambda i: (rev(i), 0))] + [acc(s) for s in acc_shapes],
        out_shape=[jax.ShapeDtypeStruct((t, D_Z), BF16)] + [jax.ShapeDtypeStruct(s, F32) for s in acc_shapes],
        scratch_shapes=[big(BF16), big(F32), big(BF16), big(F32), big(BF16), big(F32), big(F32), big(F32), big(F32),
                        big(F32), big(BF16), big(BF16), big(F32),
                        pltpu.VMEM(vec, F32), pltpu.VMEM(vec, F32), pltpu.VMEM(vec, F32)],
        compiler_params=_params("arbitrary"),
    )(z, z, h, h, dy, *pr)


def _inproj_bwd(dz, wg_in, x, dh1, pre_g, tm, tile0, nt, prev, last, token, name):
    t = x.shape[0]

    def body(*refs):
        dz_ref, w_ref, x_ref, dh1_ref, g_ref = refs[:5]
        gx_ref, gpre_ref, acc_s = refs[-3:]
        i = pl.program_id(0)

        def project():
            acc = _dot_nt(dz_ref[:, 0:W_IN_COLS], w_ref[0])
            for k in range(1, N_CHIPS):
                acc = acc + _dot_nt(dz_ref[:, k * W_IN_COLS:(k + 1) * W_IN_COLS], w_ref[k])
            acc_s[i % 2] = acc

        def norm_bwd():
            done = acc_s.at[(i + 1) % 2]
            for s in range(tm // CHUNK):
                rows = slice(s * CHUNK, (s + 1) * CHUNK)
                xv = x_ref[rows, :]
                r = lax.rsqrt(_lanemean(xv * xv) + EPS)
                xhat = xv * r
                dhn = done[rows, :]
                gpre_ref[...] += _rowsum8(dhn * xhat)
                dxh = dhn * g_ref[...]
                gx_ref[rows, :] = dh1_ref[rows, :] + r * (dxh - xhat * _lanemean(dxh * xhat))

        @pl.when(i == 0)
        def _():
            gpre_ref[...] = jnp.zeros_like(gpre_ref) if prev is None else refs[7][...]
            project()

        @pl.when((i > 0) & (i < nt))
        def _():
            norm_bwd()
            project()

        @pl.when(i == nt)
        def _():
            norm_bwd()
            if last:
                gpre_ref[...] = jnp.broadcast_to(jnp.sum(gpre_ref[...], axis=0, keepdims=True), gpre_ref.shape)

    ahead = lambda n: pl.BlockSpec((tm, n), lambda i: (tile0 + jnp.minimum(i, nt - 1), 0))
    row = lambda n: pl.BlockSpec((tm, n), lambda i: (tile0 + jnp.maximum(i - 1, 0), 0))
    small = lambda r: pl.BlockSpec((r, D_MODEL), lambda i: (0, 0))
    tok = pl.BlockSpec((SUBLANES, LANES), lambda i: (0, 0))
    in_specs = [ahead(D_Z), pl.BlockSpec(wg_in.shape, lambda i: (0, 0, 0), pipeline_mode=pl.Buffered(1)),
                row(D_MODEL), row(D_MODEL), small(1), tok]
    args = [dz, wg_in, x, dh1, pre_g, token]
    aliases = {}
    if prev is not None:
        in_specs += [ANY, small(SUBLANES)]
        args += list(prev)
        aliases = {6: 0}
    return pl.pallas_call(
        body, name=name, grid=(nt + 1,), in_specs=in_specs, out_specs=[row(D_MODEL), small(SUBLANES)],
        out_shape=[jax.ShapeDtypeStruct((t, D_MODEL), F32), jax.ShapeDtypeStruct((SUBLANES, D_MODEL), F32)],
        input_output_aliases=aliases,
        scratch_shapes=[pltpu.VMEM((2, tm, D_MODEL), F32)],
        compiler_params=_params("arbitrary"),
    )(*args)


def _weight_grad(a, b, name, kb, nb, tk, tn, tt, token, a_transposed=False):
    t = b.shape[0]
    tt = min(tt, t)

    def body(a_ref, b_ref, token_ref, o_ref):
        @pl.when(pl.program_id(2) == 0)
        def _():
            o_ref[...] = jnp.zeros_like(o_ref)

        o_ref[...] += (_dot if a_transposed else _dot_tn)(a_ref[...], b_ref[...])

    a_spec = (pl.BlockSpec((tk, tt), lambda j, i, s: (i, s)) if a_transposed
              else pl.BlockSpec((tt, tk), lambda j, i, s: (s, i)))
    return pl.pallas_call(
        body, name=name, grid=(nb, kb, t // tt),
        in_specs=[a_spec, pl.BlockSpec((tt, tn), lambda j, i, s: (s, j)),
                  pl.BlockSpec((SUBLANES, LANES), lambda j, i, s: (0, 0))],
        out_specs=pl.BlockSpec((None, None, tk, tn), lambda j, i, s: (j, i, 0, 0)),
        out_shape=jax.ShapeDtypeStruct((nb, kb, tk, tn), F32),
        compiler_params=_params("parallel", "parallel", "arbitrary"),
    )(a, b, token)


def _place():
    x, y, c = lax.axis_index("x"), lax.axis_index("y"), lax.axis_index("c")
    return x, y, c


def _chip_of(x, y):
    return 2 * x + y


HBM = pl.BlockSpec(memory_space=pltpu.HBM)
SEM = pl.BlockSpec(memory_space=pltpu.SEMAPHORE)
EFFECT = pltpu.SideEffectType.DATAFLOW_SIDE_EFFECTING


def _hbm(a):
    return pltpu.with_memory_space_constraint(a, pltpu.HBM)


def _landing(shape, dtype):
    return _hbm(lax.empty(shape, dtype))


def _exchange_start(name, arrays, ncopies, build, after=None):
    n = len(arrays)
    extra = [] if after is None else [after]

    def body(*refs):
        ins, token = refs[:n], refs[-1]
        send_sems, recv_sems = refs[n + len(extra)], refs[n + len(extra) + 1]
        for cp in build(ins, send_sems, recv_sems):
            cp.start()
        token[...] = jnp.zeros_like(token)

    outs = pl.pallas_call(
        body, name=name,
        out_shape=(pltpu.SemaphoreType.DMA((ncopies,)), pltpu.SemaphoreType.DMA((ncopies,)),
                   *[pltpu.HBM(a.shape, a.dtype) for a in arrays], jax.ShapeDtypeStruct((SUBLANES, LANES), F32)),
        in_specs=[HBM] * n + [ANY] * len(extra),
        out_specs=(SEM, SEM, *[HBM] * n, pl.BlockSpec(memory_space=pltpu.VMEM)),
        input_output_aliases={q: q + 2 for q in range(n)},
        compiler_params=pltpu.CompilerParams(has_side_effects=EFFECT),
    )(*[_hbm(a) for a in arrays], *extra)
    return (outs[0], outs[1], list(outs[2:2 + n])), outs[-1]


def _exchange_wait(name, started, after, build):
    send, recv, arrays = started
    n = len(arrays)

    def body(*refs):
        ins, send_sems, recv_sems = refs[:n], refs[n], refs[n + 1]
        for cp in build(ins, send_sems, recv_sems):
            cp.wait_send()
            cp.wait_recv()

    return pl.pallas_call(
        body, name=name, out_shape=tuple(pltpu.HBM(a.shape, a.dtype) for a in arrays),
        in_specs=[HBM] * n + [SEM, SEM, ANY], out_specs=tuple([HBM] * n),
        input_output_aliases={q: q for q in range(n)},
        compiler_params=pltpu.CompilerParams(has_side_effects=EFFECT),
    )(*arrays, send, recv, after)


def _exchange_wait_start(name, started, after, build_wait, ncopies, build_start):
    send, recv, arrays = started
    n = len(arrays)

    def body(*refs):
        ins, send_sems, recv_sems = refs[:n], refs[n], refs[n + 1]
        send2, recv2, token = refs[n + 3], refs[n + 4], refs[-1]
        arrived = build_wait(ins, send_sems, recv_sems)
        for cp, onward in zip(arrived, build_start(ins, send2, recv2)):
            cp.wait_recv()
            onward.start()
        for cp in arrived:
            cp.wait_send()
        token[...] = jnp.zeros_like(token)

    outs = pl.pallas_call(
        body, name=name,
        out_shape=(pltpu.SemaphoreType.DMA((ncopies,)), pltpu.SemaphoreType.DMA((ncopies,)),
                   *[pltpu.HBM(a.shape, a.dtype) for a in arrays], jax.ShapeDtypeStruct((SUBLANES, LANES), F32)),
        in_specs=[HBM] * n + [SEM, SEM, ANY], out_specs=(SEM, SEM, *[HBM] * n, pl.BlockSpec(memory_space=pltpu.VMEM)),
        input_output_aliases={q: q + 2 for q in range(n)},
        compiler_params=pltpu.CompilerParams(has_side_effects=EFFECT),
    )(*arrays, send, recv, after)
    return (outs[0], outs[1], list(outs[2:2 + n])), outs[-1]


def _cast_into_slot(w, kc, name, dtype=BF16, token=None):
    rows, cols = w.shape
    tr = min(rows, 4 * SUM_TILE)
    extra = [] if token is None else [token]

    def body(kc_ref, w_ref, *rest):
        rest[-1][...] = w_ref[...].astype(dtype)

    grid_spec = pltpu.PrefetchScalarGridSpec(
        num_scalar_prefetch=1, grid=(rows // tr,),
        in_specs=[pl.BlockSpec((tr, cols), lambda r, kc: (r, 0))]
                 + [pl.BlockSpec((SUBLANES, LANES), lambda r, kc: (0, 0))] * len(extra),
        out_specs=pl.BlockSpec((None, tr, cols), lambda r, kc: (kc[0], r, 0)))
    return pl.pallas_call(
        body, name=name, grid_spec=grid_spec, out_shape=jax.ShapeDtypeStruct((N_CHIPS, rows, cols), dtype),
        compiler_params=_params("arbitrary"),
    )(kc, w, *extra)


def _gather_ici_copies(n):
    def build(refs, send_sems, recv_sems):
        x, y, c = _place()
        mine = lambda b: refs[b].at[_chip_of(x, y), c]
        chips = [(1 - x, y), (x, 1 - y), (1 - x, 1 - y)]
        return [pltpu.make_async_remote_copy(
            src_ref=mine(b), dst_ref=mine(b), send_sem=send_sems.at[3 * b + j], recv_sem=recv_sems.at[3 * b + j],
            device_id=(*chip, c), device_id_type=MESH) for b in range(n) for j, chip in enumerate(chips)]
    return build


def _gather_direct_copies(n):
    def build(refs, send_sems, recv_sems):
        x, y, c = _place()
        mine = lambda b: refs[b].at[_chip_of(x, y)]
        chips = [(1 - x, y), (x, 1 - y), (1 - x, 1 - y)]
        return [pltpu.make_async_remote_copy(
            src_ref=mine(b), dst_ref=mine(b), send_sem=send_sems.at[3 * b + j], recv_sem=recv_sems.at[3 * b + j],
            device_id=(*chip, c), device_id_type=MESH) for b in range(n) for j, chip in enumerate(chips)]
    return build


def _gather_relay_copies(n):
    def build(refs, send_sems, recv_sems):
        x, y, c = _place()
        chips = [(1 - x, y), (x, 1 - y), (1 - x, 1 - y)]
        cps = []
        for b in range(n):
            for j, chip in enumerate(chips):
                got = refs[b].at[_chip_of(*chip), c]
                cps.append(pltpu.make_async_remote_copy(
                    src_ref=got, dst_ref=got, send_sem=send_sems.at[3 * b + j], recv_sem=recv_sems.at[3 * b + j],
                    device_id=(x, y, 1 - c), device_id_type=MESH))
        return cps
    return build


def _sibling_copies(n):
    def build(refs, send_sems, recv_sems):
        x, y, c = _place()
        return [pltpu.make_async_remote_copy(
            src_ref=refs[b].at[:, 1 - c], dst_ref=refs[n + b], send_sem=send_sems.at[b], recv_sem=recv_sems.at[b],
            device_id=(x, y, 1 - c), device_id_type=MESH) for b in range(n)]
    return build


def _chip_copies(n):
    def build(refs, send_sems, recv_sems):
        x, y, c = _place()
        chips = [(1 - x, y), (x, 1 - y), (1 - x, 1 - y)]
        return [pltpu.make_async_remote_copy(
            src_ref=refs[b].at[_chip_of(*chip)], dst_ref=refs[n + b].at[j],
            send_sem=send_sems.at[3 * b + j], recv_sem=recv_sems.at[3 * b + j],
            device_id=(*chip, c), device_id_type=MESH) for b in range(n) for j, chip in enumerate(chips)]
    return build


def _finish_copies(n, n_all):
    def build(refs, send_sems, recv_sems):
        x, y, c = _place()
        cps = [pltpu.make_async_remote_copy(
            src_ref=refs[b].at[c], dst_ref=refs[b].at[c], send_sem=send_sems.at[b], recv_sem=recv_sems.at[b],
            device_id=(x, y, 1 - c), device_id_type=MESH) for b in range(n)]
        flips = [(fx, fy, fc) for fx in (0, 1) for fy in (0, 1) for fc in (0, 1)][1:]
        for b in range(n_all):
            mine = refs[n + b].at[_chip_of(x, y), c]
            cps += [pltpu.make_async_remote_copy(
                src_ref=mine, dst_ref=mine, send_sem=send_sems.at[n + 7 * b + q], recv_sem=recv_sems.at[n + 7 * b + q],
                device_id=(x ^ fx, y ^ fy, c ^ fc), device_id_type=MESH) for q, (fx, fy, fc) in enumerate(flips)]
        return cps
    return build


def _pair_sum(g, r1, kc, name, tr, send_dtype):
    nk, _, rows, cols = g.shape
    tr = min(tr, rows)

    def body(kc_ref, g_ref, r_ref, p_ref, own_ref):
        s = g_ref[...] + r_ref[...]
        p_ref[...] = s.astype(send_dtype)

        @pl.when(pl.program_id(1) == kc_ref[0])
        def _():
            own_ref[...] = s

    grid_spec = pltpu.PrefetchScalarGridSpec(
        num_scalar_prefetch=1, grid=(rows // tr, nk),
        in_specs=[pl.BlockSpec((None, None, tr, cols), lambda r, k, kc: (k, kc[1], r, 0)),
                  pl.BlockSpec((None, tr, cols), lambda r, k, kc: (k, r, 0))],
        out_specs=[pl.BlockSpec((None, tr, cols), lambda r, k, kc: (k, r, 0)),
                   pl.BlockSpec((tr, cols), lambda r, k, kc: (r, 0))])
    return pl.pallas_call(
        body, name=name, grid_spec=grid_spec,
        out_shape=[jax.ShapeDtypeStruct((nk, rows, cols), send_dtype), jax.ShapeDtypeStruct((rows, cols), F32)],
        compiler_params=_params("arbitrary", "arbitrary"),
    )(kc, g, r1)


def _chip_sum(own, r2, slot, lead, name, tr):
    rows, cols = own.shape
    tr = min(tr, rows)
    nl = len(lead)

    def body(slot_ref, o_ref, r_ref, s_ref):
        s = o_ref[...]
        for j in range(3):
            s = s + r_ref[j].astype(F32)
        s_ref[...] = s

    grid_spec = pltpu.PrefetchScalarGridSpec(
        num_scalar_prefetch=1, grid=(rows // tr,),
        in_specs=[pl.BlockSpec((tr, cols), lambda r, sl: (r, 0)), pl.BlockSpec((3, tr, cols), lambda r, sl: (0, r, 0))],
        out_specs=pl.BlockSpec((None,) * nl + (tr, cols), lambda r, sl: tuple(sl[q] for q in range(nl)) + (r, 0)))
    return pl.pallas_call(
        body, name=name, grid_spec=grid_spec, out_shape=jax.ShapeDtypeStruct(tuple(lead) + (rows, cols), F32),
        compiler_params=_params("arbitrary"),
    )(slot, own, r2)


def _adam_update(w, g, m, v):
    nm = ADAM_B1 * m + (1.0 - ADAM_B1) * g
    nv = ADAM_B2 * v + (1.0 - ADAM_B2) * (g * g)
    m_hat = nm / (1.0 - ADAM_B1 ** ADAM_STEP)
    v_hat = nv / (1.0 - ADAM_B2 ** ADAM_STEP)
    return -ADAM_LR * (m_hat / (jnp.sqrt(v_hat) + ADAM_EPS) + ADAM_WD * w), nm, nv


def _adamw(w, g, m, v, name, tr, token):
    rows, cols = w.shape
    tr = min(tr, rows)

    def body(w_ref, g_ref, m_ref, v_ref, token_ref, go_ref, d_ref, nm_ref, nv_ref):
        gv = g_ref[...]
        go_ref[...] = gv
        d_ref[...], nm_ref[...], nv_ref[...] = _adam_update(w_ref[...], gv, m_ref[...], v_ref[...])

    spec = pl.BlockSpec((tr, cols), lambda r: (r, 0))
    return pl.pallas_call(
        body, name=name, grid=(rows // tr,),
        in_specs=[spec] * 4 + [pl.BlockSpec((SUBLANES, LANES), lambda r: (0, 0))], out_specs=[spec] * 4,
        out_shape=[jax.ShapeDtypeStruct((rows, cols), F32)] * 4,
        compiler_params=_params("parallel"),
    )(w, g, m, v, token)


def _adamw_small(packed_g, pre_g_parts, ws, ms, vs):
    names = ["pre_g"] + [n for n, _ in SMALL_ROWS if n != "conv_w"]
    rows = dict(SMALL_ROWS)
    offset, at = {}, 0
    for n, r in SMALL_ROWS:
        offset[n] = at
        at += r
    k = len(names)

    def body(*refs):
        g_ref, pg_ref = refs[0], refs[1]
        w_refs, m_refs, v_refs = refs[2:2 + k], refs[2 + k:2 + 2 * k], refs[2 + 2 * k:2 + 3 * k]
        outs = refs[2 + 3 * k:]
        go, do, mo, vo = outs[:k], outs[k:2 * k], outs[2 * k:3 * k], outs[3 * k:4 * k]
        pre = pg_ref[0]
        for dev in range(1, 8):
            pre = pre + pg_ref[dev]
        outs[4 * k][...] = pre[D_MODEL // LANES:, :]
        for i, n in enumerate(names):
            shp = w_refs[i].shape
            if len(shp) == 2 and shp[0] == 1:
                for r in range(shp[1] // LANES):
                    cols = slice(r * LANES, (r + 1) * LANES)
                    g = pre[r:r + 1, :] if n == "pre_g" else g_ref[offset[n] + r:offset[n] + r + 1, :]
                    go[i][:, cols] = g
                    do[i][:, cols], mo[i][:, cols], vo[i][:, cols] = _adam_update(
                        w_refs[i][:, cols], g, m_refs[i][:, cols], v_refs[i][:, cols])
            else:
                g = g_ref[offset[n]:offset[n] + rows[n], :].reshape(shp)
                go[i][...] = g
                do[i][...], mo[i][...], vo[i][...] = _adam_update(w_refs[i][...], g, m_refs[i][...], v_refs[i][...])

    vm = pl.BlockSpec(memory_space=pltpu.VMEM)
    args = [packed_g, pre_g_parts] + [src[n] for src in (ws, ms, vs) for n in names]
    out_shape = [jax.ShapeDtypeStruct(ws[n].shape, F32) for _ in range(4) for n in names]
    out_shape.append(jax.ShapeDtypeStruct((SUBLANES, LANES), F32))
    outs = pl.pallas_call(
        body, name="adamw_small", in_specs=[vm] * len(args), out_specs=[vm] * (4 * k + 1), out_shape=out_shape,
    )(*args)
    return [dict(zip(names, outs[q * k:(q + 1) * k])) for q in range(4)], outs[4 * k]


def _into_slot(v, tail, slot, lead, name):
    n = v.shape[1]
    nl = len(lead)
    rows = n // LANES + SUBLANES

    def body(slot_ref, v_ref, t_ref, o_ref):
        for r in range(n // LANES):
            o_ref[r:r + 1, :] = v_ref[0:1, r * LANES:(r + 1) * LANES]
        o_ref[n // LANES:, :] = t_ref[...]

    grid_spec = pltpu.PrefetchScalarGridSpec(
        num_scalar_prefetch=1, grid=(1,),
        in_specs=[pl.BlockSpec(v.shape, lambda i, sl: (0, 0)), pl.BlockSpec(tail.shape, lambda i, sl: (0, 0))],
        out_specs=pl.BlockSpec((None,) * nl + (rows, LANES), lambda i, sl: tuple(sl[q] for q in range(nl)) + (0, 0)))
    return pl.pallas_call(
        body, name=name, grid_spec=grid_spec, out_shape=jax.ShapeDtypeStruct(tuple(lead) + (rows, LANES), F32),
    )(slot, v, tail)


def _pack_small(parts):
    names = [n for n, _ in SMALL_ROWS]
    offset, at = {}, 0
    for n, r in SMALL_ROWS:
        offset[n] = at
        at += r

    def body(*refs):
        ins, o_ref = dict(zip(names, refs[:-1])), refs[-1]
        o_ref[SMALL_USED:, :] = jnp.zeros((SMALL_TOTAL - SMALL_USED, LANES), F32)
        for n, rows in SMALL_ROWS:
            ref, at = ins[n], offset[n]
            if n == "gmlp_bs":
                for h in range(N_HEADS):
                    o_ref[at + h:at + h + 1, :] = jnp.transpose(ref[:, h * CHUNK:(h + 1) * CHUNK])[0:1, :]
            elif n == "conv_w":
                for k in range(CONV_W):
                    for r in range(D_HALF // LANES):
                        row = at + k * (D_HALF // LANES) + r
                        o_ref[row:row + 1, :] = ref[k * SUBLANES:k * SUBLANES + 1, r * LANES:(r + 1) * LANES]
            elif ref.ndim == 3:
                o_ref[at:at + rows, :] = ref[...].reshape(rows, LANES)
            else:
                for r in range(rows):
                    o_ref[at + r:at + r + 1, :] = ref[0:1, r * LANES:(r + 1) * LANES]

    vm = pl.BlockSpec(memory_space=pltpu.VMEM)
    return pl.pallas_call(
        body, name="pack_small", in_specs=[vm] * len(names), out_specs=vm,
        out_shape=jax.ShapeDtypeStruct((SMALL_TOTAL, LANES), F32),
    )(*[parts[n] for n in names])


def kernel(x, p, pre_g, w_in, gmlp_ln_g, gmlp_ln_b, gmlp_ws, gmlp_bs, conv_w, conv_b, w_a, b_a, w_x, b_x, lam, gmlp_out_g, lru_out_g, w_out, post_g, w_pe, w_pg, loss_target, m_pre_g, m_w_in, m_gmlp_ln_g, m_gmlp_ln_b, m_gmlp_ws, m_gmlp_bs, m_conv_w, m_conv_b, m_w_a, m_b_a, m_w_x, m_b_x, m_lam, m_gmlp_out_g, m_lru_out_g, m_w_out, m_post_g, m_w_pe, m_w_pg, v_pre_g, v_w_in, v_gmlp_ln_g, v_gmlp_ln_b, v_gmlp_ws, v_gmlp_bs, v_conv_w, v_conv_b, v_w_a, v_b_a, v_w_x, v_b_x, v_lam, v_gmlp_out_g, v_lru_out_g, v_w_out, v_post_g, v_w_pe, v_w_pg):
    weights = dict(pre_g=pre_g, w_in=w_in, gmlp_ln_g=gmlp_ln_g, gmlp_ln_b=gmlp_ln_b, gmlp_ws=gmlp_ws, gmlp_bs=gmlp_bs,
                   conv_w=conv_w, conv_b=conv_b, w_a=w_a, b_a=b_a, w_x=w_x, b_x=b_x, lam=lam, gmlp_out_g=gmlp_out_g,
                   lru_out_g=lru_out_g, w_out=w_out, post_g=post_g, w_pe=w_pe, w_pg=w_pg)
    mom_m = dict(pre_g=m_pre_g, w_in=m_w_in, gmlp_ln_g=m_gmlp_ln_g, gmlp_ln_b=m_gmlp_ln_b, gmlp_ws=m_gmlp_ws,
                 gmlp_bs=m_gmlp_bs, conv_w=m_conv_w, conv_b=m_conv_b, w_a=m_w_a, b_a=m_b_a, w_x=m_w_x, b_x=m_b_x,
                 lam=m_lam, gmlp_out_g=m_gmlp_out_g, lru_out_g=m_lru_out_g, w_out=m_w_out, post_g=m_post_g,
                 w_pe=m_w_pe, w_pg=m_w_pg)
    mom_v = dict(pre_g=v_pre_g, w_in=v_w_in, gmlp_ln_g=v_gmlp_ln_g, gmlp_ln_b=v_gmlp_ln_b, gmlp_ws=v_gmlp_ws,
                 gmlp_bs=v_gmlp_bs, conv_w=v_conv_w, conv_b=v_conv_b, w_a=v_w_a, b_a=v_b_a, w_x=v_w_x, b_x=v_b_x,
                 lam=v_lam, gmlp_out_g=v_gmlp_out_g, lru_out_g=v_lru_out_g, w_out=v_w_out, post_g=v_post_g,
                 w_pe=v_w_pe, w_pg=v_w_pg)
    order = list(weights)
    xi, yi, ci = _place()
    me = _chip_of(xi, yi)
    kc = jnp.stack([me, ci]).astype(jnp.int32)

    x2 = x[0]
    p2 = p[0, 0]
    tgt = loss_target[0]

    first = [_cast_into_slot(w_in[0], kc, "cast_w_in").reshape(N_CHIPS, 2, D_MODEL // 2, W_IN_COLS),
             _cast_into_slot(conv_w[0, :, 0, :], kc, "conv_w_into_slot", F32).reshape(N_CHIPS, 2, CONV_W // 2, CONV_COLS)]
    in_st, in_tok = _exchange_start("gather_in_start", first, 6, _gather_ici_copies(2))
    later = [_cast_into_slot(w_out[0], kc, "cast_w_out", token=in_tok).reshape(N_CHIPS, 2, W_ROWS // 2, D_MODEL),
             _cast_into_slot(w_pg[0], kc, "cast_w_pg", token=in_tok).reshape(N_CHIPS, 2, W_ROWS // 2, D_MODEL),
             _cast_into_slot(w_pe[0], kc, "cast_w_pe", token=in_tok).reshape(N_CHIPS, 2, D_PLE // 2, W_PE_COLS)]
    gather_st, gather_tok = _exchange_start("gather_start", later, 9, _gather_direct_copies(3), after=in_tok)
    hn, z_own, hn_t = _inproj_local(x2, pre_g, w_in[0], ROW_TILE, gather_tok)
    in_st, in_tok = _exchange_wait_start("gather_in_relay", in_st, z_own, _gather_ici_copies(2), 6,
                                         _gather_relay_copies(2))
    g_in, g_cw = _exchange_wait("gather_in_wait", in_st, in_tok, _gather_relay_copies(2))
    wg_in = g_in.reshape(N_CHIPS, D_MODEL, W_IN_COLS)
    cw_full = jnp.transpose(g_cw.reshape(N_CHIPS, CONV_W, CONV_COLS), (1, 0, 2)).reshape(CONV_W, D_HALF)

    causal = jnp.tril(jnp.ones((CHUNK, CHUNK), dtype=bool))
    ws_m = jnp.where(causal[None], gmlp_ws[0], 0.0)
    prm = dict(
        ln_g=gmlp_ln_g, ln_b=gmlp_ln_b, wt=ws_m.astype(BF16), wtt=jnp.transpose(ws_m, (0, 2, 1)).astype(BF16),
        bsx=jnp.repeat(jnp.transpose(gmlp_bs[0]), CHUNK, axis=1),
        conv_w=cw_full, conv_b=conv_b, w_a=w_a[0].astype(BF16), w_x=w_x[0].astype(BF16),
        b_a=b_a[0].reshape(1, D_HALF), b_x=b_x[0].reshape(1, D_HALF), lam=lam, oga=gmlp_out_g, ogb=lru_out_g)

    z, y, h = _inproj_branches_fwd(hn, z_own, wg_in, kc, prm, ROW_TILE, gather_tok)
    g_out, g_pg, g_pe = _exchange_wait("gather_wait", gather_st, y, _gather_direct_copies(3))
    wg_out = g_out.reshape(D_MODEL, D_MODEL)
    wg_pg = g_pg.reshape(D_MODEL, D_MODEL)
    wg_pe = g_pe.reshape(N_CHIPS, D_PLE, W_PE_COLS)
    h1, dq, dh1, do, dy, gw_pe, g_post, loss_acc = _head_fwd_bwd(x2, y, p2, tgt, post_g, wg_out, wg_pg, wg_pe,
                                                                 ROW_TILE)

    def sibling_start(tag, bufs):
        lands = [_landing((b.shape[0],) + b.shape[2:], b.dtype) for b in bufs]
        return _exchange_start("sibling_start_" + tag, bufs + lands, len(bufs), _sibling_copies(len(bufs)))

    def pair_then_chip_start(tag, started, after, names, tiles, dtypes):
        n = len(names)
        got = _exchange_wait("sibling_wait_" + tag, started, after, _sibling_copies(n))
        pairs = [_pair_sum(got[b], got[n + b], kc, "pair_sum_" + names[b], tiles[b], dtypes[b]) for b in range(n)]
        lands = [_landing((3,) + pr[0].shape[1:], pr[0].dtype) for pr in pairs]
        return _exchange_start("chip_start_" + tag, [pr[0] for pr in pairs] + lands, 3 * n, _chip_copies(n)), pairs

    def sum_then_finish_start(tag, started, pairs, after, names, tiles, small, to_all=()):
        n = len(names)
        got = _exchange_wait("chip_wait_" + tag, started, after, _chip_copies(n))
        sums = [_chip_sum(pairs[b][1], got[n + b], kc if small and b == n - 1 else kc[1:],
                          (N_CHIPS, 2) if small and b == n - 1 else (2,), "chip_sum_" + names[b], tiles[b])
                for b in range(n)]
        nbig = n - 1 if small else n
        n_all = n - nbig + len(to_all)
        return _exchange_start("finish_start_" + tag, sums + list(to_all), nbig + 7 * n_all,
                               _finish_copies(nbig, n_all))

    gw_pe = gw_pe.reshape(N_CHIPS, 2, D_PLE // 2, W_PE_COLS)
    token0 = jnp.zeros((SUBLANES, LANES), F32)
    gw_out = _weight_grad(y, do, "grad_w_out", 2, 1, D_MODEL // 2, D_MODEL, CONTRACT_TILE, token0)
    gw_pg = _weight_grad(h1, dq, "grad_w_pg", 2, 1, D_MODEL // 2, D_MODEL, CONTRACT_TILE, token0)
    gw_out = gw_out.reshape(N_CHIPS, 2, W_ROWS // 2, D_MODEL)
    gw_pg = gw_pg.reshape(N_CHIPS, 2, W_ROWS // 2, D_MODEL)

    names_a, tiles_a = ["w_out", "w_pg", "w_pe"], [SUM_TILE] * 3
    st, tok = sibling_start("a", [gw_out, gw_pg, gw_pe])
    (dz, g_oga, g_ogb, g_lng, g_lnb, g_bsx, g_ws, g_cw, g_cb, g_wa, g_ba, g_wx, g_bx, g_lam) = _branches_bwd(
        z, h, dy, prm, ROW_TILE, tok)
    (st, tok), pairs_a = pair_then_chip_start("a", st, dz, names_a, tiles_a, [BF16] * 3)
    gw_in = _weight_grad(hn_t, dz, "grad_w_in", 2, N_CHIPS, D_MODEL // 2, W_IN_COLS, CONTRACT_TILE, tok,
                         a_transposed=True)
    fin_a, tok = sum_then_finish_start("a", st, pairs_a, gw_in, names_a, tiles_a, False)

    small_g = dict(
        gmlp_ln_g=g_lng, gmlp_ln_b=g_lnb, gmlp_ws=g_ws, gmlp_bs=g_bsx, conv_w=g_cw, conv_b=g_cb, w_a=g_wa, b_a=g_ba,
        w_x=g_wx, b_x=g_bx, lam=g_lam, gmlp_out_g=g_oga, lru_out_g=g_ogb, post_g=g_post)
    gsm = _pack_small(small_g).reshape(N_CHIPS, 2, SMALL_PIECE, LANES)

    names_b, tiles_b = ["w_in", "small"], [2 * SUM_TILE, SMALL_PIECE]
    n_tiles = x2.shape[0] // ROW_TILE
    n_lo = max(1, (5 * n_tiles) // 16)
    st, tok_b = _exchange_start(
        "sibling_start_b", [gw_in, gsm] + [_landing((N_CHIPS,) + b.shape[2:], F32) for b in (gw_in, gsm)], 2,
        _sibling_copies(2), after=tok)
    part = _inproj_bwd(dz, wg_in, x2, dh1, pre_g, ROW_TILE, 0, n_lo, None, False, tok_b, "inproj_bwd_lo")
    f_out, f_pg, f_pe = _exchange_wait("finish_wait_a", fin_a, part[1], _finish_copies(3, 0))
    (st, tok_b), pairs_b = pair_then_chip_start("b", st, part[1], names_b, tiles_b, [BF16, F32])
    grad_x, g_pre = _inproj_bwd(dz, wg_in, x2, dh1, pre_g, ROW_TILE, n_lo, n_tiles - n_lo, part, True, tok_b,
                                "inproj_bwd_hi")
    pre_parts = _into_slot(g_pre, loss_acc, kc, (N_CHIPS, 2), "pre_g_into_slot")
    fin_b, tok_b = sum_then_finish_start("b", st, pairs_b, g_pre, names_b, tiles_b, True, to_all=[pre_parts])

    grads, deltas, new_m, new_v = {}, {}, {}, {}

    def adam_big(n, g2d, tr, token):
        shp = weights[n].shape
        g, d, nm, nv = _adamw(weights[n][0], g2d, mom_m[n][0], mom_v[n][0], "adamw_" + n, tr, token)
        grads[n], deltas[n], new_m[n], new_v[n] = g.reshape(shp), d.reshape(shp), nm.reshape(shp), nv.reshape(shp)
        return d

    as_token = lambda d: d[:SUBLANES, :LANES]
    last = adam_big("w_out", f_out.reshape(W_ROWS, D_MODEL), SUM_TILE, tok_b)
    last = adam_big("w_pg", f_pg.reshape(W_ROWS, D_MODEL), SUM_TILE, as_token(last))
    last = adam_big("w_pe", f_pe.reshape(D_PLE, W_PE_COLS), SUM_TILE, as_token(last))
    f_in, f_sm, pre_parts = _exchange_wait("finish_wait_b", fin_b, last, _finish_copies(1, 2))
    adam_big("w_in", f_in.reshape(D_MODEL, W_IN_COLS), 2 * SUM_TILE, tok_b)

    packed_g = f_sm.reshape(SMALL_TOTAL, LANES)
    small_names = ["pre_g"] + [n for n, _ in SMALL_ROWS if n != "conv_w"]
    natural = lambda src: {n: (src[n] if src[n].ndim == 2 else src[n][0]) for n in small_names}
    outs, loss_block = _adamw_small(packed_g, pre_parts.reshape(8, D_MODEL // LANES + SUBLANES, LANES),
                                    natural(weights), natural(mom_m), natural(mom_v))
    loss = loss_block[0, 0]
    for dst, got in zip((grads, deltas, new_m, new_v), outs):
        for n in small_names:
            dst[n] = got[n].reshape(weights[n].shape)
    at = sum(r for n, r in SMALL_ROWS[:[n for n, _ in SMALL_ROWS].index("conv_w")])
    g_cw_all = packed_g[at:at + CONV_W * D_HALF // LANES].reshape(CONV_W, D_HALF)
    g_conv = lax.dynamic_slice_in_dim(g_cw_all, me * CONV_COLS, CONV_COLS, axis=1)
    g, d, nm, nv = _adamw(conv_w[0, :, 0, :], g_conv, m_conv_w[0, :, 0, :], v_conv_w[0, :, 0, :], "adamw_conv_w", CONV_W,
                          tok_b)
    cshape = conv_w.shape
    grads["conv_w"], deltas["conv_w"] = g.reshape(cshape), d.reshape(cshape)
    new_m["conv_w"], new_v["conv_w"] = nm.reshape(cshape), nv.reshape(cshape)

    return (loss, grad_x.reshape(x.shape), *[grads[n] for n in order], *[deltas[n] for n in order],
            *[new_m[n] for n in order], *[new_v[n] for n in order])
```

```python
import math

import jax
import jax.numpy as jnp
from jax import lax
from jax.experimental import pallas as pl
from jax.experimental.pallas import tpu as pltpu

F32 = jnp.float32
BF16 = jnp.bfloat16

D_MODEL = 2048
D_HALF = 1024
D_Z = 5120
D_PLE = 256
CHUNK = 128
N_HEADS = 8
N_CHIPS = 4
W_IN_COLS = D_Z // N_CHIPS
W_ROWS = D_MODEL // N_CHIPS
W_PE_COLS = D_MODEL // N_CHIPS
CONV_W = 4
CONV_COLS = D_HALF // N_CHIPS
EPS = 1e-6
LRU_C = 8.0
ADAM_LR, ADAM_B1, ADAM_B2, ADAM_EPS, ADAM_WD, ADAM_STEP = 0.001, 0.9, 0.999, 1e-08, 0.01, 10

SUBLANES = 8
LANES = 128
VMEM_LIMIT = 56 * 1024 * 1024
ROW_TILE = 256
CONTRACT_TILE = 2048
SUM_TILE = 256

SMALL_ROWS = (("gmlp_ln_g", 8), ("gmlp_ln_b", 8), ("gmlp_ws", 1024), ("gmlp_bs", 8),
              ("conv_w", 32), ("conv_b", 8), ("w_a", 1024), ("b_a", 8), ("w_x", 1024), ("b_x", 8),
              ("lam", 8), ("gmlp_out_g", 8), ("lru_out_g", 8), ("post_g", 16))
SMALL_USED = sum(r for _, r in SMALL_ROWS)
SMALL_PIECE = 400
SMALL_TOTAL = 8 * SMALL_PIECE

MESH = pl.DeviceIdType.MESH
ANY = pl.BlockSpec(memory_space=pl.ANY)

_GELU_C0 = math.sqrt(2.0 / math.pi)
_GELU_C1 = 0.044715


def _params(*sem):
    return pltpu.CompilerParams(dimension_semantics=sem, vmem_limit_bytes=VMEM_LIMIT)


def _dot(a, b):
    return jnp.dot(a, b, preferred_element_type=F32)


def _dot_nt(a, b):
    return lax.dot_general(a, b, (((1,), (1,)), ((), ())), preferred_element_type=F32)


def _dot_tn(a, b):
    return lax.dot_general(a, b, (((0,), (0,)), ((), ())), preferred_element_type=F32)


def _gelu(x):
    t = jnp.tanh(_GELU_C0 * (x + _GELU_C1 * (x * x * x)))
    return 0.5 * x * (1.0 + t), t


def _gelu_grad(x, t):
    return 0.5 * (1.0 + t) + 0.5 * x * (1.0 - t * t) * (_GELU_C0 * (1.0 + 3.0 * _GELU_C1 * x * x))


def _rowsum8(v):
    r, n = v.shape
    return jnp.sum(v.reshape(r // SUBLANES, SUBLANES, n), axis=0)


def _lanemean(v):
    return jnp.mean(v, axis=-1, keepdims=True)


def _shift_down(v, halo8, k):
    if k == 0:
        return v
    r = pltpu.roll(v, k, 0)
    hr = pltpu.roll(halo8, k, 0)
    row = lax.broadcasted_iota(jnp.int32, halo8.shape, 0)
    top = jnp.where(row < k, hr, r[0:SUBLANES])
    return jnp.concatenate([top, r[SUBLANES:]], axis=0)


def _shift_up(v, next8, k):
    if k == 0:
        return v
    n = v.shape[0]
    r = pltpu.roll(v, n - k, 0)
    nr = pltpu.roll(next8, SUBLANES - k, 0)
    row = lax.broadcasted_iota(jnp.int32, next8.shape, 0)
    bot = jnp.where(row >= SUBLANES - k, nr, r[n - SUBLANES:])
    return jnp.concatenate([r[:n - SUBLANES], bot], axis=0)


def _layernorm_parts(vg):
    mu = _lanemean(vg)
    xc = vg - mu
    rstd = lax.rsqrt(_lanemean(xc * xc) + EPS)
    return xc * rstd, rstd


def _spatial_mix(wt_ref, vn_ref, bsx_ref, mixed_ref, tm):
    for c in range(tm // CHUNK):
        rows = slice(c * CHUNK, (c + 1) * CHUNK)
        for h in range(N_HEADS):
            cols = slice(h * CHUNK, (h + 1) * CHUNK)
            mixed_ref[rows, cols] = _dot(wt_ref[h], vn_ref[rows, cols]) + bsx_ref[:, cols]


def _conv_taps(xb, halo8):
    return [_shift_down(xb, halo8, CONV_W - 1 - k) for k in range(CONV_W)]


def _lru_gates(xc_bf_ref, wa_ref, wx_ref, ba_ref, bx_ref, r_ref, i_ref):
    for h in range(N_HEADS):
        cols = slice(h * CHUNK, (h + 1) * CHUNK)
        xh = xc_bf_ref[:, cols]
        r_ref[:, cols] = jax.nn.sigmoid(_dot(xh, wa_ref[h]) + ba_ref[:, cols])
        i_ref[:, cols] = jax.nn.sigmoid(_dot(xh, wx_ref[h]) + bx_ref[:, cols])


def _softplus_neg(lam):
    return jnp.maximum(-lam, 0.0) + jnp.log(1.0 + jnp.exp(-jnp.abs(lam)))


def _decay_parts(r, lam):
    la = (-LRU_C * _softplus_neg(lam)) * r
    a = jnp.exp(la)
    th = -jnp.tanh(la)
    mult = jnp.sqrt(2.0 * th / (1.0 + th))
    return a, mult


def _z_group(zref, g, rows=slice(None)):
    lo = g * D_HALF
    blk, off = lo // W_IN_COLS, lo % W_IN_COLS
    if off + D_HALF <= W_IN_COLS:
        return zref[blk, rows, off:off + D_HALF]
    return jnp.concatenate([zref[blk, rows, off:W_IN_COLS], zref[blk + 1, rows, 0:off + D_HALF - W_IN_COLS]], axis=1)


def _inproj_local(x, pre_g, w_own, tm, token):
    t = x.shape[0]

    def body(x_ref, g_ref, w_ref, token_ref, hn_ref, zl_ref, hnt_ref, wbf_s):
        @pl.when(pl.program_id(0) == 0)
        def _():
            wbf_s[...] = w_ref[...].astype(BF16)

        xv = x_ref[...]
        hnf = xv * lax.rsqrt(_lanemean(xv * xv) + EPS) * g_ref[...]
        hn = hnf.astype(BF16)
        hn_ref[...] = hn
        hnt_ref[...] = hnf.T.astype(BF16)
        zl_ref[...] = _dot(hn, wbf_s[...]).astype(BF16)

    row = lambda n: pl.BlockSpec((tm, n), lambda i: (i, 0))
    const = lambda shp: pl.BlockSpec(shp, lambda i: (0, 0), pipeline_mode=pl.Buffered(1))
    return pl.pallas_call(
        body, name="inproj_local", grid=(t // tm,),
        in_specs=[row(D_MODEL), const((1, D_MODEL)), const((D_MODEL, W_IN_COLS)), const((SUBLANES, LANES))],
        out_specs=[row(D_MODEL), row(W_IN_COLS), pl.BlockSpec((D_MODEL, tm), lambda i: (0, i))],
        out_shape=[jax.ShapeDtypeStruct((t, D_MODEL), BF16), jax.ShapeDtypeStruct((t, W_IN_COLS), BF16),
                   jax.ShapeDtypeStruct((D_MODEL, t), BF16)],
        scratch_shapes=[pltpu.VMEM((D_MODEL, W_IN_COLS), BF16)],
        compiler_params=_params("arbitrary"),
    )(x, pre_g, w_own, token)


def _inproj_branches_fwd(hn, z_own, wg_in, kc, prm, tm, token):
    t = hn.shape[0]
    nt = t // tm
    hb = tm // SUBLANES

    def body(kc_ref, hn_ref, zo_ref, w1_ref, w2_ref, w3_ref,
             lng_ref, lnb_ref, wt_ref, bsx_ref, cw_ref, cb_ref, wa_ref, wx_ref, ba_ref, bx_ref, lam_ref,
             oga_ref, ogb_ref, token_ref,
             z_ref, y_ref, h_ref,
             zbuf0, zbuf1, vn_s, mixed_s, xcbf_s, r_s, i_s, ug_s, halo_s, carry_s):
        s = pl.program_id(0)
        me = kc_ref[0]
        w_refs = (None, w1_ref, w2_ref, w3_ref)

        @pl.when(s == 0)
        def _():
            zbuf1[...] = jnp.zeros_like(zbuf1)

        @pl.when(s <= 1)
        def _():
            carry_s[...] = jnp.zeros_like(carry_s)
            halo_s[...] = jnp.zeros_like(halo_s)

        def step(zw, zr):
            def project(r):
                blk = (me + r) % N_CHIPS
                zb = zo_ref[...] if r == 0 else _dot(hn_ref[...], w_refs[r][...]).astype(BF16)
                z_ref[blk] = zb
                zw[blk] = zb

            zin = lambda g: _z_group(zr, g).astype(F32)
            always = [s >= 0] * 4

            @pl.when(always[0])
            def _():
                project(0)
                ug, _ = _gelu(zin(0))
                ug_s[...] = ug
                vg, _ = _gelu(zin(1))
                vhat, _ = _layernorm_parts(vg)
                vn_s[...] = (vhat * lng_ref[...] + lnb_ref[...]).astype(BF16)

            @pl.when(always[1])
            def _():
                project(1)
                _spatial_mix(wt_ref, vn_s, bsx_ref, mixed_s, tm)
                ga = zin(2)
                ya = ug_s[...] * mixed_s[...] * (ga * jax.nn.sigmoid(ga))
                ra = lax.rsqrt(_lanemean(ya * ya) + EPS)
                y_ref[:, 0:D_HALF] = (ya * ra * oga_ref[...]).astype(BF16)

            @pl.when(always[2])
            def _():
                project(2)
                xb = zin(3)
                taps = _conv_taps(xb, halo_s[...])
                halo_s[...] = xb[tm - SUBLANES:]
                xc = cb_ref[...] + taps[0] * cw_ref[0:1, :]
                for k in range(1, CONV_W):
                    xc = xc + taps[k] * cw_ref[k:k + 1, :]
                xcbf_s[...] = xc.astype(BF16)
                _lru_gates(xcbf_s, wa_ref, wx_ref, ba_ref, bx_ref, r_s, i_s)
                a, mult = _decay_parts(r_s[...], lam_ref[...])
                row = lax.broadcasted_iota(jnp.int32, a.shape, 0)
                mult = jnp.where(jnp.logical_and(s == 1, row == 0), 1.0, mult)
                r_s[...] = a
                i_s[...] = mult * (i_s[...] * xc)

            @pl.when(always[3])
            def _():
                project(3)
                a = r_s[...]
                b = i_s[...]
                r8 = lax.broadcasted_iota(jnp.int32, a.shape, 0) & (SUBLANES - 1)
                for d in (1, 2, 4):
                    a_sh = pltpu.roll(a, d, 0)
                    b_sh = pltpu.roll(b, d, 0)
                    m = r8 >= d
                    b = jnp.where(m, a * b_sh + b, b)
                    a = jnp.where(m, a * a_sh, a)
                carry = carry_s[...]
                for g in range(hb):
                    rows = slice(g * SUBLANES, (g + 1) * SUBLANES)
                    hg = a[rows] * carry + b[rows]
                    h_ref[rows, :] = hg
                    carry = jnp.broadcast_to(hg[SUBLANES - 1:SUBLANES, :], hg.shape)
                carry_s[...] = carry
                gb = zin(4)
                yb = h_ref[...] * (gb * jax.nn.sigmoid(gb))
                rb = lax.rsqrt(_lanemean(yb * yb) + EPS)
                y_ref[:, D_HALF:] = (yb * rb * ogb_ref[...]).astype(BF16)

        @pl.when(s % 2 == 0)
        def _():
            step(zbuf0, zbuf1)

        @pl.when(s % 2 == 1)
        def _():
            step(zbuf1, zbuf0)

    const = lambda a: pl.BlockSpec(a.shape, lambda s, kc, n=a.ndim: (0,) * n, pipeline_mode=pl.Buffered(1))
    proj = lambda n: pl.BlockSpec((tm, n), lambda s, kc: (jnp.minimum(s, nt - 1), 0))
    head = lambda n: pl.BlockSpec((tm, n), lambda s, kc: (jnp.maximum(s - 1, 0), 0))
    other = lambda r: pl.BlockSpec((None, D_MODEL, W_IN_COLS), lambda s, kc, r=r: ((kc[0] + r) % N_CHIPS, 0, 0),
                                   pipeline_mode=pl.Buffered(1))
    names = ("ln_g", "ln_b", "wt", "bsx", "conv_w", "conv_b", "w_a", "w_x", "b_a", "b_x", "lam", "oga", "ogb")
    pr = [prm[n] for n in names] + [token]
    big = lambda dt: pltpu.VMEM((tm, D_HALF), dt)
    zblocks = pltpu.VMEM((N_CHIPS, tm, W_IN_COLS), BF16)
    grid_spec = pltpu.PrefetchScalarGridSpec(
        num_scalar_prefetch=1, grid=(nt + 1,),
        in_specs=[proj(D_MODEL), proj(W_IN_COLS), other(1), other(2), other(3)] + [const(a) for a in pr],
        out_specs=[pl.BlockSpec((N_CHIPS, tm, W_IN_COLS), lambda s, kc: (0, jnp.minimum(s, nt - 1), 0)),
                   head(D_MODEL), head(D_HALF)],
        scratch_shapes=[zblocks, zblocks, big(BF16), big(F32), big(BF16), big(F32), big(F32), big(F32),
                        pltpu.VMEM((SUBLANES, D_HALF), F32), pltpu.VMEM((SUBLANES, D_HALF), F32)])
    return pl.pallas_call(
        body, name="inproj_branches_fwd", grid_spec=grid_spec,
        out_shape=[jax.ShapeDtypeStruct((N_CHIPS, t, W_IN_COLS), BF16), jax.ShapeDtypeStruct((t, D_MODEL), BF16),
                   jax.ShapeDtypeStruct((t, D_HALF), F32)],
        compiler_params=_params("arbitrary"),
    )(kc, hn, z_own, wg_in, wg_in, wg_in, *pr)


def _head_fwd_bwd(x, y, p, tgt, post_g, w_out, w_pg, wg_pe, tm):
    t = x.shape[0]

    def body(x_ref, y_ref, p_ref, tgt_ref, pg_ref, wo_ref, wpg_ref, wpe_ref,
             h1_ref, dq_ref, dh1_ref, do_ref, dy_ref, gwpe_ref, gpost_ref, loss_ref, dout_s):
        i = pl.program_id(0)

        @pl.when(i == 0)
        def _():
            gpost_ref[...] = jnp.zeros_like(gpost_ref)
            gwpe_ref[...] = jnp.zeros_like(gwpe_ref)
            loss_ref[...] = jnp.zeros_like(loss_ref)

        pb = p_ref[...].astype(BF16)
        pes = [_dot(pb, wpe_ref[k]) for k in range(N_CHIPS)]
        o = _dot(y_ref[...], wo_ref[...])
        r3 = lax.rsqrt(_lanemean(o * o) + EPS)
        on = o * r3
        h1 = x_ref[...] + on * pg_ref[...]
        h1b = h1.astype(BF16)
        h1_ref[...] = h1b
        gt = jax.nn.sigmoid(_dot(h1b, wpg_ref[...]))
        for k in range(N_CHIPS):
            cols = slice(k * W_PE_COLS, (k + 1) * W_PE_COLS)
            pe = pes[k]
            g = gt[:, cols]
            d = h1[:, cols] + pe * g - tgt_ref[:, cols]
            loss_ref[...] += jnp.sum(d * d) * (0.5 / D_MODEL)
            dout = d * (1.0 / D_MODEL)
            dout_s[:, cols] = dout
            dg = dout * g
            gwpe_ref[k] += _dot_tn(pb, dg.astype(BF16))
            dq_ref[:, cols] = (dg * pe * (1.0 - g)).astype(BF16)
        dh1 = dout_s[...] + _dot_nt(dq_ref[...], wpg_ref[...])
        dh1_ref[...] = dh1
        gpost_ref[...] += _rowsum8(dh1 * on)
        don = dh1 * pg_ref[...]
        dob = (r3 * (don - on * _lanemean(don * on))).astype(BF16)
        do_ref[...] = dob
        dy_ref[...] = _dot_nt(dob, wo_ref[...])

        @pl.when(i == pl.num_programs(0) - 1)
        def _():
            gpost_ref[...] = jnp.broadcast_to(jnp.sum(gpost_ref[...], axis=0, keepdims=True), gpost_ref.shape)

    row = lambda n: pl.BlockSpec((tm, n), lambda i: (i, 0))
    const = lambda shp: pl.BlockSpec(shp, lambda i, n=len(shp): (0,) * n, pipeline_mode=pl.Buffered(1))
    acc = lambda shp: pl.BlockSpec(shp, lambda i, n=len(shp): (0,) * n)
    return pl.pallas_call(
        body, name="head_fwd_bwd", grid=(t // tm,),
        in_specs=[row(D_MODEL), row(D_MODEL), row(D_PLE), row(D_MODEL), const((1, D_MODEL)),
                  const((D_MODEL, D_MODEL)), const((D_MODEL, D_MODEL)), const((N_CHIPS, D_PLE, W_PE_COLS))],
        out_specs=[row(D_MODEL), row(D_MODEL), row(D_MODEL), row(D_MODEL), row(D_MODEL),
                   acc((N_CHIPS, D_PLE, W_PE_COLS)), acc((SUBLANES, D_MODEL)), acc((SUBLANES, LANES))],
        out_shape=[jax.ShapeDtypeStruct((t, D_MODEL), BF16), jax.ShapeDtypeStruct((t, D_MODEL), BF16),
                   jax.ShapeDtypeStruct((t, D_MODEL), F32), jax.ShapeDtypeStruct((t, D_MODEL), BF16),
                   jax.ShapeDtypeStruct((t, D_MODEL), F32),
                   jax.ShapeDtypeStruct((N_CHIPS, D_PLE, W_PE_COLS), F32),
                   jax.ShapeDtypeStruct((SUBLANES, D_MODEL), F32), jax.ShapeDtypeStruct((SUBLANES, LANES), F32)],
        scratch_shapes=[pltpu.VMEM((tm, D_MODEL), F32)],
        compiler_params=_params("arbitrary"),
    )(x, y, p, tgt, post_g, w_out, w_pg, wg_pe)


def _branches_bwd(z, h, dy, prm, tm, token):
    t = h.shape[0]
    nt = t // tm
    hb = tm // SUBLANES

    def body(z_ref, zh_ref, h_ref, hh_ref, dy_ref,
             lng_ref, lnb_ref, wt_ref, wtt_ref, bsx_ref, cw_ref, cb_ref, wa_ref, wx_ref, ba_ref, bx_ref, lam_ref,
             oga_ref, ogb_ref, token_ref,
             dz_ref, g_oga, g_ogb, g_lng, g_lnb, g_bsx, g_ws, g_cw, g_cb, g_wa, g_ba, g_wx, g_bx, g_lam,
             vn_s, mixed_s, dm_s, dvn_s, xcbf_s, r_s, i_s, a_s, b_s, dh_s, dpr_s, dpi_s, dxc_s,
             ca_s, cd_s, cx_s):
        step_i = pl.program_id(0)
        tile = nt - 1 - step_i
        accs = (g_oga, g_ogb, g_lng, g_lnb, g_bsx, g_ws, g_cw, g_cb, g_wa, g_ba, g_wx, g_bx, g_lam)

        @pl.when(step_i == 0)
        def _():
            for r in accs + (ca_s, cd_s, cx_s):
                r[...] = jnp.zeros_like(r)

        dy_a = dy_ref[:, 0:D_HALF]
        dy_b = dy_ref[:, D_HALF:]

        u = _z_group(z_ref, 0).astype(F32)
        ug, tu = _gelu(u)
        v = _z_group(z_ref, 1).astype(F32)
        vg, tv = _gelu(v)
        vhat, rstd = _layernorm_parts(vg)
        vn_s[...] = (vhat * lng_ref[...] + lnb_ref[...]).astype(BF16)
        _spatial_mix(wt_ref, vn_s, bsx_ref, mixed_s, tm)
        mixed = mixed_s[...]
        ga = _z_group(z_ref, 2).astype(F32)
        sga = jax.nn.sigmoid(ga)
        sa = ga * sga
        um = ug * mixed
        ya = um * sa
        ra = lax.rsqrt(_lanemean(ya * ya) + EPS)
        yahat = ya * ra
        g_oga[...] += _rowsum8(dy_a * yahat)
        dn = dy_a * oga_ref[...]
        dya = ra * (dn - yahat * _lanemean(dn * yahat))
        dz_ref[:, 2 * D_HALF:3 * D_HALF] = (dya * um * (sga * (1.0 + ga * (1.0 - sga)))).astype(BF16)
        dz_ref[:, 0:D_HALF] = (dya * mixed * sa * _gelu_grad(u, tu)).astype(BF16)
        dmixed = dya * ug * sa
        g_bsx[...] += jnp.sum(dmixed.reshape(tm // CHUNK, CHUNK, D_HALF), axis=0)
        dm_s[...] = dmixed.astype(BF16)
        for c in range(tm // CHUNK):
            rows = slice(c * CHUNK, (c + 1) * CHUNK)
            for hd in range(N_HEADS):
                cols = slice(hd * CHUNK, (hd + 1) * CHUNK)
                dmh = dm_s[rows, cols]
                dvn_s[rows, cols] = _dot(wtt_ref[hd], dmh)
                g_ws[hd] += _dot_nt(dmh, vn_s[rows, cols])
        dvn = dvn_s[...]
        g_lng[...] += _rowsum8(dvn * vhat)
        g_lnb[...] += _rowsum8(dvn)
        dvh = dvn * lng_ref[...]
        dvg = rstd * (dvh - _lanemean(dvh) - vhat * _lanemean(dvh * vhat))
        dz_ref[:, D_HALF:2 * D_HALF] = (dvg * _gelu_grad(v, tv)).astype(BF16)

        xb = _z_group(z_ref, 3).astype(F32)
        halo = jnp.where(tile == 0, 0.0, _z_group(zh_ref, 3).astype(F32)[SUBLANES:])
        taps = _conv_taps(xb, halo)
        xc = cb_ref[...] + taps[0] * cw_ref[0:1, :]
        for k in range(1, CONV_W):
            xc = xc + taps[k] * cw_ref[k:k + 1, :]
        xcbf_s[...] = xc.astype(BF16)
        _lru_gates(xcbf_s, wa_ref, wx_ref, ba_ref, bx_ref, r_s, i_s)
        rg = r_s[...]
        ig = i_s[...]
        lam = lam_ref[...]
        a, mult_true = _decay_parts(rg, lam)
        row = lax.broadcasted_iota(jnp.int32, a.shape, 0)
        first = jnp.logical_and(tile == 0, row == 0)
        mult = jnp.where(first, 1.0, mult_true)
        hcur = h_ref[...]
        hprev = _shift_down(hcur, jnp.where(tile == 0, 0.0, hh_ref[...]), 1)
        gb = _z_group(z_ref, 4).astype(F32)
        sgb = jax.nn.sigmoid(gb)
        sb = gb * sgb
        yb = hcur * sb
        rb = lax.rsqrt(_lanemean(yb * yb) + EPS)
        ybhat = yb * rb
        g_ogb[...] += _rowsum8(dy_b * ybhat)
        dn = dy_b * ogb_ref[...]
        dyb = rb * (dn - ybhat * _lanemean(dn * ybhat))
        dz_ref[:, 4 * D_HALF:5 * D_HALF] = (dyb * hcur * (sgb * (1.0 + gb * (1.0 - sgb)))).astype(BF16)

        an = _shift_up(a, ca_s[...], 1)
        bb = dyb * sb
        r8 = row & (SUBLANES - 1)
        for d in (1, 2, 4):
            a_sh = pltpu.roll(an, tm - d, 0)
            b_sh = pltpu.roll(bb, tm - d, 0)
            m = r8 + d < SUBLANES
            bb = jnp.where(m, an * b_sh + bb, bb)
            an = jnp.where(m, an * a_sh, an)
        a_s[...] = an
        b_s[...] = bb

        def step(g, carry):
            sl = pl.ds(pl.multiple_of((hb - 1 - g) * SUBLANES, SUBLANES), SUBLANES)
            dg = a_s[sl, :] * carry + b_s[sl, :]
            dh_s[sl, :] = dg
            return jnp.broadcast_to(dg[0:1, :], dg.shape)

        cd_s[...] = lax.fori_loop(0, hb, step, cd_s[...])
        ca_s[...] = jnp.broadcast_to(a[0:1, :], ca_s.shape)
        dh = dh_s[...]
        da = dh * hprev
        gx = ig * xc
        dla = da * a - jnp.where(first, 0.0, dh * gx * (a * a / mult_true))
        g_lam[...] += _rowsum8(dla * rg)
        dr = dla * (-LRU_C * _softplus_neg(lam))
        dpr = dr * rg * (1.0 - rg)
        dpi = (dh * mult * xc) * ig * (1.0 - ig)
        g_ba[...] += _rowsum8(dpr)
        g_bx[...] += _rowsum8(dpi)
        dpr_s[...] = dpr.astype(BF16)
        dpi_s[...] = dpi.astype(BF16)
        for hd in range(N_HEADS):
            cols = slice(hd * CHUNK, (hd + 1) * CHUNK)
            xh = xcbf_s[:, cols]
            dprh = dpr_s[:, cols]
            dpih = dpi_s[:, cols]
            g_wa[hd] += _dot_tn(xh, dprh)
            g_wx[hd] += _dot_tn(xh, dpih)
            dxc_s[:, cols] = _dot_nt(dprh, wa_ref[hd]) + _dot_nt(dpih, wx_ref[hd])
        dxc = dxc_s[...] + dh * mult * ig
        g_cb[...] += _rowsum8(dxc)
        for k in range(CONV_W):
            g_cw[k * SUBLANES:(k + 1) * SUBLANES, :] += _rowsum8(dxc * taps[k])
        nxt = cx_s[...]
        dxb = dxc * cw_ref[CONV_W - 1:CONV_W, :]
        for j in range(1, CONV_W):
            dxb = dxb + _shift_up(dxc, nxt, j) * cw_ref[CONV_W - 1 - j:CONV_W - j, :]
        dz_ref[:, 3 * D_HALF:4 * D_HALF] = dxb.astype(BF16)
        cx_s[...] = dxc[0:SUBLANES]

        @pl.when(step_i == nt - 1)
        def _():
            for r in (g_oga, g_ogb, g_lng, g_lnb, g_cb, g_ba, g_bx):
                r[...] = jnp.broadcast_to(jnp.sum(r[...], axis=0, keepdims=True), r.shape)
            lam_f = LRU_C * jax.nn.sigmoid(-lam_ref[...])
            g_lam[...] = jnp.broadcast_to(jnp.sum(g_lam[...], axis=0, keepdims=True) * lam_f, g_lam.shape)
            for k in range(CONV_W):
                blk = g_cw[k * SUBLANES:(k + 1) * SUBLANES, :]
                g_cw[k * SUBLANES:(k + 1) * SUBLANES, :] = jnp.broadcast_to(jnp.sum(blk, axis=0, keepdims=True), blk.shape)
            tri = (lax.broadcasted_iota(jnp.int32, (CHUNK, CHUNK), 0) >= lax.broadcasted_iota(jnp.int32, (CHUNK, CHUNK), 1))
            for hd in range(N_HEADS):
                cols = slice(hd * CHUNK, (hd + 1) * CHUNK)
                g_ws[hd] = jnp.where(tri, g_ws[hd], 0.0)
                blk = g_bsx[:, cols]
                g_bsx[:, cols] = jnp.broadcast_to(jnp.sum(blk, axis=1, keepdims=True), blk.shape)

    rev = lambda i: nt - 1 - i
    zspec = pl.BlockSpec((N_CHIPS, tm, W_IN_COLS), lambda i: (0, rev(i), 0))
    halo = lambda col: pl.BlockSpec((SUBLANES, D_HALF), lambda i: (jnp.maximum(rev(i) * hb - 1, 0), col))
    zhalo = pl.BlockSpec((N_CHIPS, 2 * SUBLANES, W_IN_COLS), lambda i: (0, jnp.maximum(rev(i) * (hb // 2) - 1, 0), 0))
    full = lambda a: pl.BlockSpec(a.shape, lambda i, n=a.ndim: (0,) * n)
    acc = lambda shp: pl.BlockSpec(shp, lambda i, n=len(shp): (0,) * n)
    names = ("ln_g", "ln_b", "wt", "wtt", "bsx", "conv_w", "conv_b", "w_a", "w_x", "b_a", "b_x", "lam", "oga", "ogb")
    pr = [prm[n] for n in names] + [token]
    vec = (SUBLANES, D_HALF)
    mat = (N_HEADS, CHUNK, CHUNK)
    acc_shapes = [vec, vec, vec, vec, (CHUNK, D_HALF), mat, (CONV_W * SUBLANES, D_HALF), vec, mat, vec, mat, vec, vec]
    big = lambda dt: pltpu.VMEM((tm, D_HALF), dt)
    return pl.pallas_call(
        body, name="branches_bwd", grid=(nt,),
        in_specs=[zspec, zhalo,
                  pl.BlockSpec((tm, D_HALF), lambda i: (rev(i), 0)), halo(0),
                  pl.BlockSpec((tm, D_MODEL), lambda i: (rev(i), 0))] + [full(a) for a in pr],
        out_specs=[pl.BlockSpec((tm, D_Z), lambda i: (rev(i), 0))] + [acc(s) for s in acc_shapes],
        out_shape=[jax.ShapeDtypeStruct((t, D_Z), BF16)] + [jax.ShapeDtypeStruct(s, F32) for s in acc_shapes],
        scratch_shapes=[big(BF16), big(F32), big(BF16), big(F32), big(BF16), big(F32), big(F32), big(F32), big(F32),
                        big(F32), big(BF16), big(BF16), big(F32),
                        pltpu.VMEM(vec, F32), pltpu.VMEM(vec, F32), pltpu.VMEM(vec, F32)],
        compiler_params=_params("arbitrary"),
    )(z, z, h, h, dy, *pr)


def _inproj_bwd(dz, wg_in, x, dh1, pre_g, tm, tile0, nt, prev, last, token, name):
    t = x.shape[0]

    def body(*refs):
        dz_ref, w_ref, x_ref, dh1_ref, g_ref = refs[:5]
        gx_ref, gpre_ref, acc_s = refs[-3:]
        i = pl.program_id(0)

        @pl.when(i == 0)
        def _():
            gpre_ref[...] = jnp.zeros_like(gpre_ref) if prev is None else refs[7][...]

        acc = _dot_nt(dz_ref[:, 0:W_IN_COLS], w_ref[0])
        for k in range(1, N_CHIPS):
            acc = acc + _dot_nt(dz_ref[:, k * W_IN_COLS:(k + 1) * W_IN_COLS], w_ref[k])
        acc_s[...] = acc
        for s in range(tm // CHUNK):
            rows = slice(s * CHUNK, (s + 1) * CHUNK)
            xv = x_ref[rows, :]
            r = lax.rsqrt(_lanemean(xv * xv) + EPS)
            xhat = xv * r
            dhn = acc_s[rows, :]
            gpre_ref[...] += _rowsum8(dhn * xhat)
            dxh = dhn * g_ref[...]
            gx_ref[rows, :] = dh1_ref[rows, :] + r * (dxh - xhat * _lanemean(dxh * xhat))

        if last:
            @pl.when(i == nt - 1)
            def _():
                gpre_ref[...] = jnp.broadcast_to(jnp.sum(gpre_ref[...], axis=0, keepdims=True), gpre_ref.shape)

    row = lambda n: pl.BlockSpec((tm, n), lambda i: (tile0 + i, 0))
    small = lambda r: pl.BlockSpec((r, D_MODEL), lambda i: (0, 0))
    tok = pl.BlockSpec((SUBLANES, LANES), lambda i: (0, 0))
    in_specs = [row(D_Z), pl.BlockSpec(wg_in.shape, lambda i: (0, 0, 0), pipeline_mode=pl.Buffered(1)),
                row(D_MODEL), row(D_MODEL), small(1), tok]
    args = [dz, wg_in, x, dh1, pre_g, token]
    aliases = {}
    if prev is not None:
        in_specs += [ANY, small(SUBLANES)]
        args += list(prev)
        aliases = {6: 0}
    return pl.pallas_call(
        body, name=name, grid=(nt,), in_specs=in_specs, out_specs=[row(D_MODEL), small(SUBLANES)],
        out_shape=[jax.ShapeDtypeStruct((t, D_MODEL), F32), jax.ShapeDtypeStruct((SUBLANES, D_MODEL), F32)],
        input_output_aliases=aliases,
        scratch_shapes=[pltpu.VMEM((tm, D_MODEL), F32)],
        compiler_params=_params("arbitrary"),
    )(*args)


def _weight_grad(a, b, name, kb, nb, tk, tn, tt, token, a_transposed=False):
    t = b.shape[0]
    tt = min(tt, t)

    def body(a_ref, b_ref, token_ref, o_ref):
        @pl.when(pl.program_id(2) == 0)
        def _():
            o_ref[...] = jnp.zeros_like(o_ref)

        o_ref[...] += (_dot if a_transposed else _dot_tn)(a_ref[...], b_ref[...])

    a_spec = (pl.BlockSpec((tk, tt), lambda j, i, s: (i, s)) if a_transposed
              else pl.BlockSpec((tt, tk), lambda j, i, s: (s, i)))
    return pl.pallas_call(
        body, name=name, grid=(nb, kb, t // tt),
        in_specs=[a_spec, pl.BlockSpec((tt, tn), lambda j, i, s: (s, j)),
                  pl.BlockSpec((SUBLANES, LANES), lambda j, i, s: (0, 0))],
        out_specs=pl.BlockSpec((None, None, tk, tn), lambda j, i, s: (j, i, 0, 0)),
        out_shape=jax.ShapeDtypeStruct((nb, kb, tk, tn), F32),
        compiler_params=_params("parallel", "parallel", "arbitrary"),
    )(a, b, token)


def _place():
    x, y, c = lax.axis_index("x"), lax.axis_index("y"), lax.axis_index("c")
    return x, y, c


def _chip_of(x, y):
    return 2 * x + y


HBM = pl.BlockSpec(memory_space=pltpu.HBM)
SEM = pl.BlockSpec(memory_space=pltpu.SEMAPHORE)
EFFECT = pltpu.SideEffectType.DATAFLOW_SIDE_EFFECTING


def _hbm(a):
    return pltpu.with_memory_space_constraint(a, pltpu.HBM)


def _landing(shape, dtype):
    return _hbm(lax.empty(shape, dtype))


def _exchange_start(name, arrays, ncopies, build, after=None):
    n = len(arrays)
    extra = [] if after is None else [after]

    def body(*refs):
        ins, token = refs[:n], refs[-1]
        send_sems, recv_sems = refs[n + len(extra)], refs[n + len(extra) + 1]
        for cp in build(ins, send_sems, recv_sems):
            cp.start()
        token[...] = jnp.zeros_like(token)

    outs = pl.pallas_call(
        body, name=name,
        out_shape=(pltpu.SemaphoreType.DMA((ncopies,)), pltpu.SemaphoreType.DMA((ncopies,)),
                   *[pltpu.HBM(a.shape, a.dtype) for a in arrays], jax.ShapeDtypeStruct((SUBLANES, LANES), F32)),
        in_specs=[HBM] * n + [ANY] * len(extra),
        out_specs=(SEM, SEM, *[HBM] * n, pl.BlockSpec(memory_space=pltpu.VMEM)),
        input_output_aliases={q: q + 2 for q in range(n)},
        compiler_params=pltpu.CompilerParams(has_side_effects=EFFECT),
    )(*[_hbm(a) for a in arrays], *extra)
    return (outs[0], outs[1], list(outs[2:2 + n])), outs[-1]


def _exchange_wait(name, started, after, build):
    send, recv, arrays = started
    n = len(arrays)

    def body(*refs):
        ins, send_sems, recv_sems = refs[:n], refs[n], refs[n + 1]
        for cp in build(ins, send_sems, recv_sems):
            cp.wait_send()
            cp.wait_recv()

    return pl.pallas_call(
        body, name=name, out_shape=tuple(pltpu.HBM(a.shape, a.dtype) for a in arrays),
        in_specs=[HBM] * n + [SEM, SEM, ANY], out_specs=tuple([HBM] * n),
        input_output_aliases={q: q for q in range(n)},
        compiler_params=pltpu.CompilerParams(has_side_effects=EFFECT),
    )(*arrays, send, recv, after)


def _exchange_wait_start(name, started, after, build_wait, ncopies, build_start):
    send, recv, arrays = started
    n = len(arrays)

    def body(*refs):
        ins, send_sems, recv_sems = refs[:n], refs[n], refs[n + 1]
        send2, recv2, token = refs[n + 3], refs[n + 4], refs[-1]
        arrived = build_wait(ins, send_sems, recv_sems)
        for cp, onward in zip(arrived, build_start(ins, send2, recv2)):
            cp.wait_recv()
            onward.start()
        for cp in arrived:
            cp.wait_send()
        token[...] = jnp.zeros_like(token)

    outs = pl.pallas_call(
        body, name=name,
        out_shape=(pltpu.SemaphoreType.DMA((ncopies,)), pltpu.SemaphoreType.DMA((ncopies,)),
                   *[pltpu.HBM(a.shape, a.dtype) for a in arrays], jax.ShapeDtypeStruct((SUBLANES, LANES), F32)),
        in_specs=[HBM] * n + [SEM, SEM, ANY], out_specs=(SEM, SEM, *[HBM] * n, pl.BlockSpec(memory_space=pltpu.VMEM)),
        input_output_aliases={q: q + 2 for q in range(n)},
        compiler_params=pltpu.CompilerParams(has_side_effects=EFFECT),
    )(*arrays, send, recv, after)
    return (outs[0], outs[1], list(outs[2:2 + n])), outs[-1]


def _cast_into_slot(w, kc, name, dtype=BF16, token=None):
    rows, cols = w.shape
    tr = min(rows, 4 * SUM_TILE)
    extra = [] if token is None else [token]

    def body(kc_ref, w_ref, *rest):
        rest[-1][...] = w_ref[...].astype(dtype)

    grid_spec = pltpu.PrefetchScalarGridSpec(
        num_scalar_prefetch=1, grid=(rows // tr,),
        in_specs=[pl.BlockSpec((tr, cols), lambda r, kc: (r, 0))]
                 + [pl.BlockSpec((SUBLANES, LANES), lambda r, kc: (0, 0))] * len(extra),
        out_specs=pl.BlockSpec((None, tr, cols), lambda r, kc: (kc[0], r, 0)))
    return pl.pallas_call(
        body, name=name, grid_spec=grid_spec, out_shape=jax.ShapeDtypeStruct((N_CHIPS, rows, cols), dtype),
        compiler_params=_params("arbitrary"),
    )(kc, w, *extra)


def _gather_ici_copies(n):
    def build(refs, send_sems, recv_sems):
        x, y, c = _place()
        mine = lambda b: refs[b].at[_chip_of(x, y), c]
        chips = [(1 - x, y), (x, 1 - y), (1 - x, 1 - y)]
        return [pltpu.make_async_remote_copy(
            src_ref=mine(b), dst_ref=mine(b), send_sem=send_sems.at[3 * b + j], recv_sem=recv_sems.at[3 * b + j],
            device_id=(*chip, c), device_id_type=MESH) for b in range(n) for j, chip in enumerate(chips)]
    return build


def _gather_direct_copies(n):
    def build(refs, send_sems, recv_sems):
        x, y, c = _place()
        mine = lambda b: refs[b].at[_chip_of(x, y)]
        chips = [(1 - x, y), (x, 1 - y), (1 - x, 1 - y)]
        return [pltpu.make_async_remote_copy(
            src_ref=mine(b), dst_ref=mine(b), send_sem=send_sems.at[3 * b + j], recv_sem=recv_sems.at[3 * b + j],
            device_id=(*chip, c), device_id_type=MESH) for b in range(n) for j, chip in enumerate(chips)]
    return build


def _gather_relay_copies(n):
    def build(refs, send_sems, recv_sems):
        x, y, c = _place()
        chips = [(1 - x, y), (x, 1 - y), (1 - x, 1 - y)]
        cps = []
        for b in range(n):
            for j, chip in enumerate(chips):
                got = refs[b].at[_chip_of(*chip), c]
                cps.append(pltpu.make_async_remote_copy(
                    src_ref=got, dst_ref=got, send_sem=send_sems.at[3 * b + j], recv_sem=recv_sems.at[3 * b + j],
                    device_id=(x, y, 1 - c), device_id_type=MESH))
        return cps
    return build


def _sibling_copies(n):
    def build(refs, send_sems, recv_sems):
        x, y, c = _place()
        return [pltpu.make_async_remote_copy(
            src_ref=refs[b].at[:, 1 - c], dst_ref=refs[n + b], send_sem=send_sems.at[b], recv_sem=recv_sems.at[b],
            device_id=(x, y, 1 - c), device_id_type=MESH) for b in range(n)]
    return build


def _chip_copies(n):
    def build(refs, send_sems, recv_sems):
        x, y, c = _place()
        chips = [(1 - x, y), (x, 1 - y), (1 - x, 1 - y)]
        return [pltpu.make_async_remote_copy(
            src_ref=refs[b].at[_chip_of(*chip)], dst_ref=refs[n + b].at[j],
            send_sem=send_sems.at[3 * b + j], recv_sem=recv_sems.at[3 * b + j],
            device_id=(*chip, c), device_id_type=MESH) for b in range(n) for j, chip in enumerate(chips)]
    return build


def _finish_copies(n, n_all):
    def build(refs, send_sems, recv_sems):
        x, y, c = _place()
        cps = [pltpu.make_async_remote_copy(
            src_ref=refs[b].at[c], dst_ref=refs[b].at[c], send_sem=send_sems.at[b], recv_sem=recv_sems.at[b],
            device_id=(x, y, 1 - c), device_id_type=MESH) for b in range(n)]
        flips = [(fx, fy, fc) for fx in (0, 1) for fy in (0, 1) for fc in (0, 1)][1:]
        for b in range(n_all):
            mine = refs[n + b].at[_chip_of(x, y), c]
            cps += [pltpu.make_async_remote_copy(
                src_ref=mine, dst_ref=mine, send_sem=send_sems.at[n + 7 * b + q], recv_sem=recv_sems.at[n + 7 * b + q],
                device_id=(x ^ fx, y ^ fy, c ^ fc), device_id_type=MESH) for q, (fx, fy, fc) in enumerate(flips)]
        return cps
    return build


def _pair_sum(g, r1, kc, name, tr, send_dtype):
    nk, _, rows, cols = g.shape
    tr = min(tr, rows)

    def body(kc_ref, g_ref, r_ref, p_ref, own_ref):
        s = g_ref[...] + r_ref[...]
        p_ref[...] = s.astype(send_dtype)

        @pl.when(pl.program_id(1) == kc_ref[0])
        def _():
            own_ref[...] = s

    grid_spec = pltpu.PrefetchScalarGridSpec(
        num_scalar_prefetch=1, grid=(rows // tr, nk),
        in_specs=[pl.BlockSpec((None, None, tr, cols), lambda r, k, kc: (k, kc[1], r, 0)),
                  pl.BlockSpec((None, tr, cols), lambda r, k, kc: (k, r, 0))],
        out_specs=[pl.BlockSpec((None, tr, cols), lambda r, k, kc: (k, r, 0)),
                   pl.BlockSpec((tr, cols), lambda r, k, kc: (r, 0))])
    return pl.pallas_call(
        body, name=name, grid_spec=grid_spec,
        out_shape=[jax.ShapeDtypeStruct((nk, rows, cols), send_dtype), jax.ShapeDtypeStruct((rows, cols), F32)],
        compiler_params=_params("arbitrary", "arbitrary"),
    )(kc, g, r1)


def _chip_sum(own, r2, slot, lead, name, tr):
    rows, cols = own.shape
    tr = min(tr, rows)
    nl = len(lead)

    def body(slot_ref, o_ref, r_ref, s_ref):
        s = o_ref[...]
        for j in range(3):
            s = s + r_ref[j].astype(F32)
        s_ref[...] = s

    grid_spec = pltpu.PrefetchScalarGridSpec(
        num_scalar_prefetch=1, grid=(rows // tr,),
        in_specs=[pl.BlockSpec((tr, cols), lambda r, sl: (r, 0)), pl.BlockSpec((3, tr, cols), lambda r, sl: (0, r, 0))],
        out_specs=pl.BlockSpec((None,) * nl + (tr, cols), lambda r, sl: tuple(sl[q] for q in range(nl)) + (r, 0)))
    return pl.pallas_call(
        body, name=name, grid_spec=grid_spec, out_shape=jax.ShapeDtypeStruct(tuple(lead) + (rows, cols), F32),
        compiler_params=_params("arbitrary"),
    )(slot, own, r2)


def _adam_update(w, g, m, v):
    nm = ADAM_B1 * m + (1.0 - ADAM_B1) * g
    nv = ADAM_B2 * v + (1.0 - ADAM_B2) * (g * g)
    m_hat = nm / (1.0 - ADAM_B1 ** ADAM_STEP)
    v_hat = nv / (1.0 - ADAM_B2 ** ADAM_STEP)
    return -ADAM_LR * (m_hat / (jnp.sqrt(v_hat) + ADAM_EPS) + ADAM_WD * w), nm, nv


def _adamw(w, g, m, v, name, tr, token):
    rows, cols = w.shape
    tr = min(tr, rows)

    def body(w_ref, g_ref, m_ref, v_ref, token_ref, go_ref, d_ref, nm_ref, nv_ref):
        gv = g_ref[...]
        go_ref[...] = gv
        d_ref[...], nm_ref[...], nv_ref[...] = _adam_update(w_ref[...], gv, m_ref[...], v_ref[...])

    spec = pl.BlockSpec((tr, cols), lambda r: (r, 0))
    return pl.pallas_call(
        body, name=name, grid=(rows // tr,),
        in_specs=[spec] * 4 + [pl.BlockSpec((SUBLANES, LANES), lambda r: (0, 0))], out_specs=[spec] * 4,
        out_shape=[jax.ShapeDtypeStruct((rows, cols), F32)] * 4,
        compiler_params=_params("parallel"),
    )(w, g, m, v, token)


def _adamw_small(packed_g, pre_g_parts, ws, ms, vs):
    names = ["pre_g"] + [n for n, _ in SMALL_ROWS if n != "conv_w"]
    rows = dict(SMALL_ROWS)
    offset, at = {}, 0
    for n, r in SMALL_ROWS:
        offset[n] = at
        at += r
    k = len(names)

    def body(*refs):
        g_ref, pg_ref = refs[0], refs[1]
        w_refs, m_refs, v_refs = refs[2:2 + k], refs[2 + k:2 + 2 * k], refs[2 + 2 * k:2 + 3 * k]
        outs = refs[2 + 3 * k:]
        go, do, mo, vo = outs[:k], outs[k:2 * k], outs[2 * k:3 * k], outs[3 * k:4 * k]
        pre = pg_ref[0]
        for dev in range(1, 8):
            pre = pre + pg_ref[dev]
        outs[4 * k][...] = pre[D_MODEL // LANES:, :]
        for i, n in enumerate(names):
            shp = w_refs[i].shape
            if len(shp) == 2 and shp[0] == 1:
                for r in range(shp[1] // LANES):
                    cols = slice(r * LANES, (r + 1) * LANES)
                    g = pre[r:r + 1, :] if n == "pre_g" else g_ref[offset[n] + r:offset[n] + r + 1, :]
                    go[i][:, cols] = g
                    do[i][:, cols], mo[i][:, cols], vo[i][:, cols] = _adam_update(
                        w_refs[i][:, cols], g, m_refs[i][:, cols], v_refs[i][:, cols])
            else:
                g = g_ref[offset[n]:offset[n] + rows[n], :].reshape(shp)
                go[i][...] = g
                do[i][...], mo[i][...], vo[i][...] = _adam_update(w_refs[i][...], g, m_refs[i][...], v_refs[i][...])

    vm = pl.BlockSpec(memory_space=pltpu.VMEM)
    args = [packed_g, pre_g_parts] + [src[n] for src in (ws, ms, vs) for n in names]
    out_shape = [jax.ShapeDtypeStruct(ws[n].shape, F32) for _ in range(4) for n in names]
    out_shape.append(jax.ShapeDtypeStruct((SUBLANES, LANES), F32))
    outs = pl.pallas_call(
        body, name="adamw_small", in_specs=[vm] * len(args), out_specs=[vm] * (4 * k + 1), out_shape=out_shape,
    )(*args)
    return [dict(zip(names, outs[q * k:(q + 1) * k])) for q in range(4)], outs[4 * k]


def _into_slot(v, tail, slot, lead, name):
    n = v.shape[1]
    nl = len(lead)
    rows = n // LANES + SUBLANES

    def body(slot_ref, v_ref, t_ref, o_ref):
        for r in range(n // LANES):
            o_ref[r:r + 1, :] = v_ref[0:1, r * LANES:(r + 1) * LANES]
        o_ref[n // LANES:, :] = t_ref[...]

    grid_spec = pltpu.PrefetchScalarGridSpec(
        num_scalar_prefetch=1, grid=(1,),
        in_specs=[pl.BlockSpec(v.shape, lambda i, sl: (0, 0)), pl.BlockSpec(tail.shape, lambda i, sl: (0, 0))],
        out_specs=pl.BlockSpec((None,) * nl + (rows, LANES), lambda i, sl: tuple(sl[q] for q in range(nl)) + (0, 0)))
    return pl.pallas_call(
        body, name=name, grid_spec=grid_spec, out_shape=jax.ShapeDtypeStruct(tuple(lead) + (rows, LANES), F32),
    )(slot, v, tail)


def _pack_small(parts):
    names = [n for n, _ in SMALL_ROWS]
    offset, at = {}, 0
    for n, r in SMALL_ROWS:
        offset[n] = at
        at += r

    def body(*refs):
        ins, o_ref = dict(zip(names, refs[:-1])), refs[-1]
        o_ref[SMALL_USED:, :] = jnp.zeros((SMALL_TOTAL - SMALL_USED, LANES), F32)
        for n, rows in SMALL_ROWS:
            ref, at = ins[n], offset[n]
            if n == "gmlp_bs":
                for h in range(N_HEADS):
                    o_ref[at + h:at + h + 1, :] = jnp.transpose(ref[:, h * CHUNK:(h + 1) * CHUNK])[0:1, :]
            elif n == "conv_w":
                for k in range(CONV_W):
                    for r in range(D_HALF // LANES):
                        row = at + k * (D_HALF // LANES) + r
                        o_ref[row:row + 1, :] = ref[k * SUBLANES:k * SUBLANES + 1, r * LANES:(r + 1) * LANES]
            elif ref.ndim == 3:
                o_ref[at:at + rows, :] = ref[...].reshape(rows, LANES)
            else:
                for r in range(rows):
                    o_ref[at + r:at + r + 1, :] = ref[0:1, r * LANES:(r + 1) * LANES]

    vm = pl.BlockSpec(memory_space=pltpu.VMEM)
    return pl.pallas_call(
        body, name="pack_small", in_specs=[vm] * len(names), out_specs=vm,
        out_shape=jax.ShapeDtypeStruct((SMALL_TOTAL, LANES), F32),
    )(*[parts[n] for n in names])


def kernel(x, p, pre_g, w_in, gmlp_ln_g, gmlp_ln_b, gmlp_ws, gmlp_bs, conv_w, conv_b, w_a, b_a, w_x, b_x, lam, gmlp_out_g, lru_out_g, w_out, post_g, w_pe, w_pg, loss_target, m_pre_g, m_w_in, m_gmlp_ln_g, m_gmlp_ln_b, m_gmlp_ws, m_gmlp_bs, m_conv_w, m_conv_b, m_w_a, m_b_a, m_w_x, m_b_x, m_lam, m_gmlp_out_g, m_lru_out_g, m_w_out, m_post_g, m_w_pe, m_w_pg, v_pre_g, v_w_in, v_gmlp_ln_g, v_gmlp_ln_b, v_gmlp_ws, v_gmlp_bs, v_conv_w, v_conv_b, v_w_a, v_b_a, v_w_x, v_b_x, v_lam, v_gmlp_out_g, v_lru_out_g, v_w_out, v_post_g, v_w_pe, v_w_pg):
    weights = dict(pre_g=pre_g, w_in=w_in, gmlp_ln_g=gmlp_ln_g, gmlp_ln_b=gmlp_ln_b, gmlp_ws=gmlp_ws, gmlp_bs=gmlp_bs,
                   conv_w=conv_w, conv_b=conv_b, w_a=w_a, b_a=b_a, w_x=w_x, b_x=b_x, lam=lam, gmlp_out_g=gmlp_out_g,
                   lru_out_g=lru_out_g, w_out=w_out, post_g=post_g, w_pe=w_pe, w_pg=w_pg)
    mom_m = dict(pre_g=m_pre_g, w_in=m_w_in, gmlp_ln_g=m_gmlp_ln_g, gmlp_ln_b=m_gmlp_ln_b, gmlp_ws=m_gmlp_ws,
                 gmlp_bs=m_gmlp_bs, conv_w=m_conv_w, conv_b=m_conv_b, w_a=m_w_a, b_a=m_b_a, w_x=m_w_x, b_x=m_b_x,
                 lam=m_lam, gmlp_out_g=m_gmlp_out_g, lru_out_g=m_lru_out_g, w_out=m_w_out, post_g=m_post_g,
                 w_pe=m_w_pe, w_pg=m_w_pg)
    mom_v = dict(pre_g=v_pre_g, w_in=v_w_in, gmlp_ln_g=v_gmlp_ln_g, gmlp_ln_b=v_gmlp_ln_b, gmlp_ws=v_gmlp_ws,
                 gmlp_bs=v_gmlp_bs, conv_w=v_conv_w, conv_b=v_conv_b, w_a=v_w_a, b_a=v_b_a, w_x=v_w_x, b_x=v_b_x,
                 lam=v_lam, gmlp_out_g=v_gmlp_out_g, lru_out_g=v_lru_out_g, w_out=v_w_out, post_g=v_post_g,
                 w_pe=v_w_pe, w_pg=v_w_pg)
    order = list(weights)
    xi, yi, ci = _place()
    me = _chip_of(xi, yi)
    kc = jnp.stack([me, ci]).astype(jnp.int32)

    x2 = x[0]
    p2 = p[0, 0]
    tgt = loss_target[0]

    first = [_cast_into_slot(w_in[0], kc, "cast_w_in").reshape(N_CHIPS, 2, D_MODEL // 2, W_IN_COLS),
             _cast_into_slot(conv_w[0, :, 0, :], kc, "conv_w_into_slot", F32).reshape(N_CHIPS, 2, CONV_W // 2, CONV_COLS)]
    in_st, in_tok = _exchange_start("gather_in_start", first, 6, _gather_ici_copies(2))
    later = [_cast_into_slot(w_out[0], kc, "cast_w_out", token=in_tok).reshape(N_CHIPS, 2, W_ROWS // 2, D_MODEL),
             _cast_into_slot(w_pg[0], kc, "cast_w_pg", token=in_tok).reshape(N_CHIPS, 2, W_ROWS // 2, D_MODEL),
             _cast_into_slot(w_pe[0], kc, "cast_w_pe", token=in_tok).reshape(N_CHIPS, 2, D_PLE // 2, W_PE_COLS)]
    gather_st, gather_tok = _exchange_start("gather_start", later, 9, _gather_direct_copies(3), after=in_tok)
    causal = jnp.tril(jnp.ones((CHUNK, CHUNK), dtype=bool))
    ws_m = jnp.where(causal[None], gmlp_ws[0], 0.0)
    prepared = dict(
        wt=ws_m.astype(BF16), wtt=jnp.transpose(ws_m, (0, 2, 1)).astype(BF16),
        bsx=jnp.repeat(jnp.transpose(gmlp_bs[0]), CHUNK, axis=1), w_a=w_a[0].astype(BF16), w_x=w_x[0].astype(BF16),
        b_a=b_a[0].reshape(1, D_HALF), b_x=b_x[0].reshape(1, D_HALF))
    gather_tok, prepared = lax.optimization_barrier((gather_tok, prepared))
    hn, z_own, hn_t = _inproj_local(x2, pre_g, w_in[0], ROW_TILE, gather_tok)
    in_st, in_tok = _exchange_wait_start("gather_in_relay", in_st, z_own, _gather_ici_copies(2), 6,
                                         _gather_relay_copies(2))
    g_in, g_cw = _exchange_wait("gather_in_wait", in_st, in_tok, _gather_relay_copies(2))
    wg_in = g_in.reshape(N_CHIPS, D_MODEL, W_IN_COLS)
    cw_full = jnp.transpose(g_cw.reshape(N_CHIPS, CONV_W, CONV_COLS), (1, 0, 2)).reshape(CONV_W, D_HALF)

    prm = dict(
        prepared, ln_g=gmlp_ln_g, ln_b=gmlp_ln_b, conv_w=cw_full, conv_b=conv_b, lam=lam, oga=gmlp_out_g,
        ogb=lru_out_g)

    z, y, h = _inproj_branches_fwd(hn, z_own, wg_in, kc, prm, ROW_TILE, gather_tok)
    g_out, g_pg, g_pe = _exchange_wait("gather_wait", gather_st, y, _gather_direct_copies(3))
    wg_out = g_out.reshape(D_MODEL, D_MODEL)
    wg_pg = g_pg.reshape(D_MODEL, D_MODEL)
    wg_pe = g_pe.reshape(N_CHIPS, D_PLE, W_PE_COLS)
    h1, dq, dh1, do, dy, gw_pe, g_post, loss_acc = _head_fwd_bwd(x2, y, p2, tgt, post_g, wg_out, wg_pg, wg_pe,
                                                                 ROW_TILE)

    def sibling_start(tag, bufs):
        lands = [_landing((b.shape[0],) + b.shape[2:], b.dtype) for b in bufs]
        return _exchange_start("sibling_start_" + tag, bufs + lands, len(bufs), _sibling_copies(len(bufs)))

    def pair_then_chip_start(tag, started, after, names, tiles, dtypes):
        n = len(names)
        got = _exchange_wait("sibling_wait_" + tag, started, after, _sibling_copies(n))
        pairs = [_pair_sum(got[b], got[n + b], kc, "pair_sum_" + names[b], tiles[b], dtypes[b]) for b in range(n)]
        lands = [_landing((3,) + pr[0].shape[1:], pr[0].dtype) for pr in pairs]
        return _exchange_start("chip_start_" + tag, [pr[0] for pr in pairs] + lands, 3 * n, _chip_copies(n)), pairs

    def sum_then_finish_start(tag, started, pairs, after, names, tiles, small, to_all=()):
        n = len(names)
        got = _exchange_wait("chip_wait_" + tag, started, after, _chip_copies(n))
        sums = [_chip_sum(pairs[b][1], got[n + b], kc if small and b == n - 1 else kc[1:],
                          (N_CHIPS, 2) if small and b == n - 1 else (2,), "chip_sum_" + names[b], tiles[b])
                for b in range(n)]
        nbig = n - 1 if small else n
        n_all = n - nbig + len(to_all)
        return _exchange_start("finish_start_" + tag, sums + list(to_all), nbig + 7 * n_all,
                               _finish_copies(nbig, n_all))

    gw_pe = gw_pe.reshape(N_CHIPS, 2, D_PLE // 2, W_PE_COLS)
    token0 = jnp.zeros((SUBLANES, LANES), F32)
    gw_out = _weight_grad(y, do, "grad_w_out", 2, 1, D_MODEL // 2, D_MODEL, CONTRACT_TILE, token0)
    gw_pg = _weight_grad(h1, dq, "grad_w_pg", 2, 1, D_MODEL // 2, D_MODEL, CONTRACT_TILE, token0)
    gw_out = gw_out.reshape(N_CHIPS, 2, W_ROWS // 2, D_MODEL)
    gw_pg = gw_pg.reshape(N_CHIPS, 2, W_ROWS // 2, D_MODEL)

    names_a, tiles_a = ["w_out", "w_pg", "w_pe"], [SUM_TILE] * 3
    st, tok = sibling_start("a", [gw_out, gw_pg, gw_pe])
    (dz, g_oga, g_ogb, g_lng, g_lnb, g_bsx, g_ws, g_cw, g_cb, g_wa, g_ba, g_wx, g_bx, g_lam) = _branches_bwd(
        z, h, dy, prm, ROW_TILE, tok)
    (st, tok), pairs_a = pair_then_chip_start("a", st, dz, names_a, tiles_a, [BF16] * 3)
    gw_in = _weight_grad(hn_t, dz, "grad_w_in", 2, N_CHIPS, D_MODEL // 2, W_IN_COLS, CONTRACT_TILE, tok,
                         a_transposed=True)
    fin_a, tok = sum_then_finish_start("a", st, pairs_a, gw_in, names_a, tiles_a, False)

    small_g = dict(
        gmlp_ln_g=g_lng, gmlp_ln_b=g_lnb, gmlp_ws=g_ws, gmlp_bs=g_bsx, conv_w=g_cw, conv_b=g_cb, w_a=g_wa, b_a=g_ba,
        w_x=g_wx, b_x=g_bx, lam=g_lam, gmlp_out_g=g_oga, lru_out_g=g_ogb, post_g=g_post)
    gsm = _pack_small(small_g).reshape(N_CHIPS, 2, SMALL_PIECE, LANES)

    names_b, tiles_b = ["w_in", "small"], [2 * SUM_TILE, SMALL_PIECE]
    n_tiles = x2.shape[0] // ROW_TILE
    n_lo = max(1, (5 * n_tiles) // 16)
    st, tok_b = _exchange_start(
        "sibling_start_b", [gw_in, gsm] + [_landing((N_CHIPS,) + b.shape[2:], F32) for b in (gw_in, gsm)], 2,
        _sibling_copies(2), after=tok)
    part = _inproj_bwd(dz, wg_in, x2, dh1, pre_g, ROW_TILE, 0, n_lo, None, False, tok_b, "inproj_bwd_lo")
    f_out, f_pg, f_pe = _exchange_wait("finish_wait_a", fin_a, part[1], _finish_copies(3, 0))
    (st, tok_b), pairs_b = pair_then_chip_start("b", st, part[1], names_b, tiles_b, [BF16, F32])
    grad_x, g_pre = _inproj_bwd(dz, wg_in, x2, dh1, pre_g, ROW_TILE, n_lo, n_tiles - n_lo, part, True, tok_b,
                                "inproj_bwd_hi")
    pre_parts = _into_slot(g_pre, loss_acc, kc, (N_CHIPS, 2), "pre_g_into_slot")
    fin_b, tok_b = sum_then_finish_start("b", st, pairs_b, g_pre, names_b, tiles_b, True, to_all=[pre_parts])

    grads, deltas, new_m, new_v = {}, {}, {}, {}

    def adam_big(n, g2d, tr, token):
        shp = weights[n].shape
        g, d, nm, nv = _adamw(weights[n][0], g2d, mom_m[n][0], mom_v[n][0], "adamw_" + n, tr, token)
        grads[n], deltas[n], new_m[n], new_v[n] = g.reshape(shp), d.reshape(shp), nm.reshape(shp), nv.reshape(shp)
        return d

    as_token = lambda d: d[:SUBLANES, :LANES]
    last = adam_big("w_out", f_out.reshape(W_ROWS, D_MODEL), SUM_TILE, tok_b)
    last = adam_big("w_pg", f_pg.reshape(W_ROWS, D_MODEL), SUM_TILE, as_token(last))
    last = adam_big("w_pe", f_pe.reshape(D_PLE, W_PE_COLS), SUM_TILE, as_token(last))
    f_in, f_sm, pre_parts = _exchange_wait("finish_wait_b", fin_b, last, _finish_copies(1, 2))
    adam_big("w_in", f_in.reshape(D_MODEL, W_IN_COLS), 2 * SUM_TILE, tok_b)

    packed_g = f_sm.reshape(SMALL_TOTAL, LANES)
    small_names = ["pre_g"] + [n for n, _ in SMALL_ROWS if n != "conv_w"]
    natural = lambda src: {n: (src[n] if src[n].ndim == 2 else src[n][0]) for n in small_names}
    outs, loss_block = _adamw_small(packed_g, pre_parts.reshape(8, D_MODEL // LANES + SUBLANES, LANES),
                                    natural(weights), natural(mom_m), natural(mom_v))
    loss = loss_block[0, 0]
    for dst, got in zip((grads, deltas, new_m, new_v), outs):
        for n in small_names:
            dst[n] = got[n].reshape(weights[n].shape)
    at = sum(r for n, r in SMALL_ROWS[:[n for n, _ in SMALL_ROWS].index("conv_w")])
    g_cw_all = packed_g[at:at + CONV_W * D_HALF // LANES].reshape(CONV_W, D_HALF)
    g_conv = lax.dynamic_slice_in_dim(g_cw_all, me * CONV_COLS, CONV_COLS, axis=1)
    g, d, nm, nv = _adamw(conv_w[0, :, 0, :], g_conv, m_conv_w[0, :, 0, :], v_conv_w[0, :, 0, :], "adamw_conv_w", CONV_W,
                          tok_b)
    cshape = conv_w.shape
    grads["conv_w"], deltas["conv_w"] = g.reshape(cshape), d.reshape(cshape)
    new_m["conv_w"], new_v["conv_w"] = nm.reshape(cshape), nv.reshape(cshape)

    return (loss, grad_x.reshape(x.shape), *[grads[n] for n in order], *[deltas[n] for n in order],
            *[new_m[n] for n in order], *[new_v[n] for n in order])
```
